```python
import math
import numpy as np
import jax, jax.numpy as jnp
from jax import lax

D_MODEL = 1024
BATCH = 8
SEQ = 2048
DEPTH = 4

CHUNK = 64
N_MEM = 256
RMS_EPS = 1e-6
S5_WIDTH = D_MODEL // 2
S5_GROUP = 16
S5_GROUPS = S5_WIDTH // S5_GROUP
S5_STATE = 64
S5_DT_MIN = 0.001
S5_DT_MAX = 0.1
GDN_HEAD_DIM = 128
GDN_WIDTH = D_MODEL // 2
GDN_HEADS = GDN_WIDTH // GDN_HEAD_DIM
GDN_CONV = 4
GDN_DT_MIN = 0.001
GDN_DT_MAX = 0.1
AB_IN = S5_WIDTH + 4 * GDN_WIDTH + 2 * GDN_HEADS
CA_HEADS = 16
CA_HEAD_DIM = D_MODEL // CA_HEADS
CA_LEFT_CHUNKS = 8
CA_BAND = (CA_LEFT_CHUNKS + 1) * CHUNK
MAX_REL_DIST = 128
XA_HEADS = 4
XA_HEAD_DIM = D_MODEL // XA_HEADS
FFN_HIDDEN = -(-8 * D_MODEL // (3 * 256)) * 256
N_EVEN = (DEPTH + 1) // 2
N_ODD = DEPTH // 2

kernel_name = "hybrid_s5_deltanet_chunkattn_encoder"

F32 = jnp.float32


def rms_norm(x, g):
    x32 = x.astype(F32)
    y = x32 * lax.rsqrt(jnp.mean(x32 * x32, axis=-1, keepdims=True) + RMS_EPS)
    return (y * g.astype(F32)).astype(x.dtype)


def l2_norm(t):
    return t * lax.rsqrt(jnp.sum(t * t, axis=-1, keepdims=True) + RMS_EPS)


def s5_mixer(u, a_re, a_im, log_dt, b_re, b_im, c_re, c_im, d, w_glu, b_glu):
    bsz, seq, _ = u.shape
    u32 = u.astype(F32).reshape(bsz, seq, S5_GROUPS, S5_GROUP)
    lam = lax.complex(a_re.astype(F32), a_im.astype(F32))
    dt = jnp.exp(log_dt.astype(F32))[:, None]
    lam_bar = jnp.exp(lam * dt)
    b_bar = ((lam_bar - 1.0) / lam)[:, :, None] * lax.complex(b_re.astype(F32), b_im.astype(F32))
    c = lax.complex(c_re.astype(F32), c_im.astype(F32))
    bu = jnp.einsum('bsgc,gpc->bsgp', u32.astype(jnp.complex64), b_bar)
    lam_seq = jnp.broadcast_to(lam_bar, bu.shape)

    def combine(e_prev, e_next):
        a_p, s_p = e_prev
        a_n, s_n = e_next
        return a_n * a_p, a_n * s_p + s_n

    _, states = lax.associative_scan(combine, (lam_seq, bu), axis=1)
    y = jnp.real(jnp.einsum('bsgp,gcp->bsgc', states, c)) + d.astype(F32) * u32
    h = jax.nn.gelu(y.reshape(bsz, seq, S5_WIDTH))
    out = h * jax.nn.sigmoid(h @ w_glu.astype(F32) + b_glu.astype(F32))
    return out.astype(u.dtype)


def gated_deltanet(q, k, v, gate, a_logit, b_logit, conv_w, a_log, dt_bias, out_g):
    bsz, seq, _ = q.shape
    n = seq // CHUNK
    qkv = jnp.concatenate([q, k, v], axis=-1).astype(F32)
    qkv = lax.conv_general_dilated(
        qkv, conv_w.astype(F32)[:, None, :], window_strides=(1,),
        padding=[(GDN_CONV - 1, 0)], dimension_numbers=('NWC', 'WIO', 'NWC'),
        feature_group_count=3 * GDN_WIDTH)
    qkv = jax.nn.silu(qkv)
    q, k, v = jnp.split(qkv, 3, axis=-1)

    def heads(t):
        return t.reshape(bsz, n, CHUNK, GDN_HEADS, GDN_HEAD_DIM).transpose(0, 3, 1, 2, 4)

    def per_head(t):
        return t.reshape(bsz, n, CHUNK, GDN_HEADS).transpose(0, 3, 1, 2)

    q = l2_norm(heads(q)) * (GDN_HEAD_DIM ** -0.5)
    k = l2_norm(heads(k))
    v = heads(v)
    beta = per_head(jax.nn.sigmoid(b_logit.astype(F32)))
    g = -jnp.exp(a_log.astype(F32)) * jax.nn.softplus(a_logit.astype(F32) + dt_bias.astype(F32))
    gcum = jnp.cumsum(per_head(g), axis=-1)
    gamma = jnp.exp(gcum)
    diff = gcum[..., :, None] - gcum[..., None, :]
    idx = jnp.arange(CHUNK)
    strict = idx[:, None] > idx[None, :]
    causal = idx[:, None] >= idx[None, :]
    decay_strict = jnp.where(strict, jnp.exp(jnp.where(strict, diff, 0.0)), 0.0)
    decay_causal = jnp.where(causal, jnp.exp(jnp.where(causal, diff, 0.0)), 0.0)
    kk = jnp.einsum('bhntd,bhnjd->bhntj', k, k)
    lower = jnp.eye(CHUNK, dtype=F32) + beta[..., :, None] * kk * decay_strict
    rhs = jnp.concatenate([beta[..., None] * v, (beta * gamma)[..., None] * k], axis=-1)
    sol = lax.linalg.triangular_solve(lower, rhs, left_side=True, lower=True, unit_diagonal=True)
    u_new, w_k = jnp.split(sol, [GDN_HEAD_DIM], axis=-1)
    qk = jnp.einsum('bhntd,bhnjd->bhntj', q, k) * decay_causal
    q_g = q * gamma[..., None]
    k_tail = k * jnp.exp(gcum[..., -1:] - gcum)[..., None]
    chunk_decay = jnp.exp(gcum[..., -1])

    def step(state, inp):
        u_c, wk_c, qk_c, qg_c, kt_c, dec_c = inp
        w = u_c - jnp.einsum('bhtd,bhde->bhte', wk_c, state)
        o = jnp.einsum('bhtd,bhde->bhte', qg_c, state) + jnp.einsum('bhtj,bhje->bhte', qk_c, w)
        state = dec_c[..., None, None] * state + jnp.einsum('bhtd,bhte->bhde', kt_c, w)
        return state, o

    state0 = jnp.zeros((bsz, GDN_HEADS, GDN_HEAD_DIM, GDN_HEAD_DIM), F32)
    xs = (jnp.moveaxis(u_new, 2, 0), jnp.moveaxis(w_k, 2, 0), jnp.moveaxis(qk, 2, 0),
          jnp.moveaxis(q_g, 2, 0), jnp.moveaxis(k_tail, 2, 0), jnp.moveaxis(chunk_decay, 2, 0))
    _, o = lax.scan(step, state0, xs)
    o = o.transpose(1, 0, 3, 2, 4).reshape(bsz, seq, GDN_HEADS, GDN_HEAD_DIM)
    o = rms_norm(o, out_g) * jax.nn.silu(gate.astype(F32).reshape(bsz, seq, GDN_HEADS, GDN_HEAD_DIM))
    return o.reshape(bsz, seq, GDN_WIDTH).astype(gate.dtype)


def chunk_attention(h, w_qkv, w_out, q_g, k_g, rel_bias):
    bsz, seq, _ = h.shape
    n = seq // CHUNK
    qkv = (h @ w_qkv).reshape(bsz, seq, 3, CA_HEADS, CA_HEAD_DIM)
    q = rms_norm(qkv[:, :, 0], q_g).astype(F32)
    k = rms_norm(qkv[:, :, 1], k_g).astype(F32)
    v = qkv[:, :, 2].astype(F32)
    pad = CA_LEFT_CHUNKS * CHUNK
    k_pad = jnp.pad(k, ((0, 0), (pad, 0), (0, 0), (0, 0)))
    v_pad = jnp.pad(v, ((0, 0), (pad, 0), (0, 0), (0, 0)))
    rel = np.arange(CHUNK)[:, None] - np.arange(CA_BAND)[None, :] + pad
    rel_idx = np.clip(rel, -MAX_REL_DIST, MAX_REL_DIST) + MAX_REL_DIST
    bias = rel_bias.astype(F32)[:, rel_idx]
    band_pos = jnp.arange(CA_BAND) - pad
    scale = CA_HEAD_DIM ** -0.5

    def one_chunk(c):
        start = c * CHUNK
        q_c = lax.dynamic_slice_in_dim(q, start, CHUNK, axis=1)
        k_c = lax.dynamic_slice_in_dim(k_pad, start, CA_BAND, axis=1)
        v_c = lax.dynamic_slice_in_dim(v_pad, start, CA_BAND, axis=1)
        s = jnp.einsum('bqhd,bkhd->bhqk', q_c, k_c) * scale + bias
        s = jnp.where((band_pos + start) >= 0, s, -1e30)
        p = jax.nn.softmax(s, axis=-1)
        return jnp.einsum('bhqk,bkhd->bqhd', p, v_c)

    o = lax.map(one_chunk, jnp.arange(n))
    o = o.transpose(1, 0, 2, 3, 4).reshape(bsz, seq, D_MODEL).astype(h.dtype)
    return o @ w_out


def memory_cross_attention(h, mem_n, w_q, w_kv, w_out, q_g, k_g):
    bsz, seq, _ = h.shape
    n_mem = mem_n.shape[1]
    q = rms_norm((h @ w_q).reshape(bsz, seq, XA_HEADS, XA_HEAD_DIM), q_g).astype(F32)
    kv = (mem_n @ w_kv).reshape(bsz, n_mem, 2, XA_HEADS, XA_HEAD_DIM)
    k = rms_norm(kv[:, :, 0], k_g).astype(F32)
    v = kv[:, :, 1].astype(F32)
    s = jnp.einsum('bqhd,bkhd->bhqk', q, k) * (XA_HEAD_DIM ** -0.5)
    p = jax.nn.softmax(s, axis=-1)
    o = jnp.einsum('bhqk,bkhd->bqhd', p, v).reshape(bsz, seq, D_MODEL).astype(h.dtype)
    return o @ w_out


def swiglu(h, w_gate, w_up, w_down):
    return (jax.nn.silu(h @ w_gate) * (h @ w_up)) @ w_down


def _fwd_setup_inputs(seed: int = 0) -> dict:
    key = jax.random.key(seed)
    keys = jax.random.split(key, 64)
    counter = iter(range(64))

    def nk():
        return keys[next(counter)]

    def nrm(shape, scale):
        return scale * jax.random.normal(nk(), shape, F32)

    def gain(shape):
        return 1.0 + nrm(shape, 0.02)

    D, NE, NO = D_MODEL, N_EVEN, N_ODD
    G, P, CG = S5_GROUPS, S5_STATE, S5_GROUP
    x = nrm((BATCH, SEQ, D), 1.0)
    mem = nrm((BATCH, N_MEM, D), 1.0)
    ab_norm_g = gain((NE, D))
    ab_w_in = nrm((NE, D, AB_IN), D ** -0.5)
    ab_w_out = nrm((NE, S5_WIDTH + GDN_WIDTH, D), (S5_WIDTH + GDN_WIDTH) ** -0.5)
    s5_a_re = -0.5 + nrm((NE, G, P), 0.01)
    s5_a_im = math.pi * jnp.arange(P, dtype=F32) + nrm((NE, G, P), 0.01)
    s5_log_dt = jax.random.uniform(nk(), (NE, G), F32, math.log(S5_DT_MIN), math.log(S5_DT_MAX))
    s5_b_re = nrm((NE, G, P, CG), (2 * CG) ** -0.5)
    s5_b_im = nrm((NE, G, P, CG), (2 * CG) ** -0.5)
    s5_c_re = nrm((NE, G, CG, P), (2 * P) ** -0.5)
    s5_c_im = nrm((NE, G, CG, P), (2 * P) ** -0.5)
    s5_d = nrm((NE, G, CG), 1.0)
    s5_w_glu = nrm((NE, S5_WIDTH, S5_WIDTH), S5_WIDTH ** -0.5)
    s5_b_glu = nrm((NE, S5_WIDTH), 0.01)
    gdn_conv_w = nrm((NE, GDN_CONV, 3 * GDN_WIDTH), GDN_CONV ** -0.5)
    gdn_a_log = jnp.log(jax.random.uniform(nk(), (NE, GDN_HEADS), F32, 1.0, 16.0))
    gdn_dt = jnp.exp(jax.random.uniform(nk(), (NE, GDN_HEADS), F32, math.log(GDN_DT_MIN), math.log(GDN_DT_MAX)))
    gdn_dt_bias = gdn_dt + jnp.log(-jnp.expm1(-gdn_dt))
    gdn_out_norm_g = gain((NE, GDN_HEAD_DIM))
    c_norm_g = gain((NO, D))
    c_w_qkv = nrm((NO, D, 3 * D), D ** -0.5)
    c_w_out = nrm((NO, D, D), D ** -0.5)
    c_q_norm_g = gain((NO, CA_HEAD_DIM))
    c_k_norm_g = gain((NO, CA_HEAD_DIM))
    c_rel_bias = nrm((NO, CA_HEADS, 2 * MAX_REL_DIST + 1), 0.1)
    mem_norm_g = gain((D,))
    xa_norm_g = gain((DEPTH, D))
    xa_w_q = nrm((DEPTH, D, D), D ** -0.5)
    xa_w_kv = nrm((DEPTH, D, 2 * D), D ** -0.5)
    xa_w_out = nrm((DEPTH, D, D), D ** -0.5)
    xa_q_norm_g = gain((DEPTH, XA_HEAD_DIM))
    xa_k_norm_g = gain((DEPTH, XA_HEAD_DIM))
    f_norm_g = gain((DEPTH, D))
    f_w_gate = nrm((DEPTH, D, FFN_HIDDEN), D ** -0.5)
    f_w_up = nrm((DEPTH, D, FFN_HIDDEN), D ** -0.5)
    f_w_down = nrm((DEPTH, FFN_HIDDEN, D), FFN_HIDDEN ** -0.5)
    return {
        "x": x, "mem": mem,
        "ab_norm_g": ab_norm_g, "ab_w_in": ab_w_in, "ab_w_out": ab_w_out,
        "s5_a_re": s5_a_re, "s5_a_im": s5_a_im, "s5_log_dt": s5_log_dt,
        "s5_b_re": s5_b_re, "s5_b_im": s5_b_im, "s5_c_re": s5_c_re, "s5_c_im": s5_c_im,
        "s5_d": s5_d, "s5_w_glu": s5_w_glu, "s5_b_glu": s5_b_glu,
        "gdn_conv_w": gdn_conv_w, "gdn_a_log": gdn_a_log, "gdn_dt_bias": gdn_dt_bias,
        "gdn_out_norm_g": gdn_out_norm_g,
        "c_norm_g": c_norm_g, "c_w_qkv": c_w_qkv, "c_w_out": c_w_out,
        "c_q_norm_g": c_q_norm_g, "c_k_norm_g": c_k_norm_g, "c_rel_bias": c_rel_bias,
        "mem_norm_g": mem_norm_g, "xa_norm_g": xa_norm_g, "xa_w_q": xa_w_q,
        "xa_w_kv": xa_w_kv, "xa_w_out": xa_w_out, "xa_q_norm_g": xa_q_norm_g,
        "xa_k_norm_g": xa_k_norm_g,
        "f_norm_g": f_norm_g, "f_w_gate": f_w_gate, "f_w_up": f_w_up, "f_w_down": f_w_down,
    }


def _fwd_reference(x, mem, ab_norm_g, ab_w_in, ab_w_out, s5_a_re, s5_a_im, s5_log_dt,
              s5_b_re, s5_b_im, s5_c_re, s5_c_im, s5_d, s5_w_glu, s5_b_glu,
              gdn_conv_w, gdn_a_log, gdn_dt_bias, gdn_out_norm_g,
              c_norm_g, c_w_qkv, c_w_out, c_q_norm_g, c_k_norm_g, c_rel_bias,
              mem_norm_g, xa_norm_g, xa_w_q, xa_w_kv, xa_w_out, xa_q_norm_g, xa_k_norm_g,
              f_norm_g, f_w_gate, f_w_up, f_w_down):
    split_points = [S5_WIDTH, S5_WIDTH + GDN_WIDTH, S5_WIDTH + 2 * GDN_WIDTH,
                    S5_WIDTH + 3 * GDN_WIDTH, S5_WIDTH + 4 * GDN_WIDTH,
                    S5_WIDTH + 4 * GDN_WIDTH + GDN_HEADS]
    mem_n = rms_norm(mem, mem_norm_g)
    for layer in range(DEPTH):
        i = layer // 2
        if layer % 2 == 0:
            h = rms_norm(x, ab_norm_g[i])
            proj = h @ ab_w_in[i]
            u, q, k, v, gate, a_logit, b_logit = jnp.split(proj, split_points, axis=-1)
            a_out = s5_mixer(u, s5_a_re[i], s5_a_im[i], s5_log_dt[i], s5_b_re[i], s5_b_im[i],
                             s5_c_re[i], s5_c_im[i], s5_d[i], s5_w_glu[i], s5_b_glu[i])
            b_out = gated_deltanet(q, k, v, gate, a_logit, b_logit, gdn_conv_w[i],
                                   gdn_a_log[i], gdn_dt_bias[i], gdn_out_norm_g[i])
            x = x + jnp.concatenate([a_out, b_out], axis=-1) @ ab_w_out[i]
        else:
            x = x + chunk_attention(rms_norm(x, c_norm_g[i]), c_w_qkv[i], c_w_out[i],
                                    c_q_norm_g[i], c_k_norm_g[i], c_rel_bias[i])
        x = x + memory_cross_attention(rms_norm(x, xa_norm_g[layer]), mem_n, xa_w_q[layer],
                                       xa_w_kv[layer], xa_w_out[layer],
                                       xa_q_norm_g[layer], xa_k_norm_g[layer])
        x = x + swiglu(rms_norm(x, f_norm_g[layer]), f_w_gate[layer], f_w_up[layer], f_w_down[layer])
    return x


import jax as _jax
import jax.numpy as _jnp

TWIN_FORMAT = 'train_step'
FWD_PARAMS = ['x', 'mem', 'ab_norm_g', 'ab_w_in', 'ab_w_out', 's5_a_re', 's5_a_im', 's5_log_dt', 's5_b_re', 's5_b_im', 's5_c_re', 's5_c_im', 's5_d', 's5_w_glu', 's5_b_glu', 'gdn_conv_w', 'gdn_a_log', 'gdn_dt_bias', 'gdn_out_norm_g', 'c_norm_g', 'c_w_qkv', 'c_w_out', 'c_q_norm_g', 'c_k_norm_g', 'c_rel_bias', 'mem_norm_g', 'xa_norm_g', 'xa_w_q', 'xa_w_kv', 'xa_w_out', 'xa_q_norm_g', 'xa_k_norm_g', 'f_norm_g', 'f_w_gate', 'f_w_up', 'f_w_down']
TWIN_WEIGHTS = ['ab_norm_g', 'ab_w_in', 'ab_w_out', 's5_a_re', 's5_a_im', 's5_log_dt', 's5_b_re', 's5_b_im', 's5_c_re', 's5_c_im', 's5_d', 's5_w_glu', 's5_b_glu', 'gdn_conv_w', 'gdn_a_log', 'gdn_dt_bias', 'gdn_out_norm_g', 'c_norm_g', 'c_w_qkv', 'c_w_out', 'c_q_norm_g', 'c_k_norm_g', 'c_rel_bias', 'mem_norm_g', 'xa_norm_g', 'xa_w_q', 'xa_w_kv', 'xa_w_out', 'xa_q_norm_g', 'xa_k_norm_g', 'f_norm_g', 'f_w_gate', 'f_w_up', 'f_w_down']
TWIN_DIFF_INPUT = 'x'
TWIN_INPUTS = ['x', 'mem', 'ab_norm_g', 'ab_w_in', 'ab_w_out', 's5_a_re', 's5_a_im', 's5_log_dt', 's5_b_re', 's5_b_im', 's5_c_re', 's5_c_im', 's5_d', 's5_w_glu', 's5_b_glu', 'gdn_conv_w', 'gdn_a_log', 'gdn_dt_bias', 'gdn_out_norm_g', 'c_norm_g', 'c_w_qkv', 'c_w_out', 'c_q_norm_g', 'c_k_norm_g', 'c_rel_bias', 'mem_norm_g', 'xa_norm_g', 'xa_w_q', 'xa_w_kv', 'xa_w_out', 'xa_q_norm_g', 'xa_k_norm_g', 'f_norm_g', 'f_w_gate', 'f_w_up', 'f_w_down', 'loss_target', 'm_ab_norm_g', 'm_ab_w_in', 'm_ab_w_out', 'm_s5_a_re', 'm_s5_a_im', 'm_s5_log_dt', 'm_s5_b_re', 'm_s5_b_im', 'm_s5_c_re', 'm_s5_c_im', 'm_s5_d', 'm_s5_w_glu', 'm_s5_b_glu', 'm_gdn_conv_w', 'm_gdn_a_log', 'm_gdn_dt_bias', 'm_gdn_out_norm_g', 'm_c_norm_g', 'm_c_w_qkv', 'm_c_w_out', 'm_c_q_norm_g', 'm_c_k_norm_g', 'm_c_rel_bias', 'm_mem_norm_g', 'm_xa_norm_g', 'm_xa_w_q', 'm_xa_w_kv', 'm_xa_w_out', 'm_xa_q_norm_g', 'm_xa_k_norm_g', 'm_f_norm_g', 'm_f_w_gate', 'm_f_w_up', 'm_f_w_down', 'v_ab_norm_g', 'v_ab_w_in', 'v_ab_w_out', 'v_s5_a_re', 'v_s5_a_im', 'v_s5_log_dt', 'v_s5_b_re', 'v_s5_b_im', 'v_s5_c_re', 'v_s5_c_im', 'v_s5_d', 'v_s5_w_glu', 'v_s5_b_glu', 'v_gdn_conv_w', 'v_gdn_a_log', 'v_gdn_dt_bias', 'v_gdn_out_norm_g', 'v_c_norm_g', 'v_c_w_qkv', 'v_c_w_out', 'v_c_q_norm_g', 'v_c_k_norm_g', 'v_c_rel_bias', 'v_mem_norm_g', 'v_xa_norm_g', 'v_xa_w_q', 'v_xa_w_kv', 'v_xa_w_out', 'v_xa_q_norm_g', 'v_xa_k_norm_g', 'v_f_norm_g', 'v_f_w_gate', 'v_f_w_up', 'v_f_w_down']
TWIN_OUTPUTS = ['loss', 'grad_x', 'grad_ab_norm_g', 'grad_ab_w_in', 'grad_ab_w_out', 'grad_s5_a_re', 'grad_s5_a_im', 'grad_s5_log_dt', 'grad_s5_b_re', 'grad_s5_b_im', 'grad_s5_c_re', 'grad_s5_c_im', 'grad_s5_d', 'grad_s5_w_glu', 'grad_s5_b_glu', 'grad_gdn_conv_w', 'grad_gdn_a_log', 'grad_gdn_dt_bias', 'grad_gdn_out_norm_g', 'grad_c_norm_g', 'grad_c_w_qkv', 'grad_c_w_out', 'grad_c_q_norm_g', 'grad_c_k_norm_g', 'grad_c_rel_bias', 'grad_mem_norm_g', 'grad_xa_norm_g', 'grad_xa_w_q', 'grad_xa_w_kv', 'grad_xa_w_out', 'grad_xa_q_norm_g', 'grad_xa_k_norm_g', 'grad_f_norm_g', 'grad_f_w_gate', 'grad_f_w_up', 'grad_f_w_down', 'delta_ab_norm_g', 'delta_ab_w_in', 'delta_ab_w_out', 'delta_s5_a_re', 'delta_s5_a_im', 'delta_s5_log_dt', 'delta_s5_b_re', 'delta_s5_b_im', 'delta_s5_c_re', 'delta_s5_c_im', 'delta_s5_d', 'delta_s5_w_glu', 'delta_s5_b_glu', 'delta_gdn_conv_w', 'delta_gdn_a_log', 'delta_gdn_dt_bias', 'delta_gdn_out_norm_g', 'delta_c_norm_g', 'delta_c_w_qkv', 'delta_c_w_out', 'delta_c_q_norm_g', 'delta_c_k_norm_g', 'delta_c_rel_bias', 'delta_mem_norm_g', 'delta_xa_norm_g', 'delta_xa_w_q', 'delta_xa_w_kv', 'delta_xa_w_out', 'delta_xa_q_norm_g', 'delta_xa_k_norm_g', 'delta_f_norm_g', 'delta_f_w_gate', 'delta_f_w_up', 'delta_f_w_down', 'new_m_ab_norm_g', 'new_m_ab_w_in', 'new_m_ab_w_out', 'new_m_s5_a_re', 'new_m_s5_a_im', 'new_m_s5_log_dt', 'new_m_s5_b_re', 'new_m_s5_b_im', 'new_m_s5_c_re', 'new_m_s5_c_im', 'new_m_s5_d', 'new_m_s5_w_glu', 'new_m_s5_b_glu', 'new_m_gdn_conv_w', 'new_m_gdn_a_log', 'new_m_gdn_dt_bias', 'new_m_gdn_out_norm_g', 'new_m_c_norm_g', 'new_m_c_w_qkv', 'new_m_c_w_out', 'new_m_c_q_norm_g', 'new_m_c_k_norm_g', 'new_m_c_rel_bias', 'new_m_mem_norm_g', 'new_m_xa_norm_g', 'new_m_xa_w_q', 'new_m_xa_w_kv', 'new_m_xa_w_out', 'new_m_xa_q_norm_g', 'new_m_xa_k_norm_g', 'new_m_f_norm_g', 'new_m_f_w_gate', 'new_m_f_w_up', 'new_m_f_w_down', 'new_v_ab_norm_g', 'new_v_ab_w_in', 'new_v_ab_w_out', 'new_v_s5_a_re', 'new_v_s5_a_im', 'new_v_s5_log_dt', 'new_v_s5_b_re', 'new_v_s5_b_im', 'new_v_s5_c_re', 'new_v_s5_c_im', 'new_v_s5_d', 'new_v_s5_w_glu', 'new_v_s5_b_glu', 'new_v_gdn_conv_w', 'new_v_gdn_a_log', 'new_v_gdn_dt_bias', 'new_v_gdn_out_norm_g', 'new_v_c_norm_g', 'new_v_c_w_qkv', 'new_v_c_w_out', 'new_v_c_q_norm_g', 'new_v_c_k_norm_g', 'new_v_c_rel_bias', 'new_v_mem_norm_g', 'new_v_xa_norm_g', 'new_v_xa_w_q', 'new_v_xa_w_kv', 'new_v_xa_w_out', 'new_v_xa_q_norm_g', 'new_v_xa_k_norm_g', 'new_v_f_norm_g', 'new_v_f_w_gate', 'new_v_f_w_up', 'new_v_f_w_down']
TWIN_LEAF_KINDS = {'loss': 'loss', 'grad_x': 'grad_x', 'grad_ab_norm_g': 'grad_w', 'grad_ab_w_in': 'grad_w', 'grad_ab_w_out': 'grad_w', 'grad_s5_a_re': 'grad_w', 'grad_s5_a_im': 'grad_w', 'grad_s5_log_dt': 'grad_w', 'grad_s5_b_re': 'grad_w', 'grad_s5_b_im': 'grad_w', 'grad_s5_c_re': 'grad_w', 'grad_s5_c_im': 'grad_w', 'grad_s5_d': 'grad_w', 'grad_s5_w_glu': 'grad_w', 'grad_s5_b_glu': 'grad_w', 'grad_gdn_conv_w': 'grad_w', 'grad_gdn_a_log': 'grad_w', 'grad_gdn_dt_bias': 'grad_w', 'grad_gdn_out_norm_g': 'grad_w', 'grad_c_norm_g': 'grad_w', 'grad_c_w_qkv': 'grad_w', 'grad_c_w_out': 'grad_w', 'grad_c_q_norm_g': 'grad_w', 'grad_c_k_norm_g': 'grad_w', 'grad_c_rel_bias': 'grad_w', 'grad_mem_norm_g': 'grad_w', 'grad_xa_norm_g': 'grad_w', 'grad_xa_w_q': 'grad_w', 'grad_xa_w_kv': 'grad_w', 'grad_xa_w_out': 'grad_w', 'grad_xa_q_norm_g': 'grad_w', 'grad_xa_k_norm_g': 'grad_w', 'grad_f_norm_g': 'grad_w', 'grad_f_w_gate': 'grad_w', 'grad_f_w_up': 'grad_w', 'grad_f_w_down': 'grad_w', 'delta_ab_norm_g': 'delta_w', 'delta_ab_w_in': 'delta_w', 'delta_ab_w_out': 'delta_w', 'delta_s5_a_re': 'delta_w', 'delta_s5_a_im': 'delta_w', 'delta_s5_log_dt': 'delta_w', 'delta_s5_b_re': 'delta_w', 'delta_s5_b_im': 'delta_w', 'delta_s5_c_re': 'delta_w', 'delta_s5_c_im': 'delta_w', 'delta_s5_d': 'delta_w', 'delta_s5_w_glu': 'delta_w', 'delta_s5_b_glu': 'delta_w', 'delta_gdn_conv_w': 'delta_w', 'delta_gdn_a_log': 'delta_w', 'delta_gdn_dt_bias': 'delta_w', 'delta_gdn_out_norm_g': 'delta_w', 'delta_c_norm_g': 'delta_w', 'delta_c_w_qkv': 'delta_w', 'delta_c_w_out': 'delta_w', 'delta_c_q_norm_g': 'delta_w', 'delta_c_k_norm_g': 'delta_w', 'delta_c_rel_bias': 'delta_w', 'delta_mem_norm_g': 'delta_w', 'delta_xa_norm_g': 'delta_w', 'delta_xa_w_q': 'delta_w', 'delta_xa_w_kv': 'delta_w', 'delta_xa_w_out': 'delta_w', 'delta_xa_q_norm_g': 'delta_w', 'delta_xa_k_norm_g': 'delta_w', 'delta_f_norm_g': 'delta_w', 'delta_f_w_gate': 'delta_w', 'delta_f_w_up': 'delta_w', 'delta_f_w_down': 'delta_w', 'new_m_ab_norm_g': 'new_m', 'new_m_ab_w_in': 'new_m', 'new_m_ab_w_out': 'new_m', 'new_m_s5_a_re': 'new_m', 'new_m_s5_a_im': 'new_m', 'new_m_s5_log_dt': 'new_m', 'new_m_s5_b_re': 'new_m', 'new_m_s5_b_im': 'new_m', 'new_m_s5_c_re': 'new_m', 'new_m_s5_c_im': 'new_m', 'new_m_s5_d': 'new_m', 'new_m_s5_w_glu': 'new_m', 'new_m_s5_b_glu': 'new_m', 'new_m_gdn_conv_w': 'new_m', 'new_m_gdn_a_log': 'new_m', 'new_m_gdn_dt_bias': 'new_m', 'new_m_gdn_out_norm_g': 'new_m', 'new_m_c_norm_g': 'new_m', 'new_m_c_w_qkv': 'new_m', 'new_m_c_w_out': 'new_m', 'new_m_c_q_norm_g': 'new_m', 'new_m_c_k_norm_g': 'new_m', 'new_m_c_rel_bias': 'new_m', 'new_m_mem_norm_g': 'new_m', 'new_m_xa_norm_g': 'new_m', 'new_m_xa_w_q': 'new_m', 'new_m_xa_w_kv': 'new_m', 'new_m_xa_w_out': 'new_m', 'new_m_xa_q_norm_g': 'new_m', 'new_m_xa_k_norm_g': 'new_m', 'new_m_f_norm_g': 'new_m', 'new_m_f_w_gate': 'new_m', 'new_m_f_w_up': 'new_m', 'new_m_f_w_down': 'new_m', 'new_v_ab_norm_g': 'new_v', 'new_v_ab_w_in': 'new_v', 'new_v_ab_w_out': 'new_v', 'new_v_s5_a_re': 'new_v', 'new_v_s5_a_im': 'new_v', 'new_v_s5_log_dt': 'new_v', 'new_v_s5_b_re': 'new_v', 'new_v_s5_b_im': 'new_v', 'new_v_s5_c_re': 'new_v', 'new_v_s5_c_im': 'new_v', 'new_v_s5_d': 'new_v', 'new_v_s5_w_glu': 'new_v', 'new_v_s5_b_glu': 'new_v', 'new_v_gdn_conv_w': 'new_v', 'new_v_gdn_a_log': 'new_v', 'new_v_gdn_dt_bias': 'new_v', 'new_v_gdn_out_norm_g': 'new_v', 'new_v_c_norm_g': 'new_v', 'new_v_c_w_qkv': 'new_v', 'new_v_c_w_out': 'new_v', 'new_v_c_q_norm_g': 'new_v', 'new_v_c_k_norm_g': 'new_v', 'new_v_c_rel_bias': 'new_v', 'new_v_mem_norm_g': 'new_v', 'new_v_xa_norm_g': 'new_v', 'new_v_xa_w_q': 'new_v', 'new_v_xa_w_kv': 'new_v', 'new_v_xa_w_out': 'new_v', 'new_v_xa_q_norm_g': 'new_v', 'new_v_xa_k_norm_g': 'new_v', 'new_v_f_norm_g': 'new_v', 'new_v_f_w_gate': 'new_v', 'new_v_f_w_up': 'new_v', 'new_v_f_w_down': 'new_v'}


def _forward(args):
    return _fwd_reference(*[args[k] for k in FWD_PARAMS])


def _output_shape():
    out = _jax.eval_shape(lambda: _forward(_fwd_setup_inputs(0)))
    return out.shape, out.dtype

N_MICROBATCH = 1
ADAM_LR = 0.001
ADAM_B1 = 0.9
ADAM_B2 = 0.999
ADAM_EPS = 1e-08
ADAM_WD = 0.01
ADAM_STEP = 10
PER_EXAMPLE_BATCH_AXIS = {'x': 0, 'mem': 0, 'loss_target': 0}
SHARED_INPUTS = []
_WEIGHT_DTYPES = {'ab_norm_g': _jnp.float32, 'ab_w_in': _jnp.float32, 'ab_w_out': _jnp.float32, 's5_a_re': _jnp.float32, 's5_a_im': _jnp.float32, 's5_log_dt': _jnp.float32, 's5_b_re': _jnp.float32, 's5_b_im': _jnp.float32, 's5_c_re': _jnp.float32, 's5_c_im': _jnp.float32, 's5_d': _jnp.float32, 's5_w_glu': _jnp.float32, 's5_b_glu': _jnp.float32, 'gdn_conv_w': _jnp.float32, 'gdn_a_log': _jnp.float32, 'gdn_dt_bias': _jnp.float32, 'gdn_out_norm_g': _jnp.float32, 'c_norm_g': _jnp.float32, 'c_w_qkv': _jnp.float32, 'c_w_out': _jnp.float32, 'c_q_norm_g': _jnp.float32, 'c_k_norm_g': _jnp.float32, 'c_rel_bias': _jnp.float32, 'mem_norm_g': _jnp.float32, 'xa_norm_g': _jnp.float32, 'xa_w_q': _jnp.float32, 'xa_w_kv': _jnp.float32, 'xa_w_out': _jnp.float32, 'xa_q_norm_g': _jnp.float32, 'xa_k_norm_g': _jnp.float32, 'f_norm_g': _jnp.float32, 'f_w_gate': _jnp.float32, 'f_w_up': _jnp.float32, 'f_w_down': _jnp.float32}
MOMENT_SCALE = {'ab_norm_g': 4.681469e+00, 'ab_w_in': 4.230180e-01, 'ab_w_out': 1.484229e+00, 's5_a_re': 3.015901e-02, 's5_a_im': 2.763851e-02, 's5_log_dt': 4.919969e+00, 's5_b_re': 2.091005e-02, 's5_b_im': 2.448491e-02, 's5_c_re': 3.779086e-02, 's5_c_im': 3.376434e-02, 's5_d': 3.555027e+00, 's5_w_glu': 6.840779e-01, 's5_b_glu': 1.955260e+00, 'gdn_conv_w': 6.260332e-01, 'gdn_a_log': 1.868575e+01, 'gdn_dt_bias': 1.777327e+01, 'gdn_out_norm_g': 2.381232e+01, 'c_norm_g': 1.161038e+00, 'c_w_qkv': 5.428727e-01, 'c_w_out': 8.596002e-01, 'c_q_norm_g': 1.025378e+00, 'c_k_norm_g': 1.026313e+00, 'c_rel_bias': 2.876254e-02, 'mem_norm_g': 1.001263e+00, 'xa_norm_g': 7.446973e-02, 'xa_w_q': 7.408860e-02, 'xa_w_kv': 2.014295e-01, 'xa_w_out': 2.759236e-01, 'xa_q_norm_g': 6.285641e-01, 'xa_k_norm_g': 6.294928e-01, 'f_norm_g': 1.225469e+01, 'f_w_gate': 2.729282e-01, 'f_w_up': 2.327772e-01, 'f_w_down': 3.714890e-01}


def _to_microbatches(a, axis):
    t = _jnp.moveaxis(a, axis, 0)
    t = t.reshape((N_MICROBATCH, t.shape[0] // N_MICROBATCH) + t.shape[1:])
    return _jnp.moveaxis(t, 1, axis + 1)


def setup_inputs(seed: int = 0) -> dict:
    inp = _fwd_setup_inputs(seed)
    key = _jax.random.fold_in(_jax.random.key(seed), 7919)
    shape, _ = _output_shape()
    out = dict(inp)
    out["loss_target"] = _jax.random.normal(_jax.random.fold_in(key, 0), shape, _jnp.float32)
    for i, name in enumerate(TWIN_WEIGHTS):
        w = inp[name].astype(_jnp.float32)
        if MOMENT_SCALE is None:
            s = _jnp.sqrt(_jnp.mean(_jnp.square(w)) + 1e-30)
        else:
            s = MOMENT_SCALE[name]
        km, kv = _jax.random.split(_jax.random.fold_in(key, i + 1))
        out[name] = w
        out["m_" + name] = s * _jax.random.normal(km, w.shape, _jnp.float32)
        out["v_" + name] = (s * s) * _jax.random.uniform(kv, w.shape, _jnp.float32, 0.5, 1.5)
    if N_MICROBATCH > 1:
        for name, axis in PER_EXAMPLE_BATCH_AXIS.items():
            out[name] = _to_microbatches(out[name], axis)
    return {'x': out['x'], 'mem': out['mem'], 'ab_norm_g': out['ab_norm_g'], 'ab_w_in': out['ab_w_in'], 'ab_w_out': out['ab_w_out'], 's5_a_re': out['s5_a_re'], 's5_a_im': out['s5_a_im'], 's5_log_dt': out['s5_log_dt'], 's5_b_re': out['s5_b_re'], 's5_b_im': out['s5_b_im'], 's5_c_re': out['s5_c_re'], 's5_c_im': out['s5_c_im'], 's5_d': out['s5_d'], 's5_w_glu': out['s5_w_glu'], 's5_b_glu': out['s5_b_glu'], 'gdn_conv_w': out['gdn_conv_w'], 'gdn_a_log': out['gdn_a_log'], 'gdn_dt_bias': out['gdn_dt_bias'], 'gdn_out_norm_g': out['gdn_out_norm_g'], 'c_norm_g': out['c_norm_g'], 'c_w_qkv': out['c_w_qkv'], 'c_w_out': out['c_w_out'], 'c_q_norm_g': out['c_q_norm_g'], 'c_k_norm_g': out['c_k_norm_g'], 'c_rel_bias': out['c_rel_bias'], 'mem_norm_g': out['mem_norm_g'], 'xa_norm_g': out['xa_norm_g'], 'xa_w_q': out['xa_w_q'], 'xa_w_kv': out['xa_w_kv'], 'xa_w_out': out['xa_w_out'], 'xa_q_norm_g': out['xa_q_norm_g'], 'xa_k_norm_g': out['xa_k_norm_g'], 'f_norm_g': out['f_norm_g'], 'f_w_gate': out['f_w_gate'], 'f_w_up': out['f_w_up'], 'f_w_down': out['f_w_down'], 'loss_target': out['loss_target'], 'm_ab_norm_g': out['m_ab_norm_g'], 'm_ab_w_in': out['m_ab_w_in'], 'm_ab_w_out': out['m_ab_w_out'], 'm_s5_a_re': out['m_s5_a_re'], 'm_s5_a_im': out['m_s5_a_im'], 'm_s5_log_dt': out['m_s5_log_dt'], 'm_s5_b_re': out['m_s5_b_re'], 'm_s5_b_im': out['m_s5_b_im'], 'm_s5_c_re': out['m_s5_c_re'], 'm_s5_c_im': out['m_s5_c_im'], 'm_s5_d': out['m_s5_d'], 'm_s5_w_glu': out['m_s5_w_glu'], 'm_s5_b_glu': out['m_s5_b_glu'], 'm_gdn_conv_w': out['m_gdn_conv_w'], 'm_gdn_a_log': out['m_gdn_a_log'], 'm_gdn_dt_bias': out['m_gdn_dt_bias'], 'm_gdn_out_norm_g': out['m_gdn_out_norm_g'], 'm_c_norm_g': out['m_c_norm_g'], 'm_c_w_qkv': out['m_c_w_qkv'], 'm_c_w_out': out['m_c_w_out'], 'm_c_q_norm_g': out['m_c_q_norm_g'], 'm_c_k_norm_g': out['m_c_k_norm_g'], 'm_c_rel_bias': out['m_c_rel_bias'], 'm_mem_norm_g': out['m_mem_norm_g'], 'm_xa_norm_g': out['m_xa_norm_g'], 'm_xa_w_q': out['m_xa_w_q'], 'm_xa_w_kv': out['m_xa_w_kv'], 'm_xa_w_out': out['m_xa_w_out'], 'm_xa_q_norm_g': out['m_xa_q_norm_g'], 'm_xa_k_norm_g': out['m_xa_k_norm_g'], 'm_f_norm_g': out['m_f_norm_g'], 'm_f_w_gate': out['m_f_w_gate'], 'm_f_w_up': out['m_f_w_up'], 'm_f_w_down': out['m_f_w_down'], 'v_ab_norm_g': out['v_ab_norm_g'], 'v_ab_w_in': out['v_ab_w_in'], 'v_ab_w_out': out['v_ab_w_out'], 'v_s5_a_re': out['v_s5_a_re'], 'v_s5_a_im': out['v_s5_a_im'], 'v_s5_log_dt': out['v_s5_log_dt'], 'v_s5_b_re': out['v_s5_b_re'], 'v_s5_b_im': out['v_s5_b_im'], 'v_s5_c_re': out['v_s5_c_re'], 'v_s5_c_im': out['v_s5_c_im'], 'v_s5_d': out['v_s5_d'], 'v_s5_w_glu': out['v_s5_w_glu'], 'v_s5_b_glu': out['v_s5_b_glu'], 'v_gdn_conv_w': out['v_gdn_conv_w'], 'v_gdn_a_log': out['v_gdn_a_log'], 'v_gdn_dt_bias': out['v_gdn_dt_bias'], 'v_gdn_out_norm_g': out['v_gdn_out_norm_g'], 'v_c_norm_g': out['v_c_norm_g'], 'v_c_w_qkv': out['v_c_w_qkv'], 'v_c_w_out': out['v_c_w_out'], 'v_c_q_norm_g': out['v_c_q_norm_g'], 'v_c_k_norm_g': out['v_c_k_norm_g'], 'v_c_rel_bias': out['v_c_rel_bias'], 'v_mem_norm_g': out['v_mem_norm_g'], 'v_xa_norm_g': out['v_xa_norm_g'], 'v_xa_w_q': out['v_xa_w_q'], 'v_xa_w_kv': out['v_xa_w_kv'], 'v_xa_w_out': out['v_xa_w_out'], 'v_xa_q_norm_g': out['v_xa_q_norm_g'], 'v_xa_k_norm_g': out['v_xa_k_norm_g'], 'v_f_norm_g': out['v_f_norm_g'], 'v_f_w_gate': out['v_f_w_gate'], 'v_f_w_up': out['v_f_w_up'], 'v_f_w_down': out['v_f_w_down']}


def _loss(weights, diff, rest, loss_target):
    with _jax.named_scope("forward"):
        args = {**rest, TWIN_DIFF_INPUT: diff, **{k: w.astype(_WEIGHT_DTYPES[k]) for k, w in weights.items()}}
        y = _forward(args)
    with _jax.named_scope("loss_head"):
        err = _jnp.square(y.astype(_jnp.float32) - loss_target)
        return 0.5 * _jnp.sum(_jnp.mean(err, axis=-1)) if err.ndim else 0.5 * err


def _adamw(w, g, m, v):
    m = ADAM_B1 * m + (1.0 - ADAM_B1) * g
    v = ADAM_B2 * v + (1.0 - ADAM_B2) * _jnp.square(g)
    m_hat = m / (1.0 - ADAM_B1 ** ADAM_STEP)
    v_hat = v / (1.0 - ADAM_B2 ** ADAM_STEP)
    delta = -ADAM_LR * (m_hat / (_jnp.sqrt(v_hat) + ADAM_EPS) + ADAM_WD * w)
    return delta, m, v


def reference(x, mem, ab_norm_g, ab_w_in, ab_w_out, s5_a_re, s5_a_im, s5_log_dt, s5_b_re, s5_b_im, s5_c_re, s5_c_im, s5_d, s5_w_glu, s5_b_glu, gdn_conv_w, gdn_a_log, gdn_dt_bias, gdn_out_norm_g, c_norm_g, c_w_qkv, c_w_out, c_q_norm_g, c_k_norm_g, c_rel_bias, mem_norm_g, xa_norm_g, xa_w_q, xa_w_kv, xa_w_out, xa_q_norm_g, xa_k_norm_g, f_norm_g, f_w_gate, f_w_up, f_w_down, loss_target, m_ab_norm_g, m_ab_w_in, m_ab_w_out, m_s5_a_re, m_s5_a_im, m_s5_log_dt, m_s5_b_re, m_s5_b_im, m_s5_c_re, m_s5_c_im, m_s5_d, m_s5_w_glu, m_s5_b_glu, m_gdn_conv_w, m_gdn_a_log, m_gdn_dt_bias, m_gdn_out_norm_g, m_c_norm_g, m_c_w_qkv, m_c_w_out, m_c_q_norm_g, m_c_k_norm_g, m_c_rel_bias, m_mem_norm_g, m_xa_norm_g, m_xa_w_q, m_xa_w_kv, m_xa_w_out, m_xa_q_norm_g, m_xa_k_norm_g, m_f_norm_g, m_f_w_gate, m_f_w_up, m_f_w_down, v_ab_norm_g, v_ab_w_in, v_ab_w_out, v_s5_a_re, v_s5_a_im, v_s5_log_dt, v_s5_b_re, v_s5_b_im, v_s5_c_re, v_s5_c_im, v_s5_d, v_s5_w_glu, v_s5_b_glu, v_gdn_conv_w, v_gdn_a_log, v_gdn_dt_bias, v_gdn_out_norm_g, v_c_norm_g, v_c_w_qkv, v_c_w_out, v_c_q_norm_g, v_c_k_norm_g, v_c_rel_bias, v_mem_norm_g, v_xa_norm_g, v_xa_w_q, v_xa_w_kv, v_xa_w_out, v_xa_q_norm_g, v_xa_k_norm_g, v_f_norm_g, v_f_w_gate, v_f_w_up, v_f_w_down):
    given = dict(x=x, mem=mem, ab_norm_g=ab_norm_g, ab_w_in=ab_w_in, ab_w_out=ab_w_out, s5_a_re=s5_a_re, s5_a_im=s5_a_im, s5_log_dt=s5_log_dt, s5_b_re=s5_b_re, s5_b_im=s5_b_im, s5_c_re=s5_c_re, s5_c_im=s5_c_im, s5_d=s5_d, s5_w_glu=s5_w_glu, s5_b_glu=s5_b_glu, gdn_conv_w=gdn_conv_w, gdn_a_log=gdn_a_log, gdn_dt_bias=gdn_dt_bias, gdn_out_norm_g=gdn_out_norm_g, c_norm_g=c_norm_g, c_w_qkv=c_w_qkv, c_w_out=c_w_out, c_q_norm_g=c_q_norm_g, c_k_norm_g=c_k_norm_g, c_rel_bias=c_rel_bias, mem_norm_g=mem_norm_g, xa_norm_g=xa_norm_g, xa_w_q=xa_w_q, xa_w_kv=xa_w_kv, xa_w_out=xa_w_out, xa_q_norm_g=xa_q_norm_g, xa_k_norm_g=xa_k_norm_g, f_norm_g=f_norm_g, f_w_gate=f_w_gate, f_w_up=f_w_up, f_w_down=f_w_down, loss_target=loss_target, m_ab_norm_g=m_ab_norm_g, m_ab_w_in=m_ab_w_in, m_ab_w_out=m_ab_w_out, m_s5_a_re=m_s5_a_re, m_s5_a_im=m_s5_a_im, m_s5_log_dt=m_s5_log_dt, m_s5_b_re=m_s5_b_re, m_s5_b_im=m_s5_b_im, m_s5_c_re=m_s5_c_re, m_s5_c_im=m_s5_c_im, m_s5_d=m_s5_d, m_s5_w_glu=m_s5_w_glu, m_s5_b_glu=m_s5_b_glu, m_gdn_conv_w=m_gdn_conv_w, m_gdn_a_log=m_gdn_a_log, m_gdn_dt_bias=m_gdn_dt_bias, m_gdn_out_norm_g=m_gdn_out_norm_g, m_c_norm_g=m_c_norm_g, m_c_w_qkv=m_c_w_qkv, m_c_w_out=m_c_w_out, m_c_q_norm_g=m_c_q_norm_g, m_c_k_norm_g=m_c_k_norm_g, m_c_rel_bias=m_c_rel_bias, m_mem_norm_g=m_mem_norm_g, m_xa_norm_g=m_xa_norm_g, m_xa_w_q=m_xa_w_q, m_xa_w_kv=m_xa_w_kv, m_xa_w_out=m_xa_w_out, m_xa_q_norm_g=m_xa_q_norm_g, m_xa_k_norm_g=m_xa_k_norm_g, m_f_norm_g=m_f_norm_g, m_f_w_gate=m_f_w_gate, m_f_w_up=m_f_w_up, m_f_w_down=m_f_w_down, v_ab_norm_g=v_ab_norm_g, v_ab_w_in=v_ab_w_in, v_ab_w_out=v_ab_w_out, v_s5_a_re=v_s5_a_re, v_s5_a_im=v_s5_a_im, v_s5_log_dt=v_s5_log_dt, v_s5_b_re=v_s5_b_re, v_s5_b_im=v_s5_b_im, v_s5_c_re=v_s5_c_re, v_s5_c_im=v_s5_c_im, v_s5_d=v_s5_d, v_s5_w_glu=v_s5_w_glu, v_s5_b_glu=v_s5_b_glu, v_gdn_conv_w=v_gdn_conv_w, v_gdn_a_log=v_gdn_a_log, v_gdn_dt_bias=v_gdn_dt_bias, v_gdn_out_norm_g=v_gdn_out_norm_g, v_c_norm_g=v_c_norm_g, v_c_w_qkv=v_c_w_qkv, v_c_w_out=v_c_w_out, v_c_q_norm_g=v_c_q_norm_g, v_c_k_norm_g=v_c_k_norm_g, v_c_rel_bias=v_c_rel_bias, v_mem_norm_g=v_mem_norm_g, v_xa_norm_g=v_xa_norm_g, v_xa_w_q=v_xa_w_q, v_xa_w_kv=v_xa_w_kv, v_xa_w_out=v_xa_w_out, v_xa_q_norm_g=v_xa_q_norm_g, v_xa_k_norm_g=v_xa_k_norm_g, v_f_norm_g=v_f_norm_g, v_f_w_gate=v_f_w_gate, v_f_w_up=v_f_w_up, v_f_w_down=v_f_w_down)
    weights = {n: given[n] for n in TWIN_WEIGHTS}
    shared = {n: given[n] for n in SHARED_INPUTS}
    per_example = {n: given[n] for n in ['x', 'mem']}
    grad_fn = _jax.value_and_grad(_loss, argnums=(0, 1))

    def one_microbatch(ex, loss_target):
        ex = dict(ex)
        diff = ex.pop(TWIN_DIFF_INPUT)
        return grad_fn(weights, diff, {**shared, **ex}, loss_target)

    if N_MICROBATCH == 1:
        loss, (grad_w, grad_x) = one_microbatch(per_example, given["loss_target"])
    else:
        def body(carry, xs):
            loss_sum, grad_sum = carry
            l_k, (gw_k, gx_k) = one_microbatch(xs[0], xs[1])
            with _jax.named_scope("update"):
                return (loss_sum + l_k, _jax.tree.map(_jnp.add, grad_sum, gw_k)), gx_k

        init = (_jnp.zeros((), _jnp.float32), _jax.tree.map(_jnp.zeros_like, weights))
        (loss, grad_w), grad_x = _jax.lax.scan(body, init, (per_example, given["loss_target"]))
    with _jax.named_scope("update"):
        delta_w, new_m, new_v = {}, {}, {}
        for n in TWIN_WEIGHTS:
            delta_w[n], new_m[n], new_v[n] = _adamw(weights[n], grad_w[n], given["m_" + n], given["v_" + n])
    return (loss, grad_x, *[grad_w[n] for n in TWIN_WEIGHTS], *[delta_w[n] for n in TWIN_WEIGHTS],
            *[new_m[n] for n in TWIN_WEIGHTS], *[new_v[n] for n in TWIN_WEIGHTS])
```

```python
import functools
import math

import jax
import jax.numpy as jnp
import numpy as np
from jax import lax
from jax.experimental import pallas as pl
from jax.experimental.pallas import tpu as pltpu

F32 = jnp.float32
BF16 = jnp.bfloat16
HI = lax.Precision.HIGHEST

N_DEV = 8
D_MODEL = 1024
SEQ = 2048
DEPTH = 4
CHUNK = 64
N_MEM = 256
RMS_EPS = 1e-6
S5_WIDTH = 512
S5_GROUP = 16
S5_GROUPS = 32
S5_STATE = 64
GDN_HEAD_DIM = 128
GDN_WIDTH = 512
GDN_HEADS = 4
GDN_CONV = 4
AB_IN = S5_WIDTH + 4 * GDN_WIDTH + 2 * GDN_HEADS
AB_IN_PAD = 2688
CA_HEADS = 16
CA_HEAD_DIM = 64
CA_LEFT = 8
CA_BAND = (CA_LEFT + 1) * CHUNK
CA_PAD = CA_LEFT * CHUNK
MAX_REL = 128
XA_HEADS = 4
XA_HEAD_DIM = 256
FFN = 2816
ADAM_LR, ADAM_B1, ADAM_B2, ADAM_EPS, ADAM_WD, ADAM_STEP = 0.001, 0.9, 0.999, 1e-08, 0.01, 10

VMEM_LIMIT = 48 * 1024 * 1024
LANE = 128
SUBLANE = 8


def _cparams(sem=None):
    return pltpu.CompilerParams(dimension_semantics=sem, vmem_limit_bytes=VMEM_LIMIT)


def _divisor_tile(n, target, unit=LANE):
    if n <= target:
        return n
    best = None
    for t in range(unit, target + 1, unit):
        if n % t == 0:
            best = t
    assert best is not None, (n, target)
    return best


def _matmul(a, b, *, ta=False, tb=False, out_dtype=F32, name="mm"):
    if ta:
        k_dim, m_dim = a.shape
    else:
        m_dim, k_dim = a.shape
    if tb:
        n_dim, kb = b.shape
    else:
        kb, n_dim = b.shape
    assert kb == k_dim, (a.shape, b.shape, ta, tb)
    tm = _divisor_tile(m_dim, 512)
    tn = _divisor_tile(n_dim, 512)
    tk = _divisor_tile(k_dim, 1408)
    nk = k_dim // tk
    dims = (((0 if ta else 1,), (1 if tb else 0,)), ((), ()))

    def body(a_ref, b_ref, o_ref, acc_ref):
        k = pl.program_id(2)

        @pl.when(k == 0)
        def _():
            acc_ref[...] = jnp.zeros_like(acc_ref)

        acc_ref[...] += lax.dot_general(a_ref[...].astype(BF16), b_ref[...].astype(BF16), dims,
                                        preferred_element_type=F32)

        @pl.when(k == nk - 1)
        def _():
            o_ref[...] = acc_ref[...].astype(o_ref.dtype)

    a_spec = pl.BlockSpec((tk, tm), lambda i, j, k: (k, i)) if ta else pl.BlockSpec((tm, tk), lambda i, j, k: (i, k))
    b_spec = pl.BlockSpec((tn, tk), lambda i, j, k: (j, k)) if tb else pl.BlockSpec((tk, tn), lambda i, j, k: (k, j))
    return pl.pallas_call(
        body,
        grid=(m_dim // tm, n_dim // tn, nk),
        in_specs=[a_spec, b_spec],
        out_specs=pl.BlockSpec((tm, tn), lambda i, j, k: (i, j)),
        out_shape=jax.ShapeDtypeStruct((m_dim, n_dim), out_dtype),
        scratch_shapes=[pltpu.VMEM((tm, tn), F32)],
        compiler_params=_cparams(("parallel", "parallel", "arbitrary")),
        name=name,
    )(a, b)


@jax.custom_vjp
def linear(a, w):
    return _matmul(a, w, name="linear_fwd")


def _linear_fwd(a, w):
    return _matmul(a, w, name="linear_fwd"), (a, w)


def _linear_bwd(res, dy):
    a, w = res
    da = _matmul(dy, w, tb=True, name="linear_da")
    dw = _matmul(a, dy, ta=True, out_dtype=w.dtype, name="linear_dw")
    return da, dw


linear.defvjp(_linear_fwd, _linear_bwd)


def make_rowop(fn, name, tm=256):
    def specs(rows, params):
        row_specs = [pl.BlockSpec((tm, r.shape[1]), lambda i: (i, 0)) for r in rows]
        par_specs = [pl.BlockSpec(p.shape, lambda i: (0, 0)) for p in params]
        return row_specs, par_specs

    def out_structs(rows, params):
        tiles = [jax.ShapeDtypeStruct((tm, r.shape[1]), r.dtype) for r in rows]
        return jax.eval_shape(lambda r, p: fn(*r, *p), tiles, list(params))

    def fwd_call(rows, params):
        m_dim = rows[0].shape[0]
        n_in = len(rows) + len(params)
        outs = out_structs(rows, params)

        def body(*refs):
            res = fn(*[r[...] for r in refs[:n_in]])
            for o_ref, r in zip(refs[n_in:], res):
                o_ref[...] = r.astype(o_ref.dtype)

        row_specs, par_specs = specs(rows, params)
        return pl.pallas_call(
            body,
            grid=(m_dim // tm,),
            in_specs=row_specs + par_specs,
            out_specs=[pl.BlockSpec((tm, o.shape[1]), lambda i: (i, 0)) for o in outs],
            out_shape=[jax.ShapeDtypeStruct((m_dim, o.shape[1]), o.dtype) for o in outs],
            compiler_params=_cparams(("parallel",)),
            name=name + "_fwd",
        )(*rows, *params)

    def bwd_call(rows, params, cts):
        m_dim = rows[0].shape[0]
        n_rows, n_par = len(rows), len(params)
        n_in = n_rows + n_par
        n_ct = len(cts)

        def body(*refs):
            vals = [r[...] for r in refs[:n_in]]
            ct_vals = tuple(r[...] for r in refs[n_in:n_in + n_ct])
            drow_refs = refs[n_in + n_ct:n_in + n_ct + n_rows]
            dpar_refs = refs[n_in + n_ct + n_rows:]
            _, pullback = jax.vjp(fn, *vals)
            grads = pullback(ct_vals)
            for d_ref, g in zip(drow_refs, grads[:n_rows]):
                d_ref[...] = g

            @pl.when(pl.program_id(0) == 0)
            def _():
                for d_ref in dpar_refs:
                    d_ref[...] = jnp.zeros_like(d_ref)

            for d_ref, g in zip(dpar_refs, grads[n_rows:]):
                d_ref[...] += g

        row_specs, par_specs = specs(rows, params)
        ct_specs = [pl.BlockSpec((tm, c.shape[1]), lambda i: (i, 0)) for c in cts]
        res = pl.pallas_call(
            body,
            grid=(m_dim // tm,),
            in_specs=row_specs + par_specs + ct_specs,
            out_specs=row_specs + par_specs,
            out_shape=[jax.ShapeDtypeStruct(r.shape, r.dtype) for r in rows]
            + [jax.ShapeDtypeStruct(p.shape, p.dtype) for p in params],
            compiler_params=_cparams(("arbitrary",)),
            name=name + "_bwd",
        )(*rows, *params, *cts)
        return tuple(res[:n_rows]), tuple(res[n_rows:])

    @jax.custom_vjp
    def op(rows, params):
        return tuple(fwd_call(rows, params))

    def op_fwd(rows, params):
        return tuple(fwd_call(rows, params)), (rows, params)

    def op_bwd(res, cts):
        rows, params = res
        return bwd_call(rows, params, tuple(cts))

    op.defvjp(op_fwd, op_bwd)
    return op


def _rms(x, g):
    return x * lax.rsqrt(jnp.mean(x * x, axis=-1, keepdims=True) + RMS_EPS) * g


def _sigmoid(x):
    return 1.0 / (1.0 + jnp.exp(-x))


def _silu(x):
    return x * _sigmoid(x)


def _bdot(a, b, dims=(((1,), (0,)), ((), ()))):
    return lax.dot_general(a.astype(BF16), b.astype(BF16), dims, preferred_element_type=F32)


def _rmsnorm_fn(x, g):
    return (_rms(x, g),)


def rmsnorm(x, g, name):
    return make_rowop(_rmsnorm_fn, name)((x,), (g.reshape(1, -1),))[0]


def _gelu_tanh(x):
    return 0.5 * x * (1.0 + jnp.tanh(0.7978845608028654 * (x + 0.044715 * x * x * x)))


def _softplus(x):
    return jnp.maximum(x, 0.0) + jnp.log(1.0 + jnp.exp(-jnp.abs(x)))


def _s5_post_fn(y, w_glu, b_glu):
    h = _gelu_tanh(y)
    return (h * _sigmoid(_bdot(h, w_glu) + b_glu),)


def _swiglu_fn(g, u):
    return (_silu(g) * u,)


def _loss_fn(y, t):
    err = y - t
    return (0.5 * jnp.mean(err * err, axis=-1, keepdims=True),)


def _pair_headnorm(x, g2):
    lo = lax.broadcasted_iota(jnp.int32, x.shape, 1) < CA_HEAD_DIM
    sq = x * x
    s_lo = jnp.sum(jnp.where(lo, sq, 0.0), axis=-1, keepdims=True)
    s_hi = jnp.sum(jnp.where(lo, 0.0, sq), axis=-1, keepdims=True)
    ms = jnp.where(lo, s_lo, s_hi) * (1.0 / CA_HEAD_DIM)
    return x * lax.rsqrt(ms + RMS_EPS) * g2


def _ca_qknorm_fn(qkv, qg2, kg2):
    qs, ks = [], []
    for j in range(D_MODEL // LANE):
        qs.append(_pair_headnorm(qkv[:, j * LANE:(j + 1) * LANE], qg2))
        ks.append(_pair_headnorm(qkv[:, D_MODEL + j * LANE:D_MODEL + (j + 1) * LANE], kg2))
    return jnp.concatenate(qs, axis=1), jnp.concatenate(ks, axis=1)


def _xattn_fn(q, k, v, qg, kg):
    outs = []
    for h in range(XA_HEADS):
        sl = slice(h * XA_HEAD_DIM, (h + 1) * XA_HEAD_DIM)
        qh = _rms(q[:, sl], qg)
        kh = _rms(k[:, sl], kg)
        s = _bdot(qh, kh, (((1,), (1,)), ((), ()))) * (XA_HEAD_DIM ** -0.5)
        p = jnp.exp(s - jnp.max(s, axis=-1, keepdims=True))
        p = p / jnp.sum(p, axis=-1, keepdims=True)
        outs.append(_bdot(p, v[:, sl]))
    return (jnp.concatenate(outs, axis=1),)


def _gdn_prep_fn(x0, x1, x2, x3, ab, conv_w, alog, dtb):
    c = conv_w[3:4, :] * x0 + conv_w[2:3, :] * x1 + conv_w[1:2, :] * x2 + conv_w[0:1, :] * x3
    c = _silu(c)
    qs, ks = [], []
    for h in range(GDN_HEADS):
        qh = c[:, h * LANE:(h + 1) * LANE]
        kh = c[:, GDN_WIDTH + h * LANE:GDN_WIDTH + (h + 1) * LANE]
        qs.append(qh * lax.rsqrt(jnp.sum(qh * qh, axis=-1, keepdims=True) + RMS_EPS) * (GDN_HEAD_DIM ** -0.5))
        ks.append(kh * lax.rsqrt(jnp.sum(kh * kh, axis=-1, keepdims=True) + RMS_EPS))
    lane = lax.broadcasted_iota(jnp.int32, ab.shape, 1)
    g = -jnp.exp(alog) * _softplus(ab + dtb)
    beta = _sigmoid(ab)
    bg = jnp.where(lane < GDN_HEADS, g, jnp.where(lane < 2 * GDN_HEADS, beta, 0.0))
    return jnp.concatenate(qs, axis=1), jnp.concatenate(ks, axis=1), c[:, 2 * GDN_WIDTH:], bg


def _gdn_out_fn(o, gate, og):
    outs = []
    for h in range(GDN_HEADS):
        sl = slice(h * LANE, (h + 1) * LANE)
        outs.append(_rms(o[:, sl], og) * _silu(gate[:, sl]))
    return (jnp.concatenate(outs, axis=1),)


def _ca_math(q2, kb2, vb2, bias2, c):
    lane = lax.broadcasted_iota(jnp.int32, q2.shape, 1)
    kpos = lax.broadcasted_iota(jnp.int32, (CHUNK, CA_BAND), 1) + c * CHUNK
    valid = kpos >= CA_PAD
    out = jnp.zeros(q2.shape, F32)
    for h in range(2):
        mine = (lane >= h * CA_HEAD_DIM) & (lane < (h + 1) * CA_HEAD_DIM)
        qh = jnp.where(mine, q2, 0.0)
        s = _bdot(qh, kb2, (((1,), (1,)), ((), ()))) * (CA_HEAD_DIM ** -0.5) + bias2[h]
        s = jnp.where(valid, s, -1e30)
        p = jnp.exp(s - jnp.max(s, axis=-1, keepdims=True))
        p = p / jnp.sum(p, axis=-1, keepdims=True)
        out = out + jnp.where(mine, _bdot(p, vb2), 0.0)
    return out


def _ca_specs(seq):
    n_chunks = seq // CHUNK
    q_spec = pl.BlockSpec((CHUNK, LANE), lambda hp, c: (c, hp))
    kv_spec = pl.BlockSpec((seq + CA_PAD, LANE), lambda hp, c: (0, hp))
    b_spec = pl.BlockSpec((2, CHUNK, CA_BAND), lambda hp, c: (hp, 0, 0))
    return (D_MODEL // LANE, n_chunks), q_spec, kv_spec, b_spec


def _ca_fwd_call(q, kpad, vpad, bias):
    grid, q_spec, kv_spec, b_spec = _ca_specs(q.shape[0])

    def body(q_ref, k_ref, v_ref, b_ref, o_ref):
        c = pl.program_id(1)
        start = pl.multiple_of(c * CHUNK, CHUNK)
        o_ref[...] = _ca_math(q_ref[...], k_ref[pl.ds(start, CA_BAND), :], v_ref[pl.ds(start, CA_BAND), :],
                              b_ref[...], c)

    return pl.pallas_call(
        body, grid=grid, in_specs=[q_spec, kv_spec, kv_spec, b_spec], out_specs=q_spec,
        out_shape=jax.ShapeDtypeStruct(q.shape, F32),
        compiler_params=_cparams(("parallel", "arbitrary")), name="chunkattn_fwd",
    )(q, kpad, vpad, bias)


def _ca_bwd_call(q, kpad, vpad, bias, do):
    grid, q_spec, kv_spec, b_spec = _ca_specs(q.shape[0])

    def body(q_ref, k_ref, v_ref, b_ref, do_ref, dq_ref, dk_ref, dv_ref, db_ref):
        c = pl.program_id(1)
        start = pl.multiple_of(c * CHUNK, CHUNK)

        @pl.when(c == 0)
        def _():
            dk_ref[...] = jnp.zeros_like(dk_ref)
            dv_ref[...] = jnp.zeros_like(dv_ref)
            db_ref[...] = jnp.zeros_like(db_ref)

        _, pullback = jax.vjp(lambda a, b, d, e: _ca_math(a, b, d, e, c), q_ref[...],
                              k_ref[pl.ds(start, CA_BAND), :], v_ref[pl.ds(start, CA_BAND), :], b_ref[...])
        dq, dkb, dvb, dbias = pullback(do_ref[...])
        dq_ref[...] = dq
        dk_ref[pl.ds(start, CA_BAND), :] += dkb
        dv_ref[pl.ds(start, CA_BAND), :] += dvb
        db_ref[...] += dbias

    return pl.pallas_call(
        body, grid=grid, in_specs=[q_spec, kv_spec, kv_spec, b_spec, q_spec],
        out_specs=[q_spec, kv_spec, kv_spec, b_spec],
        out_shape=[jax.ShapeDtypeStruct(q.shape, F32), jax.ShapeDtypeStruct(kpad.shape, F32),
                   jax.ShapeDtypeStruct(vpad.shape, F32), jax.ShapeDtypeStruct(bias.shape, F32)],
        compiler_params=_cparams(("parallel", "arbitrary")), name="chunkattn_bwd",
    )(q, kpad, vpad, bias, do)


@jax.custom_vjp
def chunk_attn_core(q, kpad, vpad, bias):
    return _ca_fwd_call(q, kpad, vpad, bias)


def _ca_core_fwd(q, kpad, vpad, bias):
    return _ca_fwd_call(q, kpad, vpad, bias), (q, kpad, vpad, bias)


def _ca_core_bwd(res, do):
    return tuple(_ca_bwd_call(*res, do))


chunk_attn_core.defvjp(_ca_core_fwd, _ca_core_bwd)


S5_GB = 4
S5_U = S5_WIDTH // S5_GB
S5_L = S5_GROUPS * S5_STATE // S5_GB


def _cmul(ar, ai, br, bi):
    return ar * br - ai * bi, ar * bi + ai * br


def _hdot(a, b, dims=(((1,), (0,)), ((), ()))):
    return lax.dot_general(a, b, dims, precision=HI, preferred_element_type=F32)


_NT = (((1,), (1,)), ((), ()))
_TN = (((0,), (0,)), ((), ()))


def _s5_tables(lr, li, reverse):
    p = {1: (lr, li)}
    p[2] = _cmul(*p[1], *p[1])
    p[4] = _cmul(*p[2], *p[2])
    p[3] = _cmul(*p[2], *p[1])
    p[5] = _cmul(*p[4], *p[1])
    p[6] = _cmul(*p[4], *p[2])
    p[7] = _cmul(*p[4], *p[3])
    p[8] = _cmul(*p[4], *p[4])
    row = lax.broadcasted_iota(jnp.int32, (SUBLANE, lr.shape[1]), 0)
    tr = jnp.zeros(row.shape, F32)
    ti = jnp.zeros(row.shape, F32)
    for i in range(SUBLANE):
        k = SUBLANE - i if reverse else i + 1
        tr = jnp.where(row == i, p[k][0], tr)
        ti = jnp.where(row == i, p[k][1], ti)
    return p, (tr, ti), row


def _s5_block_scan(xr, xi, p, tab, row, hr, hi, reverse):
    for k in (1, 2, 4):
        if reverse:
            sr = jnp.where(row < SUBLANE - k, pltpu.roll(xr, SUBLANE - k, 0), 0.0)
            si = jnp.where(row < SUBLANE - k, pltpu.roll(xi, SUBLANE - k, 0), 0.0)
        else:
            sr = jnp.where(row >= k, pltpu.roll(xr, k, 0), 0.0)
            si = jnp.where(row >= k, pltpu.roll(xi, k, 0), 0.0)
        ar, ai = _cmul(p[k][0], p[k][1], sr, si)
        xr, xi = xr + ar, xi + ai
    cr, ci = _cmul(tab[0], tab[1], hr, hi)
    return xr + cr, xi + ci


def _s5_forward_scan(sr_ref, si_ref, lr, li):
    n_blocks = sr_ref.shape[0] // SUBLANE
    p, tab, row = _s5_tables(lr, li, False)

    def step(b, carry):
        base = pl.multiple_of(b * SUBLANE, SUBLANE)
        xr, xi = _s5_block_scan(sr_ref[pl.ds(base, SUBLANE), :], si_ref[pl.ds(base, SUBLANE), :],
                                p, tab, row, carry[0], carry[1], False)
        sr_ref[pl.ds(base, SUBLANE), :] = xr
        si_ref[pl.ds(base, SUBLANE), :] = xi
        return xr[SUBLANE - 1:SUBLANE, :], xi[SUBLANE - 1:SUBLANE, :]

    zero = jnp.zeros((1, lr.shape[1]), F32)
    lax.fori_loop(0, n_blocks, step, (zero, zero))


def _s5_specs(seq):
    u_spec = pl.BlockSpec((seq, S5_U), lambda g: (0, g))
    bd_spec = pl.BlockSpec((1, S5_U, S5_L), lambda g: (g, 0, 0))
    cd_spec = pl.BlockSpec((1, S5_L, S5_U), lambda g: (g, 0, 0))
    lam_spec = pl.BlockSpec((1, 2, S5_L), lambda g: (g, 0, 0))
    d_spec = pl.BlockSpec((1, S5_U), lambda g: (0, g))
    return u_spec, bd_spec, cd_spec, lam_spec, d_spec


S5_ROWS = 256


def _row_chunks(seq, fn):
    rows_per = min(S5_ROWS, seq)

    def step(r, carry):
        fn(pl.ds(pl.multiple_of(r * rows_per, rows_per), rows_per))
        return carry

    lax.fori_loop(0, seq // rows_per, step, 0)


def _s5_fwd_call(u, bdr, bdi, cdr, cdi, lam, d):
    seq = u.shape[0]
    u_spec, bd_spec, cd_spec, lam_spec, d_spec = _s5_specs(seq)

    def body(u_ref, bdr_ref, bdi_ref, cdr_ref, cdi_ref, lam_ref, d_ref, y_ref, sr_ref, si_ref):
        def project_in(rows):
            uv = u_ref[rows, :]
            sr_ref[rows, :] = _hdot(uv, bdr_ref[0])
            si_ref[rows, :] = _hdot(uv, bdi_ref[0])

        def project_out(rows):
            y_ref[rows, :] = (_hdot(sr_ref[rows, :], cdr_ref[0]) - _hdot(si_ref[rows, :], cdi_ref[0])
                              + d_ref[...] * u_ref[rows, :])

        _row_chunks(seq, project_in)
        _s5_forward_scan(sr_ref, si_ref, lam_ref[0, 0:1, :], lam_ref[0, 1:2, :])
        _row_chunks(seq, project_out)

    return pl.pallas_call(
        body, grid=(S5_GB,), in_specs=[u_spec, bd_spec, bd_spec, cd_spec, cd_spec, lam_spec, d_spec],
        out_specs=u_spec, out_shape=jax.ShapeDtypeStruct(u.shape, F32),
        scratch_shapes=[pltpu.VMEM((seq, S5_L), F32), pltpu.VMEM((seq, S5_L), F32)],
        compiler_params=_cparams(("parallel",)), name="s5_fwd",
    )(u, bdr, bdi, cdr, cdi, lam, d)


def _s5_bwd_call(u, bdr, bdi, cdr, cdi, lam, d, dy):
    seq = u.shape[0]
    n_blocks = seq // SUBLANE
    u_spec, bd_spec, cd_spec, lam_spec, d_spec = _s5_specs(seq)

    def body(u_ref, bdr_ref, bdi_ref, cdr_ref, cdi_ref, lam_ref, d_ref, dy_ref,
             du_ref, dbdr_ref, dbdi_ref, dcdr_ref, dcdi_ref, dlam_ref, dd_ref, sr_ref, si_ref, gr_ref, gi_ref):
        lr, li = lam_ref[0, 0:1, :], lam_ref[0, 1:2, :]

        def project_in(rows):
            uv = u_ref[rows, :]
            dyv = dy_ref[rows, :]
            sr_ref[rows, :] = _hdot(uv, bdr_ref[0])
            si_ref[rows, :] = _hdot(uv, bdi_ref[0])
            gr_ref[rows, :] = _hdot(dyv, cdr_ref[0], _NT)
            gi_ref[rows, :] = -_hdot(dyv, cdi_ref[0], _NT)

        _row_chunks(seq, project_in)
        _s5_forward_scan(sr_ref, si_ref, lr, li)
        p, tab, row = _s5_tables(lr, -li, True)

        def step(i, carry):
            hr, hi, acc_r, acc_i = carry
            b = n_blocks - 1 - i
            base = pl.multiple_of(b * SUBLANE, SUBLANE)
            xr, xi = _s5_block_scan(gr_ref[pl.ds(base, SUBLANE), :], gi_ref[pl.ds(base, SUBLANE), :],
                                    p, tab, row, hr, hi, True)
            gr_ref[pl.ds(base, SUBLANE), :] = xr
            gi_ref[pl.ds(base, SUBLANE), :] = xi
            prev = pl.multiple_of(jnp.maximum(b - 1, 0) * SUBLANE, SUBLANE)
            keep = (b > 0).astype(F32)
            last_r = sr_ref[pl.ds(prev, SUBLANE), :][SUBLANE - 1:SUBLANE, :] * keep
            last_i = si_ref[pl.ds(prev, SUBLANE), :][SUBLANE - 1:SUBLANE, :] * keep
            pr = jnp.where(row >= 1, pltpu.roll(sr_ref[pl.ds(base, SUBLANE), :], 1, 0), last_r)
            pi = jnp.where(row >= 1, pltpu.roll(si_ref[pl.ds(base, SUBLANE), :], 1, 0), last_i)
            acc_r = acc_r + pr * xr + pi * xi
            acc_i = acc_i + pr * xi - pi * xr
            return xr[0:1, :], xi[0:1, :], acc_r, acc_i

        zero = jnp.zeros((1, S5_L), F32)
        zacc = jnp.zeros((SUBLANE, S5_L), F32)
        _, _, acc_r, acc_i = lax.fori_loop(0, n_blocks, step, (zero, zero, zacc, zacc))
        dlam_ref[0, 0:1, :] = jnp.sum(acc_r, axis=0, keepdims=True)
        dlam_ref[0, 1:2, :] = jnp.sum(acc_i, axis=0, keepdims=True)
        for ref in (dbdr_ref, dbdi_ref, dcdr_ref, dcdi_ref, dd_ref):
            ref[...] = jnp.zeros_like(ref)

        def grads(rows):
            uv, dyv = u_ref[rows, :], dy_ref[rows, :]
            grv, giv = gr_ref[rows, :], gi_ref[rows, :]
            du_ref[rows, :] = _hdot(grv, bdr_ref[0], _NT) + _hdot(giv, bdi_ref[0], _NT) + d_ref[...] * dyv
            dbdr_ref[0] += _hdot(uv, grv, _TN)
            dbdi_ref[0] += _hdot(uv, giv, _TN)
            dcdr_ref[0] += _hdot(sr_ref[rows, :], dyv, _TN)
            dcdi_ref[0] -= _hdot(si_ref[rows, :], dyv, _TN)
            dd_ref[...] += jnp.sum(dyv * uv, axis=0, keepdims=True)

        _row_chunks(seq, grads)

    scratch = [pltpu.VMEM((seq, S5_L), F32) for _ in range(4)]
    return pl.pallas_call(
        body, grid=(S5_GB,),
        in_specs=[u_spec, bd_spec, bd_spec, cd_spec, cd_spec, lam_spec, d_spec, u_spec],
        out_specs=[u_spec, bd_spec, bd_spec, cd_spec, cd_spec, lam_spec, d_spec],
        out_shape=[jax.ShapeDtypeStruct(a.shape, F32) for a in (u, bdr, bdi, cdr, cdi, lam, d)],
        scratch_shapes=scratch, compiler_params=_cparams(("parallel",)), name="s5_bwd",
    )(u, bdr, bdi, cdr, cdi, lam, d, dy)


@jax.custom_vjp
def s5_core(u, bdr, bdi, cdr, cdi, lam, d):
    return _s5_fwd_call(u, bdr, bdi, cdr, cdi, lam, d)


def _s5_core_fwd(*args):
    return _s5_fwd_call(*args), args


def _s5_core_bwd(res, dy):
    return tuple(_s5_bwd_call(*res, dy))


s5_core.defvjp(_s5_core_fwd, _s5_core_bwd)


def _s5_discretize(a_re, a_im, log_dt, b_re, b_im, c_re, c_im, d):
    dt = jnp.exp(log_dt)[:, None]
    mag = jnp.exp(a_re * dt)
    lbr, lbi = mag * jnp.cos(a_im * dt), mag * jnp.sin(a_im * dt)
    den = a_re * a_re + a_im * a_im
    fr = ((lbr - 1.0) * a_re + lbi * a_im) / den
    fi = (lbi * a_re - (lbr - 1.0) * a_im) / den
    bbr = fr[:, :, None] * b_re - fi[:, :, None] * b_im
    bbi = fr[:, :, None] * b_im + fi[:, :, None] * b_re
    eye = jnp.eye(S5_GROUPS // S5_GB, dtype=F32)
    gl = S5_GROUPS // S5_GB

    def bd(t):
        return jnp.einsum('bgpc,gh->bgchp', t.reshape(S5_GB, gl, S5_STATE, S5_GROUP), eye).reshape(S5_GB, S5_U, S5_L)

    def cd(t):
        return jnp.einsum('bgcp,gh->bgphc', t.reshape(S5_GB, gl, S5_GROUP, S5_STATE), eye).reshape(S5_GB, S5_L, S5_U)

    lam = jnp.stack([lbr.reshape(S5_GB, S5_L), lbi.reshape(S5_GB, S5_L)], axis=1)
    return bd(bbr), bd(bbi), cd(c_re), cd(c_im), lam, d.reshape(1, S5_WIDTH)


def _gdn_chunk(q, k, v, g_col, b_col, st):
    r = lax.broadcasted_iota(jnp.int32, (CHUNK, CHUNK), 0)
    c = lax.broadcasted_iota(jnp.int32, (CHUNK, CHUNK), 1)
    eye = (r == c).astype(F32)
    strict = r > c
    causal = r >= c
    g_row = jnp.sum(g_col * eye, axis=0, keepdims=True)
    gcum = jnp.sum(jnp.where(causal, g_row, 0.0), axis=1, keepdims=True)
    gcum_row = jnp.sum(gcum * eye, axis=0, keepdims=True)
    diff = gcum - gcum_row
    decay_strict = jnp.where(strict, jnp.exp(jnp.where(strict, diff, 0.0)), 0.0)
    decay_causal = jnp.where(causal, jnp.exp(jnp.where(causal, diff, 0.0)), 0.0)
    gamma = jnp.exp(gcum)
    g_last = jnp.sum(jnp.where(lax.broadcasted_iota(jnp.int32, (CHUNK, 1), 0) == CHUNK - 1, gcum, 0.0),
                     axis=0, keepdims=True)
    kk = _hdot(k, k, _NT)
    neg_a = -(b_col * kk * decay_strict)
    x = jnp.concatenate([b_col * v, (b_col * gamma) * k], axis=1)
    npow = neg_a
    for i in range(6):
        x = x + _hdot(npow, x)
        if i < 5:
            npow = _hdot(npow, npow)
    u_new, w_k = x[:, :GDN_HEAD_DIM], x[:, GDN_HEAD_DIM:]
    qk = _hdot(q, k, _NT) * decay_causal
    q_g = q * gamma
    k_tail = k * jnp.exp(g_last - gcum)
    w = u_new - _hdot(w_k, st)
    o = _hdot(q_g, st) + _hdot(qk, w)
    st_new = jnp.exp(g_last) * st + _hdot(k_tail, w, _TN)
    return o, st_new


def _gdn_cols(bgv, h):
    lane = lax.broadcasted_iota(jnp.int32, bgv.shape, 1)
    g_col = jnp.sum(jnp.where(lane == h, bgv, 0.0), axis=1, keepdims=True)
    b_col = jnp.sum(jnp.where(lane == GDN_HEADS + h, bgv, 0.0), axis=1, keepdims=True)
    return g_col, b_col


def _gdn_fwd_call(q, k, v, bg):
    seq = q.shape[0]
    n_chunks = seq // CHUNK
    x_spec = pl.BlockSpec((CHUNK, GDN_WIDTH), lambda n: (n, 0))
    bg_spec = pl.BlockSpec((CHUNK, LANE), lambda n: (n, 0))
    st_spec = pl.BlockSpec((1, GDN_WIDTH, GDN_HEAD_DIM), lambda n: (n, 0, 0))

    def body(q_ref, k_ref, v_ref, bg_ref, o_ref, st_out_ref, st_ref):
        @pl.when(pl.program_id(0) == 0)
        def _():
            st_ref[...] = jnp.zeros_like(st_ref)

        bgv = bg_ref[...]
        for h in range(GDN_HEADS):
            sl = slice(h * GDN_HEAD_DIM, (h + 1) * GDN_HEAD_DIM)
            g_col, b_col = _gdn_cols(bgv, h)
            st = st_ref[sl, :]
            st_out_ref[0, sl, :] = st
            o, st_new = _gdn_chunk(q_ref[:, sl], k_ref[:, sl], v_ref[:, sl], g_col, b_col, st)
            o_ref[:, sl] = o
            st_ref[sl, :] = st_new

    return pl.pallas_call(
        body, grid=(n_chunks,), in_specs=[x_spec, x_spec, x_spec, bg_spec], out_specs=[x_spec, st_spec],
        out_shape=[jax.ShapeDtypeStruct(q.shape, F32),
                   jax.ShapeDtypeStruct((n_chunks, GDN_WIDTH, GDN_HEAD_DIM), F32)],
        scratch_shapes=[pltpu.VMEM((GDN_WIDTH, GDN_HEAD_DIM), F32)],
        compiler_params=_cparams(("arbitrary",)), name="gdn_fwd",
    )(q, k, v, bg)


def _gdn_bwd_call(q, k, v, bg, states, do):
    seq = q.shape[0]
    n_chunks = seq // CHUNK
    x_spec = pl.BlockSpec((CHUNK, GDN_WIDTH), lambda i: (n_chunks - 1 - i, 0))
    bg_spec = pl.BlockSpec((CHUNK, LANE), lambda i: (n_chunks - 1 - i, 0))
    st_spec = pl.BlockSpec((1, GDN_WIDTH, GDN_HEAD_DIM), lambda i: (n_chunks - 1 - i, 0, 0))

    def body(q_ref, k_ref, v_ref, bg_ref, st_in_ref, do_ref, dq_ref, dk_ref, dv_ref, dbg_ref, dst_ref):
        @pl.when(pl.program_id(0) == 0)
        def _():
            dst_ref[...] = jnp.zeros_like(dst_ref)

        bgv = bg_ref[...]
        lane = lax.broadcasted_iota(jnp.int32, bgv.shape, 1)
        dbg = jnp.zeros(bgv.shape, F32)
        for h in range(GDN_HEADS):
            sl = slice(h * GDN_HEAD_DIM, (h + 1) * GDN_HEAD_DIM)
            g_col, b_col = _gdn_cols(bgv, h)
            _, pullback = jax.vjp(_gdn_chunk, q_ref[:, sl], k_ref[:, sl], v_ref[:, sl], g_col, b_col,
                                  st_in_ref[0, sl, :])
            dq, dk, dv, dg, db, dst = pullback((do_ref[:, sl], dst_ref[sl, :]))
            dq_ref[:, sl] = dq
            dk_ref[:, sl] = dk
            dv_ref[:, sl] = dv
            dst_ref[sl, :] = dst
            dbg = dbg + jnp.where(lane == h, dg, 0.0) + jnp.where(lane == GDN_HEADS + h, db, 0.0)
        dbg_ref[...] = dbg

    return pl.pallas_call(
        body, grid=(n_chunks,), in_specs=[x_spec, x_spec, x_spec, bg_spec, st_spec, x_spec],
        out_specs=[x_spec, x_spec, x_spec, bg_spec],
        out_shape=[jax.ShapeDtypeStruct(q.shape, F32)] * 3 + [jax.ShapeDtypeStruct(bg.shape, F32)],
        scratch_shapes=[pltpu.VMEM((GDN_WIDTH, GDN_HEAD_DIM), F32)],
        compiler_params=_cparams(("arbitrary",)), name="gdn_bwd",
    )(q, k, v, bg, states, do)


@jax.custom_vjp
def gdn_core(q, k, v, bg):
    return _gdn_fwd_call(q, k, v, bg)[0]


def _gdn_core_fwd(q, k, v, bg):
    o, states = _gdn_fwd_call(q, k, v, bg)
    return o, (q, k, v, bg, states)


def _gdn_core_bwd(res, do):
    return tuple(_gdn_bwd_call(*res, do))


gdn_core.defvjp(_gdn_core_fwd, _gdn_core_bwd)


def _row(v):
    return v.reshape(1, -1)


def _lane_pad(v):
    return jnp.pad(v, (0, LANE - v.shape[0])).reshape(1, LANE)


def _delay_rows(x, k):
    return jnp.pad(x, ((k, 0), (0, 0)))[:x.shape[0]]


def s5_mixer(u, a_re, a_im, log_dt, b_re, b_im, c_re, c_im, d, w_glu, b_glu):
    y = s5_core(u, *_s5_discretize(a_re, a_im, log_dt, b_re, b_im, c_re, c_im, d))
    return make_rowop(_s5_post_fn, "s5_post")((y,), (w_glu, _row(b_glu)))[0]


def gated_deltanet(qkv, gate, ab, conv_w, a_log, dt_bias, out_g):
    rows = (qkv, _delay_rows(qkv, 1), _delay_rows(qkv, 2), _delay_rows(qkv, 3), ab)
    q, k, v, bg = make_rowop(_gdn_prep_fn, "gdn_prep")(rows, (conv_w, _lane_pad(a_log), _lane_pad(dt_bias)))
    o = gdn_core(q, k, v, bg)
    return make_rowop(_gdn_out_fn, "gdn_out")((o, gate), (_row(out_g),))[0]


def chunk_attention(h, w_qkv, w_out, q_g, k_g, rel_bias):
    qkv = linear(h, w_qkv)
    qn, kn = make_rowop(_ca_qknorm_fn, "ca_qknorm")((qkv,), (_row(jnp.tile(q_g, 2)), _row(jnp.tile(k_g, 2))))
    kpad = jnp.pad(kn, ((CA_PAD, 0), (0, 0)))
    vpad = jnp.pad(qkv[:, 2 * D_MODEL:], ((CA_PAD, 0), (0, 0)))
    o = chunk_attn_core(qn, kpad, vpad, _rel_bias_matrix(rel_bias))
    return linear(o, w_out)


def memory_cross_attention(h, mem_n, w_q, w_kv, w_out, q_g, k_g):
    q = linear(h, w_q)
    kv = linear(mem_n, w_kv)
    o = make_rowop(_xattn_fn, "xattn")((q,), (kv[:, :D_MODEL], kv[:, D_MODEL:], _row(q_g), _row(k_g)))[0]
    return linear(o, w_out)


def swiglu(h, w_gate, w_up, w_down):
    a = make_rowop(_swiglu_fn, "swiglu")((linear(h, w_gate), linear(h, w_up)), ())[0]
    return linear(a, w_down)


def _rel_bias_matrix(rel_bias):
    n_near = 2 * MAX_REL + 1 - (MAX_REL - CHUNK + 1)
    n_far = CHUNK - 1 + CA_BAND - n_near
    near = rel_bias[:, MAX_REL - CHUNK + 1:]
    far = jnp.broadcast_to(rel_bias[:, 2 * MAX_REL:], (rel_bias.shape[0], n_far))
    tvr = jnp.flip(jnp.concatenate([near, far], axis=1), axis=1)
    return jnp.stack([tvr[:, CHUNK - 1 - qi:CHUNK - 1 - qi + CA_BAND] for qi in range(CHUNK)], axis=1)


def _exchange(arrays, modes, name):
    n = len(arrays)
    out_shapes = [jax.ShapeDtypeStruct((N_DEV,) + a.shape if m == "gather" else a.shape, a.dtype)
                  for a, m in zip(arrays, modes)]

    def body(*refs):
        ins, outs = refs[:n], refs[n:2 * n]
        send_sems, recv_sems, local_sems = refs[2 * n:]
        x, y, c = lax.axis_index("x"), lax.axis_index("y"), lax.axis_index("c")
        me = 4 * x + 2 * y + c
        pending = []
        for i in range(n):
            gather = modes[i] == "gather"
            local = pltpu.make_async_copy(ins[i] if gather else ins[i].at[me], outs[i].at[me], local_sems.at[i])
            local.start()
            pending.append(local)
        for k in range(1, N_DEV):
            px, py, pc = (x + (k >> 2)) % 2, (y + ((k >> 1) & 1)) % 2, (c + (k & 1)) % 2
            peer = 4 * px + 2 * py + pc
            for i in range(n):
                src = ins[i] if modes[i] == "gather" else ins[i].at[peer]
                sem = i * (N_DEV - 1) + k - 1
                send = pltpu.make_async_remote_copy(src_ref=src, dst_ref=outs[i].at[me], send_sem=send_sems.at[sem],
                                                    recv_sem=recv_sems.at[sem], device_id=(px, py, pc),
                                                    device_id_type=pl.DeviceIdType.MESH)
                send.start()
                arrival = pltpu.make_async_remote_copy(src_ref=src, dst_ref=outs[i].at[peer],
                                                       send_sem=send_sems.at[sem], recv_sem=recv_sems.at[sem],
                                                       device_id=(px, py, pc), device_id_type=pl.DeviceIdType.MESH)
                pending.append((send, arrival))
        for item in pending:
            if isinstance(item, tuple):
                item[0].wait_send()
                item[1].wait_recv()
            else:
                item.wait()

    any_spec = pl.BlockSpec(memory_space=pl.ANY)
    return pl.pallas_call(
        body, in_specs=[any_spec] * n, out_specs=[any_spec] * n, out_shape=out_shapes,
        scratch_shapes=[pltpu.SemaphoreType.DMA((n * (N_DEV - 1),)), pltpu.SemaphoreType.DMA((n * (N_DEV - 1),)),
                        pltpu.SemaphoreType.DMA((n,))],
        name=name,
    )(*arrays)


def _adam_call(w, m, v, g_slots, name):
    rows, cols = w.shape
    tr = rows
    if rows * cols > 256 * 1024:
        tr = max(t for t in range(SUBLANE, rows, SUBLANE) if rows % t == 0 and t * cols <= 256 * 1024)
    c1 = 1.0 - ADAM_B1 ** ADAM_STEP
    c2 = 1.0 - ADAM_B2 ** ADAM_STEP

    def body(w_ref, m_ref, v_ref, g_ref, grad_ref, delta_ref, nm_ref, nv_ref):
        g = g_ref[0].astype(F32)
        for k in range(1, N_DEV):
            g = g + g_ref[k].astype(F32)
        m_new = ADAM_B1 * m_ref[...] + (1.0 - ADAM_B1) * g
        v_new = ADAM_B2 * v_ref[...] + (1.0 - ADAM_B2) * (g * g)
        m_hat = m_new / c1
        v_hat = v_new / c2
        grad_ref[...] = g
        delta_ref[...] = -ADAM_LR * (m_hat / (jnp.sqrt(v_hat) + ADAM_EPS) + ADAM_WD * w_ref[...])
        nm_ref[...] = m_new
        nv_ref[...] = v_new

    spec = pl.BlockSpec((tr, cols), lambda i: (i, 0))
    return pl.pallas_call(
        body, grid=(rows // tr,), in_specs=[spec, spec, spec, pl.BlockSpec((N_DEV, tr, cols), lambda i: (0, i, 0))],
        out_specs=[spec] * 4, out_shape=[jax.ShapeDtypeStruct(w.shape, F32)] * 4,
        compiler_params=_cparams(("parallel",)), name=name,
    )(w, m, v, g_slots)


WEIGHT_NAMES = ['ab_norm_g', 'ab_w_in', 'ab_w_out', 's5_a_re', 's5_a_im', 's5_log_dt', 's5_b_re', 's5_b_im', 's5_c_re',
                's5_c_im', 's5_d', 's5_w_glu', 's5_b_glu', 'gdn_conv_w', 'gdn_a_log', 'gdn_dt_bias', 'gdn_out_norm_g',
                'c_norm_g', 'c_w_qkv', 'c_w_out', 'c_q_norm_g', 'c_k_norm_g', 'c_rel_bias', 'mem_norm_g', 'xa_norm_g',
                'xa_w_q', 'xa_w_kv', 'xa_w_out', 'xa_q_norm_g', 'xa_k_norm_g', 'f_norm_g', 'f_w_gate', 'f_w_up',
                'f_w_down']

SHARDED = {
    'ab_w_in': ('col', BF16), 'ab_w_out': ('row', BF16), 's5_w_glu': ('row', BF16), 'gdn_conv_w': ('col', F32),
    'c_norm_g': ('col', F32), 'c_w_qkv': ('col', BF16), 'c_w_out': ('row', BF16), 'xa_w_q': ('row', BF16),
    'xa_w_kv': ('col', BF16), 'xa_w_out': ('row', BF16), 'f_w_gate': ('col', BF16), 'f_w_up': ('col', BF16),
    'f_w_down': ('row', BF16),
}
REPLICATED = [n for n in WEIGHT_NAMES if n not in SHARDED]
PACK_UNIT = SUBLANE * LANE


def _as3d(a):
    return a.reshape(a.shape[0], 1, a.shape[1]) if a.ndim == 2 else a


def _layers_from_gathered(g, axis):
    out = []
    for i in range(g.shape[1]):
        gi = g[:, i]
        if axis == "row":
            out.append(gi.reshape(gi.shape[0] * gi.shape[1], gi.shape[2]))
        else:
            out.append(jnp.transpose(gi, (1, 0, 2)).reshape(gi.shape[1], gi.shape[0] * gi.shape[2]))
    return out


def _chunks_from_layers(layers, axis):
    parts = []
    for full in layers:
        if axis == "row":
            parts.append(full.reshape(N_DEV, full.shape[0] // N_DEV, full.shape[1]))
        else:
            parts.append(jnp.transpose(full.reshape(full.shape[0], N_DEV, full.shape[1] // N_DEV), (1, 0, 2)))
    return jnp.stack(parts, axis=1)


def _pack(arrays):
    flat = []
    for a in arrays:
        size = a.size
        padded = -(-size // PACK_UNIT) * PACK_UNIT
        flat.append(jnp.pad(a.reshape(-1), (0, padded - size)).reshape(-1, LANE))
    return jnp.concatenate(flat, axis=0)


def _unpack(buf, shapes):
    out, row = [], 0
    for shape in shapes:
        size = math.prod(shape)
        rows = -(-size // PACK_UNIT) * SUBLANE
        out.append(buf[row:row + rows].reshape(-1)[:size].reshape(shape))
        row += rows
    return out


def _forward_loss(big, small, x, mem, target):
    mem_n = rmsnorm(mem, small['mem_norm_g'], "mem_norm")
    for layer in range(DEPTH):
        i = layer // 2
        if layer % 2 == 0:
            h = rmsnorm(x, small['ab_norm_g'][i], "ab_norm")
            proj = linear(h, big['ab_w_in'][i])
            u = proj[:, :S5_WIDTH]
            qkv = proj[:, S5_WIDTH:S5_WIDTH + 3 * GDN_WIDTH]
            gate = proj[:, S5_WIDTH + 3 * GDN_WIDTH:S5_WIDTH + 4 * GDN_WIDTH]
            ab = proj[:, S5_WIDTH + 4 * GDN_WIDTH:]
            a_out = s5_mixer(u, small['s5_a_re'][i], small['s5_a_im'][i], small['s5_log_dt'][i], small['s5_b_re'][i],
                             small['s5_b_im'][i], small['s5_c_re'][i], small['s5_c_im'][i], small['s5_d'][i],
                             big['s5_w_glu'][i], small['s5_b_glu'][i])
            b_out = gated_deltanet(qkv, gate, ab, big['gdn_conv_w'][i], small['gdn_a_log'][i],
                                   small['gdn_dt_bias'][i], small['gdn_out_norm_g'][i])
            x = x + linear(jnp.concatenate([a_out, b_out], axis=1), big['ab_w_out'][i])
        else:
            h = rmsnorm(x, big['c_norm_g'][i].reshape(-1), "c_norm")
            x = x + chunk_attention(h, big['c_w_qkv'][i], big['c_w_out'][i], small['c_q_norm_g'][i],
                                    small['c_k_norm_g'][i], small['c_rel_bias'][i])
        h = rmsnorm(x, small['xa_norm_g'][layer], "xa_norm")
        x = x + memory_cross_attention(h, mem_n, big['xa_w_q'][layer], big['xa_w_kv'][layer], big['xa_w_out'][layer],
                                       small['xa_q_norm_g'][layer], small['xa_k_norm_g'][layer])
        h = rmsnorm(x, small['f_norm_g'][layer], "f_norm")
        x = x + swiglu(h, big['f_w_gate'][layer], big['f_w_up'][layer], big['f_w_down'][layer])
    row_loss = make_rowop(_loss_fn, "loss")((x, target), ())[0]
    return jnp.sum(row_loss)


def kernel(x, mem, ab_norm_g, ab_w_in, ab_w_out, s5_a_re, s5_a_im, s5_log_dt, s5_b_re, s5_b_im, s5_c_re, s5_c_im, s5_d, s5_w_glu, s5_b_glu, gdn_conv_w, gdn_a_log, gdn_dt_bias, gdn_out_norm_g, c_norm_g, c_w_qkv, c_w_out, c_q_norm_g, c_k_norm_g, c_rel_bias, mem_norm_g, xa_norm_g, xa_w_q, xa_w_kv, xa_w_out, xa_q_norm_g, xa_k_norm_g, f_norm_g, f_w_gate, f_w_up, f_w_down, loss_target, m_ab_norm_g, m_ab_w_in, m_ab_w_out, m_s5_a_re, m_s5_a_im, m_s5_log_dt, m_s5_b_re, m_s5_b_im, m_s5_c_re, m_s5_c_im, m_s5_d, m_s5_w_glu, m_s5_b_glu, m_gdn_conv_w, m_gdn_a_log, m_gdn_dt_bias, m_gdn_out_norm_g, m_c_norm_g, m_c_w_qkv, m_c_w_out, m_c_q_norm_g, m_c_k_norm_g, m_c_rel_bias, m_mem_norm_g, m_xa_norm_g, m_xa_w_q, m_xa_w_kv, m_xa_w_out, m_xa_q_norm_g, m_xa_k_norm_g, m_f_norm_g, m_f_w_gate, m_f_w_up, m_f_w_down, v_ab_norm_g, v_ab_w_in, v_ab_w_out, v_s5_a_re, v_s5_a_im, v_s5_log_dt, v_s5_b_re, v_s5_b_im, v_s5_c_re, v_s5_c_im, v_s5_d, v_s5_w_glu, v_s5_b_glu, v_gdn_conv_w, v_gdn_a_log, v_gdn_dt_bias, v_gdn_out_norm_g, v_c_norm_g, v_c_w_qkv, v_c_w_out, v_c_q_norm_g, v_c_k_norm_g, v_c_rel_bias, v_mem_norm_g, v_xa_norm_g, v_xa_w_q, v_xa_w_kv, v_xa_w_out, v_xa_q_norm_g, v_xa_k_norm_g, v_f_norm_g, v_f_w_gate, v_f_w_up, v_f_w_down):
    given = dict(locals())
    sharded = list(SHARDED)

    shards = [_as3d(given[n]).astype(SHARDED[n][1]) for n in sharded]
    gathered = _exchange(shards, ["gather"] * len(shards), "weights_allgather")
    big = {}
    for n, g in zip(sharded, gathered):
        layers = _layers_from_gathered(g, SHARDED[n][0])
        if n == 'ab_w_in':
            layers = [jnp.pad(w, ((0, 0), (0, AB_IN_PAD - AB_IN))) for w in layers]
        if n == 's5_w_glu':
            layers = [w.astype(F32) for w in layers]
        big[n] = layers
    small = {n: given[n] for n in REPLICATED}

    loss_local, (g_big, g_small, g_x) = jax.value_and_grad(_forward_loss, argnums=(0, 1, 2))(
        big, small, x[0], mem[0], loss_target[0])

    chunks = []
    for n in sharded:
        layers = g_big[n]
        if n == 'ab_w_in':
            layers = [w[:, :AB_IN] for w in layers]
        chunks.append(_chunks_from_layers(layers, SHARDED[n][0]).astype(SHARDED[n][1]))
    packed_g = _pack([g_small[n] for n in REPLICATED])
    received = _exchange(chunks + [packed_g], ["scatter"] * len(chunks) + ["gather"], "grads_exchange")

    results = {}
    for n, slots in zip(sharded, received[:-1]):
        shape = given[n].shape
        to2d = lambda a: a.reshape(-1, a.shape[-1])
        outs = _adam_call(to2d(given[n]), to2d(given['m_' + n]), to2d(given['v_' + n]),
                          slots.reshape(N_DEV, -1, slots.shape[-1]), "adamw_" + n)
        results[n] = [o.reshape(shape) for o in outs]
    outs = _adam_call(_pack([given[n] for n in REPLICATED]), _pack([given['m_' + n] for n in REPLICATED]),
                      _pack([given['v_' + n] for n in REPLICATED]), received[-1], "adamw_replicated")
    shapes = [given[n].shape for n in REPLICATED]
    for j, parts in enumerate(zip(*[_unpack(o, shapes) for o in outs])):
        results[REPLICATED[j]] = list(parts)

    loss = lax.psum(loss_local, ("x", "y", "c"))
    return (loss, g_x[None], *[results[n][0] for n in WEIGHT_NAMES], *[results[n][1] for n in WEIGHT_NAMES],
            *[results[n][2] for n in WEIGHT_NAMES], *[results[n][3] for n in WEIGHT_NAMES])
```

```python
import functools
import math

import jax
import jax.numpy as jnp
import numpy as np
from jax import lax
from jax.experimental import pallas as pl
from jax.experimental.pallas import tpu as pltpu

F32 = jnp.float32
BF16 = jnp.bfloat16
HI = lax.Precision.HIGHEST

N_DEV = 8
D_MODEL = 1024
SEQ = 2048
DEPTH = 4
CHUNK = 64
N_MEM = 256
RMS_EPS = 1e-6
S5_WIDTH = 512
S5_GROUP = 16
S5_GROUPS = 32
S5_STATE = 64
GDN_HEAD_DIM = 128
GDN_WIDTH = 512
GDN_HEADS = 4
GDN_CONV = 4
AB_IN = S5_WIDTH + 4 * GDN_WIDTH + 2 * GDN_HEADS
AB_IN_PAD = 2688
CA_HEADS = 16
CA_HEAD_DIM = 64
CA_LEFT = 8
CA_BAND = (CA_LEFT + 1) * CHUNK
CA_PAD = CA_LEFT * CHUNK
MAX_REL = 128
XA_HEADS = 4
XA_HEAD_DIM = 256
FFN = 2816
ADAM_LR, ADAM_B1, ADAM_B2, ADAM_EPS, ADAM_WD, ADAM_STEP = 0.001, 0.9, 0.999, 1e-08, 0.01, 10

VMEM_LIMIT = 48 * 1024 * 1024
LANE = 128
SUBLANE = 8


def _cparams(sem=None):
    return pltpu.CompilerParams(dimension_semantics=sem, vmem_limit_bytes=VMEM_LIMIT)


def _divisor_tile(n, target, unit=LANE):
    if n <= target:
        return n
    best = None
    for t in range(unit, target + 1, unit):
        if n % t == 0:
            best = t
    assert best is not None, (n, target)
    return best


def _matmul(a, b, *, ta=False, tb=False, out_dtype=F32, name="mm"):
    if ta:
        k_dim, m_dim = a.shape
    else:
        m_dim, k_dim = a.shape
    if tb:
        n_dim, kb = b.shape
    else:
        kb, n_dim = b.shape
    assert kb == k_dim, (a.shape, b.shape, ta, tb)
    tm = _divisor_tile(m_dim, 512)
    tn = _divisor_tile(n_dim, 512)
    tk = _divisor_tile(k_dim, 1408)
    nk = k_dim // tk
    dims = (((0 if ta else 1,), (1 if tb else 0,)), ((), ()))

    def body(a_ref, b_ref, o_ref, acc_ref):
        k = pl.program_id(2)

        @pl.when(k == 0)
        def _():
            acc_ref[...] = jnp.zeros_like(acc_ref)

        acc_ref[...] += lax.dot_general(a_ref[...].astype(BF16), b_ref[...].astype(BF16), dims,
                                        preferred_element_type=F32)

        @pl.when(k == nk - 1)
        def _():
            o_ref[...] = acc_ref[...].astype(o_ref.dtype)

    a_spec = pl.BlockSpec((tk, tm), lambda i, j, k: (k, i)) if ta else pl.BlockSpec((tm, tk), lambda i, j, k: (i, k))
    b_spec = pl.BlockSpec((tn, tk), lambda i, j, k: (j, k)) if tb else pl.BlockSpec((tk, tn), lambda i, j, k: (k, j))
    return pl.pallas_call(
        body,
        grid=(m_dim // tm, n_dim // tn, nk),
        in_specs=[a_spec, b_spec],
        out_specs=pl.BlockSpec((tm, tn), lambda i, j, k: (i, j)),
        out_shape=jax.ShapeDtypeStruct((m_dim, n_dim), out_dtype),
        scratch_shapes=[pltpu.VMEM((tm, tn), F32)],
        compiler_params=_cparams(("parallel", "parallel", "arbitrary")),
        name=name,
    )(a, b)


@jax.custom_vjp
def linear(a, w):
    return _matmul(a, w, name="linear_fwd")


def _linear_fwd(a, w):
    return _matmul(a, w, name="linear_fwd"), (a, w)


def _linear_bwd(res, dy):
    a, w = res
    da = _matmul(dy, w, tb=True, name="linear_da")
    dw = _matmul(a, dy, ta=True, out_dtype=w.dtype, name="linear_dw")
    return da, dw


linear.defvjp(_linear_fwd, _linear_bwd)


def make_rowop(fn, name, tm=256):
    def specs(rows, params):
        row_specs = [pl.BlockSpec((tm, r.shape[1]), lambda i: (i, 0)) for r in rows]
        par_specs = [pl.BlockSpec(p.shape, lambda i: (0, 0)) for p in params]
        return row_specs, par_specs

    def out_structs(rows, params):
        tiles = [jax.ShapeDtypeStruct((tm, r.shape[1]), r.dtype) for r in rows]
        return jax.eval_shape(lambda r, p: fn(*r, *p), tiles, list(params))

    def fwd_call(rows, params):
        m_dim = rows[0].shape[0]
        n_in = len(rows) + len(params)
        outs = out_structs(rows, params)

        def body(*refs):
            res = fn(*[r[...] for r in refs[:n_in]])
            for o_ref, r in zip(refs[n_in:], res):
                o_ref[...] = r.astype(o_ref.dtype)

        row_specs, par_specs = specs(rows, params)
        return pl.pallas_call(
            body,
            grid=(m_dim // tm,),
            in_specs=row_specs + par_specs,
            out_specs=[pl.BlockSpec((tm, o.shape[1]), lambda i: (i, 0)) for o in outs],
            out_shape=[jax.ShapeDtypeStruct((m_dim, o.shape[1]), o.dtype) for o in outs],
            compiler_params=_cparams(("parallel",)),
            name=name + "_fwd",
        )(*rows, *params)

    def bwd_call(rows, params, cts):
        m_dim = rows[0].shape[0]
        n_rows, n_par = len(rows), len(params)
        n_in = n_rows + n_par
        n_ct = len(cts)

        def body(*refs):
            vals = [r[...] for r in refs[:n_in]]
            ct_vals = tuple(r[...] for r in refs[n_in:n_in + n_ct])
            drow_refs = refs[n_in + n_ct:n_in + n_ct + n_rows]
            dpar_refs = refs[n_in + n_ct + n_rows:]
            _, pullback = jax.vjp(fn, *vals)
            grads = pullback(ct_vals)
            for d_ref, g in zip(drow_refs, grads[:n_rows]):
                d_ref[...] = g

            @pl.when(pl.program_id(0) == 0)
            def _():
                for d_ref in dpar_refs:
                    d_ref[...] = jnp.zeros_like(d_ref)

            for d_ref, g in zip(dpar_refs, grads[n_rows:]):
                d_ref[...] += g

        row_specs, par_specs = specs(rows, params)
        ct_specs = [pl.BlockSpec((tm, c.shape[1]), lambda i: (i, 0)) for c in cts]
        res = pl.pallas_call(
            body,
            grid=(m_dim // tm,),
            in_specs=row_specs + par_specs + ct_specs,
            out_specs=row_specs + par_specs,
            out_shape=[jax.ShapeDtypeStruct(r.shape, r.dtype) for r in rows]
            + [jax.ShapeDtypeStruct(p.shape, p.dtype) for p in params],
            compiler_params=_cparams(("arbitrary",)),
            name=name + "_bwd",
        )(*rows, *params, *cts)
        return tuple(res[:n_rows]), tuple(res[n_rows:])

    @jax.custom_vjp
    def op(rows, params):
        return tuple(fwd_call(rows, params))

    def op_fwd(rows, params):
        return tuple(fwd_call(rows, params)), (rows, params)

    def op_bwd(res, cts):
        rows, params = res
        return bwd_call(rows, params, tuple(cts))

    op.defvjp(op_fwd, op_bwd)
    return op


def _rms(x, g):
    return x * lax.rsqrt(jnp.mean(x * x, axis=-1, keepdims=True) + RMS_EPS) * g


def _sigmoid(x):
    return 1.0 / (1.0 + jnp.exp(-x))


def _silu(x):
    return x * _sigmoid(x)


def _bdot(a, b, dims=(((1,), (0,)), ((), ()))):
    return lax.dot_general(a.astype(BF16), b.astype(BF16), dims, preferred_element_type=F32)


def _rmsnorm_fn(x, g):
    return (_rms(x, g),)


def rmsnorm(x, g, name):
    return make_rowop(_rmsnorm_fn, name)((x,), (g.reshape(1, -1),))[0]


def _gelu_tanh(x):
    return 0.5 * x * (1.0 + jnp.tanh(0.7978845608028654 * (x + 0.044715 * x * x * x)))


def _softplus(x):
    return jnp.maximum(x, 0.0) + jnp.log(1.0 + jnp.exp(-jnp.abs(x)))


def _s5_post_fn(y, w_glu, b_glu):
    h = _gelu_tanh(y)
    return (h * _sigmoid(_bdot(h, w_glu) + b_glu),)


def _swiglu_fn(g, u):
    return (_silu(g) * u,)


def _loss_fn(y, t):
    err = y - t
    return (0.5 * jnp.mean(err * err, axis=-1, keepdims=True),)


def _pair_headnorm(x, g2):
    lo = lax.broadcasted_iota(jnp.int32, x.shape, 1) < CA_HEAD_DIM
    sq = x * x
    s_lo = jnp.sum(jnp.where(lo, sq, 0.0), axis=-1, keepdims=True)
    s_hi = jnp.sum(jnp.where(lo, 0.0, sq), axis=-1, keepdims=True)
    ms = jnp.where(lo, s_lo, s_hi) * (1.0 / CA_HEAD_DIM)
    return x * lax.rsqrt(ms + RMS_EPS) * g2


def _ca_qknorm_fn(qkv, qg2, kg2):
    qs, ks = [], []
    for j in range(D_MODEL // LANE):
        qs.append(_pair_headnorm(qkv[:, j * LANE:(j + 1) * LANE], qg2))
        ks.append(_pair_headnorm(qkv[:, D_MODEL + j * LANE:D_MODEL + (j + 1) * LANE], kg2))
    return jnp.concatenate(qs, axis=1), jnp.concatenate(ks, axis=1)


def _xattn_fn(q, k, v, qg, kg):
    outs = []
    for h in range(XA_HEADS):
        sl = slice(h * XA_HEAD_DIM, (h + 1) * XA_HEAD_DIM)
        qh = _rms(q[:, sl], qg)
        kh = _rms(k[:, sl], kg)
        s = _bdot(qh, kh, (((1,), (1,)), ((), ()))) * (XA_HEAD_DIM ** -0.5)
        p = jnp.exp(s - jnp.max(s, axis=-1, keepdims=True))
        p = p / jnp.sum(p, axis=-1, keepdims=True)
        outs.append(_bdot(p, v[:, sl]))
    return (jnp.concatenate(outs, axis=1),)


def _gdn_prep_fn(x0, x1, x2, x3, ab, conv_w, alog, dtb):
    c = conv_w[3:4, :] * x0 + conv_w[2:3, :] * x1 + conv_w[1:2, :] * x2 + conv_w[0:1, :] * x3
    c = _silu(c)
    qs, ks = [], []
    for h in range(GDN_HEADS):
        qh = c[:, h * LANE:(h + 1) * LANE]
        kh = c[:, GDN_WIDTH + h * LANE:GDN_WIDTH + (h + 1) * LANE]
        qs.append(qh * lax.rsqrt(jnp.sum(qh * qh, axis=-1, keepdims=True) + RMS_EPS) * (GDN_HEAD_DIM ** -0.5))
        ks.append(kh * lax.rsqrt(jnp.sum(kh * kh, axis=-1, keepdims=True) + RMS_EPS))
    lane = lax.broadcasted_iota(jnp.int32, ab.shape, 1)
    g = -jnp.exp(alog) * _softplus(ab + dtb)
    beta = _sigmoid(ab)
    bg = jnp.where(lane < GDN_HEADS, g, jnp.where(lane < 2 * GDN_HEADS, beta, 0.0))
    return jnp.concatenate(qs, axis=1), jnp.concatenate(ks, axis=1), c[:, 2 * GDN_WIDTH:], bg


def _gdn_out_fn(o, gate, og):
    outs = []
    for h in range(GDN_HEADS):
        sl = slice(h * LANE, (h + 1) * LANE)
        outs.append(_rms(o[:, sl], og) * _silu(gate[:, sl]))
    return (jnp.concatenate(outs, axis=1),)


CA_QB = 4 * CHUNK
CA_KB = CA_QB + CA_PAD


def _ca_math(q2, kb2, vb2, bias2, c):
    lane = lax.broadcasted_iota(jnp.int32, q2.shape, 1)
    qc = lax.broadcasted_iota(jnp.int32, (CA_QB, CA_KB), 0) // CHUNK
    kc = lax.broadcasted_iota(jnp.int32, (CA_QB, CA_KB), 1) // CHUNK
    valid = (kc >= qc) & (kc <= qc + CA_LEFT) & (kc + c * (CA_QB // CHUNK) >= CA_LEFT)
    out = jnp.zeros(q2.shape, F32)
    for h in range(2):
        mine = (lane >= h * CA_HEAD_DIM) & (lane < (h + 1) * CA_HEAD_DIM)
        qh = jnp.where(mine, q2, 0.0)
        s = _bdot(qh, kb2, (((1,), (1,)), ((), ()))) * (CA_HEAD_DIM ** -0.5) + bias2[h]
        s = jnp.where(valid, s, -1e30)
        p = jnp.exp(s - jnp.max(s, axis=-1, keepdims=True))
        p = p / jnp.sum(p, axis=-1, keepdims=True)
        out = out + jnp.where(mine, _bdot(p, vb2), 0.0)
    return out


def _ca_specs(seq):
    q_spec = pl.BlockSpec((CA_QB, LANE), lambda hp, c: (c, hp))
    kv_spec = pl.BlockSpec((seq + CA_PAD, LANE), lambda hp, c: (0, hp))
    b_spec = pl.BlockSpec((2, CA_QB, CA_KB), lambda hp, c: (hp, 0, 0))
    return (D_MODEL // LANE, seq // CA_QB), q_spec, kv_spec, b_spec


def _ca_fwd_call(q, kpad, vpad, bias):
    grid, q_spec, kv_spec, b_spec = _ca_specs(q.shape[0])

    def body(q_ref, k_ref, v_ref, b_ref, o_ref):
        c = pl.program_id(1)
        start = pl.multiple_of(c * CA_QB, CA_QB)
        o_ref[...] = _ca_math(q_ref[...], k_ref[pl.ds(start, CA_KB), :], v_ref[pl.ds(start, CA_KB), :],
                              b_ref[...], c)

    return pl.pallas_call(
        body, grid=grid, in_specs=[q_spec, kv_spec, kv_spec, b_spec], out_specs=q_spec,
        out_shape=jax.ShapeDtypeStruct(q.shape, F32),
        compiler_params=_cparams(("parallel", "arbitrary")), name="chunkattn_fwd",
    )(q, kpad, vpad, bias)


def _ca_bwd_call(q, kpad, vpad, bias, do):
    grid, q_spec, kv_spec, b_spec = _ca_specs(q.shape[0])

    def body(q_ref, k_ref, v_ref, b_ref, do_ref, dq_ref, dk_ref, dv_ref, db_ref):
        c = pl.program_id(1)
        start = pl.multiple_of(c * CA_QB, CA_QB)

        @pl.when(c == 0)
        def _():
            dk_ref[...] = jnp.zeros_like(dk_ref)
            dv_ref[...] = jnp.zeros_like(dv_ref)
            db_ref[...] = jnp.zeros_like(db_ref)

        _, pullback = jax.vjp(lambda a, b, d, e: _ca_math(a, b, d, e, c), q_ref[...],
                              k_ref[pl.ds(start, CA_KB), :], v_ref[pl.ds(start, CA_KB), :], b_ref[...])
        dq, dkb, dvb, dbias = pullback(do_ref[...])
        dq_ref[...] = dq
        dk_ref[pl.ds(start, CA_KB), :] += dkb
        dv_ref[pl.ds(start, CA_KB), :] += dvb
        db_ref[...] += dbias

    return pl.pallas_call(
        body, grid=grid, in_specs=[q_spec, kv_spec, kv_spec, b_spec, q_spec],
        out_specs=[q_spec, kv_spec, kv_spec, b_spec],
        out_shape=[jax.ShapeDtypeStruct(q.shape, F32), jax.ShapeDtypeStruct(kpad.shape, F32),
                   jax.ShapeDtypeStruct(vpad.shape, F32), jax.ShapeDtypeStruct(bias.shape, F32)],
        compiler_params=_cparams(("parallel", "arbitrary")), name="chunkattn_bwd",
    )(q, kpad, vpad, bias, do)


@jax.custom_vjp
def chunk_attn_core(q, kpad, vpad, bias):
    return _ca_fwd_call(q, kpad, vpad, bias)


def _ca_core_fwd(q, kpad, vpad, bias):
    return _ca_fwd_call(q, kpad, vpad, bias), (q, kpad, vpad, bias)


def _ca_core_bwd(res, do):
    return tuple(_ca_bwd_call(*res, do))


chunk_attn_core.defvjp(_ca_core_fwd, _ca_core_bwd)


S5_GB = 4
S5_U = S5_WIDTH // S5_GB
S5_L = S5_GROUPS * S5_STATE // S5_GB


def _cmul(ar, ai, br, bi):
    return ar * br - ai * bi, ar * bi + ai * br


def _hdot(a, b, dims=(((1,), (0,)), ((), ()))):
    return lax.dot_general(a, b, dims, precision=HI, preferred_element_type=F32)


_NT = (((1,), (1,)), ((), ()))
_TN = (((0,), (0,)), ((), ()))


def _s5_tables(lr, li, reverse):
    p = {1: (lr, li)}
    p[2] = _cmul(*p[1], *p[1])
    p[4] = _cmul(*p[2], *p[2])
    p[3] = _cmul(*p[2], *p[1])
    p[5] = _cmul(*p[4], *p[1])
    p[6] = _cmul(*p[4], *p[2])
    p[7] = _cmul(*p[4], *p[3])
    p[8] = _cmul(*p[4], *p[4])
    row = lax.broadcasted_iota(jnp.int32, (SUBLANE, lr.shape[1]), 0)
    tr = jnp.zeros(row.shape, F32)
    ti = jnp.zeros(row.shape, F32)
    for i in range(SUBLANE):
        k = SUBLANE - i if reverse else i + 1
        tr = jnp.where(row == i, p[k][0], tr)
        ti = jnp.where(row == i, p[k][1], ti)
    return p, (tr, ti), row


def _s5_block_scan(xr, xi, p, tab, row, hr, hi, reverse):
    for k in (1, 2, 4):
        if reverse:
            sr = jnp.where(row < SUBLANE - k, pltpu.roll(xr, SUBLANE - k, 0), 0.0)
            si = jnp.where(row < SUBLANE - k, pltpu.roll(xi, SUBLANE - k, 0), 0.0)
        else:
            sr = jnp.where(row >= k, pltpu.roll(xr, k, 0), 0.0)
            si = jnp.where(row >= k, pltpu.roll(xi, k, 0), 0.0)
        ar, ai = _cmul(p[k][0], p[k][1], sr, si)
        xr, xi = xr + ar, xi + ai
    cr, ci = _cmul(tab[0], tab[1], hr, hi)
    return xr + cr, xi + ci


def _s5_forward_scan(sr_ref, si_ref, lr, li):
    n_blocks = sr_ref.shape[0] // SUBLANE
    p, tab, row = _s5_tables(lr, li, False)

    def step(b, carry):
        base = pl.multiple_of(b * SUBLANE, SUBLANE)
        xr, xi = _s5_block_scan(sr_ref[pl.ds(base, SUBLANE), :], si_ref[pl.ds(base, SUBLANE), :],
                                p, tab, row, carry[0], carry[1], False)
        sr_ref[pl.ds(base, SUBLANE), :] = xr
        si_ref[pl.ds(base, SUBLANE), :] = xi
        return xr[SUBLANE - 1:SUBLANE, :], xi[SUBLANE - 1:SUBLANE, :]

    zero = jnp.zeros((1, lr.shape[1]), F32)
    lax.fori_loop(0, n_blocks, step, (zero, zero))


def _s5_specs(seq):
    u_spec = pl.BlockSpec((seq, S5_U), lambda g: (0, g))
    bd_spec = pl.BlockSpec((1, S5_U, S5_L), lambda g: (g, 0, 0))
    cd_spec = pl.BlockSpec((1, S5_L, S5_U), lambda g: (g, 0, 0))
    lam_spec = pl.BlockSpec((1, 2, S5_L), lambda g: (g, 0, 0))
    d_spec = pl.BlockSpec((1, S5_U), lambda g: (0, g))
    return u_spec, bd_spec, cd_spec, lam_spec, d_spec


S5_ROWS = 256


def _row_chunks(seq, fn):
    rows_per = min(S5_ROWS, seq)

    def step(r, carry):
        fn(pl.ds(pl.multiple_of(r * rows_per, rows_per), rows_per))
        return carry

    lax.fori_loop(0, seq // rows_per, step, 0)


def _s5_fwd_call(u, bdr, bdi, cdr, cdi, lam, d):
    seq = u.shape[0]
    u_spec, bd_spec, cd_spec, lam_spec, d_spec = _s5_specs(seq)

    def body(u_ref, bdr_ref, bdi_ref, cdr_ref, cdi_ref, lam_ref, d_ref, y_ref, sr_ref, si_ref):
        def project_in(rows):
            uv = u_ref[rows, :]
            sr_ref[rows, :] = _hdot(uv, bdr_ref[0])
            si_ref[rows, :] = _hdot(uv, bdi_ref[0])

        def project_out(rows):
            y_ref[rows, :] = (_hdot(sr_ref[rows, :], cdr_ref[0]) - _hdot(si_ref[rows, :], cdi_ref[0])
                              + d_ref[...] * u_ref[rows, :])

        _row_chunks(seq, project_in)
        _s5_forward_scan(sr_ref, si_ref, lam_ref[0, 0:1, :], lam_ref[0, 1:2, :])
        _row_chunks(seq, project_out)

    return pl.pallas_call(
        body, grid=(S5_GB,), in_specs=[u_spec, bd_spec, bd_spec, cd_spec, cd_spec, lam_spec, d_spec],
        out_specs=u_spec, out_shape=jax.ShapeDtypeStruct(u.shape, F32),
        scratch_shapes=[pltpu.VMEM((seq, S5_L), F32), pltpu.VMEM((seq, S5_L), F32)],
        compiler_params=_cparams(("parallel",)), name="s5_fwd",
    )(u, bdr, bdi, cdr, cdi, lam, d)


def _s5_bwd_call(u, bdr, bdi, cdr, cdi, lam, d, dy):
    seq = u.shape[0]
    n_blocks = seq // SUBLANE
    u_spec, bd_spec, cd_spec, lam_spec, d_spec = _s5_specs(seq)

    def body(u_ref, bdr_ref, bdi_ref, cdr_ref, cdi_ref, lam_ref, d_ref, dy_ref,
             du_ref, dbdr_ref, dbdi_ref, dcdr_ref, dcdi_ref, dlam_ref, dd_ref, sr_ref, si_ref, gr_ref, gi_ref):
        lr, li = lam_ref[0, 0:1, :], lam_ref[0, 1:2, :]

        def project_in(rows):
            uv = u_ref[rows, :]
            dyv = dy_ref[rows, :]
            sr_ref[rows, :] = _hdot(uv, bdr_ref[0])
            si_ref[rows, :] = _hdot(uv, bdi_ref[0])
            gr_ref[rows, :] = _hdot(dyv, cdr_ref[0], _NT)
            gi_ref[rows, :] = -_hdot(dyv, cdi_ref[0], _NT)

        _row_chunks(seq, project_in)
        _s5_forward_scan(sr_ref, si_ref, lr, li)
        p, tab, row = _s5_tables(lr, -li, True)

        def step(i, carry):
            hr, hi, acc_r, acc_i = carry
            b = n_blocks - 1 - i
            base = pl.multiple_of(b * SUBLANE, SUBLANE)
            xr, xi = _s5_block_scan(gr_ref[pl.ds(base, SUBLANE), :], gi_ref[pl.ds(base, SUBLANE), :],
                                    p, tab, row, hr, hi, True)
            gr_ref[pl.ds(base, SUBLANE), :] = xr
            gi_ref[pl.ds(base, SUBLANE), :] = xi
            prev = pl.multiple_of(jnp.maximum(b - 1, 0) * SUBLANE, SUBLANE)
            keep = (b > 0).astype(F32)
            last_r = sr_ref[pl.ds(prev, SUBLANE), :][SUBLANE - 1:SUBLANE, :] * keep
            last_i = si_ref[pl.ds(prev, SUBLANE), :][SUBLANE - 1:SUBLANE, :] * keep
            pr = jnp.where(row >= 1, pltpu.roll(sr_ref[pl.ds(base, SUBLANE), :], 1, 0), last_r)
            pi = jnp.where(row >= 1, pltpu.roll(si_ref[pl.ds(base, SUBLANE), :], 1, 0), last_i)
            acc_r = acc_r + pr * xr + pi * xi
            acc_i = acc_i + pr * xi - pi * xr
            return xr[0:1, :], xi[0:1, :], acc_r, acc_i

        zero = jnp.zeros((1, S5_L), F32)
        zacc = jnp.zeros((SUBLANE, S5_L), F32)
        _, _, acc_r, acc_i = lax.fori_loop(0, n_blocks, step, (zero, zero, zacc, zacc))
        dlam_ref[0, 0:1, :] = jnp.sum(acc_r, axis=0, keepdims=True)
        dlam_ref[0, 1:2, :] = jnp.sum(acc_i, axis=0, keepdims=True)
        for ref in (dbdr_ref, dbdi_ref, dcdr_ref, dcdi_ref, dd_ref):
            ref[...] = jnp.zeros_like(ref)

        def grads(rows):
            uv, dyv = u_ref[rows, :], dy_ref[rows, :]
            grv, giv = gr_ref[rows, :], gi_ref[rows, :]
            du_ref[rows, :] = _hdot(grv, bdr_ref[0], _NT) + _hdot(giv, bdi_ref[0], _NT) + d_ref[...] * dyv
            dbdr_ref[0] += _hdot(uv, grv, _TN)
            dbdi_ref[0] += _hdot(uv, giv, _TN)
            dcdr_ref[0] += _hdot(sr_ref[rows, :], dyv, _TN)
            dcdi_ref[0] -= _hdot(si_ref[rows, :], dyv, _TN)
            dd_ref[...] += jnp.sum(dyv * uv, axis=0, keepdims=True)

        _row_chunks(seq, grads)

    scratch = [pltpu.VMEM((seq, S5_L), F32) for _ in range(4)]
    return pl.pallas_call(
        body, grid=(S5_GB,),
        in_specs=[u_spec, bd_spec, bd_spec, cd_spec, cd_spec, lam_spec, d_spec, u_spec],
        out_specs=[u_spec, bd_spec, bd_spec, cd_spec, cd_spec, lam_spec, d_spec],
        out_shape=[jax.ShapeDtypeStruct(a.shape, F32) for a in (u, bdr, bdi, cdr, cdi, lam, d)],
        scratch_shapes=scratch, compiler_params=_cparams(("parallel",)), name="s5_bwd",
    )(u, bdr, bdi, cdr, cdi, lam, d, dy)


@jax.custom_vjp
def s5_core(u, bdr, bdi, cdr, cdi, lam, d):
    return _s5_fwd_call(u, bdr, bdi, cdr, cdi, lam, d)


def _s5_core_fwd(*args):
    return _s5_fwd_call(*args), args


def _s5_core_bwd(res, dy):
    return tuple(_s5_bwd_call(*res, dy))


s5_core.defvjp(_s5_core_fwd, _s5_core_bwd)


def _s5_discretize(a_re, a_im, log_dt, b_re, b_im, c_re, c_im, d):
    dt = jnp.exp(log_dt)[:, None]
    mag = jnp.exp(a_re * dt)
    lbr, lbi = mag * jnp.cos(a_im * dt), mag * jnp.sin(a_im * dt)
    den = a_re * a_re + a_im * a_im
    fr = ((lbr - 1.0) * a_re + lbi * a_im) / den
    fi = (lbi * a_re - (lbr - 1.0) * a_im) / den
    bbr = fr[:, :, None] * b_re - fi[:, :, None] * b_im
    bbi = fr[:, :, None] * b_im + fi[:, :, None] * b_re
    eye = jnp.eye(S5_GROUPS // S5_GB, dtype=F32)
    gl = S5_GROUPS // S5_GB

    def bd(t):
        return jnp.einsum('bgpc,gh->bgchp', t.reshape(S5_GB, gl, S5_STATE, S5_GROUP), eye).reshape(S5_GB, S5_U, S5_L)

    def cd(t):
        return jnp.einsum('bgcp,gh->bgphc', t.reshape(S5_GB, gl, S5_GROUP, S5_STATE), eye).reshape(S5_GB, S5_L, S5_U)

    lam = jnp.stack([lbr.reshape(S5_GB, S5_L), lbi.reshape(S5_GB, S5_L)], axis=1)
    return bd(bbr), bd(bbi), cd(c_re), cd(c_im), lam, d.reshape(1, S5_WIDTH)


def _gdn_chunk(q, k, v, g_col, b_col, st):
    r = lax.broadcasted_iota(jnp.int32, (CHUNK, CHUNK), 0)
    c = lax.broadcasted_iota(jnp.int32, (CHUNK, CHUNK), 1)
    eye = (r == c).astype(F32)
    strict = r > c
    causal = r >= c
    g_row = jnp.sum(g_col * eye, axis=0, keepdims=True)
    gcum = jnp.sum(jnp.where(causal, g_row, 0.0), axis=1, keepdims=True)
    gcum_row = jnp.sum(gcum * eye, axis=0, keepdims=True)
    diff = gcum - gcum_row
    decay_strict = jnp.where(strict, jnp.exp(jnp.where(strict, diff, 0.0)), 0.0)
    decay_causal = jnp.where(causal, jnp.exp(jnp.where(causal, diff, 0.0)), 0.0)
    gamma = jnp.exp(gcum)
    g_last = jnp.sum(jnp.where(lax.broadcasted_iota(jnp.int32, (CHUNK, 1), 0) == CHUNK - 1, gcum, 0.0),
                     axis=0, keepdims=True)
    kk = _bdot(k, k, _NT)
    neg_a = -(b_col * kk * decay_strict)
    x = jnp.concatenate([b_col * v, (b_col * gamma) * k], axis=1)
    npow = neg_a
    for i in range(5):
        y = _hdot(npow, jnp.concatenate([x, npow], axis=1))
        x = x + y[:, :2 * GDN_HEAD_DIM]
        npow = y[:, 2 * GDN_HEAD_DIM:]
    x = x + _hdot(npow, x)
    u_new, w_k = x[:, :GDN_HEAD_DIM], x[:, GDN_HEAD_DIM:]
    qk = _bdot(q, k, _NT) * decay_causal
    q_g = q * gamma
    k_tail = k * jnp.exp(g_last - gcum)
    w = u_new - _bdot(w_k, st)
    o = _bdot(q_g, st) + _bdot(qk, w)
    st_new = jnp.exp(g_last) * st + _bdot(k_tail, w, _TN)
    return o, st_new


def _gdn_cols(bgv, h):
    lane = lax.broadcasted_iota(jnp.int32, bgv.shape, 1)
    g_col = jnp.sum(jnp.where(lane == h, bgv, 0.0), axis=1, keepdims=True)
    b_col = jnp.sum(jnp.where(lane == GDN_HEADS + h, bgv, 0.0), axis=1, keepdims=True)
    return g_col, b_col


def _gdn_fwd_call(q, k, v, bg):
    seq = q.shape[0]
    n_chunks = seq // CHUNK
    x_spec = pl.BlockSpec((CHUNK, GDN_WIDTH), lambda n: (n, 0))
    bg_spec = pl.BlockSpec((CHUNK, LANE), lambda n: (n, 0))
    st_spec = pl.BlockSpec((1, GDN_WIDTH, GDN_HEAD_DIM), lambda n: (n, 0, 0))

    def body(q_ref, k_ref, v_ref, bg_ref, o_ref, st_out_ref, st_ref):
        @pl.when(pl.program_id(0) == 0)
        def _():
            st_ref[...] = jnp.zeros_like(st_ref)

        bgv = bg_ref[...]
        for h in range(GDN_HEADS):
            sl = slice(h * GDN_HEAD_DIM, (h + 1) * GDN_HEAD_DIM)
            g_col, b_col = _gdn_cols(bgv, h)
            st = st_ref[sl, :]
            st_out_ref[0, sl, :] = st
            o, st_new = _gdn_chunk(q_ref[:, sl], k_ref[:, sl], v_ref[:, sl], g_col, b_col, st)
            o_ref[:, sl] = o
            st_ref[sl, :] = st_new

    return pl.pallas_call(
        body, grid=(n_chunks,), in_specs=[x_spec, x_spec, x_spec, bg_spec], out_specs=[x_spec, st_spec],
        out_shape=[jax.ShapeDtypeStruct(q.shape, F32),
                   jax.ShapeDtypeStruct((n_chunks, GDN_WIDTH, GDN_HEAD_DIM), F32)],
        scratch_shapes=[pltpu.VMEM((GDN_WIDTH, GDN_HEAD_DIM), F32)],
        compiler_params=_cparams(("arbitrary",)), name="gdn_fwd",
    )(q, k, v, bg)


def _gdn_bwd_call(q, k, v, bg, states, do):
    seq = q.shape[0]
    n_chunks = seq // CHUNK
    x_spec = pl.BlockSpec((CHUNK, GDN_WIDTH), lambda i: (n_chunks - 1 - i, 0))
    bg_spec = pl.BlockSpec((CHUNK, LANE), lambda i: (n_chunks - 1 - i, 0))
    st_spec = pl.BlockSpec((1, GDN_WIDTH, GDN_HEAD_DIM), lambda i: (n_chunks - 1 - i, 0, 0))

    def body(q_ref, k_ref, v_ref, bg_ref, st_in_ref, do_ref, dq_ref, dk_ref, dv_ref, dbg_ref, dst_ref):
        @pl.when(pl.program_id(0) == 0)
        def _():
            dst_ref[...] = jnp.zeros_like(dst_ref)

        bgv = bg_ref[...]
        lane = lax.broadcasted_iota(jnp.int32, bgv.shape, 1)
        dbg = jnp.zeros(bgv.shape, F32)
        for h in range(GDN_HEADS):
            sl = slice(h * GDN_HEAD_DIM, (h + 1) * GDN_HEAD_DIM)
            g_col, b_col = _gdn_cols(bgv, h)
            _, pullback = jax.vjp(_gdn_chunk, q_ref[:, sl], k_ref[:, sl], v_ref[:, sl], g_col, b_col,
                                  st_in_ref[0, sl, :])
            dq, dk, dv, dg, db, dst = pullback((do_ref[:, sl], dst_ref[sl, :]))
            dq_ref[:, sl] = dq
            dk_ref[:, sl] = dk
            dv_ref[:, sl] = dv
            dst_ref[sl, :] = dst
            dbg = dbg + jnp.where(lane == h, dg, 0.0) + jnp.where(lane == GDN_HEADS + h, db, 0.0)
        dbg_ref[...] = dbg

    return pl.pallas_call(
        body, grid=(n_chunks,), in_specs=[x_spec, x_spec, x_spec, bg_spec, st_spec, x_spec],
        out_specs=[x_spec, x_spec, x_spec, bg_spec],
        out_shape=[jax.ShapeDtypeStruct(q.shape, F32)] * 3 + [jax.ShapeDtypeStruct(bg.shape, F32)],
        scratch_shapes=[pltpu.VMEM((GDN_WIDTH, GDN_HEAD_DIM), F32)],
        compiler_params=_cparams(("arbitrary",)), name="gdn_bwd",
    )(q, k, v, bg, states, do)


@jax.custom_vjp
def gdn_core(q, k, v, bg):
    return _gdn_fwd_call(q, k, v, bg)[0]


def _gdn_core_fwd(q, k, v, bg):
    o, states = _gdn_fwd_call(q, k, v, bg)
    return o, (q, k, v, bg, states)


def _gdn_core_bwd(res, do):
    return tuple(_gdn_bwd_call(*res, do))


gdn_core.defvjp(_gdn_core_fwd, _gdn_core_bwd)


def _row(v):
    return v.reshape(1, -1)


def _lane_pad(v):
    return jnp.pad(v, (0, LANE - v.shape[0])).reshape(1, LANE)


def _delay_rows(x, k):
    return jnp.pad(x, ((k, 0), (0, 0)))[:x.shape[0]]


def s5_mixer(u, a_re, a_im, log_dt, b_re, b_im, c_re, c_im, d, w_glu, b_glu):
    y = s5_core(u, *_s5_discretize(a_re, a_im, log_dt, b_re, b_im, c_re, c_im, d))
    return make_rowop(_s5_post_fn, "s5_post")((y,), (w_glu, _row(b_glu)))[0]


def gated_deltanet(qkv, gate, ab, conv_w, a_log, dt_bias, out_g):
    rows = (qkv, _delay_rows(qkv, 1), _delay_rows(qkv, 2), _delay_rows(qkv, 3), ab)
    q, k, v, bg = make_rowop(_gdn_prep_fn, "gdn_prep")(rows, (conv_w, _lane_pad(a_log), _lane_pad(dt_bias)))
    o = gdn_core(q, k, v, bg)
    return make_rowop(_gdn_out_fn, "gdn_out")((o, gate), (_row(out_g),))[0]


def chunk_attention(h, w_qkv, w_out, q_g, k_g, rel_bias):
    qkv = linear(h, w_qkv)
    qn, kn = make_rowop(_ca_qknorm_fn, "ca_qknorm")((qkv,), (_row(jnp.tile(q_g, 2)), _row(jnp.tile(k_g, 2))))
    kpad = jnp.pad(kn, ((CA_PAD, 0), (0, 0)))
    vpad = jnp.pad(qkv[:, 2 * D_MODEL:], ((CA_PAD, 0), (0, 0)))
    o = chunk_attn_core(qn, kpad, vpad, _rel_bias_matrix(rel_bias))
    return linear(o, w_out)


def memory_cross_attention(h, mem_n, w_q, w_kv, w_out, q_g, k_g):
    q = linear(h, w_q)
    kv = linear(mem_n, w_kv)
    o = make_rowop(_xattn_fn, "xattn")((q,), (kv[:, :D_MODEL], kv[:, D_MODEL:], _row(q_g), _row(k_g)))[0]
    return linear(o, w_out)


def swiglu(h, w_gate, w_up, w_down):
    a = make_rowop(_swiglu_fn, "swiglu")((linear(h, w_gate), linear(h, w_up)), ())[0]
    return linear(a, w_down)


def _rel_bias_matrix(rel_bias):
    heads = rel_bias.shape[0]
    n = CA_QB - 1 + CA_KB
    n_far = CA_KB - 1 - MAX_REL
    n_neg = n - n_far - (2 * MAX_REL + 1)
    vec = jnp.concatenate([jnp.broadcast_to(rel_bias[:, 2 * MAX_REL:], (heads, n_far)),
                           jnp.flip(rel_bias, axis=1),
                           jnp.broadcast_to(rel_bias[:, :1], (heads, n_neg))], axis=1)
    flat = jnp.tile(jnp.pad(vec, ((0, 0), (0, 1))), (1, CA_QB))[:, :CA_QB * n]
    return flat.reshape(heads, CA_QB, n)[:, :, CA_QB - 1:]


def _exchange(arrays, modes, name):
    n = len(arrays)
    out_shapes = [jax.ShapeDtypeStruct((N_DEV,) + a.shape if m == "gather" else a.shape, a.dtype)
                  for a, m in zip(arrays, modes)]

    def body(*refs):
        ins, outs = refs[:n], refs[n:2 * n]
        send_sems, recv_sems, local_sems = refs[2 * n:]
        x, y, c = lax.axis_index("x"), lax.axis_index("y"), lax.axis_index("c")
        me = 4 * x + 2 * y + c
        pending = []
        for i in range(n):
            gather = modes[i] == "gather"
            local = pltpu.make_async_copy(ins[i] if gather else ins[i].at[me], outs[i].at[me], local_sems.at[i])
            local.start()
            pending.append(local)
        for k in range(1, N_DEV):
            px, py, pc = (x + (k >> 2)) % 2, (y + ((k >> 1) & 1)) % 2, (c + (k & 1)) % 2
            peer = 4 * px + 2 * py + pc
            for i in range(n):
                src = ins[i] if modes[i] == "gather" else ins[i].at[peer]
                sem = i * (N_DEV - 1) + k - 1
                send = pltpu.make_async_remote_copy(src_ref=src, dst_ref=outs[i].at[me], send_sem=send_sems.at[sem],
                                                    recv_sem=recv_sems.at[sem], device_id=(px, py, pc),
                                                    device_id_type=pl.DeviceIdType.MESH)
                send.start()
                arrival = pltpu.make_async_remote_copy(src_ref=src, dst_ref=outs[i].at[peer],
                                                       send_sem=send_sems.at[sem], recv_sem=recv_sems.at[sem],
                                                       device_id=(px, py, pc), device_id_type=pl.DeviceIdType.MESH)
                pending.append((send, arrival))
        for item in pending:
            if isinstance(item, tuple):
                item[0].wait_send()
                item[1].wait_recv()
            else:
                item.wait()

    any_spec = pl.BlockSpec(memory_space=pl.ANY)
    return pl.pallas_call(
        body, in_specs=[any_spec] * n, out_specs=[any_spec] * n, out_shape=out_shapes,
        scratch_shapes=[pltpu.SemaphoreType.DMA((n * (N_DEV - 1),)), pltpu.SemaphoreType.DMA((n * (N_DEV - 1),)),
                        pltpu.SemaphoreType.DMA((n,))],
        name=name,
    )(*arrays)


def _adam_call(w, m, v, g_slots, name):
    rows, cols = w.shape
    tr = rows
    if rows * cols > 256 * 1024:
        tr = max(t for t in range(SUBLANE, rows, SUBLANE) if rows % t == 0 and t * cols <= 256 * 1024)
    c1 = 1.0 - ADAM_B1 ** ADAM_STEP
    c2 = 1.0 - ADAM_B2 ** ADAM_STEP

    def body(w_ref, m_ref, v_ref, g_ref, grad_ref, delta_ref, nm_ref, nv_ref):
        g = g_ref[0].astype(F32)
        for k in range(1, N_DEV):
            g = g + g_ref[k].astype(F32)
        m_new = ADAM_B1 * m_ref[...] + (1.0 - ADAM_B1) * g
        v_new = ADAM_B2 * v_ref[...] + (1.0 - ADAM_B2) * (g * g)
        m_hat = m_new / c1
        v_hat = v_new / c2
        grad_ref[...] = g
        delta_ref[...] = -ADAM_LR * (m_hat / (jnp.sqrt(v_hat) + ADAM_EPS) + ADAM_WD * w_ref[...])
        nm_ref[...] = m_new
        nv_ref[...] = v_new

    spec = pl.BlockSpec((tr, cols), lambda i: (i, 0))
    return pl.pallas_call(
        body, grid=(rows // tr,), in_specs=[spec, spec, spec, pl.BlockSpec((N_DEV, tr, cols), lambda i: (0, i, 0))],
        out_specs=[spec] * 4, out_shape=[jax.ShapeDtypeStruct(w.shape, F32)] * 4,
        compiler_params=_cparams(("parallel",)), name=name,
    )(w, m, v, g_slots)


WEIGHT_NAMES = ['ab_norm_g', 'ab_w_in', 'ab_w_out', 's5_a_re', 's5_a_im', 's5_log_dt', 's5_b_re', 's5_b_im', 's5_c_re',
                's5_c_im', 's5_d', 's5_w_glu', 's5_b_glu', 'gdn_conv_w', 'gdn_a_log', 'gdn_dt_bias', 'gdn_out_norm_g',
                'c_norm_g', 'c_w_qkv', 'c_w_out', 'c_q_norm_g', 'c_k_norm_g', 'c_rel_bias', 'mem_norm_g', 'xa_norm_g',
                'xa_w_q', 'xa_w_kv', 'xa_w_out', 'xa_q_norm_g', 'xa_k_norm_g', 'f_norm_g', 'f_w_gate', 'f_w_up',
                'f_w_down']

SHARDED = {
    'ab_w_in': ('col', BF16), 'ab_w_out': ('row', BF16), 's5_w_glu': ('row', BF16), 'gdn_conv_w': ('col', F32),
    'c_norm_g': ('col', F32), 'c_w_qkv': ('col', BF16), 'c_w_out': ('row', BF16), 'xa_w_q': ('row', BF16),
    'xa_w_kv': ('col', BF16), 'xa_w_out': ('row', BF16), 'f_w_gate': ('col', BF16), 'f_w_up': ('col', BF16),
    'f_w_down': ('row', BF16),
}
REPLICATED = [n for n in WEIGHT_NAMES if n not in SHARDED]
PACK_UNIT = SUBLANE * LANE


def _as3d(a):
    return a.reshape(a.shape[0], 1, a.shape[1]) if a.ndim == 2 else a


def _layers_from_gathered(g, axis):
    out = []
    for i in range(g.shape[1]):
        gi = g[:, i]
        if axis == "row":
            out.append(gi.reshape(gi.shape[0] * gi.shape[1], gi.shape[2]))
        else:
            out.append(jnp.transpose(gi, (1, 0, 2)).reshape(gi.shape[1], gi.shape[0] * gi.shape[2]))
    return out


def _chunks_from_layers(layers, axis):
    parts = []
    for full in layers:
        if axis == "row":
            parts.append(full.reshape(N_DEV, full.shape[0] // N_DEV, full.shape[1]))
        else:
            parts.append(jnp.transpose(full.reshape(full.shape[0], N_DEV, full.shape[1] // N_DEV), (1, 0, 2)))
    return jnp.stack(parts, axis=1)


def _pack(arrays):
    flat = []
    for a in arrays:
        size = a.size
        padded = -(-size // PACK_UNIT) * PACK_UNIT
        flat.append(jnp.pad(a.reshape(-1), (0, padded - size)).reshape(-1, LANE))
    return jnp.concatenate(flat, axis=0)


def _unpack(buf, shapes):
    out, row = [], 0
    for shape in shapes:
        size = math.prod(shape)
        rows = -(-size // PACK_UNIT) * SUBLANE
        out.append(buf[row:row + rows].reshape(-1)[:size].reshape(shape))
        row += rows
    return out


def _forward_loss(big, small, x, mem, target):
    mem_n = rmsnorm(mem, small['mem_norm_g'], "mem_norm")
    for layer in range(DEPTH):
        i = layer // 2
        if layer % 2 == 0:
            h = rmsnorm(x, small['ab_norm_g'][i], "ab_norm")
            proj = linear(h, big['ab_w_in'][i])
            u = proj[:, :S5_WIDTH]
            qkv = proj[:, S5_WIDTH:S5_WIDTH + 3 * GDN_WIDTH]
            gate = proj[:, S5_WIDTH + 3 * GDN_WIDTH:S5_WIDTH + 4 * GDN_WIDTH]
            ab = proj[:, S5_WIDTH + 4 * GDN_WIDTH:]
            a_out = s5_mixer(u, small['s5_a_re'][i], small['s5_a_im'][i], small['s5_log_dt'][i], small['s5_b_re'][i],
                             small['s5_b_im'][i], small['s5_c_re'][i], small['s5_c_im'][i], small['s5_d'][i],
                             big['s5_w_glu'][i], small['s5_b_glu'][i])
            b_out = gated_deltanet(qkv, gate, ab, big['gdn_conv_w'][i], small['gdn_a_log'][i],
                                   small['gdn_dt_bias'][i], small['gdn_out_norm_g'][i])
            x = x + linear(jnp.concatenate([a_out, b_out], axis=1), big['ab_w_out'][i])
        else:
            h = rmsnorm(x, big['c_norm_g'][i].reshape(-1), "c_norm")
            x = x + chunk_attention(h, big['c_w_qkv'][i], big['c_w_out'][i], small['c_q_norm_g'][i],
                                    small['c_k_norm_g'][i], small['c_rel_bias'][i])
        h = rmsnorm(x, small['xa_norm_g'][layer], "xa_norm")
        x = x + memory_cross_attention(h, mem_n, big['xa_w_q'][layer], big['xa_w_kv'][layer], big['xa_w_out'][layer],
                                       small['xa_q_norm_g'][layer], small['xa_k_norm_g'][layer])
        h = rmsnorm(x, small['f_norm_g'][layer], "f_norm")
        x = x + swiglu(h, big['f_w_gate'][layer], big['f_w_up'][layer], big['f_w_down'][layer])
    row_loss = make_rowop(_loss_fn, "loss")((x, target), ())[0]
    return jnp.sum(row_loss)


def kernel(x, mem, ab_norm_g, ab_w_in, ab_w_out, s5_a_re, s5_a_im, s5_log_dt, s5_b_re, s5_b_im, s5_c_re, s5_c_im, s5_d, s5_w_glu, s5_b_glu, gdn_conv_w, gdn_a_log, gdn_dt_bias, gdn_out_norm_g, c_norm_g, c_w_qkv, c_w_out, c_q_norm_g, c_k_norm_g, c_rel_bias, mem_norm_g, xa_norm_g, xa_w_q, xa_w_kv, xa_w_out, xa_q_norm_g, xa_k_norm_g, f_norm_g, f_w_gate, f_w_up, f_w_down, loss_target, m_ab_norm_g, m_ab_w_in, m_ab_w_out, m_s5_a_re, m_s5_a_im, m_s5_log_dt, m_s5_b_re, m_s5_b_im, m_s5_c_re, m_s5_c_im, m_s5_d, m_s5_w_glu, m_s5_b_glu, m_gdn_conv_w, m_gdn_a_log, m_gdn_dt_bias, m_gdn_out_norm_g, m_c_norm_g, m_c_w_qkv, m_c_w_out, m_c_q_norm_g, m_c_k_norm_g, m_c_rel_bias, m_mem_norm_g, m_xa_norm_g, m_xa_w_q, m_xa_w_kv, m_xa_w_out, m_xa_q_norm_g, m_xa_k_norm_g, m_f_norm_g, m_f_w_gate, m_f_w_up, m_f_w_down, v_ab_norm_g, v_ab_w_in, v_ab_w_out, v_s5_a_re, v_s5_a_im, v_s5_log_dt, v_s5_b_re, v_s5_b_im, v_s5_c_re, v_s5_c_im, v_s5_d, v_s5_w_glu, v_s5_b_glu, v_gdn_conv_w, v_gdn_a_log, v_gdn_dt_bias, v_gdn_out_norm_g, v_c_norm_g, v_c_w_qkv, v_c_w_out, v_c_q_norm_g, v_c_k_norm_g, v_c_rel_bias, v_mem_norm_g, v_xa_norm_g, v_xa_w_q, v_xa_w_kv, v_xa_w_out, v_xa_q_norm_g, v_xa_k_norm_g, v_f_norm_g, v_f_w_gate, v_f_w_up, v_f_w_down):
    given = dict(locals())
    sharded = list(SHARDED)

    shards = [_as3d(given[n]).astype(SHARDED[n][1]) for n in sharded]
    gathered = _exchange(shards, ["gather"] * len(shards), "weights_allgather")
    big = {}
    for n, g in zip(sharded, gathered):
        layers = _layers_from_gathered(g, SHARDED[n][0])
        if n == 'ab_w_in':
            layers = [jnp.pad(w, ((0, 0), (0, AB_IN_PAD - AB_IN))) for w in layers]
        if n == 's5_w_glu':
            layers = [w.astype(F32) for w in layers]
        big[n] = layers
    small = {n: given[n] for n in REPLICATED}

    loss_local, (g_big, g_small, g_x) = jax.value_and_grad(_forward_loss, argnums=(0, 1, 2))(
        big, small, x[0], mem[0], loss_target[0])

    chunks = []
    for n in sharded:
        layers = g_big[n]
        if n == 'ab_w_in':
            layers = [w[:, :AB_IN] for w in layers]
        chunks.append(_chunks_from_layers(layers, SHARDED[n][0]).astype(SHARDED[n][1]))
    packed_g = _pack([g_small[n] for n in REPLICATED])
    received = _exchange(chunks + [packed_g], ["scatter"] * len(chunks) + ["gather"], "grads_exchange")

    results = {}
    for n, slots in zip(sharded, received[:-1]):
        shape = given[n].shape
        to2d = lambda a: a.reshape(-1, a.shape[-1])
        outs = _adam_call(to2d(given[n]), to2d(given['m_' + n]), to2d(given['v_' + n]),
                          slots.reshape(N_DEV, -1, slots.shape[-1]), "adamw_" + n)
        results[n] = [o.reshape(shape) for o in outs]
    outs = _adam_call(_pack([given[n] for n in REPLICATED]), _pack([given['m_' + n] for n in REPLICATED]),
                      _pack([given['v_' + n] for n in REPLICATED]), received[-1], "adamw_replicated")
    shapes = [given[n].shape for n in REPLICATED]
    for j, parts in enumerate(zip(*[_unpack(o, shapes) for o in outs])):
        results[REPLICATED[j]] = list(parts)

    loss = lax.psum(loss_local, ("x", "y", "c"))
    return (loss, g_x[None], *[results[n][0] for n in WEIGHT_NAMES], *[results[n][1] for n in WEIGHT_NAMES],
            *[results[n][2] for n in WEIGHT_NAMES], *[results[n][3] for n in WEIGHT_NAMES])
```

```python
import functools
import math

import jax
import jax.numpy as jnp
import numpy as np
from jax import lax
from jax.experimental import pallas as pl
from jax.experimental.pallas import tpu as pltpu

F32 = jnp.float32
BF16 = jnp.bfloat16
HI = lax.Precision.HIGHEST

N_DEV = 8
D_MODEL = 1024
SEQ = 2048
DEPTH = 4
CHUNK = 64
N_MEM = 256
RMS_EPS = 1e-6
S5_WIDTH = 512
S5_GROUP = 16
S5_GROUPS = 32
S5_STATE = 64
GDN_HEAD_DIM = 128
GDN_WIDTH = 512
GDN_HEADS = 4
GDN_CONV = 4
AB_IN = S5_WIDTH + 4 * GDN_WIDTH + 2 * GDN_HEADS
AB_IN_PAD = 2688
CA_HEADS = 16
CA_HEAD_DIM = 64
CA_LEFT = 8
CA_BAND = (CA_LEFT + 1) * CHUNK
CA_PAD = CA_LEFT * CHUNK
MAX_REL = 128
XA_HEADS = 4
XA_HEAD_DIM = 256
FFN = 2816
ADAM_LR, ADAM_B1, ADAM_B2, ADAM_EPS, ADAM_WD, ADAM_STEP = 0.001, 0.9, 0.999, 1e-08, 0.01, 10

VMEM_LIMIT = 48 * 1024 * 1024
LANE = 128
SUBLANE = 8


def _cparams(sem=None):
    return pltpu.CompilerParams(dimension_semantics=sem, vmem_limit_bytes=VMEM_LIMIT)


def _divisor_tile(n, target, unit=LANE):
    if n <= target:
        return n
    best = None
    for t in range(unit, target + 1, unit):
        if n % t == 0:
            best = t
    assert best is not None, (n, target)
    return best


def _matmul(a, b, *, ta=False, tb=False, out_dtype=F32, name="mm"):
    if ta:
        k_dim, m_dim = a.shape
    else:
        m_dim, k_dim = a.shape
    if tb:
        n_dim, kb = b.shape
    else:
        kb, n_dim = b.shape
    assert kb == k_dim, (a.shape, b.shape, ta, tb)
    tm = _divisor_tile(m_dim, 512)
    tn = _divisor_tile(n_dim, 512)
    tk = _divisor_tile(k_dim, 1408)
    nk = k_dim // tk
    dims = (((0 if ta else 1,), (1 if tb else 0,)), ((), ()))

    def body(a_ref, b_ref, o_ref, acc_ref):
        k = pl.program_id(2)

        @pl.when(k == 0)
        def _():
            acc_ref[...] = jnp.zeros_like(acc_ref)

        acc_ref[...] += lax.dot_general(a_ref[...].astype(BF16), b_ref[...].astype(BF16), dims,
                                        preferred_element_type=F32)

        @pl.when(k == nk - 1)
        def _():
            o_ref[...] = acc_ref[...].astype(o_ref.dtype)

    a_spec = pl.BlockSpec((tk, tm), lambda i, j, k: (k, i)) if ta else pl.BlockSpec((tm, tk), lambda i, j, k: (i, k))
    b_spec = pl.BlockSpec((tn, tk), lambda i, j, k: (j, k)) if tb else pl.BlockSpec((tk, tn), lambda i, j, k: (k, j))
    return pl.pallas_call(
        body,
        grid=(m_dim // tm, n_dim // tn, nk),
        in_specs=[a_spec, b_spec],
        out_specs=pl.BlockSpec((tm, tn), lambda i, j, k: (i, j)),
        out_shape=jax.ShapeDtypeStruct((m_dim, n_dim), out_dtype),
        scratch_shapes=[pltpu.VMEM((tm, tn), F32)],
        compiler_params=_cparams(("parallel", "parallel", "arbitrary")),
        name=name,
    )(a, b)


@jax.custom_vjp
def linear(a, w):
    return _matmul(a, w, name="linear_fwd")


def _linear_fwd(a, w):
    return _matmul(a, w, name="linear_fwd"), (a, w)


def _linear_bwd(res, dy):
    a, w = res
    da = _matmul(dy, w, tb=True, name="linear_da")
    dw = _matmul(a, dy, ta=True, out_dtype=w.dtype, name="linear_dw")
    return da, dw


linear.defvjp(_linear_fwd, _linear_bwd)


def _mm_call(name, a, b, out_struct, grid, a_spec, b_spec, o_spec, dims, lead):
    nk = grid[-1]
    acc_shape = o_spec.block_shape[1:] if lead[2] else o_spec.block_shape

    def body(a_ref, b_ref, o_ref, acc_ref):
        k = pl.program_id(len(grid) - 1)

        @pl.when(k == 0)
        def _():
            acc_ref[...] = jnp.zeros_like(acc_ref)

        av = a_ref[0] if lead[0] else a_ref[...]
        bv = b_ref[0] if lead[1] else b_ref[...]
        acc_ref[...] += lax.dot_general(av.astype(BF16), bv.astype(BF16), dims, preferred_element_type=F32)

        @pl.when(k == nk - 1)
        def _():
            if lead[2]:
                o_ref[0] = acc_ref[...].astype(o_ref.dtype)
            else:
                o_ref[...] = acc_ref[...].astype(o_ref.dtype)

    return pl.pallas_call(
        body, grid=grid, in_specs=[a_spec, b_spec], out_specs=o_spec, out_shape=out_struct,
        scratch_shapes=[pltpu.VMEM(tuple(acc_shape), F32)],
        compiler_params=_cparams(("parallel", "parallel", "arbitrary")), name=name,
    )(a, b)


_NN = (((1,), (0,)), ((), ()))
_NT_DIMS = (((1,), (1,)), ((), ()))
_TN_DIMS = (((0,), (0,)), ((), ()))


def _cols_fwd(a, g, dm_out):
    m_dim, k_dim = a.shape
    _, _, c_dim = g.shape
    tm = _divisor_tile(m_dim, 512)
    tk = _divisor_tile(k_dim, 1024)
    a_spec = pl.BlockSpec((tm, tk), lambda j, i, k: (i, k))
    b_spec = pl.BlockSpec((1, tk, c_dim), lambda j, i, k: (j, k, 0))
    if dm_out:
        o_spec = pl.BlockSpec((1, tm, c_dim), lambda j, i, k: (j, i, 0))
        out = jax.ShapeDtypeStruct((N_DEV, m_dim, c_dim), F32)
    else:
        o_spec = pl.BlockSpec((tm, c_dim), lambda j, i, k: (i, j))
        out = jax.ShapeDtypeStruct((m_dim, N_DEV * c_dim), F32)
    return _mm_call("cols_fwd", a, g, out, (N_DEV, m_dim // tm, k_dim // tk), a_spec, b_spec, o_spec, _NN,
                    (False, True, dm_out))


def _cols_da(dy, g, dm_out):
    _, k_dim, c_dim = g.shape
    m_dim = dy.shape[1] if dm_out else dy.shape[0]
    tm = _divisor_tile(m_dim, 512)
    tk = _divisor_tile(k_dim, 512)
    if dm_out:
        a_spec = pl.BlockSpec((1, tm, c_dim), lambda i, kb, j: (j, i, 0))
    else:
        a_spec = pl.BlockSpec((tm, c_dim), lambda i, kb, j: (i, j))
    b_spec = pl.BlockSpec((1, tk, c_dim), lambda i, kb, j: (j, kb, 0))
    o_spec = pl.BlockSpec((tm, tk), lambda i, kb, j: (i, kb))
    return _mm_call("cols_da", dy, g, jax.ShapeDtypeStruct((m_dim, k_dim), F32), (m_dim // tm, k_dim // tk, N_DEV),
                    a_spec, b_spec, o_spec, _NT_DIMS, (dm_out, True, False))


def _cols_dg(a, dy, g, dm_out):
    _, k_dim, c_dim = g.shape
    m_dim = a.shape[0]
    tm = _divisor_tile(m_dim, 1024)
    tk = _divisor_tile(k_dim, 512)
    a_spec = pl.BlockSpec((tm, tk), lambda j, kb, m: (m, kb))
    if dm_out:
        b_spec = pl.BlockSpec((1, tm, c_dim), lambda j, kb, m: (j, m, 0))
    else:
        b_spec = pl.BlockSpec((tm, c_dim), lambda j, kb, m: (m, j))
    o_spec = pl.BlockSpec((1, tk, c_dim), lambda j, kb, m: (j, kb, 0))
    return _mm_call("cols_dg", a, dy, jax.ShapeDtypeStruct(g.shape, g.dtype), (N_DEV, k_dim // tk, m_dim // tm),
                    a_spec, b_spec, o_spec, _TN_DIMS, (False, dm_out, True))


def _make_linear_cols(dm_out):
    @jax.custom_vjp
    def op(a, g):
        return _cols_fwd(a, g, dm_out)

    def fwd(a, g):
        return _cols_fwd(a, g, dm_out), (a, g)

    def bwd(res, dy):
        a, g = res
        return _cols_da(dy, g, dm_out), _cols_dg(a, dy, g, dm_out)

    op.defvjp(fwd, bwd)
    return op


linear_cols = _make_linear_cols(False)
linear_cols_dm = _make_linear_cols(True)


def _rows_fwd(a, g):
    _, m_dim, r_dim = a.shape
    n_dim = g.shape[2]
    tm = _divisor_tile(m_dim, 512)
    tn = _divisor_tile(n_dim, 512)
    a_spec = pl.BlockSpec((1, tm, r_dim), lambda i, n, j: (j, i, 0))
    b_spec = pl.BlockSpec((1, r_dim, tn), lambda i, n, j: (j, 0, n))
    o_spec = pl.BlockSpec((tm, tn), lambda i, n, j: (i, n))
    return _mm_call("rows_fwd", a, g, jax.ShapeDtypeStruct((m_dim, n_dim), F32), (m_dim // tm, n_dim // tn, N_DEV),
                    a_spec, b_spec, o_spec, _NN, (True, True, False))


def _rows_da(dy, g):
    m_dim, n_dim = dy.shape
    r_dim = g.shape[1]
    tm = _divisor_tile(m_dim, 512)
    tn = _divisor_tile(n_dim, 1024)
    a_spec = pl.BlockSpec((tm, tn), lambda j, i, n: (i, n))
    b_spec = pl.BlockSpec((1, r_dim, tn), lambda j, i, n: (j, 0, n))
    o_spec = pl.BlockSpec((1, tm, r_dim), lambda j, i, n: (j, i, 0))
    return _mm_call("rows_da", dy, g, jax.ShapeDtypeStruct((N_DEV, m_dim, r_dim), F32),
                    (N_DEV, m_dim // tm, n_dim // tn), a_spec, b_spec, o_spec, _NT_DIMS, (False, True, True))


def _rows_dg(a, dy, g):
    _, m_dim, r_dim = a.shape
    n_dim = dy.shape[1]
    tm = _divisor_tile(m_dim, 1024)
    tn = _divisor_tile(n_dim, 512)
    a_spec = pl.BlockSpec((1, tm, r_dim), lambda j, n, m: (j, m, 0))
    b_spec = pl.BlockSpec((tm, tn), lambda j, n, m: (m, n))
    o_spec = pl.BlockSpec((1, r_dim, tn), lambda j, n, m: (j, 0, n))
    return _mm_call("rows_dg", a, dy, jax.ShapeDtypeStruct(g.shape, g.dtype), (N_DEV, n_dim // tn, m_dim // tm),
                    a_spec, b_spec, o_spec, _TN_DIMS, (True, False, True))


@jax.custom_vjp
def linear_rows_dm(a, g):
    return _rows_fwd(a, g)


def _linear_rows_dm_fwd(a, g):
    return _rows_fwd(a, g), (a, g)


def _linear_rows_dm_bwd(res, dy):
    a, g = res
    return _rows_da(dy, g), _rows_dg(a, dy, g)


linear_rows_dm.defvjp(_linear_rows_dm_fwd, _linear_rows_dm_bwd)


def _cols_to_natural_call(g, width):
    _, k_dim, c_dim = g.shape
    tk = _divisor_tile(k_dim, 256, SUBLANE)

    def body(g_ref, o_ref):
        for j in range(N_DEV):
            o_ref[:, j * c_dim:(j + 1) * c_dim] = g_ref[j]
        if width > N_DEV * c_dim:
            o_ref[:, N_DEV * c_dim:] = jnp.zeros((tk, width - N_DEV * c_dim), o_ref.dtype)

    return pl.pallas_call(
        body, grid=(k_dim // tk,), in_specs=[pl.BlockSpec((N_DEV, tk, c_dim), lambda i: (0, i, 0))],
        out_specs=pl.BlockSpec((tk, width), lambda i: (i, 0)), out_shape=jax.ShapeDtypeStruct((k_dim, width), g.dtype),
        compiler_params=_cparams(("parallel",)), name="cols_to_natural",
    )(g)


def _natural_to_cols_call(w, c_dim):
    k_dim, width = w.shape
    tk = _divisor_tile(k_dim, 256, SUBLANE)

    def body(w_ref, o_ref):
        for j in range(N_DEV):
            o_ref[j] = w_ref[:, j * c_dim:(j + 1) * c_dim]

    return pl.pallas_call(
        body, grid=(k_dim // tk,), in_specs=[pl.BlockSpec((tk, width), lambda i: (i, 0))],
        out_specs=pl.BlockSpec((N_DEV, tk, c_dim), lambda i: (0, i, 0)),
        out_shape=jax.ShapeDtypeStruct((N_DEV, k_dim, c_dim), w.dtype),
        compiler_params=_cparams(("parallel",)), name="natural_to_cols",
    )(w)


@functools.partial(jax.custom_vjp, nondiff_argnums=(1,))
def cols_to_natural(g, width):
    return _cols_to_natural_call(g, width)


def _cols_to_natural_fwd(g, width):
    return _cols_to_natural_call(g, width), g.shape[2]


def _cols_to_natural_bwd(width, c_dim, dw):
    return (_natural_to_cols_call(dw, c_dim),)


cols_to_natural.defvjp(_cols_to_natural_fwd, _cols_to_natural_bwd)


def make_rowop(fn, name, tm=256):
    def specs(rows, params):
        row_specs = [pl.BlockSpec((tm, r.shape[1]), lambda i: (i, 0)) for r in rows]
        par_specs = [pl.BlockSpec(p.shape, lambda i: (0, 0)) for p in params]
        return row_specs, par_specs

    def out_structs(rows, params):
        tiles = [jax.ShapeDtypeStruct((tm, r.shape[1]), r.dtype) for r in rows]
        return jax.eval_shape(lambda r, p: fn(*r, *p), tiles, list(params))

    def fwd_call(rows, params):
        m_dim = rows[0].shape[0]
        n_in = len(rows) + len(params)
        outs = out_structs(rows, params)

        def body(*refs):
            res = fn(*[r[...] for r in refs[:n_in]])
            for o_ref, r in zip(refs[n_in:], res):
                o_ref[...] = r.astype(o_ref.dtype)

        row_specs, par_specs = specs(rows, params)
        return pl.pallas_call(
            body,
            grid=(m_dim // tm,),
            in_specs=row_specs + par_specs,
            out_specs=[pl.BlockSpec((tm, o.shape[1]), lambda i: (i, 0)) for o in outs],
            out_shape=[jax.ShapeDtypeStruct((m_dim, o.shape[1]), o.dtype) for o in outs],
            compiler_params=_cparams(("parallel",)),
            name=name + "_fwd",
        )(*rows, *params)

    def bwd_call(rows, params, cts):
        m_dim = rows[0].shape[0]
        n_rows, n_par = len(rows), len(params)
        n_in = n_rows + n_par
        n_ct = len(cts)

        def body(*refs):
            vals = [r[...] for r in refs[:n_in]]
            ct_vals = tuple(r[...] for r in refs[n_in:n_in + n_ct])
            drow_refs = refs[n_in + n_ct:n_in + n_ct + n_rows]
            dpar_refs = refs[n_in + n_ct + n_rows:]
            _, pullback = jax.vjp(fn, *vals)
            grads = pullback(ct_vals)
            for d_ref, g in zip(drow_refs, grads[:n_rows]):
                d_ref[...] = g

            @pl.when(pl.program_id(0) == 0)
            def _():
                for d_ref in dpar_refs:
                    d_ref[...] = jnp.zeros_like(d_ref)

            for d_ref, g in zip(dpar_refs, grads[n_rows:]):
                d_ref[...] += g

        row_specs, par_specs = specs(rows, params)
        ct_specs = [pl.BlockSpec((tm, c.shape[1]), lambda i: (i, 0)) for c in cts]
        res = pl.pallas_call(
            body,
            grid=(m_dim // tm,),
            in_specs=row_specs + par_specs + ct_specs,
            out_specs=row_specs + par_specs,
            out_shape=[jax.ShapeDtypeStruct(r.shape, r.dtype) for r in rows]
            + [jax.ShapeDtypeStruct(p.shape, p.dtype) for p in params],
            compiler_params=_cparams(("arbitrary",)),
            name=name + "_bwd",
        )(*rows, *params, *cts)
        return tuple(res[:n_rows]), tuple(res[n_rows:])

    @jax.custom_vjp
    def op(rows, params):
        return tuple(fwd_call(rows, params))

    def op_fwd(rows, params):
        return tuple(fwd_call(rows, params)), (rows, params)

    def op_bwd(res, cts):
        rows, params = res
        return bwd_call(rows, params, tuple(cts))

    op.defvjp(op_fwd, op_bwd)
    return op


def _rms(x, g):
    return x * lax.rsqrt(jnp.mean(x * x, axis=-1, keepdims=True) + RMS_EPS) * g


def _sigmoid(x):
    return 1.0 / (1.0 + jnp.exp(-x))


def _silu(x):
    return x * _sigmoid(x)


def _bdot(a, b, dims=(((1,), (0,)), ((), ()))):
    return lax.dot_general(a.astype(BF16), b.astype(BF16), dims, preferred_element_type=F32)


def _rmsnorm_fn(x, g):
    return (_rms(x, g),)


def rmsnorm(x, g, name):
    return make_rowop(_rmsnorm_fn, name)((x,), (g.reshape(1, -1),))[0]


def _gelu_tanh(x):
    return 0.5 * x * (1.0 + jnp.tanh(0.7978845608028654 * (x + 0.044715 * x * x * x)))


def _softplus(x):
    return jnp.maximum(x, 0.0) + jnp.log(1.0 + jnp.exp(-jnp.abs(x)))


def _s5_post_fn(y, w_glu, b_glu):
    h = _gelu_tanh(y)
    return (h * _sigmoid(_bdot(h, w_glu) + b_glu),)


def _swiglu_fn(g, u):
    return (_silu(g) * u,)


def _loss_fn(y, t):
    err = y - t
    return (0.5 * jnp.mean(err * err, axis=-1, keepdims=True),)


def _pair_headnorm(x, g2):
    lo = lax.broadcasted_iota(jnp.int32, x.shape, 1) < CA_HEAD_DIM
    sq = x * x
    s_lo = jnp.sum(jnp.where(lo, sq, 0.0), axis=-1, keepdims=True)
    s_hi = jnp.sum(jnp.where(lo, 0.0, sq), axis=-1, keepdims=True)
    ms = jnp.where(lo, s_lo, s_hi) * (1.0 / CA_HEAD_DIM)
    return x * lax.rsqrt(ms + RMS_EPS) * g2


def _ca_qknorm_fn(qkv, qg2, kg2):
    qs, ks = [], []
    for j in range(D_MODEL // LANE):
        qs.append(_pair_headnorm(qkv[:, j * LANE:(j + 1) * LANE], qg2))
        ks.append(_pair_headnorm(qkv[:, D_MODEL + j * LANE:D_MODEL + (j + 1) * LANE], kg2))
    return jnp.concatenate(qs, axis=1), jnp.concatenate(ks, axis=1)


def _xattn_fn(q, k, v, qg, kg):
    outs = []
    for h in range(XA_HEADS):
        sl = slice(h * XA_HEAD_DIM, (h + 1) * XA_HEAD_DIM)
        qh = _rms(q[:, sl], qg)
        kh = _rms(k[:, sl], kg)
        s = _bdot(qh, kh, (((1,), (1,)), ((), ()))) * (XA_HEAD_DIM ** -0.5)
        p = jnp.exp(s - jnp.max(s, axis=-1, keepdims=True))
        p = p / jnp.sum(p, axis=-1, keepdims=True)
        outs.append(_bdot(p, v[:, sl]))
    return (jnp.concatenate(outs, axis=1),)


def _gdn_prep_fn(x0, x1, x2, x3, ab, conv_w, alog, dtb):
    c = conv_w[3:4, :] * x0 + conv_w[2:3, :] * x1 + conv_w[1:2, :] * x2 + conv_w[0:1, :] * x3
    c = _silu(c)
    qs, ks = [], []
    for h in range(GDN_HEADS):
        qh = c[:, h * LANE:(h + 1) * LANE]
        kh = c[:, GDN_WIDTH + h * LANE:GDN_WIDTH + (h + 1) * LANE]
        qs.append(qh * lax.rsqrt(jnp.sum(qh * qh, axis=-1, keepdims=True) + RMS_EPS) * (GDN_HEAD_DIM ** -0.5))
        ks.append(kh * lax.rsqrt(jnp.sum(kh * kh, axis=-1, keepdims=True) + RMS_EPS))
    lane = lax.broadcasted_iota(jnp.int32, ab.shape, 1)
    g = -jnp.exp(alog) * _softplus(ab + dtb)
    beta = _sigmoid(ab)
    bg = jnp.where(lane < GDN_HEADS, g, jnp.where(lane < 2 * GDN_HEADS, beta, 0.0))
    return jnp.concatenate(qs, axis=1), jnp.concatenate(ks, axis=1), c[:, 2 * GDN_WIDTH:], bg


def _gdn_out_fn(o, gate, og):
    outs = []
    for h in range(GDN_HEADS):
        sl = slice(h * LANE, (h + 1) * LANE)
        outs.append(_rms(o[:, sl], og) * _silu(gate[:, sl]))
    return (jnp.concatenate(outs, axis=1),)


CA_QB = 4 * CHUNK
CA_KB = CA_QB + CA_PAD


def _ca_math(q2, kb2, vb2, bias2, c):
    lane = lax.broadcasted_iota(jnp.int32, q2.shape, 1)
    qc = lax.broadcasted_iota(jnp.int32, (CA_QB, CA_KB), 0) // CHUNK
    kc = lax.broadcasted_iota(jnp.int32, (CA_QB, CA_KB), 1) // CHUNK
    valid = (kc >= qc) & (kc <= qc + CA_LEFT) & (kc + c * (CA_QB // CHUNK) >= CA_LEFT)
    out = jnp.zeros(q2.shape, F32)
    for h in range(2):
        mine = (lane >= h * CA_HEAD_DIM) & (lane < (h + 1) * CA_HEAD_DIM)
        qh = jnp.where(mine, q2, 0.0)
        s = _bdot(qh, kb2, (((1,), (1,)), ((), ()))) * (CA_HEAD_DIM ** -0.5) + bias2[h]
        s = jnp.where(valid, s, -1e30)
        p = jnp.exp(s - jnp.max(s, axis=-1, keepdims=True))
        p = p / jnp.sum(p, axis=-1, keepdims=True)
        out = out + jnp.where(mine, _bdot(p, vb2), 0.0)
    return out


CA_VEC = CA_QB + CA_KB


def _ca_specs(seq):
    q_spec = pl.BlockSpec((CA_QB, LANE), lambda hp, c: (c, hp))
    kv_spec = pl.BlockSpec((seq + CA_PAD, LANE), lambda hp, c: (0, hp))
    b_spec = pl.BlockSpec((1, 2, CA_VEC), lambda hp, c: (hp, 0, 0))
    return (D_MODEL // LANE, seq // CA_QB), q_spec, kv_spec, b_spec


def _ca_bias_from_vector(vec_ref, bias_ref):
    for h in range(2):
        rows = jnp.broadcast_to(vec_ref[0, h:h + 1, :], (CA_QB, CA_VEC))
        bias_ref[h] = pltpu.roll(rows, 0, 1, stride=1, stride_axis=0)[:, CA_QB:]


def _ca_vector_grad(dbias):
    d = jnp.concatenate([jnp.zeros((CA_QB, CA_QB), F32), dbias], axis=1)
    row = lax.broadcasted_iota(jnp.int32, d.shape, 0)
    for bit in range(CA_QB.bit_length() - 1):
        d = jnp.where((row >> bit) & 1 == 1, pltpu.roll(d, CA_VEC - (1 << bit), 1), d)
    return jnp.sum(d, axis=0, keepdims=True)


def _ca_fwd_call(q, kpad, vpad, vec):
    grid, q_spec, kv_spec, b_spec = _ca_specs(q.shape[0])

    def body(q_ref, k_ref, v_ref, vec_ref, o_ref, bias_ref):
        c = pl.program_id(1)
        start = pl.multiple_of(c * CA_QB, CA_QB)

        @pl.when(c == 0)
        def _():
            _ca_bias_from_vector(vec_ref, bias_ref)

        o_ref[...] = _ca_math(q_ref[...], k_ref[pl.ds(start, CA_KB), :], v_ref[pl.ds(start, CA_KB), :],
                              bias_ref[...], c)

    return pl.pallas_call(
        body, grid=grid, in_specs=[q_spec, kv_spec, kv_spec, b_spec], out_specs=q_spec,
        out_shape=jax.ShapeDtypeStruct(q.shape, F32), scratch_shapes=[pltpu.VMEM((2, CA_QB, CA_KB), F32)],
        compiler_params=_cparams(("parallel", "arbitrary")), name="chunkattn_fwd",
    )(q, kpad, vpad, vec)


def _ca_bwd_call(q, kpad, vpad, vec, do):
    grid, q_spec, kv_spec, b_spec = _ca_specs(q.shape[0])
    last = grid[1] - 1

    def body(q_ref, k_ref, v_ref, vec_ref, do_ref, dq_ref, dk_ref, dv_ref, dvec_ref, bias_ref, dbias_ref):
        c = pl.program_id(1)
        start = pl.multiple_of(c * CA_QB, CA_QB)

        @pl.when(c == 0)
        def _():
            _ca_bias_from_vector(vec_ref, bias_ref)
            dk_ref[...] = jnp.zeros_like(dk_ref)
            dv_ref[...] = jnp.zeros_like(dv_ref)
            dbias_ref[...] = jnp.zeros_like(dbias_ref)

        _, pullback = jax.vjp(lambda a, b, d, e: _ca_math(a, b, d, e, c), q_ref[...],
                              k_ref[pl.ds(start, CA_KB), :], v_ref[pl.ds(start, CA_KB), :], bias_ref[...])
        dq, dkb, dvb, dbias = pullback(do_ref[...])
        dq_ref[...] = dq
        dk_ref[pl.ds(start, CA_KB), :] += dkb
        dv_ref[pl.ds(start, CA_KB), :] += dvb
        dbias_ref[...] += dbias

        @pl.when(c == last)
        def _():
            for h in range(2):
                dvec_ref[0, h:h + 1, :] = _ca_vector_grad(dbias_ref[h])

    return pl.pallas_call(
        body, grid=grid, in_specs=[q_spec, kv_spec, kv_spec, b_spec, q_spec],
        out_specs=[q_spec, kv_spec, kv_spec, b_spec],
        out_shape=[jax.ShapeDtypeStruct(q.shape, F32), jax.ShapeDtypeStruct(kpad.shape, F32),
                   jax.ShapeDtypeStruct(vpad.shape, F32), jax.ShapeDtypeStruct(vec.shape, F32)],
        scratch_shapes=[pltpu.VMEM((2, CA_QB, CA_KB), F32), pltpu.VMEM((2, CA_QB, CA_KB), F32)],
        compiler_params=_cparams(("parallel", "arbitrary")), name="chunkattn_bwd",
    )(q, kpad, vpad, vec, do)


@jax.custom_vjp
def chunk_attn_core(q, kpad, vpad, vec):
    return _ca_fwd_call(q, kpad, vpad, vec)


def _ca_core_fwd(q, kpad, vpad, vec):
    return _ca_fwd_call(q, kpad, vpad, vec), (q, kpad, vpad, vec)


def _ca_core_bwd(res, do):
    return tuple(_ca_bwd_call(*res, do))


chunk_attn_core.defvjp(_ca_core_fwd, _ca_core_bwd)


S5_GB = 4
S5_U = S5_WIDTH // S5_GB
S5_L = S5_GROUPS * S5_STATE // S5_GB


def _cmul(ar, ai, br, bi):
    return ar * br - ai * bi, ar * bi + ai * br


def _hdot(a, b, dims=(((1,), (0,)), ((), ()))):
    return lax.dot_general(a, b, dims, precision=HI, preferred_element_type=F32)


_NT = (((1,), (1,)), ((), ()))
_TN = (((0,), (0,)), ((), ()))


def _s5_tables(lr, li, reverse):
    p = {1: (lr, li)}
    p[2] = _cmul(*p[1], *p[1])
    p[4] = _cmul(*p[2], *p[2])
    p[3] = _cmul(*p[2], *p[1])
    p[5] = _cmul(*p[4], *p[1])
    p[6] = _cmul(*p[4], *p[2])
    p[7] = _cmul(*p[4], *p[3])
    p[8] = _cmul(*p[4], *p[4])
    row = lax.broadcasted_iota(jnp.int32, (SUBLANE, lr.shape[1]), 0)
    tr = jnp.zeros(row.shape, F32)
    ti = jnp.zeros(row.shape, F32)
    for i in range(SUBLANE):
        k = SUBLANE - i if reverse else i + 1
        tr = jnp.where(row == i, p[k][0], tr)
        ti = jnp.where(row == i, p[k][1], ti)
    return p, (tr, ti), row


def _s5_block_scan(xr, xi, p, tab, row, hr, hi, reverse):
    for k in (1, 2, 4):
        if reverse:
            sr = jnp.where(row < SUBLANE - k, pltpu.roll(xr, SUBLANE - k, 0), 0.0)
            si = jnp.where(row < SUBLANE - k, pltpu.roll(xi, SUBLANE - k, 0), 0.0)
        else:
            sr = jnp.where(row >= k, pltpu.roll(xr, k, 0), 0.0)
            si = jnp.where(row >= k, pltpu.roll(xi, k, 0), 0.0)
        ar, ai = _cmul(p[k][0], p[k][1], sr, si)
        xr, xi = xr + ar, xi + ai
    cr, ci = _cmul(tab[0], tab[1], hr, hi)
    return xr + cr, xi + ci


def _s5_forward_scan(sr_ref, si_ref, lr, li):
    n_blocks = sr_ref.shape[0] // SUBLANE
    p, tab, row = _s5_tables(lr, li, False)

    def step(b, carry):
        base = pl.multiple_of(b * SUBLANE, SUBLANE)
        xr, xi = _s5_block_scan(sr_ref[pl.ds(base, SUBLANE), :], si_ref[pl.ds(base, SUBLANE), :],
                                p, tab, row, carry[0], carry[1], False)
        sr_ref[pl.ds(base, SUBLANE), :] = xr
        si_ref[pl.ds(base, SUBLANE), :] = xi
        return xr[SUBLANE - 1:SUBLANE, :], xi[SUBLANE - 1:SUBLANE, :]

    zero = jnp.zeros((1, lr.shape[1]), F32)
    lax.fori_loop(0, n_blocks, step, (zero, zero))


def _s5_specs(seq):
    u_spec = pl.BlockSpec((seq, S5_U), lambda g: (0, g))
    bd_spec = pl.BlockSpec((1, S5_U, S5_L), lambda g: (g, 0, 0))
    cd_spec = pl.BlockSpec((1, S5_L, S5_U), lambda g: (g, 0, 0))
    lam_spec = pl.BlockSpec((1, 2, S5_L), lambda g: (g, 0, 0))
    d_spec = pl.BlockSpec((1, S5_U), lambda g: (0, g))
    return u_spec, bd_spec, cd_spec, lam_spec, d_spec


S5_ROWS = 256


def _row_chunks(seq, fn):
    rows_per = min(S5_ROWS, seq)

    def step(r, carry):
        fn(pl.ds(pl.multiple_of(r * rows_per, rows_per), rows_per))
        return carry

    lax.fori_loop(0, seq // rows_per, step, 0)


def _s5_fwd_call(u, bdr, bdi, cdr, cdi, lam, d):
    seq = u.shape[0]
    u_spec, bd_spec, cd_spec, lam_spec, d_spec = _s5_specs(seq)

    def body(u_ref, bdr_ref, bdi_ref, cdr_ref, cdi_ref, lam_ref, d_ref, y_ref, sr_ref, si_ref):
        def project_in(rows):
            uv = u_ref[rows, :]
            sr_ref[rows, :] = _hdot(uv, bdr_ref[0])
            si_ref[rows, :] = _hdot(uv, bdi_ref[0])

        def project_out(rows):
            y_ref[rows, :] = (_hdot(sr_ref[rows, :], cdr_ref[0]) - _hdot(si_ref[rows, :], cdi_ref[0])
                              + d_ref[...] * u_ref[rows, :])

        _row_chunks(seq, project_in)
        _s5_forward_scan(sr_ref, si_ref, lam_ref[0, 0:1, :], lam_ref[0, 1:2, :])
        _row_chunks(seq, project_out)

    return pl.pallas_call(
        body, grid=(S5_GB,), in_specs=[u_spec, bd_spec, bd_spec, cd_spec, cd_spec, lam_spec, d_spec],
        out_specs=u_spec, out_shape=jax.ShapeDtypeStruct(u.shape, F32),
        scratch_shapes=[pltpu.VMEM((seq, S5_L), F32), pltpu.VMEM((seq, S5_L), F32)],
        compiler_params=_cparams(("parallel",)), name="s5_fwd",
    )(u, bdr, bdi, cdr, cdi, lam, d)


def _s5_bwd_call(u, bdr, bdi, cdr, cdi, lam, d, dy):
    seq = u.shape[0]
    n_blocks = seq // SUBLANE
    u_spec, bd_spec, cd_spec, lam_spec, d_spec = _s5_specs(seq)

    def body(u_ref, bdr_ref, bdi_ref, cdr_ref, cdi_ref, lam_ref, d_ref, dy_ref,
             du_ref, dbdr_ref, dbdi_ref, dcdr_ref, dcdi_ref, dlam_ref, dd_ref, sr_ref, si_ref, gr_ref, gi_ref):
        lr, li = lam_ref[0, 0:1, :], lam_ref[0, 1:2, :]

        def project_in(rows):
            uv = u_ref[rows, :]
            dyv = dy_ref[rows, :]
            sr_ref[rows, :] = _hdot(uv, bdr_ref[0])
            si_ref[rows, :] = _hdot(uv, bdi_ref[0])
            gr_ref[rows, :] = _hdot(dyv, cdr_ref[0], _NT)
            gi_ref[rows, :] = -_hdot(dyv, cdi_ref[0], _NT)

        _row_chunks(seq, project_in)
        _s5_forward_scan(sr_ref, si_ref, lr, li)
        p, tab, row = _s5_tables(lr, -li, True)

        def step(i, carry):
            hr, hi, acc_r, acc_i = carry
            b = n_blocks - 1 - i
            base = pl.multiple_of(b * SUBLANE, SUBLANE)
            xr, xi = _s5_block_scan(gr_ref[pl.ds(base, SUBLANE), :], gi_ref[pl.ds(base, SUBLANE), :],
                                    p, tab, row, hr, hi, True)
            gr_ref[pl.ds(base, SUBLANE), :] = xr
            gi_ref[pl.ds(base, SUBLANE), :] = xi
            prev = pl.multiple_of(jnp.maximum(b - 1, 0) * SUBLANE, SUBLANE)
            keep = (b > 0).astype(F32)
            last_r = sr_ref[pl.ds(prev, SUBLANE), :][SUBLANE - 1:SUBLANE, :] * keep
            last_i = si_ref[pl.ds(prev, SUBLANE), :][SUBLANE - 1:SUBLANE, :] * keep
            pr = jnp.where(row >= 1, pltpu.roll(sr_ref[pl.ds(base, SUBLANE), :], 1, 0), last_r)
            pi = jnp.where(row >= 1, pltpu.roll(si_ref[pl.ds(base, SUBLANE), :], 1, 0), last_i)
            acc_r = acc_r + pr * xr + pi * xi
            acc_i = acc_i + pr * xi - pi * xr
            return xr[0:1, :], xi[0:1, :], acc_r, acc_i

        zero = jnp.zeros((1, S5_L), F32)
        zacc = jnp.zeros((SUBLANE, S5_L), F32)
        _, _, acc_r, acc_i = lax.fori_loop(0, n_blocks, step, (zero, zero, zacc, zacc))
        dlam_ref[0, 0:1, :] = jnp.sum(acc_r, axis=0, keepdims=True)
        dlam_ref[0, 1:2, :] = jnp.sum(acc_i, axis=0, keepdims=True)
        for ref in (dbdr_ref, dbdi_ref, dcdr_ref, dcdi_ref, dd_ref):
            ref[...] = jnp.zeros_like(ref)

        def grads(rows):
            uv, dyv = u_ref[rows, :], dy_ref[rows, :]
            grv, giv = gr_ref[rows, :], gi_ref[rows, :]
            du_ref[rows, :] = _hdot(grv, bdr_ref[0], _NT) + _hdot(giv, bdi_ref[0], _NT) + d_ref[...] * dyv
            dbdr_ref[0] += _hdot(uv, grv, _TN)
            dbdi_ref[0] += _hdot(uv, giv, _TN)
            dcdr_ref[0] += _hdot(sr_ref[rows, :], dyv, _TN)
            dcdi_ref[0] -= _hdot(si_ref[rows, :], dyv, _TN)
            dd_ref[...] += jnp.sum(dyv * uv, axis=0, keepdims=True)

        _row_chunks(seq, grads)

    scratch = [pltpu.VMEM((seq, S5_L), F32) for _ in range(4)]
    return pl.pallas_call(
        body, grid=(S5_GB,),
        in_specs=[u_spec, bd_spec, bd_spec, cd_spec, cd_spec, lam_spec, d_spec, u_spec],
        out_specs=[u_spec, bd_spec, bd_spec, cd_spec, cd_spec, lam_spec, d_spec],
        out_shape=[jax.ShapeDtypeStruct(a.shape, F32) for a in (u, bdr, bdi, cdr, cdi, lam, d)],
        scratch_shapes=scratch, compiler_params=_cparams(("parallel",)), name="s5_bwd",
    )(u, bdr, bdi, cdr, cdi, lam, d, dy)


@jax.custom_vjp
def s5_core(u, bdr, bdi, cdr, cdi, lam, d):
    return _s5_fwd_call(u, bdr, bdi, cdr, cdi, lam, d)


def _s5_core_fwd(*args):
    return _s5_fwd_call(*args), args


def _s5_core_bwd(res, dy):
    return tuple(_s5_bwd_call(*res, dy))


s5_core.defvjp(_s5_core_fwd, _s5_core_bwd)


def _s5_discretize(a_re, a_im, log_dt, b_re, b_im, c_re, c_im, d):
    dt = jnp.exp(log_dt)[:, None]
    mag = jnp.exp(a_re * dt)
    lbr, lbi = mag * jnp.cos(a_im * dt), mag * jnp.sin(a_im * dt)
    den = a_re * a_re + a_im * a_im
    fr = ((lbr - 1.0) * a_re + lbi * a_im) / den
    fi = (lbi * a_re - (lbr - 1.0) * a_im) / den
    bbr = fr[:, :, None] * b_re - fi[:, :, None] * b_im
    bbi = fr[:, :, None] * b_im + fi[:, :, None] * b_re
    eye = jnp.eye(S5_GROUPS // S5_GB, dtype=F32)
    gl = S5_GROUPS // S5_GB

    def bd(t):
        return jnp.einsum('bgpc,gh->bgchp', t.reshape(S5_GB, gl, S5_STATE, S5_GROUP), eye).reshape(S5_GB, S5_U, S5_L)

    def cd(t):
        return jnp.einsum('bgcp,gh->bgphc', t.reshape(S5_GB, gl, S5_GROUP, S5_STATE), eye).reshape(S5_GB, S5_L, S5_U)

    lam = jnp.stack([lbr.reshape(S5_GB, S5_L), lbi.reshape(S5_GB, S5_L)], axis=1)
    return bd(bbr), bd(bbi), cd(c_re), cd(c_im), lam, d.reshape(1, S5_WIDTH)


def _gdn_chunk(q, k, v, g_col, b_col, st):
    r = lax.broadcasted_iota(jnp.int32, (CHUNK, CHUNK), 0)
    c = lax.broadcasted_iota(jnp.int32, (CHUNK, CHUNK), 1)
    eye = (r == c).astype(F32)
    strict = r > c
    causal = r >= c
    g_row = jnp.sum(g_col * eye, axis=0, keepdims=True)
    gcum = jnp.sum(jnp.where(causal, g_row, 0.0), axis=1, keepdims=True)
    gcum_row = jnp.sum(gcum * eye, axis=0, keepdims=True)
    diff = gcum - gcum_row
    decay_strict = jnp.where(strict, jnp.exp(jnp.where(strict, diff, 0.0)), 0.0)
    decay_causal = jnp.where(causal, jnp.exp(jnp.where(causal, diff, 0.0)), 0.0)
    gamma = jnp.exp(gcum)
    g_last = jnp.sum(jnp.where(lax.broadcasted_iota(jnp.int32, (CHUNK, 1), 0) == CHUNK - 1, gcum, 0.0),
                     axis=0, keepdims=True)
    kk = _bdot(k, k, _NT)
    neg_a = -(b_col * kk * decay_strict)
    x = jnp.concatenate([b_col * v, (b_col * gamma) * k], axis=1)
    npow = neg_a
    for i in range(5):
        y = _hdot(npow, jnp.concatenate([x, npow], axis=1))
        x = x + y[:, :2 * GDN_HEAD_DIM]
        npow = y[:, 2 * GDN_HEAD_DIM:]
    x = x + _hdot(npow, x)
    u_new, w_k = x[:, :GDN_HEAD_DIM], x[:, GDN_HEAD_DIM:]
    qk = _bdot(q, k, _NT) * decay_causal
    q_g = q * gamma
    k_tail = k * jnp.exp(g_last - gcum)
    w = u_new - _bdot(w_k, st)
    o = _bdot(q_g, st) + _bdot(qk, w)
    st_new = jnp.exp(g_last) * st + _bdot(k_tail, w, _TN)
    return o, st_new


def _gdn_cols(bgv, h):
    lane = lax.broadcasted_iota(jnp.int32, bgv.shape, 1)
    g_col = jnp.sum(jnp.where(lane == h, bgv, 0.0), axis=1, keepdims=True)
    b_col = jnp.sum(jnp.where(lane == GDN_HEADS + h, bgv, 0.0), axis=1, keepdims=True)
    return g_col, b_col


def _gdn_fwd_call(q, k, v, bg):
    seq = q.shape[0]
    n_chunks = seq // CHUNK
    x_spec = pl.BlockSpec((CHUNK, GDN_WIDTH), lambda n: (n, 0))
    bg_spec = pl.BlockSpec((CHUNK, LANE), lambda n: (n, 0))
    st_spec = pl.BlockSpec((1, GDN_WIDTH, GDN_HEAD_DIM), lambda n: (n, 0, 0))

    def body(q_ref, k_ref, v_ref, bg_ref, o_ref, st_out_ref, st_ref):
        @pl.when(pl.program_id(0) == 0)
        def _():
            st_ref[...] = jnp.zeros_like(st_ref)

        bgv = bg_ref[...]
        for h in range(GDN_HEADS):
            sl = slice(h * GDN_HEAD_DIM, (h + 1) * GDN_HEAD_DIM)
            g_col, b_col = _gdn_cols(bgv, h)
            st = st_ref[sl, :]
            st_out_ref[0, sl, :] = st
            o, st_new = _gdn_chunk(q_ref[:, sl], k_ref[:, sl], v_ref[:, sl], g_col, b_col, st)
            o_ref[:, sl] = o
            st_ref[sl, :] = st_new

    return pl.pallas_call(
        body, grid=(n_chunks,), in_specs=[x_spec, x_spec, x_spec, bg_spec], out_specs=[x_spec, st_spec],
        out_shape=[jax.ShapeDtypeStruct(q.shape, F32),
                   jax.ShapeDtypeStruct((n_chunks, GDN_WIDTH, GDN_HEAD_DIM), F32)],
        scratch_shapes=[pltpu.VMEM((GDN_WIDTH, GDN_HEAD_DIM), F32)],
        compiler_params=_cparams(("arbitrary",)), name="gdn_fwd",
    )(q, k, v, bg)


def _gdn_bwd_call(q, k, v, bg, states, do):
    seq = q.shape[0]
    n_chunks = seq // CHUNK
    x_spec = pl.BlockSpec((CHUNK, GDN_WIDTH), lambda i: (n_chunks - 1 - i, 0))
    bg_spec = pl.BlockSpec((CHUNK, LANE), lambda i: (n_chunks - 1 - i, 0))
    st_spec = pl.BlockSpec((1, GDN_WIDTH, GDN_HEAD_DIM), lambda i: (n_chunks - 1 - i, 0, 0))

    def body(q_ref, k_ref, v_ref, bg_ref, st_in_ref, do_ref, dq_ref, dk_ref, dv_ref, dbg_ref, dst_ref):
        @pl.when(pl.program_id(0) == 0)
        def _():
            dst_ref[...] = jnp.zeros_like(dst_ref)

        bgv = bg_ref[...]
        lane = lax.broadcasted_iota(jnp.int32, bgv.shape, 1)
        dbg = jnp.zeros(bgv.shape, F32)
        for h in range(GDN_HEADS):
            sl = slice(h * GDN_HEAD_DIM, (h + 1) * GDN_HEAD_DIM)
            g_col, b_col = _gdn_cols(bgv, h)
            _, pullback = jax.vjp(_gdn_chunk, q_ref[:, sl], k_ref[:, sl], v_ref[:, sl], g_col, b_col,
                                  st_in_ref[0, sl, :])
            dq, dk, dv, dg, db, dst = pullback((do_ref[:, sl], dst_ref[sl, :]))
            dq_ref[:, sl] = dq
            dk_ref[:, sl] = dk
            dv_ref[:, sl] = dv
            dst_ref[sl, :] = dst
            dbg = dbg + jnp.where(lane == h, dg, 0.0) + jnp.where(lane == GDN_HEADS + h, db, 0.0)
        dbg_ref[...] = dbg

    return pl.pallas_call(
        body, grid=(n_chunks,), in_specs=[x_spec, x_spec, x_spec, bg_spec, st_spec, x_spec],
        out_specs=[x_spec, x_spec, x_spec, bg_spec],
        out_shape=[jax.ShapeDtypeStruct(q.shape, F32)] * 3 + [jax.ShapeDtypeStruct(bg.shape, F32)],
        scratch_shapes=[pltpu.VMEM((GDN_WIDTH, GDN_HEAD_DIM), F32)],
        compiler_params=_cparams(("arbitrary",)), name="gdn_bwd",
    )(q, k, v, bg, states, do)


@jax.custom_vjp
def gdn_core(q, k, v, bg):
    return _gdn_fwd_call(q, k, v, bg)[0]


def _gdn_core_fwd(q, k, v, bg):
    o, states = _gdn_fwd_call(q, k, v, bg)
    return o, (q, k, v, bg, states)


def _gdn_core_bwd(res, do):
    return tuple(_gdn_bwd_call(*res, do))


gdn_core.defvjp(_gdn_core_fwd, _gdn_core_bwd)


def _row(v):
    return v.reshape(1, -1)


def _lane_pad(v):
    return jnp.pad(v, (0, LANE - v.shape[0])).reshape(1, LANE)


def _delay_rows(x, k):
    return jnp.pad(x, ((k, 0), (0, 0)))[:x.shape[0]]


def s5_mixer(u, a_re, a_im, log_dt, b_re, b_im, c_re, c_im, d, w_glu, b_glu):
    y = s5_core(u, *_s5_discretize(a_re, a_im, log_dt, b_re, b_im, c_re, c_im, d))
    return make_rowop(_s5_post_fn, "s5_post")((y,), (w_glu, _row(b_glu)))[0]


def gated_deltanet(qkv, gate, ab, conv_w, a_log, dt_bias, out_g):
    rows = (qkv, _delay_rows(qkv, 1), _delay_rows(qkv, 2), _delay_rows(qkv, 3), ab)
    q, k, v, bg = make_rowop(_gdn_prep_fn, "gdn_prep")(rows, (conv_w, _lane_pad(a_log), _lane_pad(dt_bias)))
    o = gdn_core(q, k, v, bg)
    return make_rowop(_gdn_out_fn, "gdn_out")((o, gate), (_row(out_g),))[0]


def chunk_attention(h, w_qkv, w_out, q_g, k_g, rel_bias):
    qkv = linear_cols(h, w_qkv)
    qn, kn = make_rowop(_ca_qknorm_fn, "ca_qknorm")((qkv,), (_row(jnp.tile(q_g, 2)), _row(jnp.tile(k_g, 2))))
    kpad = jnp.pad(kn, ((CA_PAD, 0), (0, 0)))
    vpad = jnp.pad(qkv[:, 2 * D_MODEL:], ((CA_PAD, 0), (0, 0)))
    o = chunk_attn_core(qn, kpad, vpad, _rel_bias_vector(rel_bias))
    return linear(o, w_out)


def memory_cross_attention(h, mem_n, w_q, w_kv, w_out, q_g, k_g):
    q = linear(h, w_q)
    kv = linear_cols(mem_n, w_kv)
    o = make_rowop(_xattn_fn, "xattn")((q,), (kv[:, :D_MODEL], kv[:, D_MODEL:], _row(q_g), _row(k_g)))[0]
    return linear(o, w_out)


def swiglu(h, w_gate, w_up, w_down):
    g = linear_cols_dm(h, w_gate)
    u = linear_cols_dm(h, w_up)
    rows = g.shape[0] * g.shape[1]
    a = make_rowop(_swiglu_fn, "swiglu")((g.reshape(rows, -1), u.reshape(rows, -1)), ())[0]
    return linear_rows_dm(a.reshape(g.shape), w_down)


def _rel_bias_vector(rel_bias):
    heads = rel_bias.shape[0]
    n_far = CA_KB - 1 - MAX_REL
    n_neg = CA_VEC - 1 - n_far - (2 * MAX_REL + 1)
    vec = jnp.concatenate([jnp.zeros((heads, 1), F32),
                           jnp.broadcast_to(rel_bias[:, 2 * MAX_REL:], (heads, n_far)),
                           jnp.flip(rel_bias, axis=1),
                           jnp.broadcast_to(rel_bias[:, :1], (heads, n_neg))], axis=1)
    return vec.reshape(heads // 2, 2, CA_VEC)


def _exchange(arrays, modes, name):
    n = len(arrays)
    out_shapes = [jax.ShapeDtypeStruct((N_DEV,) + a.shape if m == "gather" else a.shape, a.dtype)
                  for a, m in zip(arrays, modes)]

    def body(*refs):
        ins, outs = refs[:n], refs[n:2 * n]
        send_sems, recv_sems, local_sems = refs[2 * n:]
        x, y, c = lax.axis_index("x"), lax.axis_index("y"), lax.axis_index("c")
        me = 4 * x + 2 * y + c
        pending = []
        for i in range(n):
            gather = modes[i] == "gather"
            local = pltpu.make_async_copy(ins[i] if gather else ins[i].at[me], outs[i].at[me], local_sems.at[i])
            local.start()
            pending.append(local)
        for k in range(1, N_DEV):
            px, py, pc = (x + (k >> 2)) % 2, (y + ((k >> 1) & 1)) % 2, (c + (k & 1)) % 2
            peer = 4 * px + 2 * py + pc
            for i in range(n):
                src = ins[i] if modes[i] == "gather" else ins[i].at[peer]
                sem = i * (N_DEV - 1) + k - 1
                send = pltpu.make_async_remote_copy(src_ref=src, dst_ref=outs[i].at[me], send_sem=send_sems.at[sem],
                                                    recv_sem=recv_sems.at[sem], device_id=(px, py, pc),
                                                    device_id_type=pl.DeviceIdType.MESH)
                send.start()
                arrival = pltpu.make_async_remote_copy(src_ref=src, dst_ref=outs[i].at[peer],
                                                       send_sem=send_sems.at[sem], recv_sem=recv_sems.at[sem],
                                                       device_id=(px, py, pc), device_id_type=pl.DeviceIdType.MESH)
                pending.append((send, arrival))
        for item in pending:
            if isinstance(item, tuple):
                item[0].wait_send()
                item[1].wait_recv()
            else:
                item.wait()

    any_spec = pl.BlockSpec(memory_space=pl.ANY)
    return pl.pallas_call(
        body, in_specs=[any_spec] * n, out_specs=[any_spec] * n, out_shape=out_shapes,
        scratch_shapes=[pltpu.SemaphoreType.DMA((n * (N_DEV - 1),)), pltpu.SemaphoreType.DMA((n * (N_DEV - 1),)),
                        pltpu.SemaphoreType.DMA((n,))],
        name=name,
    )(*arrays)


_HBM_SPEC = pl.BlockSpec(memory_space=pltpu.HBM)
_SEM_SPEC = pl.BlockSpec(memory_space=pltpu.SEMAPHORE)
_SIDE_EFFECT = pltpu.SideEffectType.DATAFLOW_SIDE_EFFECTING


def _peer(x, y, c, k):
    return (x + (k >> 2)) % 2, (y + ((k >> 1) & 1)) % 2, (c + (k & 1)) % 2


def _exchange_start(arrays, modes, after, name):
    n = len(arrays)
    n_sem = n * (N_DEV - 1)
    lands = [pltpu.with_memory_space_constraint(lax.empty((N_DEV,) + a.shape if m == "gather" else a.shape, a.dtype),
                                                pltpu.HBM) for a, m in zip(arrays, modes)]
    arrays = [pltpu.with_memory_space_constraint(a, pltpu.HBM) for a in arrays]

    def body(*refs):
        ins, zones = refs[:n], refs[n:2 * n]
        send_sems, recv_sems = refs[2 * n + 1], refs[2 * n + 2]
        token_ref, local_sems = refs[2 * n + 3 + 2 * n], refs[2 * n + 4 + 2 * n]
        x, y, c = lax.axis_index("x"), lax.axis_index("y"), lax.axis_index("c")
        me = 4 * x + 2 * y + c
        own = []
        for i in range(n):
            cp = pltpu.make_async_copy(ins[i] if modes[i] == "gather" else ins[i].at[me], zones[i].at[me],
                                       local_sems.at[i])
            cp.start()
            own.append(cp)
        for k in range(1, N_DEV):
            px, py, pc = _peer(x, y, c, k)
            peer = 4 * px + 2 * py + pc
            for i in range(n):
                sem = i * (N_DEV - 1) + k - 1
                pltpu.make_async_remote_copy(src_ref=ins[i] if modes[i] == "gather" else ins[i].at[peer],
                                             dst_ref=zones[i].at[me], send_sem=send_sems.at[sem],
                                             recv_sem=recv_sems.at[sem], device_id=(px, py, pc),
                                             device_id_type=pl.DeviceIdType.MESH).start()
        for cp in own:
            cp.wait()
        token_ref[...] = jnp.zeros_like(token_ref)

    out_shape = ((pltpu.SemaphoreType.DMA((n_sem,)), pltpu.SemaphoreType.DMA((n_sem,)))
                 + tuple(pltpu.HBM(a.shape, a.dtype) for a in arrays) + tuple(pltpu.HBM(z.shape, z.dtype) for z in lands)
                 + (jax.ShapeDtypeStruct((SUBLANE, LANE), F32),))
    res = pl.pallas_call(
        body, name=name, out_shape=out_shape,
        in_specs=[_HBM_SPEC] * (2 * n) + [pl.BlockSpec(memory_space=pl.ANY)],
        out_specs=(_SEM_SPEC, _SEM_SPEC) + (_HBM_SPEC,) * (2 * n) + (pl.BlockSpec(memory_space=pltpu.VMEM),),
        input_output_aliases={i: 2 + i for i in range(2 * n)},
        scratch_shapes=[pltpu.SemaphoreType.DMA((n,))],
        compiler_params=pltpu.CompilerParams(has_side_effects=_SIDE_EFFECT),
    )(*arrays, *lands, after)
    return res[0], res[1], list(res[2:2 + n]), list(res[2 + n:2 + 2 * n]), res[2 + 2 * n]


def _exchange_wait(started, modes, after, name):
    send_sems, recv_sems, sources, zones, _ = started
    n = len(sources)

    def body(*refs):
        ins, lands = refs[:n], refs[n:2 * n]
        send_ref, recv_ref = refs[2 * n], refs[2 * n + 1]
        x, y, c = lax.axis_index("x"), lax.axis_index("y"), lax.axis_index("c")
        for k in range(1, N_DEV):
            px, py, pc = _peer(x, y, c, k)
            peer = 4 * px + 2 * py + pc
            for i in range(n):
                sem = i * (N_DEV - 1) + k - 1
                cp = pltpu.make_async_remote_copy(src_ref=ins[i] if modes[i] == "gather" else ins[i].at[peer],
                                                  dst_ref=lands[i].at[peer], send_sem=send_ref.at[sem],
                                                  recv_sem=recv_ref.at[sem], device_id=(px, py, pc),
                                                  device_id_type=pl.DeviceIdType.MESH)
                cp.wait_send()
                cp.wait_recv()

    res = pl.pallas_call(
        body, name=name,
        out_shape=tuple(pltpu.HBM(a.shape, a.dtype) for a in sources) + tuple(pltpu.HBM(z.shape, z.dtype) for z in zones),
        in_specs=[_HBM_SPEC] * (2 * n) + [_SEM_SPEC, _SEM_SPEC, pl.BlockSpec(memory_space=pl.ANY)],
        out_specs=(_HBM_SPEC,) * (2 * n), input_output_aliases={i: i for i in range(2 * n)},
        compiler_params=pltpu.CompilerParams(has_side_effects=_SIDE_EFFECT),
    )(*sources, *zones, send_sems, recv_sems, after)
    return list(res[n:])


ADAM_TILE = 64 * 1024


def _adam_call(w, m, v, slots, name):
    n_layers, rows, cols = w.shape
    tr = rows
    if n_layers * rows * cols > ADAM_TILE:
        fits = [t for t in range(SUBLANE, rows, SUBLANE) if rows % t == 0 and n_layers * t * cols <= ADAM_TILE]
        tr = max(fits) if fits else SUBLANE
    c1 = 1.0 - ADAM_B1 ** ADAM_STEP
    c2 = 1.0 - ADAM_B2 ** ADAM_STEP

    def body(*refs):
        w_ref, m_ref, v_ref = refs[:3]
        slot_refs = refs[3:3 + n_layers]
        grad_ref, delta_ref, nm_ref, nv_ref = refs[3 + n_layers:]
        for layer in range(n_layers):
            g = slot_refs[layer][0].astype(F32)
            for k in range(1, N_DEV):
                g = g + slot_refs[layer][k].astype(F32)
            m_new = ADAM_B1 * m_ref[layer] + (1.0 - ADAM_B1) * g
            v_new = ADAM_B2 * v_ref[layer] + (1.0 - ADAM_B2) * (g * g)
            m_hat = m_new / c1
            v_hat = v_new / c2
            grad_ref[layer] = g
            delta_ref[layer] = -ADAM_LR * (m_hat / (jnp.sqrt(v_hat) + ADAM_EPS) + ADAM_WD * w_ref[layer])
            nm_ref[layer] = m_new
            nv_ref[layer] = v_new

    spec = pl.BlockSpec((n_layers, tr, cols), lambda i: (0, i, 0))
    slot_spec = pl.BlockSpec((N_DEV, tr, cols), lambda i: (0, i, 0))
    return pl.pallas_call(
        body, grid=(rows // tr,), in_specs=[spec, spec, spec] + [slot_spec] * n_layers,
        out_specs=[spec] * 4, out_shape=[jax.ShapeDtypeStruct(w.shape, F32)] * 4,
        compiler_params=_cparams(("parallel",)), name=name,
    )(w, m, v, *slots)


WEIGHT_NAMES = ['ab_norm_g', 'ab_w_in', 'ab_w_out', 's5_a_re', 's5_a_im', 's5_log_dt', 's5_b_re', 's5_b_im', 's5_c_re',
                's5_c_im', 's5_d', 's5_w_glu', 's5_b_glu', 'gdn_conv_w', 'gdn_a_log', 'gdn_dt_bias', 'gdn_out_norm_g',
                'c_norm_g', 'c_w_qkv', 'c_w_out', 'c_q_norm_g', 'c_k_norm_g', 'c_rel_bias', 'mem_norm_g', 'xa_norm_g',
                'xa_w_q', 'xa_w_kv', 'xa_w_out', 'xa_q_norm_g', 'xa_k_norm_g', 'f_norm_g', 'f_w_gate', 'f_w_up',
                'f_w_down']

SHARDED = {
    'ab_w_in': ('col', BF16), 'ab_w_out': ('row', BF16), 's5_w_glu': ('row', BF16), 'gdn_conv_w': ('col', F32),
    'c_norm_g': ('col', F32), 'c_w_qkv': ('col', BF16), 'c_w_out': ('row', BF16), 'xa_w_q': ('row', BF16),
    'xa_w_kv': ('col', BF16), 'xa_w_out': ('row', BF16), 'f_w_gate': ('col', BF16), 'f_w_up': ('col', BF16),
    'f_w_down': ('row', BF16),
}
GATHERED_AS_IS = ('c_w_qkv', 'xa_w_kv', 'f_w_gate', 'f_w_up', 'f_w_down')
REPLICATED = [n for n in WEIGHT_NAMES if n not in SHARDED]
PACK_UNIT = SUBLANE * LANE


def _full_from_gathered(g, axis):
    if axis == "row":
        return g.reshape(g.shape[0] * g.shape[1], g.shape[2])
    return jnp.transpose(g, (1, 0, 2)).reshape(g.shape[1], g.shape[0] * g.shape[2])


def _pack(arrays):
    flat = []
    for a in arrays:
        size = a.size
        padded = -(-size // PACK_UNIT) * PACK_UNIT
        flat.append(jnp.pad(a.reshape(-1), (0, padded - size)).reshape(-1, LANE))
    return jnp.concatenate(flat, axis=0)


def _unpack(buf, shapes):
    out, row = [], 0
    for shape in shapes:
        size = math.prod(shape)
        rows = -(-size // PACK_UNIT) * SUBLANE
        out.append(buf[row:row + rows].reshape(-1)[:size].reshape(shape))
        row += rows
    return out


EVEN_SHARDED = ['ab_w_in', 'ab_w_out', 's5_w_glu', 'gdn_conv_w']
ODD_SHARDED = ['c_norm_g', 'c_w_qkv', 'c_w_out']
ALL_SHARDED = ['xa_w_q', 'xa_w_kv', 'xa_w_out', 'f_w_gate', 'f_w_up', 'f_w_down']
EVEN_SMALL = ['ab_norm_g', 's5_a_re', 's5_a_im', 's5_log_dt', 's5_b_re', 's5_b_im', 's5_c_re', 's5_c_im', 's5_d',
              's5_b_glu', 'gdn_a_log', 'gdn_dt_bias', 'gdn_out_norm_g']
ODD_SMALL = ['c_q_norm_g', 'c_k_norm_g', 'c_rel_bias']
ALL_SMALL = ['xa_norm_g', 'xa_q_norm_g', 'xa_k_norm_g', 'f_norm_g']


def _layer_params(layer):
    mixer_big, mixer_small = (EVEN_SHARDED, EVEN_SMALL) if layer % 2 == 0 else (ODD_SHARDED, ODD_SMALL)
    big = [(n, layer // 2) for n in mixer_big] + [(n, layer) for n in ALL_SHARDED]
    small = [(n, layer // 2) for n in mixer_small] + [(n, layer) for n in ALL_SMALL]
    return big, small


def _layer_forward(layer, landed, small, x, mem_n):
    big = {}
    for (n, _), g in zip(_layer_params(layer)[0], landed):
        if n == 'ab_w_in':
            big[n] = cols_to_natural(g, AB_IN_PAD)
        elif n in GATHERED_AS_IS:
            big[n] = g
        elif n == 's5_w_glu':
            big[n] = _full_from_gathered(g, 'row').astype(F32)
        else:
            big[n] = _full_from_gathered(g, SHARDED[n][0])
    if layer % 2 == 0:
        h = rmsnorm(x, small['ab_norm_g'], "ab_norm")
        proj = linear(h, big['ab_w_in'])
        u = proj[:, :S5_WIDTH]
        qkv = proj[:, S5_WIDTH:S5_WIDTH + 3 * GDN_WIDTH]
        gate = proj[:, S5_WIDTH + 3 * GDN_WIDTH:S5_WIDTH + 4 * GDN_WIDTH]
        ab = proj[:, S5_WIDTH + 4 * GDN_WIDTH:]
        a_out = s5_mixer(u, small['s5_a_re'], small['s5_a_im'], small['s5_log_dt'], small['s5_b_re'], small['s5_b_im'],
                         small['s5_c_re'], small['s5_c_im'], small['s5_d'], big['s5_w_glu'], small['s5_b_glu'])
        b_out = gated_deltanet(qkv, gate, ab, big['gdn_conv_w'], small['gdn_a_log'], small['gdn_dt_bias'],
                               small['gdn_out_norm_g'])
        x = x + linear(jnp.concatenate([a_out, b_out], axis=1), big['ab_w_out'])
    else:
        h = rmsnorm(x, big['c_norm_g'].reshape(-1), "c_norm")
        x = x + chunk_attention(h, big['c_w_qkv'], big['c_w_out'], small['c_q_norm_g'], small['c_k_norm_g'],
                                small['c_rel_bias'])
    h = rmsnorm(x, small['xa_norm_g'], "xa_norm")
    x = x + memory_cross_attention(h, mem_n, big['xa_w_q'], big['xa_w_kv'], big['xa_w_out'], small['xa_q_norm_g'],
                                   small['xa_k_norm_g'])
    h = rmsnorm(x, small['f_norm_g'], "f_norm")
    return x + swiglu(h, big['f_w_gate'], big['f_w_up'], big['f_w_down'])


def _loss_rows(x, target):
    return jnp.sum(make_rowop(_loss_fn, "loss")((x, target), ())[0])


def kernel(x, mem, ab_norm_g, ab_w_in, ab_w_out, s5_a_re, s5_a_im, s5_log_dt, s5_b_re, s5_b_im, s5_c_re, s5_c_im, s5_d, s5_w_glu, s5_b_glu, gdn_conv_w, gdn_a_log, gdn_dt_bias, gdn_out_norm_g, c_norm_g, c_w_qkv, c_w_out, c_q_norm_g, c_k_norm_g, c_rel_bias, mem_norm_g, xa_norm_g, xa_w_q, xa_w_kv, xa_w_out, xa_q_norm_g, xa_k_norm_g, f_norm_g, f_w_gate, f_w_up, f_w_down, loss_target, m_ab_norm_g, m_ab_w_in, m_ab_w_out, m_s5_a_re, m_s5_a_im, m_s5_log_dt, m_s5_b_re, m_s5_b_im, m_s5_c_re, m_s5_c_im, m_s5_d, m_s5_w_glu, m_s5_b_glu, m_gdn_conv_w, m_gdn_a_log, m_gdn_dt_bias, m_gdn_out_norm_g, m_c_norm_g, m_c_w_qkv, m_c_w_out, m_c_q_norm_g, m_c_k_norm_g, m_c_rel_bias, m_mem_norm_g, m_xa_norm_g, m_xa_w_q, m_xa_w_kv, m_xa_w_out, m_xa_q_norm_g, m_xa_k_norm_g, m_f_norm_g, m_f_w_gate, m_f_w_up, m_f_w_down, v_ab_norm_g, v_ab_w_in, v_ab_w_out, v_s5_a_re, v_s5_a_im, v_s5_log_dt, v_s5_b_re, v_s5_b_im, v_s5_c_re, v_s5_c_im, v_s5_d, v_s5_w_glu, v_s5_b_glu, v_gdn_conv_w, v_gdn_a_log, v_gdn_dt_bias, v_gdn_out_norm_g, v_c_norm_g, v_c_w_qkv, v_c_w_out, v_c_q_norm_g, v_c_k_norm_g, v_c_rel_bias, v_mem_norm_g, v_xa_norm_g, v_xa_w_q, v_xa_w_kv, v_xa_w_out, v_xa_q_norm_g, v_xa_k_norm_g, v_f_norm_g, v_f_w_gate, v_f_w_up, v_f_w_down):
    given = dict(locals())
    no_after = jnp.zeros((SUBLANE, LANE), F32)

    def shard(n, idx):
        a = given[n][idx]
        return (a.reshape(1, -1) if a.ndim == 1 else a).astype(SHARDED[n][1])

    def gather_start(layer, after):
        arrays = [shard(n, idx) for n, idx in _layer_params(layer)[0]]
        return _exchange_start(arrays, ["gather"] * len(arrays), after, "gather_start_%d" % layer)

    act = x[0]
    mem_n, mem_pullback = jax.vjp(lambda m, g: rmsnorm(m, g, "mem_norm"), mem[0], mem_norm_g)
    started = gather_start(0, no_after)
    pullbacks = []
    for layer in range(DEPTH):
        landed = _exchange_wait(started, ["gather"] * len(started[2]), act, "gather_wait_%d" % layer)
        if layer + 1 < DEPTH:
            started = gather_start(layer + 1, landed[0])
            act = act + started[4][0, 0]
        small = {n: given[n][idx] for n, idx in _layer_params(layer)[1]}
        act, pullback = jax.vjp(functools.partial(_layer_forward, layer), landed, small, act, mem_n)
        pullbacks.append(pullback)
    loss_local, loss_pullback = jax.vjp(_loss_rows, act, loss_target[0])
    d_act = loss_pullback(jnp.ones((), F32))[0]

    d_mem_n = jnp.zeros_like(mem_n)
    g_small = {}
    received = [None] * DEPTH
    started, after = None, no_after
    for layer in reversed(range(DEPTH)):
        d_landed, d_small, d_act, d_mem = pullbacks[layer](d_act)
        d_mem_n = d_mem_n + d_mem
        for n, idx in _layer_params(layer)[1]:
            g_small[(n, idx)] = d_small[n]
        if started is not None:
            received[layer + 1] = _exchange_wait(started, ["scatter"] * len(started[2]), d_act,
                                                 "scatter_wait_%d" % (layer + 1))
            after = received[layer + 1][0]
        started = _exchange_start(list(d_landed), ["scatter"] * len(d_landed), after, "scatter_start_%d" % layer)
        d_act = d_act + started[4][0, 0]
    g_small[('mem_norm_g', None)] = mem_pullback(d_mem_n)[1]

    def small_grad(n):
        if n == 'mem_norm_g':
            return g_small[(n, None)]
        return jnp.stack([g_small[(n, i)] for i in range(given[n].shape[0])], axis=0)

    packed = _exchange([_pack([small_grad(n) for n in REPLICATED])], ["gather"], "small_grads_allgather")[0]
    received[0] = _exchange_wait(started, ["scatter"] * len(started[2]), packed, "scatter_wait_0")

    results = {}
    for n in SHARDED:
        slots = {}
        for layer in range(DEPTH):
            for (pn, idx), r in zip(_layer_params(layer)[0], received[layer]):
                if pn == n:
                    slots[idx] = r
        shape = given[n].shape
        to3d = lambda a: a.reshape(a.shape[0], -1, a.shape[-1])
        outs = _adam_call(to3d(given[n]), to3d(given['m_' + n]), to3d(given['v_' + n]),
                          [slots[i] for i in range(len(slots))], "adamw_" + n)
        results[n] = [o.reshape(shape) for o in outs]
    outs = _adam_call(_pack([given[n] for n in REPLICATED])[None], _pack([given['m_' + n] for n in REPLICATED])[None],
                      _pack([given['v_' + n] for n in REPLICATED])[None], [packed], "adamw_replicated")
    shapes = [given[n].shape for n in REPLICATED]
    for j, parts in enumerate(zip(*[_unpack(o[0], shapes) for o in outs])):
        results[REPLICATED[j]] = list(parts)

    loss = lax.psum(loss_local, ("x", "y", "c"))
    return (loss, d_act[None], *[results[n][0] for n in WEIGHT_NAMES], *[results[n][1] for n in WEIGHT_NAMES],
            *[results[n][2] for n in WEIGHT_NAMES], *[results[n][3] for n in WEIGHT_NAMES])
```

```python
import functools
import math

import jax
import jax.numpy as jnp
import numpy as np
from jax import lax
from jax.experimental import pallas as pl
from jax.experimental.pallas import tpu as pltpu

F32 = jnp.float32
BF16 = jnp.bfloat16
HI = lax.Precision.HIGHEST

N_DEV = 8
D_MODEL = 1024
SEQ = 2048
DEPTH = 4
CHUNK = 64
N_MEM = 256
RMS_EPS = 1e-6
S5_WIDTH = 512
S5_GROUP = 16
S5_GROUPS = 32
S5_STATE = 64
GDN_HEAD_DIM = 128
GDN_WIDTH = 512
GDN_HEADS = 4
GDN_CONV = 4
AB_IN = S5_WIDTH + 4 * GDN_WIDTH + 2 * GDN_HEADS
AB_IN_PAD = 2688
CA_HEADS = 16
CA_HEAD_DIM = 64
CA_LEFT = 8
CA_BAND = (CA_LEFT + 1) * CHUNK
CA_PAD = CA_LEFT * CHUNK
MAX_REL = 128
XA_HEADS = 4
XA_HEAD_DIM = 256
FFN = 2816
ADAM_LR, ADAM_B1, ADAM_B2, ADAM_EPS, ADAM_WD, ADAM_STEP = 0.001, 0.9, 0.999, 1e-08, 0.01, 10

VMEM_LIMIT = 48 * 1024 * 1024
LANE = 128
SUBLANE = 8


def _cparams(sem=None):
    return pltpu.CompilerParams(dimension_semantics=sem, vmem_limit_bytes=VMEM_LIMIT)


def _divisor_tile(n, target, unit=LANE):
    if n <= target:
        return n
    best = None
    for t in range(unit, target + 1, unit):
        if n % t == 0:
            best = t
    assert best is not None, (n, target)
    return best


def _matmul(a, b, *, ta=False, tb=False, out_dtype=F32, name="mm"):
    if ta:
        k_dim, m_dim = a.shape
    else:
        m_dim, k_dim = a.shape
    if tb:
        n_dim, kb = b.shape
    else:
        kb, n_dim = b.shape
    assert kb == k_dim, (a.shape, b.shape, ta, tb)
    tm = _divisor_tile(m_dim, 512)
    tn = _divisor_tile(n_dim, 512)
    tk = _divisor_tile(k_dim, 1408)
    nk = k_dim // tk
    dims = (((0 if ta else 1,), (1 if tb else 0,)), ((), ()))

    def body(a_ref, b_ref, o_ref, acc_ref):
        k = pl.program_id(2)

        @pl.when(k == 0)
        def _():
            acc_ref[...] = jnp.zeros_like(acc_ref)

        acc_ref[...] += lax.dot_general(a_ref[...].astype(BF16), b_ref[...].astype(BF16), dims,
                                        preferred_element_type=F32)

        @pl.when(k == nk - 1)
        def _():
            o_ref[...] = acc_ref[...].astype(o_ref.dtype)

    a_spec = pl.BlockSpec((tk, tm), lambda i, j, k: (k, i)) if ta else pl.BlockSpec((tm, tk), lambda i, j, k: (i, k))
    b_spec = pl.BlockSpec((tn, tk), lambda i, j, k: (j, k)) if tb else pl.BlockSpec((tk, tn), lambda i, j, k: (k, j))
    return pl.pallas_call(
        body,
        grid=(m_dim // tm, n_dim // tn, nk),
        in_specs=[a_spec, b_spec],
        out_specs=pl.BlockSpec((tm, tn), lambda i, j, k: (i, j)),
        out_shape=jax.ShapeDtypeStruct((m_dim, n_dim), out_dtype),
        scratch_shapes=[pltpu.VMEM((tm, tn), F32)],
        compiler_params=_cparams(("parallel", "parallel", "arbitrary")),
        name=name,
    )(a, b)


@jax.custom_vjp
def linear(a, w):
    return _matmul(a, w, name="linear_fwd")


def _linear_fwd(a, w):
    return _matmul(a, w, name="linear_fwd"), (a, w)


def _linear_bwd(res, dy):
    a, w = res
    da = _matmul(dy, w, tb=True, name="linear_da")
    dw = _matmul(a, dy, ta=True, out_dtype=w.dtype, name="linear_dw")
    return da, dw


linear.defvjp(_linear_fwd, _linear_bwd)


def _mm_call(name, a, b, out_struct, grid, a_spec, b_spec, o_spec, dims, lead):
    nk = grid[-1]
    acc_shape = o_spec.block_shape[1:] if lead[2] else o_spec.block_shape

    def body(a_ref, b_ref, o_ref, acc_ref):
        k = pl.program_id(len(grid) - 1)

        @pl.when(k == 0)
        def _():
            acc_ref[...] = jnp.zeros_like(acc_ref)

        av = a_ref[0] if lead[0] else a_ref[...]
        bv = b_ref[0] if lead[1] else b_ref[...]
        acc_ref[...] += lax.dot_general(av.astype(BF16), bv.astype(BF16), dims, preferred_element_type=F32)

        @pl.when(k == nk - 1)
        def _():
            if lead[2]:
                o_ref[0] = acc_ref[...].astype(o_ref.dtype)
            else:
                o_ref[...] = acc_ref[...].astype(o_ref.dtype)

    return pl.pallas_call(
        body, grid=grid, in_specs=[a_spec, b_spec], out_specs=o_spec, out_shape=out_struct,
        scratch_shapes=[pltpu.VMEM(tuple(acc_shape), F32)],
        compiler_params=_cparams(("parallel", "parallel", "arbitrary")), name=name,
    )(a, b)


_NN = (((1,), (0,)), ((), ()))
_NT_DIMS = (((1,), (1,)), ((), ()))
_TN_DIMS = (((0,), (0,)), ((), ()))


def _cols_fwd(a, g, dm_out):
    m_dim, k_dim = a.shape
    _, _, c_dim = g.shape
    tm = _divisor_tile(m_dim, 512)
    tk = _divisor_tile(k_dim, 1024)
    a_spec = pl.BlockSpec((tm, tk), lambda j, i, k: (i, k))
    b_spec = pl.BlockSpec((1, tk, c_dim), lambda j, i, k: (j, k, 0))
    if dm_out:
        o_spec = pl.BlockSpec((1, tm, c_dim), lambda j, i, k: (j, i, 0))
        out = jax.ShapeDtypeStruct((N_DEV, m_dim, c_dim), F32)
    else:
        o_spec = pl.BlockSpec((tm, c_dim), lambda j, i, k: (i, j))
        out = jax.ShapeDtypeStruct((m_dim, N_DEV * c_dim), F32)
    return _mm_call("cols_fwd", a, g, out, (N_DEV, m_dim // tm, k_dim // tk), a_spec, b_spec, o_spec, _NN,
                    (False, True, dm_out))


def _cols_da(dy, g, dm_out):
    _, k_dim, c_dim = g.shape
    m_dim = dy.shape[1] if dm_out else dy.shape[0]
    tm = _divisor_tile(m_dim, 512)
    tk = _divisor_tile(k_dim, 512)
    if dm_out:
        a_spec = pl.BlockSpec((1, tm, c_dim), lambda i, kb, j: (j, i, 0))
    else:
        a_spec = pl.BlockSpec((tm, c_dim), lambda i, kb, j: (i, j))
    b_spec = pl.BlockSpec((1, tk, c_dim), lambda i, kb, j: (j, kb, 0))
    o_spec = pl.BlockSpec((tm, tk), lambda i, kb, j: (i, kb))
    return _mm_call("cols_da", dy, g, jax.ShapeDtypeStruct((m_dim, k_dim), F32), (m_dim // tm, k_dim // tk, N_DEV),
                    a_spec, b_spec, o_spec, _NT_DIMS, (dm_out, True, False))


def _cols_dg(a, dy, g, dm_out):
    _, k_dim, c_dim = g.shape
    m_dim = a.shape[0]
    tm = _divisor_tile(m_dim, 1024)
    tk = _divisor_tile(k_dim, 512)
    a_spec = pl.BlockSpec((tm, tk), lambda j, kb, m: (m, kb))
    if dm_out:
        b_spec = pl.BlockSpec((1, tm, c_dim), lambda j, kb, m: (j, m, 0))
    else:
        b_spec = pl.BlockSpec((tm, c_dim), lambda j, kb, m: (m, j))
    o_spec = pl.BlockSpec((1, tk, c_dim), lambda j, kb, m: (j, kb, 0))
    return _mm_call("cols_dg", a, dy, jax.ShapeDtypeStruct(g.shape, g.dtype), (N_DEV, k_dim // tk, m_dim // tm),
                    a_spec, b_spec, o_spec, _TN_DIMS, (False, dm_out, True))


def _make_linear_cols(dm_out):
    @jax.custom_vjp
    def op(a, g):
        return _cols_fwd(a, g, dm_out)

    def fwd(a, g):
        return _cols_fwd(a, g, dm_out), (a, g)

    def bwd(res, dy):
        a, g = res
        return _cols_da(dy, g, dm_out), _cols_dg(a, dy, g, dm_out)

    op.defvjp(fwd, bwd)
    return op


linear_cols = _make_linear_cols(False)
linear_cols_dm = _make_linear_cols(True)


def _rows_fwd(a, g):
    _, m_dim, r_dim = a.shape
    n_dim = g.shape[2]
    tm = _divisor_tile(m_dim, 512)
    tn = _divisor_tile(n_dim, 512)
    a_spec = pl.BlockSpec((1, tm, r_dim), lambda i, n, j: (j, i, 0))
    b_spec = pl.BlockSpec((1, r_dim, tn), lambda i, n, j: (j, 0, n))
    o_spec = pl.BlockSpec((tm, tn), lambda i, n, j: (i, n))
    return _mm_call("rows_fwd", a, g, jax.ShapeDtypeStruct((m_dim, n_dim), F32), (m_dim // tm, n_dim // tn, N_DEV),
                    a_spec, b_spec, o_spec, _NN, (True, True, False))


def _rows_da(dy, g):
    m_dim, n_dim = dy.shape
    r_dim = g.shape[1]
    tm = _divisor_tile(m_dim, 512)
    tn = _divisor_tile(n_dim, 1024)
    a_spec = pl.BlockSpec((tm, tn), lambda j, i, n: (i, n))
    b_spec = pl.BlockSpec((1, r_dim, tn), lambda j, i, n: (j, 0, n))
    o_spec = pl.BlockSpec((1, tm, r_dim), lambda j, i, n: (j, i, 0))
    return _mm_call("rows_da", dy, g, jax.ShapeDtypeStruct((N_DEV, m_dim, r_dim), F32),
                    (N_DEV, m_dim // tm, n_dim // tn), a_spec, b_spec, o_spec, _NT_DIMS, (False, True, True))


def _rows_dg(a, dy, g):
    _, m_dim, r_dim = a.shape
    n_dim = dy.shape[1]
    tm = _divisor_tile(m_dim, 1024)
    tn = _divisor_tile(n_dim, 512)
    a_spec = pl.BlockSpec((1, tm, r_dim), lambda j, n, m: (j, m, 0))
    b_spec = pl.BlockSpec((tm, tn), lambda j, n, m: (m, n))
    o_spec = pl.BlockSpec((1, r_dim, tn), lambda j, n, m: (j, 0, n))
    return _mm_call("rows_dg", a, dy, jax.ShapeDtypeStruct(g.shape, g.dtype), (N_DEV, n_dim // tn, m_dim // tm),
                    a_spec, b_spec, o_spec, _TN_DIMS, (True, False, True))


@jax.custom_vjp
def linear_rows_dm(a, g):
    return _rows_fwd(a, g)


def _linear_rows_dm_fwd(a, g):
    return _rows_fwd(a, g), (a, g)


def _linear_rows_dm_bwd(res, dy):
    a, g = res
    return _rows_da(dy, g), _rows_dg(a, dy, g)


linear_rows_dm.defvjp(_linear_rows_dm_fwd, _linear_rows_dm_bwd)


def _cols_to_natural_call(g, width):
    _, k_dim, c_dim = g.shape
    tk = _divisor_tile(k_dim, 256, SUBLANE)

    def body(g_ref, o_ref):
        for j in range(N_DEV):
            o_ref[:, j * c_dim:(j + 1) * c_dim] = g_ref[j]
        if width > N_DEV * c_dim:
            o_ref[:, N_DEV * c_dim:] = jnp.zeros((tk, width - N_DEV * c_dim), o_ref.dtype)

    return pl.pallas_call(
        body, grid=(k_dim // tk,), in_specs=[pl.BlockSpec((N_DEV, tk, c_dim), lambda i: (0, i, 0))],
        out_specs=pl.BlockSpec((tk, width), lambda i: (i, 0)), out_shape=jax.ShapeDtypeStruct((k_dim, width), g.dtype),
        compiler_params=_cparams(("parallel",)), name="cols_to_natural",
    )(g)


def _natural_to_cols_call(w, c_dim):
    k_dim, width = w.shape
    tk = _divisor_tile(k_dim, 256, SUBLANE)

    def body(w_ref, o_ref):
        for j in range(N_DEV):
            o_ref[j] = w_ref[:, j * c_dim:(j + 1) * c_dim]

    return pl.pallas_call(
        body, grid=(k_dim // tk,), in_specs=[pl.BlockSpec((tk, width), lambda i: (i, 0))],
        out_specs=pl.BlockSpec((N_DEV, tk, c_dim), lambda i: (0, i, 0)),
        out_shape=jax.ShapeDtypeStruct((N_DEV, k_dim, c_dim), w.dtype),
        compiler_params=_cparams(("parallel",)), name="natural_to_cols",
    )(w)


@functools.partial(jax.custom_vjp, nondiff_argnums=(1,))
def cols_to_natural(g, width):
    return _cols_to_natural_call(g, width)


def _cols_to_natural_fwd(g, width):
    return _cols_to_natural_call(g, width), g.shape[2]


def _cols_to_natural_bwd(width, c_dim, dw):
    return (_natural_to_cols_call(dw, c_dim),)


cols_to_natural.defvjp(_cols_to_natural_fwd, _cols_to_natural_bwd)


def make_rowop(fn, name, tm=256):
    def specs(rows, params):
        row_specs = [pl.BlockSpec((tm, r.shape[1]), lambda i: (i, 0)) for r in rows]
        par_specs = [pl.BlockSpec(p.shape, lambda i: (0, 0)) for p in params]
        return row_specs, par_specs

    def out_structs(rows, params):
        tiles = [jax.ShapeDtypeStruct((tm, r.shape[1]), r.dtype) for r in rows]
        return jax.eval_shape(lambda r, p: fn(*r, *p), tiles, list(params))

    def fwd_call(rows, params):
        m_dim = rows[0].shape[0]
        n_in = len(rows) + len(params)
        outs = out_structs(rows, params)

        def body(*refs):
            res = fn(*[r[...] for r in refs[:n_in]])
            for o_ref, r in zip(refs[n_in:], res):
                o_ref[...] = r.astype(o_ref.dtype)

        row_specs, par_specs = specs(rows, params)
        return pl.pallas_call(
            body,
            grid=(m_dim // tm,),
            in_specs=row_specs + par_specs,
            out_specs=[pl.BlockSpec((tm, o.shape[1]), lambda i: (i, 0)) for o in outs],
            out_shape=[jax.ShapeDtypeStruct((m_dim, o.shape[1]), o.dtype) for o in outs],
            compiler_params=_cparams(("parallel",)),
            name=name + "_fwd",
        )(*rows, *params)

    def bwd_call(rows, params, cts):
        m_dim = rows[0].shape[0]
        n_rows, n_par = len(rows), len(params)
        n_in = n_rows + n_par
        n_ct = len(cts)

        def body(*refs):
            vals = [r[...] for r in refs[:n_in]]
            ct_vals = tuple(r[...] for r in refs[n_in:n_in + n_ct])
            drow_refs = refs[n_in + n_ct:n_in + n_ct + n_rows]
            dpar_refs = refs[n_in + n_ct + n_rows:]
            _, pullback = jax.vjp(fn, *vals)
            grads = pullback(ct_vals)
            for d_ref, g in zip(drow_refs, grads[:n_rows]):
                d_ref[...] = g

            @pl.when(pl.program_id(0) == 0)
            def _():
                for d_ref in dpar_refs:
                    d_ref[...] = jnp.zeros_like(d_ref)

            for d_ref, g in zip(dpar_refs, grads[n_rows:]):
                d_ref[...] += g

        row_specs, par_specs = specs(rows, params)
        ct_specs = [pl.BlockSpec((tm, c.shape[1]), lambda i: (i, 0)) for c in cts]
        res = pl.pallas_call(
            body,
            grid=(m_dim // tm,),
            in_specs=row_specs + par_specs + ct_specs,
            out_specs=row_specs + par_specs,
            out_shape=[jax.ShapeDtypeStruct(r.shape, r.dtype) for r in rows]
            + [jax.ShapeDtypeStruct(p.shape, p.dtype) for p in params],
            compiler_params=_cparams(("arbitrary",)),
            name=name + "_bwd",
        )(*rows, *params, *cts)
        return tuple(res[:n_rows]), tuple(res[n_rows:])

    @jax.custom_vjp
    def op(rows, params):
        return tuple(fwd_call(rows, params))

    def op_fwd(rows, params):
        return tuple(fwd_call(rows, params)), (rows, params)

    def op_bwd(res, cts):
        rows, params = res
        return bwd_call(rows, params, tuple(cts))

    op.defvjp(op_fwd, op_bwd)
    return op


def _rms(x, g):
    return x * lax.rsqrt(jnp.mean(x * x, axis=-1, keepdims=True) + RMS_EPS) * g


def _sigmoid(x):
    return 1.0 / (1.0 + jnp.exp(-x))


def _silu(x):
    return x * _sigmoid(x)


def _bdot(a, b, dims=(((1,), (0,)), ((), ()))):
    return lax.dot_general(a.astype(BF16), b.astype(BF16), dims, preferred_element_type=F32)


def _rmsnorm_fn(x, g):
    return (_rms(x, g),)


def rmsnorm(x, g, name):
    return make_rowop(_rmsnorm_fn, name)((x,), (g.reshape(1, -1),))[0]


def _gelu_tanh(x):
    return 0.5 * x * (1.0 + jnp.tanh(0.7978845608028654 * (x + 0.044715 * x * x * x)))


def _softplus(x):
    return jnp.maximum(x, 0.0) + jnp.log(1.0 + jnp.exp(-jnp.abs(x)))


def _s5_post_fn(y, w_glu, b_glu):
    h = _gelu_tanh(y)
    return (h * _sigmoid(_bdot(h, w_glu) + b_glu),)


def _swiglu_fn(g, u):
    return (_silu(g) * u,)


def _loss_fn(y, t):
    err = y - t
    return (0.5 * jnp.mean(err * err, axis=-1, keepdims=True),)


def _pair_headnorm(x, g2):
    lo = lax.broadcasted_iota(jnp.int32, x.shape, 1) < CA_HEAD_DIM
    sq = x * x
    s_lo = jnp.sum(jnp.where(lo, sq, 0.0), axis=-1, keepdims=True)
    s_hi = jnp.sum(jnp.where(lo, 0.0, sq), axis=-1, keepdims=True)
    ms = jnp.where(lo, s_lo, s_hi) * (1.0 / CA_HEAD_DIM)
    return x * lax.rsqrt(ms + RMS_EPS) * g2


def _ca_qknorm_fn(qkv, qg2, kg2):
    qs, ks = [], []
    for j in range(D_MODEL // LANE):
        qs.append(_pair_headnorm(qkv[:, j * LANE:(j + 1) * LANE], qg2))
        ks.append(_pair_headnorm(qkv[:, D_MODEL + j * LANE:D_MODEL + (j + 1) * LANE], kg2))
    return jnp.concatenate(qs, axis=1), jnp.concatenate(ks, axis=1)


def _xattn_fn(q, k, v, qg, kg):
    outs = []
    for h in range(XA_HEADS):
        sl = slice(h * XA_HEAD_DIM, (h + 1) * XA_HEAD_DIM)
        qh = _rms(q[:, sl], qg)
        kh = _rms(k[:, sl], kg)
        s = _bdot(qh, kh, (((1,), (1,)), ((), ()))) * (XA_HEAD_DIM ** -0.5)
        p = jnp.exp(s - jnp.max(s, axis=-1, keepdims=True))
        p = p / jnp.sum(p, axis=-1, keepdims=True)
        outs.append(_bdot(p, v[:, sl]))
    return (jnp.concatenate(outs, axis=1),)


def _gdn_prep_fn(x0, x1, x2, x3, ab, conv_w, alog, dtb):
    c = conv_w[3:4, :] * x0 + conv_w[2:3, :] * x1 + conv_w[1:2, :] * x2 + conv_w[0:1, :] * x3
    c = _silu(c)
    qs, ks = [], []
    for h in range(GDN_HEADS):
        qh = c[:, h * LANE:(h + 1) * LANE]
        kh = c[:, GDN_WIDTH + h * LANE:GDN_WIDTH + (h + 1) * LANE]
        qs.append(qh * lax.rsqrt(jnp.sum(qh * qh, axis=-1, keepdims=True) + RMS_EPS) * (GDN_HEAD_DIM ** -0.5))
        ks.append(kh * lax.rsqrt(jnp.sum(kh * kh, axis=-1, keepdims=True) + RMS_EPS))
    lane = lax.broadcasted_iota(jnp.int32, ab.shape, 1)
    g = -jnp.exp(alog) * _softplus(ab + dtb)
    beta = _sigmoid(ab)
    bg = jnp.where(lane < GDN_HEADS, g, jnp.where(lane < 2 * GDN_HEADS, beta, 0.0))
    return jnp.concatenate(qs, axis=1), jnp.concatenate(ks, axis=1), c[:, 2 * GDN_WIDTH:], bg


def _gdn_out_fn(o, gate, og):
    outs = []
    for h in range(GDN_HEADS):
        sl = slice(h * LANE, (h + 1) * LANE)
        outs.append(_rms(o[:, sl], og) * _silu(gate[:, sl]))
    return (jnp.concatenate(outs, axis=1),)


CA_QB = 4 * CHUNK
CA_KB = CA_QB + CA_PAD


def _ca_math(q2, kb2, vb2, bias2, c):
    lane = lax.broadcasted_iota(jnp.int32, q2.shape, 1)
    qc = lax.broadcasted_iota(jnp.int32, (CA_QB, CA_KB), 0) // CHUNK
    kc = lax.broadcasted_iota(jnp.int32, (CA_QB, CA_KB), 1) // CHUNK
    valid = (kc >= qc) & (kc <= qc + CA_LEFT) & (kc + c * (CA_QB // CHUNK) >= CA_LEFT)
    out = jnp.zeros(q2.shape, F32)
    for h in range(2):
        mine = (lane >= h * CA_HEAD_DIM) & (lane < (h + 1) * CA_HEAD_DIM)
        qh = jnp.where(mine, q2, 0.0)
        s = _bdot(qh, kb2, (((1,), (1,)), ((), ()))) * (CA_HEAD_DIM ** -0.5) + bias2[h]
        s = jnp.where(valid, s, -1e30)
        p = jnp.exp(s - jnp.max(s, axis=-1, keepdims=True))
        p = p / jnp.sum(p, axis=-1, keepdims=True)
        out = out + jnp.where(mine, _bdot(p, vb2), 0.0)
    return out


CA_VEC = CA_QB + CA_KB


def _ca_specs(seq):
    q_spec = pl.BlockSpec((CA_QB, LANE), lambda hp, c: (c, hp))
    kv_spec = pl.BlockSpec((seq + CA_PAD, LANE), lambda hp, c: (0, hp))
    b_spec = pl.BlockSpec((1, 2, CA_VEC), lambda hp, c: (hp, 0, 0))
    return (D_MODEL // LANE, seq // CA_QB), q_spec, kv_spec, b_spec


def _ca_bias_from_vector(vec_ref, bias_ref):
    for h in range(2):
        rows = jnp.broadcast_to(vec_ref[0, h:h + 1, :], (CA_QB, CA_VEC))
        bias_ref[h] = pltpu.roll(rows, 0, 1, stride=1, stride_axis=0)[:, CA_QB:]


def _ca_vector_grad(dbias):
    d = jnp.concatenate([jnp.zeros((CA_QB, CA_QB), F32), dbias], axis=1)
    row = lax.broadcasted_iota(jnp.int32, d.shape, 0)
    for bit in range(CA_QB.bit_length() - 1):
        d = jnp.where((row >> bit) & 1 == 1, pltpu.roll(d, CA_VEC - (1 << bit), 1), d)
    return jnp.sum(d, axis=0, keepdims=True)


def _ca_fwd_call(q, kpad, vpad, vec):
    grid, q_spec, kv_spec, b_spec = _ca_specs(q.shape[0])

    def body(q_ref, k_ref, v_ref, vec_ref, o_ref, bias_ref):
        c = pl.program_id(1)
        start = pl.multiple_of(c * CA_QB, CA_QB)

        @pl.when(c == 0)
        def _():
            _ca_bias_from_vector(vec_ref, bias_ref)

        o_ref[...] = _ca_math(q_ref[...], k_ref[pl.ds(start, CA_KB), :], v_ref[pl.ds(start, CA_KB), :],
                              bias_ref[...], c)

    return pl.pallas_call(
        body, grid=grid, in_specs=[q_spec, kv_spec, kv_spec, b_spec], out_specs=q_spec,
        out_shape=jax.ShapeDtypeStruct(q.shape, F32), scratch_shapes=[pltpu.VMEM((2, CA_QB, CA_KB), F32)],
        compiler_params=_cparams(("parallel", "arbitrary")), name="chunkattn_fwd",
    )(q, kpad, vpad, vec)


def _ca_bwd_call(q, kpad, vpad, vec, do):
    grid, q_spec, kv_spec, b_spec = _ca_specs(q.shape[0])
    last = grid[1] - 1

    def body(q_ref, k_ref, v_ref, vec_ref, do_ref, dq_ref, dk_ref, dv_ref, dvec_ref, bias_ref, dbias_ref):
        c = pl.program_id(1)
        start = pl.multiple_of(c * CA_QB, CA_QB)

        @pl.when(c == 0)
        def _():
            _ca_bias_from_vector(vec_ref, bias_ref)
            dk_ref[...] = jnp.zeros_like(dk_ref)
            dv_ref[...] = jnp.zeros_like(dv_ref)
            dbias_ref[...] = jnp.zeros_like(dbias_ref)

        _, pullback = jax.vjp(lambda a, b, d, e: _ca_math(a, b, d, e, c), q_ref[...],
                              k_ref[pl.ds(start, CA_KB), :], v_ref[pl.ds(start, CA_KB), :], bias_ref[...])
        dq, dkb, dvb, dbias = pullback(do_ref[...])
        dq_ref[...] = dq
        dk_ref[pl.ds(start, CA_KB), :] += dkb
        dv_ref[pl.ds(start, CA_KB), :] += dvb
        dbias_ref[...] += dbias

        @pl.when(c == last)
        def _():
            for h in range(2):
                dvec_ref[0, h:h + 1, :] = _ca_vector_grad(dbias_ref[h])

    return pl.pallas_call(
        body, grid=grid, in_specs=[q_spec, kv_spec, kv_spec, b_spec, q_spec],
        out_specs=[q_spec, kv_spec, kv_spec, b_spec],
        out_shape=[jax.ShapeDtypeStruct(q.shape, F32), jax.ShapeDtypeStruct(kpad.shape, F32),
                   jax.ShapeDtypeStruct(vpad.shape, F32), jax.ShapeDtypeStruct(vec.shape, F32)],
        scratch_shapes=[pltpu.VMEM((2, CA_QB, CA_KB), F32), pltpu.VMEM((2, CA_QB, CA_KB), F32)],
        compiler_params=_cparams(("parallel", "arbitrary")), name="chunkattn_bwd",
    )(q, kpad, vpad, vec, do)


@jax.custom_vjp
def chunk_attn_core(q, kpad, vpad, vec):
    return _ca_fwd_call(q, kpad, vpad, vec)


def _ca_core_fwd(q, kpad, vpad, vec):
    return _ca_fwd_call(q, kpad, vpad, vec), (q, kpad, vpad, vec)


def _ca_core_bwd(res, do):
    return tuple(_ca_bwd_call(*res, do))


chunk_attn_core.defvjp(_ca_core_fwd, _ca_core_bwd)


S5_GB = 4
S5_U = S5_WIDTH // S5_GB
S5_L = S5_GROUPS * S5_STATE // S5_GB


def _cmul(ar, ai, br, bi):
    return ar * br - ai * bi, ar * bi + ai * br


def _hdot(a, b, dims=(((1,), (0,)), ((), ()))):
    return lax.dot_general(a, b, dims, precision=HI, preferred_element_type=F32)


_NT = (((1,), (1,)), ((), ()))
_TN = (((0,), (0,)), ((), ()))


def _s5_tables(lr, li, reverse):
    p = {1: (lr, li)}
    p[2] = _cmul(*p[1], *p[1])
    p[4] = _cmul(*p[2], *p[2])
    p[3] = _cmul(*p[2], *p[1])
    p[5] = _cmul(*p[4], *p[1])
    p[6] = _cmul(*p[4], *p[2])
    p[7] = _cmul(*p[4], *p[3])
    p[8] = _cmul(*p[4], *p[4])
    row = lax.broadcasted_iota(jnp.int32, (SUBLANE, lr.shape[1]), 0)
    tr = jnp.zeros(row.shape, F32)
    ti = jnp.zeros(row.shape, F32)
    for i in range(SUBLANE):
        k = SUBLANE - i if reverse else i + 1
        tr = jnp.where(row == i, p[k][0], tr)
        ti = jnp.where(row == i, p[k][1], ti)
    return p, (tr, ti), row


def _s5_block_scan(xr, xi, p, tab, row, hr, hi, reverse):
    for k in (1, 2, 4):
        if reverse:
            sr = jnp.where(row < SUBLANE - k, pltpu.roll(xr, SUBLANE - k, 0), 0.0)
            si = jnp.where(row < SUBLANE - k, pltpu.roll(xi, SUBLANE - k, 0), 0.0)
        else:
            sr = jnp.where(row >= k, pltpu.roll(xr, k, 0), 0.0)
            si = jnp.where(row >= k, pltpu.roll(xi, k, 0), 0.0)
        ar, ai = _cmul(p[k][0], p[k][1], sr, si)
        xr, xi = xr + ar, xi + ai
    cr, ci = _cmul(tab[0], tab[1], hr, hi)
    return xr + cr, xi + ci


def _s5_forward_scan(sr_ref, si_ref, lr, li):
    n_blocks = sr_ref.shape[0] // SUBLANE
    p, tab, row = _s5_tables(lr, li, False)

    def step(b, carry):
        base = pl.multiple_of(b * SUBLANE, SUBLANE)
        xr, xi = _s5_block_scan(sr_ref[pl.ds(base, SUBLANE), :], si_ref[pl.ds(base, SUBLANE), :],
                                p, tab, row, carry[0], carry[1], False)
        sr_ref[pl.ds(base, SUBLANE), :] = xr
        si_ref[pl.ds(base, SUBLANE), :] = xi
        return xr[SUBLANE - 1:SUBLANE, :], xi[SUBLANE - 1:SUBLANE, :]

    zero = jnp.zeros((1, lr.shape[1]), F32)
    lax.fori_loop(0, n_blocks, step, (zero, zero))


def _s5_specs(seq):
    u_spec = pl.BlockSpec((seq, S5_U), lambda g: (0, g))
    bd_spec = pl.BlockSpec((1, S5_U, S5_L), lambda g: (g, 0, 0))
    cd_spec = pl.BlockSpec((1, S5_L, S5_U), lambda g: (g, 0, 0))
    lam_spec = pl.BlockSpec((1, 2, S5_L), lambda g: (g, 0, 0))
    d_spec = pl.BlockSpec((1, S5_U), lambda g: (0, g))
    return u_spec, bd_spec, cd_spec, lam_spec, d_spec


S5_ROWS = 256


def _row_chunks(seq, fn):
    rows_per = min(S5_ROWS, seq)

    def step(r, carry):
        fn(pl.ds(pl.multiple_of(r * rows_per, rows_per), rows_per))
        return carry

    lax.fori_loop(0, seq // rows_per, step, 0)


def _s5_fwd_call(u, bdr, bdi, cdr, cdi, lam, d):
    seq = u.shape[0]
    u_spec, bd_spec, cd_spec, lam_spec, d_spec = _s5_specs(seq)

    def body(u_ref, bdr_ref, bdi_ref, cdr_ref, cdi_ref, lam_ref, d_ref, y_ref, sr_ref, si_ref):
        def project_in(rows):
            uv = u_ref[rows, :]
            sr_ref[rows, :] = _hdot(uv, bdr_ref[0])
            si_ref[rows, :] = _hdot(uv, bdi_ref[0])

        def project_out(rows):
            y_ref[rows, :] = (_hdot(sr_ref[rows, :], cdr_ref[0]) - _hdot(si_ref[rows, :], cdi_ref[0])
                              + d_ref[...] * u_ref[rows, :])

        _row_chunks(seq, project_in)
        _s5_forward_scan(sr_ref, si_ref, lam_ref[0, 0:1, :], lam_ref[0, 1:2, :])
        _row_chunks(seq, project_out)

    return pl.pallas_call(
        body, grid=(S5_GB,), in_specs=[u_spec, bd_spec, bd_spec, cd_spec, cd_spec, lam_spec, d_spec],
        out_specs=u_spec, out_shape=jax.ShapeDtypeStruct(u.shape, F32),
        scratch_shapes=[pltpu.VMEM((seq, S5_L), F32), pltpu.VMEM((seq, S5_L), F32)],
        compiler_params=_cparams(("parallel",)), name="s5_fwd",
    )(u, bdr, bdi, cdr, cdi, lam, d)


def _s5_bwd_call(u, bdr, bdi, cdr, cdi, lam, d, dy):
    seq = u.shape[0]
    n_blocks = seq // SUBLANE
    u_spec, bd_spec, cd_spec, lam_spec, d_spec = _s5_specs(seq)

    def body(u_ref, bdr_ref, bdi_ref, cdr_ref, cdi_ref, lam_ref, d_ref, dy_ref,
             du_ref, dbdr_ref, dbdi_ref, dcdr_ref, dcdi_ref, dlam_ref, dd_ref, sr_ref, si_ref, gr_ref, gi_ref):
        lr, li = lam_ref[0, 0:1, :], lam_ref[0, 1:2, :]

        def project_in(rows):
            uv = u_ref[rows, :]
            dyv = dy_ref[rows, :]
            sr_ref[rows, :] = _hdot(uv, bdr_ref[0])
            si_ref[rows, :] = _hdot(uv, bdi_ref[0])
            gr_ref[rows, :] = _hdot(dyv, cdr_ref[0], _NT)
            gi_ref[rows, :] = -_hdot(dyv, cdi_ref[0], _NT)

        _row_chunks(seq, project_in)
        _s5_forward_scan(sr_ref, si_ref, lr, li)
        p, tab, row = _s5_tables(lr, -li, True)

        def step(i, carry):
            hr, hi, acc_r, acc_i = carry
            b = n_blocks - 1 - i
            base = pl.multiple_of(b * SUBLANE, SUBLANE)
            xr, xi = _s5_block_scan(gr_ref[pl.ds(base, SUBLANE), :], gi_ref[pl.ds(base, SUBLANE), :],
                                    p, tab, row, hr, hi, True)
            gr_ref[pl.ds(base, SUBLANE), :] = xr
            gi_ref[pl.ds(base, SUBLANE), :] = xi
            prev = pl.multiple_of(jnp.maximum(b - 1, 0) * SUBLANE, SUBLANE)
            keep = (b > 0).astype(F32)
            last_r = sr_ref[pl.ds(prev, SUBLANE), :][SUBLANE - 1:SUBLANE, :] * keep
            last_i = si_ref[pl.ds(prev, SUBLANE), :][SUBLANE - 1:SUBLANE, :] * keep
            pr = jnp.where(row >= 1, pltpu.roll(sr_ref[pl.ds(base, SUBLANE), :], 1, 0), last_r)
            pi = jnp.where(row >= 1, pltpu.roll(si_ref[pl.ds(base, SUBLANE), :], 1, 0), last_i)
            acc_r = acc_r + pr * xr + pi * xi
            acc_i = acc_i + pr * xi - pi * xr
            return xr[0:1, :], xi[0:1, :], acc_r, acc_i

        zero = jnp.zeros((1, S5_L), F32)
        zacc = jnp.zeros((SUBLANE, S5_L), F32)
        _, _, acc_r, acc_i = lax.fori_loop(0, n_blocks, step, (zero, zero, zacc, zacc))
        dlam_ref[0, 0:1, :] = jnp.sum(acc_r, axis=0, keepdims=True)
        dlam_ref[0, 1:2, :] = jnp.sum(acc_i, axis=0, keepdims=True)
        for ref in (dbdr_ref, dbdi_ref, dcdr_ref, dcdi_ref, dd_ref):
            ref[...] = jnp.zeros_like(ref)

        def grads(rows):
            uv, dyv = u_ref[rows, :], dy_ref[rows, :]
            grv, giv = gr_ref[rows, :], gi_ref[rows, :]
            du_ref[rows, :] = _hdot(grv, bdr_ref[0], _NT) + _hdot(giv, bdi_ref[0], _NT) + d_ref[...] * dyv
            dbdr_ref[0] += _hdot(uv, grv, _TN)
            dbdi_ref[0] += _hdot(uv, giv, _TN)
            dcdr_ref[0] += _hdot(sr_ref[rows, :], dyv, _TN)
            dcdi_ref[0] -= _hdot(si_ref[rows, :], dyv, _TN)
            dd_ref[...] += jnp.sum(dyv * uv, axis=0, keepdims=True)

        _row_chunks(seq, grads)

    scratch = [pltpu.VMEM((seq, S5_L), F32) for _ in range(4)]
    return pl.pallas_call(
        body, grid=(S5_GB,),
        in_specs=[u_spec, bd_spec, bd_spec, cd_spec, cd_spec, lam_spec, d_spec, u_spec],
        out_specs=[u_spec, bd_spec, bd_spec, cd_spec, cd_spec, lam_spec, d_spec],
        out_shape=[jax.ShapeDtypeStruct(a.shape, F32) for a in (u, bdr, bdi, cdr, cdi, lam, d)],
        scratch_shapes=scratch, compiler_params=_cparams(("parallel",)), name="s5_bwd",
    )(u, bdr, bdi, cdr, cdi, lam, d, dy)


@jax.custom_vjp
def s5_core(u, bdr, bdi, cdr, cdi, lam, d):
    return _s5_fwd_call(u, bdr, bdi, cdr, cdi, lam, d)


def _s5_core_fwd(*args):
    return _s5_fwd_call(*args), args


def _s5_core_bwd(res, dy):
    return tuple(_s5_bwd_call(*res, dy))


s5_core.defvjp(_s5_core_fwd, _s5_core_bwd)


def _s5_discretize(a_re, a_im, log_dt, b_re, b_im, c_re, c_im, d):
    dt = jnp.exp(log_dt)[:, None]
    mag = jnp.exp(a_re * dt)
    lbr, lbi = mag * jnp.cos(a_im * dt), mag * jnp.sin(a_im * dt)
    den = a_re * a_re + a_im * a_im
    fr = ((lbr - 1.0) * a_re + lbi * a_im) / den
    fi = (lbi * a_re - (lbr - 1.0) * a_im) / den
    bbr = fr[:, :, None] * b_re - fi[:, :, None] * b_im
    bbi = fr[:, :, None] * b_im + fi[:, :, None] * b_re
    eye = jnp.eye(S5_GROUPS // S5_GB, dtype=F32)
    gl = S5_GROUPS // S5_GB

    def bd(t):
        return jnp.einsum('bgpc,gh->bgchp', t.reshape(S5_GB, gl, S5_STATE, S5_GROUP), eye).reshape(S5_GB, S5_U, S5_L)

    def cd(t):
        return jnp.einsum('bgcp,gh->bgphc', t.reshape(S5_GB, gl, S5_GROUP, S5_STATE), eye).reshape(S5_GB, S5_L, S5_U)

    lam = jnp.stack([lbr.reshape(S5_GB, S5_L), lbi.reshape(S5_GB, S5_L)], axis=1)
    return bd(bbr), bd(bbi), cd(c_re), cd(c_im), lam, d.reshape(1, S5_WIDTH)


@jax.custom_vjp
def _unit_lower_solve(neg_a, rhs, tinv):
    return _hdot(tinv, rhs)


def _unit_lower_solve_fwd(neg_a, rhs, tinv):
    x = _hdot(tinv, rhs)
    return x, (x, tinv)


def _unit_lower_solve_bwd(res, dx):
    x, tinv = res
    g = _hdot(tinv, dx, _TN)
    return _hdot(g, x, _NT), g, jnp.zeros_like(tinv)


_unit_lower_solve.defvjp(_unit_lower_solve_fwd, _unit_lower_solve_bwd)


def _unit_lower_inverse(neg_a):
    r = lax.broadcasted_iota(jnp.int32, neg_a.shape, 0)
    c = lax.broadcasted_iota(jnp.int32, neg_a.shape, 1)
    p = (r == c).astype(F32) + neg_a
    npow = _hdot(neg_a, neg_a)
    for _ in range(4):
        y = _hdot(jnp.concatenate([p, npow], axis=0), npow)
        p = p + y[:CHUNK]
        npow = y[CHUNK:]
    return p + _hdot(p, npow)


def _gdn_chunk(q, k, v, g_col, b_col, st, tinv=None):
    r = lax.broadcasted_iota(jnp.int32, (CHUNK, CHUNK), 0)
    c = lax.broadcasted_iota(jnp.int32, (CHUNK, CHUNK), 1)
    eye = (r == c).astype(F32)
    strict = r > c
    causal = r >= c
    g_row = jnp.sum(g_col * eye, axis=0, keepdims=True)
    gcum = jnp.sum(jnp.where(causal, g_row, 0.0), axis=1, keepdims=True)
    gcum_row = jnp.sum(gcum * eye, axis=0, keepdims=True)
    diff = gcum - gcum_row
    decay_strict = jnp.where(strict, jnp.exp(jnp.where(strict, diff, 0.0)), 0.0)
    decay_causal = jnp.where(causal, jnp.exp(jnp.where(causal, diff, 0.0)), 0.0)
    gamma = jnp.exp(gcum)
    g_last = jnp.sum(jnp.where(lax.broadcasted_iota(jnp.int32, (CHUNK, 1), 0) == CHUNK - 1, gcum, 0.0),
                     axis=0, keepdims=True)
    kk = _bdot(k, k, _NT)
    neg_a = -(b_col * kk * decay_strict)
    if tinv is None:
        tinv = _unit_lower_inverse(neg_a)
    x = _unit_lower_solve(neg_a, jnp.concatenate([b_col * v, (b_col * gamma) * k], axis=1), lax.stop_gradient(tinv))
    u_new, w_k = x[:, :GDN_HEAD_DIM], x[:, GDN_HEAD_DIM:]
    qk = _bdot(q, k, _NT) * decay_causal
    q_g = q * gamma
    k_tail = k * jnp.exp(g_last - gcum)
    w = u_new - _bdot(w_k, st)
    o = _bdot(q_g, st) + _bdot(qk, w)
    st_new = jnp.exp(g_last) * st + _bdot(k_tail, w, _TN)
    return o, st_new, tinv


def _gdn_cols(bgv, h):
    lane = lax.broadcasted_iota(jnp.int32, bgv.shape, 1)
    g_col = jnp.sum(jnp.where(lane == h, bgv, 0.0), axis=1, keepdims=True)
    b_col = jnp.sum(jnp.where(lane == GDN_HEADS + h, bgv, 0.0), axis=1, keepdims=True)
    return g_col, b_col


def _gdn_fwd_call(q, k, v, bg):
    seq = q.shape[0]
    n_chunks = seq // CHUNK
    x_spec = pl.BlockSpec((CHUNK, GDN_WIDTH), lambda n: (n, 0))
    bg_spec = pl.BlockSpec((CHUNK, LANE), lambda n: (n, 0))
    st_spec = pl.BlockSpec((1, GDN_WIDTH, GDN_HEAD_DIM), lambda n: (n, 0, 0))
    ti_spec = pl.BlockSpec((1, GDN_HEADS * CHUNK, CHUNK), lambda n: (n, 0, 0))

    def body(q_ref, k_ref, v_ref, bg_ref, o_ref, st_out_ref, ti_out_ref, st_ref):
        @pl.when(pl.program_id(0) == 0)
        def _():
            st_ref[...] = jnp.zeros_like(st_ref)

        bgv = bg_ref[...]
        for h in range(GDN_HEADS):
            sl = slice(h * GDN_HEAD_DIM, (h + 1) * GDN_HEAD_DIM)
            g_col, b_col = _gdn_cols(bgv, h)
            st = st_ref[sl, :]
            st_out_ref[0, sl, :] = st
            o, st_new, tinv = _gdn_chunk(q_ref[:, sl], k_ref[:, sl], v_ref[:, sl], g_col, b_col, st)
            o_ref[:, sl] = o
            st_ref[sl, :] = st_new
            ti_out_ref[0, h * CHUNK:(h + 1) * CHUNK, :] = tinv

    return pl.pallas_call(
        body, grid=(n_chunks,), in_specs=[x_spec, x_spec, x_spec, bg_spec], out_specs=[x_spec, st_spec, ti_spec],
        out_shape=[jax.ShapeDtypeStruct(q.shape, F32),
                   jax.ShapeDtypeStruct((n_chunks, GDN_WIDTH, GDN_HEAD_DIM), F32),
                   jax.ShapeDtypeStruct((n_chunks, GDN_HEADS * CHUNK, CHUNK), F32)],
        scratch_shapes=[pltpu.VMEM((GDN_WIDTH, GDN_HEAD_DIM), F32)],
        compiler_params=_cparams(("arbitrary",)), name="gdn_fwd",
    )(q, k, v, bg)


def _gdn_bwd_call(q, k, v, bg, states, tinvs, do):
    seq = q.shape[0]
    n_chunks = seq // CHUNK
    x_spec = pl.BlockSpec((CHUNK, GDN_WIDTH), lambda i: (n_chunks - 1 - i, 0))
    bg_spec = pl.BlockSpec((CHUNK, LANE), lambda i: (n_chunks - 1 - i, 0))
    st_spec = pl.BlockSpec((1, GDN_WIDTH, GDN_HEAD_DIM), lambda i: (n_chunks - 1 - i, 0, 0))
    ti_spec = pl.BlockSpec((1, GDN_HEADS * CHUNK, CHUNK), lambda i: (n_chunks - 1 - i, 0, 0))

    def body(q_ref, k_ref, v_ref, bg_ref, st_in_ref, ti_ref, do_ref, dq_ref, dk_ref, dv_ref, dbg_ref, dst_ref):
        @pl.when(pl.program_id(0) == 0)
        def _():
            dst_ref[...] = jnp.zeros_like(dst_ref)

        bgv = bg_ref[...]
        lane = lax.broadcasted_iota(jnp.int32, bgv.shape, 1)
        dbg = jnp.zeros(bgv.shape, F32)
        for h in range(GDN_HEADS):
            sl = slice(h * GDN_HEAD_DIM, (h + 1) * GDN_HEAD_DIM)
            g_col, b_col = _gdn_cols(bgv, h)
            tinv = ti_ref[0, h * CHUNK:(h + 1) * CHUNK, :]
            _, pullback = jax.vjp(lambda *a: _gdn_chunk(*a, tinv=tinv)[:2], q_ref[:, sl], k_ref[:, sl], v_ref[:, sl],
                                  g_col, b_col, st_in_ref[0, sl, :])
            dq, dk, dv, dg, db, dst = pullback((do_ref[:, sl], dst_ref[sl, :]))
            dq_ref[:, sl] = dq
            dk_ref[:, sl] = dk
            dv_ref[:, sl] = dv
            dst_ref[sl, :] = dst
            dbg = dbg + jnp.where(lane == h, dg, 0.0) + jnp.where(lane == GDN_HEADS + h, db, 0.0)
        dbg_ref[...] = dbg

    return pl.pallas_call(
        body, grid=(n_chunks,), in_specs=[x_spec, x_spec, x_spec, bg_spec, st_spec, ti_spec, x_spec],
        out_specs=[x_spec, x_spec, x_spec, bg_spec],
        out_shape=[jax.ShapeDtypeStruct(q.shape, F32)] * 3 + [jax.ShapeDtypeStruct(bg.shape, F32)],
        scratch_shapes=[pltpu.VMEM((GDN_WIDTH, GDN_HEAD_DIM), F32)],
        compiler_params=_cparams(("arbitrary",)), name="gdn_bwd",
    )(q, k, v, bg, states, tinvs, do)


@jax.custom_vjp
def gdn_core(q, k, v, bg):
    return _gdn_fwd_call(q, k, v, bg)[0]


def _gdn_core_fwd(q, k, v, bg):
    o, states, tinvs = _gdn_fwd_call(q, k, v, bg)
    return o, (q, k, v, bg, states, tinvs)


def _gdn_core_bwd(res, do):
    return tuple(_gdn_bwd_call(*res, do))


gdn_core.defvjp(_gdn_core_fwd, _gdn_core_bwd)


def _row(v):
    return v.reshape(1, -1)


def _lane_pad(v):
    return jnp.pad(v, (0, LANE - v.shape[0])).reshape(1, LANE)


def _delay_rows(x, k):
    return jnp.pad(x, ((k, 0), (0, 0)))[:x.shape[0]]


def s5_mixer(u, a_re, a_im, log_dt, b_re, b_im, c_re, c_im, d, w_glu, b_glu):
    y = s5_core(u, *_s5_discretize(a_re, a_im, log_dt, b_re, b_im, c_re, c_im, d))
    return make_rowop(_s5_post_fn, "s5_post")((y,), (w_glu, _row(b_glu)))[0]


def gated_deltanet(qkv, gate, ab, conv_w, a_log, dt_bias, out_g):
    rows = (qkv, _delay_rows(qkv, 1), _delay_rows(qkv, 2), _delay_rows(qkv, 3), ab)
    q, k, v, bg = make_rowop(_gdn_prep_fn, "gdn_prep")(rows, (conv_w, _lane_pad(a_log), _lane_pad(dt_bias)))
    o = gdn_core(q, k, v, bg)
    return make_rowop(_gdn_out_fn, "gdn_out")((o, gate), (_row(out_g),))[0]


def chunk_attention(h, w_qkv, w_out, q_g, k_g, rel_bias):
    qkv = linear_cols(h, w_qkv)
    qn, kn = make_rowop(_ca_qknorm_fn, "ca_qknorm")((qkv,), (_row(jnp.tile(q_g, 2)), _row(jnp.tile(k_g, 2))))
    kpad = jnp.pad(kn, ((CA_PAD, 0), (0, 0)))
    vpad = jnp.pad(qkv[:, 2 * D_MODEL:], ((CA_PAD, 0), (0, 0)))
    o = chunk_attn_core(qn, kpad, vpad, _rel_bias_vector(rel_bias))
    return linear(o, w_out)


def memory_cross_attention(h, mem_n, w_q, w_kv, w_out, q_g, k_g):
    q = linear(h, w_q)
    kv = linear_cols(mem_n, w_kv)
    o = make_rowop(_xattn_fn, "xattn")((q,), (kv[:, :D_MODEL], kv[:, D_MODEL:], _row(q_g), _row(k_g)))[0]
    return linear(o, w_out)


def swiglu(h, w_gate, w_up, w_down):
    g = linear_cols_dm(h, w_gate)
    u = linear_cols_dm(h, w_up)
    rows = g.shape[0] * g.shape[1]
    a = make_rowop(_swiglu_fn, "swiglu")((g.reshape(rows, -1), u.reshape(rows, -1)), ())[0]
    return linear_rows_dm(a.reshape(g.shape), w_down)


def _rel_bias_vector(rel_bias):
    heads = rel_bias.shape[0]
    n_far = CA_KB - 1 - MAX_REL
    n_neg = CA_VEC - 1 - n_far - (2 * MAX_REL + 1)
    vec = jnp.concatenate([jnp.zeros((heads, 1), F32),
                           jnp.broadcast_to(rel_bias[:, 2 * MAX_REL:], (heads, n_far)),
                           jnp.flip(rel_bias, axis=1),
                           jnp.broadcast_to(rel_bias[:, :1], (heads, n_neg))], axis=1)
    return vec.reshape(heads // 2, 2, CA_VEC)


def _exchange(arrays, modes, name):
    n = len(arrays)
    out_shapes = [jax.ShapeDtypeStruct((N_DEV,) + a.shape if m == "gather" else a.shape, a.dtype)
                  for a, m in zip(arrays, modes)]

    def body(*refs):
        ins, outs = refs[:n], refs[n:2 * n]
        send_sems, recv_sems, local_sems = refs[2 * n:]
        x, y, c = lax.axis_index("x"), lax.axis_index("y"), lax.axis_index("c")
        me = 4 * x + 2 * y + c
        pending = []
        for i in range(n):
            gather = modes[i] == "gather"
            local = pltpu.make_async_copy(ins[i] if gather else ins[i].at[me], outs[i].at[me], local_sems.at[i])
            local.start()
            pending.append(local)
        for k in range(1, N_DEV):
            px, py, pc = (x + (k >> 2)) % 2, (y + ((k >> 1) & 1)) % 2, (c + (k & 1)) % 2
            peer = 4 * px + 2 * py + pc
            for i in range(n):
                src = ins[i] if modes[i] == "gather" else ins[i].at[peer]
                sem = i * (N_DEV - 1) + k - 1
                send = pltpu.make_async_remote_copy(src_ref=src, dst_ref=outs[i].at[me], send_sem=send_sems.at[sem],
                                                    recv_sem=recv_sems.at[sem], device_id=(px, py, pc),
                                                    device_id_type=pl.DeviceIdType.MESH)
                send.start()
                arrival = pltpu.make_async_remote_copy(src_ref=src, dst_ref=outs[i].at[peer],
                                                       send_sem=send_sems.at[sem], recv_sem=recv_sems.at[sem],
                                                       device_id=(px, py, pc), device_id_type=pl.DeviceIdType.MESH)
                pending.append((send, arrival))
        for item in pending:
            if isinstance(item, tuple):
                item[0].wait_send()
                item[1].wait_recv()
            else:
                item.wait()

    any_spec = pl.BlockSpec(memory_space=pl.ANY)
    return pl.pallas_call(
        body, in_specs=[any_spec] * n, out_specs=[any_spec] * n, out_shape=out_shapes,
        scratch_shapes=[pltpu.SemaphoreType.DMA((n * (N_DEV - 1),)), pltpu.SemaphoreType.DMA((n * (N_DEV - 1),)),
                        pltpu.SemaphoreType.DMA((n,))],
        name=name,
    )(*arrays)


_HBM_SPEC = pl.BlockSpec(memory_space=pltpu.HBM)
_SEM_SPEC = pl.BlockSpec(memory_space=pltpu.SEMAPHORE)
_SIDE_EFFECT = pltpu.SideEffectType.DATAFLOW_SIDE_EFFECTING


def _peer(x, y, c, k):
    return (x + (k >> 2)) % 2, (y + ((k >> 1) & 1)) % 2, (c + (k & 1)) % 2


def _exchange_start(arrays, modes, after, name):
    n = len(arrays)
    n_sem = n * (N_DEV - 1)
    lands = [pltpu.with_memory_space_constraint(lax.empty((N_DEV,) + a.shape if m == "gather" else a.shape, a.dtype),
                                                pltpu.HBM) for a, m in zip(arrays, modes)]
    arrays = [pltpu.with_memory_space_constraint(a, pltpu.HBM) for a in arrays]

    def body(*refs):
        ins, zones = refs[:n], refs[n:2 * n]
        send_sems, recv_sems = refs[2 * n + 1], refs[2 * n + 2]
        token_ref, local_sems = refs[2 * n + 3 + 2 * n], refs[2 * n + 4 + 2 * n]
        x, y, c = lax.axis_index("x"), lax.axis_index("y"), lax.axis_index("c")
        me = 4 * x + 2 * y + c
        own = []
        for i in range(n):
            cp = pltpu.make_async_copy(ins[i] if modes[i] == "gather" else ins[i].at[me], zones[i].at[me],
                                       local_sems.at[i])
            cp.start()
            own.append(cp)
        for i in sorted(range(n), key=lambda j: arrays[j].size * arrays[j].dtype.itemsize):
            for k in range(1, N_DEV):
                px, py, pc = _peer(x, y, c, k)
                peer = 4 * px + 2 * py + pc
                sem = i * (N_DEV - 1) + k - 1
                pltpu.make_async_remote_copy(src_ref=ins[i] if modes[i] == "gather" else ins[i].at[peer],
                                             dst_ref=zones[i].at[me], send_sem=send_sems.at[sem],
                                             recv_sem=recv_sems.at[sem], device_id=(px, py, pc),
                                             device_id_type=pl.DeviceIdType.MESH).start()
        for cp in own:
            cp.wait()
        token_ref[...] = jnp.zeros_like(token_ref)

    out_shape = ((pltpu.SemaphoreType.DMA((n_sem,)), pltpu.SemaphoreType.DMA((n_sem,)))
                 + tuple(pltpu.HBM(a.shape, a.dtype) for a in arrays) + tuple(pltpu.HBM(z.shape, z.dtype) for z in lands)
                 + (jax.ShapeDtypeStruct((SUBLANE, LANE), F32),))
    res = pl.pallas_call(
        body, name=name, out_shape=out_shape,
        in_specs=[_HBM_SPEC] * (2 * n) + [pl.BlockSpec(memory_space=pl.ANY)],
        out_specs=(_SEM_SPEC, _SEM_SPEC) + (_HBM_SPEC,) * (2 * n) + (pl.BlockSpec(memory_space=pltpu.VMEM),),
        input_output_aliases={i: 2 + i for i in range(2 * n)},
        scratch_shapes=[pltpu.SemaphoreType.DMA((n,))],
        compiler_params=pltpu.CompilerParams(has_side_effects=_SIDE_EFFECT),
    )(*arrays, *lands, after)
    return res[0], res[1], list(res[2:2 + n]), list(res[2 + n:2 + 2 * n]), res[2 + 2 * n]


def _exchange_wait(started, modes, after, name):
    send_sems, recv_sems, sources, zones, _ = started
    n = len(sources)

    def body(*refs):
        ins, lands = refs[:n], refs[n:2 * n]
        send_ref, recv_ref = refs[2 * n], refs[2 * n + 1]
        x, y, c = lax.axis_index("x"), lax.axis_index("y"), lax.axis_index("c")
        for k in range(1, N_DEV):
            px, py, pc = _peer(x, y, c, k)
            peer = 4 * px + 2 * py + pc
            for i in range(n):
                sem = i * (N_DEV - 1) + k - 1
                cp = pltpu.make_async_remote_copy(src_ref=ins[i] if modes[i] == "gather" else ins[i].at[peer],
                                                  dst_ref=lands[i].at[peer], send_sem=send_ref.at[sem],
                                                  recv_sem=recv_ref.at[sem], device_id=(px, py, pc),
                                                  device_id_type=pl.DeviceIdType.MESH)
                cp.wait_send()
                cp.wait_recv()

    res = pl.pallas_call(
        body, name=name,
        out_shape=tuple(pltpu.HBM(a.shape, a.dtype) for a in sources) + tuple(pltpu.HBM(z.shape, z.dtype) for z in zones),
        in_specs=[_HBM_SPEC] * (2 * n) + [_SEM_SPEC, _SEM_SPEC, pl.BlockSpec(memory_space=pl.ANY)],
        out_specs=(_HBM_SPEC,) * (2 * n), input_output_aliases={i: i for i in range(2 * n)},
        compiler_params=pltpu.CompilerParams(has_side_effects=_SIDE_EFFECT),
    )(*sources, *zones, send_sems, recv_sems, after)
    return list(res[n:])


ADAM_TILE = 64 * 1024


def _adam_call(w, m, v, slots, name):
    n_layers, rows, cols = w.shape
    tr = rows
    if n_layers * rows * cols > ADAM_TILE:
        fits = [t for t in range(SUBLANE, rows, SUBLANE) if rows % t == 0 and n_layers * t * cols <= ADAM_TILE]
        tr = max(fits) if fits else SUBLANE
    c1 = 1.0 - ADAM_B1 ** ADAM_STEP
    c2 = 1.0 - ADAM_B2 ** ADAM_STEP

    def body(*refs):
        w_ref, m_ref, v_ref = refs[:3]
        slot_refs = refs[3:3 + n_layers]
        grad_ref, delta_ref, nm_ref, nv_ref = refs[3 + n_layers:]
        for layer in range(n_layers):
            g = slot_refs[layer][0].astype(F32)
            for k in range(1, N_DEV):
                g = g + slot_refs[layer][k].astype(F32)
            m_new = ADAM_B1 * m_ref[layer] + (1.0 - ADAM_B1) * g
            v_new = ADAM_B2 * v_ref[layer] + (1.0 - ADAM_B2) * (g * g)
            m_hat = m_new / c1
            v_hat = v_new / c2
            grad_ref[layer] = g
            delta_ref[layer] = -ADAM_LR * (m_hat / (jnp.sqrt(v_hat) + ADAM_EPS) + ADAM_WD * w_ref[layer])
            nm_ref[layer] = m_new
            nv_ref[layer] = v_new

    spec = pl.BlockSpec((n_layers, tr, cols), lambda i: (0, i, 0))
    slot_spec = pl.BlockSpec((N_DEV, tr, cols), lambda i: (0, i, 0))
    return pl.pallas_call(
        body, grid=(rows // tr,), in_specs=[spec, spec, spec] + [slot_spec] * n_layers,
        out_specs=[spec] * 4, out_shape=[jax.ShapeDtypeStruct(w.shape, F32)] * 4,
        compiler_params=_cparams(("parallel",)), name=name,
    )(w, m, v, *slots)


WEIGHT_NAMES = ['ab_norm_g', 'ab_w_in', 'ab_w_out', 's5_a_re', 's5_a_im', 's5_log_dt', 's5_b_re', 's5_b_im', 's5_c_re',
                's5_c_im', 's5_d', 's5_w_glu', 's5_b_glu', 'gdn_conv_w', 'gdn_a_log', 'gdn_dt_bias', 'gdn_out_norm_g',
                'c_norm_g', 'c_w_qkv', 'c_w_out', 'c_q_norm_g', 'c_k_norm_g', 'c_rel_bias', 'mem_norm_g', 'xa_norm_g',
                'xa_w_q', 'xa_w_kv', 'xa_w_out', 'xa_q_norm_g', 'xa_k_norm_g', 'f_norm_g', 'f_w_gate', 'f_w_up',
                'f_w_down']

SHARDED = {
    'ab_w_in': ('col', BF16), 'ab_w_out': ('row', BF16), 's5_w_glu': ('row', BF16), 'gdn_conv_w': ('col', F32),
    'c_norm_g': ('col', F32), 'c_w_qkv': ('col', BF16), 'c_w_out': ('row', BF16), 'xa_w_q': ('row', BF16),
    'xa_w_kv': ('col', BF16), 'xa_w_out': ('row', BF16), 'f_w_gate': ('col', BF16), 'f_w_up': ('col', BF16),
    'f_w_down': ('row', BF16),
}
GATHERED_AS_IS = ('c_w_qkv', 'xa_w_kv', 'f_w_gate', 'f_w_up', 'f_w_down')
REPLICATED = [n for n in WEIGHT_NAMES if n not in SHARDED]
PACK_UNIT = SUBLANE * LANE


def _full_from_gathered(g, axis):
    if axis == "row":
        return g.reshape(g.shape[0] * g.shape[1], g.shape[2])
    return jnp.transpose(g, (1, 0, 2)).reshape(g.shape[1], g.shape[0] * g.shape[2])


def _pack(arrays):
    flat = []
    for a in arrays:
        size = a.size
        padded = -(-size // PACK_UNIT) * PACK_UNIT
        flat.append(jnp.pad(a.reshape(-1), (0, padded - size)).reshape(-1, LANE))
    return jnp.concatenate(flat, axis=0)


def _unpack(buf, shapes):
    out, row = [], 0
    for shape in shapes:
        size = math.prod(shape)
        rows = -(-size // PACK_UNIT) * SUBLANE
        out.append(buf[row:row + rows].reshape(-1)[:size].reshape(shape))
        row += rows
    return out


EVEN_SHARDED = ['ab_w_in', 'ab_w_out', 's5_w_glu', 'gdn_conv_w']
ODD_SHARDED = ['c_norm_g', 'c_w_qkv', 'c_w_out']
ALL_SHARDED = ['xa_w_q', 'xa_w_kv', 'xa_w_out', 'f_w_gate', 'f_w_up', 'f_w_down']
EVEN_SMALL = ['ab_norm_g', 's5_a_re', 's5_a_im', 's5_log_dt', 's5_b_re', 's5_b_im', 's5_c_re', 's5_c_im', 's5_d',
              's5_b_glu', 'gdn_a_log', 'gdn_dt_bias', 'gdn_out_norm_g']
ODD_SMALL = ['c_q_norm_g', 'c_k_norm_g', 'c_rel_bias']
ALL_SMALL = ['xa_norm_g', 'xa_q_norm_g', 'xa_k_norm_g', 'f_norm_g']


def _layer_params(layer):
    mixer_big, mixer_small = (EVEN_SHARDED, EVEN_SMALL) if layer % 2 == 0 else (ODD_SHARDED, ODD_SMALL)
    big = [(n, layer // 2) for n in mixer_big] + [(n, layer) for n in ALL_SHARDED]
    small = [(n, layer // 2) for n in mixer_small] + [(n, layer) for n in ALL_SMALL]
    return big, small


def _layer_forward(layer, landed, small, x, mem_n):
    big = {}
    for (n, _), g in zip(_layer_params(layer)[0], landed):
        if n == 'ab_w_in':
            big[n] = cols_to_natural(g, AB_IN_PAD)
        elif n in GATHERED_AS_IS:
            big[n] = g
        elif n == 's5_w_glu':
            big[n] = _full_from_gathered(g, 'row').astype(F32)
        else:
            big[n] = _full_from_gathered(g, SHARDED[n][0])
    if layer % 2 == 0:
        h = rmsnorm(x, small['ab_norm_g'], "ab_norm")
        proj = linear(h, big['ab_w_in'])
        u = proj[:, :S5_WIDTH]
        qkv = proj[:, S5_WIDTH:S5_WIDTH + 3 * GDN_WIDTH]
        gate = proj[:, S5_WIDTH + 3 * GDN_WIDTH:S5_WIDTH + 4 * GDN_WIDTH]
        ab = proj[:, S5_WIDTH + 4 * GDN_WIDTH:]
        a_out = s5_mixer(u, small['s5_a_re'], small['s5_a_im'], small['s5_log_dt'], small['s5_b_re'], small['s5_b_im'],
                         small['s5_c_re'], small['s5_c_im'], small['s5_d'], big['s5_w_glu'], small['s5_b_glu'])
        b_out = gated_deltanet(qkv, gate, ab, big['gdn_conv_w'], small['gdn_a_log'], small['gdn_dt_bias'],
                               small['gdn_out_norm_g'])
        x = x + linear(jnp.concatenate([a_out, b_out], axis=1), big['ab_w_out'])
    else:
        h = rmsnorm(x, big['c_norm_g'].reshape(-1), "c_norm")
        x = x + chunk_attention(h, big['c_w_qkv'], big['c_w_out'], small['c_q_norm_g'], small['c_k_norm_g'],
                                small['c_rel_bias'])
    h = rmsnorm(x, small['xa_norm_g'], "xa_norm")
    x = x + memory_cross_attention(h, mem_n, big['xa_w_q'], big['xa_w_kv'], big['xa_w_out'], small['xa_q_norm_g'],
                                   small['xa_k_norm_g'])
    h = rmsnorm(x, small['f_norm_g'], "f_norm")
    return x + swiglu(h, big['f_w_gate'], big['f_w_up'], big['f_w_down'])


def _loss_rows(x, target):
    return jnp.sum(make_rowop(_loss_fn, "loss")((x, target), ())[0])


def kernel(x, mem, ab_norm_g, ab_w_in, ab_w_out, s5_a_re, s5_a_im, s5_log_dt, s5_b_re, s5_b_im, s5_c_re, s5_c_im, s5_d, s5_w_glu, s5_b_glu, gdn_conv_w, gdn_a_log, gdn_dt_bias, gdn_out_norm_g, c_norm_g, c_w_qkv, c_w_out, c_q_norm_g, c_k_norm_g, c_rel_bias, mem_norm_g, xa_norm_g, xa_w_q, xa_w_kv, xa_w_out, xa_q_norm_g, xa_k_norm_g, f_norm_g, f_w_gate, f_w_up, f_w_down, loss_target, m_ab_norm_g, m_ab_w_in, m_ab_w_out, m_s5_a_re, m_s5_a_im, m_s5_log_dt, m_s5_b_re, m_s5_b_im, m_s5_c_re, m_s5_c_im, m_s5_d, m_s5_w_glu, m_s5_b_glu, m_gdn_conv_w, m_gdn_a_log, m_gdn_dt_bias, m_gdn_out_norm_g, m_c_norm_g, m_c_w_qkv, m_c_w_out, m_c_q_norm_g, m_c_k_norm_g, m_c_rel_bias, m_mem_norm_g, m_xa_norm_g, m_xa_w_q, m_xa_w_kv, m_xa_w_out, m_xa_q_norm_g, m_xa_k_norm_g, m_f_norm_g, m_f_w_gate, m_f_w_up, m_f_w_down, v_ab_norm_g, v_ab_w_in, v_ab_w_out, v_s5_a_re, v_s5_a_im, v_s5_log_dt, v_s5_b_re, v_s5_b_im, v_s5_c_re, v_s5_c_im, v_s5_d, v_s5_w_glu, v_s5_b_glu, v_gdn_conv_w, v_gdn_a_log, v_gdn_dt_bias, v_gdn_out_norm_g, v_c_norm_g, v_c_w_qkv, v_c_w_out, v_c_q_norm_g, v_c_k_norm_g, v_c_rel_bias, v_mem_norm_g, v_xa_norm_g, v_xa_w_q, v_xa_w_kv, v_xa_w_out, v_xa_q_norm_g, v_xa_k_norm_g, v_f_norm_g, v_f_w_gate, v_f_w_up, v_f_w_down):
    given = dict(locals())
    no_after = jnp.zeros((SUBLANE, LANE), F32)

    def shard(n, idx):
        a = given[n][idx]
        return (a.reshape(1, -1) if a.ndim == 1 else a).astype(SHARDED[n][1])

    def gather_start(layer, after):
        arrays = [shard(n, idx) for n, idx in _layer_params(layer)[0]]
        return _exchange_start(arrays, ["gather"] * len(arrays), after, "gather_start_%d" % layer)

    act = x[0]
    mem_n, mem_pullback = jax.vjp(lambda m, g: rmsnorm(m, g, "mem_norm"), mem[0], mem_norm_g)
    started = gather_start(0, no_after)
    pullbacks = []
    for layer in range(DEPTH):
        landed = _exchange_wait(started, ["gather"] * len(started[2]), act, "gather_wait_%d" % layer)
        if layer + 1 < DEPTH:
            started = gather_start(layer + 1, landed[0])
            act = act + started[4][0, 0]
        small = {n: given[n][idx] for n, idx in _layer_params(layer)[1]}
        act, pullback = jax.vjp(functools.partial(_layer_forward, layer), landed, small, act, mem_n)
        pullbacks.append(pullback)
    loss_local, loss_pullback = jax.vjp(_loss_rows, act, loss_target[0])
    d_act = loss_pullback(jnp.ones((), F32))[0]

    d_mem_n = jnp.zeros_like(mem_n)
    g_small = {}
    received = [None] * DEPTH
    started, after = None, no_after
    for layer in reversed(range(DEPTH)):
        d_landed, d_small, d_act, d_mem = pullbacks[layer](d_act)
        d_mem_n = d_mem_n + d_mem
        for n, idx in _layer_params(layer)[1]:
            g_small[(n, idx)] = d_small[n]
        if started is not None:
            received[layer + 1] = _exchange_wait(started, ["scatter"] * len(started[2]), d_act,
                                                 "scatter_wait_%d" % (layer + 1))
            after = received[layer + 1][0]
        started = _exchange_start(list(d_landed), ["scatter"] * len(d_landed), after, "scatter_start_%d" % layer)
        d_act = d_act + started[4][0, 0]
    g_small[('mem_norm_g', None)] = mem_pullback(d_mem_n)[1]

    def small_grad(n):
        if n == 'mem_norm_g':
            return g_small[(n, None)]
        return jnp.stack([g_small[(n, i)] for i in range(given[n].shape[0])], axis=0)

    packed = _exchange([_pack([small_grad(n) for n in REPLICATED])], ["gather"], "small_grads_allgather")[0]
    received[0] = _exchange_wait(started, ["scatter"] * len(started[2]), packed, "scatter_wait_0")

    results = {}
    for n in SHARDED:
        slots = {}
        for layer in range(DEPTH):
            for (pn, idx), r in zip(_layer_params(layer)[0], received[layer]):
                if pn == n:
                    slots[idx] = r
        shape = given[n].shape
        to3d = lambda a: a.reshape(a.shape[0], -1, a.shape[-1])
        outs = _adam_call(to3d(given[n]), to3d(given['m_' + n]), to3d(given['v_' + n]),
                          [slots[i] for i in range(len(slots))], "adamw_" + n)
        results[n] = [o.reshape(shape) for o in outs]
    outs = _adam_call(_pack([given[n] for n in REPLICATED])[None], _pack([given['m_' + n] for n in REPLICATED])[None],
                      _pack([given['v_' + n] for n in REPLICATED])[None], [packed], "adamw_replicated")
    shapes = [given[n].shape for n in REPLICATED]
    for j, parts in enumerate(zip(*[_unpack(o[0], shapes) for o in outs])):
        results[REPLICATED[j]] = list(parts)

    loss = lax.psum(loss_local, ("x", "y", "c"))
    return (loss, d_act[None], *[results[n][0] for n in WEIGHT_NAMES], *[results[n][1] for n in WEIGHT_NAMES],
            *[results[n][2] for n in WEIGHT_NAMES], *[results[n][3] for n in WEIGHT_NAMES])
```

```python
import functools
import math

import jax
import jax.numpy as jnp
import numpy as np
from jax import lax
from jax.experimental import pallas as pl
from jax.experimental.pallas import tpu as pltpu

F32 = jnp.float32
BF16 = jnp.bfloat16
HI = lax.Precision.HIGHEST

N_DEV = 8
D_MODEL = 1024
SEQ = 2048
DEPTH = 4
CHUNK = 64
N_MEM = 256
RMS_EPS = 1e-6
S5_WIDTH = 512
S5_GROUP = 16
S5_GROUPS = 32
S5_STATE = 64
GDN_HEAD_DIM = 128
GDN_WIDTH = 512
GDN_HEADS = 4
GDN_CONV = 4
AB_IN = S5_WIDTH + 4 * GDN_WIDTH + 2 * GDN_HEADS
AB_IN_PAD = 2688
CA_HEADS = 16
CA_HEAD_DIM = 64
CA_LEFT = 8
CA_BAND = (CA_LEFT + 1) * CHUNK
CA_PAD = CA_LEFT * CHUNK
MAX_REL = 128
XA_HEADS = 4
XA_HEAD_DIM = 256
FFN = 2816
ADAM_LR, ADAM_B1, ADAM_B2, ADAM_EPS, ADAM_WD, ADAM_STEP = 0.001, 0.9, 0.999, 1e-08, 0.01, 10

VMEM_LIMIT = 48 * 1024 * 1024
LANE = 128
SUBLANE = 8


def _cparams(sem=None):
    return pltpu.CompilerParams(dimension_semantics=sem, vmem_limit_bytes=VMEM_LIMIT)


def _divisor_tile(n, target, unit=LANE):
    if n <= target:
        return n
    best = None
    for t in range(unit, target + 1, unit):
        if n % t == 0:
            best = t
    assert best is not None, (n, target)
    return best


def _matmul(a, b, *, ta=False, tb=False, out_dtype=F32, name="mm"):
    if ta:
        k_dim, m_dim = a.shape
    else:
        m_dim, k_dim = a.shape
    if tb:
        n_dim, kb = b.shape
    else:
        kb, n_dim = b.shape
    assert kb == k_dim, (a.shape, b.shape, ta, tb)
    tm = _divisor_tile(m_dim, 512)
    tn = _divisor_tile(n_dim, 512)
    tk = _divisor_tile(k_dim, 1408)
    nk = k_dim // tk
    dims = (((0 if ta else 1,), (1 if tb else 0,)), ((), ()))

    def body(a_ref, b_ref, o_ref, acc_ref):
        k = pl.program_id(2)

        @pl.when(k == 0)
        def _():
            acc_ref[...] = jnp.zeros_like(acc_ref)

        acc_ref[...] += lax.dot_general(a_ref[...].astype(BF16), b_ref[...].astype(BF16), dims,
                                        preferred_element_type=F32)

        @pl.when(k == nk - 1)
        def _():
            o_ref[...] = acc_ref[...].astype(o_ref.dtype)

    a_spec = pl.BlockSpec((tk, tm), lambda i, j, k: (k, i)) if ta else pl.BlockSpec((tm, tk), lambda i, j, k: (i, k))
    b_spec = pl.BlockSpec((tn, tk), lambda i, j, k: (j, k)) if tb else pl.BlockSpec((tk, tn), lambda i, j, k: (k, j))
    return pl.pallas_call(
        body,
        grid=(m_dim // tm, n_dim // tn, nk),
        in_specs=[a_spec, b_spec],
        out_specs=pl.BlockSpec((tm, tn), lambda i, j, k: (i, j)),
        out_shape=jax.ShapeDtypeStruct((m_dim, n_dim), out_dtype),
        scratch_shapes=[pltpu.VMEM((tm, tn), F32)],
        compiler_params=_cparams(("parallel", "parallel", "arbitrary")),
        name=name,
    )(a, b)


@jax.custom_vjp
def linear(a, w):
    return _matmul(a, w, name="linear_fwd")


def _linear_fwd(a, w):
    return _matmul(a, w, name="linear_fwd"), (a, w)


def _linear_bwd(res, dy):
    a, w = res
    da = _matmul(dy, w, tb=True, name="linear_da")
    dw = _matmul(a, dy, ta=True, out_dtype=w.dtype, name="linear_dw")
    return da, dw


linear.defvjp(_linear_fwd, _linear_bwd)


def _mm_call(name, a, b, out_struct, grid, a_spec, b_spec, o_spec, dims, lead):
    nk = grid[-1]
    acc_shape = o_spec.block_shape[1:] if lead[2] else o_spec.block_shape

    def body(a_ref, b_ref, o_ref, acc_ref):
        k = pl.program_id(len(grid) - 1)

        @pl.when(k == 0)
        def _():
            acc_ref[...] = jnp.zeros_like(acc_ref)

        av = a_ref[0] if lead[0] else a_ref[...]
        bv = b_ref[0] if lead[1] else b_ref[...]
        acc_ref[...] += lax.dot_general(av.astype(BF16), bv.astype(BF16), dims, preferred_element_type=F32)

        @pl.when(k == nk - 1)
        def _():
            if lead[2]:
                o_ref[0] = acc_ref[...].astype(o_ref.dtype)
            else:
                o_ref[...] = acc_ref[...].astype(o_ref.dtype)

    return pl.pallas_call(
        body, grid=grid, in_specs=[a_spec, b_spec], out_specs=o_spec, out_shape=out_struct,
        scratch_shapes=[pltpu.VMEM(tuple(acc_shape), F32)],
        compiler_params=_cparams(("parallel", "parallel", "arbitrary")), name=name,
    )(a, b)


_NN = (((1,), (0,)), ((), ()))
_NT_DIMS = (((1,), (1,)), ((), ()))
_TN_DIMS = (((0,), (0,)), ((), ()))


def _cols_fwd(a, g, dm_out):
    m_dim, k_dim = a.shape
    _, _, c_dim = g.shape
    tm = _divisor_tile(m_dim, 512)
    tk = _divisor_tile(k_dim, 1024)
    a_spec = pl.BlockSpec((tm, tk), lambda j, i, k: (i, k))
    b_spec = pl.BlockSpec((1, tk, c_dim), lambda j, i, k: (j, k, 0))
    if dm_out:
        o_spec = pl.BlockSpec((1, tm, c_dim), lambda j, i, k: (j, i, 0))
        out = jax.ShapeDtypeStruct((N_DEV, m_dim, c_dim), F32)
    else:
        o_spec = pl.BlockSpec((tm, c_dim), lambda j, i, k: (i, j))
        out = jax.ShapeDtypeStruct((m_dim, N_DEV * c_dim), F32)
    return _mm_call("cols_fwd", a, g, out, (N_DEV, m_dim // tm, k_dim // tk), a_spec, b_spec, o_spec, _NN,
                    (False, True, dm_out))


def _cols_da(dy, g, dm_out):
    _, k_dim, c_dim = g.shape
    m_dim = dy.shape[1] if dm_out else dy.shape[0]
    tm = _divisor_tile(m_dim, 512)
    tk = _divisor_tile(k_dim, 512)
    if dm_out:
        a_spec = pl.BlockSpec((1, tm, c_dim), lambda i, kb, j: (j, i, 0))
    else:
        a_spec = pl.BlockSpec((tm, c_dim), lambda i, kb, j: (i, j))
    b_spec = pl.BlockSpec((1, tk, c_dim), lambda i, kb, j: (j, kb, 0))
    o_spec = pl.BlockSpec((tm, tk), lambda i, kb, j: (i, kb))
    return _mm_call("cols_da", dy, g, jax.ShapeDtypeStruct((m_dim, k_dim), F32), (m_dim // tm, k_dim // tk, N_DEV),
                    a_spec, b_spec, o_spec, _NT_DIMS, (dm_out, True, False))


def _cols_dg(a, dy, g, dm_out):
    _, k_dim, c_dim = g.shape
    m_dim = a.shape[0]
    tm = _divisor_tile(m_dim, 1024)
    tk = _divisor_tile(k_dim, 512)
    a_spec = pl.BlockSpec((tm, tk), lambda j, kb, m: (m, kb))
    if dm_out:
        b_spec = pl.BlockSpec((1, tm, c_dim), lambda j, kb, m: (j, m, 0))
    else:
        b_spec = pl.BlockSpec((tm, c_dim), lambda j, kb, m: (m, j))
    o_spec = pl.BlockSpec((1, tk, c_dim), lambda j, kb, m: (j, kb, 0))
    return _mm_call("cols_dg", a, dy, jax.ShapeDtypeStruct(g.shape, g.dtype), (N_DEV, k_dim // tk, m_dim // tm),
                    a_spec, b_spec, o_spec, _TN_DIMS, (False, dm_out, True))


def _make_linear_cols(dm_out):
    @jax.custom_vjp
    def op(a, g):
        return _cols_fwd(a, g, dm_out)

    def fwd(a, g):
        return _cols_fwd(a, g, dm_out), (a, g)

    def bwd(res, dy):
        a, g = res
        return _cols_da(dy, g, dm_out), _cols_dg(a, dy, g, dm_out)

    op.defvjp(fwd, bwd)
    return op


linear_cols = _make_linear_cols(False)
linear_cols_dm = _make_linear_cols(True)


def _rows_fwd(a, g):
    _, m_dim, r_dim = a.shape
    n_dim = g.shape[2]
    tm = _divisor_tile(m_dim, 512)
    tn = _divisor_tile(n_dim, 512)
    a_spec = pl.BlockSpec((1, tm, r_dim), lambda i, n, j: (j, i, 0))
    b_spec = pl.BlockSpec((1, r_dim, tn), lambda i, n, j: (j, 0, n))
    o_spec = pl.BlockSpec((tm, tn), lambda i, n, j: (i, n))
    return _mm_call("rows_fwd", a, g, jax.ShapeDtypeStruct((m_dim, n_dim), F32), (m_dim // tm, n_dim // tn, N_DEV),
                    a_spec, b_spec, o_spec, _NN, (True, True, False))


def _rows_da(dy, g):
    m_dim, n_dim = dy.shape
    r_dim = g.shape[1]
    tm = _divisor_tile(m_dim, 512)
    tn = _divisor_tile(n_dim, 1024)
    a_spec = pl.BlockSpec((tm, tn), lambda j, i, n: (i, n))
    b_spec = pl.BlockSpec((1, r_dim, tn), lambda j, i, n: (j, 0, n))
    o_spec = pl.BlockSpec((1, tm, r_dim), lambda j, i, n: (j, i, 0))
    return _mm_call("rows_da", dy, g, jax.ShapeDtypeStruct((N_DEV, m_dim, r_dim), F32),
                    (N_DEV, m_dim // tm, n_dim // tn), a_spec, b_spec, o_spec, _NT_DIMS, (False, True, True))


def _rows_dg(a, dy, g):
    _, m_dim, r_dim = a.shape
    n_dim = dy.shape[1]
    tm = _divisor_tile(m_dim, 1024)
    tn = _divisor_tile(n_dim, 512)
    a_spec = pl.BlockSpec((1, tm, r_dim), lambda j, n, m: (j, m, 0))
    b_spec = pl.BlockSpec((tm, tn), lambda j, n, m: (m, n))
    o_spec = pl.BlockSpec((1, r_dim, tn), lambda j, n, m: (j, 0, n))
    return _mm_call("rows_dg", a, dy, jax.ShapeDtypeStruct(g.shape, g.dtype), (N_DEV, n_dim // tn, m_dim // tm),
                    a_spec, b_spec, o_spec, _TN_DIMS, (True, False, True))


@jax.custom_vjp
def linear_rows_dm(a, g):
    return _rows_fwd(a, g)


def _linear_rows_dm_fwd(a, g):
    return _rows_fwd(a, g), (a, g)


def _linear_rows_dm_bwd(res, dy):
    a, g = res
    return _rows_da(dy, g), _rows_dg(a, dy, g)


linear_rows_dm.defvjp(_linear_rows_dm_fwd, _linear_rows_dm_bwd)


def _cols_to_natural_call(g, width):
    _, k_dim, c_dim = g.shape
    tk = _divisor_tile(k_dim, 256, SUBLANE)

    def body(g_ref, o_ref):
        for j in range(N_DEV):
            o_ref[:, j * c_dim:(j + 1) * c_dim] = g_ref[j]
        if width > N_DEV * c_dim:
            o_ref[:, N_DEV * c_dim:] = jnp.zeros((tk, width - N_DEV * c_dim), o_ref.dtype)

    return pl.pallas_call(
        body, grid=(k_dim // tk,), in_specs=[pl.BlockSpec((N_DEV, tk, c_dim), lambda i: (0, i, 0))],
        out_specs=pl.BlockSpec((tk, width), lambda i: (i, 0)), out_shape=jax.ShapeDtypeStruct((k_dim, width), g.dtype),
        compiler_params=_cparams(("parallel",)), name="cols_to_natural",
    )(g)


def _natural_to_cols_call(w, c_dim):
    k_dim, width = w.shape
    tk = _divisor_tile(k_dim, 256, SUBLANE)

    def body(w_ref, o_ref):
        for j in range(N_DEV):
            o_ref[j] = w_ref[:, j * c_dim:(j + 1) * c_dim]

    return pl.pallas_call(
        body, grid=(k_dim // tk,), in_specs=[pl.BlockSpec((tk, width), lambda i: (i, 0))],
        out_specs=pl.BlockSpec((N_DEV, tk, c_dim), lambda i: (0, i, 0)),
        out_shape=jax.ShapeDtypeStruct((N_DEV, k_dim, c_dim), w.dtype),
        compiler_params=_cparams(("parallel",)), name="natural_to_cols",
    )(w)


@functools.partial(jax.custom_vjp, nondiff_argnums=(1,))
def cols_to_natural(g, width):
    return _cols_to_natural_call(g, width)


def _cols_to_natural_fwd(g, width):
    return _cols_to_natural_call(g, width), g.shape[2]


def _cols_to_natural_bwd(width, c_dim, dw):
    return (_natural_to_cols_call(dw, c_dim),)


cols_to_natural.defvjp(_cols_to_natural_fwd, _cols_to_natural_bwd)


def make_rowop(fn, name, tm=256):
    def specs(rows, params):
        row_specs = [pl.BlockSpec((tm, r.shape[1]), lambda i: (i, 0)) for r in rows]
        par_specs = [pl.BlockSpec(p.shape, lambda i: (0, 0)) for p in params]
        return row_specs, par_specs

    def out_structs(rows, params):
        tiles = [jax.ShapeDtypeStruct((tm, r.shape[1]), r.dtype) for r in rows]
        return jax.eval_shape(lambda r, p: fn(*r, *p), tiles, list(params))

    def fwd_call(rows, params):
        m_dim = rows[0].shape[0]
        n_in = len(rows) + len(params)
        outs = out_structs(rows, params)

        def body(*refs):
            res = fn(*[r[...] for r in refs[:n_in]])
            for o_ref, r in zip(refs[n_in:], res):
                o_ref[...] = r.astype(o_ref.dtype)

        row_specs, par_specs = specs(rows, params)
        return pl.pallas_call(
            body,
            grid=(m_dim // tm,),
            in_specs=row_specs + par_specs,
            out_specs=[pl.BlockSpec((tm, o.shape[1]), lambda i: (i, 0)) for o in outs],
            out_shape=[jax.ShapeDtypeStruct((m_dim, o.shape[1]), o.dtype) for o in outs],
            compiler_params=_cparams(("parallel",)),
            name=name + "_fwd",
        )(*rows, *params)

    def bwd_call(rows, params, cts):
        m_dim = rows[0].shape[0]
        n_rows, n_par = len(rows), len(params)
        n_in = n_rows + n_par
        n_ct = len(cts)

        def body(*refs):
            vals = [r[...] for r in refs[:n_in]]
            ct_vals = tuple(r[...] for r in refs[n_in:n_in + n_ct])
            drow_refs = refs[n_in + n_ct:n_in + n_ct + n_rows]
            dpar_refs = refs[n_in + n_ct + n_rows:]
            _, pullback = jax.vjp(fn, *vals)
            grads = pullback(ct_vals)
            for d_ref, g in zip(drow_refs, grads[:n_rows]):
                d_ref[...] = g

            @pl.when(pl.program_id(0) == 0)
            def _():
                for d_ref in dpar_refs:
                    d_ref[...] = jnp.zeros_like(d_ref)

            for d_ref, g in zip(dpar_refs, grads[n_rows:]):
                d_ref[...] += g

        row_specs, par_specs = specs(rows, params)
        ct_specs = [pl.BlockSpec((tm, c.shape[1]), lambda i: (i, 0)) for c in cts]
        res = pl.pallas_call(
            body,
            grid=(m_dim // tm,),
            in_specs=row_specs + par_specs + ct_specs,
            out_specs=row_specs + par_specs,
            out_shape=[jax.ShapeDtypeStruct(r.shape, r.dtype) for r in rows]
            + [jax.ShapeDtypeStruct(p.shape, p.dtype) for p in params],
            compiler_params=_cparams(("arbitrary",)),
            name=name + "_bwd",
        )(*rows, *params, *cts)
        return tuple(res[:n_rows]), tuple(res[n_rows:])

    @jax.custom_vjp
    def op(rows, params):
        return tuple(fwd_call(rows, params))

    def op_fwd(rows, params):
        return tuple(fwd_call(rows, params)), (rows, params)

    def op_bwd(res, cts):
        rows, params = res
        return bwd_call(rows, params, tuple(cts))

    op.defvjp(op_fwd, op_bwd)
    return op


def _rms(x, g):
    return x * lax.rsqrt(jnp.mean(x * x, axis=-1, keepdims=True) + RMS_EPS) * g


def _sigmoid(x):
    return 1.0 / (1.0 + jnp.exp(-x))


def _silu(x):
    return x * _sigmoid(x)


def _bdot(a, b, dims=(((1,), (0,)), ((), ()))):
    return lax.dot_general(a.astype(BF16), b.astype(BF16), dims, preferred_element_type=F32)


def _rmsnorm_fn(x, g):
    return (_rms(x, g),)


def rmsnorm(x, g, name):
    return make_rowop(_rmsnorm_fn, name)((x,), (g.reshape(1, -1),))[0]


def _gelu_tanh(x):
    return 0.5 * x * (1.0 + jnp.tanh(0.7978845608028654 * (x + 0.044715 * x * x * x)))


def _softplus(x):
    return jnp.maximum(x, 0.0) + jnp.log(1.0 + jnp.exp(-jnp.abs(x)))


def _s5_post_fn(y, w_glu, b_glu):
    h = _gelu_tanh(y)
    return (h * _sigmoid(_bdot(h, w_glu) + b_glu),)


def _swiglu_fn(g, u):
    return (_silu(g) * u,)


def _loss_fn(y, t):
    err = y - t
    return (0.5 * jnp.mean(err * err, axis=-1, keepdims=True),)


def _pair_headnorm(x, g2):
    lo = lax.broadcasted_iota(jnp.int32, x.shape, 1) < CA_HEAD_DIM
    sq = x * x
    s_lo = jnp.sum(jnp.where(lo, sq, 0.0), axis=-1, keepdims=True)
    s_hi = jnp.sum(jnp.where(lo, 0.0, sq), axis=-1, keepdims=True)
    ms = jnp.where(lo, s_lo, s_hi) * (1.0 / CA_HEAD_DIM)
    return x * lax.rsqrt(ms + RMS_EPS) * g2


def _ca_qknorm_fn(qkv, qg2, kg2):
    qs, ks = [], []
    for j in range(D_MODEL // LANE):
        qs.append(_pair_headnorm(qkv[:, j * LANE:(j + 1) * LANE], qg2))
        ks.append(_pair_headnorm(qkv[:, D_MODEL + j * LANE:D_MODEL + (j + 1) * LANE], kg2))
    return jnp.concatenate(qs, axis=1), jnp.concatenate(ks, axis=1)


def _xattn_fn(q, k, v, qg, kg):
    outs = []
    for h in range(XA_HEADS):
        sl = slice(h * XA_HEAD_DIM, (h + 1) * XA_HEAD_DIM)
        qh = _rms(q[:, sl], qg)
        kh = _rms(k[:, sl], kg)
        s = _bdot(qh, kh, (((1,), (1,)), ((), ()))) * (XA_HEAD_DIM ** -0.5)
        p = jnp.exp(s - jnp.max(s, axis=-1, keepdims=True))
        p = p / jnp.sum(p, axis=-1, keepdims=True)
        outs.append(_bdot(p, v[:, sl]))
    return (jnp.concatenate(outs, axis=1),)


def _gdn_prep_fn(x0, x1, x2, x3, ab, conv_w, alog, dtb):
    c = conv_w[3:4, :] * x0 + conv_w[2:3, :] * x1 + conv_w[1:2, :] * x2 + conv_w[0:1, :] * x3
    c = _silu(c)
    qs, ks = [], []
    for h in range(GDN_HEADS):
        qh = c[:, h * LANE:(h + 1) * LANE]
        kh = c[:, GDN_WIDTH + h * LANE:GDN_WIDTH + (h + 1) * LANE]
        qs.append(qh * lax.rsqrt(jnp.sum(qh * qh, axis=-1, keepdims=True) + RMS_EPS) * (GDN_HEAD_DIM ** -0.5))
        ks.append(kh * lax.rsqrt(jnp.sum(kh * kh, axis=-1, keepdims=True) + RMS_EPS))
    lane = lax.broadcasted_iota(jnp.int32, ab.shape, 1)
    g = -jnp.exp(alog) * _softplus(ab + dtb)
    beta = _sigmoid(ab)
    bg = jnp.where(lane < GDN_HEADS, g, jnp.where(lane < 2 * GDN_HEADS, beta, 0.0))
    return jnp.concatenate(qs, axis=1), jnp.concatenate(ks, axis=1), c[:, 2 * GDN_WIDTH:], bg


def _gdn_out_fn(o, gate, og):
    outs = []
    for h in range(GDN_HEADS):
        sl = slice(h * LANE, (h + 1) * LANE)
        outs.append(_rms(o[:, sl], og) * _silu(gate[:, sl]))
    return (jnp.concatenate(outs, axis=1),)


CA_QB = 4 * CHUNK
CA_KB = CA_QB + CA_PAD


def _ca_math(q2, kb2, vb2, bias2, c):
    lane = lax.broadcasted_iota(jnp.int32, q2.shape, 1)
    qc = lax.broadcasted_iota(jnp.int32, (CA_QB, CA_KB), 0) // CHUNK
    kc = lax.broadcasted_iota(jnp.int32, (CA_QB, CA_KB), 1) // CHUNK
    valid = (kc >= qc) & (kc <= qc + CA_LEFT) & (kc + c * (CA_QB // CHUNK) >= CA_LEFT)
    out = jnp.zeros(q2.shape, F32)
    for h in range(2):
        mine = (lane >= h * CA_HEAD_DIM) & (lane < (h + 1) * CA_HEAD_DIM)
        qh = jnp.where(mine, q2, 0.0)
        s = _bdot(qh, kb2, (((1,), (1,)), ((), ()))) * (CA_HEAD_DIM ** -0.5) + bias2[h]
        s = jnp.where(valid, s, -1e30)
        p = jnp.exp(s - jnp.max(s, axis=-1, keepdims=True))
        p = p / jnp.sum(p, axis=-1, keepdims=True)
        out = out + jnp.where(mine, _bdot(p, vb2), 0.0)
    return out


CA_VEC = CA_QB + CA_KB


def _ca_specs(seq):
    q_spec = pl.BlockSpec((CA_QB, LANE), lambda hp, c: (c, hp))
    kv_spec = pl.BlockSpec((seq + CA_PAD, LANE), lambda hp, c: (0, hp))
    b_spec = pl.BlockSpec((1, 2, CA_VEC), lambda hp, c: (hp, 0, 0))
    return (D_MODEL // LANE, seq // CA_QB), q_spec, kv_spec, b_spec


def _ca_bias_from_vector(vec_ref, bias_ref):
    for h in range(2):
        rows = jnp.broadcast_to(vec_ref[0, h:h + 1, :], (CA_QB, CA_VEC))
        bias_ref[h] = pltpu.roll(rows, 0, 1, stride=1, stride_axis=0)[:, CA_QB:]


def _ca_vector_grad(dbias):
    d = jnp.concatenate([jnp.zeros((CA_QB, CA_QB), F32), dbias], axis=1)
    row = lax.broadcasted_iota(jnp.int32, d.shape, 0)
    for bit in range(CA_QB.bit_length() - 1):
        d = jnp.where((row >> bit) & 1 == 1, pltpu.roll(d, CA_VEC - (1 << bit), 1), d)
    return jnp.sum(d, axis=0, keepdims=True)


def _ca_fwd_call(q, kpad, vpad, vec):
    grid, q_spec, kv_spec, b_spec = _ca_specs(q.shape[0])

    def body(q_ref, k_ref, v_ref, vec_ref, o_ref, bias_ref):
        c = pl.program_id(1)
        start = pl.multiple_of(c * CA_QB, CA_QB)

        @pl.when(c == 0)
        def _():
            _ca_bias_from_vector(vec_ref, bias_ref)

        o_ref[...] = _ca_math(q_ref[...], k_ref[pl.ds(start, CA_KB), :], v_ref[pl.ds(start, CA_KB), :],
                              bias_ref[...], c)

    return pl.pallas_call(
        body, grid=grid, in_specs=[q_spec, kv_spec, kv_spec, b_spec], out_specs=q_spec,
        out_shape=jax.ShapeDtypeStruct(q.shape, F32), scratch_shapes=[pltpu.VMEM((2, CA_QB, CA_KB), F32)],
        compiler_params=_cparams(("parallel", "arbitrary")), name="chunkattn_fwd",
    )(q, kpad, vpad, vec)


def _ca_bwd_call(q, kpad, vpad, vec, do):
    grid, q_spec, kv_spec, b_spec = _ca_specs(q.shape[0])
    last = grid[1] - 1

    def body(q_ref, k_ref, v_ref, vec_ref, do_ref, dq_ref, dk_ref, dv_ref, dvec_ref, bias_ref, dbias_ref):
        c = pl.program_id(1)
        start = pl.multiple_of(c * CA_QB, CA_QB)

        @pl.when(c == 0)
        def _():
            _ca_bias_from_vector(vec_ref, bias_ref)
            dk_ref[...] = jnp.zeros_like(dk_ref)
            dv_ref[...] = jnp.zeros_like(dv_ref)
            dbias_ref[...] = jnp.zeros_like(dbias_ref)

        _, pullback = jax.vjp(lambda a, b, d, e: _ca_math(a, b, d, e, c), q_ref[...],
                              k_ref[pl.ds(start, CA_KB), :], v_ref[pl.ds(start, CA_KB), :], bias_ref[...])
        dq, dkb, dvb, dbias = pullback(do_ref[...])
        dq_ref[...] = dq
        dk_ref[pl.ds(start, CA_KB), :] += dkb
        dv_ref[pl.ds(start, CA_KB), :] += dvb
        dbias_ref[...] += dbias

        @pl.when(c == last)
        def _():
            for h in range(2):
                dvec_ref[0, h:h + 1, :] = _ca_vector_grad(dbias_ref[h])

    return pl.pallas_call(
        body, grid=grid, in_specs=[q_spec, kv_spec, kv_spec, b_spec, q_spec],
        out_specs=[q_spec, kv_spec, kv_spec, b_spec],
        out_shape=[jax.ShapeDtypeStruct(q.shape, F32), jax.ShapeDtypeStruct(kpad.shape, F32),
                   jax.ShapeDtypeStruct(vpad.shape, F32), jax.ShapeDtypeStruct(vec.shape, F32)],
        scratch_shapes=[pltpu.VMEM((2, CA_QB, CA_KB), F32), pltpu.VMEM((2, CA_QB, CA_KB), F32)],
        compiler_params=_cparams(("parallel", "arbitrary")), name="chunkattn_bwd",
    )(q, kpad, vpad, vec, do)


@jax.custom_vjp
def chunk_attn_core(q, kpad, vpad, vec):
    return _ca_fwd_call(q, kpad, vpad, vec)


def _ca_core_fwd(q, kpad, vpad, vec):
    return _ca_fwd_call(q, kpad, vpad, vec), (q, kpad, vpad, vec)


def _ca_core_bwd(res, do):
    return tuple(_ca_bwd_call(*res, do))


chunk_attn_core.defvjp(_ca_core_fwd, _ca_core_bwd)


S5_GB = 4
S5_U = S5_WIDTH // S5_GB
S5_L = S5_GROUPS * S5_STATE // S5_GB


def _cmul(ar, ai, br, bi):
    return ar * br - ai * bi, ar * bi + ai * br


def _hdot(a, b, dims=(((1,), (0,)), ((), ()))):
    return lax.dot_general(a, b, dims, precision=HI, preferred_element_type=F32)


_NT = (((1,), (1,)), ((), ()))
_TN = (((0,), (0,)), ((), ()))


def _s5_tables(lr, li, reverse):
    p = {1: (lr, li)}
    p[2] = _cmul(*p[1], *p[1])
    p[4] = _cmul(*p[2], *p[2])
    p[3] = _cmul(*p[2], *p[1])
    p[5] = _cmul(*p[4], *p[1])
    p[6] = _cmul(*p[4], *p[2])
    p[7] = _cmul(*p[4], *p[3])
    p[8] = _cmul(*p[4], *p[4])
    row = lax.broadcasted_iota(jnp.int32, (SUBLANE, lr.shape[1]), 0)
    tr = jnp.zeros(row.shape, F32)
    ti = jnp.zeros(row.shape, F32)
    for i in range(SUBLANE):
        k = SUBLANE - i if reverse else i + 1
        tr = jnp.where(row == i, p[k][0], tr)
        ti = jnp.where(row == i, p[k][1], ti)
    return p, (tr, ti), row


def _s5_block_scan(xr, xi, p, tab, row, hr, hi, reverse):
    for k in (1, 2, 4):
        if reverse:
            sr = jnp.where(row < SUBLANE - k, pltpu.roll(xr, SUBLANE - k, 0), 0.0)
            si = jnp.where(row < SUBLANE - k, pltpu.roll(xi, SUBLANE - k, 0), 0.0)
        else:
            sr = jnp.where(row >= k, pltpu.roll(xr, k, 0), 0.0)
            si = jnp.where(row >= k, pltpu.roll(xi, k, 0), 0.0)
        ar, ai = _cmul(p[k][0], p[k][1], sr, si)
        xr, xi = xr + ar, xi + ai
    cr, ci = _cmul(tab[0], tab[1], hr, hi)
    return xr + cr, xi + ci


def _s5_forward_scan(sr_ref, si_ref, lr, li):
    n_blocks = sr_ref.shape[0] // SUBLANE
    p, tab, row = _s5_tables(lr, li, False)

    def step(b, carry):
        base = pl.multiple_of(b * SUBLANE, SUBLANE)
        xr, xi = _s5_block_scan(sr_ref[pl.ds(base, SUBLANE), :], si_ref[pl.ds(base, SUBLANE), :],
                                p, tab, row, carry[0], carry[1], False)
        sr_ref[pl.ds(base, SUBLANE), :] = xr
        si_ref[pl.ds(base, SUBLANE), :] = xi
        return xr[SUBLANE - 1:SUBLANE, :], xi[SUBLANE - 1:SUBLANE, :]

    zero = jnp.zeros((1, lr.shape[1]), F32)
    lax.fori_loop(0, n_blocks, step, (zero, zero))


def _s5_specs(seq):
    u_spec = pl.BlockSpec((seq, S5_U), lambda g: (0, g))
    bd_spec = pl.BlockSpec((1, S5_U, S5_L), lambda g: (g, 0, 0))
    cd_spec = pl.BlockSpec((1, S5_L, S5_U), lambda g: (g, 0, 0))
    lam_spec = pl.BlockSpec((1, 2, S5_L), lambda g: (g, 0, 0))
    d_spec = pl.BlockSpec((1, S5_U), lambda g: (0, g))
    return u_spec, bd_spec, cd_spec, lam_spec, d_spec


S5_ROWS = 256


def _row_chunks(seq, fn):
    rows_per = min(S5_ROWS, seq)

    def step(r, carry):
        fn(pl.ds(pl.multiple_of(r * rows_per, rows_per), rows_per))
        return carry

    lax.fori_loop(0, seq // rows_per, step, 0)


def _s5_fwd_call(u, bdr, bdi, cdr, cdi, lam, d):
    seq = u.shape[0]
    u_spec, bd_spec, cd_spec, lam_spec, d_spec = _s5_specs(seq)

    def body(u_ref, bdr_ref, bdi_ref, cdr_ref, cdi_ref, lam_ref, d_ref, y_ref, sr_ref, si_ref):
        def project_in(rows):
            uv = u_ref[rows, :]
            sr_ref[rows, :] = _hdot(uv, bdr_ref[0])
            si_ref[rows, :] = _hdot(uv, bdi_ref[0])

        def project_out(rows):
            y_ref[rows, :] = (_hdot(sr_ref[rows, :], cdr_ref[0]) - _hdot(si_ref[rows, :], cdi_ref[0])
                              + d_ref[...] * u_ref[rows, :])

        _row_chunks(seq, project_in)
        _s5_forward_scan(sr_ref, si_ref, lam_ref[0, 0:1, :], lam_ref[0, 1:2, :])
        _row_chunks(seq, project_out)

    return pl.pallas_call(
        body, grid=(S5_GB,), in_specs=[u_spec, bd_spec, bd_spec, cd_spec, cd_spec, lam_spec, d_spec],
        out_specs=u_spec, out_shape=jax.ShapeDtypeStruct(u.shape, F32),
        scratch_shapes=[pltpu.VMEM((seq, S5_L), F32), pltpu.VMEM((seq, S5_L), F32)],
        compiler_params=_cparams(("parallel",)), name="s5_fwd",
    )(u, bdr, bdi, cdr, cdi, lam, d)


def _s5_bwd_call(u, bdr, bdi, cdr, cdi, lam, d, dy):
    seq = u.shape[0]
    n_blocks = seq // SUBLANE
    u_spec, bd_spec, cd_spec, lam_spec, d_spec = _s5_specs(seq)

    def body(u_ref, bdr_ref, bdi_ref, cdr_ref, cdi_ref, lam_ref, d_ref, dy_ref,
             du_ref, dbdr_ref, dbdi_ref, dcdr_ref, dcdi_ref, dlam_ref, dd_ref, sr_ref, si_ref, gr_ref, gi_ref):
        lr, li = lam_ref[0, 0:1, :], lam_ref[0, 1:2, :]

        def project_in(rows):
            uv = u_ref[rows, :]
            dyv = dy_ref[rows, :]
            sr_ref[rows, :] = _hdot(uv, bdr_ref[0])
            si_ref[rows, :] = _hdot(uv, bdi_ref[0])
            gr_ref[rows, :] = _hdot(dyv, cdr_ref[0], _NT)
            gi_ref[rows, :] = -_hdot(dyv, cdi_ref[0], _NT)

        _row_chunks(seq, project_in)
        _s5_forward_scan(sr_ref, si_ref, lr, li)
        p, tab, row = _s5_tables(lr, -li, True)

        def step(i, carry):
            hr, hi, acc_r, acc_i = carry
            b = n_blocks - 1 - i
            base = pl.multiple_of(b * SUBLANE, SUBLANE)
            xr, xi = _s5_block_scan(gr_ref[pl.ds(base, SUBLANE), :], gi_ref[pl.ds(base, SUBLANE), :],
                                    p, tab, row, hr, hi, True)
            gr_ref[pl.ds(base, SUBLANE), :] = xr
            gi_ref[pl.ds(base, SUBLANE), :] = xi
            prev = pl.multiple_of(jnp.maximum(b - 1, 0) * SUBLANE, SUBLANE)
            keep = (b > 0).astype(F32)
            last_r = sr_ref[pl.ds(prev, SUBLANE), :][SUBLANE - 1:SUBLANE, :] * keep
            last_i = si_ref[pl.ds(prev, SUBLANE), :][SUBLANE - 1:SUBLANE, :] * keep
            pr = jnp.where(row >= 1, pltpu.roll(sr_ref[pl.ds(base, SUBLANE), :], 1, 0), last_r)
            pi = jnp.where(row >= 1, pltpu.roll(si_ref[pl.ds(base, SUBLANE), :], 1, 0), last_i)
            acc_r = acc_r + pr * xr + pi * xi
            acc_i = acc_i + pr * xi - pi * xr
            return xr[0:1, :], xi[0:1, :], acc_r, acc_i

        zero = jnp.zeros((1, S5_L), F32)
        zacc = jnp.zeros((SUBLANE, S5_L), F32)
        _, _, acc_r, acc_i = lax.fori_loop(0, n_blocks, step, (zero, zero, zacc, zacc))
        dlam_ref[0, 0:1, :] = jnp.sum(acc_r, axis=0, keepdims=True)
        dlam_ref[0, 1:2, :] = jnp.sum(acc_i, axis=0, keepdims=True)
        for ref in (dbdr_ref, dbdi_ref, dcdr_ref, dcdi_ref, dd_ref):
            ref[...] = jnp.zeros_like(ref)

        def grads(rows):
            uv, dyv = u_ref[rows, :], dy_ref[rows, :]
            grv, giv = gr_ref[rows, :], gi_ref[rows, :]
            du_ref[rows, :] = _hdot(grv, bdr_ref[0], _NT) + _hdot(giv, bdi_ref[0], _NT) + d_ref[...] * dyv
            dbdr_ref[0] += _hdot(uv, grv, _TN)
            dbdi_ref[0] += _hdot(uv, giv, _TN)
            dcdr_ref[0] += _hdot(sr_ref[rows, :], dyv, _TN)
            dcdi_ref[0] -= _hdot(si_ref[rows, :], dyv, _TN)
            dd_ref[...] += jnp.sum(dyv * uv, axis=0, keepdims=True)

        _row_chunks(seq, grads)

    scratch = [pltpu.VMEM((seq, S5_L), F32) for _ in range(4)]
    return pl.pallas_call(
        body, grid=(S5_GB,),
        in_specs=[u_spec, bd_spec, bd_spec, cd_spec, cd_spec, lam_spec, d_spec, u_spec],
        out_specs=[u_spec, bd_spec, bd_spec, cd_spec, cd_spec, lam_spec, d_spec],
        out_shape=[jax.ShapeDtypeStruct(a.shape, F32) for a in (u, bdr, bdi, cdr, cdi, lam, d)],
        scratch_shapes=scratch, compiler_params=_cparams(("parallel",)), name="s5_bwd",
    )(u, bdr, bdi, cdr, cdi, lam, d, dy)


@jax.custom_vjp
def s5_core(u, bdr, bdi, cdr, cdi, lam, d):
    return _s5_fwd_call(u, bdr, bdi, cdr, cdi, lam, d)


def _s5_core_fwd(*args):
    return _s5_fwd_call(*args), args


def _s5_core_bwd(res, dy):
    return tuple(_s5_bwd_call(*res, dy))


s5_core.defvjp(_s5_core_fwd, _s5_core_bwd)


def _s5_discretize(a_re, a_im, log_dt, b_re, b_im, c_re, c_im, d):
    dt = jnp.exp(log_dt)[:, None]
    mag = jnp.exp(a_re * dt)
    lbr, lbi = mag * jnp.cos(a_im * dt), mag * jnp.sin(a_im * dt)
    den = a_re * a_re + a_im * a_im
    fr = ((lbr - 1.0) * a_re + lbi * a_im) / den
    fi = (lbi * a_re - (lbr - 1.0) * a_im) / den
    bbr = fr[:, :, None] * b_re - fi[:, :, None] * b_im
    bbi = fr[:, :, None] * b_im + fi[:, :, None] * b_re
    eye = jnp.eye(S5_GROUPS // S5_GB, dtype=F32)
    gl = S5_GROUPS // S5_GB

    def bd(t):
        return jnp.einsum('bgpc,gh->bgchp', t.reshape(S5_GB, gl, S5_STATE, S5_GROUP), eye).reshape(S5_GB, S5_U, S5_L)

    def cd(t):
        return jnp.einsum('bgcp,gh->bgphc', t.reshape(S5_GB, gl, S5_GROUP, S5_STATE), eye).reshape(S5_GB, S5_L, S5_U)

    lam = jnp.stack([lbr.reshape(S5_GB, S5_L), lbi.reshape(S5_GB, S5_L)], axis=1)
    return bd(bbr), bd(bbi), cd(c_re), cd(c_im), lam, d.reshape(1, S5_WIDTH)


@jax.custom_vjp
def _unit_lower_solve(neg_a, rhs, tinv):
    return _hdot(tinv, rhs)


def _unit_lower_solve_fwd(neg_a, rhs, tinv):
    x = _hdot(tinv, rhs)
    return x, (x, tinv)


def _unit_lower_solve_bwd(res, dx):
    x, tinv = res
    g = _hdot(tinv, dx, _TN)
    return _hdot(g, x, _NT), g, jnp.zeros_like(tinv)


_unit_lower_solve.defvjp(_unit_lower_solve_fwd, _unit_lower_solve_bwd)


def _unit_lower_inverse(neg_a):
    r = lax.broadcasted_iota(jnp.int32, neg_a.shape, 0)
    c = lax.broadcasted_iota(jnp.int32, neg_a.shape, 1)
    p = (r == c).astype(F32) + neg_a
    npow = _hdot(neg_a, neg_a)
    for _ in range(4):
        y = _hdot(jnp.concatenate([p, npow], axis=0), npow)
        p = p + y[:CHUNK]
        npow = y[CHUNK:]
    return p + _hdot(p, npow)


def _gdn_chunk(q, k, v, g_col, b_col, st, tinv=None):
    r = lax.broadcasted_iota(jnp.int32, (CHUNK, CHUNK), 0)
    c = lax.broadcasted_iota(jnp.int32, (CHUNK, CHUNK), 1)
    eye = (r == c).astype(F32)
    strict = r > c
    causal = r >= c
    g_row = jnp.sum(g_col * eye, axis=0, keepdims=True)
    gcum = jnp.sum(jnp.where(causal, g_row, 0.0), axis=1, keepdims=True)
    gcum_row = jnp.sum(gcum * eye, axis=0, keepdims=True)
    diff = gcum - gcum_row
    decay_strict = jnp.where(strict, jnp.exp(jnp.where(strict, diff, 0.0)), 0.0)
    decay_causal = jnp.where(causal, jnp.exp(jnp.where(causal, diff, 0.0)), 0.0)
    gamma = jnp.exp(gcum)
    g_last = jnp.sum(jnp.where(lax.broadcasted_iota(jnp.int32, (CHUNK, 1), 0) == CHUNK - 1, gcum, 0.0),
                     axis=0, keepdims=True)
    kk = _bdot(k, k, _NT)
    neg_a = -(b_col * kk * decay_strict)
    if tinv is None:
        tinv = _unit_lower_inverse(neg_a)
    x = _unit_lower_solve(neg_a, jnp.concatenate([b_col * v, (b_col * gamma) * k], axis=1), lax.stop_gradient(tinv))
    u_new, w_k = x[:, :GDN_HEAD_DIM], x[:, GDN_HEAD_DIM:]
    qk = _bdot(q, k, _NT) * decay_causal
    q_g = q * gamma
    k_tail = k * jnp.exp(g_last - gcum)
    w = u_new - _bdot(w_k, st)
    o = _bdot(q_g, st) + _bdot(qk, w)
    st_new = jnp.exp(g_last) * st + _bdot(k_tail, w, _TN)
    return o, st_new, tinv


def _gdn_cols(bgv, h):
    lane = lax.broadcasted_iota(jnp.int32, bgv.shape, 1)
    g_col = jnp.sum(jnp.where(lane == h, bgv, 0.0), axis=1, keepdims=True)
    b_col = jnp.sum(jnp.where(lane == GDN_HEADS + h, bgv, 0.0), axis=1, keepdims=True)
    return g_col, b_col


def _gdn_fwd_call(q, k, v, bg):
    seq = q.shape[0]
    n_chunks = seq // CHUNK
    x_spec = pl.BlockSpec((CHUNK, GDN_WIDTH), lambda n: (n, 0))
    bg_spec = pl.BlockSpec((CHUNK, LANE), lambda n: (n, 0))
    st_spec = pl.BlockSpec((1, GDN_WIDTH, GDN_HEAD_DIM), lambda n: (n, 0, 0))
    ti_spec = pl.BlockSpec((1, GDN_HEADS * CHUNK, CHUNK), lambda n: (n, 0, 0))

    def body(q_ref, k_ref, v_ref, bg_ref, o_ref, st_out_ref, ti_out_ref, st_ref):
        @pl.when(pl.program_id(0) == 0)
        def _():
            st_ref[...] = jnp.zeros_like(st_ref)

        bgv = bg_ref[...]
        for h in range(GDN_HEADS):
            sl = slice(h * GDN_HEAD_DIM, (h + 1) * GDN_HEAD_DIM)
            g_col, b_col = _gdn_cols(bgv, h)
            st = st_ref[sl, :]
            st_out_ref[0, sl, :] = st
            o, st_new, tinv = _gdn_chunk(q_ref[:, sl], k_ref[:, sl], v_ref[:, sl], g_col, b_col, st)
            o_ref[:, sl] = o
            st_ref[sl, :] = st_new
            ti_out_ref[0, h * CHUNK:(h + 1) * CHUNK, :] = tinv

    return pl.pallas_call(
        body, grid=(n_chunks,), in_specs=[x_spec, x_spec, x_spec, bg_spec], out_specs=[x_spec, st_spec, ti_spec],
        out_shape=[jax.ShapeDtypeStruct(q.shape, F32),
                   jax.ShapeDtypeStruct((n_chunks, GDN_WIDTH, GDN_HEAD_DIM), F32),
                   jax.ShapeDtypeStruct((n_chunks, GDN_HEADS * CHUNK, CHUNK), F32)],
        scratch_shapes=[pltpu.VMEM((GDN_WIDTH, GDN_HEAD_DIM), F32)],
        compiler_params=_cparams(("arbitrary",)), name="gdn_fwd",
    )(q, k, v, bg)


def _gdn_bwd_call(q, k, v, bg, states, tinvs, do):
    seq = q.shape[0]
    n_chunks = seq // CHUNK
    x_spec = pl.BlockSpec((CHUNK, GDN_WIDTH), lambda i: (n_chunks - 1 - i, 0))
    bg_spec = pl.BlockSpec((CHUNK, LANE), lambda i: (n_chunks - 1 - i, 0))
    st_spec = pl.BlockSpec((1, GDN_WIDTH, GDN_HEAD_DIM), lambda i: (n_chunks - 1 - i, 0, 0))
    ti_spec = pl.BlockSpec((1, GDN_HEADS * CHUNK, CHUNK), lambda i: (n_chunks - 1 - i, 0, 0))

    def body(q_ref, k_ref, v_ref, bg_ref, st_in_ref, ti_ref, do_ref, dq_ref, dk_ref, dv_ref, dbg_ref, dst_ref):
        @pl.when(pl.program_id(0) == 0)
        def _():
            dst_ref[...] = jnp.zeros_like(dst_ref)

        bgv = bg_ref[...]
        lane = lax.broadcasted_iota(jnp.int32, bgv.shape, 1)
        dbg = jnp.zeros(bgv.shape, F32)
        for h in range(GDN_HEADS):
            sl = slice(h * GDN_HEAD_DIM, (h + 1) * GDN_HEAD_DIM)
            g_col, b_col = _gdn_cols(bgv, h)
            tinv = ti_ref[0, h * CHUNK:(h + 1) * CHUNK, :]
            _, pullback = jax.vjp(lambda *a: _gdn_chunk(*a, tinv=tinv)[:2], q_ref[:, sl], k_ref[:, sl], v_ref[:, sl],
                                  g_col, b_col, st_in_ref[0, sl, :])
            dq, dk, dv, dg, db, dst = pullback((do_ref[:, sl], dst_ref[sl, :]))
            dq_ref[:, sl] = dq
            dk_ref[:, sl] = dk
            dv_ref[:, sl] = dv
            dst_ref[sl, :] = dst
            dbg = dbg + jnp.where(lane == h, dg, 0.0) + jnp.where(lane == GDN_HEADS + h, db, 0.0)
        dbg_ref[...] = dbg

    return pl.pallas_call(
        body, grid=(n_chunks,), in_specs=[x_spec, x_spec, x_spec, bg_spec, st_spec, ti_spec, x_spec],
        out_specs=[x_spec, x_spec, x_spec, bg_spec],
        out_shape=[jax.ShapeDtypeStruct(q.shape, F32)] * 3 + [jax.ShapeDtypeStruct(bg.shape, F32)],
        scratch_shapes=[pltpu.VMEM((GDN_WIDTH, GDN_HEAD_DIM), F32)],
        compiler_params=_cparams(("arbitrary",)), name="gdn_bwd",
    )(q, k, v, bg, states, tinvs, do)


@jax.custom_vjp
def gdn_core(q, k, v, bg):
    return _gdn_fwd_call(q, k, v, bg)[0]


def _gdn_core_fwd(q, k, v, bg):
    o, states, tinvs = _gdn_fwd_call(q, k, v, bg)
    return o, (q, k, v, bg, states, tinvs)


def _gdn_core_bwd(res, do):
    return tuple(_gdn_bwd_call(*res, do))


gdn_core.defvjp(_gdn_core_fwd, _gdn_core_bwd)


def _row(v):
    return v.reshape(1, -1)


def _lane_pad(v):
    return jnp.pad(v, (0, LANE - v.shape[0])).reshape(1, LANE)


def _delay_rows(x, k):
    return jnp.pad(x, ((k, 0), (0, 0)))[:x.shape[0]]


def s5_mixer(u, a_re, a_im, log_dt, b_re, b_im, c_re, c_im, d, w_glu, b_glu):
    y = s5_core(u, *_s5_discretize(a_re, a_im, log_dt, b_re, b_im, c_re, c_im, d))
    return make_rowop(_s5_post_fn, "s5_post")((y,), (w_glu, _row(b_glu)))[0]


def gated_deltanet(qkv, gate, ab, conv_w, a_log, dt_bias, out_g):
    rows = (qkv, _delay_rows(qkv, 1), _delay_rows(qkv, 2), _delay_rows(qkv, 3), ab)
    q, k, v, bg = make_rowop(_gdn_prep_fn, "gdn_prep")(rows, (conv_w, _lane_pad(a_log), _lane_pad(dt_bias)))
    o = gdn_core(q, k, v, bg)
    return make_rowop(_gdn_out_fn, "gdn_out")((o, gate), (_row(out_g),))[0]


def chunk_attention(h, w_qkv, w_out, q_g, k_g, rel_bias):
    qkv = linear_cols(h, w_qkv)
    qn, kn = make_rowop(_ca_qknorm_fn, "ca_qknorm")((qkv,), (_row(jnp.tile(q_g, 2)), _row(jnp.tile(k_g, 2))))
    kpad = jnp.pad(kn, ((CA_PAD, 0), (0, 0)))
    vpad = jnp.pad(qkv[:, 2 * D_MODEL:], ((CA_PAD, 0), (0, 0)))
    o = chunk_attn_core(qn, kpad, vpad, _rel_bias_vector(rel_bias))
    return linear(o, w_out)


def memory_cross_attention(h, mem_n, w_q, w_kv, w_out, q_g, k_g):
    q = linear(h, w_q)
    kv = linear_cols(mem_n, w_kv)
    o = make_rowop(_xattn_fn, "xattn")((q,), (kv[:, :D_MODEL], kv[:, D_MODEL:], _row(q_g), _row(k_g)))[0]
    return linear(o, w_out)


def swiglu(h, w_gate, w_up, w_down):
    g = linear_cols_dm(h, w_gate)
    u = linear_cols_dm(h, w_up)
    rows = g.shape[0] * g.shape[1]
    a = make_rowop(_swiglu_fn, "swiglu")((g.reshape(rows, -1), u.reshape(rows, -1)), ())[0]
    return linear_rows_dm(a.reshape(g.shape), w_down)


def _rel_bias_vector(rel_bias):
    heads = rel_bias.shape[0]
    n_far = CA_KB - 1 - MAX_REL
    n_neg = CA_VEC - 1 - n_far - (2 * MAX_REL + 1)
    vec = jnp.concatenate([jnp.zeros((heads, 1), F32),
                           jnp.broadcast_to(rel_bias[:, 2 * MAX_REL:], (heads, n_far)),
                           jnp.flip(rel_bias, axis=1),
                           jnp.broadcast_to(rel_bias[:, :1], (heads, n_neg))], axis=1)
    return vec.reshape(heads // 2, 2, CA_VEC)


def _exchange(arrays, modes, name):
    n = len(arrays)
    out_shapes = [jax.ShapeDtypeStruct((N_DEV,) + a.shape if m == "gather" else a.shape, a.dtype)
                  for a, m in zip(arrays, modes)]

    def body(*refs):
        ins, outs = refs[:n], refs[n:2 * n]
        send_sems, recv_sems, local_sems = refs[2 * n:]
        x, y, c = lax.axis_index("x"), lax.axis_index("y"), lax.axis_index("c")
        me = 4 * x + 2 * y + c
        pending = []
        for i in range(n):
            gather = modes[i] == "gather"
            local = pltpu.make_async_copy(ins[i] if gather else ins[i].at[me], outs[i].at[me], local_sems.at[i])
            local.start()
            pending.append(local)
        for k in range(1, N_DEV):
            px, py, pc = (x + (k >> 2)) % 2, (y + ((k >> 1) & 1)) % 2, (c + (k & 1)) % 2
            peer = 4 * px + 2 * py + pc
            for i in range(n):
                src = ins[i] if modes[i] == "gather" else ins[i].at[peer]
                sem = i * (N_DEV - 1) + k - 1
                send = pltpu.make_async_remote_copy(src_ref=src, dst_ref=outs[i].at[me], send_sem=send_sems.at[sem],
                                                    recv_sem=recv_sems.at[sem], device_id=(px, py, pc),
                                                    device_id_type=pl.DeviceIdType.MESH)
                send.start()
                arrival = pltpu.make_async_remote_copy(src_ref=src, dst_ref=outs[i].at[peer],
                                                       send_sem=send_sems.at[sem], recv_sem=recv_sems.at[sem],
                                                       device_id=(px, py, pc), device_id_type=pl.DeviceIdType.MESH)
                pending.append((send, arrival))
        for item in pending:
            if isinstance(item, tuple):
                item[0].wait_send()
                item[1].wait_recv()
            else:
                item.wait()

    any_spec = pl.BlockSpec(memory_space=pl.ANY)
    return pl.pallas_call(
        body, in_specs=[any_spec] * n, out_specs=[any_spec] * n, out_shape=out_shapes,
        scratch_shapes=[pltpu.SemaphoreType.DMA((n * (N_DEV - 1),)), pltpu.SemaphoreType.DMA((n * (N_DEV - 1),)),
                        pltpu.SemaphoreType.DMA((n,))],
        name=name,
    )(*arrays)


_HBM_SPEC = pl.BlockSpec(memory_space=pltpu.HBM)
_SEM_SPEC = pl.BlockSpec(memory_space=pltpu.SEMAPHORE)
_SIDE_EFFECT = pltpu.SideEffectType.DATAFLOW_SIDE_EFFECTING


def _peer(x, y, c, k):
    return (x + (k >> 2)) % 2, (y + ((k >> 1) & 1)) % 2, (c + (k & 1)) % 2


def _exchange_start(arrays, modes, after, name):
    n = len(arrays)
    n_sem = n * (N_DEV - 1)
    lands = [pltpu.with_memory_space_constraint(lax.empty((N_DEV,) + a.shape if m == "gather" else a.shape, a.dtype),
                                                pltpu.HBM) for a, m in zip(arrays, modes)]
    arrays = [pltpu.with_memory_space_constraint(a, pltpu.HBM) for a in arrays]

    def body(*refs):
        ins, zones = refs[:n], refs[n:2 * n]
        send_sems, recv_sems, own_sems = refs[2 * n + 1:2 * n + 4]
        token_ref = refs[4 * n + 4]
        x, y, c = lax.axis_index("x"), lax.axis_index("y"), lax.axis_index("c")
        me = 4 * x + 2 * y + c
        for i in range(n):
            pltpu.make_async_copy(ins[i] if modes[i] == "gather" else ins[i].at[me], zones[i].at[me],
                                  own_sems.at[i]).start()
        for k in range(1, N_DEV):
            px, py, pc = _peer(x, y, c, k)
            peer = 4 * px + 2 * py + pc
            for i in range(n):
                sem = i * (N_DEV - 1) + k - 1
                pltpu.make_async_remote_copy(src_ref=ins[i] if modes[i] == "gather" else ins[i].at[peer],
                                             dst_ref=zones[i].at[me], send_sem=send_sems.at[sem],
                                             recv_sem=recv_sems.at[sem], device_id=(px, py, pc),
                                             device_id_type=pl.DeviceIdType.MESH).start()
        token_ref[...] = jnp.zeros_like(token_ref)

    out_shape = ((pltpu.SemaphoreType.DMA((n_sem,)), pltpu.SemaphoreType.DMA((n_sem,)), pltpu.SemaphoreType.DMA((n,)))
                 + tuple(pltpu.HBM(a.shape, a.dtype) for a in arrays) + tuple(pltpu.HBM(z.shape, z.dtype) for z in lands)
                 + (jax.ShapeDtypeStruct((SUBLANE, LANE), F32),))
    res = pl.pallas_call(
        body, name=name, out_shape=out_shape,
        in_specs=[_HBM_SPEC] * (2 * n) + [pl.BlockSpec(memory_space=pl.ANY)],
        out_specs=(_SEM_SPEC,) * 3 + (_HBM_SPEC,) * (2 * n) + (pl.BlockSpec(memory_space=pltpu.VMEM),),
        input_output_aliases={i: 3 + i for i in range(2 * n)},
        compiler_params=pltpu.CompilerParams(has_side_effects=_SIDE_EFFECT),
    )(*arrays, *lands, after)
    return tuple(res[:3]), list(res[3:3 + n]), list(res[3 + n:3 + 2 * n]), res[3 + 2 * n]


def _exchange_wait(started, modes, after, name):
    sems, sources, zones, _ = started
    n = len(sources)

    def body(*refs):
        ins, lands = refs[:n], refs[n:2 * n]
        send_ref, recv_ref, own_ref = refs[2 * n:2 * n + 3]
        x, y, c = lax.axis_index("x"), lax.axis_index("y"), lax.axis_index("c")
        me = 4 * x + 2 * y + c
        for i in range(n):
            pltpu.make_async_copy(ins[i] if modes[i] == "gather" else ins[i].at[me], lands[i].at[me],
                                  own_ref.at[i]).wait()
        for k in range(1, N_DEV):
            px, py, pc = _peer(x, y, c, k)
            peer = 4 * px + 2 * py + pc
            for i in range(n):
                sem = i * (N_DEV - 1) + k - 1
                cp = pltpu.make_async_remote_copy(src_ref=ins[i] if modes[i] == "gather" else ins[i].at[peer],
                                                  dst_ref=lands[i].at[peer], send_sem=send_ref.at[sem],
                                                  recv_sem=recv_ref.at[sem], device_id=(px, py, pc),
                                                  device_id_type=pl.DeviceIdType.MESH)
                cp.wait_send()
                cp.wait_recv()

    res = pl.pallas_call(
        body, name=name,
        out_shape=tuple(pltpu.HBM(a.shape, a.dtype) for a in sources) + tuple(pltpu.HBM(z.shape, z.dtype) for z in zones),
        in_specs=[_HBM_SPEC] * (2 * n) + [_SEM_SPEC] * 3 + [pl.BlockSpec(memory_space=pl.ANY)],
        out_specs=(_HBM_SPEC,) * (2 * n), input_output_aliases={i: i for i in range(2 * n)},
        compiler_params=pltpu.CompilerParams(has_side_effects=_SIDE_EFFECT),
    )(*sources, *zones, *sems, after)
    return list(res[n:])


ADAM_TILE = 64 * 1024


def _adam_call(w, m, v, slots, name):
    n_layers, rows, cols = w.shape
    tr = rows
    if n_layers * rows * cols > ADAM_TILE:
        fits = [t for t in range(SUBLANE, rows, SUBLANE) if rows % t == 0 and n_layers * t * cols <= ADAM_TILE]
        tr = max(fits) if fits else SUBLANE
    c1 = 1.0 - ADAM_B1 ** ADAM_STEP
    c2 = 1.0 - ADAM_B2 ** ADAM_STEP

    def body(*refs):
        w_ref, m_ref, v_ref = refs[:3]
        slot_refs = refs[3:3 + n_layers]
        grad_ref, delta_ref, nm_ref, nv_ref = refs[3 + n_layers:]
        for layer in range(n_layers):
            g = slot_refs[layer][0].astype(F32)
            for k in range(1, N_DEV):
                g = g + slot_refs[layer][k].astype(F32)
            m_new = ADAM_B1 * m_ref[layer] + (1.0 - ADAM_B1) * g
            v_new = ADAM_B2 * v_ref[layer] + (1.0 - ADAM_B2) * (g * g)
            m_hat = m_new / c1
            v_hat = v_new / c2
            grad_ref[layer] = g
            delta_ref[layer] = -ADAM_LR * (m_hat / (jnp.sqrt(v_hat) + ADAM_EPS) + ADAM_WD * w_ref[layer])
            nm_ref[layer] = m_new
            nv_ref[layer] = v_new

    spec = pl.BlockSpec((n_layers, tr, cols), lambda i: (0, i, 0))
    slot_spec = pl.BlockSpec((N_DEV, tr, cols), lambda i: (0, i, 0))
    return pl.pallas_call(
        body, grid=(rows // tr,), in_specs=[spec, spec, spec] + [slot_spec] * n_layers,
        out_specs=[spec] * 4, out_shape=[jax.ShapeDtypeStruct(w.shape, F32)] * 4,
        compiler_params=_cparams(("parallel",)), name=name,
    )(w, m, v, *slots)


WEIGHT_NAMES = ['ab_norm_g', 'ab_w_in', 'ab_w_out', 's5_a_re', 's5_a_im', 's5_log_dt', 's5_b_re', 's5_b_im', 's5_c_re',
                's5_c_im', 's5_d', 's5_w_glu', 's5_b_glu', 'gdn_conv_w', 'gdn_a_log', 'gdn_dt_bias', 'gdn_out_norm_g',
                'c_norm_g', 'c_w_qkv', 'c_w_out', 'c_q_norm_g', 'c_k_norm_g', 'c_rel_bias', 'mem_norm_g', 'xa_norm_g',
                'xa_w_q', 'xa_w_kv', 'xa_w_out', 'xa_q_norm_g', 'xa_k_norm_g', 'f_norm_g', 'f_w_gate', 'f_w_up',
                'f_w_down']

SHARDED = {
    'ab_w_in': ('col', BF16), 'ab_w_out': ('row', BF16), 's5_w_glu': ('row', BF16), 'gdn_conv_w': ('col', F32),
    'c_norm_g': ('col', F32), 'c_w_qkv': ('col', BF16), 'c_w_out': ('row', BF16), 'xa_w_q': ('row', BF16),
    'xa_w_kv': ('col', BF16), 'xa_w_out': ('row', BF16), 'f_w_gate': ('col', BF16), 'f_w_up': ('col', BF16),
    'f_w_down': ('row', BF16),
}
GATHERED_AS_IS = ('c_w_qkv', 'xa_w_kv', 'f_w_gate', 'f_w_up', 'f_w_down')
REPLICATED = [n for n in WEIGHT_NAMES if n not in SHARDED]
PACK_UNIT = SUBLANE * LANE


def _full_from_gathered(g, axis):
    if axis == "row":
        return g.reshape(g.shape[0] * g.shape[1], g.shape[2])
    return jnp.transpose(g, (1, 0, 2)).reshape(g.shape[1], g.shape[0] * g.shape[2])


def _pack(arrays):
    flat = []
    for a in arrays:
        size = a.size
        padded = -(-size // PACK_UNIT) * PACK_UNIT
        flat.append(jnp.pad(a.reshape(-1), (0, padded - size)).reshape(-1, LANE))
    return jnp.concatenate(flat, axis=0)


def _unpack(buf, shapes):
    out, row = [], 0
    for shape in shapes:
        size = math.prod(shape)
        rows = -(-size // PACK_UNIT) * SUBLANE
        out.append(buf[row:row + rows].reshape(-1)[:size].reshape(shape))
        row += rows
    return out


N_STAGES = 2 * DEPTH
EVEN_SHARDED = ['ab_w_in', 'ab_w_out', 's5_w_glu', 'gdn_conv_w']
ODD_SHARDED = ['c_norm_g', 'c_w_qkv', 'c_w_out']
ALL_SHARDED = ['xa_w_q', 'xa_w_kv', 'xa_w_out', 'f_w_gate', 'f_w_up', 'f_w_down']
EVEN_SMALL = ['ab_norm_g', 's5_a_re', 's5_a_im', 's5_log_dt', 's5_b_re', 's5_b_im', 's5_c_re', 's5_c_im', 's5_d',
              's5_b_glu', 'gdn_a_log', 'gdn_dt_bias', 'gdn_out_norm_g']
ODD_SMALL = ['c_q_norm_g', 'c_k_norm_g', 'c_rel_bias']
ALL_SMALL = ['xa_norm_g', 'xa_q_norm_g', 'xa_k_norm_g', 'f_norm_g']


def _stage_params(stage):
    layer, part = divmod(stage, 2)
    if part == 1:
        return [(n, layer) for n in ALL_SHARDED], [(n, layer) for n in ALL_SMALL]
    big, small = (EVEN_SHARDED, EVEN_SMALL) if layer % 2 == 0 else (ODD_SHARDED, ODD_SMALL)
    return [(n, layer // 2) for n in big], [(n, layer // 2) for n in small]


def _stage_forward(stage, landed, small, x, mem_n):
    layer, part = divmod(stage, 2)
    big = {}
    for (n, _), g in zip(_stage_params(stage)[0], landed):
        if n == 'ab_w_in':
            big[n] = cols_to_natural(g, AB_IN_PAD)
        elif n in GATHERED_AS_IS:
            big[n] = g
        elif n == 's5_w_glu':
            big[n] = _full_from_gathered(g, 'row').astype(F32)
        else:
            big[n] = _full_from_gathered(g, SHARDED[n][0])
    if part == 1:
        h = rmsnorm(x, small['xa_norm_g'], "xa_norm")
        x = x + memory_cross_attention(h, mem_n, big['xa_w_q'], big['xa_w_kv'], big['xa_w_out'],
                                       small['xa_q_norm_g'], small['xa_k_norm_g'])
        h = rmsnorm(x, small['f_norm_g'], "f_norm")
        return x + swiglu(h, big['f_w_gate'], big['f_w_up'], big['f_w_down'])
    if layer % 2 == 0:
        h = rmsnorm(x, small['ab_norm_g'], "ab_norm")
        proj = linear(h, big['ab_w_in'])
        u = proj[:, :S5_WIDTH]
        qkv = proj[:, S5_WIDTH:S5_WIDTH + 3 * GDN_WIDTH]
        gate = proj[:, S5_WIDTH + 3 * GDN_WIDTH:S5_WIDTH + 4 * GDN_WIDTH]
        ab = proj[:, S5_WIDTH + 4 * GDN_WIDTH:]
        a_out = s5_mixer(u, small['s5_a_re'], small['s5_a_im'], small['s5_log_dt'], small['s5_b_re'], small['s5_b_im'],
                         small['s5_c_re'], small['s5_c_im'], small['s5_d'], big['s5_w_glu'], small['s5_b_glu'])
        b_out = gated_deltanet(qkv, gate, ab, big['gdn_conv_w'], small['gdn_a_log'], small['gdn_dt_bias'],
                               small['gdn_out_norm_g'])
        return x + linear(jnp.concatenate([a_out, b_out], axis=1), big['ab_w_out'])
    h = rmsnorm(x, big['c_norm_g'].reshape(-1), "c_norm")
    return x + chunk_attention(h, big['c_w_qkv'], big['c_w_out'], small['c_q_norm_g'], small['c_k_norm_g'],
                               small['c_rel_bias'])


def _loss_rows(x, target):
    return jnp.sum(make_rowop(_loss_fn, "loss")((x, target), ())[0])


def kernel(x, mem, ab_norm_g, ab_w_in, ab_w_out, s5_a_re, s5_a_im, s5_log_dt, s5_b_re, s5_b_im, s5_c_re, s5_c_im, s5_d, s5_w_glu, s5_b_glu, gdn_conv_w, gdn_a_log, gdn_dt_bias, gdn_out_norm_g, c_norm_g, c_w_qkv, c_w_out, c_q_norm_g, c_k_norm_g, c_rel_bias, mem_norm_g, xa_norm_g, xa_w_q, xa_w_kv, xa_w_out, xa_q_norm_g, xa_k_norm_g, f_norm_g, f_w_gate, f_w_up, f_w_down, loss_target, m_ab_norm_g, m_ab_w_in, m_ab_w_out, m_s5_a_re, m_s5_a_im, m_s5_log_dt, m_s5_b_re, m_s5_b_im, m_s5_c_re, m_s5_c_im, m_s5_d, m_s5_w_glu, m_s5_b_glu, m_gdn_conv_w, m_gdn_a_log, m_gdn_dt_bias, m_gdn_out_norm_g, m_c_norm_g, m_c_w_qkv, m_c_w_out, m_c_q_norm_g, m_c_k_norm_g, m_c_rel_bias, m_mem_norm_g, m_xa_norm_g, m_xa_w_q, m_xa_w_kv, m_xa_w_out, m_xa_q_norm_g, m_xa_k_norm_g, m_f_norm_g, m_f_w_gate, m_f_w_up, m_f_w_down, v_ab_norm_g, v_ab_w_in, v_ab_w_out, v_s5_a_re, v_s5_a_im, v_s5_log_dt, v_s5_b_re, v_s5_b_im, v_s5_c_re, v_s5_c_im, v_s5_d, v_s5_w_glu, v_s5_b_glu, v_gdn_conv_w, v_gdn_a_log, v_gdn_dt_bias, v_gdn_out_norm_g, v_c_norm_g, v_c_w_qkv, v_c_w_out, v_c_q_norm_g, v_c_k_norm_g, v_c_rel_bias, v_mem_norm_g, v_xa_norm_g, v_xa_w_q, v_xa_w_kv, v_xa_w_out, v_xa_q_norm_g, v_xa_k_norm_g, v_f_norm_g, v_f_w_gate, v_f_w_up, v_f_w_down):
    given = dict(locals())
    no_after = jnp.zeros((SUBLANE, LANE), F32)

    def shard(n, idx):
        a = given[n][idx]
        return (a.reshape(1, -1) if a.ndim == 1 else a).astype(SHARDED[n][1])

    def gather_start(stage, after):
        arrays = [shard(n, idx) for n, idx in _stage_params(stage)[0]]
        return _exchange_start(arrays, ["gather"] * len(arrays), after, "gather_start_%d" % stage)

    act = x[0]
    mem_n, mem_pullback = jax.vjp(lambda m, g: rmsnorm(m, g, "mem_norm"), mem[0], mem_norm_g)
    started = gather_start(0, no_after)
    pullbacks = []
    for stage in range(N_STAGES):
        landed = _exchange_wait(started, ["gather"] * len(started[1]), act, "gather_wait_%d" % stage)
        if stage + 1 < N_STAGES:
            started = gather_start(stage + 1, landed[0])
            act = act + started[3][0, 0]
        small = {n: given[n][idx] for n, idx in _stage_params(stage)[1]}
        act, pullback = jax.vjp(functools.partial(_stage_forward, stage), landed, small, act, mem_n)
        pullbacks.append(pullback)
    loss_local, loss_pullback = jax.vjp(_loss_rows, act, loss_target[0])
    d_act = loss_pullback(jnp.ones((), F32))[0]

    d_mem_n = jnp.zeros_like(mem_n)
    g_small = {}
    received = [None] * N_STAGES
    started, after = None, no_after
    for stage in reversed(range(N_STAGES)):
        d_landed, d_small, d_act, d_mem = pullbacks[stage](d_act)
        if stage % 2 == 1:
            d_mem_n = d_mem_n + d_mem
        for n, idx in _stage_params(stage)[1]:
            g_small[(n, idx)] = d_small[n]
        if started is not None:
            received[stage + 1] = _exchange_wait(started, ["scatter"] * len(started[1]), d_act,
                                                 "scatter_wait_%d" % (stage + 1))
            after = received[stage + 1][0]
        started = _exchange_start(list(d_landed), ["scatter"] * len(d_landed), after, "scatter_start_%d" % stage)
        d_act = d_act + started[3][0, 0]
    g_small[('mem_norm_g', None)] = mem_pullback(d_mem_n)[1]

    def small_grad(n):
        if n == 'mem_norm_g':
            return g_small[(n, None)]
        return jnp.stack([g_small[(n, i)] for i in range(given[n].shape[0])], axis=0)

    packed = _exchange([_pack([small_grad(n) for n in REPLICATED])], ["gather"], "small_grads_allgather")[0]
    received[0] = _exchange_wait(started, ["scatter"] * len(started[1]), packed, "scatter_wait_0")

    results = {}
    for n in SHARDED:
        slots = {}
        for stage in range(N_STAGES):
            for (pn, idx), r in zip(_stage_params(stage)[0], received[stage]):
                if pn == n:
                    slots[idx] = r
        shape = given[n].shape
        to3d = lambda a: a.reshape(a.shape[0], -1, a.shape[-1])
        outs = _adam_call(to3d(given[n]), to3d(given['m_' + n]), to3d(given['v_' + n]),
                          [slots[i] for i in range(len(slots))], "adamw_" + n)
        results[n] = [o.reshape(shape) for o in outs]
    outs = _adam_call(_pack([given[n] for n in REPLICATED])[None], _pack([given['m_' + n] for n in REPLICATED])[None],
                      _pack([given['v_' + n] for n in REPLICATED])[None], [packed], "adamw_replicated")
    shapes = [given[n].shape for n in REPLICATED]
    for j, parts in enumerate(zip(*[_unpack(o[0], shapes) for o in outs])):
        results[REPLICATED[j]] = list(parts)

    loss = lax.psum(loss_local, ("x", "y", "c"))
    return (loss, d_act[None], *[results[n][0] for n in WEIGHT_NAMES], *[results[n][1] for n in WEIGHT_NAMES],
            *[results[n][2] for n in WEIGHT_NAMES], *[results[n][3] for n in WEIGHT_NAMES])
```

```python
import functools
import math

import jax
import jax.numpy as jnp
import numpy as np
from jax import lax
from jax.experimental import pallas as pl
from jax.experimental.pallas import tpu as pltpu

F32 = jnp.float32
BF16 = jnp.bfloat16
HI = lax.Precision.HIGHEST

N_DEV = 8
D_MODEL = 1024
SEQ = 2048
DEPTH = 4
CHUNK = 64
N_MEM = 256
RMS_EPS = 1e-6
S5_WIDTH = 512
S5_GROUP = 16
S5_GROUPS = 32
S5_STATE = 64
GDN_HEAD_DIM = 128
GDN_WIDTH = 512
GDN_HEADS = 4
GDN_CONV = 4
AB_IN = S5_WIDTH + 4 * GDN_WIDTH + 2 * GDN_HEADS
AB_IN_PAD = 2688
CA_HEADS = 16
CA_HEAD_DIM = 64
CA_LEFT = 8
CA_BAND = (CA_LEFT + 1) * CHUNK
CA_PAD = CA_LEFT * CHUNK
MAX_REL = 128
XA_HEADS = 4
XA_HEAD_DIM = 256
FFN = 2816
ADAM_LR, ADAM_B1, ADAM_B2, ADAM_EPS, ADAM_WD, ADAM_STEP = 0.001, 0.9, 0.999, 1e-08, 0.01, 10

VMEM_LIMIT = 48 * 1024 * 1024
LANE = 128
SUBLANE = 8


def _cparams(sem=None):
    return pltpu.CompilerParams(dimension_semantics=sem, vmem_limit_bytes=VMEM_LIMIT)


def _divisor_tile(n, target, unit=LANE):
    if n <= target:
        return n
    best = None
    for t in range(unit, target + 1, unit):
        if n % t == 0:
            best = t
    assert best is not None, (n, target)
    return best


def _matmul(a, b, *, ta=False, tb=False, out_dtype=F32, name="mm"):
    if ta:
        k_dim, m_dim = a.shape
    else:
        m_dim, k_dim = a.shape
    if tb:
        n_dim, kb = b.shape
    else:
        kb, n_dim = b.shape
    assert kb == k_dim, (a.shape, b.shape, ta, tb)
    tm = _divisor_tile(m_dim, 1024)
    tn = _divisor_tile(n_dim, 512)
    tk = _divisor_tile(k_dim, 1408)
    nk = k_dim // tk
    dims = (((0 if ta else 1,), (1 if tb else 0,)), ((), ()))

    def body(a_ref, b_ref, o_ref, acc_ref):
        k = pl.program_id(2)

        @pl.when(k == 0)
        def _():
            acc_ref[...] = jnp.zeros_like(acc_ref)

        acc_ref[...] += lax.dot_general(a_ref[...].astype(BF16), b_ref[...].astype(BF16), dims,
                                        preferred_element_type=F32)

        @pl.when(k == nk - 1)
        def _():
            o_ref[...] = acc_ref[...].astype(o_ref.dtype)

    a_spec = pl.BlockSpec((tk, tm), lambda i, j, k: (k, i)) if ta else pl.BlockSpec((tm, tk), lambda i, j, k: (i, k))
    b_spec = pl.BlockSpec((tn, tk), lambda i, j, k: (j, k)) if tb else pl.BlockSpec((tk, tn), lambda i, j, k: (k, j))
    return pl.pallas_call(
        body,
        grid=(m_dim // tm, n_dim // tn, nk),
        in_specs=[a_spec, b_spec],
        out_specs=pl.BlockSpec((tm, tn), lambda i, j, k: (i, j)),
        out_shape=jax.ShapeDtypeStruct((m_dim, n_dim), out_dtype),
        scratch_shapes=[pltpu.VMEM((tm, tn), F32)],
        compiler_params=_cparams(("parallel", "parallel", "arbitrary")),
        name=name,
    )(a, b)


@jax.custom_vjp
def linear(a, w):
    return _matmul(a, w, name="linear_fwd")


def _linear_fwd(a, w):
    return _matmul(a, w, name="linear_fwd"), (a, w)


def _linear_bwd(res, dy):
    a, w = res
    da = _matmul(dy, w, tb=True, name="linear_da")
    dw = _matmul(a, dy, ta=True, out_dtype=w.dtype, name="linear_dw")
    return da, dw


linear.defvjp(_linear_fwd, _linear_bwd)


def _mm_call(name, a, b, out_struct, grid, a_spec, b_spec, o_spec, dims, lead):
    nk = grid[-1]
    acc_shape = o_spec.block_shape[1:] if lead[2] else o_spec.block_shape

    def body(a_ref, b_ref, o_ref, acc_ref):
        k = pl.program_id(len(grid) - 1)

        @pl.when(k == 0)
        def _():
            acc_ref[...] = jnp.zeros_like(acc_ref)

        av = a_ref[0] if lead[0] else a_ref[...]
        bv = b_ref[0] if lead[1] else b_ref[...]
        acc_ref[...] += lax.dot_general(av.astype(BF16), bv.astype(BF16), dims, preferred_element_type=F32)

        @pl.when(k == nk - 1)
        def _():
            if lead[2]:
                o_ref[0] = acc_ref[...].astype(o_ref.dtype)
            else:
                o_ref[...] = acc_ref[...].astype(o_ref.dtype)

    return pl.pallas_call(
        body, grid=grid, in_specs=[a_spec, b_spec], out_specs=o_spec, out_shape=out_struct,
        scratch_shapes=[pltpu.VMEM(tuple(acc_shape), F32)],
        compiler_params=_cparams(("parallel", "parallel", "arbitrary")), name=name,
    )(a, b)


_NN = (((1,), (0,)), ((), ()))
_NT_DIMS = (((1,), (1,)), ((), ()))
_TN_DIMS = (((0,), (0,)), ((), ()))


def _cols_fwd(a, g, dm_out):
    m_dim, k_dim = a.shape
    _, _, c_dim = g.shape
    tm = _divisor_tile(m_dim, 1024)
    tk = _divisor_tile(k_dim, 1024)
    a_spec = pl.BlockSpec((tm, tk), lambda i, j, k: (i, k))
    b_spec = pl.BlockSpec((1, tk, c_dim), lambda i, j, k: (j, k, 0))
    if dm_out:
        o_spec = pl.BlockSpec((1, tm, c_dim), lambda i, j, k: (j, i, 0))
        out = jax.ShapeDtypeStruct((N_DEV, m_dim, c_dim), F32)
    else:
        o_spec = pl.BlockSpec((tm, c_dim), lambda i, j, k: (i, j))
        out = jax.ShapeDtypeStruct((m_dim, N_DEV * c_dim), F32)
    return _mm_call("cols_fwd", a, g, out, (m_dim // tm, N_DEV, k_dim // tk), a_spec, b_spec, o_spec, _NN,
                    (False, True, dm_out))


def _cols_da(dy, g, dm_out):
    _, k_dim, c_dim = g.shape
    m_dim = dy.shape[1] if dm_out else dy.shape[0]
    tm = _divisor_tile(m_dim, 1024)
    tk = _divisor_tile(k_dim, 1024)
    if dm_out:
        a_spec = pl.BlockSpec((1, tm, c_dim), lambda i, kb, j: (j, i, 0))
    else:
        a_spec = pl.BlockSpec((tm, c_dim), lambda i, kb, j: (i, j))
    b_spec = pl.BlockSpec((1, tk, c_dim), lambda i, kb, j: (j, kb, 0))
    o_spec = pl.BlockSpec((tm, tk), lambda i, kb, j: (i, kb))
    return _mm_call("cols_da", dy, g, jax.ShapeDtypeStruct((m_dim, k_dim), F32), (m_dim // tm, k_dim // tk, N_DEV),
                    a_spec, b_spec, o_spec, _NT_DIMS, (dm_out, True, False))


def _cols_dg(a, dy, g, dm_out):
    _, k_dim, c_dim = g.shape
    m_dim = a.shape[0]
    tm = _divisor_tile(m_dim, 2048)
    tk = _divisor_tile(k_dim, 512)
    a_spec = pl.BlockSpec((tm, tk), lambda kb, j, m: (m, kb))
    if dm_out:
        b_spec = pl.BlockSpec((1, tm, c_dim), lambda kb, j, m: (j, m, 0))
    else:
        b_spec = pl.BlockSpec((tm, c_dim), lambda kb, j, m: (m, j))
    o_spec = pl.BlockSpec((1, tk, c_dim), lambda kb, j, m: (j, kb, 0))
    return _mm_call("cols_dg", a, dy, jax.ShapeDtypeStruct(g.shape, g.dtype), (k_dim // tk, N_DEV, m_dim // tm),
                    a_spec, b_spec, o_spec, _TN_DIMS, (False, dm_out, True))


def _make_linear_cols(dm_out):
    @jax.custom_vjp
    def op(a, g):
        return _cols_fwd(a, g, dm_out)

    def fwd(a, g):
        return _cols_fwd(a, g, dm_out), (a, g)

    def bwd(res, dy):
        a, g = res
        return _cols_da(dy, g, dm_out), _cols_dg(a, dy, g, dm_out)

    op.defvjp(fwd, bwd)
    return op


linear_cols = _make_linear_cols(False)
linear_cols_dm = _make_linear_cols(True)


def _rows_fwd(a, g):
    _, m_dim, r_dim = a.shape
    n_dim = g.shape[2]
    tm = _divisor_tile(m_dim, 1024)
    tn = _divisor_tile(n_dim, 1024)
    a_spec = pl.BlockSpec((1, tm, r_dim), lambda i, n, j: (j, i, 0))
    b_spec = pl.BlockSpec((1, r_dim, tn), lambda i, n, j: (j, 0, n))
    o_spec = pl.BlockSpec((tm, tn), lambda i, n, j: (i, n))
    return _mm_call("rows_fwd", a, g, jax.ShapeDtypeStruct((m_dim, n_dim), F32), (m_dim // tm, n_dim // tn, N_DEV),
                    a_spec, b_spec, o_spec, _NN, (True, True, False))


def _rows_da(dy, g):
    m_dim, n_dim = dy.shape
    r_dim = g.shape[1]
    tm = _divisor_tile(m_dim, 1024)
    tn = _divisor_tile(n_dim, 1024)
    a_spec = pl.BlockSpec((tm, tn), lambda i, j, n: (i, n))
    b_spec = pl.BlockSpec((1, r_dim, tn), lambda i, j, n: (j, 0, n))
    o_spec = pl.BlockSpec((1, tm, r_dim), lambda i, j, n: (j, i, 0))
    return _mm_call("rows_da", dy, g, jax.ShapeDtypeStruct((N_DEV, m_dim, r_dim), F32),
                    (m_dim // tm, N_DEV, n_dim // tn), a_spec, b_spec, o_spec, _NT_DIMS, (False, True, True))


def _rows_dg(a, dy, g):
    _, m_dim, r_dim = a.shape
    n_dim = dy.shape[1]
    tm = _divisor_tile(m_dim, 2048)
    tn = _divisor_tile(n_dim, 512)
    a_spec = pl.BlockSpec((1, tm, r_dim), lambda j, n, m: (j, m, 0))
    b_spec = pl.BlockSpec((tm, tn), lambda j, n, m: (m, n))
    o_spec = pl.BlockSpec((1, r_dim, tn), lambda j, n, m: (j, 0, n))
    return _mm_call("rows_dg", a, dy, jax.ShapeDtypeStruct(g.shape, g.dtype), (N_DEV, n_dim // tn, m_dim // tm),
                    a_spec, b_spec, o_spec, _TN_DIMS, (True, False, True))


@jax.custom_vjp
def linear_rows_dm(a, g):
    return _rows_fwd(a, g)


def _linear_rows_dm_fwd(a, g):
    return _rows_fwd(a, g), (a, g)


def _linear_rows_dm_bwd(res, dy):
    a, g = res
    return _rows_da(dy, g), _rows_dg(a, dy, g)


linear_rows_dm.defvjp(_linear_rows_dm_fwd, _linear_rows_dm_bwd)


def _cols_to_natural_call(g, width):
    _, k_dim, c_dim = g.shape
    tk = _divisor_tile(k_dim, 256, SUBLANE)

    def body(g_ref, o_ref):
        for j in range(N_DEV):
            o_ref[:, j * c_dim:(j + 1) * c_dim] = g_ref[j]
        if width > N_DEV * c_dim:
            o_ref[:, N_DEV * c_dim:] = jnp.zeros((tk, width - N_DEV * c_dim), o_ref.dtype)

    return pl.pallas_call(
        body, grid=(k_dim // tk,), in_specs=[pl.BlockSpec((N_DEV, tk, c_dim), lambda i: (0, i, 0))],
        out_specs=pl.BlockSpec((tk, width), lambda i: (i, 0)), out_shape=jax.ShapeDtypeStruct((k_dim, width), g.dtype),
        compiler_params=_cparams(("parallel",)), name="cols_to_natural",
    )(g)


def _natural_to_cols_call(w, c_dim):
    k_dim, width = w.shape
    tk = _divisor_tile(k_dim, 256, SUBLANE)

    def body(w_ref, o_ref):
        for j in range(N_DEV):
            o_ref[j] = w_ref[:, j * c_dim:(j + 1) * c_dim]

    return pl.pallas_call(
        body, grid=(k_dim // tk,), in_specs=[pl.BlockSpec((tk, width), lambda i: (i, 0))],
        out_specs=pl.BlockSpec((N_DEV, tk, c_dim), lambda i: (0, i, 0)),
        out_shape=jax.ShapeDtypeStruct((N_DEV, k_dim, c_dim), w.dtype),
        compiler_params=_cparams(("parallel",)), name="natural_to_cols",
    )(w)


@functools.partial(jax.custom_vjp, nondiff_argnums=(1,))
def cols_to_natural(g, width):
    return _cols_to_natural_call(g, width)


def _cols_to_natural_fwd(g, width):
    return _cols_to_natural_call(g, width), g.shape[2]


def _cols_to_natural_bwd(width, c_dim, dw):
    return (_natural_to_cols_call(dw, c_dim),)


cols_to_natural.defvjp(_cols_to_natural_fwd, _cols_to_natural_bwd)


def make_rowop(fn, name, tm=256):
    def specs(rows, params):
        row_specs = [pl.BlockSpec((tm, r.shape[1]), lambda i: (i, 0)) for r in rows]
        par_specs = [pl.BlockSpec(p.shape, lambda i: (0, 0)) for p in params]
        return row_specs, par_specs

    def out_structs(rows, params):
        tiles = [jax.ShapeDtypeStruct((tm, r.shape[1]), r.dtype) for r in rows]
        return jax.eval_shape(lambda r, p: fn(*r, *p), tiles, list(params))

    def fwd_call(rows, params):
        m_dim = rows[0].shape[0]
        n_in = len(rows) + len(params)
        outs = out_structs(rows, params)

        def body(*refs):
            res = fn(*[r[...] for r in refs[:n_in]])
            for o_ref, r in zip(refs[n_in:], res):
                o_ref[...] = r.astype(o_ref.dtype)

        row_specs, par_specs = specs(rows, params)
        return pl.pallas_call(
            body,
            grid=(m_dim // tm,),
            in_specs=row_specs + par_specs,
            out_specs=[pl.BlockSpec((tm, o.shape[1]), lambda i: (i, 0)) for o in outs],
            out_shape=[jax.ShapeDtypeStruct((m_dim, o.shape[1]), o.dtype) for o in outs],
            compiler_params=_cparams(("parallel",)),
            name=name + "_fwd",
        )(*rows, *params)

    def bwd_call(rows, params, cts):
        m_dim = rows[0].shape[0]
        n_rows, n_par = len(rows), len(params)
        n_in = n_rows + n_par
        n_ct = len(cts)

        def body(*refs):
            vals = [r[...] for r in refs[:n_in]]
            ct_vals = tuple(r[...] for r in refs[n_in:n_in + n_ct])
            drow_refs = refs[n_in + n_ct:n_in + n_ct + n_rows]
            dpar_refs = refs[n_in + n_ct + n_rows:]
            _, pullback = jax.vjp(fn, *vals)
            grads = pullback(ct_vals)
            for d_ref, g in zip(drow_refs, grads[:n_rows]):
                d_ref[...] = g

            @pl.when(pl.program_id(0) == 0)
            def _():
                for d_ref in dpar_refs:
                    d_ref[...] = jnp.zeros_like(d_ref)

            for d_ref, g in zip(dpar_refs, grads[n_rows:]):
                d_ref[...] += g

        row_specs, par_specs = specs(rows, params)
        ct_specs = [pl.BlockSpec((tm, c.shape[1]), lambda i: (i, 0)) for c in cts]
        res = pl.pallas_call(
            body,
            grid=(m_dim // tm,),
            in_specs=row_specs + par_specs + ct_specs,
            out_specs=row_specs + par_specs,
            out_shape=[jax.ShapeDtypeStruct(r.shape, r.dtype) for r in rows]
            + [jax.ShapeDtypeStruct(p.shape, p.dtype) for p in params],
            compiler_params=_cparams(("arbitrary",)),
            name=name + "_bwd",
        )(*rows, *params, *cts)
        return tuple(res[:n_rows]), tuple(res[n_rows:])

    @jax.custom_vjp
    def op(rows, params):
        return tuple(fwd_call(rows, params))

    def op_fwd(rows, params):
        return tuple(fwd_call(rows, params)), (rows, params)

    def op_bwd(res, cts):
        rows, params = res
        return bwd_call(rows, params, tuple(cts))

    op.defvjp(op_fwd, op_bwd)
    return op


def _rms(x, g):
    return x * lax.rsqrt(jnp.mean(x * x, axis=-1, keepdims=True) + RMS_EPS) * g


def _sigmoid(x):
    return 1.0 / (1.0 + jnp.exp(-x))


def _silu(x):
    return x * _sigmoid(x)


def _bdot(a, b, dims=(((1,), (0,)), ((), ()))):
    return lax.dot_general(a.astype(BF16), b.astype(BF16), dims, preferred_element_type=F32)


def _rmsnorm_fn(x, g):
    return (_rms(x, g),)


def rmsnorm(x, g, name):
    return make_rowop(_rmsnorm_fn, name)((x,), (g.reshape(1, -1),))[0]


def _gelu_tanh(x):
    return 0.5 * x * (1.0 + jnp.tanh(0.7978845608028654 * (x + 0.044715 * x * x * x)))


def _softplus(x):
    return jnp.maximum(x, 0.0) + jnp.log(1.0 + jnp.exp(-jnp.abs(x)))


def _s5_post_fn(y, w_glu, b_glu):
    h = _gelu_tanh(y)
    return (h * _sigmoid(_bdot(h, w_glu) + b_glu),)


def _swiglu_fn(g, u):
    return (_silu(g) * u,)


def _loss_fn(y, t):
    err = y - t
    return (0.5 * jnp.mean(err * err, axis=-1, keepdims=True),)


def _pair_headnorm(x, g2):
    lo = lax.broadcasted_iota(jnp.int32, x.shape, 1) < CA_HEAD_DIM
    sq = x * x
    s_lo = jnp.sum(jnp.where(lo, sq, 0.0), axis=-1, keepdims=True)
    s_hi = jnp.sum(jnp.where(lo, 0.0, sq), axis=-1, keepdims=True)
    ms = jnp.where(lo, s_lo, s_hi) * (1.0 / CA_HEAD_DIM)
    return x * lax.rsqrt(ms + RMS_EPS) * g2


def _ca_qknorm_fn(qkv, qg2, kg2):
    qs, ks = [], []
    for j in range(D_MODEL // LANE):
        qs.append(_pair_headnorm(qkv[:, j * LANE:(j + 1) * LANE], qg2))
        ks.append(_pair_headnorm(qkv[:, D_MODEL + j * LANE:D_MODEL + (j + 1) * LANE], kg2))
    return jnp.concatenate(qs, axis=1), jnp.concatenate(ks, axis=1)


def _xattn_fn(q, k, v, qg, kg):
    outs = []
    for h in range(XA_HEADS):
        sl = slice(h * XA_HEAD_DIM, (h + 1) * XA_HEAD_DIM)
        qh = _rms(q[:, sl], qg)
        kh = _rms(k[:, sl], kg)
        s = _bdot(qh, kh, (((1,), (1,)), ((), ()))) * (XA_HEAD_DIM ** -0.5)
        p = jnp.exp(s - jnp.max(s, axis=-1, keepdims=True))
        p = p / jnp.sum(p, axis=-1, keepdims=True)
        outs.append(_bdot(p, v[:, sl]))
    return (jnp.concatenate(outs, axis=1),)


def _gdn_prep_fn(x0, x1, x2, x3, ab, conv_w, alog, dtb):
    c = conv_w[3:4, :] * x0 + conv_w[2:3, :] * x1 + conv_w[1:2, :] * x2 + conv_w[0:1, :] * x3
    c = _silu(c)
    qs, ks = [], []
    for h in range(GDN_HEADS):
        qh = c[:, h * LANE:(h + 1) * LANE]
        kh = c[:, GDN_WIDTH + h * LANE:GDN_WIDTH + (h + 1) * LANE]
        qs.append(qh * lax.rsqrt(jnp.sum(qh * qh, axis=-1, keepdims=True) + RMS_EPS) * (GDN_HEAD_DIM ** -0.5))
        ks.append(kh * lax.rsqrt(jnp.sum(kh * kh, axis=-1, keepdims=True) + RMS_EPS))
    lane = lax.broadcasted_iota(jnp.int32, ab.shape, 1)
    g = -jnp.exp(alog) * _softplus(ab + dtb)
    beta = _sigmoid(ab)
    bg = jnp.where(lane < GDN_HEADS, g, jnp.where(lane < 2 * GDN_HEADS, beta, 0.0))
    return jnp.concatenate(qs, axis=1), jnp.concatenate(ks, axis=1), c[:, 2 * GDN_WIDTH:], bg


def _gdn_out_fn(o, gate, og):
    outs = []
    for h in range(GDN_HEADS):
        sl = slice(h * LANE, (h + 1) * LANE)
        outs.append(_rms(o[:, sl], og) * _silu(gate[:, sl]))
    return (jnp.concatenate(outs, axis=1),)


CA_QB = 4 * CHUNK
CA_KB = CA_QB + CA_PAD


def _ca_math(q2, kb2, vb2, bias2, c):
    lane = lax.broadcasted_iota(jnp.int32, q2.shape, 1)
    qc = lax.broadcasted_iota(jnp.int32, (CA_QB, CA_KB), 0) // CHUNK
    kc = lax.broadcasted_iota(jnp.int32, (CA_QB, CA_KB), 1) // CHUNK
    valid = (kc >= qc) & (kc <= qc + CA_LEFT) & (kc + c * (CA_QB // CHUNK) >= CA_LEFT)
    out = jnp.zeros(q2.shape, F32)
    for h in range(2):
        mine = (lane >= h * CA_HEAD_DIM) & (lane < (h + 1) * CA_HEAD_DIM)
        qh = jnp.where(mine, q2, 0.0)
        s = _bdot(qh, kb2, (((1,), (1,)), ((), ()))) * (CA_HEAD_DIM ** -0.5) + bias2[h]
        s = jnp.where(valid, s, -1e30)
        p = jnp.exp(s - jnp.max(s, axis=-1, keepdims=True))
        p = p / jnp.sum(p, axis=-1, keepdims=True)
        out = out + jnp.where(mine, _bdot(p, vb2), 0.0)
    return out


CA_VEC = CA_QB + CA_KB


def _ca_specs(seq):
    q_spec = pl.BlockSpec((CA_QB, LANE), lambda hp, c: (c, hp))
    kv_spec = pl.BlockSpec((seq + CA_PAD, LANE), lambda hp, c: (0, hp))
    b_spec = pl.BlockSpec((1, 2, CA_VEC), lambda hp, c: (hp, 0, 0))
    return (D_MODEL // LANE, seq // CA_QB), q_spec, kv_spec, b_spec


def _ca_bias_from_vector(vec_ref, bias_ref):
    for h in range(2):
        rows = jnp.broadcast_to(vec_ref[0, h:h + 1, :], (CA_QB, CA_VEC))
        bias_ref[h] = pltpu.roll(rows, 0, 1, stride=1, stride_axis=0)[:, CA_QB:]


def _ca_vector_grad(dbias):
    d = jnp.concatenate([jnp.zeros((CA_QB, CA_QB), F32), dbias], axis=1)
    row = lax.broadcasted_iota(jnp.int32, d.shape, 0)
    for bit in range(CA_QB.bit_length() - 1):
        d = jnp.where((row >> bit) & 1 == 1, pltpu.roll(d, CA_VEC - (1 << bit), 1), d)
    return jnp.sum(d, axis=0, keepdims=True)


def _ca_fwd_call(q, kpad, vpad, vec):
    grid, q_spec, kv_spec, b_spec = _ca_specs(q.shape[0])

    def body(q_ref, k_ref, v_ref, vec_ref, o_ref, bias_ref):
        c = pl.program_id(1)
        start = pl.multiple_of(c * CA_QB, CA_QB)

        @pl.when(c == 0)
        def _():
            _ca_bias_from_vector(vec_ref, bias_ref)

        o_ref[...] = _ca_math(q_ref[...], k_ref[pl.ds(start, CA_KB), :], v_ref[pl.ds(start, CA_KB), :],
                              bias_ref[...], c)

    return pl.pallas_call(
        body, grid=grid, in_specs=[q_spec, kv_spec, kv_spec, b_spec], out_specs=q_spec,
        out_shape=jax.ShapeDtypeStruct(q.shape, F32), scratch_shapes=[pltpu.VMEM((2, CA_QB, CA_KB), F32)],
        compiler_params=_cparams(("parallel", "arbitrary")), name="chunkattn_fwd",
    )(q, kpad, vpad, vec)


def _ca_bwd_call(q, kpad, vpad, vec, do):
    grid, q_spec, kv_spec, b_spec = _ca_specs(q.shape[0])
    last = grid[1] - 1

    def body(q_ref, k_ref, v_ref, vec_ref, do_ref, dq_ref, dk_ref, dv_ref, dvec_ref, bias_ref, dbias_ref):
        c = pl.program_id(1)
        start = pl.multiple_of(c * CA_QB, CA_QB)

        @pl.when(c == 0)
        def _():
            _ca_bias_from_vector(vec_ref, bias_ref)
            dk_ref[...] = jnp.zeros_like(dk_ref)
            dv_ref[...] = jnp.zeros_like(dv_ref)
            dbias_ref[...] = jnp.zeros_like(dbias_ref)

        _, pullback = jax.vjp(lambda a, b, d, e: _ca_math(a, b, d, e, c), q_ref[...],
                              k_ref[pl.ds(start, CA_KB), :], v_ref[pl.ds(start, CA_KB), :], bias_ref[...])
        dq, dkb, dvb, dbias = pullback(do_ref[...])
        dq_ref[...] = dq
        dk_ref[pl.ds(start, CA_KB), :] += dkb
        dv_ref[pl.ds(start, CA_KB), :] += dvb
        dbias_ref[...] += dbias

        @pl.when(c == last)
        def _():
            for h in range(2):
                dvec_ref[0, h:h + 1, :] = _ca_vector_grad(dbias_ref[h])

    return pl.pallas_call(
        body, grid=grid, in_specs=[q_spec, kv_spec, kv_spec, b_spec, q_spec],
        out_specs=[q_spec, kv_spec, kv_spec, b_spec],
        out_shape=[jax.ShapeDtypeStruct(q.shape, F32), jax.ShapeDtypeStruct(kpad.shape, F32),
                   jax.ShapeDtypeStruct(vpad.shape, F32), jax.ShapeDtypeStruct(vec.shape, F32)],
        scratch_shapes=[pltpu.VMEM((2, CA_QB, CA_KB), F32), pltpu.VMEM((2, CA_QB, CA_KB), F32)],
        compiler_params=_cparams(("parallel", "arbitrary")), name="chunkattn_bwd",
    )(q, kpad, vpad, vec, do)


@jax.custom_vjp
def chunk_attn_core(q, kpad, vpad, vec):
    return _ca_fwd_call(q, kpad, vpad, vec)


def _ca_core_fwd(q, kpad, vpad, vec):
    return _ca_fwd_call(q, kpad, vpad, vec), (q, kpad, vpad, vec)


def _ca_core_bwd(res, do):
    return tuple(_ca_bwd_call(*res, do))


chunk_attn_core.defvjp(_ca_core_fwd, _ca_core_bwd)


S5_GB = 4
S5_U = S5_WIDTH // S5_GB
S5_L = S5_GROUPS * S5_STATE // S5_GB


def _cmul(ar, ai, br, bi):
    return ar * br - ai * bi, ar * bi + ai * br


def _hdot(a, b, dims=(((1,), (0,)), ((), ()))):
    return lax.dot_general(a, b, dims, precision=HI, preferred_element_type=F32)


_NT = (((1,), (1,)), ((), ()))
_TN = (((0,), (0,)), ((), ()))


def _s5_tables(lr, li, reverse):
    p = {1: (lr, li)}
    p[2] = _cmul(*p[1], *p[1])
    p[4] = _cmul(*p[2], *p[2])
    p[3] = _cmul(*p[2], *p[1])
    p[5] = _cmul(*p[4], *p[1])
    p[6] = _cmul(*p[4], *p[2])
    p[7] = _cmul(*p[4], *p[3])
    p[8] = _cmul(*p[4], *p[4])
    row = lax.broadcasted_iota(jnp.int32, (SUBLANE, lr.shape[1]), 0)
    tr = jnp.zeros(row.shape, F32)
    ti = jnp.zeros(row.shape, F32)
    for i in range(SUBLANE):
        k = SUBLANE - i if reverse else i + 1
        tr = jnp.where(row == i, p[k][0], tr)
        ti = jnp.where(row == i, p[k][1], ti)
    return p, (tr, ti), row


def _s5_block_scan(xr, xi, p, tab, row, hr, hi, reverse):
    for k in (1, 2, 4):
        if reverse:
            sr = jnp.where(row < SUBLANE - k, pltpu.roll(xr, SUBLANE - k, 0), 0.0)
            si = jnp.where(row < SUBLANE - k, pltpu.roll(xi, SUBLANE - k, 0), 0.0)
        else:
            sr = jnp.where(row >= k, pltpu.roll(xr, k, 0), 0.0)
            si = jnp.where(row >= k, pltpu.roll(xi, k, 0), 0.0)
        ar, ai = _cmul(p[k][0], p[k][1], sr, si)
        xr, xi = xr + ar, xi + ai
    cr, ci = _cmul(tab[0], tab[1], hr, hi)
    return xr + cr, xi + ci


def _s5_forward_scan(sr_ref, si_ref, lr, li):
    n_blocks = sr_ref.shape[0] // SUBLANE
    p, tab, row = _s5_tables(lr, li, False)

    def step(b, carry):
        base = pl.multiple_of(b * SUBLANE, SUBLANE)
        xr, xi = _s5_block_scan(sr_ref[pl.ds(base, SUBLANE), :], si_ref[pl.ds(base, SUBLANE), :],
                                p, tab, row, carry[0], carry[1], False)
        sr_ref[pl.ds(base, SUBLANE), :] = xr
        si_ref[pl.ds(base, SUBLANE), :] = xi
        return xr[SUBLANE - 1:SUBLANE, :], xi[SUBLANE - 1:SUBLANE, :]

    zero = jnp.zeros((1, lr.shape[1]), F32)
    lax.fori_loop(0, n_blocks, step, (zero, zero))


def _s5_specs(seq):
    u_spec = pl.BlockSpec((seq, S5_U), lambda g: (0, g))
    bd_spec = pl.BlockSpec((1, S5_U, S5_L), lambda g: (g, 0, 0))
    cd_spec = pl.BlockSpec((1, S5_L, S5_U), lambda g: (g, 0, 0))
    lam_spec = pl.BlockSpec((1, 2, S5_L), lambda g: (g, 0, 0))
    d_spec = pl.BlockSpec((1, S5_U), lambda g: (0, g))
    return u_spec, bd_spec, cd_spec, lam_spec, d_spec


S5_ROWS = 256


def _row_chunks(seq, fn):
    rows_per = min(S5_ROWS, seq)

    def step(r, carry):
        fn(pl.ds(pl.multiple_of(r * rows_per, rows_per), rows_per))
        return carry

    lax.fori_loop(0, seq // rows_per, step, 0)


def _s5_fwd_call(u, bdr, bdi, cdr, cdi, lam, d):
    seq = u.shape[0]
    u_spec, bd_spec, cd_spec, lam_spec, d_spec = _s5_specs(seq)

    def body(u_ref, bdr_ref, bdi_ref, cdr_ref, cdi_ref, lam_ref, d_ref, y_ref, sr_ref, si_ref):
        def project_in(rows):
            uv = u_ref[rows, :]
            sr_ref[rows, :] = _bdot(uv, bdr_ref[0])
            si_ref[rows, :] = _bdot(uv, bdi_ref[0])

        def project_out(rows):
            y_ref[rows, :] = (_bdot(sr_ref[rows, :], cdr_ref[0]) - _bdot(si_ref[rows, :], cdi_ref[0])
                              + d_ref[...] * u_ref[rows, :])

        _row_chunks(seq, project_in)
        _s5_forward_scan(sr_ref, si_ref, lam_ref[0, 0:1, :], lam_ref[0, 1:2, :])
        _row_chunks(seq, project_out)

    return pl.pallas_call(
        body, grid=(S5_GB,), in_specs=[u_spec, bd_spec, bd_spec, cd_spec, cd_spec, lam_spec, d_spec],
        out_specs=u_spec, out_shape=jax.ShapeDtypeStruct(u.shape, F32),
        scratch_shapes=[pltpu.VMEM((seq, S5_L), F32), pltpu.VMEM((seq, S5_L), F32)],
        compiler_params=_cparams(("parallel",)), name="s5_fwd",
    )(u, bdr, bdi, cdr, cdi, lam, d)


def _s5_bwd_call(u, bdr, bdi, cdr, cdi, lam, d, dy):
    seq = u.shape[0]
    n_blocks = seq // SUBLANE
    u_spec, bd_spec, cd_spec, lam_spec, d_spec = _s5_specs(seq)

    def body(u_ref, bdr_ref, bdi_ref, cdr_ref, cdi_ref, lam_ref, d_ref, dy_ref,
             du_ref, dbdr_ref, dbdi_ref, dcdr_ref, dcdi_ref, dlam_ref, dd_ref, sr_ref, si_ref, gr_ref, gi_ref):
        lr, li = lam_ref[0, 0:1, :], lam_ref[0, 1:2, :]

        def project_in(rows):
            uv = u_ref[rows, :]
            dyv = dy_ref[rows, :]
            sr_ref[rows, :] = _bdot(uv, bdr_ref[0])
            si_ref[rows, :] = _bdot(uv, bdi_ref[0])
            gr_ref[rows, :] = _bdot(dyv, cdr_ref[0], _NT)
            gi_ref[rows, :] = -_bdot(dyv, cdi_ref[0], _NT)

        _row_chunks(seq, project_in)
        _s5_forward_scan(sr_ref, si_ref, lr, li)
        p, tab, row = _s5_tables(lr, -li, True)

        def step(i, carry):
            hr, hi, acc_r, acc_i = carry
            b = n_blocks - 1 - i
            base = pl.multiple_of(b * SUBLANE, SUBLANE)
            xr, xi = _s5_block_scan(gr_ref[pl.ds(base, SUBLANE), :], gi_ref[pl.ds(base, SUBLANE), :],
                                    p, tab, row, hr, hi, True)
            gr_ref[pl.ds(base, SUBLANE), :] = xr
            gi_ref[pl.ds(base, SUBLANE), :] = xi
            prev = pl.multiple_of(jnp.maximum(b - 1, 0) * SUBLANE, SUBLANE)
            keep = (b > 0).astype(F32)
            last_r = sr_ref[pl.ds(prev, SUBLANE), :][SUBLANE - 1:SUBLANE, :] * keep
            last_i = si_ref[pl.ds(prev, SUBLANE), :][SUBLANE - 1:SUBLANE, :] * keep
            pr = jnp.where(row >= 1, pltpu.roll(sr_ref[pl.ds(base, SUBLANE), :], 1, 0), last_r)
            pi = jnp.where(row >= 1, pltpu.roll(si_ref[pl.ds(base, SUBLANE), :], 1, 0), last_i)
            acc_r = acc_r + pr * xr + pi * xi
            acc_i = acc_i + pr * xi - pi * xr
            return xr[0:1, :], xi[0:1, :], acc_r, acc_i

        zero = jnp.zeros((1, S5_L), F32)
        zacc = jnp.zeros((SUBLANE, S5_L), F32)
        _, _, acc_r, acc_i = lax.fori_loop(0, n_blocks, step, (zero, zero, zacc, zacc))
        dlam_ref[0, 0:1, :] = jnp.sum(acc_r, axis=0, keepdims=True)
        dlam_ref[0, 1:2, :] = jnp.sum(acc_i, axis=0, keepdims=True)
        for ref in (dbdr_ref, dbdi_ref, dcdr_ref, dcdi_ref, dd_ref):
            ref[...] = jnp.zeros_like(ref)

        def grads(rows):
            uv, dyv = u_ref[rows, :], dy_ref[rows, :]
            grv, giv = gr_ref[rows, :], gi_ref[rows, :]
            du_ref[rows, :] = _bdot(grv, bdr_ref[0], _NT) + _bdot(giv, bdi_ref[0], _NT) + d_ref[...] * dyv
            dbdr_ref[0] += _bdot(uv, grv, _TN)
            dbdi_ref[0] += _bdot(uv, giv, _TN)
            dcdr_ref[0] += _bdot(sr_ref[rows, :], dyv, _TN)
            dcdi_ref[0] -= _bdot(si_ref[rows, :], dyv, _TN)
            dd_ref[...] += jnp.sum(dyv * uv, axis=0, keepdims=True)

        _row_chunks(seq, grads)

    scratch = [pltpu.VMEM((seq, S5_L), F32) for _ in range(4)]
    return pl.pallas_call(
        body, grid=(S5_GB,),
        in_specs=[u_spec, bd_spec, bd_spec, cd_spec, cd_spec, lam_spec, d_spec, u_spec],
        out_specs=[u_spec, bd_spec, bd_spec, cd_spec, cd_spec, lam_spec, d_spec],
        out_shape=[jax.ShapeDtypeStruct(a.shape, F32) for a in (u, bdr, bdi, cdr, cdi, lam, d)],
        scratch_shapes=scratch, compiler_params=_cparams(("parallel",)), name="s5_bwd",
    )(u, bdr, bdi, cdr, cdi, lam, d, dy)


@jax.custom_vjp
def s5_core(u, bdr, bdi, cdr, cdi, lam, d):
    return _s5_fwd_call(u, bdr, bdi, cdr, cdi, lam, d)


def _s5_core_fwd(*args):
    return _s5_fwd_call(*args), args


def _s5_core_bwd(res, dy):
    return tuple(_s5_bwd_call(*res, dy))


s5_core.defvjp(_s5_core_fwd, _s5_core_bwd)


def _s5_discretize(a_re, a_im, log_dt, b_re, b_im, c_re, c_im, d):
    dt = jnp.exp(log_dt)[:, None]
    mag = jnp.exp(a_re * dt)
    lbr, lbi = mag * jnp.cos(a_im * dt), mag * jnp.sin(a_im * dt)
    den = a_re * a_re + a_im * a_im
    fr = ((lbr - 1.0) * a_re + lbi * a_im) / den
    fi = (lbi * a_re - (lbr - 1.0) * a_im) / den
    bbr = fr[:, :, None] * b_re - fi[:, :, None] * b_im
    bbi = fr[:, :, None] * b_im + fi[:, :, None] * b_re
    eye = jnp.eye(S5_GROUPS // S5_GB, dtype=F32)
    gl = S5_GROUPS // S5_GB

    def bd(t):
        return jnp.einsum('bgpc,gh->bgchp', t.reshape(S5_GB, gl, S5_STATE, S5_GROUP), eye).reshape(S5_GB, S5_U, S5_L)

    def cd(t):
        return jnp.einsum('bgcp,gh->bgphc', t.reshape(S5_GB, gl, S5_GROUP, S5_STATE), eye).reshape(S5_GB, S5_L, S5_U)

    lam = jnp.stack([lbr.reshape(S5_GB, S5_L), lbi.reshape(S5_GB, S5_L)], axis=1)
    return bd(bbr), bd(bbi), cd(c_re), cd(c_im), lam, d.reshape(1, S5_WIDTH)


@jax.custom_vjp
def _unit_lower_solve(neg_a, rhs, tinv):
    return _hdot(tinv, rhs)


def _unit_lower_solve_fwd(neg_a, rhs, tinv):
    x = _hdot(tinv, rhs)
    return x, (x, tinv)


def _unit_lower_solve_bwd(res, dx):
    x, tinv = res
    g = _hdot(tinv, dx, _TN)
    return _hdot(g, x, _NT), g, jnp.zeros_like(tinv)


_unit_lower_solve.defvjp(_unit_lower_solve_fwd, _unit_lower_solve_bwd)


def _unit_lower_inverse(neg_a):
    r = lax.broadcasted_iota(jnp.int32, neg_a.shape, 0)
    c = lax.broadcasted_iota(jnp.int32, neg_a.shape, 1)
    p = (r == c).astype(F32) + neg_a
    npow = _hdot(neg_a, neg_a)
    for _ in range(4):
        y = _hdot(jnp.concatenate([p, npow], axis=0), npow)
        p = p + y[:CHUNK]
        npow = y[CHUNK:]
    return p + _hdot(p, npow)


def _gdn_chunk(q, k, v, g_col, b_col, st, tinv=None):
    r = lax.broadcasted_iota(jnp.int32, (CHUNK, CHUNK), 0)
    c = lax.broadcasted_iota(jnp.int32, (CHUNK, CHUNK), 1)
    eye = (r == c).astype(F32)
    strict = r > c
    causal = r >= c
    g_row = jnp.sum(g_col * eye, axis=0, keepdims=True)
    gcum = jnp.sum(jnp.where(causal, g_row, 0.0), axis=1, keepdims=True)
    gcum_row = jnp.sum(gcum * eye, axis=0, keepdims=True)
    diff = gcum - gcum_row
    decay_strict = jnp.where(strict, jnp.exp(jnp.where(strict, diff, 0.0)), 0.0)
    decay_causal = jnp.where(causal, jnp.exp(jnp.where(causal, diff, 0.0)), 0.0)
    gamma = jnp.exp(gcum)
    g_last = jnp.sum(jnp.where(lax.broadcasted_iota(jnp.int32, (CHUNK, 1), 0) == CHUNK - 1, gcum, 0.0),
                     axis=0, keepdims=True)
    kk = _bdot(k, k, _NT)
    neg_a = -(b_col * kk * decay_strict)
    if tinv is None:
        tinv = _unit_lower_inverse(neg_a)
    x = _unit_lower_solve(neg_a, jnp.concatenate([b_col * v, (b_col * gamma) * k], axis=1), lax.stop_gradient(tinv))
    u_new, w_k = x[:, :GDN_HEAD_DIM], x[:, GDN_HEAD_DIM:]
    qk = _bdot(q, k, _NT) * decay_causal
    q_g = q * gamma
    k_tail = k * jnp.exp(g_last - gcum)
    w = u_new - _bdot(w_k, st)
    o = _bdot(q_g, st) + _bdot(qk, w)
    st_new = jnp.exp(g_last) * st + _bdot(k_tail, w, _TN)
    return o, st_new, tinv


def _gdn_cols(bgv, h):
    lane = lax.broadcasted_iota(jnp.int32, bgv.shape, 1)
    g_col = jnp.sum(jnp.where(lane == h, bgv, 0.0), axis=1, keepdims=True)
    b_col = jnp.sum(jnp.where(lane == GDN_HEADS + h, bgv, 0.0), axis=1, keepdims=True)
    return g_col, b_col


def _gdn_fwd_call(q, k, v, bg):
    seq = q.shape[0]
    n_chunks = seq // CHUNK
    x_spec = pl.BlockSpec((CHUNK, GDN_WIDTH), lambda n: (n, 0))
    bg_spec = pl.BlockSpec((CHUNK, LANE), lambda n: (n, 0))
    st_spec = pl.BlockSpec((1, GDN_WIDTH, GDN_HEAD_DIM), lambda n: (n, 0, 0))
    ti_spec = pl.BlockSpec((1, GDN_HEADS * CHUNK, CHUNK), lambda n: (n, 0, 0))

    def body(q_ref, k_ref, v_ref, bg_ref, o_ref, st_out_ref, ti_out_ref, st_ref):
        @pl.when(pl.program_id(0) == 0)
        def _():
            st_ref[...] = jnp.zeros_like(st_ref)

        bgv = bg_ref[...]
        for h in range(GDN_HEADS):
            sl = slice(h * GDN_HEAD_DIM, (h + 1) * GDN_HEAD_DIM)
            g_col, b_col = _gdn_cols(bgv, h)
            st = st_ref[sl, :]
            st_out_ref[0, sl, :] = st
            o, st_new, tinv = _gdn_chunk(q_ref[:, sl], k_ref[:, sl], v_ref[:, sl], g_col, b_col, st)
            o_ref[:, sl] = o
            st_ref[sl, :] = st_new
            ti_out_ref[0, h * CHUNK:(h + 1) * CHUNK, :] = tinv

    return pl.pallas_call(
        body, grid=(n_chunks,), in_specs=[x_spec, x_spec, x_spec, bg_spec], out_specs=[x_spec, st_spec, ti_spec],
        out_shape=[jax.ShapeDtypeStruct(q.shape, F32),
                   jax.ShapeDtypeStruct((n_chunks, GDN_WIDTH, GDN_HEAD_DIM), F32),
                   jax.ShapeDtypeStruct((n_chunks, GDN_HEADS * CHUNK, CHUNK), F32)],
        scratch_shapes=[pltpu.VMEM((GDN_WIDTH, GDN_HEAD_DIM), F32)],
        compiler_params=_cparams(("arbitrary",)), name="gdn_fwd",
    )(q, k, v, bg)


def _gdn_bwd_call(q, k, v, bg, states, tinvs, do):
    seq = q.shape[0]
    n_chunks = seq // CHUNK
    x_spec = pl.BlockSpec((CHUNK, GDN_WIDTH), lambda i: (n_chunks - 1 - i, 0))
    bg_spec = pl.BlockSpec((CHUNK, LANE), lambda i: (n_chunks - 1 - i, 0))
    st_spec = pl.BlockSpec((1, GDN_WIDTH, GDN_HEAD_DIM), lambda i: (n_chunks - 1 - i, 0, 0))
    ti_spec = pl.BlockSpec((1, GDN_HEADS * CHUNK, CHUNK), lambda i: (n_chunks - 1 - i, 0, 0))

    def body(q_ref, k_ref, v_ref, bg_ref, st_in_ref, ti_ref, do_ref, dq_ref, dk_ref, dv_ref, dbg_ref, dst_ref):
        @pl.when(pl.program_id(0) == 0)
        def _():
            dst_ref[...] = jnp.zeros_like(dst_ref)

        bgv = bg_ref[...]
        lane = lax.broadcasted_iota(jnp.int32, bgv.shape, 1)
        dbg = jnp.zeros(bgv.shape, F32)
        for h in range(GDN_HEADS):
            sl = slice(h * GDN_HEAD_DIM, (h + 1) * GDN_HEAD_DIM)
            g_col, b_col = _gdn_cols(bgv, h)
            tinv = ti_ref[0, h * CHUNK:(h + 1) * CHUNK, :]
            _, pullback = jax.vjp(lambda *a: _gdn_chunk(*a, tinv=tinv)[:2], q_ref[:, sl], k_ref[:, sl], v_ref[:, sl],
                                  g_col, b_col, st_in_ref[0, sl, :])
            dq, dk, dv, dg, db, dst = pullback((do_ref[:, sl], dst_ref[sl, :]))
            dq_ref[:, sl] = dq
            dk_ref[:, sl] = dk
            dv_ref[:, sl] = dv
            dst_ref[sl, :] = dst
            dbg = dbg + jnp.where(lane == h, dg, 0.0) + jnp.where(lane == GDN_HEADS + h, db, 0.0)
        dbg_ref[...] = dbg

    return pl.pallas_call(
        body, grid=(n_chunks,), in_specs=[x_spec, x_spec, x_spec, bg_spec, st_spec, ti_spec, x_spec],
        out_specs=[x_spec, x_spec, x_spec, bg_spec],
        out_shape=[jax.ShapeDtypeStruct(q.shape, F32)] * 3 + [jax.ShapeDtypeStruct(bg.shape, F32)],
        scratch_shapes=[pltpu.VMEM((GDN_WIDTH, GDN_HEAD_DIM), F32)],
        compiler_params=_cparams(("arbitrary",)), name="gdn_bwd",
    )(q, k, v, bg, states, tinvs, do)


@jax.custom_vjp
def gdn_core(q, k, v, bg):
    return _gdn_fwd_call(q, k, v, bg)[0]


def _gdn_core_fwd(q, k, v, bg):
    o, states, tinvs = _gdn_fwd_call(q, k, v, bg)
    return o, (q, k, v, bg, states, tinvs)


def _gdn_core_bwd(res, do):
    return tuple(_gdn_bwd_call(*res, do))


gdn_core.defvjp(_gdn_core_fwd, _gdn_core_bwd)


def _row(v):
    return v.reshape(1, -1)


def _lane_pad(v):
    return jnp.pad(v, (0, LANE - v.shape[0])).reshape(1, LANE)


def _delay_rows(x, k):
    return jnp.pad(x, ((k, 0), (0, 0)))[:x.shape[0]]


def s5_mixer(u, a_re, a_im, log_dt, b_re, b_im, c_re, c_im, d, w_glu, b_glu):
    y = s5_core(u, *_s5_discretize(a_re, a_im, log_dt, b_re, b_im, c_re, c_im, d))
    return make_rowop(_s5_post_fn, "s5_post")((y,), (w_glu, _row(b_glu)))[0]


def gated_deltanet(qkv, gate, ab, conv_w, a_log, dt_bias, out_g):
    rows = (qkv, _delay_rows(qkv, 1), _delay_rows(qkv, 2), _delay_rows(qkv, 3), ab)
    q, k, v, bg = make_rowop(_gdn_prep_fn, "gdn_prep")(rows, (conv_w, _lane_pad(a_log), _lane_pad(dt_bias)))
    o = gdn_core(q, k, v, bg)
    return make_rowop(_gdn_out_fn, "gdn_out")((o, gate), (_row(out_g),))[0]


def chunk_attention(h, w_qkv, w_out, q_g, k_g, rel_bias):
    qkv = linear_cols(h, w_qkv)
    qn, kn = make_rowop(_ca_qknorm_fn, "ca_qknorm")((qkv,), (_row(jnp.tile(q_g, 2)), _row(jnp.tile(k_g, 2))))
    kpad = jnp.pad(kn, ((CA_PAD, 0), (0, 0)))
    vpad = jnp.pad(qkv[:, 2 * D_MODEL:], ((CA_PAD, 0), (0, 0)))
    o = chunk_attn_core(qn, kpad, vpad, _rel_bias_vector(rel_bias))
    return linear(o, w_out)


def memory_cross_attention(h, mem_n, w_q, w_kv, w_out, q_g, k_g):
    q = linear(h, w_q)
    kv = linear_cols(mem_n, w_kv)
    o = make_rowop(_xattn_fn, "xattn")((q,), (kv[:, :D_MODEL], kv[:, D_MODEL:], _row(q_g), _row(k_g)))[0]
    return linear(o, w_out)


def swiglu(h, w_gate, w_up, w_down):
    g = linear_cols_dm(h, w_gate)
    u = linear_cols_dm(h, w_up)
    rows = g.shape[0] * g.shape[1]
    a = make_rowop(_swiglu_fn, "swiglu")((g.reshape(rows, -1), u.reshape(rows, -1)), ())[0]
    return linear_rows_dm(a.reshape(g.shape), w_down)


def _rel_bias_vector(rel_bias):
    heads = rel_bias.shape[0]
    n_far = CA_KB - 1 - MAX_REL
    n_neg = CA_VEC - 1 - n_far - (2 * MAX_REL + 1)
    vec = jnp.concatenate([jnp.zeros((heads, 1), F32),
                           jnp.broadcast_to(rel_bias[:, 2 * MAX_REL:], (heads, n_far)),
                           jnp.flip(rel_bias, axis=1),
                           jnp.broadcast_to(rel_bias[:, :1], (heads, n_neg))], axis=1)
    return vec.reshape(heads // 2, 2, CA_VEC)


def _exchange(arrays, modes, name):
    n = len(arrays)
    out_shapes = [jax.ShapeDtypeStruct((N_DEV,) + a.shape if m == "gather" else a.shape, a.dtype)
                  for a, m in zip(arrays, modes)]

    def body(*refs):
        ins, outs = refs[:n], refs[n:2 * n]
        send_sems, recv_sems, local_sems = refs[2 * n:]
        x, y, c = lax.axis_index("x"), lax.axis_index("y"), lax.axis_index("c")
        me = 4 * x + 2 * y + c
        pending = []
        for i in range(n):
            gather = modes[i] == "gather"
            local = pltpu.make_async_copy(ins[i] if gather else ins[i].at[me], outs[i].at[me], local_sems.at[i])
            local.start()
            pending.append(local)
        for k in range(1, N_DEV):
            px, py, pc = (x + (k >> 2)) % 2, (y + ((k >> 1) & 1)) % 2, (c + (k & 1)) % 2
            peer = 4 * px + 2 * py + pc
            for i in range(n):
                src = ins[i] if modes[i] == "gather" else ins[i].at[peer]
                sem = i * (N_DEV - 1) + k - 1
                send = pltpu.make_async_remote_copy(src_ref=src, dst_ref=outs[i].at[me], send_sem=send_sems.at[sem],
                                                    recv_sem=recv_sems.at[sem], device_id=(px, py, pc),
                                                    device_id_type=pl.DeviceIdType.MESH)
                send.start()
                arrival = pltpu.make_async_remote_copy(src_ref=src, dst_ref=outs[i].at[peer],
                                                       send_sem=send_sems.at[sem], recv_sem=recv_sems.at[sem],
                                                       device_id=(px, py, pc), device_id_type=pl.DeviceIdType.MESH)
                pending.append((send, arrival))
        for item in pending:
            if isinstance(item, tuple):
                item[0].wait_send()
                item[1].wait_recv()
            else:
                item.wait()

    any_spec = pl.BlockSpec(memory_space=pl.ANY)
    return pl.pallas_call(
        body, in_specs=[any_spec] * n, out_specs=[any_spec] * n, out_shape=out_shapes,
        scratch_shapes=[pltpu.SemaphoreType.DMA((n * (N_DEV - 1),)), pltpu.SemaphoreType.DMA((n * (N_DEV - 1),)),
                        pltpu.SemaphoreType.DMA((n,))],
        name=name,
    )(*arrays)


_HBM_SPEC = pl.BlockSpec(memory_space=pltpu.HBM)
_SEM_SPEC = pl.BlockSpec(memory_space=pltpu.SEMAPHORE)
_SIDE_EFFECT = pltpu.SideEffectType.DATAFLOW_SIDE_EFFECTING


def _peer(x, y, c, k):
    return (x + (k >> 2)) % 2, (y + ((k >> 1) & 1)) % 2, (c + (k & 1)) % 2


def _exchange_start(arrays, modes, after, name):
    n = len(arrays)
    n_sem = n * (N_DEV - 1)
    lands = [pltpu.with_memory_space_constraint(lax.empty((N_DEV,) + a.shape if m == "gather" else a.shape, a.dtype),
                                                pltpu.HBM) for a, m in zip(arrays, modes)]
    arrays = [pltpu.with_memory_space_constraint(a, pltpu.HBM) for a in arrays]

    def body(*refs):
        ins, zones = refs[:n], refs[n:2 * n]
        send_sems, recv_sems, own_sems = refs[2 * n + 1:2 * n + 4]
        token_ref = refs[4 * n + 4]
        x, y, c = lax.axis_index("x"), lax.axis_index("y"), lax.axis_index("c")
        me = 4 * x + 2 * y + c
        for i in range(n):
            pltpu.make_async_copy(ins[i] if modes[i] == "gather" else ins[i].at[me], zones[i].at[me],
                                  own_sems.at[i]).start()
        for k in range(1, N_DEV):
            px, py, pc = _peer(x, y, c, k)
            peer = 4 * px + 2 * py + pc
            for i in range(n):
                sem = i * (N_DEV - 1) + k - 1
                pltpu.make_async_remote_copy(src_ref=ins[i] if modes[i] == "gather" else ins[i].at[peer],
                                             dst_ref=zones[i].at[me], send_sem=send_sems.at[sem],
                                             recv_sem=recv_sems.at[sem], device_id=(px, py, pc),
                                             device_id_type=pl.DeviceIdType.MESH).start()
        token_ref[...] = jnp.zeros_like(token_ref)

    out_shape = ((pltpu.SemaphoreType.DMA((n_sem,)), pltpu.SemaphoreType.DMA((n_sem,)), pltpu.SemaphoreType.DMA((n,)))
                 + tuple(pltpu.HBM(a.shape, a.dtype) for a in arrays) + tuple(pltpu.HBM(z.shape, z.dtype) for z in lands)
                 + (jax.ShapeDtypeStruct((SUBLANE, LANE), F32),))
    res = pl.pallas_call(
        body, name=name, out_shape=out_shape,
        in_specs=[_HBM_SPEC] * (2 * n) + [pl.BlockSpec(memory_space=pl.ANY)],
        out_specs=(_SEM_SPEC,) * 3 + (_HBM_SPEC,) * (2 * n) + (pl.BlockSpec(memory_space=pltpu.VMEM),),
        input_output_aliases={i: 3 + i for i in range(2 * n)},
        compiler_params=pltpu.CompilerParams(has_side_effects=_SIDE_EFFECT),
    )(*arrays, *lands, after)
    return tuple(res[:3]), list(res[3:3 + n]), list(res[3 + n:3 + 2 * n]), res[3 + 2 * n]


def _exchange_wait(started, modes, after, name):
    sems, sources, zones, _ = started
    n = len(sources)

    def body(*refs):
        ins, lands = refs[:n], refs[n:2 * n]
        send_ref, recv_ref, own_ref = refs[2 * n:2 * n + 3]
        x, y, c = lax.axis_index("x"), lax.axis_index("y"), lax.axis_index("c")
        me = 4 * x + 2 * y + c
        for i in range(n):
            pltpu.make_async_copy(ins[i] if modes[i] == "gather" else ins[i].at[me], lands[i].at[me],
                                  own_ref.at[i]).wait()
        for k in range(1, N_DEV):
            px, py, pc = _peer(x, y, c, k)
            peer = 4 * px + 2 * py + pc
            for i in range(n):
                sem = i * (N_DEV - 1) + k - 1
                cp = pltpu.make_async_remote_copy(src_ref=ins[i] if modes[i] == "gather" else ins[i].at[peer],
                                                  dst_ref=lands[i].at[peer], send_sem=send_ref.at[sem],
                                                  recv_sem=recv_ref.at[sem], device_id=(px, py, pc),
                                                  device_id_type=pl.DeviceIdType.MESH)
                cp.wait_send()
                cp.wait_recv()

    res = pl.pallas_call(
        body, name=name,
        out_shape=tuple(pltpu.HBM(a.shape, a.dtype) for a in sources) + tuple(pltpu.HBM(z.shape, z.dtype) for z in zones),
        in_specs=[_HBM_SPEC] * (2 * n) + [_SEM_SPEC] * 3 + [pl.BlockSpec(memory_space=pl.ANY)],
        out_specs=(_HBM_SPEC,) * (2 * n), input_output_aliases={i: i for i in range(2 * n)},
        compiler_params=pltpu.CompilerParams(has_side_effects=_SIDE_EFFECT),
    )(*sources, *zones, *sems, after)
    return list(res[n:])


ADAM_TILE = 64 * 1024


def _adam_call(w, m, v, slots, name):
    n_layers, rows, cols = w.shape
    tr = rows
    if n_layers * rows * cols > ADAM_TILE:
        fits = [t for t in range(SUBLANE, rows, SUBLANE) if rows % t == 0 and n_layers * t * cols <= ADAM_TILE]
        tr = max(fits) if fits else SUBLANE
    c1 = 1.0 - ADAM_B1 ** ADAM_STEP
    c2 = 1.0 - ADAM_B2 ** ADAM_STEP

    def body(*refs):
        w_ref, m_ref, v_ref = refs[:3]
        slot_refs = refs[3:3 + n_layers]
        grad_ref, delta_ref, nm_ref, nv_ref = refs[3 + n_layers:]
        for layer in range(n_layers):
            g = slot_refs[layer][0].astype(F32)
            for k in range(1, N_DEV):
                g = g + slot_refs[layer][k].astype(F32)
            m_new = ADAM_B1 * m_ref[layer] + (1.0 - ADAM_B1) * g
            v_new = ADAM_B2 * v_ref[layer] + (1.0 - ADAM_B2) * (g * g)
            m_hat = m_new / c1
            v_hat = v_new / c2
            grad_ref[layer] = g
            delta_ref[layer] = -ADAM_LR * (m_hat / (jnp.sqrt(v_hat) + ADAM_EPS) + ADAM_WD * w_ref[layer])
            nm_ref[layer] = m_new
            nv_ref[layer] = v_new

    spec = pl.BlockSpec((n_layers, tr, cols), lambda i: (0, i, 0))
    slot_spec = pl.BlockSpec((N_DEV, tr, cols), lambda i: (0, i, 0))
    return pl.pallas_call(
        body, grid=(rows // tr,), in_specs=[spec, spec, spec] + [slot_spec] * n_layers,
        out_specs=[spec] * 4, out_shape=[jax.ShapeDtypeStruct(w.shape, F32)] * 4,
        compiler_params=_cparams(("parallel",)), name=name,
    )(w, m, v, *slots)


WEIGHT_NAMES = ['ab_norm_g', 'ab_w_in', 'ab_w_out', 's5_a_re', 's5_a_im', 's5_log_dt', 's5_b_re', 's5_b_im', 's5_c_re',
                's5_c_im', 's5_d', 's5_w_glu', 's5_b_glu', 'gdn_conv_w', 'gdn_a_log', 'gdn_dt_bias', 'gdn_out_norm_g',
                'c_norm_g', 'c_w_qkv', 'c_w_out', 'c_q_norm_g', 'c_k_norm_g', 'c_rel_bias', 'mem_norm_g', 'xa_norm_g',
                'xa_w_q', 'xa_w_kv', 'xa_w_out', 'xa_q_norm_g', 'xa_k_norm_g', 'f_norm_g', 'f_w_gate', 'f_w_up',
                'f_w_down']

SHARDED = {
    'ab_w_in': ('col', BF16), 'ab_w_out': ('row', BF16), 's5_w_glu': ('row', BF16), 'gdn_conv_w': ('col', F32),
    'c_norm_g': ('col', F32), 'c_w_qkv': ('col', BF16), 'c_w_out': ('row', BF16), 'xa_w_q': ('row', BF16),
    'xa_w_kv': ('col', BF16), 'xa_w_out': ('row', BF16), 'f_w_gate': ('col', BF16), 'f_w_up': ('col', BF16),
    'f_w_down': ('row', BF16),
}
GATHERED_AS_IS = ('c_w_qkv', 'xa_w_kv', 'f_w_gate', 'f_w_up', 'f_w_down')
REPLICATED = [n for n in WEIGHT_NAMES if n not in SHARDED]
PACK_UNIT = SUBLANE * LANE


def _full_from_gathered(g, axis):
    if axis == "row":
        return g.reshape(g.shape[0] * g.shape[1], g.shape[2])
    return jnp.transpose(g, (1, 0, 2)).reshape(g.shape[1], g.shape[0] * g.shape[2])


def _pack(arrays):
    flat = []
    for a in arrays:
        size = a.size
        padded = -(-size // PACK_UNIT) * PACK_UNIT
        flat.append(jnp.pad(a.reshape(-1), (0, padded - size)).reshape(-1, LANE))
    return jnp.concatenate(flat, axis=0)


def _unpack(buf, shapes):
    out, row = [], 0
    for shape in shapes:
        size = math.prod(shape)
        rows = -(-size // PACK_UNIT) * SUBLANE
        out.append(buf[row:row + rows].reshape(-1)[:size].reshape(shape))
        row += rows
    return out


N_STAGES = 2 * DEPTH
EVEN_SHARDED = ['ab_w_in', 'ab_w_out', 's5_w_glu', 'gdn_conv_w']
ODD_SHARDED = ['c_norm_g', 'c_w_qkv', 'c_w_out']
ALL_SHARDED = ['xa_w_q', 'xa_w_kv', 'xa_w_out', 'f_w_gate', 'f_w_up', 'f_w_down']
EVEN_SMALL = ['ab_norm_g', 's5_a_re', 's5_a_im', 's5_log_dt', 's5_b_re', 's5_b_im', 's5_c_re', 's5_c_im', 's5_d',
              's5_b_glu', 'gdn_a_log', 'gdn_dt_bias', 'gdn_out_norm_g']
ODD_SMALL = ['c_q_norm_g', 'c_k_norm_g', 'c_rel_bias']
ALL_SMALL = ['xa_norm_g', 'xa_q_norm_g', 'xa_k_norm_g', 'f_norm_g']


def _stage_params(stage):
    layer, part = divmod(stage, 2)
    if part == 1:
        return [(n, layer) for n in ALL_SHARDED], [(n, layer) for n in ALL_SMALL]
    big, small = (EVEN_SHARDED, EVEN_SMALL) if layer % 2 == 0 else (ODD_SHARDED, ODD_SMALL)
    return [(n, layer // 2) for n in big], [(n, layer // 2) for n in small]


def _stage_forward(stage, landed, small, x, mem_n):
    layer, part = divmod(stage, 2)
    big = {}
    for (n, _), g in zip(_stage_params(stage)[0], landed):
        if n == 'ab_w_in':
            big[n] = cols_to_natural(g, AB_IN_PAD)
        elif n in GATHERED_AS_IS:
            big[n] = g
        elif n == 's5_w_glu':
            big[n] = _full_from_gathered(g, 'row').astype(F32)
        else:
            big[n] = _full_from_gathered(g, SHARDED[n][0])
    if part == 1:
        h = rmsnorm(x, small['xa_norm_g'], "xa_norm")
        x = x + memory_cross_attention(h, mem_n, big['xa_w_q'], big['xa_w_kv'], big['xa_w_out'],
                                       small['xa_q_norm_g'], small['xa_k_norm_g'])
        h = rmsnorm(x, small['f_norm_g'], "f_norm")
        return x + swiglu(h, big['f_w_gate'], big['f_w_up'], big['f_w_down'])
    if layer % 2 == 0:
        h = rmsnorm(x, small['ab_norm_g'], "ab_norm")
        proj = linear(h, big['ab_w_in'])
        u = proj[:, :S5_WIDTH]
        qkv = proj[:, S5_WIDTH:S5_WIDTH + 3 * GDN_WIDTH]
        gate = proj[:, S5_WIDTH + 3 * GDN_WIDTH:S5_WIDTH + 4 * GDN_WIDTH]
        ab = proj[:, S5_WIDTH + 4 * GDN_WIDTH:]
        a_out = s5_mixer(u, small['s5_a_re'], small['s5_a_im'], small['s5_log_dt'], small['s5_b_re'], small['s5_b_im'],
                         small['s5_c_re'], small['s5_c_im'], small['s5_d'], big['s5_w_glu'], small['s5_b_glu'])
        b_out = gated_deltanet(qkv, gate, ab, big['gdn_conv_w'], small['gdn_a_log'], small['gdn_dt_bias'],
                               small['gdn_out_norm_g'])
        return x + linear(jnp.concatenate([a_out, b_out], axis=1), big['ab_w_out'])
    h = rmsnorm(x, big['c_norm_g'].reshape(-1), "c_norm")
    return x + chunk_attention(h, big['c_w_qkv'], big['c_w_out'], small['c_q_norm_g'], small['c_k_norm_g'],
                               small['c_rel_bias'])


def _loss_rows(x, target):
    return jnp.sum(make_rowop(_loss_fn, "loss")((x, target), ())[0])


def kernel(x, mem, ab_norm_g, ab_w_in, ab_w_out, s5_a_re, s5_a_im, s5_log_dt, s5_b_re, s5_b_im, s5_c_re, s5_c_im, s5_d, s5_w_glu, s5_b_glu, gdn_conv_w, gdn_a_log, gdn_dt_bias, gdn_out_norm_g, c_norm_g, c_w_qkv, c_w_out, c_q_norm_g, c_k_norm_g, c_rel_bias, mem_norm_g, xa_norm_g, xa_w_q, xa_w_kv, xa_w_out, xa_q_norm_g, xa_k_norm_g, f_norm_g, f_w_gate, f_w_up, f_w_down, loss_target, m_ab_norm_g, m_ab_w_in, m_ab_w_out, m_s5_a_re, m_s5_a_im, m_s5_log_dt, m_s5_b_re, m_s5_b_im, m_s5_c_re, m_s5_c_im, m_s5_d, m_s5_w_glu, m_s5_b_glu, m_gdn_conv_w, m_gdn_a_log, m_gdn_dt_bias, m_gdn_out_norm_g, m_c_norm_g, m_c_w_qkv, m_c_w_out, m_c_q_norm_g, m_c_k_norm_g, m_c_rel_bias, m_mem_norm_g, m_xa_norm_g, m_xa_w_q, m_xa_w_kv, m_xa_w_out, m_xa_q_norm_g, m_xa_k_norm_g, m_f_norm_g, m_f_w_gate, m_f_w_up, m_f_w_down, v_ab_norm_g, v_ab_w_in, v_ab_w_out, v_s5_a_re, v_s5_a_im, v_s5_log_dt, v_s5_b_re, v_s5_b_im, v_s5_c_re, v_s5_c_im, v_s5_d, v_s5_w_glu, v_s5_b_glu, v_gdn_conv_w, v_gdn_a_log, v_gdn_dt_bias, v_gdn_out_norm_g, v_c_norm_g, v_c_w_qkv, v_c_w_out, v_c_q_norm_g, v_c_k_norm_g, v_c_rel_bias, v_mem_norm_g, v_xa_norm_g, v_xa_w_q, v_xa_w_kv, v_xa_w_out, v_xa_q_norm_g, v_xa_k_norm_g, v_f_norm_g, v_f_w_gate, v_f_w_up, v_f_w_down):
    given = dict(locals())
    no_after = jnp.zeros((SUBLANE, LANE), F32)

    def shard(n, idx):
        a = given[n][idx]
        return (a.reshape(1, -1) if a.ndim == 1 else a).astype(SHARDED[n][1])

    def gather_start(stage, after):
        arrays = [shard(n, idx) for n, idx in _stage_params(stage)[0]]
        return _exchange_start(arrays, ["gather"] * len(arrays), after, "gather_start_%d" % stage)

    act = x[0]
    mem_n, mem_pullback = jax.vjp(lambda m, g: rmsnorm(m, g, "mem_norm"), mem[0], mem_norm_g)
    started = gather_start(0, no_after)
    pullbacks = []
    for stage in range(N_STAGES):
        landed = _exchange_wait(started, ["gather"] * len(started[1]), act, "gather_wait_%d" % stage)
        if stage + 1 < N_STAGES:
            started = gather_start(stage + 1, landed[0])
            act = act + started[3][0, 0]
        small = {n: given[n][idx] for n, idx in _stage_params(stage)[1]}
        act, pullback = jax.vjp(functools.partial(_stage_forward, stage), landed, small, act, mem_n)
        pullbacks.append(pullback)
    loss_local, loss_pullback = jax.vjp(_loss_rows, act, loss_target[0])
    d_act = loss_pullback(jnp.ones((), F32))[0]

    d_mem_n = jnp.zeros_like(mem_n)
    g_small = {}
    received = [None] * N_STAGES
    started, after = None, no_after
    for stage in reversed(range(N_STAGES)):
        d_landed, d_small, d_act, d_mem = pullbacks[stage](d_act)
        if stage % 2 == 1:
            d_mem_n = d_mem_n + d_mem
        for n, idx in _stage_params(stage)[1]:
            g_small[(n, idx)] = d_small[n]
        if started is not None:
            received[stage + 1] = _exchange_wait(started, ["scatter"] * len(started[1]), d_act,
                                                 "scatter_wait_%d" % (stage + 1))
            after = received[stage + 1][0]
        started = _exchange_start(list(d_landed), ["scatter"] * len(d_landed), after, "scatter_start_%d" % stage)
        d_act = d_act + started[3][0, 0]
    g_small[('mem_norm_g', None)] = mem_pullback(d_mem_n)[1]

    def small_grad(n):
        if n == 'mem_norm_g':
            return g_small[(n, None)]
        return jnp.stack([g_small[(n, i)] for i in range(given[n].shape[0])], axis=0)

    packed = _exchange([_pack([small_grad(n) for n in REPLICATED])], ["gather"], "small_grads_allgather")[0]
    received[0] = _exchange_wait(started, ["scatter"] * len(started[1]), packed, "scatter_wait_0")

    results = {}
    for n in SHARDED:
        slots = {}
        for stage in range(N_STAGES):
            for (pn, idx), r in zip(_stage_params(stage)[0], received[stage]):
                if pn == n:
                    slots[idx] = r
        shape = given[n].shape
        to3d = lambda a: a.reshape(a.shape[0], -1, a.shape[-1])
        outs = _adam_call(to3d(given[n]), to3d(given['m_' + n]), to3d(given['v_' + n]),
                          [slots[i] for i in range(len(slots))], "adamw_" + n)
        results[n] = [o.reshape(shape) for o in outs]
    outs = _adam_call(_pack([given[n] for n in REPLICATED])[None], _pack([given['m_' + n] for n in REPLICATED])[None],
                      _pack([given['v_' + n] for n in REPLICATED])[None], [packed], "adamw_replicated")
    shapes = [given[n].shape for n in REPLICATED]
    for j, parts in enumerate(zip(*[_unpack(o[0], shapes) for o in outs])):
        results[REPLICATED[j]] = list(parts)

    loss = lax.psum(loss_local, ("x", "y", "c"))
    return (loss, d_act[None], *[results[n][0] for n in WEIGHT_NAMES], *[results[n][1] for n in WEIGHT_NAMES],
            *[results[n][2] for n in WEIGHT_NAMES], *[results[n][3] for n in WEIGHT_NAMES])
```

```python
import functools
import math

import jax
import jax.numpy as jnp
import numpy as np
from jax import lax
from jax.experimental import pallas as pl
from jax.experimental.pallas import tpu as pltpu

F32 = jnp.float32
BF16 = jnp.bfloat16
HI = lax.Precision.HIGHEST

N_DEV = 8
D_MODEL = 1024
SEQ = 2048
DEPTH = 4
CHUNK = 64
N_MEM = 256
RMS_EPS = 1e-6
S5_WIDTH = 512
S5_GROUP = 16
S5_GROUPS = 32
S5_STATE = 64
GDN_HEAD_DIM = 128
GDN_WIDTH = 512
GDN_HEADS = 4
GDN_CONV = 4
AB_IN = S5_WIDTH + 4 * GDN_WIDTH + 2 * GDN_HEADS
AB_IN_PAD = 2688
CA_HEADS = 16
CA_HEAD_DIM = 64
CA_LEFT = 8
CA_BAND = (CA_LEFT + 1) * CHUNK
CA_PAD = CA_LEFT * CHUNK
MAX_REL = 128
XA_HEADS = 4
XA_HEAD_DIM = 256
FFN = 2816
ADAM_LR, ADAM_B1, ADAM_B2, ADAM_EPS, ADAM_WD, ADAM_STEP = 0.001, 0.9, 0.999, 1e-08, 0.01, 10

VMEM_LIMIT = 48 * 1024 * 1024
LANE = 128
SUBLANE = 8


def _cparams(sem=None):
    return pltpu.CompilerParams(dimension_semantics=sem, vmem_limit_bytes=VMEM_LIMIT)


def _divisor_tile(n, target, unit=LANE):
    if n <= target:
        return n
    best = None
    for t in range(unit, target + 1, unit):
        if n % t == 0:
            best = t
    assert best is not None, (n, target)
    return best


def _matmul(a, b, *, ta=False, tb=False, out_dtype=F32, name="mm", res=None):
    if ta:
        k_dim, m_dim = a.shape
    else:
        m_dim, k_dim = a.shape
    if tb:
        n_dim, kb = b.shape
    else:
        kb, n_dim = b.shape
    assert kb == k_dim, (a.shape, b.shape, ta, tb)
    tm = _divisor_tile(m_dim, 1024)
    tn = _divisor_tile(n_dim, 512)
    tk = _divisor_tile(k_dim, 1408)
    nk = k_dim // tk
    dims = (((0 if ta else 1,), (1 if tb else 0,)), ((), ()))

    def body(a_ref, b_ref, *rest):
        res_ref = rest[0] if res is not None else None
        o_ref, acc_ref = rest[-2:]
        k = pl.program_id(2)

        @pl.when(k == 0)
        def _():
            acc_ref[...] = jnp.zeros_like(acc_ref)

        acc_ref[...] += lax.dot_general(a_ref[...].astype(BF16), b_ref[...].astype(BF16), dims,
                                        preferred_element_type=F32)

        @pl.when(k == nk - 1)
        def _():
            total = acc_ref[...] if res is None else acc_ref[...] + res_ref[...]
            o_ref[...] = total.astype(o_ref.dtype)

    a_spec = pl.BlockSpec((tk, tm), lambda i, j, k: (k, i)) if ta else pl.BlockSpec((tm, tk), lambda i, j, k: (i, k))
    b_spec = pl.BlockSpec((tn, tk), lambda i, j, k: (j, k)) if tb else pl.BlockSpec((tk, tn), lambda i, j, k: (k, j))
    o_spec = pl.BlockSpec((tm, tn), lambda i, j, k: (i, j))
    return pl.pallas_call(
        body,
        grid=(m_dim // tm, n_dim // tn, nk),
        in_specs=[a_spec, b_spec] + ([o_spec] if res is not None else []),
        out_specs=o_spec,
        out_shape=jax.ShapeDtypeStruct((m_dim, n_dim), out_dtype),
        scratch_shapes=[pltpu.VMEM((tm, tn), F32)],
        compiler_params=_cparams(("parallel", "parallel", "arbitrary")),
        name=name,
    )(*((a, b) if res is None else (a, b, res)))


@jax.custom_vjp
def linear(a, w):
    return _matmul(a, w, name="linear_fwd")


def _linear_fwd(a, w):
    return _matmul(a, w, name="linear_fwd"), (a, w)


def _linear_bwd(res, dy):
    a, w = res
    da = _matmul(dy, w, tb=True, name="linear_da")
    dw = _matmul(a, dy, ta=True, out_dtype=w.dtype, name="linear_dw")
    return da, dw


linear.defvjp(_linear_fwd, _linear_bwd)


@jax.custom_vjp
def linear_res(a, w, x):
    return _matmul(a, w, name="linear_res_fwd", res=x)


def _linear_res_fwd(a, w, x):
    return _matmul(a, w, name="linear_res_fwd", res=x), (a, w)


def _linear_res_bwd(res, dy):
    return _linear_bwd(res, dy) + (dy,)


linear_res.defvjp(_linear_res_fwd, _linear_res_bwd)


def _mm_call(name, a, b, out_struct, grid, a_spec, b_spec, o_spec, dims, lead, res=None):
    nk = grid[-1]
    acc_shape = o_spec.block_shape[1:] if lead[2] else o_spec.block_shape

    def body(a_ref, b_ref, *rest):
        res_ref = rest[0] if res is not None else None
        o_ref, acc_ref = rest[-2:]
        k = pl.program_id(len(grid) - 1)

        @pl.when(k == 0)
        def _():
            acc_ref[...] = jnp.zeros_like(acc_ref)

        av = a_ref[0] if lead[0] else a_ref[...]
        bv = b_ref[0] if lead[1] else b_ref[...]
        acc_ref[...] += lax.dot_general(av.astype(BF16), bv.astype(BF16), dims, preferred_element_type=F32)

        @pl.when(k == nk - 1)
        def _():
            if lead[2]:
                o_ref[0] = acc_ref[...].astype(o_ref.dtype)
            elif res is not None:
                o_ref[...] = (acc_ref[...] + res_ref[...]).astype(o_ref.dtype)
            else:
                o_ref[...] = acc_ref[...].astype(o_ref.dtype)

    return pl.pallas_call(
        body, grid=grid, in_specs=[a_spec, b_spec] + ([o_spec] if res is not None else []), out_specs=o_spec,
        out_shape=out_struct, scratch_shapes=[pltpu.VMEM(tuple(acc_shape), F32)],
        compiler_params=_cparams(("parallel", "parallel", "arbitrary")), name=name,
    )(*((a, b) if res is None else (a, b, res)))


_NN = (((1,), (0,)), ((), ()))
_NT_DIMS = (((1,), (1,)), ((), ()))
_TN_DIMS = (((0,), (0,)), ((), ()))


def _cols_fwd(a, g, dm_out):
    m_dim, k_dim = a.shape
    _, _, c_dim = g.shape
    tm = _divisor_tile(m_dim, 1024)
    tk = _divisor_tile(k_dim, 1024)
    a_spec = pl.BlockSpec((tm, tk), lambda i, j, k: (i, k))
    b_spec = pl.BlockSpec((1, tk, c_dim), lambda i, j, k: (j, k, 0))
    if dm_out:
        o_spec = pl.BlockSpec((1, tm, c_dim), lambda i, j, k: (j, i, 0))
        out = jax.ShapeDtypeStruct((N_DEV, m_dim, c_dim), F32)
    else:
        o_spec = pl.BlockSpec((tm, c_dim), lambda i, j, k: (i, j))
        out = jax.ShapeDtypeStruct((m_dim, N_DEV * c_dim), F32)
    return _mm_call("cols_fwd", a, g, out, (m_dim // tm, N_DEV, k_dim // tk), a_spec, b_spec, o_spec, _NN,
                    (False, True, dm_out))


def _cols_da(dy, g, dm_out):
    _, k_dim, c_dim = g.shape
    m_dim = dy.shape[1] if dm_out else dy.shape[0]
    tm = _divisor_tile(m_dim, 1024)
    tk = _divisor_tile(k_dim, 1024)
    if dm_out:
        a_spec = pl.BlockSpec((1, tm, c_dim), lambda i, kb, j: (j, i, 0))
    else:
        a_spec = pl.BlockSpec((tm, c_dim), lambda i, kb, j: (i, j))
    b_spec = pl.BlockSpec((1, tk, c_dim), lambda i, kb, j: (j, kb, 0))
    o_spec = pl.BlockSpec((tm, tk), lambda i, kb, j: (i, kb))
    return _mm_call("cols_da", dy, g, jax.ShapeDtypeStruct((m_dim, k_dim), F32), (m_dim // tm, k_dim // tk, N_DEV),
                    a_spec, b_spec, o_spec, _NT_DIMS, (dm_out, True, False))


def _cols_dg(a, dy, g, dm_out):
    _, k_dim, c_dim = g.shape
    m_dim = a.shape[0]
    tm = _divisor_tile(m_dim, 2048)
    tk = _divisor_tile(k_dim, 512)
    a_spec = pl.BlockSpec((tm, tk), lambda kb, j, m: (m, kb))
    if dm_out:
        b_spec = pl.BlockSpec((1, tm, c_dim), lambda kb, j, m: (j, m, 0))
    else:
        b_spec = pl.BlockSpec((tm, c_dim), lambda kb, j, m: (m, j))
    o_spec = pl.BlockSpec((1, tk, c_dim), lambda kb, j, m: (j, kb, 0))
    return _mm_call("cols_dg", a, dy, jax.ShapeDtypeStruct(g.shape, g.dtype), (k_dim // tk, N_DEV, m_dim // tm),
                    a_spec, b_spec, o_spec, _TN_DIMS, (False, dm_out, True))


def _make_linear_cols(dm_out):
    @jax.custom_vjp
    def op(a, g):
        return _cols_fwd(a, g, dm_out)

    def fwd(a, g):
        return _cols_fwd(a, g, dm_out), (a, g)

    def bwd(res, dy):
        a, g = res
        return _cols_da(dy, g, dm_out), _cols_dg(a, dy, g, dm_out)

    op.defvjp(fwd, bwd)
    return op


linear_cols = _make_linear_cols(False)
linear_cols_dm = _make_linear_cols(True)


def _rows_fwd(a, g, res):
    _, m_dim, r_dim = a.shape
    n_dim = g.shape[2]
    tm = _divisor_tile(m_dim, 1024)
    tn = _divisor_tile(n_dim, 1024)
    a_spec = pl.BlockSpec((1, tm, r_dim), lambda i, n, j: (j, i, 0))
    b_spec = pl.BlockSpec((1, r_dim, tn), lambda i, n, j: (j, 0, n))
    o_spec = pl.BlockSpec((tm, tn), lambda i, n, j: (i, n))
    return _mm_call("rows_fwd", a, g, jax.ShapeDtypeStruct((m_dim, n_dim), F32), (m_dim // tm, n_dim // tn, N_DEV),
                    a_spec, b_spec, o_spec, _NN, (True, True, False), res=res)


def _rows_da(dy, g):
    m_dim, n_dim = dy.shape
    r_dim = g.shape[1]
    tm = _divisor_tile(m_dim, 1024)
    tn = _divisor_tile(n_dim, 1024)
    a_spec = pl.BlockSpec((tm, tn), lambda i, j, n: (i, n))
    b_spec = pl.BlockSpec((1, r_dim, tn), lambda i, j, n: (j, 0, n))
    o_spec = pl.BlockSpec((1, tm, r_dim), lambda i, j, n: (j, i, 0))
    return _mm_call("rows_da", dy, g, jax.ShapeDtypeStruct((N_DEV, m_dim, r_dim), F32),
                    (m_dim // tm, N_DEV, n_dim // tn), a_spec, b_spec, o_spec, _NT_DIMS, (False, True, True))


def _rows_dg(a, dy, g):
    _, m_dim, r_dim = a.shape
    n_dim = dy.shape[1]
    tm = _divisor_tile(m_dim, 2048)
    tn = _divisor_tile(n_dim, 512)
    a_spec = pl.BlockSpec((1, tm, r_dim), lambda j, n, m: (j, m, 0))
    b_spec = pl.BlockSpec((tm, tn), lambda j, n, m: (m, n))
    o_spec = pl.BlockSpec((1, r_dim, tn), lambda j, n, m: (j, 0, n))
    return _mm_call("rows_dg", a, dy, jax.ShapeDtypeStruct(g.shape, g.dtype), (N_DEV, n_dim // tn, m_dim // tm),
                    a_spec, b_spec, o_spec, _TN_DIMS, (True, False, True))


@jax.custom_vjp
def linear_rows_dm(a, g, x):
    return _rows_fwd(a, g, x)


def _linear_rows_dm_fwd(a, g, x):
    return _rows_fwd(a, g, x), (a, g)


def _linear_rows_dm_bwd(res, dy):
    a, g = res
    return _rows_da(dy, g), _rows_dg(a, dy, g), dy


linear_rows_dm.defvjp(_linear_rows_dm_fwd, _linear_rows_dm_bwd)


def _cols_to_natural_call(g, width):
    _, k_dim, c_dim = g.shape
    tk = _divisor_tile(k_dim, 256, SUBLANE)

    def body(g_ref, o_ref):
        for j in range(N_DEV):
            o_ref[:, j * c_dim:(j + 1) * c_dim] = g_ref[j]
        if width > N_DEV * c_dim:
            o_ref[:, N_DEV * c_dim:] = jnp.zeros((tk, width - N_DEV * c_dim), o_ref.dtype)

    return pl.pallas_call(
        body, grid=(k_dim // tk,), in_specs=[pl.BlockSpec((N_DEV, tk, c_dim), lambda i: (0, i, 0))],
        out_specs=pl.BlockSpec((tk, width), lambda i: (i, 0)), out_shape=jax.ShapeDtypeStruct((k_dim, width), g.dtype),
        compiler_params=_cparams(("parallel",)), name="cols_to_natural",
    )(g)


def _natural_to_cols_call(w, c_dim):
    k_dim, width = w.shape
    tk = _divisor_tile(k_dim, 256, SUBLANE)

    def body(w_ref, o_ref):
        for j in range(N_DEV):
            o_ref[j] = w_ref[:, j * c_dim:(j + 1) * c_dim]

    return pl.pallas_call(
        body, grid=(k_dim // tk,), in_specs=[pl.BlockSpec((tk, width), lambda i: (i, 0))],
        out_specs=pl.BlockSpec((N_DEV, tk, c_dim), lambda i: (0, i, 0)),
        out_shape=jax.ShapeDtypeStruct((N_DEV, k_dim, c_dim), w.dtype),
        compiler_params=_cparams(("parallel",)), name="natural_to_cols",
    )(w)


@functools.partial(jax.custom_vjp, nondiff_argnums=(1,))
def cols_to_natural(g, width):
    return _cols_to_natural_call(g, width)


def _cols_to_natural_fwd(g, width):
    return _cols_to_natural_call(g, width), g.shape[2]


def _cols_to_natural_bwd(width, c_dim, dw):
    return (_natural_to_cols_call(dw, c_dim),)


cols_to_natural.defvjp(_cols_to_natural_fwd, _cols_to_natural_bwd)


def make_rowop(fn, name, tm=256, passthrough=0):
    def specs(rows, params):
        row_specs = [pl.BlockSpec((tm, r.shape[1]), lambda i: (i, 0)) for r in rows]
        par_specs = [pl.BlockSpec(p.shape, lambda i: (0, 0)) for p in params]
        return row_specs, par_specs

    def out_structs(rows, params):
        tiles = [jax.ShapeDtypeStruct((tm, r.shape[1]), r.dtype) for r in rows]
        return jax.eval_shape(lambda r, p: fn(*r, *p), tiles, list(params))

    def fwd_call(rows, params):
        m_dim = rows[0].shape[0]
        n_in = len(rows) + len(params)
        outs = out_structs(rows, params)

        def body(*refs):
            res = fn(*[r[...] for r in refs[:n_in]])
            for o_ref, r in zip(refs[n_in:], res):
                o_ref[...] = r.astype(o_ref.dtype)

        row_specs, par_specs = specs(rows, params)
        return pl.pallas_call(
            body,
            grid=(m_dim // tm,),
            in_specs=row_specs + par_specs,
            out_specs=[pl.BlockSpec((tm, o.shape[1]), lambda i: (i, 0)) for o in outs],
            out_shape=[jax.ShapeDtypeStruct((m_dim, o.shape[1]), o.dtype) for o in outs],
            compiler_params=_cparams(("parallel",)),
            name=name + "_fwd",
        )(*rows, *params)

    def bwd_call(rows, params, cts):
        m_dim = rows[0].shape[0]
        n_rows, n_par = len(rows), len(params)
        n_in = n_rows + n_par
        n_ct = len(cts)
        n_fn = n_ct - passthrough

        def body(*refs):
            vals = [r[...] for r in refs[:n_in]]
            ct_vals = tuple(r[...] for r in refs[n_in:n_in + n_fn])
            pass_refs = refs[n_in + n_fn:n_in + n_ct]
            drow_refs = refs[n_in + n_ct:n_in + n_ct + n_rows]
            dpar_refs = refs[n_in + n_ct + n_rows:]
            _, pullback = jax.vjp(fn, *vals)
            grads = pullback(ct_vals)
            for i, (d_ref, g) in enumerate(zip(drow_refs, grads[:n_rows])):
                d_ref[...] = g + pass_refs[i][...] if i < passthrough else g

            @pl.when(pl.program_id(0) == 0)
            def _():
                for d_ref in dpar_refs:
                    d_ref[...] = jnp.zeros_like(d_ref)

            for d_ref, g in zip(dpar_refs, grads[n_rows:]):
                d_ref[...] += g

        row_specs, par_specs = specs(rows, params)
        ct_specs = [pl.BlockSpec((tm, c.shape[1]), lambda i: (i, 0)) for c in cts]
        res = pl.pallas_call(
            body,
            grid=(m_dim // tm,),
            in_specs=row_specs + par_specs + ct_specs,
            out_specs=row_specs + par_specs,
            out_shape=[jax.ShapeDtypeStruct(r.shape, r.dtype) for r in rows]
            + [jax.ShapeDtypeStruct(p.shape, p.dtype) for p in params],
            compiler_params=_cparams(("arbitrary",)),
            name=name + "_bwd",
        )(*rows, *params, *cts)
        return tuple(res[:n_rows]), tuple(res[n_rows:])

    @jax.custom_vjp
    def op(rows, params):
        return tuple(fwd_call(rows, params)) + tuple(rows[:passthrough])

    def op_fwd(rows, params):
        return tuple(fwd_call(rows, params)) + tuple(rows[:passthrough]), (rows, params)

    def op_bwd(res, cts):
        rows, params = res
        return bwd_call(rows, params, tuple(cts))

    op.defvjp(op_fwd, op_bwd)
    return op


def _rms(x, g):
    return x * lax.rsqrt(jnp.mean(x * x, axis=-1, keepdims=True) + RMS_EPS) * g


def _sigmoid(x):
    return 1.0 / (1.0 + jnp.exp(-x))


def _silu(x):
    return x * _sigmoid(x)


def _bdot(a, b, dims=(((1,), (0,)), ((), ()))):
    return lax.dot_general(a.astype(BF16), b.astype(BF16), dims, preferred_element_type=F32)


def _rmsnorm_fn(x, g):
    return (_rms(x, g),)


def rmsnorm(x, g, name):
    return make_rowop(_rmsnorm_fn, name)((x,), (g.reshape(1, -1),))[0]


def rmsnorm_res(x, g, name):
    return make_rowop(_rmsnorm_fn, name, passthrough=1)((x,), (g.reshape(1, -1),))


def _gelu_tanh(x):
    return 0.5 * x * (1.0 + jnp.tanh(0.7978845608028654 * (x + 0.044715 * x * x * x)))


def _softplus(x):
    return jnp.maximum(x, 0.0) + jnp.log(1.0 + jnp.exp(-jnp.abs(x)))


def _s5_post_fn(y, w_glu, b_glu):
    h = _gelu_tanh(y)
    return (h * _sigmoid(_bdot(h, w_glu) + b_glu),)


def _swiglu_fn(g, u):
    return (_silu(g) * u,)


def _loss_fn(y, t):
    err = y - t
    return (0.5 * jnp.mean(err * err, axis=-1, keepdims=True),)


def _pair_headnorm(x, g2):
    lo = lax.broadcasted_iota(jnp.int32, x.shape, 1) < CA_HEAD_DIM
    sq = x * x
    s_lo = jnp.sum(jnp.where(lo, sq, 0.0), axis=-1, keepdims=True)
    s_hi = jnp.sum(jnp.where(lo, 0.0, sq), axis=-1, keepdims=True)
    ms = jnp.where(lo, s_lo, s_hi) * (1.0 / CA_HEAD_DIM)
    return x * lax.rsqrt(ms + RMS_EPS) * g2


def _ca_qknorm_fn(qkv, qg2, kg2):
    qs, ks = [], []
    for j in range(D_MODEL // LANE):
        qs.append(_pair_headnorm(qkv[:, j * LANE:(j + 1) * LANE], qg2))
        ks.append(_pair_headnorm(qkv[:, D_MODEL + j * LANE:D_MODEL + (j + 1) * LANE], kg2))
    return jnp.concatenate(qs, axis=1), jnp.concatenate(ks, axis=1)


def _xattn_fn(q, k, v, qg, kg):
    outs = []
    for h in range(XA_HEADS):
        sl = slice(h * XA_HEAD_DIM, (h + 1) * XA_HEAD_DIM)
        qh = _rms(q[:, sl], qg)
        kh = _rms(k[:, sl], kg)
        s = _bdot(qh, kh, (((1,), (1,)), ((), ()))) * (XA_HEAD_DIM ** -0.5)
        p = jnp.exp(s - jnp.max(s, axis=-1, keepdims=True))
        p = p / jnp.sum(p, axis=-1, keepdims=True)
        outs.append(_bdot(p, v[:, sl]))
    return (jnp.concatenate(outs, axis=1),)


def _gdn_prep_fn(x0, x1, x2, x3, ab, conv_w, alog, dtb):
    c = conv_w[3:4, :] * x0 + conv_w[2:3, :] * x1 + conv_w[1:2, :] * x2 + conv_w[0:1, :] * x3
    c = _silu(c)
    qs, ks = [], []
    for h in range(GDN_HEADS):
        qh = c[:, h * LANE:(h + 1) * LANE]
        kh = c[:, GDN_WIDTH + h * LANE:GDN_WIDTH + (h + 1) * LANE]
        qs.append(qh * lax.rsqrt(jnp.sum(qh * qh, axis=-1, keepdims=True) + RMS_EPS) * (GDN_HEAD_DIM ** -0.5))
        ks.append(kh * lax.rsqrt(jnp.sum(kh * kh, axis=-1, keepdims=True) + RMS_EPS))
    lane = lax.broadcasted_iota(jnp.int32, ab.shape, 1)
    g = -jnp.exp(alog) * _softplus(ab + dtb)
    beta = _sigmoid(ab)
    bg = jnp.where(lane < GDN_HEADS, g, jnp.where(lane < 2 * GDN_HEADS, beta, 0.0))
    return jnp.concatenate(qs, axis=1), jnp.concatenate(ks, axis=1), c[:, 2 * GDN_WIDTH:], bg


def _gdn_out_fn(o, gate, og):
    outs = []
    for h in range(GDN_HEADS):
        sl = slice(h * LANE, (h + 1) * LANE)
        outs.append(_rms(o[:, sl], og) * _silu(gate[:, sl]))
    return (jnp.concatenate(outs, axis=1),)


CA_QB = 4 * CHUNK
CA_KB = CA_QB + CA_PAD


def _ca_math(q2, kb2, vb2, bias2, c):
    lane = lax.broadcasted_iota(jnp.int32, q2.shape, 1)
    qc = lax.broadcasted_iota(jnp.int32, (CA_QB, CA_KB), 0) // CHUNK
    kc = lax.broadcasted_iota(jnp.int32, (CA_QB, CA_KB), 1) // CHUNK
    valid = (kc >= qc) & (kc <= qc + CA_LEFT) & (kc + c * (CA_QB // CHUNK) >= CA_LEFT)
    out = jnp.zeros(q2.shape, F32)
    for h in range(2):
        mine = (lane >= h * CA_HEAD_DIM) & (lane < (h + 1) * CA_HEAD_DIM)
        qh = jnp.where(mine, q2, 0.0)
        s = _bdot(qh, kb2, (((1,), (1,)), ((), ()))) * (CA_HEAD_DIM ** -0.5) + bias2[h]
        s = jnp.where(valid, s, -1e30)
        p = jnp.exp(s - jnp.max(s, axis=-1, keepdims=True))
        p = p / jnp.sum(p, axis=-1, keepdims=True)
        out = out + jnp.where(mine, _bdot(p, vb2), 0.0)
    return out


CA_VEC = CA_QB + CA_KB


def _ca_specs(seq):
    q_spec = pl.BlockSpec((CA_QB, LANE), lambda hp, c: (c, hp))
    kv_spec = pl.BlockSpec((seq + CA_PAD, LANE), lambda hp, c: (0, hp))
    b_spec = pl.BlockSpec((1, 2, CA_VEC), lambda hp, c: (hp, 0, 0))
    return (D_MODEL // LANE, seq // CA_QB), q_spec, kv_spec, b_spec


def _ca_bias_from_vector(vec_ref, bias_ref):
    for h in range(2):
        rows = jnp.broadcast_to(vec_ref[0, h:h + 1, :], (CA_QB, CA_VEC))
        bias_ref[h] = pltpu.roll(rows, 0, 1, stride=1, stride_axis=0)[:, CA_QB:]


def _ca_vector_grad(dbias):
    d = jnp.concatenate([jnp.zeros((CA_QB, CA_QB), F32), dbias], axis=1)
    row = lax.broadcasted_iota(jnp.int32, d.shape, 0)
    for bit in range(CA_QB.bit_length() - 1):
        d = jnp.where((row >> bit) & 1 == 1, pltpu.roll(d, CA_VEC - (1 << bit), 1), d)
    return jnp.sum(d, axis=0, keepdims=True)


def _ca_fwd_call(q, kpad, vpad, vec):
    grid, q_spec, kv_spec, b_spec = _ca_specs(q.shape[0])

    def body(q_ref, k_ref, v_ref, vec_ref, o_ref, bias_ref):
        c = pl.program_id(1)
        start = pl.multiple_of(c * CA_QB, CA_QB)

        @pl.when(c == 0)
        def _():
            _ca_bias_from_vector(vec_ref, bias_ref)

        o_ref[...] = _ca_math(q_ref[...], k_ref[pl.ds(start, CA_KB), :], v_ref[pl.ds(start, CA_KB), :],
                              bias_ref[...], c)

    return pl.pallas_call(
        body, grid=grid, in_specs=[q_spec, kv_spec, kv_spec, b_spec], out_specs=q_spec,
        out_shape=jax.ShapeDtypeStruct(q.shape, F32), scratch_shapes=[pltpu.VMEM((2, CA_QB, CA_KB), F32)],
        compiler_params=_cparams(("parallel", "arbitrary")), name="chunkattn_fwd",
    )(q, kpad, vpad, vec)


def _ca_bwd_call(q, kpad, vpad, vec, do):
    grid, q_spec, kv_spec, b_spec = _ca_specs(q.shape[0])
    last = grid[1] - 1

    def body(q_ref, k_ref, v_ref, vec_ref, do_ref, dq_ref, dk_ref, dv_ref, dvec_ref, bias_ref, dbias_ref):
        c = pl.program_id(1)
        start = pl.multiple_of(c * CA_QB, CA_QB)

        @pl.when(c == 0)
        def _():
            _ca_bias_from_vector(vec_ref, bias_ref)
            dk_ref[...] = jnp.zeros_like(dk_ref)
            dv_ref[...] = jnp.zeros_like(dv_ref)
            dbias_ref[...] = jnp.zeros_like(dbias_ref)

        _, pullback = jax.vjp(lambda a, b, d, e: _ca_math(a, b, d, e, c), q_ref[...],
                              k_ref[pl.ds(start, CA_KB), :], v_ref[pl.ds(start, CA_KB), :], bias_ref[...])
        dq, dkb, dvb, dbias = pullback(do_ref[...])
        dq_ref[...] = dq
        dk_ref[pl.ds(start, CA_KB), :] += dkb
        dv_ref[pl.ds(start, CA_KB), :] += dvb
        dbias_ref[...] += dbias

        @pl.when(c == last)
        def _():
            for h in range(2):
                dvec_ref[0, h:h + 1, :] = _ca_vector_grad(dbias_ref[h])

    return pl.pallas_call(
        body, grid=grid, in_specs=[q_spec, kv_spec, kv_spec, b_spec, q_spec],
        out_specs=[q_spec, kv_spec, kv_spec, b_spec],
        out_shape=[jax.ShapeDtypeStruct(q.shape, F32), jax.ShapeDtypeStruct(kpad.shape, F32),
                   jax.ShapeDtypeStruct(vpad.shape, F32), jax.ShapeDtypeStruct(vec.shape, F32)],
        scratch_shapes=[pltpu.VMEM((2, CA_QB, CA_KB), F32), pltpu.VMEM((2, CA_QB, CA_KB), F32)],
        compiler_params=_cparams(("parallel", "arbitrary")), name="chunkattn_bwd",
    )(q, kpad, vpad, vec, do)


@jax.custom_vjp
def chunk_attn_core(q, kpad, vpad, vec):
    return _ca_fwd_call(q, kpad, vpad, vec)


def _ca_core_fwd(q, kpad, vpad, vec):
    return _ca_fwd_call(q, kpad, vpad, vec), (q, kpad, vpad, vec)


def _ca_core_bwd(res, do):
    return tuple(_ca_bwd_call(*res, do))


chunk_attn_core.defvjp(_ca_core_fwd, _ca_core_bwd)


S5_GB = 4
S5_U = S5_WIDTH // S5_GB
S5_L = S5_GROUPS * S5_STATE // S5_GB


def _cmul(ar, ai, br, bi):
    return ar * br - ai * bi, ar * bi + ai * br


def _hdot(a, b, dims=(((1,), (0,)), ((), ()))):
    return lax.dot_general(a, b, dims, precision=HI, preferred_element_type=F32)


_NT = (((1,), (1,)), ((), ()))
_TN = (((0,), (0,)), ((), ()))


def _s5_tables(lr, li, reverse):
    p = {1: (lr, li)}
    p[2] = _cmul(*p[1], *p[1])
    p[4] = _cmul(*p[2], *p[2])
    p[3] = _cmul(*p[2], *p[1])
    p[5] = _cmul(*p[4], *p[1])
    p[6] = _cmul(*p[4], *p[2])
    p[7] = _cmul(*p[4], *p[3])
    p[8] = _cmul(*p[4], *p[4])
    row = lax.broadcasted_iota(jnp.int32, (SUBLANE, lr.shape[1]), 0)
    tr = jnp.zeros(row.shape, F32)
    ti = jnp.zeros(row.shape, F32)
    for i in range(SUBLANE):
        k = SUBLANE - i if reverse else i + 1
        tr = jnp.where(row == i, p[k][0], tr)
        ti = jnp.where(row == i, p[k][1], ti)
    return p, (tr, ti), row


def _s5_block_scan(xr, xi, p, tab, row, hr, hi, reverse):
    for k in (1, 2, 4):
        if reverse:
            sr = jnp.where(row < SUBLANE - k, pltpu.roll(xr, SUBLANE - k, 0), 0.0)
            si = jnp.where(row < SUBLANE - k, pltpu.roll(xi, SUBLANE - k, 0), 0.0)
        else:
            sr = jnp.where(row >= k, pltpu.roll(xr, k, 0), 0.0)
            si = jnp.where(row >= k, pltpu.roll(xi, k, 0), 0.0)
        ar, ai = _cmul(p[k][0], p[k][1], sr, si)
        xr, xi = xr + ar, xi + ai
    cr, ci = _cmul(tab[0], tab[1], hr, hi)
    return xr + cr, xi + ci


def _s5_forward_scan(sr_ref, si_ref, lr, li):
    n_blocks = sr_ref.shape[0] // SUBLANE
    p, tab, row = _s5_tables(lr, li, False)

    def step(b, carry):
        base = pl.multiple_of(b * SUBLANE, SUBLANE)
        xr, xi = _s5_block_scan(sr_ref[pl.ds(base, SUBLANE), :], si_ref[pl.ds(base, SUBLANE), :],
                                p, tab, row, carry[0], carry[1], False)
        sr_ref[pl.ds(base, SUBLANE), :] = xr
        si_ref[pl.ds(base, SUBLANE), :] = xi
        return xr[SUBLANE - 1:SUBLANE, :], xi[SUBLANE - 1:SUBLANE, :]

    zero = jnp.zeros((1, lr.shape[1]), F32)
    lax.fori_loop(0, n_blocks, step, (zero, zero))


def _s5_specs(seq):
    u_spec = pl.BlockSpec((seq, S5_U), lambda g: (0, g))
    bd_spec = pl.BlockSpec((1, S5_U, S5_L), lambda g: (g, 0, 0))
    cd_spec = pl.BlockSpec((1, S5_L, S5_U), lambda g: (g, 0, 0))
    lam_spec = pl.BlockSpec((1, 2, S5_L), lambda g: (g, 0, 0))
    d_spec = pl.BlockSpec((1, S5_U), lambda g: (0, g))
    return u_spec, bd_spec, cd_spec, lam_spec, d_spec


S5_ROWS = 256


def _row_chunks(seq, fn):
    rows_per = min(S5_ROWS, seq)

    def step(r, carry):
        fn(pl.ds(pl.multiple_of(r * rows_per, rows_per), rows_per))
        return carry

    lax.fori_loop(0, seq // rows_per, step, 0)


def _s5_fwd_call(u, bdr, bdi, cdr, cdi, lam, d):
    seq = u.shape[0]
    u_spec, bd_spec, cd_spec, lam_spec, d_spec = _s5_specs(seq)

    def body(u_ref, bdr_ref, bdi_ref, cdr_ref, cdi_ref, lam_ref, d_ref, y_ref, sr_ref, si_ref):
        def project_in(rows):
            uv = u_ref[rows, :]
            sr_ref[rows, :] = _bdot(uv, bdr_ref[0])
            si_ref[rows, :] = _bdot(uv, bdi_ref[0])

        def project_out(rows):
            y_ref[rows, :] = (_bdot(sr_ref[rows, :], cdr_ref[0]) - _bdot(si_ref[rows, :], cdi_ref[0])
                              + d_ref[...] * u_ref[rows, :])

        _row_chunks(seq, project_in)
        _s5_forward_scan(sr_ref, si_ref, lam_ref[0, 0:1, :], lam_ref[0, 1:2, :])
        _row_chunks(seq, project_out)

    return pl.pallas_call(
        body, grid=(S5_GB,), in_specs=[u_spec, bd_spec, bd_spec, cd_spec, cd_spec, lam_spec, d_spec],
        out_specs=u_spec, out_shape=jax.ShapeDtypeStruct(u.shape, F32),
        scratch_shapes=[pltpu.VMEM((seq, S5_L), F32), pltpu.VMEM((seq, S5_L), F32)],
        compiler_params=_cparams(("parallel",)), name="s5_fwd",
    )(u, bdr, bdi, cdr, cdi, lam, d)


def _s5_bwd_call(u, bdr, bdi, cdr, cdi, lam, d, dy):
    seq = u.shape[0]
    n_blocks = seq // SUBLANE
    u_spec, bd_spec, cd_spec, lam_spec, d_spec = _s5_specs(seq)

    def body(u_ref, bdr_ref, bdi_ref, cdr_ref, cdi_ref, lam_ref, d_ref, dy_ref,
             du_ref, dbdr_ref, dbdi_ref, dcdr_ref, dcdi_ref, dlam_ref, dd_ref, sr_ref, si_ref, gr_ref, gi_ref):
        lr, li = lam_ref[0, 0:1, :], lam_ref[0, 1:2, :]

        def project_in(rows):
            uv = u_ref[rows, :]
            dyv = dy_ref[rows, :]
            sr_ref[rows, :] = _bdot(uv, bdr_ref[0])
            si_ref[rows, :] = _bdot(uv, bdi_ref[0])
            gr_ref[rows, :] = _bdot(dyv, cdr_ref[0], _NT)
            gi_ref[rows, :] = -_bdot(dyv, cdi_ref[0], _NT)

        _row_chunks(seq, project_in)
        _s5_forward_scan(sr_ref, si_ref, lr, li)
        p, tab, row = _s5_tables(lr, -li, True)

        def step(i, carry):
            hr, hi, acc_r, acc_i = carry
            b = n_blocks - 1 - i
            base = pl.multiple_of(b * SUBLANE, SUBLANE)
            xr, xi = _s5_block_scan(gr_ref[pl.ds(base, SUBLANE), :], gi_ref[pl.ds(base, SUBLANE), :],
                                    p, tab, row, hr, hi, True)
            gr_ref[pl.ds(base, SUBLANE), :] = xr
            gi_ref[pl.ds(base, SUBLANE), :] = xi
            prev = pl.multiple_of(jnp.maximum(b - 1, 0) * SUBLANE, SUBLANE)
            keep = (b > 0).astype(F32)
            last_r = sr_ref[pl.ds(prev, SUBLANE), :][SUBLANE - 1:SUBLANE, :] * keep
            last_i = si_ref[pl.ds(prev, SUBLANE), :][SUBLANE - 1:SUBLANE, :] * keep
            pr = jnp.where(row >= 1, pltpu.roll(sr_ref[pl.ds(base, SUBLANE), :], 1, 0), last_r)
            pi = jnp.where(row >= 1, pltpu.roll(si_ref[pl.ds(base, SUBLANE), :], 1, 0), last_i)
            acc_r = acc_r + pr * xr + pi * xi
            acc_i = acc_i + pr * xi - pi * xr
            return xr[0:1, :], xi[0:1, :], acc_r, acc_i

        zero = jnp.zeros((1, S5_L), F32)
        zacc = jnp.zeros((SUBLANE, S5_L), F32)
        _, _, acc_r, acc_i = lax.fori_loop(0, n_blocks, step, (zero, zero, zacc, zacc))
        dlam_ref[0, 0:1, :] = jnp.sum(acc_r, axis=0, keepdims=True)
        dlam_ref[0, 1:2, :] = jnp.sum(acc_i, axis=0, keepdims=True)
        for ref in (dbdr_ref, dbdi_ref, dcdr_ref, dcdi_ref, dd_ref):
            ref[...] = jnp.zeros_like(ref)

        def grads(rows):
            uv, dyv = u_ref[rows, :], dy_ref[rows, :]
            grv, giv = gr_ref[rows, :], gi_ref[rows, :]
            du_ref[rows, :] = _bdot(grv, bdr_ref[0], _NT) + _bdot(giv, bdi_ref[0], _NT) + d_ref[...] * dyv
            dbdr_ref[0] += _bdot(uv, grv, _TN)
            dbdi_ref[0] += _bdot(uv, giv, _TN)
            dcdr_ref[0] += _bdot(sr_ref[rows, :], dyv, _TN)
            dcdi_ref[0] -= _bdot(si_ref[rows, :], dyv, _TN)
            dd_ref[...] += jnp.sum(dyv * uv, axis=0, keepdims=True)

        _row_chunks(seq, grads)

    scratch = [pltpu.VMEM((seq, S5_L), F32) for _ in range(4)]
    return pl.pallas_call(
        body, grid=(S5_GB,),
        in_specs=[u_spec, bd_spec, bd_spec, cd_spec, cd_spec, lam_spec, d_spec, u_spec],
        out_specs=[u_spec, bd_spec, bd_spec, cd_spec, cd_spec, lam_spec, d_spec],
        out_shape=[jax.ShapeDtypeStruct(a.shape, F32) for a in (u, bdr, bdi, cdr, cdi, lam, d)],
        scratch_shapes=scratch, compiler_params=_cparams(("parallel",)), name="s5_bwd",
    )(u, bdr, bdi, cdr, cdi, lam, d, dy)


@jax.custom_vjp
def s5_core(u, bdr, bdi, cdr, cdi, lam, d):
    return _s5_fwd_call(u, bdr, bdi, cdr, cdi, lam, d)


def _s5_core_fwd(*args):
    return _s5_fwd_call(*args), args


def _s5_core_bwd(res, dy):
    return tuple(_s5_bwd_call(*res, dy))


s5_core.defvjp(_s5_core_fwd, _s5_core_bwd)


def _s5_discretize(a_re, a_im, log_dt, b_re, b_im, c_re, c_im, d):
    dt = jnp.exp(log_dt)[:, None]
    mag = jnp.exp(a_re * dt)
    lbr, lbi = mag * jnp.cos(a_im * dt), mag * jnp.sin(a_im * dt)
    den = a_re * a_re + a_im * a_im
    fr = ((lbr - 1.0) * a_re + lbi * a_im) / den
    fi = (lbi * a_re - (lbr - 1.0) * a_im) / den
    bbr = fr[:, :, None] * b_re - fi[:, :, None] * b_im
    bbi = fr[:, :, None] * b_im + fi[:, :, None] * b_re
    eye = jnp.eye(S5_GROUPS // S5_GB, dtype=F32)
    gl = S5_GROUPS // S5_GB

    def bd(t):
        return jnp.einsum('bgpc,gh->bgchp', t.reshape(S5_GB, gl, S5_STATE, S5_GROUP), eye).reshape(S5_GB, S5_U, S5_L)

    def cd(t):
        return jnp.einsum('bgcp,gh->bgphc', t.reshape(S5_GB, gl, S5_GROUP, S5_STATE), eye).reshape(S5_GB, S5_L, S5_U)

    lam = jnp.stack([lbr.reshape(S5_GB, S5_L), lbi.reshape(S5_GB, S5_L)], axis=1)
    return bd(bbr), bd(bbi), cd(c_re), cd(c_im), lam, d.reshape(1, S5_WIDTH)


@jax.custom_vjp
def _unit_lower_solve(neg_a, rhs, tinv):
    return _hdot(tinv, rhs)


def _unit_lower_solve_fwd(neg_a, rhs, tinv):
    x = _hdot(tinv, rhs)
    return x, (x, tinv)


def _unit_lower_solve_bwd(res, dx):
    x, tinv = res
    g = _hdot(tinv, dx, _TN)
    return _hdot(g, x, _NT), g, jnp.zeros_like(tinv)


_unit_lower_solve.defvjp(_unit_lower_solve_fwd, _unit_lower_solve_bwd)


def _unit_lower_inverse(neg_a):
    r = lax.broadcasted_iota(jnp.int32, neg_a.shape, 0)
    c = lax.broadcasted_iota(jnp.int32, neg_a.shape, 1)
    p = (r == c).astype(F32) + neg_a
    npow = _hdot(neg_a, neg_a)
    for _ in range(4):
        y = _hdot(jnp.concatenate([p, npow], axis=0), npow)
        p = p + y[:CHUNK]
        npow = y[CHUNK:]
    return p + _hdot(p, npow)


def _gdn_chunk(q, k, v, g_col, b_col, st, tinv=None):
    r = lax.broadcasted_iota(jnp.int32, (CHUNK, CHUNK), 0)
    c = lax.broadcasted_iota(jnp.int32, (CHUNK, CHUNK), 1)
    eye = (r == c).astype(F32)
    strict = r > c
    causal = r >= c
    g_row = jnp.sum(g_col * eye, axis=0, keepdims=True)
    gcum = jnp.sum(jnp.where(causal, g_row, 0.0), axis=1, keepdims=True)
    gcum_row = jnp.sum(gcum * eye, axis=0, keepdims=True)
    diff = gcum - gcum_row
    decay_strict = jnp.where(strict, jnp.exp(jnp.where(strict, diff, 0.0)), 0.0)
    decay_causal = jnp.where(causal, jnp.exp(jnp.where(causal, diff, 0.0)), 0.0)
    gamma = jnp.exp(gcum)
    g_last = jnp.sum(jnp.where(lax.broadcasted_iota(jnp.int32, (CHUNK, 1), 0) == CHUNK - 1, gcum, 0.0),
                     axis=0, keepdims=True)
    kk = _bdot(k, k, _NT)
    neg_a = -(b_col * kk * decay_strict)
    if tinv is None:
        tinv = _unit_lower_inverse(neg_a)
    x = _unit_lower_solve(neg_a, jnp.concatenate([b_col * v, (b_col * gamma) * k], axis=1), lax.stop_gradient(tinv))
    u_new, w_k = x[:, :GDN_HEAD_DIM], x[:, GDN_HEAD_DIM:]
    qk = _bdot(q, k, _NT) * decay_causal
    q_g = q * gamma
    k_tail = k * jnp.exp(g_last - gcum)
    w = u_new - _bdot(w_k, st)
    o = _bdot(q_g, st) + _bdot(qk, w)
    st_new = jnp.exp(g_last) * st + _bdot(k_tail, w, _TN)
    return o, st_new, tinv


def _gdn_cols(bgv, h):
    lane = lax.broadcasted_iota(jnp.int32, bgv.shape, 1)
    g_col = jnp.sum(jnp.where(lane == h, bgv, 0.0), axis=1, keepdims=True)
    b_col = jnp.sum(jnp.where(lane == GDN_HEADS + h, bgv, 0.0), axis=1, keepdims=True)
    return g_col, b_col


def _gdn_fwd_call(q, k, v, bg):
    seq = q.shape[0]
    n_chunks = seq // CHUNK
    x_spec = pl.BlockSpec((CHUNK, GDN_WIDTH), lambda n: (n, 0))
    bg_spec = pl.BlockSpec((CHUNK, LANE), lambda n: (n, 0))
    st_spec = pl.BlockSpec((1, GDN_WIDTH, GDN_HEAD_DIM), lambda n: (n, 0, 0))
    ti_spec = pl.BlockSpec((1, GDN_HEADS * CHUNK, CHUNK), lambda n: (n, 0, 0))

    def body(q_ref, k_ref, v_ref, bg_ref, o_ref, st_out_ref, ti_out_ref, st_ref):
        @pl.when(pl.program_id(0) == 0)
        def _():
            st_ref[...] = jnp.zeros_like(st_ref)

        bgv = bg_ref[...]
        for h in range(GDN_HEADS):
            sl = slice(h * GDN_HEAD_DIM, (h + 1) * GDN_HEAD_DIM)
            g_col, b_col = _gdn_cols(bgv, h)
            st = st_ref[sl, :]
            st_out_ref[0, sl, :] = st
            o, st_new, tinv = _gdn_chunk(q_ref[:, sl], k_ref[:, sl], v_ref[:, sl], g_col, b_col, st)
            o_ref[:, sl] = o
            st_ref[sl, :] = st_new
            ti_out_ref[0, h * CHUNK:(h + 1) * CHUNK, :] = tinv

    return pl.pallas_call(
        body, grid=(n_chunks,), in_specs=[x_spec, x_spec, x_spec, bg_spec], out_specs=[x_spec, st_spec, ti_spec],
        out_shape=[jax.ShapeDtypeStruct(q.shape, F32),
                   jax.ShapeDtypeStruct((n_chunks, GDN_WIDTH, GDN_HEAD_DIM), F32),
                   jax.ShapeDtypeStruct((n_chunks, GDN_HEADS * CHUNK, CHUNK), F32)],
        scratch_shapes=[pltpu.VMEM((GDN_WIDTH, GDN_HEAD_DIM), F32)],
        compiler_params=_cparams(("arbitrary",)), name="gdn_fwd",
    )(q, k, v, bg)


def _gdn_bwd_call(q, k, v, bg, states, tinvs, do):
    seq = q.shape[0]
    n_chunks = seq // CHUNK
    x_spec = pl.BlockSpec((CHUNK, GDN_WIDTH), lambda i: (n_chunks - 1 - i, 0))
    bg_spec = pl.BlockSpec((CHUNK, LANE), lambda i: (n_chunks - 1 - i, 0))
    st_spec = pl.BlockSpec((1, GDN_WIDTH, GDN_HEAD_DIM), lambda i: (n_chunks - 1 - i, 0, 0))
    ti_spec = pl.BlockSpec((1, GDN_HEADS * CHUNK, CHUNK), lambda i: (n_chunks - 1 - i, 0, 0))

    def body(q_ref, k_ref, v_ref, bg_ref, st_in_ref, ti_ref, do_ref, dq_ref, dk_ref, dv_ref, dbg_ref, dst_ref):
        @pl.when(pl.program_id(0) == 0)
        def _():
            dst_ref[...] = jnp.zeros_like(dst_ref)

        bgv = bg_ref[...]
        lane = lax.broadcasted_iota(jnp.int32, bgv.shape, 1)
        dbg = jnp.zeros(bgv.shape, F32)
        for h in range(GDN_HEADS):
            sl = slice(h * GDN_HEAD_DIM, (h + 1) * GDN_HEAD_DIM)
            g_col, b_col = _gdn_cols(bgv, h)
            tinv = ti_ref[0, h * CHUNK:(h + 1) * CHUNK, :]
            _, pullback = jax.vjp(lambda *a: _gdn_chunk(*a, tinv=tinv)[:2], q_ref[:, sl], k_ref[:, sl], v_ref[:, sl],
                                  g_col, b_col, st_in_ref[0, sl, :])
            dq, dk, dv, dg, db, dst = pullback((do_ref[:, sl], dst_ref[sl, :]))
            dq_ref[:, sl] = dq
            dk_ref[:, sl] = dk
            dv_ref[:, sl] = dv
            dst_ref[sl, :] = dst
            dbg = dbg + jnp.where(lane == h, dg, 0.0) + jnp.where(lane == GDN_HEADS + h, db, 0.0)
        dbg_ref[...] = dbg

    return pl.pallas_call(
        body, grid=(n_chunks,), in_specs=[x_spec, x_spec, x_spec, bg_spec, st_spec, ti_spec, x_spec],
        out_specs=[x_spec, x_spec, x_spec, bg_spec],
        out_shape=[jax.ShapeDtypeStruct(q.shape, F32)] * 3 + [jax.ShapeDtypeStruct(bg.shape, F32)],
        scratch_shapes=[pltpu.VMEM((GDN_WIDTH, GDN_HEAD_DIM), F32)],
        compiler_params=_cparams(("arbitrary",)), name="gdn_bwd",
    )(q, k, v, bg, states, tinvs, do)


@jax.custom_vjp
def gdn_core(q, k, v, bg):
    return _gdn_fwd_call(q, k, v, bg)[0]


def _gdn_core_fwd(q, k, v, bg):
    o, states, tinvs = _gdn_fwd_call(q, k, v, bg)
    return o, (q, k, v, bg, states, tinvs)


def _gdn_core_bwd(res, do):
    return tuple(_gdn_bwd_call(*res, do))


gdn_core.defvjp(_gdn_core_fwd, _gdn_core_bwd)


def _row(v):
    return v.reshape(1, -1)


def _lane_pad(v):
    return jnp.pad(v, (0, LANE - v.shape[0])).reshape(1, LANE)


def _delay_rows(x, k):
    return jnp.pad(x, ((k, 0), (0, 0)))[:x.shape[0]]


def s5_mixer(u, a_re, a_im, log_dt, b_re, b_im, c_re, c_im, d, w_glu, b_glu):
    y = s5_core(u, *_s5_discretize(a_re, a_im, log_dt, b_re, b_im, c_re, c_im, d))
    return make_rowop(_s5_post_fn, "s5_post")((y,), (w_glu, _row(b_glu)))[0]


def gated_deltanet(qkv, gate, ab, conv_w, a_log, dt_bias, out_g):
    rows = (qkv, _delay_rows(qkv, 1), _delay_rows(qkv, 2), _delay_rows(qkv, 3), ab)
    q, k, v, bg = make_rowop(_gdn_prep_fn, "gdn_prep")(rows, (conv_w, _lane_pad(a_log), _lane_pad(dt_bias)))
    o = gdn_core(q, k, v, bg)
    return make_rowop(_gdn_out_fn, "gdn_out")((o, gate), (_row(out_g),))[0]


def chunk_attention(x, h, w_qkv, w_out, q_g, k_g, rel_bias):
    qkv = linear_cols(h, w_qkv)
    qn, kn = make_rowop(_ca_qknorm_fn, "ca_qknorm")((qkv,), (_row(jnp.tile(q_g, 2)), _row(jnp.tile(k_g, 2))))
    kpad = jnp.pad(kn, ((CA_PAD, 0), (0, 0)))
    vpad = jnp.pad(qkv[:, 2 * D_MODEL:], ((CA_PAD, 0), (0, 0)))
    o = chunk_attn_core(qn, kpad, vpad, _rel_bias_vector(rel_bias))
    return linear_res(o, w_out, x)


def memory_cross_attention(x, h, mem_n, w_q, w_kv, w_out, q_g, k_g):
    q = linear(h, w_q)
    kv = linear_cols(mem_n, w_kv)
    o = make_rowop(_xattn_fn, "xattn")((q,), (kv[:, :D_MODEL], kv[:, D_MODEL:], _row(q_g), _row(k_g)))[0]
    return linear_res(o, w_out, x)


def swiglu(x, h, w_gate, w_up, w_down):
    g = linear_cols_dm(h, w_gate)
    u = linear_cols_dm(h, w_up)
    rows = g.shape[0] * g.shape[1]
    a = make_rowop(_swiglu_fn, "swiglu")((g.reshape(rows, -1), u.reshape(rows, -1)), ())[0]
    return linear_rows_dm(a.reshape(g.shape), w_down, x)


def _rel_bias_vector(rel_bias):
    heads = rel_bias.shape[0]
    n_far = CA_KB - 1 - MAX_REL
    n_neg = CA_VEC - 1 - n_far - (2 * MAX_REL + 1)
    vec = jnp.concatenate([jnp.zeros((heads, 1), F32),
                           jnp.broadcast_to(rel_bias[:, 2 * MAX_REL:], (heads, n_far)),
                           jnp.flip(rel_bias, axis=1),
                           jnp.broadcast_to(rel_bias[:, :1], (heads, n_neg))], axis=1)
    return vec.reshape(heads // 2, 2, CA_VEC)


def _exchange(arrays, modes, name):
    n = len(arrays)
    out_shapes = [jax.ShapeDtypeStruct((N_DEV,) + a.shape if m == "gather" else a.shape, a.dtype)
                  for a, m in zip(arrays, modes)]

    def body(*refs):
        ins, outs = refs[:n], refs[n:2 * n]
        send_sems, recv_sems, local_sems = refs[2 * n:]
        x, y, c = lax.axis_index("x"), lax.axis_index("y"), lax.axis_index("c")
        me = 4 * x + 2 * y + c
        pending = []
        for i in range(n):
            gather = modes[i] == "gather"
            local = pltpu.make_async_copy(ins[i] if gather else ins[i].at[me], outs[i].at[me], local_sems.at[i])
            local.start()
            pending.append(local)
        for k in range(1, N_DEV):
            px, py, pc = (x + (k >> 2)) % 2, (y + ((k >> 1) & 1)) % 2, (c + (k & 1)) % 2
            peer = 4 * px + 2 * py + pc
            for i in range(n):
                src = ins[i] if modes[i] == "gather" else ins[i].at[peer]
                sem = i * (N_DEV - 1) + k - 1
                send = pltpu.make_async_remote_copy(src_ref=src, dst_ref=outs[i].at[me], send_sem=send_sems.at[sem],
                                                    recv_sem=recv_sems.at[sem], device_id=(px, py, pc),
                                                    device_id_type=pl.DeviceIdType.MESH)
                send.start()
                arrival = pltpu.make_async_remote_copy(src_ref=src, dst_ref=outs[i].at[peer],
                                                       send_sem=send_sems.at[sem], recv_sem=recv_sems.at[sem],
                                                       device_id=(px, py, pc), device_id_type=pl.DeviceIdType.MESH)
                pending.append((send, arrival))
        for item in pending:
            if isinstance(item, tuple):
                item[0].wait_send()
                item[1].wait_recv()
            else:
                item.wait()

    any_spec = pl.BlockSpec(memory_space=pl.ANY)
    return pl.pallas_call(
        body, in_specs=[any_spec] * n, out_specs=[any_spec] * n, out_shape=out_shapes,
        scratch_shapes=[pltpu.SemaphoreType.DMA((n * (N_DEV - 1),)), pltpu.SemaphoreType.DMA((n * (N_DEV - 1),)),
                        pltpu.SemaphoreType.DMA((n,))],
        name=name,
    )(*arrays)


_HBM_SPEC = pl.BlockSpec(memory_space=pltpu.HBM)
_SEM_SPEC = pl.BlockSpec(memory_space=pltpu.SEMAPHORE)
_SIDE_EFFECT = pltpu.SideEffectType.DATAFLOW_SIDE_EFFECTING


def _peer(x, y, c, k):
    return (x + (k >> 2)) % 2, (y + ((k >> 1) & 1)) % 2, (c + (k & 1)) % 2


def _exchange_start(arrays, modes, after, name, carry):
    n = len(arrays)
    n_sem = n * (N_DEV - 1)
    lands = [pltpu.with_memory_space_constraint(lax.empty((N_DEV,) + a.shape if m == "gather" else a.shape, a.dtype),
                                                pltpu.HBM) for a, m in zip(arrays, modes)]
    arrays = [pltpu.with_memory_space_constraint(a, pltpu.HBM) for a in arrays]

    def body(*refs):
        ins, zones = refs[:n], refs[n:2 * n]
        send_sems, recv_sems, own_sems = refs[2 * n + 2:2 * n + 5]
        x, y, c = lax.axis_index("x"), lax.axis_index("y"), lax.axis_index("c")
        me = 4 * x + 2 * y + c
        for i in range(n):
            pltpu.make_async_copy(ins[i] if modes[i] == "gather" else ins[i].at[me], zones[i].at[me],
                                  own_sems.at[i]).start()
        for k in range(1, N_DEV):
            px, py, pc = _peer(x, y, c, k)
            peer = 4 * px + 2 * py + pc
            for i in range(n):
                sem = i * (N_DEV - 1) + k - 1
                pltpu.make_async_remote_copy(src_ref=ins[i] if modes[i] == "gather" else ins[i].at[peer],
                                             dst_ref=zones[i].at[me], send_sem=send_sems.at[sem],
                                             recv_sem=recv_sems.at[sem], device_id=(px, py, pc),
                                             device_id_type=pl.DeviceIdType.MESH).start()

    carry = pltpu.with_memory_space_constraint(carry, pltpu.HBM)
    out_shape = ((pltpu.SemaphoreType.DMA((n_sem,)), pltpu.SemaphoreType.DMA((n_sem,)), pltpu.SemaphoreType.DMA((n,)))
                 + tuple(pltpu.HBM(a.shape, a.dtype) for a in arrays) + tuple(pltpu.HBM(z.shape, z.dtype) for z in lands)
                 + (pltpu.HBM(carry.shape, carry.dtype),))
    aliases = {i: 3 + i for i in range(2 * n)}
    aliases[2 * n + 1] = 3 + 2 * n
    res = pl.pallas_call(
        body, name=name, out_shape=out_shape,
        in_specs=[_HBM_SPEC] * (2 * n) + [pl.BlockSpec(memory_space=pl.ANY), _HBM_SPEC],
        out_specs=(_SEM_SPEC,) * 3 + (_HBM_SPEC,) * (2 * n + 1),
        input_output_aliases=aliases,
        compiler_params=pltpu.CompilerParams(has_side_effects=_SIDE_EFFECT),
    )(*arrays, *lands, after, carry)
    return tuple(res[:3]), list(res[3:3 + n]), list(res[3 + n:3 + 2 * n]), res[3 + 2 * n]


def _exchange_wait(started, modes, after, name):
    sems, sources, zones, _ = started
    n = len(sources)

    def body(*refs):
        ins, lands = refs[:n], refs[n:2 * n]
        send_ref, recv_ref, own_ref = refs[2 * n:2 * n + 3]
        x, y, c = lax.axis_index("x"), lax.axis_index("y"), lax.axis_index("c")
        me = 4 * x + 2 * y + c
        for i in range(n):
            pltpu.make_async_copy(ins[i] if modes[i] == "gather" else ins[i].at[me], lands[i].at[me],
                                  own_ref.at[i]).wait()
        for k in range(1, N_DEV):
            px, py, pc = _peer(x, y, c, k)
            peer = 4 * px + 2 * py + pc
            for i in range(n):
                sem = i * (N_DEV - 1) + k - 1
                cp = pltpu.make_async_remote_copy(src_ref=ins[i] if modes[i] == "gather" else ins[i].at[peer],
                                                  dst_ref=lands[i].at[peer], send_sem=send_ref.at[sem],
                                                  recv_sem=recv_ref.at[sem], device_id=(px, py, pc),
                                                  device_id_type=pl.DeviceIdType.MESH)
                cp.wait_send()
                cp.wait_recv()

    res = pl.pallas_call(
        body, name=name,
        out_shape=tuple(pltpu.HBM(a.shape, a.dtype) for a in sources) + tuple(pltpu.HBM(z.shape, z.dtype) for z in zones),
        in_specs=[_HBM_SPEC] * (2 * n) + [_SEM_SPEC] * 3 + [pl.BlockSpec(memory_space=pl.ANY)],
        out_specs=(_HBM_SPEC,) * (2 * n), input_output_aliases={i: i for i in range(2 * n)},
        compiler_params=pltpu.CompilerParams(has_side_effects=_SIDE_EFFECT),
    )(*sources, *zones, *sems, after)
    return list(res[n:])


ADAM_TILE = 64 * 1024


def _adam_call(w, m, v, slots, name):
    n_layers, rows, cols = w.shape
    tr = rows
    if n_layers * rows * cols > ADAM_TILE:
        fits = [t for t in range(SUBLANE, rows, SUBLANE) if rows % t == 0 and n_layers * t * cols <= ADAM_TILE]
        tr = max(fits) if fits else SUBLANE
    c1 = 1.0 - ADAM_B1 ** ADAM_STEP
    c2 = 1.0 - ADAM_B2 ** ADAM_STEP

    def body(*refs):
        w_ref, m_ref, v_ref = refs[:3]
        slot_refs = refs[3:3 + n_layers]
        grad_ref, delta_ref, nm_ref, nv_ref = refs[3 + n_layers:]
        for layer in range(n_layers):
            g = slot_refs[layer][0].astype(F32)
            for k in range(1, N_DEV):
                g = g + slot_refs[layer][k].astype(F32)
            m_new = ADAM_B1 * m_ref[layer] + (1.0 - ADAM_B1) * g
            v_new = ADAM_B2 * v_ref[layer] + (1.0 - ADAM_B2) * (g * g)
            m_hat = m_new / c1
            v_hat = v_new / c2
            grad_ref[layer] = g
            delta_ref[layer] = -ADAM_LR * (m_hat / (jnp.sqrt(v_hat) + ADAM_EPS) + ADAM_WD * w_ref[layer])
            nm_ref[layer] = m_new
            nv_ref[layer] = v_new

    spec = pl.BlockSpec((n_layers, tr, cols), lambda i: (0, i, 0))
    slot_spec = pl.BlockSpec((N_DEV, tr, cols), lambda i: (0, i, 0))
    return pl.pallas_call(
        body, grid=(rows // tr,), in_specs=[spec, spec, spec] + [slot_spec] * n_layers,
        out_specs=[spec] * 4, out_shape=[jax.ShapeDtypeStruct(w.shape, F32)] * 4,
        compiler_params=_cparams(("parallel",)), name=name,
    )(w, m, v, *slots)


WEIGHT_NAMES = ['ab_norm_g', 'ab_w_in', 'ab_w_out', 's5_a_re', 's5_a_im', 's5_log_dt', 's5_b_re', 's5_b_im', 's5_c_re',
                's5_c_im', 's5_d', 's5_w_glu', 's5_b_glu', 'gdn_conv_w', 'gdn_a_log', 'gdn_dt_bias', 'gdn_out_norm_g',
                'c_norm_g', 'c_w_qkv', 'c_w_out', 'c_q_norm_g', 'c_k_norm_g', 'c_rel_bias', 'mem_norm_g', 'xa_norm_g',
                'xa_w_q', 'xa_w_kv', 'xa_w_out', 'xa_q_norm_g', 'xa_k_norm_g', 'f_norm_g', 'f_w_gate', 'f_w_up',
                'f_w_down']

SHARDED = {
    'ab_w_in': ('col', BF16), 'ab_w_out': ('row', BF16), 's5_w_glu': ('row', BF16), 'gdn_conv_w': ('col', F32),
    'c_norm_g': ('col', F32), 'c_w_qkv': ('col', BF16), 'c_w_out': ('row', BF16), 'xa_w_q': ('row', BF16),
    'xa_w_kv': ('col', BF16), 'xa_w_out': ('row', BF16), 'f_w_gate': ('col', BF16), 'f_w_up': ('col', BF16),
    'f_w_down': ('row', BF16),
}
GATHERED_AS_IS = ('c_w_qkv', 'xa_w_kv', 'f_w_gate', 'f_w_up', 'f_w_down')
REPLICATED = [n for n in WEIGHT_NAMES if n not in SHARDED]
PACK_UNIT = SUBLANE * LANE


def _full_from_gathered(g, axis):
    if axis == "row":
        return g.reshape(g.shape[0] * g.shape[1], g.shape[2])
    return jnp.transpose(g, (1, 0, 2)).reshape(g.shape[1], g.shape[0] * g.shape[2])


def _pack(arrays):
    flat = []
    for a in arrays:
        size = a.size
        padded = -(-size // PACK_UNIT) * PACK_UNIT
        flat.append(jnp.pad(a.reshape(-1), (0, padded - size)).reshape(-1, LANE))
    return jnp.concatenate(flat, axis=0)


def _unpack(buf, shapes):
    out, row = [], 0
    for shape in shapes:
        size = math.prod(shape)
        rows = -(-size // PACK_UNIT) * SUBLANE
        out.append(buf[row:row + rows].reshape(-1)[:size].reshape(shape))
        row += rows
    return out


N_STAGES = 2 * DEPTH
EVEN_SHARDED = ['ab_w_in', 'ab_w_out', 's5_w_glu', 'gdn_conv_w']
ODD_SHARDED = ['c_norm_g', 'c_w_qkv', 'c_w_out']
ALL_SHARDED = ['xa_w_q', 'xa_w_kv', 'xa_w_out', 'f_w_gate', 'f_w_up', 'f_w_down']
EVEN_SMALL = ['ab_norm_g', 's5_a_re', 's5_a_im', 's5_log_dt', 's5_b_re', 's5_b_im', 's5_c_re', 's5_c_im', 's5_d',
              's5_b_glu', 'gdn_a_log', 'gdn_dt_bias', 'gdn_out_norm_g']
ODD_SMALL = ['c_q_norm_g', 'c_k_norm_g', 'c_rel_bias']
ALL_SMALL = ['xa_norm_g', 'xa_q_norm_g', 'xa_k_norm_g', 'f_norm_g']


def _stage_params(stage):
    layer, part = divmod(stage, 2)
    if part == 1:
        return [(n, layer) for n in ALL_SHARDED], [(n, layer) for n in ALL_SMALL]
    big, small = (EVEN_SHARDED, EVEN_SMALL) if layer % 2 == 0 else (ODD_SHARDED, ODD_SMALL)
    return [(n, layer // 2) for n in big], [(n, layer // 2) for n in small]


def _stage_forward(stage, landed, small, x, mem_n):
    layer, part = divmod(stage, 2)
    big = {}
    for (n, _), g in zip(_stage_params(stage)[0], landed):
        if n == 'ab_w_in':
            big[n] = cols_to_natural(g, AB_IN_PAD)
        elif n in GATHERED_AS_IS:
            big[n] = g
        elif n == 's5_w_glu':
            big[n] = _full_from_gathered(g, 'row').astype(F32)
        else:
            big[n] = _full_from_gathered(g, SHARDED[n][0])
    if part == 1:
        h, x = rmsnorm_res(x, small['xa_norm_g'], "xa_norm")
        x = memory_cross_attention(x, h, mem_n, big['xa_w_q'], big['xa_w_kv'], big['xa_w_out'],
                                   small['xa_q_norm_g'], small['xa_k_norm_g'])
        h, x = rmsnorm_res(x, small['f_norm_g'], "f_norm")
        return swiglu(x, h, big['f_w_gate'], big['f_w_up'], big['f_w_down'])
    if layer % 2 == 0:
        h, x = rmsnorm_res(x, small['ab_norm_g'], "ab_norm")
        w_in = big['ab_w_in']
        u = linear(h, w_in[:, :S5_WIDTH])
        qkv = linear(h, w_in[:, S5_WIDTH:S5_WIDTH + 3 * GDN_WIDTH])
        gate = linear(h, w_in[:, S5_WIDTH + 3 * GDN_WIDTH:S5_WIDTH + 4 * GDN_WIDTH])
        ab = linear(h, w_in[:, S5_WIDTH + 4 * GDN_WIDTH:])
        a_out = s5_mixer(u, small['s5_a_re'], small['s5_a_im'], small['s5_log_dt'], small['s5_b_re'], small['s5_b_im'],
                         small['s5_c_re'], small['s5_c_im'], small['s5_d'], big['s5_w_glu'], small['s5_b_glu'])
        b_out = gated_deltanet(qkv, gate, ab, big['gdn_conv_w'], small['gdn_a_log'], small['gdn_dt_bias'],
                               small['gdn_out_norm_g'])
        return linear_res(jnp.concatenate([a_out, b_out], axis=1), big['ab_w_out'], x)
    h, x = rmsnorm_res(x, big['c_norm_g'].reshape(-1), "c_norm")
    return chunk_attention(x, h, big['c_w_qkv'], big['c_w_out'], small['c_q_norm_g'], small['c_k_norm_g'],
                           small['c_rel_bias'])


def _loss_rows(x, target):
    return jnp.sum(make_rowop(_loss_fn, "loss")((x, target), ())[0])


def kernel(x, mem, ab_norm_g, ab_w_in, ab_w_out, s5_a_re, s5_a_im, s5_log_dt, s5_b_re, s5_b_im, s5_c_re, s5_c_im, s5_d, s5_w_glu, s5_b_glu, gdn_conv_w, gdn_a_log, gdn_dt_bias, gdn_out_norm_g, c_norm_g, c_w_qkv, c_w_out, c_q_norm_g, c_k_norm_g, c_rel_bias, mem_norm_g, xa_norm_g, xa_w_q, xa_w_kv, xa_w_out, xa_q_norm_g, xa_k_norm_g, f_norm_g, f_w_gate, f_w_up, f_w_down, loss_target, m_ab_norm_g, m_ab_w_in, m_ab_w_out, m_s5_a_re, m_s5_a_im, m_s5_log_dt, m_s5_b_re, m_s5_b_im, m_s5_c_re, m_s5_c_im, m_s5_d, m_s5_w_glu, m_s5_b_glu, m_gdn_conv_w, m_gdn_a_log, m_gdn_dt_bias, m_gdn_out_norm_g, m_c_norm_g, m_c_w_qkv, m_c_w_out, m_c_q_norm_g, m_c_k_norm_g, m_c_rel_bias, m_mem_norm_g, m_xa_norm_g, m_xa_w_q, m_xa_w_kv, m_xa_w_out, m_xa_q_norm_g, m_xa_k_norm_g, m_f_norm_g, m_f_w_gate, m_f_w_up, m_f_w_down, v_ab_norm_g, v_ab_w_in, v_ab_w_out, v_s5_a_re, v_s5_a_im, v_s5_log_dt, v_s5_b_re, v_s5_b_im, v_s5_c_re, v_s5_c_im, v_s5_d, v_s5_w_glu, v_s5_b_glu, v_gdn_conv_w, v_gdn_a_log, v_gdn_dt_bias, v_gdn_out_norm_g, v_c_norm_g, v_c_w_qkv, v_c_w_out, v_c_q_norm_g, v_c_k_norm_g, v_c_rel_bias, v_mem_norm_g, v_xa_norm_g, v_xa_w_q, v_xa_w_kv, v_xa_w_out, v_xa_q_norm_g, v_xa_k_norm_g, v_f_norm_g, v_f_w_gate, v_f_w_up, v_f_w_down):
    given = dict(locals())
    no_after = jnp.zeros((SUBLANE, LANE), F32)

    def shard(n, idx):
        a = given[n][idx]
        return (a.reshape(1, -1) if a.ndim == 1 else a).astype(SHARDED[n][1])

    def gather_start(stage, after, carry):
        arrays = [shard(n, idx) for n, idx in _stage_params(stage)[0]]
        return _exchange_start(arrays, ["gather"] * len(arrays), after, "gather_start_%d" % stage, carry)

    act = x[0]
    mem_n, mem_pullback = jax.vjp(lambda m, g: rmsnorm(m, g, "mem_norm"), mem[0], mem_norm_g)
    started = gather_start(0, no_after, act)
    act = started[3]
    pullbacks = []
    for stage in range(N_STAGES):
        landed = _exchange_wait(started, ["gather"] * len(started[1]), act, "gather_wait_%d" % stage)
        if stage + 1 < N_STAGES:
            started = gather_start(stage + 1, landed[0], act)
            act = started[3]
        small = {n: given[n][idx] for n, idx in _stage_params(stage)[1]}
        act, pullback = jax.vjp(functools.partial(_stage_forward, stage), landed, small, act, mem_n)
        pullbacks.append(pullback)
    loss_local, loss_pullback = jax.vjp(_loss_rows, act, loss_target[0])
    d_act = loss_pullback(jnp.ones((), F32))[0]

    d_mem_n = jnp.zeros_like(mem_n)
    g_small = {}
    received = [None] * N_STAGES
    started, after = None, no_after
    for stage in reversed(range(N_STAGES)):
        d_landed, d_small, d_act, d_mem = pullbacks[stage](d_act)
        if stage % 2 == 1:
            d_mem_n = d_mem_n + d_mem
        for n, idx in _stage_params(stage)[1]:
            g_small[(n, idx)] = d_small[n]
        if started is not None:
            received[stage + 1] = _exchange_wait(started, ["scatter"] * len(started[1]), d_act,
                                                 "scatter_wait_%d" % (stage + 1))
            after = received[stage + 1][0]
        started = _exchange_start(list(d_landed), ["scatter"] * len(d_landed), after, "scatter_start_%d" % stage,
                                  d_act)
        d_act = started[3]
    g_small[('mem_norm_g', None)] = mem_pullback(d_mem_n)[1]

    def small_grad(n):
        if n == 'mem_norm_g':
            return g_small[(n, None)]
        return jnp.stack([g_small[(n, i)] for i in range(given[n].shape[0])], axis=0)

    packed = _exchange([_pack([small_grad(n) for n in REPLICATED])], ["gather"], "small_grads_allgather")[0]
    received[0] = _exchange_wait(started, ["scatter"] * len(started[1]), packed, "scatter_wait_0")

    results = {}
    for n in SHARDED:
        slots = {}
        for stage in range(N_STAGES):
            for (pn, idx), r in zip(_stage_params(stage)[0], received[stage]):
                if pn == n:
                    slots[idx] = r
        shape = given[n].shape
        to3d = lambda a: a.reshape(a.shape[0], -1, a.shape[-1])
        outs = _adam_call(to3d(given[n]), to3d(given['m_' + n]), to3d(given['v_' + n]),
                          [slots[i] for i in range(len(slots))], "adamw_" + n)
        results[n] = [o.reshape(shape) for o in outs]
    outs = _adam_call(_pack([given[n] for n in REPLICATED])[None], _pack([given['m_' + n] for n in REPLICATED])[None],
                      _pack([given['v_' + n] for n in REPLICATED])[None], [packed], "adamw_replicated")
    shapes = [given[n].shape for n in REPLICATED]
    for j, parts in enumerate(zip(*[_unpack(o[0], shapes) for o in outs])):
        results[REPLICATED[j]] = list(parts)

    loss = lax.psum(loss_local, ("x", "y", "c"))
    return (loss, d_act[None], *[results[n][0] for n in WEIGHT_NAMES], *[results[n][1] for n in WEIGHT_NAMES],
            *[results[n][2] for n in WEIGHT_NAMES], *[results[n][3] for n in WEIGHT_NAMES])
```

```python
import functools
import math

import jax
import jax.numpy as jnp
import numpy as np
from jax import lax
from jax.experimental import pallas as pl
from jax.experimental.pallas import tpu as pltpu

F32 = jnp.float32
BF16 = jnp.bfloat16
HI = lax.Precision.HIGHEST

N_DEV = 8
D_MODEL = 1024
SEQ = 2048
DEPTH = 4
CHUNK = 64
N_MEM = 256
RMS_EPS = 1e-6
S5_WIDTH = 512
S5_GROUP = 16
S5_GROUPS = 32
S5_STATE = 64
GDN_HEAD_DIM = 128
GDN_WIDTH = 512
GDN_HEADS = 4
GDN_CONV = 4
AB_IN = S5_WIDTH + 4 * GDN_WIDTH + 2 * GDN_HEADS
AB_IN_PAD = 2688
CA_HEADS = 16
CA_HEAD_DIM = 64
CA_LEFT = 8
CA_BAND = (CA_LEFT + 1) * CHUNK
CA_PAD = CA_LEFT * CHUNK
MAX_REL = 128
XA_HEADS = 4
XA_HEAD_DIM = 256
FFN = 2816
ADAM_LR, ADAM_B1, ADAM_B2, ADAM_EPS, ADAM_WD, ADAM_STEP = 0.001, 0.9, 0.999, 1e-08, 0.01, 10

VMEM_LIMIT = 48 * 1024 * 1024
LANE = 128
SUBLANE = 8


def _cparams(sem=None):
    return pltpu.CompilerParams(dimension_semantics=sem, vmem_limit_bytes=VMEM_LIMIT)


def _divisor_tile(n, target, unit=LANE):
    if n <= target:
        return n
    best = None
    for t in range(unit, target + 1, unit):
        if n % t == 0:
            best = t
    assert best is not None, (n, target)
    return best


def _matmul(a, b, *, ta=False, tb=False, out_dtype=F32, name="mm", res=None):
    if ta:
        k_dim, m_dim = a.shape
    else:
        m_dim, k_dim = a.shape
    if tb:
        n_dim, kb = b.shape
    else:
        kb, n_dim = b.shape
    assert kb == k_dim, (a.shape, b.shape, ta, tb)
    tm = _divisor_tile(m_dim, 1024)
    tn = _divisor_tile(n_dim, 512)
    tk = _divisor_tile(k_dim, 1408)
    nk = k_dim // tk
    dims = (((0 if ta else 1,), (1 if tb else 0,)), ((), ()))

    def body(a_ref, b_ref, *rest):
        res_ref = rest[0] if res is not None else None
        o_ref, acc_ref = rest[-2:]
        k = pl.program_id(2)

        @pl.when(k == 0)
        def _():
            acc_ref[...] = jnp.zeros_like(acc_ref)

        acc_ref[...] += lax.dot_general(a_ref[...].astype(BF16), b_ref[...].astype(BF16), dims,
                                        preferred_element_type=F32)

        @pl.when(k == nk - 1)
        def _():
            total = acc_ref[...] if res is None else acc_ref[...] + res_ref[...]
            o_ref[...] = total.astype(o_ref.dtype)

    a_spec = pl.BlockSpec((tk, tm), lambda i, j, k: (k, i)) if ta else pl.BlockSpec((tm, tk), lambda i, j, k: (i, k))
    b_spec = pl.BlockSpec((tn, tk), lambda i, j, k: (j, k)) if tb else pl.BlockSpec((tk, tn), lambda i, j, k: (k, j))
    o_spec = pl.BlockSpec((tm, tn), lambda i, j, k: (i, j))
    return pl.pallas_call(
        body,
        grid=(m_dim // tm, n_dim // tn, nk),
        in_specs=[a_spec, b_spec] + ([o_spec] if res is not None else []),
        out_specs=o_spec,
        out_shape=jax.ShapeDtypeStruct((m_dim, n_dim), out_dtype),
        scratch_shapes=[pltpu.VMEM((tm, tn), F32)],
        compiler_params=_cparams(("parallel", "parallel", "arbitrary")),
        name=name,
    )(*((a, b) if res is None else (a, b, res)))


@jax.custom_vjp
def linear(a, w):
    return _matmul(a, w, name="linear_fwd")


def _linear_fwd(a, w):
    return _matmul(a, w, name="linear_fwd"), (a, w)


def _linear_bwd(res, dy):
    a, w = res
    da = _matmul(dy, w, tb=True, name="linear_da")
    dw = _matmul(a, dy, ta=True, out_dtype=w.dtype, name="linear_dw")
    return da, dw


linear.defvjp(_linear_fwd, _linear_bwd)


@jax.custom_vjp
def linear_res(a, w, x):
    return _matmul(a, w, name="linear_res_fwd", res=x)


def _linear_res_fwd(a, w, x):
    return _matmul(a, w, name="linear_res_fwd", res=x), (a, w)


def _linear_res_bwd(res, dy):
    return _linear_bwd(res, dy) + (dy,)


linear_res.defvjp(_linear_res_fwd, _linear_res_bwd)


def _mm_call(name, a, b, out_struct, grid, a_spec, b_spec, o_spec, dims, lead, res=None, keep_a=None):
    nk = grid[-1]
    acc_shape = o_spec.block_shape[1:] if lead[2] else o_spec.block_shape
    a_shape = a_spec.block_shape[1:] if lead[0] else a_spec.block_shape
    assert keep_a is None or nk == 1

    def body(a_ref, b_ref, *rest):
        res_ref = rest[0] if res is not None else None
        o_ref, acc_ref = rest[-1 - (keep_a is not None) - 1], rest[-1 - (keep_a is not None)]
        k = pl.program_id(len(grid) - 1)

        @pl.when(k == 0)
        def _():
            acc_ref[...] = jnp.zeros_like(acc_ref)

        if keep_a is None:
            av = (a_ref[0] if lead[0] else a_ref[...]).astype(BF16)
        else:
            a16_ref = rest[-1]

            @pl.when(pl.program_id(keep_a) == 0)
            def _():
                a16_ref[...] = (a_ref[0] if lead[0] else a_ref[...]).astype(BF16)

            av = a16_ref[...]
        bv = b_ref[0] if lead[1] else b_ref[...]
        acc_ref[...] += lax.dot_general(av, bv.astype(BF16), dims, preferred_element_type=F32)

        @pl.when(k == nk - 1)
        def _():
            if lead[2]:
                o_ref[0] = acc_ref[...].astype(o_ref.dtype)
            elif res is not None:
                o_ref[...] = (acc_ref[...] + res_ref[...]).astype(o_ref.dtype)
            else:
                o_ref[...] = acc_ref[...].astype(o_ref.dtype)

    return pl.pallas_call(
        body, grid=grid, in_specs=[a_spec, b_spec] + ([o_spec] if res is not None else []), out_specs=o_spec,
        out_shape=out_struct,
        scratch_shapes=[pltpu.VMEM(tuple(acc_shape), F32)] + ([pltpu.VMEM(tuple(a_shape), BF16)] if keep_a is not None else []),
        compiler_params=_cparams(("parallel", "arbitrary", "arbitrary")), name=name,
    )(*((a, b) if res is None else (a, b, res)))


_NN = (((1,), (0,)), ((), ()))
_NT_DIMS = (((1,), (1,)), ((), ()))
_TN_DIMS = (((0,), (0,)), ((), ()))


def _cols_fwd(a, g, dm_out):
    m_dim, k_dim = a.shape
    _, _, c_dim = g.shape
    tm = _divisor_tile(m_dim, 1024)
    tk = _divisor_tile(k_dim, 1024)
    a_spec = pl.BlockSpec((tm, tk), lambda i, j, k: (i, k))
    b_spec = pl.BlockSpec((1, tk, c_dim), lambda i, j, k: (j, k, 0))
    if dm_out:
        o_spec = pl.BlockSpec((1, tm, c_dim), lambda i, j, k: (j, i, 0))
        out = jax.ShapeDtypeStruct((N_DEV, m_dim, c_dim), F32)
    else:
        o_spec = pl.BlockSpec((tm, c_dim), lambda i, j, k: (i, j))
        out = jax.ShapeDtypeStruct((m_dim, N_DEV * c_dim), F32)
    return _mm_call("cols_fwd", a, g, out, (m_dim // tm, N_DEV, k_dim // tk), a_spec, b_spec, o_spec, _NN,
                    (False, True, dm_out), keep_a=1 if k_dim == tk else None)


def _cols_da(dy, g, dm_out):
    _, k_dim, c_dim = g.shape
    m_dim = dy.shape[1] if dm_out else dy.shape[0]
    tm = _divisor_tile(m_dim, 1024)
    tk = _divisor_tile(k_dim, 1024)
    if dm_out:
        a_spec = pl.BlockSpec((1, tm, c_dim), lambda i, kb, j: (j, i, 0))
    else:
        a_spec = pl.BlockSpec((tm, c_dim), lambda i, kb, j: (i, j))
    b_spec = pl.BlockSpec((1, tk, c_dim), lambda i, kb, j: (j, kb, 0))
    o_spec = pl.BlockSpec((tm, tk), lambda i, kb, j: (i, kb))
    return _mm_call("cols_da", dy, g, jax.ShapeDtypeStruct((m_dim, k_dim), F32), (m_dim // tm, k_dim // tk, N_DEV),
                    a_spec, b_spec, o_spec, _NT_DIMS, (dm_out, True, False))


def _cols_dg(a, dy, g, dm_out):
    _, k_dim, c_dim = g.shape
    m_dim = a.shape[0]
    tm = _divisor_tile(m_dim, 2048)
    tk = _divisor_tile(k_dim, 512)
    a_spec = pl.BlockSpec((tm, tk), lambda kb, j, m: (m, kb))
    if dm_out:
        b_spec = pl.BlockSpec((1, tm, c_dim), lambda kb, j, m: (j, m, 0))
    else:
        b_spec = pl.BlockSpec((tm, c_dim), lambda kb, j, m: (m, j))
    o_spec = pl.BlockSpec((1, tk, c_dim), lambda kb, j, m: (j, kb, 0))
    return _mm_call("cols_dg", a, dy, jax.ShapeDtypeStruct(g.shape, g.dtype), (k_dim // tk, N_DEV, m_dim // tm),
                    a_spec, b_spec, o_spec, _TN_DIMS, (False, dm_out, True), keep_a=1 if m_dim == tm else None)


def _make_linear_cols(dm_out):
    @jax.custom_vjp
    def op(a, g):
        return _cols_fwd(a, g, dm_out)

    def fwd(a, g):
        return _cols_fwd(a, g, dm_out), (a, g)

    def bwd(res, dy):
        a, g = res
        return _cols_da(dy, g, dm_out), _cols_dg(a, dy, g, dm_out)

    op.defvjp(fwd, bwd)
    return op


linear_cols = _make_linear_cols(False)
linear_cols_dm = _make_linear_cols(True)


def _silu_mul(g, u):
    return _silu(g) * u


def _ffn_down_fwd(g, u, w, x):
    _, m_dim, r_dim = g.shape
    n_dim = w.shape[2]
    tm = _divisor_tile(m_dim, 1024)
    tn = _divisor_tile(n_dim, 1024)
    nj = N_DEV

    def body(g_ref, u_ref, w_ref, x_ref, o_ref, acc_ref):
        j = pl.program_id(2)

        @pl.when(j == 0)
        def _():
            acc_ref[...] = jnp.zeros_like(acc_ref)

        acc_ref[...] += _bdot(_silu_mul(g_ref[0], u_ref[0]), w_ref[0])

        @pl.when(j == nj - 1)
        def _():
            o_ref[...] = acc_ref[...] + x_ref[...]

    h_spec = pl.BlockSpec((1, tm, r_dim), lambda i, n, j: (j, i, 0))
    o_spec = pl.BlockSpec((tm, tn), lambda i, n, j: (i, n))
    return pl.pallas_call(
        body, grid=(m_dim // tm, n_dim // tn, nj),
        in_specs=[h_spec, h_spec, pl.BlockSpec((1, r_dim, tn), lambda i, n, j: (j, 0, n)), o_spec], out_specs=o_spec,
        out_shape=jax.ShapeDtypeStruct((m_dim, n_dim), F32), scratch_shapes=[pltpu.VMEM((tm, tn), F32)],
        compiler_params=_cparams(("parallel", "parallel", "arbitrary")), name="ffn_down_fwd",
    )(g, u, w, x)


def _ffn_down_dh(dy, g, u, w):
    m_dim, n_dim = dy.shape
    r_dim = w.shape[1]
    tm = _divisor_tile(m_dim, 1024)
    tn = _divisor_tile(n_dim, 1024)
    nn = n_dim // tn

    def body(dy_ref, w_ref, g_ref, u_ref, dg_ref, du_ref, acc_ref):
        n = pl.program_id(2)

        @pl.when(n == 0)
        def _():
            acc_ref[...] = jnp.zeros_like(acc_ref)

        acc_ref[...] += _bdot(dy_ref[...], w_ref[0], _NT_DIMS)

        @pl.when(n == nn - 1)
        def _():
            _, pullback = jax.vjp(_silu_mul, g_ref[0], u_ref[0])
            dg_ref[0], du_ref[0] = pullback(acc_ref[...])

    h_spec = pl.BlockSpec((1, tm, r_dim), lambda i, j, n: (j, i, 0))
    return pl.pallas_call(
        body, grid=(m_dim // tm, N_DEV, nn),
        in_specs=[pl.BlockSpec((tm, tn), lambda i, j, n: (i, n)), pl.BlockSpec((1, r_dim, tn), lambda i, j, n: (j, 0, n)),
                  h_spec, h_spec],
        out_specs=[h_spec, h_spec], out_shape=[jax.ShapeDtypeStruct(g.shape, F32)] * 2,
        scratch_shapes=[pltpu.VMEM((tm, r_dim), F32)],
        compiler_params=_cparams(("parallel", "arbitrary", "arbitrary")), name="ffn_down_dh",
    )(dy, w, g, u)


def _ffn_down_dw(g, u, dy, w):
    _, m_dim, r_dim = g.shape
    n_dim = dy.shape[1]
    tm = _divisor_tile(m_dim, 2048)
    tn = _divisor_tile(n_dim, 512)
    nm = m_dim // tm

    def body(g_ref, u_ref, dy_ref, o_ref, acc_ref):
        m = pl.program_id(2)

        @pl.when(m == 0)
        def _():
            acc_ref[...] = jnp.zeros_like(acc_ref)

        acc_ref[...] += _bdot(_silu_mul(g_ref[0], u_ref[0]), dy_ref[...], _TN_DIMS)

        @pl.when(m == nm - 1)
        def _():
            o_ref[0] = acc_ref[...].astype(o_ref.dtype)

    h_spec = pl.BlockSpec((1, tm, r_dim), lambda j, n, m: (j, m, 0))
    return pl.pallas_call(
        body, grid=(N_DEV, n_dim // tn, nm),
        in_specs=[h_spec, h_spec, pl.BlockSpec((tm, tn), lambda j, n, m: (m, n))],
        out_specs=pl.BlockSpec((1, r_dim, tn), lambda j, n, m: (j, 0, n)),
        out_shape=jax.ShapeDtypeStruct(w.shape, w.dtype), scratch_shapes=[pltpu.VMEM((r_dim, tn), F32)],
        compiler_params=_cparams(("parallel", "parallel", "arbitrary")), name="ffn_down_dw",
    )(g, u, dy)


@jax.custom_vjp
def ffn_down(g, u, w, x):
    return _ffn_down_fwd(g, u, w, x)


def _ffn_down_vjp_fwd(g, u, w, x):
    return _ffn_down_fwd(g, u, w, x), (g, u, w)


def _ffn_down_vjp_bwd(res, dy):
    g, u, w = res
    dg, du = _ffn_down_dh(dy, g, u, w)
    return dg, du, _ffn_down_dw(g, u, dy, w), dy


ffn_down.defvjp(_ffn_down_vjp_fwd, _ffn_down_vjp_bwd)


def _cols_to_natural_call(g, width):
    _, k_dim, c_dim = g.shape
    tk = _divisor_tile(k_dim, 256, SUBLANE)

    def body(g_ref, o_ref):
        for j in range(N_DEV):
            o_ref[:, j * c_dim:(j + 1) * c_dim] = g_ref[j]
        if width > N_DEV * c_dim:
            o_ref[:, N_DEV * c_dim:] = jnp.zeros((tk, width - N_DEV * c_dim), o_ref.dtype)

    return pl.pallas_call(
        body, grid=(k_dim // tk,), in_specs=[pl.BlockSpec((N_DEV, tk, c_dim), lambda i: (0, i, 0))],
        out_specs=pl.BlockSpec((tk, width), lambda i: (i, 0)), out_shape=jax.ShapeDtypeStruct((k_dim, width), g.dtype),
        compiler_params=_cparams(("parallel",)), name="cols_to_natural",
    )(g)


def _natural_to_cols_call(w, c_dim):
    k_dim, width = w.shape
    tk = _divisor_tile(k_dim, 256, SUBLANE)

    def body(w_ref, o_ref):
        for j in range(N_DEV):
            o_ref[j] = w_ref[:, j * c_dim:(j + 1) * c_dim]

    return pl.pallas_call(
        body, grid=(k_dim // tk,), in_specs=[pl.BlockSpec((tk, width), lambda i: (i, 0))],
        out_specs=pl.BlockSpec((N_DEV, tk, c_dim), lambda i: (0, i, 0)),
        out_shape=jax.ShapeDtypeStruct((N_DEV, k_dim, c_dim), w.dtype),
        compiler_params=_cparams(("parallel",)), name="natural_to_cols",
    )(w)


@functools.partial(jax.custom_vjp, nondiff_argnums=(1,))
def cols_to_natural(g, width):
    return _cols_to_natural_call(g, width)


def _cols_to_natural_fwd(g, width):
    return _cols_to_natural_call(g, width), g.shape[2]


def _cols_to_natural_bwd(width, c_dim, dw):
    return (_natural_to_cols_call(dw, c_dim),)


cols_to_natural.defvjp(_cols_to_natural_fwd, _cols_to_natural_bwd)


def make_rowop(fn, name, tm=256, passthrough=0):
    def specs(rows, params):
        row_specs = [pl.BlockSpec((tm, r.shape[1]), lambda i: (i, 0)) for r in rows]
        par_specs = [pl.BlockSpec(p.shape, lambda i: (0, 0)) for p in params]
        return row_specs, par_specs

    def out_structs(rows, params):
        tiles = [jax.ShapeDtypeStruct((tm, r.shape[1]), r.dtype) for r in rows]
        return jax.eval_shape(lambda r, p: fn(*r, *p), tiles, list(params))

    def fwd_call(rows, params):
        m_dim = rows[0].shape[0]
        n_in = len(rows) + len(params)
        outs = out_structs(rows, params)

        def body(*refs):
            res = fn(*[r[...] for r in refs[:n_in]])
            for o_ref, r in zip(refs[n_in:], res):
                o_ref[...] = r.astype(o_ref.dtype)

        row_specs, par_specs = specs(rows, params)
        return pl.pallas_call(
            body,
            grid=(m_dim // tm,),
            in_specs=row_specs + par_specs,
            out_specs=[pl.BlockSpec((tm, o.shape[1]), lambda i: (i, 0)) for o in outs],
            out_shape=[jax.ShapeDtypeStruct((m_dim, o.shape[1]), o.dtype) for o in outs],
            compiler_params=_cparams(("parallel",)),
            name=name + "_fwd",
        )(*rows, *params)

    def bwd_call(rows, params, cts):
        m_dim = rows[0].shape[0]
        n_rows, n_par = len(rows), len(params)
        n_in = n_rows + n_par
        n_ct = len(cts)
        n_fn = n_ct - passthrough

        def body(*refs):
            vals = [r[...] for r in refs[:n_in]]
            ct_vals = tuple(r[...] for r in refs[n_in:n_in + n_fn])
            pass_refs = refs[n_in + n_fn:n_in + n_ct]
            drow_refs = refs[n_in + n_ct:n_in + n_ct + n_rows]
            dpar_refs = refs[n_in + n_ct + n_rows:]
            _, pullback = jax.vjp(fn, *vals)
            grads = pullback(ct_vals)
            for i, (d_ref, g) in enumerate(zip(drow_refs, grads[:n_rows])):
                d_ref[...] = g + pass_refs[i][...] if i < passthrough else g

            @pl.when(pl.program_id(0) == 0)
            def _():
                for d_ref in dpar_refs:
                    d_ref[...] = jnp.zeros_like(d_ref)

            for d_ref, g in zip(dpar_refs, grads[n_rows:]):
                d_ref[...] += g

        row_specs, par_specs = specs(rows, params)
        ct_specs = [pl.BlockSpec((tm, c.shape[1]), lambda i: (i, 0)) for c in cts]
        res = pl.pallas_call(
            body,
            grid=(m_dim // tm,),
            in_specs=row_specs + par_specs + ct_specs,
            out_specs=row_specs + par_specs,
            out_shape=[jax.ShapeDtypeStruct(r.shape, r.dtype) for r in rows]
            + [jax.ShapeDtypeStruct(p.shape, p.dtype) for p in params],
            compiler_params=_cparams(("arbitrary",)),
            name=name + "_bwd",
        )(*rows, *params, *cts)
        return tuple(res[:n_rows]), tuple(res[n_rows:])

    @jax.custom_vjp
    def op(rows, params):
        return tuple(fwd_call(rows, params)) + tuple(rows[:passthrough])

    def op_fwd(rows, params):
        return tuple(fwd_call(rows, params)) + tuple(rows[:passthrough]), (rows, params)

    def op_bwd(res, cts):
        rows, params = res
        return bwd_call(rows, params, tuple(cts))

    op.defvjp(op_fwd, op_bwd)
    return op


def _rms(x, g):
    return x * lax.rsqrt(jnp.mean(x * x, axis=-1, keepdims=True) + RMS_EPS) * g


def _sigmoid(x):
    return 1.0 / (1.0 + jnp.exp(-x))


def _silu(x):
    return x * _sigmoid(x)


def _bdot(a, b, dims=(((1,), (0,)), ((), ()))):
    return lax.dot_general(a.astype(BF16), b.astype(BF16), dims, preferred_element_type=F32)


def _rmsnorm_fn(x, g):
    return (_rms(x, g),)


def rmsnorm(x, g, name):
    return make_rowop(_rmsnorm_fn, name)((x,), (g.reshape(1, -1),))[0]


def rmsnorm_res(x, g, name):
    return make_rowop(_rmsnorm_fn, name, passthrough=1)((x,), (g.reshape(1, -1),))


def _gelu_tanh(x):
    return 0.5 * x * (1.0 + jnp.tanh(0.7978845608028654 * (x + 0.044715 * x * x * x)))


def _softplus(x):
    return jnp.maximum(x, 0.0) + jnp.log(1.0 + jnp.exp(-jnp.abs(x)))


def _s5_post_fn(y, w_glu, b_glu):
    h = _gelu_tanh(y)
    return (h * _sigmoid(_bdot(h, w_glu) + b_glu),)


def _loss_fn(y, t):
    err = y - t
    return (0.5 * jnp.mean(err * err, axis=-1, keepdims=True),)


def _pair_headnorm(x, g2):
    lo = lax.broadcasted_iota(jnp.int32, x.shape, 1) < CA_HEAD_DIM
    sq = x * x
    s_lo = jnp.sum(jnp.where(lo, sq, 0.0), axis=-1, keepdims=True)
    s_hi = jnp.sum(jnp.where(lo, 0.0, sq), axis=-1, keepdims=True)
    ms = jnp.where(lo, s_lo, s_hi) * (1.0 / CA_HEAD_DIM)
    return x * lax.rsqrt(ms + RMS_EPS) * g2


def _ca_qknorm_fn(qkv, qg2, kg2):
    qs, ks = [], []
    for j in range(D_MODEL // LANE):
        qs.append(_pair_headnorm(qkv[:, j * LANE:(j + 1) * LANE], qg2))
        ks.append(_pair_headnorm(qkv[:, D_MODEL + j * LANE:D_MODEL + (j + 1) * LANE], kg2))
    return jnp.concatenate(qs, axis=1), jnp.concatenate(ks, axis=1)


def _xattn_fn(q, k, v, qg, kg):
    outs = []
    for h in range(XA_HEADS):
        sl = slice(h * XA_HEAD_DIM, (h + 1) * XA_HEAD_DIM)
        qh = _rms(q[:, sl], qg)
        kh = _rms(k[:, sl], kg)
        s = _bdot(qh, kh, (((1,), (1,)), ((), ()))) * (XA_HEAD_DIM ** -0.5)
        p = jnp.exp(s - jnp.max(s, axis=-1, keepdims=True))
        p = p / jnp.sum(p, axis=-1, keepdims=True)
        outs.append(_bdot(p, v[:, sl]))
    return (jnp.concatenate(outs, axis=1),)


def _gdn_prep_fn(x0, x1, x2, x3, ab, conv_w, alog, dtb):
    c = conv_w[3:4, :] * x0 + conv_w[2:3, :] * x1 + conv_w[1:2, :] * x2 + conv_w[0:1, :] * x3
    c = _silu(c)
    qs, ks = [], []
    for h in range(GDN_HEADS):
        qh = c[:, h * LANE:(h + 1) * LANE]
        kh = c[:, GDN_WIDTH + h * LANE:GDN_WIDTH + (h + 1) * LANE]
        qs.append(qh * lax.rsqrt(jnp.sum(qh * qh, axis=-1, keepdims=True) + RMS_EPS) * (GDN_HEAD_DIM ** -0.5))
        ks.append(kh * lax.rsqrt(jnp.sum(kh * kh, axis=-1, keepdims=True) + RMS_EPS))
    lane = lax.broadcasted_iota(jnp.int32, ab.shape, 1)
    g = -jnp.exp(alog) * _softplus(ab + dtb)
    beta = _sigmoid(ab)
    bg = jnp.where(lane < GDN_HEADS, g, jnp.where(lane < 2 * GDN_HEADS, beta, 0.0))
    return jnp.concatenate(qs, axis=1), jnp.concatenate(ks, axis=1), c[:, 2 * GDN_WIDTH:], bg


def _gdn_out_fn(o, gate, og):
    outs = []
    for h in range(GDN_HEADS):
        sl = slice(h * LANE, (h + 1) * LANE)
        outs.append(_rms(o[:, sl], og) * _silu(gate[:, sl]))
    return (jnp.concatenate(outs, axis=1),)


CA_QB = 4 * CHUNK
CA_KB = CA_QB + CA_PAD


def _ca_math(q2, kb2, vb2, bias2, c):
    lane = lax.broadcasted_iota(jnp.int32, q2.shape, 1)
    qc = lax.broadcasted_iota(jnp.int32, (CA_QB, CA_KB), 0) // CHUNK
    kc = lax.broadcasted_iota(jnp.int32, (CA_QB, CA_KB), 1) // CHUNK
    valid = (kc >= qc) & (kc <= qc + CA_LEFT) & (kc + c * (CA_QB // CHUNK) >= CA_LEFT)
    out = jnp.zeros(q2.shape, F32)
    for h in range(2):
        mine = (lane >= h * CA_HEAD_DIM) & (lane < (h + 1) * CA_HEAD_DIM)
        qh = jnp.where(mine, q2, 0.0)
        s = _bdot(qh, kb2, (((1,), (1,)), ((), ()))) * (CA_HEAD_DIM ** -0.5) + bias2[h]
        s = jnp.where(valid, s, -1e30)
        p = jnp.exp(s - jnp.max(s, axis=-1, keepdims=True))
        p = p / jnp.sum(p, axis=-1, keepdims=True)
        out = out + jnp.where(mine, _bdot(p, vb2), 0.0)
    return out


CA_VEC = CA_QB + CA_KB


def _ca_specs(seq):
    q_spec = pl.BlockSpec((CA_QB, LANE), lambda hp, c: (c, hp))
    kv_spec = pl.BlockSpec((seq + CA_PAD, LANE), lambda hp, c: (0, hp))
    b_spec = pl.BlockSpec((1, 2, CA_VEC), lambda hp, c: (hp, 0, 0))
    return (D_MODEL // LANE, seq // CA_QB), q_spec, kv_spec, b_spec


def _ca_bias_from_vector(vec_ref, bias_ref):
    for h in range(2):
        rows = jnp.broadcast_to(vec_ref[0, h:h + 1, :], (CA_QB, CA_VEC))
        bias_ref[h] = pltpu.roll(rows, 0, 1, stride=1, stride_axis=0)[:, CA_QB:]


def _ca_vector_grad(dbias):
    d = jnp.concatenate([jnp.zeros((CA_QB, CA_QB), F32), dbias], axis=1)
    row = lax.broadcasted_iota(jnp.int32, d.shape, 0)
    for bit in range(CA_QB.bit_length() - 1):
        d = jnp.where((row >> bit) & 1 == 1, pltpu.roll(d, CA_VEC - (1 << bit), 1), d)
    return jnp.sum(d, axis=0, keepdims=True)


def _ca_fwd_call(q, kpad, vpad, vec):
    grid, q_spec, kv_spec, b_spec = _ca_specs(q.shape[0])

    def body(q_ref, k_ref, v_ref, vec_ref, o_ref, bias_ref):
        c = pl.program_id(1)
        start = pl.multiple_of(c * CA_QB, CA_QB)

        @pl.when(c == 0)
        def _():
            _ca_bias_from_vector(vec_ref, bias_ref)

        o_ref[...] = _ca_math(q_ref[...], k_ref[pl.ds(start, CA_KB), :], v_ref[pl.ds(start, CA_KB), :],
                              bias_ref[...], c)

    return pl.pallas_call(
        body, grid=grid, in_specs=[q_spec, kv_spec, kv_spec, b_spec], out_specs=q_spec,
        out_shape=jax.ShapeDtypeStruct(q.shape, F32), scratch_shapes=[pltpu.VMEM((2, CA_QB, CA_KB), F32)],
        compiler_params=_cparams(("parallel", "arbitrary")), name="chunkattn_fwd",
    )(q, kpad, vpad, vec)


def _ca_bwd_call(q, kpad, vpad, vec, do):
    grid, q_spec, kv_spec, b_spec = _ca_specs(q.shape[0])
    last = grid[1] - 1

    def body(q_ref, k_ref, v_ref, vec_ref, do_ref, dq_ref, dk_ref, dv_ref, dvec_ref, bias_ref, dbias_ref):
        c = pl.program_id(1)
        start = pl.multiple_of(c * CA_QB, CA_QB)

        @pl.when(c == 0)
        def _():
            _ca_bias_from_vector(vec_ref, bias_ref)
            dk_ref[...] = jnp.zeros_like(dk_ref)
            dv_ref[...] = jnp.zeros_like(dv_ref)
            dbias_ref[...] = jnp.zeros_like(dbias_ref)

        _, pullback = jax.vjp(lambda a, b, d, e: _ca_math(a, b, d, e, c), q_ref[...],
                              k_ref[pl.ds(start, CA_KB), :], v_ref[pl.ds(start, CA_KB), :], bias_ref[...])
        dq, dkb, dvb, dbias = pullback(do_ref[...])
        dq_ref[...] = dq
        dk_ref[pl.ds(start, CA_KB), :] += dkb
        dv_ref[pl.ds(start, CA_KB), :] += dvb
        dbias_ref[...] += dbias

        @pl.when(c == last)
        def _():
            for h in range(2):
                dvec_ref[0, h:h + 1, :] = _ca_vector_grad(dbias_ref[h])

    return pl.pallas_call(
        body, grid=grid, in_specs=[q_spec, kv_spec, kv_spec, b_spec, q_spec],
        out_specs=[q_spec, kv_spec, kv_spec, b_spec],
        out_shape=[jax.ShapeDtypeStruct(q.shape, F32), jax.ShapeDtypeStruct(kpad.shape, F32),
                   jax.ShapeDtypeStruct(vpad.shape, F32), jax.ShapeDtypeStruct(vec.shape, F32)],
        scratch_shapes=[pltpu.VMEM((2, CA_QB, CA_KB), F32), pltpu.VMEM((2, CA_QB, CA_KB), F32)],
        compiler_params=_cparams(("parallel", "arbitrary")), name="chunkattn_bwd",
    )(q, kpad, vpad, vec, do)


@jax.custom_vjp
def chunk_attn_core(q, kpad, vpad, vec):
    return _ca_fwd_call(q, kpad, vpad, vec)


def _ca_core_fwd(q, kpad, vpad, vec):
    return _ca_fwd_call(q, kpad, vpad, vec), (q, kpad, vpad, vec)


def _ca_core_bwd(res, do):
    return tuple(_ca_bwd_call(*res, do))


chunk_attn_core.defvjp(_ca_core_fwd, _ca_core_bwd)


S5_GB = 4
S5_U = S5_WIDTH // S5_GB
S5_L = S5_GROUPS * S5_STATE // S5_GB


def _cmul(ar, ai, br, bi):
    return ar * br - ai * bi, ar * bi + ai * br


def _hdot(a, b, dims=(((1,), (0,)), ((), ()))):
    return lax.dot_general(a, b, dims, precision=HI, preferred_element_type=F32)


_NT = (((1,), (1,)), ((), ()))
_TN = (((0,), (0,)), ((), ()))


def _s5_tables(lr, li, reverse):
    p = {1: (lr, li)}
    p[2] = _cmul(*p[1], *p[1])
    p[4] = _cmul(*p[2], *p[2])
    p[3] = _cmul(*p[2], *p[1])
    p[5] = _cmul(*p[4], *p[1])
    p[6] = _cmul(*p[4], *p[2])
    p[7] = _cmul(*p[4], *p[3])
    p[8] = _cmul(*p[4], *p[4])
    row = lax.broadcasted_iota(jnp.int32, (SUBLANE, lr.shape[1]), 0)
    tr = jnp.zeros(row.shape, F32)
    ti = jnp.zeros(row.shape, F32)
    for i in range(SUBLANE):
        k = SUBLANE - i if reverse else i + 1
        tr = jnp.where(row == i, p[k][0], tr)
        ti = jnp.where(row == i, p[k][1], ti)
    return p, (tr, ti), row


def _s5_block_scan(xr, xi, p, tab, row, hr, hi, reverse):
    for k in (1, 2, 4):
        if reverse:
            sr = jnp.where(row < SUBLANE - k, pltpu.roll(xr, SUBLANE - k, 0), 0.0)
            si = jnp.where(row < SUBLANE - k, pltpu.roll(xi, SUBLANE - k, 0), 0.0)
        else:
            sr = jnp.where(row >= k, pltpu.roll(xr, k, 0), 0.0)
            si = jnp.where(row >= k, pltpu.roll(xi, k, 0), 0.0)
        ar, ai = _cmul(p[k][0], p[k][1], sr, si)
        xr, xi = xr + ar, xi + ai
    cr, ci = _cmul(tab[0], tab[1], hr, hi)
    return xr + cr, xi + ci


def _s5_forward_scan(sr_ref, si_ref, lr, li):
    n_blocks = sr_ref.shape[0] // SUBLANE
    p, tab, row = _s5_tables(lr, li, False)

    def step(b, carry):
        base = pl.multiple_of(b * SUBLANE, SUBLANE)
        xr, xi = _s5_block_scan(sr_ref[pl.ds(base, SUBLANE), :], si_ref[pl.ds(base, SUBLANE), :],
                                p, tab, row, carry[0], carry[1], False)
        sr_ref[pl.ds(base, SUBLANE), :] = xr
        si_ref[pl.ds(base, SUBLANE), :] = xi
        return xr[SUBLANE - 1:SUBLANE, :], xi[SUBLANE - 1:SUBLANE, :]

    zero = jnp.zeros((1, lr.shape[1]), F32)
    lax.fori_loop(0, n_blocks, step, (zero, zero))


def _s5_specs(seq):
    u_spec = pl.BlockSpec((seq, S5_U), lambda g: (0, g))
    bd_spec = pl.BlockSpec((1, S5_U, S5_L), lambda g: (g, 0, 0))
    cd_spec = pl.BlockSpec((1, S5_L, S5_U), lambda g: (g, 0, 0))
    lam_spec = pl.BlockSpec((1, 2, S5_L), lambda g: (g, 0, 0))
    d_spec = pl.BlockSpec((1, S5_U), lambda g: (0, g))
    return u_spec, bd_spec, cd_spec, lam_spec, d_spec


S5_ROWS = 256


def _row_chunks(seq, fn):
    rows_per = min(S5_ROWS, seq)

    def step(r, carry):
        fn(pl.ds(pl.multiple_of(r * rows_per, rows_per), rows_per))
        return carry

    lax.fori_loop(0, seq // rows_per, step, 0)


def _s5_fwd_call(u, bdr, bdi, cdr, cdi, lam, d):
    seq = u.shape[0]
    u_spec, bd_spec, cd_spec, lam_spec, d_spec = _s5_specs(seq)

    def body(u_ref, bdr_ref, bdi_ref, cdr_ref, cdi_ref, lam_ref, d_ref, y_ref, sr_ref, si_ref):
        def project_in(rows):
            uv = u_ref[rows, :]
            sr_ref[rows, :] = _bdot(uv, bdr_ref[0])
            si_ref[rows, :] = _bdot(uv, bdi_ref[0])

        def project_out(rows):
            y_ref[rows, :] = (_bdot(sr_ref[rows, :], cdr_ref[0]) - _bdot(si_ref[rows, :], cdi_ref[0])
                              + d_ref[...] * u_ref[rows, :])

        _row_chunks(seq, project_in)
        _s5_forward_scan(sr_ref, si_ref, lam_ref[0, 0:1, :], lam_ref[0, 1:2, :])
        _row_chunks(seq, project_out)

    return pl.pallas_call(
        body, grid=(S5_GB,), in_specs=[u_spec, bd_spec, bd_spec, cd_spec, cd_spec, lam_spec, d_spec],
        out_specs=u_spec, out_shape=jax.ShapeDtypeStruct(u.shape, F32),
        scratch_shapes=[pltpu.VMEM((seq, S5_L), F32), pltpu.VMEM((seq, S5_L), F32)],
        compiler_params=_cparams(("parallel",)), name="s5_fwd",
    )(u, bdr, bdi, cdr, cdi, lam, d)


def _s5_bwd_call(u, bdr, bdi, cdr, cdi, lam, d, dy):
    seq = u.shape[0]
    n_blocks = seq // SUBLANE
    u_spec, bd_spec, cd_spec, lam_spec, d_spec = _s5_specs(seq)

    def body(u_ref, bdr_ref, bdi_ref, cdr_ref, cdi_ref, lam_ref, d_ref, dy_ref,
             du_ref, dbdr_ref, dbdi_ref, dcdr_ref, dcdi_ref, dlam_ref, dd_ref, sr_ref, si_ref, gr_ref, gi_ref):
        lr, li = lam_ref[0, 0:1, :], lam_ref[0, 1:2, :]

        def project_in(rows):
            uv = u_ref[rows, :]
            dyv = dy_ref[rows, :]
            sr_ref[rows, :] = _bdot(uv, bdr_ref[0])
            si_ref[rows, :] = _bdot(uv, bdi_ref[0])
            gr_ref[rows, :] = _bdot(dyv, cdr_ref[0], _NT)
            gi_ref[rows, :] = -_bdot(dyv, cdi_ref[0], _NT)

        _row_chunks(seq, project_in)
        _s5_forward_scan(sr_ref, si_ref, lr, li)
        p, tab, row = _s5_tables(lr, -li, True)

        def step(i, carry):
            hr, hi, acc_r, acc_i = carry
            b = n_blocks - 1 - i
            base = pl.multiple_of(b * SUBLANE, SUBLANE)
            xr, xi = _s5_block_scan(gr_ref[pl.ds(base, SUBLANE), :], gi_ref[pl.ds(base, SUBLANE), :],
                                    p, tab, row, hr, hi, True)
            gr_ref[pl.ds(base, SUBLANE), :] = xr
            gi_ref[pl.ds(base, SUBLANE), :] = xi
            prev = pl.multiple_of(jnp.maximum(b - 1, 0) * SUBLANE, SUBLANE)
            keep = (b > 0).astype(F32)
            last_r = sr_ref[pl.ds(prev, SUBLANE), :][SUBLANE - 1:SUBLANE, :] * keep
            last_i = si_ref[pl.ds(prev, SUBLANE), :][SUBLANE - 1:SUBLANE, :] * keep
            pr = jnp.where(row >= 1, pltpu.roll(sr_ref[pl.ds(base, SUBLANE), :], 1, 0), last_r)
            pi = jnp.where(row >= 1, pltpu.roll(si_ref[pl.ds(base, SUBLANE), :], 1, 0), last_i)
            acc_r = acc_r + pr * xr + pi * xi
            acc_i = acc_i + pr * xi - pi * xr
            return xr[0:1, :], xi[0:1, :], acc_r, acc_i

        zero = jnp.zeros((1, S5_L), F32)
        zacc = jnp.zeros((SUBLANE, S5_L), F32)
        _, _, acc_r, acc_i = lax.fori_loop(0, n_blocks, step, (zero, zero, zacc, zacc))
        dlam_ref[0, 0:1, :] = jnp.sum(acc_r, axis=0, keepdims=True)
        dlam_ref[0, 1:2, :] = jnp.sum(acc_i, axis=0, keepdims=True)
        for ref in (dbdr_ref, dbdi_ref, dcdr_ref, dcdi_ref, dd_ref):
            ref[...] = jnp.zeros_like(ref)

        def grads(rows):
            uv, dyv = u_ref[rows, :], dy_ref[rows, :]
            grv, giv = gr_ref[rows, :], gi_ref[rows, :]
            du_ref[rows, :] = _bdot(grv, bdr_ref[0], _NT) + _bdot(giv, bdi_ref[0], _NT) + d_ref[...] * dyv
            dbdr_ref[0] += _bdot(uv, grv, _TN)
            dbdi_ref[0] += _bdot(uv, giv, _TN)
            dcdr_ref[0] += _bdot(sr_ref[rows, :], dyv, _TN)
            dcdi_ref[0] -= _bdot(si_ref[rows, :], dyv, _TN)
            dd_ref[...] += jnp.sum(dyv * uv, axis=0, keepdims=True)

        _row_chunks(seq, grads)

    scratch = [pltpu.VMEM((seq, S5_L), F32) for _ in range(4)]
    return pl.pallas_call(
        body, grid=(S5_GB,),
        in_specs=[u_spec, bd_spec, bd_spec, cd_spec, cd_spec, lam_spec, d_spec, u_spec],
        out_specs=[u_spec, bd_spec, bd_spec, cd_spec, cd_spec, lam_spec, d_spec],
        out_shape=[jax.ShapeDtypeStruct(a.shape, F32) for a in (u, bdr, bdi, cdr, cdi, lam, d)],
        scratch_shapes=scratch, compiler_params=_cparams(("parallel",)), name="s5_bwd",
    )(u, bdr, bdi, cdr, cdi, lam, d, dy)


@jax.custom_vjp
def s5_core(u, bdr, bdi, cdr, cdi, lam, d):
    return _s5_fwd_call(u, bdr, bdi, cdr, cdi, lam, d)


def _s5_core_fwd(*args):
    return _s5_fwd_call(*args), args


def _s5_core_bwd(res, dy):
    return tuple(_s5_bwd_call(*res, dy))


s5_core.defvjp(_s5_core_fwd, _s5_core_bwd)


def _s5_discretize(a_re, a_im, log_dt, b_re, b_im, c_re, c_im, d):
    dt = jnp.exp(log_dt)[:, None]
    mag = jnp.exp(a_re * dt)
    lbr, lbi = mag * jnp.cos(a_im * dt), mag * jnp.sin(a_im * dt)
    den = a_re * a_re + a_im * a_im
    fr = ((lbr - 1.0) * a_re + lbi * a_im) / den
    fi = (lbi * a_re - (lbr - 1.0) * a_im) / den
    bbr = fr[:, :, None] * b_re - fi[:, :, None] * b_im
    bbi = fr[:, :, None] * b_im + fi[:, :, None] * b_re
    eye = jnp.eye(S5_GROUPS // S5_GB, dtype=F32)
    gl = S5_GROUPS // S5_GB

    def bd(t):
        return jnp.einsum('bgpc,gh->bgchp', t.reshape(S5_GB, gl, S5_STATE, S5_GROUP), eye).reshape(S5_GB, S5_U, S5_L)

    def cd(t):
        return jnp.einsum('bgcp,gh->bgphc', t.reshape(S5_GB, gl, S5_GROUP, S5_STATE), eye).reshape(S5_GB, S5_L, S5_U)

    lam = jnp.stack([lbr.reshape(S5_GB, S5_L), lbi.reshape(S5_GB, S5_L)], axis=1)
    return bd(bbr), bd(bbi), cd(c_re), cd(c_im), lam, d.reshape(1, S5_WIDTH)


@jax.custom_vjp
def _unit_lower_solve(neg_a, rhs, tinv):
    return _hdot(tinv, rhs)


def _unit_lower_solve_fwd(neg_a, rhs, tinv):
    x = _hdot(tinv, rhs)
    return x, (x, tinv)


def _unit_lower_solve_bwd(res, dx):
    x, tinv = res
    g = _hdot(tinv, dx, _TN)
    return _hdot(g, x, _NT), g, jnp.zeros_like(tinv)


_unit_lower_solve.defvjp(_unit_lower_solve_fwd, _unit_lower_solve_bwd)


def _unit_lower_inverse(neg_a):
    r = lax.broadcasted_iota(jnp.int32, neg_a.shape, 0)
    c = lax.broadcasted_iota(jnp.int32, neg_a.shape, 1)
    p = (r == c).astype(F32) + neg_a
    npow = _hdot(neg_a, neg_a)
    for _ in range(4):
        y = _hdot(jnp.concatenate([p, npow], axis=0), npow)
        p = p + y[:CHUNK]
        npow = y[CHUNK:]
    return p + _hdot(p, npow)


def _gdn_chunk(q, k, v, g_col, b_col, st, tinv=None):
    r = lax.broadcasted_iota(jnp.int32, (CHUNK, CHUNK), 0)
    c = lax.broadcasted_iota(jnp.int32, (CHUNK, CHUNK), 1)
    eye = (r == c).astype(F32)
    strict = r > c
    causal = r >= c
    g_row = jnp.sum(g_col * eye, axis=0, keepdims=True)
    gcum = jnp.sum(jnp.where(causal, g_row, 0.0), axis=1, keepdims=True)
    gcum_row = jnp.sum(gcum * eye, axis=0, keepdims=True)
    diff = gcum - gcum_row
    decay_strict = jnp.where(strict, jnp.exp(jnp.where(strict, diff, 0.0)), 0.0)
    decay_causal = jnp.where(causal, jnp.exp(jnp.where(causal, diff, 0.0)), 0.0)
    gamma = jnp.exp(gcum)
    g_last = jnp.sum(jnp.where(lax.broadcasted_iota(jnp.int32, (CHUNK, 1), 0) == CHUNK - 1, gcum, 0.0),
                     axis=0, keepdims=True)
    kk = _bdot(k, k, _NT)
    neg_a = -(b_col * kk * decay_strict)
    if tinv is None:
        tinv = _unit_lower_inverse(neg_a)
    x = _unit_lower_solve(neg_a, jnp.concatenate([b_col * v, (b_col * gamma) * k], axis=1), lax.stop_gradient(tinv))
    u_new, w_k = x[:, :GDN_HEAD_DIM], x[:, GDN_HEAD_DIM:]
    qk = _bdot(q, k, _NT) * decay_causal
    q_g = q * gamma
    k_tail = k * jnp.exp(g_last - gcum)
    w = u_new - _bdot(w_k, st)
    o = _bdot(q_g, st) + _bdot(qk, w)
    st_new = jnp.exp(g_last) * st + _bdot(k_tail, w, _TN)
    return o, st_new, tinv


def _gdn_cols(bgv, h):
    lane = lax.broadcasted_iota(jnp.int32, bgv.shape, 1)
    g_col = jnp.sum(jnp.where(lane == h, bgv, 0.0), axis=1, keepdims=True)
    b_col = jnp.sum(jnp.where(lane == GDN_HEADS + h, bgv, 0.0), axis=1, keepdims=True)
    return g_col, b_col


def _gdn_fwd_call(q, k, v, bg):
    seq = q.shape[0]
    n_chunks = seq // CHUNK
    x_spec = pl.BlockSpec((CHUNK, GDN_WIDTH), lambda n: (n, 0))
    bg_spec = pl.BlockSpec((CHUNK, LANE), lambda n: (n, 0))
    st_spec = pl.BlockSpec((1, GDN_WIDTH, GDN_HEAD_DIM), lambda n: (n, 0, 0))
    ti_spec = pl.BlockSpec((1, GDN_HEADS * CHUNK, CHUNK), lambda n: (n, 0, 0))

    def body(q_ref, k_ref, v_ref, bg_ref, o_ref, st_out_ref, ti_out_ref, st_ref):
        @pl.when(pl.program_id(0) == 0)
        def _():
            st_ref[...] = jnp.zeros_like(st_ref)

        bgv = bg_ref[...]
        for h in range(GDN_HEADS):
            sl = slice(h * GDN_HEAD_DIM, (h + 1) * GDN_HEAD_DIM)
            g_col, b_col = _gdn_cols(bgv, h)
            st = st_ref[sl, :]
            st_out_ref[0, sl, :] = st
            o, st_new, tinv = _gdn_chunk(q_ref[:, sl], k_ref[:, sl], v_ref[:, sl], g_col, b_col, st)
            o_ref[:, sl] = o
            st_ref[sl, :] = st_new
            ti_out_ref[0, h * CHUNK:(h + 1) * CHUNK, :] = tinv

    return pl.pallas_call(
        body, grid=(n_chunks,), in_specs=[x_spec, x_spec, x_spec, bg_spec], out_specs=[x_spec, st_spec, ti_spec],
        out_shape=[jax.ShapeDtypeStruct(q.shape, F32),
                   jax.ShapeDtypeStruct((n_chunks, GDN_WIDTH, GDN_HEAD_DIM), F32),
                   jax.ShapeDtypeStruct((n_chunks, GDN_HEADS * CHUNK, CHUNK), F32)],
        scratch_shapes=[pltpu.VMEM((GDN_WIDTH, GDN_HEAD_DIM), F32)],
        compiler_params=_cparams(("arbitrary",)), name="gdn_fwd",
    )(q, k, v, bg)


def _gdn_bwd_call(q, k, v, bg, states, tinvs, do):
    seq = q.shape[0]
    n_chunks = seq // CHUNK
    x_spec = pl.BlockSpec((CHUNK, GDN_WIDTH), lambda i: (n_chunks - 1 - i, 0))
    bg_spec = pl.BlockSpec((CHUNK, LANE), lambda i: (n_chunks - 1 - i, 0))
    st_spec = pl.BlockSpec((1, GDN_WIDTH, GDN_HEAD_DIM), lambda i: (n_chunks - 1 - i, 0, 0))
    ti_spec = pl.BlockSpec((1, GDN_HEADS * CHUNK, CHUNK), lambda i: (n_chunks - 1 - i, 0, 0))

    def body(q_ref, k_ref, v_ref, bg_ref, st_in_ref, ti_ref, do_ref, dq_ref, dk_ref, dv_ref, dbg_ref, dst_ref):
        @pl.when(pl.program_id(0) == 0)
        def _():
            dst_ref[...] = jnp.zeros_like(dst_ref)

        bgv = bg_ref[...]
        lane = lax.broadcasted_iota(jnp.int32, bgv.shape, 1)
        dbg = jnp.zeros(bgv.shape, F32)
        for h in range(GDN_HEADS):
            sl = slice(h * GDN_HEAD_DIM, (h + 1) * GDN_HEAD_DIM)
            g_col, b_col = _gdn_cols(bgv, h)
            tinv = ti_ref[0, h * CHUNK:(h + 1) * CHUNK, :]
            _, pullback = jax.vjp(lambda *a: _gdn_chunk(*a, tinv=tinv)[:2], q_ref[:, sl], k_ref[:, sl], v_ref[:, sl],
                                  g_col, b_col, st_in_ref[0, sl, :])
            dq, dk, dv, dg, db, dst = pullback((do_ref[:, sl], dst_ref[sl, :]))
            dq_ref[:, sl] = dq
            dk_ref[:, sl] = dk
            dv_ref[:, sl] = dv
            dst_ref[sl, :] = dst
            dbg = dbg + jnp.where(lane == h, dg, 0.0) + jnp.where(lane == GDN_HEADS + h, db, 0.0)
        dbg_ref[...] = dbg

    return pl.pallas_call(
        body, grid=(n_chunks,), in_specs=[x_spec, x_spec, x_spec, bg_spec, st_spec, ti_spec, x_spec],
        out_specs=[x_spec, x_spec, x_spec, bg_spec],
        out_shape=[jax.ShapeDtypeStruct(q.shape, F32)] * 3 + [jax.ShapeDtypeStruct(bg.shape, F32)],
        scratch_shapes=[pltpu.VMEM((GDN_WIDTH, GDN_HEAD_DIM), F32)],
        compiler_params=_cparams(("arbitrary",)), name="gdn_bwd",
    )(q, k, v, bg, states, tinvs, do)


@jax.custom_vjp
def gdn_core(q, k, v, bg):
    return _gdn_fwd_call(q, k, v, bg)[0]


def _gdn_core_fwd(q, k, v, bg):
    o, states, tinvs = _gdn_fwd_call(q, k, v, bg)
    return o, (q, k, v, bg, states, tinvs)


def _gdn_core_bwd(res, do):
    return tuple(_gdn_bwd_call(*res, do))


gdn_core.defvjp(_gdn_core_fwd, _gdn_core_bwd)


def _row(v):
    return v.reshape(1, -1)


def _lane_pad(v):
    return jnp.pad(v, (0, LANE - v.shape[0])).reshape(1, LANE)


def _delay_rows(x, k):
    return jnp.pad(x, ((k, 0), (0, 0)))[:x.shape[0]]


def s5_mixer(u, a_re, a_im, log_dt, b_re, b_im, c_re, c_im, d, w_glu, b_glu):
    y = s5_core(u, *_s5_discretize(a_re, a_im, log_dt, b_re, b_im, c_re, c_im, d))
    return make_rowop(_s5_post_fn, "s5_post")((y,), (w_glu, _row(b_glu)))[0]


def gated_deltanet(qkv, gate, ab, conv_w, a_log, dt_bias, out_g):
    rows = (qkv, _delay_rows(qkv, 1), _delay_rows(qkv, 2), _delay_rows(qkv, 3), ab)
    q, k, v, bg = make_rowop(_gdn_prep_fn, "gdn_prep")(rows, (conv_w, _lane_pad(a_log), _lane_pad(dt_bias)))
    o = gdn_core(q, k, v, bg)
    return make_rowop(_gdn_out_fn, "gdn_out")((o, gate), (_row(out_g),))[0]


def chunk_attention(x, h, w_qkv, w_out, q_g, k_g, rel_bias):
    qkv = linear_cols(h, w_qkv)
    qn, kn = make_rowop(_ca_qknorm_fn, "ca_qknorm")((qkv,), (_row(jnp.tile(q_g, 2)), _row(jnp.tile(k_g, 2))))
    kpad = jnp.pad(kn, ((CA_PAD, 0), (0, 0)))
    vpad = jnp.pad(qkv[:, 2 * D_MODEL:], ((CA_PAD, 0), (0, 0)))
    o = chunk_attn_core(qn, kpad, vpad, _rel_bias_vector(rel_bias))
    return linear_res(o, w_out, x)


def memory_cross_attention(x, h, mem_n, w_q, w_kv, w_out, q_g, k_g):
    q = linear(h, w_q)
    kv = linear_cols(mem_n, w_kv)
    o = make_rowop(_xattn_fn, "xattn")((q,), (kv[:, :D_MODEL], kv[:, D_MODEL:], _row(q_g), _row(k_g)))[0]
    return linear_res(o, w_out, x)


def swiglu(x, h, w_gate, w_up, w_down):
    return ffn_down(linear_cols_dm(h, w_gate), linear_cols_dm(h, w_up), w_down, x)


def _rel_bias_vector(rel_bias):
    heads = rel_bias.shape[0]
    n_far = CA_KB - 1 - MAX_REL
    n_neg = CA_VEC - 1 - n_far - (2 * MAX_REL + 1)
    vec = jnp.concatenate([jnp.zeros((heads, 1), F32),
                           jnp.broadcast_to(rel_bias[:, 2 * MAX_REL:], (heads, n_far)),
                           jnp.flip(rel_bias, axis=1),
                           jnp.broadcast_to(rel_bias[:, :1], (heads, n_neg))], axis=1)
    return vec.reshape(heads // 2, 2, CA_VEC)


def _exchange(arrays, modes, name):
    n = len(arrays)
    out_shapes = [jax.ShapeDtypeStruct((N_DEV,) + a.shape if m == "gather" else a.shape, a.dtype)
                  for a, m in zip(arrays, modes)]

    def body(*refs):
        ins, outs = refs[:n], refs[n:2 * n]
        send_sems, recv_sems, local_sems = refs[2 * n:]
        x, y, c = lax.axis_index("x"), lax.axis_index("y"), lax.axis_index("c")
        me = 4 * x + 2 * y + c
        pending = []
        for i in range(n):
            gather = modes[i] == "gather"
            local = pltpu.make_async_copy(ins[i] if gather else ins[i].at[me], outs[i].at[me], local_sems.at[i])
            local.start()
            pending.append(local)
        for k in range(1, N_DEV):
            px, py, pc = (x + (k >> 2)) % 2, (y + ((k >> 1) & 1)) % 2, (c + (k & 1)) % 2
            peer = 4 * px + 2 * py + pc
            for i in range(n):
                src = ins[i] if modes[i] == "gather" else ins[i].at[peer]
                sem = i * (N_DEV - 1) + k - 1
                send = pltpu.make_async_remote_copy(src_ref=src, dst_ref=outs[i].at[me], send_sem=send_sems.at[sem],
                                                    recv_sem=recv_sems.at[sem], device_id=(px, py, pc),
                                                    device_id_type=pl.DeviceIdType.MESH)
                send.start()
                arrival = pltpu.make_async_remote_copy(src_ref=src, dst_ref=outs[i].at[peer],
                                                       send_sem=send_sems.at[sem], recv_sem=recv_sems.at[sem],
                                                       device_id=(px, py, pc), device_id_type=pl.DeviceIdType.MESH)
                pending.append((send, arrival))
        for item in pending:
            if isinstance(item, tuple):
                item[0].wait_send()
                item[1].wait_recv()
            else:
                item.wait()

    any_spec = pl.BlockSpec(memory_space=pl.ANY)
    return pl.pallas_call(
        body, in_specs=[any_spec] * n, out_specs=[any_spec] * n, out_shape=out_shapes,
        scratch_shapes=[pltpu.SemaphoreType.DMA((n * (N_DEV - 1),)), pltpu.SemaphoreType.DMA((n * (N_DEV - 1),)),
                        pltpu.SemaphoreType.DMA((n,))],
        name=name,
    )(*arrays)


_HBM_SPEC = pl.BlockSpec(memory_space=pltpu.HBM)
_SEM_SPEC = pl.BlockSpec(memory_space=pltpu.SEMAPHORE)
_SIDE_EFFECT = pltpu.SideEffectType.DATAFLOW_SIDE_EFFECTING


def _peer(x, y, c, k):
    return (x + (k >> 2)) % 2, (y + ((k >> 1) & 1)) % 2, (c + (k & 1)) % 2


def _exchange_start(arrays, modes, after, name, carry):
    n = len(arrays)
    n_sem = n * (N_DEV - 1)
    lands = [pltpu.with_memory_space_constraint(lax.empty((N_DEV,) + a.shape if m == "gather" else a.shape, a.dtype),
                                                pltpu.HBM) for a, m in zip(arrays, modes)]
    arrays = [pltpu.with_memory_space_constraint(a, pltpu.HBM) for a in arrays]

    def body(*refs):
        ins, zones = refs[:n], refs[n:2 * n]
        send_sems, recv_sems, own_sems = refs[2 * n + 2:2 * n + 5]
        x, y, c = lax.axis_index("x"), lax.axis_index("y"), lax.axis_index("c")
        me = 4 * x + 2 * y + c
        for i in range(n):
            pltpu.make_async_copy(ins[i] if modes[i] == "gather" else ins[i].at[me], zones[i].at[me],
                                  own_sems.at[i]).start()
        for k in range(1, N_DEV):
            px, py, pc = _peer(x, y, c, k)
            peer = 4 * px + 2 * py + pc
            for i in range(n):
                sem = i * (N_DEV - 1) + k - 1
                pltpu.make_async_remote_copy(src_ref=ins[i] if modes[i] == "gather" else ins[i].at[peer],
                                             dst_ref=zones[i].at[me], send_sem=send_sems.at[sem],
                                             recv_sem=recv_sems.at[sem], device_id=(px, py, pc),
                                             device_id_type=pl.DeviceIdType.MESH).start()

    carry = pltpu.with_memory_space_constraint(carry, pltpu.HBM)
    out_shape = ((pltpu.SemaphoreType.DMA((n_sem,)), pltpu.SemaphoreType.DMA((n_sem,)), pltpu.SemaphoreType.DMA((n,)))
                 + tuple(pltpu.HBM(a.shape, a.dtype) for a in arrays) + tuple(pltpu.HBM(z.shape, z.dtype) for z in lands)
                 + (pltpu.HBM(carry.shape, carry.dtype),))
    aliases = {i: 3 + i for i in range(2 * n)}
    aliases[2 * n + 1] = 3 + 2 * n
    res = pl.pallas_call(
        body, name=name, out_shape=out_shape,
        in_specs=[_HBM_SPEC] * (2 * n) + [pl.BlockSpec(memory_space=pl.ANY), _HBM_SPEC],
        out_specs=(_SEM_SPEC,) * 3 + (_HBM_SPEC,) * (2 * n + 1),
        input_output_aliases=aliases,
        compiler_params=pltpu.CompilerParams(has_side_effects=_SIDE_EFFECT),
    )(*arrays, *lands, after, carry)
    return tuple(res[:3]), list(res[3:3 + n]), list(res[3 + n:3 + 2 * n]), res[3 + 2 * n]


def _exchange_wait(started, modes, after, name):
    sems, sources, zones, _ = started
    n = len(sources)

    def body(*refs):
        ins, lands = refs[:n], refs[n:2 * n]
        send_ref, recv_ref, own_ref = refs[2 * n:2 * n + 3]
        x, y, c = lax.axis_index("x"), lax.axis_index("y"), lax.axis_index("c")
        me = 4 * x + 2 * y + c
        for i in range(n):
            pltpu.make_async_copy(ins[i] if modes[i] == "gather" else ins[i].at[me], lands[i].at[me],
                                  own_ref.at[i]).wait()
        for k in range(1, N_DEV):
            px, py, pc = _peer(x, y, c, k)
            peer = 4 * px + 2 * py + pc
            for i in range(n):
                sem = i * (N_DEV - 1) + k - 1
                cp = pltpu.make_async_remote_copy(src_ref=ins[i] if modes[i] == "gather" else ins[i].at[peer],
                                                  dst_ref=lands[i].at[peer], send_sem=send_ref.at[sem],
                                                  recv_sem=recv_ref.at[sem], device_id=(px, py, pc),
                                                  device_id_type=pl.DeviceIdType.MESH)
                cp.wait_send()
                cp.wait_recv()

    res = pl.pallas_call(
        body, name=name,
        out_shape=tuple(pltpu.HBM(a.shape, a.dtype) for a in sources) + tuple(pltpu.HBM(z.shape, z.dtype) for z in zones),
        in_specs=[_HBM_SPEC] * (2 * n) + [_SEM_SPEC] * 3 + [pl.BlockSpec(memory_space=pl.ANY)],
        out_specs=(_HBM_SPEC,) * (2 * n), input_output_aliases={i: i for i in range(2 * n)},
        compiler_params=pltpu.CompilerParams(has_side_effects=_SIDE_EFFECT),
    )(*sources, *zones, *sems, after)
    return list(res[n:])


ADAM_TILE = 64 * 1024


def _adam_call(w, m, v, slots, name):
    n_layers, rows, cols = w.shape
    tr = rows
    if n_layers * rows * cols > ADAM_TILE:
        fits = [t for t in range(SUBLANE, rows, SUBLANE) if rows % t == 0 and n_layers * t * cols <= ADAM_TILE]
        tr = max(fits) if fits else SUBLANE
    c1 = 1.0 - ADAM_B1 ** ADAM_STEP
    c2 = 1.0 - ADAM_B2 ** ADAM_STEP

    def body(*refs):
        w_ref, m_ref, v_ref = refs[:3]
        slot_refs = refs[3:3 + n_layers]
        grad_ref, delta_ref, nm_ref, nv_ref = refs[3 + n_layers:]
        for layer in range(n_layers):
            g = slot_refs[layer][0].astype(F32)
            for k in range(1, N_DEV):
                g = g + slot_refs[layer][k].astype(F32)
            m_new = ADAM_B1 * m_ref[layer] + (1.0 - ADAM_B1) * g
            v_new = ADAM_B2 * v_ref[layer] + (1.0 - ADAM_B2) * (g * g)
            m_hat = m_new / c1
            v_hat = v_new / c2
            grad_ref[layer] = g
            delta_ref[layer] = -ADAM_LR * (m_hat / (jnp.sqrt(v_hat) + ADAM_EPS) + ADAM_WD * w_ref[layer])
            nm_ref[layer] = m_new
            nv_ref[layer] = v_new

    spec = pl.BlockSpec((n_layers, tr, cols), lambda i: (0, i, 0))
    slot_spec = pl.BlockSpec((N_DEV, tr, cols), lambda i: (0, i, 0))
    return pl.pallas_call(
        body, grid=(rows // tr,), in_specs=[spec, spec, spec] + [slot_spec] * n_layers,
        out_specs=[spec] * 4, out_shape=[jax.ShapeDtypeStruct(w.shape, F32)] * 4,
        compiler_params=_cparams(("parallel",)), name=name,
    )(w, m, v, *slots)


WEIGHT_NAMES = ['ab_norm_g', 'ab_w_in', 'ab_w_out', 's5_a_re', 's5_a_im', 's5_log_dt', 's5_b_re', 's5_b_im', 's5_c_re',
                's5_c_im', 's5_d', 's5_w_glu', 's5_b_glu', 'gdn_conv_w', 'gdn_a_log', 'gdn_dt_bias', 'gdn_out_norm_g',
                'c_norm_g', 'c_w_qkv', 'c_w_out', 'c_q_norm_g', 'c_k_norm_g', 'c_rel_bias', 'mem_norm_g', 'xa_norm_g',
                'xa_w_q', 'xa_w_kv', 'xa_w_out', 'xa_q_norm_g', 'xa_k_norm_g', 'f_norm_g', 'f_w_gate', 'f_w_up',
                'f_w_down']

SHARDED = {
    'ab_w_in': ('col', BF16), 'ab_w_out': ('row', BF16), 's5_w_glu': ('row', BF16), 'gdn_conv_w': ('col', F32),
    'c_norm_g': ('col', F32), 'c_w_qkv': ('col', BF16), 'c_w_out': ('row', BF16), 'xa_w_q': ('row', BF16),
    'xa_w_kv': ('col', BF16), 'xa_w_out': ('row', BF16), 'f_w_gate': ('col', BF16), 'f_w_up': ('col', BF16),
    'f_w_down': ('row', BF16),
}
GATHERED_AS_IS = ('c_w_qkv', 'xa_w_kv', 'f_w_gate', 'f_w_up', 'f_w_down')
REPLICATED = [n for n in WEIGHT_NAMES if n not in SHARDED]
PACK_UNIT = SUBLANE * LANE


def _full_from_gathered(g, axis):
    if axis == "row":
        return g.reshape(g.shape[0] * g.shape[1], g.shape[2])
    return jnp.transpose(g, (1, 0, 2)).reshape(g.shape[1], g.shape[0] * g.shape[2])


def _pack(arrays):
    flat = []
    for a in arrays:
        size = a.size
        padded = -(-size // PACK_UNIT) * PACK_UNIT
        flat.append(jnp.pad(a.reshape(-1), (0, padded - size)).reshape(-1, LANE))
    return jnp.concatenate(flat, axis=0)


def _unpack(buf, shapes):
    out, row = [], 0
    for shape in shapes:
        size = math.prod(shape)
        rows = -(-size // PACK_UNIT) * SUBLANE
        out.append(buf[row:row + rows].reshape(-1)[:size].reshape(shape))
        row += rows
    return out


N_STAGES = 2 * DEPTH
EVEN_SHARDED = ['ab_w_in', 'ab_w_out', 's5_w_glu', 'gdn_conv_w']
ODD_SHARDED = ['c_norm_g', 'c_w_qkv', 'c_w_out']
ALL_SHARDED = ['xa_w_q', 'xa_w_kv', 'xa_w_out', 'f_w_gate', 'f_w_up', 'f_w_down']
EVEN_SMALL = ['ab_norm_g', 's5_a_re', 's5_a_im', 's5_log_dt', 's5_b_re', 's5_b_im', 's5_c_re', 's5_c_im', 's5_d',
              's5_b_glu', 'gdn_a_log', 'gdn_dt_bias', 'gdn_out_norm_g']
ODD_SMALL = ['c_q_norm_g', 'c_k_norm_g', 'c_rel_bias']
ALL_SMALL = ['xa_norm_g', 'xa_q_norm_g', 'xa_k_norm_g', 'f_norm_g']


def _stage_params(stage):
    layer, part = divmod(stage, 2)
    if part == 1:
        return [(n, layer) for n in ALL_SHARDED], [(n, layer) for n in ALL_SMALL]
    big, small = (EVEN_SHARDED, EVEN_SMALL) if layer % 2 == 0 else (ODD_SHARDED, ODD_SMALL)
    return [(n, layer // 2) for n in big], [(n, layer // 2) for n in small]


def _stage_forward(stage, landed, small, x, mem_n):
    layer, part = divmod(stage, 2)
    big = {}
    for (n, _), g in zip(_stage_params(stage)[0], landed):
        if n == 'ab_w_in':
            big[n] = cols_to_natural(g, AB_IN_PAD)
        elif n in GATHERED_AS_IS:
            big[n] = g
        elif n == 's5_w_glu':
            big[n] = _full_from_gathered(g, 'row').astype(F32)
        else:
            big[n] = _full_from_gathered(g, SHARDED[n][0])
    if part == 1:
        h, x = rmsnorm_res(x, small['xa_norm_g'], "xa_norm")
        x = memory_cross_attention(x, h, mem_n, big['xa_w_q'], big['xa_w_kv'], big['xa_w_out'],
                                   small['xa_q_norm_g'], small['xa_k_norm_g'])
        h, x = rmsnorm_res(x, small['f_norm_g'], "f_norm")
        return swiglu(x, h, big['f_w_gate'], big['f_w_up'], big['f_w_down'])
    if layer % 2 == 0:
        h, x = rmsnorm_res(x, small['ab_norm_g'], "ab_norm")
        w_in = big['ab_w_in']
        u = linear(h, w_in[:, :S5_WIDTH])
        qkv = linear(h, w_in[:, S5_WIDTH:S5_WIDTH + 3 * GDN_WIDTH])
        gate = linear(h, w_in[:, S5_WIDTH + 3 * GDN_WIDTH:S5_WIDTH + 4 * GDN_WIDTH])
        ab = linear(h, w_in[:, S5_WIDTH + 4 * GDN_WIDTH:])
        a_out = s5_mixer(u, small['s5_a_re'], small['s5_a_im'], small['s5_log_dt'], small['s5_b_re'], small['s5_b_im'],
                         small['s5_c_re'], small['s5_c_im'], small['s5_d'], big['s5_w_glu'], small['s5_b_glu'])
        b_out = gated_deltanet(qkv, gate, ab, big['gdn_conv_w'], small['gdn_a_log'], small['gdn_dt_bias'],
                               small['gdn_out_norm_g'])
        return linear_res(jnp.concatenate([a_out, b_out], axis=1), big['ab_w_out'], x)
    h, x = rmsnorm_res(x, big['c_norm_g'].reshape(-1), "c_norm")
    return chunk_attention(x, h, big['c_w_qkv'], big['c_w_out'], small['c_q_norm_g'], small['c_k_norm_g'],
                           small['c_rel_bias'])


def _loss_rows(x, target):
    return jnp.sum(make_rowop(_loss_fn, "loss")((x, target), ())[0])


def kernel(x, mem, ab_norm_g, ab_w_in, ab_w_out, s5_a_re, s5_a_im, s5_log_dt, s5_b_re, s5_b_im, s5_c_re, s5_c_im, s5_d, s5_w_glu, s5_b_glu, gdn_conv_w, gdn_a_log, gdn_dt_bias, gdn_out_norm_g, c_norm_g, c_w_qkv, c_w_out, c_q_norm_g, c_k_norm_g, c_rel_bias, mem_norm_g, xa_norm_g, xa_w_q, xa_w_kv, xa_w_out, xa_q_norm_g, xa_k_norm_g, f_norm_g, f_w_gate, f_w_up, f_w_down, loss_target, m_ab_norm_g, m_ab_w_in, m_ab_w_out, m_s5_a_re, m_s5_a_im, m_s5_log_dt, m_s5_b_re, m_s5_b_im, m_s5_c_re, m_s5_c_im, m_s5_d, m_s5_w_glu, m_s5_b_glu, m_gdn_conv_w, m_gdn_a_log, m_gdn_dt_bias, m_gdn_out_norm_g, m_c_norm_g, m_c_w_qkv, m_c_w_out, m_c_q_norm_g, m_c_k_norm_g, m_c_rel_bias, m_mem_norm_g, m_xa_norm_g, m_xa_w_q, m_xa_w_kv, m_xa_w_out, m_xa_q_norm_g, m_xa_k_norm_g, m_f_norm_g, m_f_w_gate, m_f_w_up, m_f_w_down, v_ab_norm_g, v_ab_w_in, v_ab_w_out, v_s5_a_re, v_s5_a_im, v_s5_log_dt, v_s5_b_re, v_s5_b_im, v_s5_c_re, v_s5_c_im, v_s5_d, v_s5_w_glu, v_s5_b_glu, v_gdn_conv_w, v_gdn_a_log, v_gdn_dt_bias, v_gdn_out_norm_g, v_c_norm_g, v_c_w_qkv, v_c_w_out, v_c_q_norm_g, v_c_k_norm_g, v_c_rel_bias, v_mem_norm_g, v_xa_norm_g, v_xa_w_q, v_xa_w_kv, v_xa_w_out, v_xa_q_norm_g, v_xa_k_norm_g, v_f_norm_g, v_f_w_gate, v_f_w_up, v_f_w_down):
    given = dict(locals())
    no_after = jnp.zeros((SUBLANE, LANE), F32)

    def shard(n, idx):
        a = given[n][idx]
        return (a.reshape(1, -1) if a.ndim == 1 else a).astype(SHARDED[n][1])

    def gather_start(stage, after, carry):
        arrays = [shard(n, idx) for n, idx in _stage_params(stage)[0]]
        return _exchange_start(arrays, ["gather"] * len(arrays), after, "gather_start_%d" % stage, carry)

    act = x[0]
    mem_n, mem_pullback = jax.vjp(lambda m, g: rmsnorm(m, g, "mem_norm"), mem[0], mem_norm_g)
    in_flight = {}
    for stage in range(2):
        in_flight[stage] = gather_start(stage, no_after, act)
        act = in_flight[stage][3]
    pullbacks = []
    for stage in range(N_STAGES):
        started = in_flight.pop(stage)
        landed = _exchange_wait(started, ["gather"] * len(started[1]), act, "gather_wait_%d" % stage)
        if stage + 2 < N_STAGES:
            in_flight[stage + 2] = gather_start(stage + 2, landed[0], act)
            act = in_flight[stage + 2][3]
        small = {n: given[n][idx] for n, idx in _stage_params(stage)[1]}
        act, pullback = jax.vjp(functools.partial(_stage_forward, stage), landed, small, act, mem_n)
        pullbacks.append(pullback)
    loss_local, loss_pullback = jax.vjp(_loss_rows, act, loss_target[0])
    d_act = loss_pullback(jnp.ones((), F32))[0]

    d_mem_n = jnp.zeros_like(mem_n)
    g_small = {}
    received = [None] * N_STAGES
    started, after = None, no_after
    for stage in reversed(range(N_STAGES)):
        d_landed, d_small, d_act, d_mem = pullbacks[stage](d_act)
        if stage % 2 == 1:
            d_mem_n = d_mem_n + d_mem
        for n, idx in _stage_params(stage)[1]:
            g_small[(n, idx)] = d_small[n]
        if started is not None:
            received[stage + 1] = _exchange_wait(started, ["scatter"] * len(started[1]), d_act,
                                                 "scatter_wait_%d" % (stage + 1))
            after = received[stage + 1][0]
        started = _exchange_start(list(d_landed), ["scatter"] * len(d_landed), after, "scatter_start_%d" % stage,
                                  d_act)
        d_act = started[3]
    g_small[('mem_norm_g', None)] = mem_pullback(d_mem_n)[1]

    def small_grad(n):
        if n == 'mem_norm_g':
            return g_small[(n, None)]
        return jnp.stack([g_small[(n, i)] for i in range(given[n].shape[0])], axis=0)

    packed = _exchange([_pack([small_grad(n) for n in REPLICATED])], ["gather"], "small_grads_allgather")[0]
    received[0] = _exchange_wait(started, ["scatter"] * len(started[1]), packed, "scatter_wait_0")

    results = {}
    for n in SHARDED:
        slots = {}
        for stage in range(N_STAGES):
            for (pn, idx), r in zip(_stage_params(stage)[0], received[stage]):
                if pn == n:
                    slots[idx] = r
        shape = given[n].shape
        to3d = lambda a: a.reshape(a.shape[0], -1, a.shape[-1])
        outs = _adam_call(to3d(given[n]), to3d(given['m_' + n]), to3d(given['v_' + n]),
                          [slots[i] for i in range(len(slots))], "adamw_" + n)
        results[n] = [o.reshape(shape) for o in outs]
    outs = _adam_call(_pack([given[n] for n in REPLICATED])[None], _pack([given['m_' + n] for n in REPLICATED])[None],
                      _pack([given['v_' + n] for n in REPLICATED])[None], [packed], "adamw_replicated")
    shapes = [given[n].shape for n in REPLICATED]
    for j, parts in enumerate(zip(*[_unpack(o[0], shapes) for o in outs])):
        results[REPLICATED[j]] = list(parts)

    loss = lax.psum(loss_local, ("x", "y", "c"))
    return (loss, d_act[None], *[results[n][0] for n in WEIGHT_NAMES], *[results[n][1] for n in WEIGHT_NAMES],
            *[results[n][2] for n in WEIGHT_NAMES], *[results[n][3] for n in WEIGHT_NAMES])
```

```python
import functools
import math

import jax
import jax.numpy as jnp
import numpy as np
from jax import lax
from jax.experimental import pallas as pl
from jax.experimental.pallas import tpu as pltpu

F32 = jnp.float32
BF16 = jnp.bfloat16
HI = lax.Precision.HIGHEST

N_DEV = 8
D_MODEL = 1024
SEQ = 2048
DEPTH = 4
CHUNK = 64
N_MEM = 256
RMS_EPS = 1e-6
S5_WIDTH = 512
S5_GROUP = 16
S5_GROUPS = 32
S5_STATE = 64
GDN_HEAD_DIM = 128
GDN_WIDTH = 512
GDN_HEADS = 4
GDN_CONV = 4
AB_IN = S5_WIDTH + 4 * GDN_WIDTH + 2 * GDN_HEADS
AB_IN_PAD = 2688
CA_HEADS = 16
CA_HEAD_DIM = 64
CA_LEFT = 8
CA_BAND = (CA_LEFT + 1) * CHUNK
CA_PAD = CA_LEFT * CHUNK
MAX_REL = 128
XA_HEADS = 4
XA_HEAD_DIM = 256
FFN = 2816
ADAM_LR, ADAM_B1, ADAM_B2, ADAM_EPS, ADAM_WD, ADAM_STEP = 0.001, 0.9, 0.999, 1e-08, 0.01, 10

VMEM_LIMIT = 48 * 1024 * 1024
LANE = 128
SUBLANE = 8


def _cparams(sem=None):
    return pltpu.CompilerParams(dimension_semantics=sem, vmem_limit_bytes=VMEM_LIMIT)


def _divisor_tile(n, target, unit=LANE):
    if n <= target:
        return n
    best = None
    for t in range(unit, target + 1, unit):
        if n % t == 0:
            best = t
    assert best is not None, (n, target)
    return best


def _matmul(a, b, *, ta=False, tb=False, out_dtype=F32, name="mm", res=None):
    if ta:
        k_dim, m_dim = a.shape
    else:
        m_dim, k_dim = a.shape
    if tb:
        n_dim, kb = b.shape
    else:
        kb, n_dim = b.shape
    assert kb == k_dim, (a.shape, b.shape, ta, tb)
    tm = _divisor_tile(m_dim, 1024)
    tn = _divisor_tile(n_dim, 512)
    tk = _divisor_tile(k_dim, 1408)
    nk = k_dim // tk
    dims = (((0 if ta else 1,), (1 if tb else 0,)), ((), ()))

    def body(a_ref, b_ref, *rest):
        res_ref = rest[0] if res is not None else None
        o_ref, acc_ref = rest[-2:]
        k = pl.program_id(2)

        @pl.when(k == 0)
        def _():
            acc_ref[...] = jnp.zeros_like(acc_ref)

        acc_ref[...] += lax.dot_general(a_ref[...].astype(BF16), b_ref[...].astype(BF16), dims,
                                        preferred_element_type=F32)

        @pl.when(k == nk - 1)
        def _():
            total = acc_ref[...] if res is None else acc_ref[...] + res_ref[...]
            o_ref[...] = total.astype(o_ref.dtype)

    a_spec = pl.BlockSpec((tk, tm), lambda i, j, k: (k, i)) if ta else pl.BlockSpec((tm, tk), lambda i, j, k: (i, k))
    b_spec = pl.BlockSpec((tn, tk), lambda i, j, k: (j, k)) if tb else pl.BlockSpec((tk, tn), lambda i, j, k: (k, j))
    o_spec = pl.BlockSpec((tm, tn), lambda i, j, k: (i, j))
    return pl.pallas_call(
        body,
        grid=(m_dim // tm, n_dim // tn, nk),
        in_specs=[a_spec, b_spec] + ([o_spec] if res is not None else []),
        out_specs=o_spec,
        out_shape=jax.ShapeDtypeStruct((m_dim, n_dim), out_dtype),
        scratch_shapes=[pltpu.VMEM((tm, tn), F32)],
        compiler_params=_cparams(("parallel", "parallel", "arbitrary")),
        name=name,
    )(*((a, b) if res is None else (a, b, res)))


@jax.custom_vjp
def linear(a, w):
    return _matmul(a, w, name="linear_fwd")


def _linear_fwd(a, w):
    return _matmul(a, w, name="linear_fwd"), (a, w)


def _linear_bwd(res, dy):
    a, w = res
    da = _matmul(dy, w, tb=True, name="linear_da")
    dw = _matmul(a, dy, ta=True, out_dtype=w.dtype, name="linear_dw")
    return da, dw


linear.defvjp(_linear_fwd, _linear_bwd)


@jax.custom_vjp
def linear_res(a, w, x):
    return _matmul(a, w, name="linear_res_fwd", res=x)


def _linear_res_fwd(a, w, x):
    return _matmul(a, w, name="linear_res_fwd", res=x), (a, w)


def _linear_res_bwd(res, dy):
    return _linear_bwd(res, dy) + (dy,)


linear_res.defvjp(_linear_res_fwd, _linear_res_bwd)


def _mm_call(name, a, b, out_struct, grid, a_spec, b_spec, o_spec, dims, lead, res=None, keep_a=None):
    nk = grid[-1]
    acc_shape = o_spec.block_shape[1:] if lead[2] else o_spec.block_shape
    a_shape = a_spec.block_shape[1:] if lead[0] else a_spec.block_shape
    assert keep_a is None or nk == 1

    def body(a_ref, b_ref, *rest):
        res_ref = rest[0] if res is not None else None
        o_ref, acc_ref = rest[-1 - (keep_a is not None) - 1], rest[-1 - (keep_a is not None)]
        k = pl.program_id(len(grid) - 1)

        @pl.when(k == 0)
        def _():
            acc_ref[...] = jnp.zeros_like(acc_ref)

        if keep_a is None:
            av = (a_ref[0] if lead[0] else a_ref[...]).astype(BF16)
        else:
            a16_ref = rest[-1]

            @pl.when(pl.program_id(keep_a) == 0)
            def _():
                a16_ref[...] = (a_ref[0] if lead[0] else a_ref[...]).astype(BF16)

            av = a16_ref[...]
        bv = b_ref[0] if lead[1] else b_ref[...]
        acc_ref[...] += lax.dot_general(av, bv.astype(BF16), dims, preferred_element_type=F32)

        @pl.when(k == nk - 1)
        def _():
            if lead[2]:
                o_ref[0] = acc_ref[...].astype(o_ref.dtype)
            elif res is not None:
                o_ref[...] = (acc_ref[...] + res_ref[...]).astype(o_ref.dtype)
            else:
                o_ref[...] = acc_ref[...].astype(o_ref.dtype)

    return pl.pallas_call(
        body, grid=grid, in_specs=[a_spec, b_spec] + ([o_spec] if res is not None else []), out_specs=o_spec,
        out_shape=out_struct,
        scratch_shapes=[pltpu.VMEM(tuple(acc_shape), F32)] + ([pltpu.VMEM(tuple(a_shape), BF16)] if keep_a is not None else []),
        compiler_params=_cparams(("parallel", "arbitrary", "arbitrary")), name=name,
    )(*((a, b) if res is None else (a, b, res)))


_NN = (((1,), (0,)), ((), ()))
_NT_DIMS = (((1,), (1,)), ((), ()))
_TN_DIMS = (((0,), (0,)), ((), ()))


def _cols_fwd(a, g, dm_out):
    m_dim, k_dim = a.shape
    _, _, c_dim = g.shape
    tm = _divisor_tile(m_dim, 1024)
    tk = _divisor_tile(k_dim, 1024)
    a_spec = pl.BlockSpec((tm, tk), lambda i, j, k: (i, k))
    b_spec = pl.BlockSpec((1, tk, c_dim), lambda i, j, k: (j, k, 0))
    if dm_out:
        o_spec = pl.BlockSpec((1, tm, c_dim), lambda i, j, k: (j, i, 0))
        out = jax.ShapeDtypeStruct((N_DEV, m_dim, c_dim), F32)
    else:
        o_spec = pl.BlockSpec((tm, c_dim), lambda i, j, k: (i, j))
        out = jax.ShapeDtypeStruct((m_dim, N_DEV * c_dim), F32)
    return _mm_call("cols_fwd", a, g, out, (m_dim // tm, N_DEV, k_dim // tk), a_spec, b_spec, o_spec, _NN,
                    (False, True, dm_out), keep_a=1 if k_dim == tk else None)


def _cols_da(dy, g, dm_out):
    _, k_dim, c_dim = g.shape
    m_dim = dy.shape[1] if dm_out else dy.shape[0]
    tm = _divisor_tile(m_dim, 1024)
    tk = _divisor_tile(k_dim, 1024)
    if dm_out:
        a_spec = pl.BlockSpec((1, tm, c_dim), lambda i, kb, j: (j, i, 0))
    else:
        a_spec = pl.BlockSpec((tm, c_dim), lambda i, kb, j: (i, j))
    b_spec = pl.BlockSpec((1, tk, c_dim), lambda i, kb, j: (j, kb, 0))
    o_spec = pl.BlockSpec((tm, tk), lambda i, kb, j: (i, kb))
    return _mm_call("cols_da", dy, g, jax.ShapeDtypeStruct((m_dim, k_dim), F32), (m_dim // tm, k_dim // tk, N_DEV),
                    a_spec, b_spec, o_spec, _NT_DIMS, (dm_out, True, False))


def _cols_dg(a, dy, g, dm_out):
    _, k_dim, c_dim = g.shape
    m_dim = a.shape[0]
    tm = _divisor_tile(m_dim, 2048)
    tk = _divisor_tile(k_dim, 512)
    a_spec = pl.BlockSpec((tm, tk), lambda kb, j, m: (m, kb))
    if dm_out:
        b_spec = pl.BlockSpec((1, tm, c_dim), lambda kb, j, m: (j, m, 0))
    else:
        b_spec = pl.BlockSpec((tm, c_dim), lambda kb, j, m: (m, j))
    o_spec = pl.BlockSpec((1, tk, c_dim), lambda kb, j, m: (j, kb, 0))
    return _mm_call("cols_dg", a, dy, jax.ShapeDtypeStruct(g.shape, g.dtype), (k_dim // tk, N_DEV, m_dim // tm),
                    a_spec, b_spec, o_spec, _TN_DIMS, (False, dm_out, True), keep_a=1 if m_dim == tm else None)


def _make_linear_cols(dm_out):
    @jax.custom_vjp
    def op(a, g):
        return _cols_fwd(a, g, dm_out)

    def fwd(a, g):
        return _cols_fwd(a, g, dm_out), (a, g)

    def bwd(res, dy):
        a, g = res
        return _cols_da(dy, g, dm_out), _cols_dg(a, dy, g, dm_out)

    op.defvjp(fwd, bwd)
    return op


linear_cols = _make_linear_cols(False)
linear_cols_dm = _make_linear_cols(True)


def _silu_mul(g, u):
    return _silu(g) * u


def _ffn_down_fwd(g, u, w, x):
    _, m_dim, r_dim = g.shape
    n_dim = w.shape[2]
    tm = _divisor_tile(m_dim, 1024)
    tn = _divisor_tile(n_dim, 1024)
    nj = N_DEV

    def body(g_ref, u_ref, w_ref, x_ref, o_ref, acc_ref):
        j = pl.program_id(2)

        @pl.when(j == 0)
        def _():
            acc_ref[...] = jnp.zeros_like(acc_ref)

        acc_ref[...] += _bdot(_silu_mul(g_ref[0], u_ref[0]), w_ref[0])

        @pl.when(j == nj - 1)
        def _():
            o_ref[...] = acc_ref[...] + x_ref[...]

    h_spec = pl.BlockSpec((1, tm, r_dim), lambda i, n, j: (j, i, 0))
    o_spec = pl.BlockSpec((tm, tn), lambda i, n, j: (i, n))
    return pl.pallas_call(
        body, grid=(m_dim // tm, n_dim // tn, nj),
        in_specs=[h_spec, h_spec, pl.BlockSpec((1, r_dim, tn), lambda i, n, j: (j, 0, n)), o_spec], out_specs=o_spec,
        out_shape=jax.ShapeDtypeStruct((m_dim, n_dim), F32), scratch_shapes=[pltpu.VMEM((tm, tn), F32)],
        compiler_params=_cparams(("parallel", "parallel", "arbitrary")), name="ffn_down_fwd",
    )(g, u, w, x)


def _ffn_down_dh(dy, g, u, w):
    m_dim, n_dim = dy.shape
    r_dim = w.shape[1]
    tm = _divisor_tile(m_dim, 1024)
    tn = _divisor_tile(n_dim, 1024)
    nn = n_dim // tn

    def body(dy_ref, w_ref, g_ref, u_ref, dg_ref, du_ref, acc_ref):
        n = pl.program_id(2)

        @pl.when(n == 0)
        def _():
            acc_ref[...] = jnp.zeros_like(acc_ref)

        acc_ref[...] += _bdot(dy_ref[...], w_ref[0], _NT_DIMS)

        @pl.when(n == nn - 1)
        def _():
            _, pullback = jax.vjp(_silu_mul, g_ref[0], u_ref[0])
            dg_ref[0], du_ref[0] = pullback(acc_ref[...])

    h_spec = pl.BlockSpec((1, tm, r_dim), lambda i, j, n: (j, i, 0))
    return pl.pallas_call(
        body, grid=(m_dim // tm, N_DEV, nn),
        in_specs=[pl.BlockSpec((tm, tn), lambda i, j, n: (i, n)), pl.BlockSpec((1, r_dim, tn), lambda i, j, n: (j, 0, n)),
                  h_spec, h_spec],
        out_specs=[h_spec, h_spec], out_shape=[jax.ShapeDtypeStruct(g.shape, F32)] * 2,
        scratch_shapes=[pltpu.VMEM((tm, r_dim), F32)],
        compiler_params=_cparams(("parallel", "arbitrary", "arbitrary")), name="ffn_down_dh",
    )(dy, w, g, u)


def _ffn_down_dw(g, u, dy, w):
    _, m_dim, r_dim = g.shape
    n_dim = dy.shape[1]
    tm = _divisor_tile(m_dim, 2048)
    tn = _divisor_tile(n_dim, 512)
    nm = m_dim // tm

    def body(g_ref, u_ref, dy_ref, o_ref, acc_ref):
        m = pl.program_id(2)

        @pl.when(m == 0)
        def _():
            acc_ref[...] = jnp.zeros_like(acc_ref)

        acc_ref[...] += _bdot(_silu_mul(g_ref[0], u_ref[0]), dy_ref[...], _TN_DIMS)

        @pl.when(m == nm - 1)
        def _():
            o_ref[0] = acc_ref[...].astype(o_ref.dtype)

    h_spec = pl.BlockSpec((1, tm, r_dim), lambda j, n, m: (j, m, 0))
    return pl.pallas_call(
        body, grid=(N_DEV, n_dim // tn, nm),
        in_specs=[h_spec, h_spec, pl.BlockSpec((tm, tn), lambda j, n, m: (m, n))],
        out_specs=pl.BlockSpec((1, r_dim, tn), lambda j, n, m: (j, 0, n)),
        out_shape=jax.ShapeDtypeStruct(w.shape, w.dtype), scratch_shapes=[pltpu.VMEM((r_dim, tn), F32)],
        compiler_params=_cparams(("parallel", "parallel", "arbitrary")), name="ffn_down_dw",
    )(g, u, dy)


@jax.custom_vjp
def ffn_down(g, u, w, x):
    return _ffn_down_fwd(g, u, w, x)


def _ffn_down_vjp_fwd(g, u, w, x):
    return _ffn_down_fwd(g, u, w, x), (g, u, w)


def _ffn_down_vjp_bwd(res, dy):
    g, u, w = res
    dg, du = _ffn_down_dh(dy, g, u, w)
    return dg, du, _ffn_down_dw(g, u, dy, w), dy


ffn_down.defvjp(_ffn_down_vjp_fwd, _ffn_down_vjp_bwd)


def _cols_to_natural_call(g, width):
    _, k_dim, c_dim = g.shape
    tk = _divisor_tile(k_dim, 256, SUBLANE)

    def body(g_ref, o_ref):
        for j in range(N_DEV):
            o_ref[:, j * c_dim:(j + 1) * c_dim] = g_ref[j]
        if width > N_DEV * c_dim:
            o_ref[:, N_DEV * c_dim:] = jnp.zeros((tk, width - N_DEV * c_dim), o_ref.dtype)

    return pl.pallas_call(
        body, grid=(k_dim // tk,), in_specs=[pl.BlockSpec((N_DEV, tk, c_dim), lambda i: (0, i, 0))],
        out_specs=pl.BlockSpec((tk, width), lambda i: (i, 0)), out_shape=jax.ShapeDtypeStruct((k_dim, width), g.dtype),
        compiler_params=_cparams(("parallel",)), name="cols_to_natural",
    )(g)


def _natural_to_cols_call(w, c_dim):
    k_dim, width = w.shape
    tk = _divisor_tile(k_dim, 256, SUBLANE)

    def body(w_ref, o_ref):
        for j in range(N_DEV):
            o_ref[j] = w_ref[:, j * c_dim:(j + 1) * c_dim]

    return pl.pallas_call(
        body, grid=(k_dim // tk,), in_specs=[pl.BlockSpec((tk, width), lambda i: (i, 0))],
        out_specs=pl.BlockSpec((N_DEV, tk, c_dim), lambda i: (0, i, 0)),
        out_shape=jax.ShapeDtypeStruct((N_DEV, k_dim, c_dim), w.dtype),
        compiler_params=_cparams(("parallel",)), name="natural_to_cols",
    )(w)


@functools.partial(jax.custom_vjp, nondiff_argnums=(1,))
def cols_to_natural(g, width):
    return _cols_to_natural_call(g, width)


def _cols_to_natural_fwd(g, width):
    return _cols_to_natural_call(g, width), g.shape[2]


def _cols_to_natural_bwd(width, c_dim, dw):
    return (_natural_to_cols_call(dw, c_dim),)


cols_to_natural.defvjp(_cols_to_natural_fwd, _cols_to_natural_bwd)


def make_rowop(fn, name, tm=256, passthrough=0):
    def specs(rows, params):
        row_specs = [pl.BlockSpec((tm, r.shape[1]), lambda i: (i, 0)) for r in rows]
        par_specs = [pl.BlockSpec(p.shape, lambda i: (0, 0)) for p in params]
        return row_specs, par_specs

    def out_structs(rows, params):
        tiles = [jax.ShapeDtypeStruct((tm, r.shape[1]), r.dtype) for r in rows]
        return jax.eval_shape(lambda r, p: fn(*r, *p), tiles, list(params))

    def fwd_call(rows, params):
        m_dim = rows[0].shape[0]
        n_in = len(rows) + len(params)
        outs = out_structs(rows, params)

        def body(*refs):
            res = fn(*[r[...] for r in refs[:n_in]])
            for o_ref, r in zip(refs[n_in:], res):
                o_ref[...] = r.astype(o_ref.dtype)

        row_specs, par_specs = specs(rows, params)
        return pl.pallas_call(
            body,
            grid=(m_dim // tm,),
            in_specs=row_specs + par_specs,
            out_specs=[pl.BlockSpec((tm, o.shape[1]), lambda i: (i, 0)) for o in outs],
            out_shape=[jax.ShapeDtypeStruct((m_dim, o.shape[1]), o.dtype) for o in outs],
            compiler_params=_cparams(("parallel",)),
            name=name + "_fwd",
        )(*rows, *params)

    def bwd_call(rows, params, cts):
        m_dim = rows[0].shape[0]
        n_rows, n_par = len(rows), len(params)
        n_in = n_rows + n_par
        n_ct = len(cts)
        n_fn = n_ct - passthrough

        def body(*refs):
            vals = [r[...] for r in refs[:n_in]]
            ct_vals = tuple(r[...] for r in refs[n_in:n_in + n_fn])
            pass_refs = refs[n_in + n_fn:n_in + n_ct]
            drow_refs = refs[n_in + n_ct:n_in + n_ct + n_rows]
            dpar_refs = refs[n_in + n_ct + n_rows:]
            _, pullback = jax.vjp(fn, *vals)
            grads = pullback(ct_vals)
            for i, (d_ref, g) in enumerate(zip(drow_refs, grads[:n_rows])):
                d_ref[...] = g + pass_refs[i][...] if i < passthrough else g

            @pl.when(pl.program_id(0) == 0)
            def _():
                for d_ref in dpar_refs:
                    d_ref[...] = jnp.zeros_like(d_ref)

            for d_ref, g in zip(dpar_refs, grads[n_rows:]):
                d_ref[...] += g

        row_specs, par_specs = specs(rows, params)
        ct_specs = [pl.BlockSpec((tm, c.shape[1]), lambda i: (i, 0)) for c in cts]
        res = pl.pallas_call(
            body,
            grid=(m_dim // tm,),
            in_specs=row_specs + par_specs + ct_specs,
            out_specs=row_specs + par_specs,
            out_shape=[jax.ShapeDtypeStruct(r.shape, r.dtype) for r in rows]
            + [jax.ShapeDtypeStruct(p.shape, p.dtype) for p in params],
            compiler_params=_cparams(("arbitrary",)),
            name=name + "_bwd",
        )(*rows, *params, *cts)
        return tuple(res[:n_rows]), tuple(res[n_rows:])

    @jax.custom_vjp
    def op(rows, params):
        return tuple(fwd_call(rows, params)) + tuple(rows[:passthrough])

    def op_fwd(rows, params):
        return tuple(fwd_call(rows, params)) + tuple(rows[:passthrough]), (rows, params)

    def op_bwd(res, cts):
        rows, params = res
        return bwd_call(rows, params, tuple(cts))

    op.defvjp(op_fwd, op_bwd)
    return op


def _rms(x, g):
    return x * lax.rsqrt(jnp.mean(x * x, axis=-1, keepdims=True) + RMS_EPS) * g


def _sigmoid(x):
    return 1.0 / (1.0 + jnp.exp(-x))


def _silu(x):
    return x * _sigmoid(x)


def _bdot(a, b, dims=(((1,), (0,)), ((), ()))):
    return lax.dot_general(a.astype(BF16), b.astype(BF16), dims, preferred_element_type=F32)


def _rmsnorm_fn(x, g):
    return (_rms(x, g),)


def rmsnorm(x, g, name):
    return make_rowop(_rmsnorm_fn, name)((x,), (g.reshape(1, -1),))[0]


def rmsnorm_res(x, g, name):
    return make_rowop(_rmsnorm_fn, name, passthrough=1)((x,), (g.reshape(1, -1),))


def _gelu_tanh(x):
    return 0.5 * x * (1.0 + jnp.tanh(0.7978845608028654 * (x + 0.044715 * x * x * x)))


def _softplus(x):
    return jnp.maximum(x, 0.0) + jnp.log(1.0 + jnp.exp(-jnp.abs(x)))


def _s5_post_fn(y, w_glu, b_glu):
    h = _gelu_tanh(y)
    return (h * _sigmoid(_bdot(h, w_glu) + b_glu),)


def _loss_fn(y, t):
    err = y - t
    return (0.5 * jnp.mean(err * err, axis=-1, keepdims=True),)


def _pair_headnorm(x, g2):
    lo = lax.broadcasted_iota(jnp.int32, x.shape, 1) < CA_HEAD_DIM
    sq = x * x
    s_lo = jnp.sum(jnp.where(lo, sq, 0.0), axis=-1, keepdims=True)
    s_hi = jnp.sum(jnp.where(lo, 0.0, sq), axis=-1, keepdims=True)
    ms = jnp.where(lo, s_lo, s_hi) * (1.0 / CA_HEAD_DIM)
    return x * lax.rsqrt(ms + RMS_EPS) * g2


def _ca_qknorm_fn(qkv, qg2, kg2):
    qs, ks = [], []
    for j in range(D_MODEL // LANE):
        qs.append(_pair_headnorm(qkv[:, j * LANE:(j + 1) * LANE], qg2))
        ks.append(_pair_headnorm(qkv[:, D_MODEL + j * LANE:D_MODEL + (j + 1) * LANE], kg2))
    return jnp.concatenate(qs, axis=1), jnp.concatenate(ks, axis=1)


def _xattn_fn(q, k, v, qg, kg):
    outs = []
    for h in range(XA_HEADS):
        sl = slice(h * XA_HEAD_DIM, (h + 1) * XA_HEAD_DIM)
        qh = _rms(q[:, sl], qg)
        kh = _rms(k[:, sl], kg)
        s = _bdot(qh, kh, (((1,), (1,)), ((), ()))) * (XA_HEAD_DIM ** -0.5)
        p = jnp.exp(s - jnp.max(s, axis=-1, keepdims=True))
        p = p / jnp.sum(p, axis=-1, keepdims=True)
        outs.append(_bdot(p, v[:, sl]))
    return (jnp.concatenate(outs, axis=1),)


def _gdn_prep_fn(x0, x1, x2, x3, ab, conv_w, alog, dtb):
    c = conv_w[3:4, :] * x0 + conv_w[2:3, :] * x1 + conv_w[1:2, :] * x2 + conv_w[0:1, :] * x3
    c = _silu(c)
    qs, ks = [], []
    for h in range(GDN_HEADS):
        qh = c[:, h * LANE:(h + 1) * LANE]
        kh = c[:, GDN_WIDTH + h * LANE:GDN_WIDTH + (h + 1) * LANE]
        qs.append(qh * lax.rsqrt(jnp.sum(qh * qh, axis=-1, keepdims=True) + RMS_EPS) * (GDN_HEAD_DIM ** -0.5))
        ks.append(kh * lax.rsqrt(jnp.sum(kh * kh, axis=-1, keepdims=True) + RMS_EPS))
    lane = lax.broadcasted_iota(jnp.int32, ab.shape, 1)
    g = -jnp.exp(alog) * _softplus(ab + dtb)
    beta = _sigmoid(ab)
    bg = jnp.where(lane < GDN_HEADS, g, jnp.where(lane < 2 * GDN_HEADS, beta, 0.0))
    return jnp.concatenate(qs, axis=1), jnp.concatenate(ks, axis=1), c[:, 2 * GDN_WIDTH:], bg


def _gdn_out_fn(o, gate, og):
    outs = []
    for h in range(GDN_HEADS):
        sl = slice(h * LANE, (h + 1) * LANE)
        outs.append(_rms(o[:, sl], og) * _silu(gate[:, sl]))
    return (jnp.concatenate(outs, axis=1),)


CA_QB = 4 * CHUNK
CA_KB = CA_QB + CA_PAD


def _ca_math(q2, kb2, vb2, bias2, c):
    lane = lax.broadcasted_iota(jnp.int32, q2.shape, 1)
    qc = lax.broadcasted_iota(jnp.int32, (CA_QB, CA_KB), 0) // CHUNK
    kc = lax.broadcasted_iota(jnp.int32, (CA_QB, CA_KB), 1) // CHUNK
    valid = (kc >= qc) & (kc <= qc + CA_LEFT) & (kc + c * (CA_QB // CHUNK) >= CA_LEFT)
    out = jnp.zeros(q2.shape, F32)
    for h in range(2):
        mine = (lane >= h * CA_HEAD_DIM) & (lane < (h + 1) * CA_HEAD_DIM)
        qh = jnp.where(mine, q2, 0.0)
        s = _bdot(qh, kb2, (((1,), (1,)), ((), ()))) * (CA_HEAD_DIM ** -0.5) + bias2[h]
        s = jnp.where(valid, s, -1e30)
        p = jnp.exp(s - jnp.max(s, axis=-1, keepdims=True))
        p = p / jnp.sum(p, axis=-1, keepdims=True)
        out = out + jnp.where(mine, _bdot(p, vb2), 0.0)
    return out


CA_VEC = CA_QB + CA_KB


def _ca_specs(seq):
    q_spec = pl.BlockSpec((CA_QB, LANE), lambda hp, c: (c, hp))
    kv_spec = pl.BlockSpec((seq + CA_PAD, LANE), lambda hp, c: (0, hp))
    b_spec = pl.BlockSpec((1, 2, CA_VEC), lambda hp, c: (hp, 0, 0))
    return (D_MODEL // LANE, seq // CA_QB), q_spec, kv_spec, b_spec


def _ca_bias_from_vector(vec_ref, bias_ref):
    for h in range(2):
        rows = jnp.broadcast_to(vec_ref[0, h:h + 1, :], (CA_QB, CA_VEC))
        bias_ref[h] = pltpu.roll(rows, 0, 1, stride=1, stride_axis=0)[:, CA_QB:]


def _ca_vector_grad(dbias):
    d = jnp.concatenate([jnp.zeros((CA_QB, CA_QB), F32), dbias], axis=1)
    row = lax.broadcasted_iota(jnp.int32, d.shape, 0)
    for bit in range(CA_QB.bit_length() - 1):
        d = jnp.where((row >> bit) & 1 == 1, pltpu.roll(d, CA_VEC - (1 << bit), 1), d)
    return jnp.sum(d, axis=0, keepdims=True)


def _ca_fwd_call(q, kpad, vpad, vec):
    grid, q_spec, kv_spec, b_spec = _ca_specs(q.shape[0])

    def body(q_ref, k_ref, v_ref, vec_ref, o_ref, bias_ref):
        c = pl.program_id(1)
        start = pl.multiple_of(c * CA_QB, CA_QB)

        @pl.when(c == 0)
        def _():
            _ca_bias_from_vector(vec_ref, bias_ref)

        o_ref[...] = _ca_math(q_ref[...], k_ref[pl.ds(start, CA_KB), :], v_ref[pl.ds(start, CA_KB), :],
                              bias_ref[...], c)

    return pl.pallas_call(
        body, grid=grid, in_specs=[q_spec, kv_spec, kv_spec, b_spec], out_specs=q_spec,
        out_shape=jax.ShapeDtypeStruct(q.shape, F32), scratch_shapes=[pltpu.VMEM((2, CA_QB, CA_KB), F32)],
        compiler_params=_cparams(("parallel", "arbitrary")), name="chunkattn_fwd",
    )(q, kpad, vpad, vec)


def _ca_bwd_call(q, kpad, vpad, vec, do):
    grid, q_spec, kv_spec, b_spec = _ca_specs(q.shape[0])
    last = grid[1] - 1

    def body(q_ref, k_ref, v_ref, vec_ref, do_ref, dq_ref, dk_ref, dv_ref, dvec_ref, bias_ref, dbias_ref):
        c = pl.program_id(1)
        start = pl.multiple_of(c * CA_QB, CA_QB)

        @pl.when(c == 0)
        def _():
            _ca_bias_from_vector(vec_ref, bias_ref)
            dk_ref[...] = jnp.zeros_like(dk_ref)
            dv_ref[...] = jnp.zeros_like(dv_ref)
            dbias_ref[...] = jnp.zeros_like(dbias_ref)

        _, pullback = jax.vjp(lambda a, b, d, e: _ca_math(a, b, d, e, c), q_ref[...],
                              k_ref[pl.ds(start, CA_KB), :], v_ref[pl.ds(start, CA_KB), :], bias_ref[...])
        dq, dkb, dvb, dbias = pullback(do_ref[...])
        dq_ref[...] = dq
        dk_ref[pl.ds(start, CA_KB), :] += dkb
        dv_ref[pl.ds(start, CA_KB), :] += dvb
        dbias_ref[...] += dbias

        @pl.when(c == last)
        def _():
            for h in range(2):
                dvec_ref[0, h:h + 1, :] = _ca_vector_grad(dbias_ref[h])

    return pl.pallas_call(
        body, grid=grid, in_specs=[q_spec, kv_spec, kv_spec, b_spec, q_spec],
        out_specs=[q_spec, kv_spec, kv_spec, b_spec],
        out_shape=[jax.ShapeDtypeStruct(q.shape, F32), jax.ShapeDtypeStruct(kpad.shape, F32),
                   jax.ShapeDtypeStruct(vpad.shape, F32), jax.ShapeDtypeStruct(vec.shape, F32)],
        scratch_shapes=[pltpu.VMEM((2, CA_QB, CA_KB), F32), pltpu.VMEM((2, CA_QB, CA_KB), F32)],
        compiler_params=_cparams(("parallel", "arbitrary")), name="chunkattn_bwd",
    )(q, kpad, vpad, vec, do)


@jax.custom_vjp
def chunk_attn_core(q, kpad, vpad, vec):
    return _ca_fwd_call(q, kpad, vpad, vec)


def _ca_core_fwd(q, kpad, vpad, vec):
    return _ca_fwd_call(q, kpad, vpad, vec), (q, kpad, vpad, vec)


def _ca_core_bwd(res, do):
    return tuple(_ca_bwd_call(*res, do))


chunk_attn_core.defvjp(_ca_core_fwd, _ca_core_bwd)


S5_GB = 4
S5_U = S5_WIDTH // S5_GB
S5_L = S5_GROUPS * S5_STATE // S5_GB


def _cmul(ar, ai, br, bi):
    return ar * br - ai * bi, ar * bi + ai * br


def _hdot(a, b, dims=(((1,), (0,)), ((), ()))):
    return lax.dot_general(a, b, dims, precision=HI, preferred_element_type=F32)


def _split_dot(a, b, dims=(((1,), (0,)), ((), ()))):
    a_hi, b_hi = a.astype(BF16), b.astype(BF16)
    a_lo = (a - a_hi.astype(F32)).astype(BF16)
    b_lo = (b - b_hi.astype(F32)).astype(BF16)
    dot = functools.partial(lax.dot_general, dimension_numbers=dims, preferred_element_type=F32)
    return dot(a_hi, b_hi) + (dot(a_hi, b_lo) + dot(a_lo, b_hi))


_NT = (((1,), (1,)), ((), ()))
_TN = (((0,), (0,)), ((), ()))


def _s5_tables(lr, li, reverse):
    p = {1: (lr, li)}
    p[2] = _cmul(*p[1], *p[1])
    p[4] = _cmul(*p[2], *p[2])
    p[3] = _cmul(*p[2], *p[1])
    p[5] = _cmul(*p[4], *p[1])
    p[6] = _cmul(*p[4], *p[2])
    p[7] = _cmul(*p[4], *p[3])
    p[8] = _cmul(*p[4], *p[4])
    row = lax.broadcasted_iota(jnp.int32, (SUBLANE, lr.shape[1]), 0)
    tr = jnp.zeros(row.shape, F32)
    ti = jnp.zeros(row.shape, F32)
    for i in range(SUBLANE):
        k = SUBLANE - i if reverse else i + 1
        tr = jnp.where(row == i, p[k][0], tr)
        ti = jnp.where(row == i, p[k][1], ti)
    return p, (tr, ti), row


def _s5_block_scan(xr, xi, p, tab, row, hr, hi, reverse):
    for k in (1, 2, 4):
        if reverse:
            sr = jnp.where(row < SUBLANE - k, pltpu.roll(xr, SUBLANE - k, 0), 0.0)
            si = jnp.where(row < SUBLANE - k, pltpu.roll(xi, SUBLANE - k, 0), 0.0)
        else:
            sr = jnp.where(row >= k, pltpu.roll(xr, k, 0), 0.0)
            si = jnp.where(row >= k, pltpu.roll(xi, k, 0), 0.0)
        ar, ai = _cmul(p[k][0], p[k][1], sr, si)
        xr, xi = xr + ar, xi + ai
    cr, ci = _cmul(tab[0], tab[1], hr, hi)
    return xr + cr, xi + ci


def _s5_forward_scan(sr_ref, si_ref, lr, li):
    n_blocks = sr_ref.shape[0] // SUBLANE
    p, tab, row = _s5_tables(lr, li, False)

    def step(b, carry):
        base = pl.multiple_of(b * SUBLANE, SUBLANE)
        xr, xi = _s5_block_scan(sr_ref[pl.ds(base, SUBLANE), :], si_ref[pl.ds(base, SUBLANE), :],
                                p, tab, row, carry[0], carry[1], False)
        sr_ref[pl.ds(base, SUBLANE), :] = xr
        si_ref[pl.ds(base, SUBLANE), :] = xi
        return xr[SUBLANE - 1:SUBLANE, :], xi[SUBLANE - 1:SUBLANE, :]

    zero = jnp.zeros((1, lr.shape[1]), F32)
    lax.fori_loop(0, n_blocks, step, (zero, zero))


def _s5_specs(seq):
    u_spec = pl.BlockSpec((seq, S5_U), lambda g: (0, g))
    bd_spec = pl.BlockSpec((1, S5_U, S5_L), lambda g: (g, 0, 0))
    cd_spec = pl.BlockSpec((1, S5_L, S5_U), lambda g: (g, 0, 0))
    lam_spec = pl.BlockSpec((1, 2, S5_L), lambda g: (g, 0, 0))
    d_spec = pl.BlockSpec((1, S5_U), lambda g: (0, g))
    return u_spec, bd_spec, cd_spec, lam_spec, d_spec


S5_ROWS = 256


def _row_chunks(seq, fn):
    rows_per = min(S5_ROWS, seq)

    def step(r, carry):
        fn(pl.ds(pl.multiple_of(r * rows_per, rows_per), rows_per))
        return carry

    lax.fori_loop(0, seq // rows_per, step, 0)


def _s5_fwd_call(u, bdr, bdi, cdr, cdi, lam, d):
    seq = u.shape[0]
    u_spec, bd_spec, cd_spec, lam_spec, d_spec = _s5_specs(seq)

    def body(u_ref, bdr_ref, bdi_ref, cdr_ref, cdi_ref, lam_ref, d_ref, y_ref, sr_ref, si_ref):
        def project_in(rows):
            uv = u_ref[rows, :]
            sr_ref[rows, :] = _bdot(uv, bdr_ref[0])
            si_ref[rows, :] = _bdot(uv, bdi_ref[0])

        def project_out(rows):
            y_ref[rows, :] = (_bdot(sr_ref[rows, :], cdr_ref[0]) - _bdot(si_ref[rows, :], cdi_ref[0])
                              + d_ref[...] * u_ref[rows, :])

        _row_chunks(seq, project_in)
        _s5_forward_scan(sr_ref, si_ref, lam_ref[0, 0:1, :], lam_ref[0, 1:2, :])
        _row_chunks(seq, project_out)

    return pl.pallas_call(
        body, grid=(S5_GB,), in_specs=[u_spec, bd_spec, bd_spec, cd_spec, cd_spec, lam_spec, d_spec],
        out_specs=u_spec, out_shape=jax.ShapeDtypeStruct(u.shape, F32),
        scratch_shapes=[pltpu.VMEM((seq, S5_L), F32), pltpu.VMEM((seq, S5_L), F32)],
        compiler_params=_cparams(("parallel",)), name="s5_fwd",
    )(u, bdr, bdi, cdr, cdi, lam, d)


def _s5_bwd_call(u, bdr, bdi, cdr, cdi, lam, d, dy):
    seq = u.shape[0]
    n_blocks = seq // SUBLANE
    u_spec, bd_spec, cd_spec, lam_spec, d_spec = _s5_specs(seq)

    def body(u_ref, bdr_ref, bdi_ref, cdr_ref, cdi_ref, lam_ref, d_ref, dy_ref,
             du_ref, dbdr_ref, dbdi_ref, dcdr_ref, dcdi_ref, dlam_ref, dd_ref, sr_ref, si_ref, gr_ref, gi_ref):
        lr, li = lam_ref[0, 0:1, :], lam_ref[0, 1:2, :]

        def project_in(rows):
            uv = u_ref[rows, :]
            dyv = dy_ref[rows, :]
            sr_ref[rows, :] = _bdot(uv, bdr_ref[0])
            si_ref[rows, :] = _bdot(uv, bdi_ref[0])
            gr_ref[rows, :] = _bdot(dyv, cdr_ref[0], _NT)
            gi_ref[rows, :] = -_bdot(dyv, cdi_ref[0], _NT)

        _row_chunks(seq, project_in)
        _s5_forward_scan(sr_ref, si_ref, lr, li)
        p, tab, row = _s5_tables(lr, -li, True)

        def step(i, carry):
            hr, hi, acc_r, acc_i = carry
            b = n_blocks - 1 - i
            base = pl.multiple_of(b * SUBLANE, SUBLANE)
            xr, xi = _s5_block_scan(gr_ref[pl.ds(base, SUBLANE), :], gi_ref[pl.ds(base, SUBLANE), :],
                                    p, tab, row, hr, hi, True)
            gr_ref[pl.ds(base, SUBLANE), :] = xr
            gi_ref[pl.ds(base, SUBLANE), :] = xi
            prev = pl.multiple_of(jnp.maximum(b - 1, 0) * SUBLANE, SUBLANE)
            keep = (b > 0).astype(F32)
            last_r = sr_ref[pl.ds(prev, SUBLANE), :][SUBLANE - 1:SUBLANE, :] * keep
            last_i = si_ref[pl.ds(prev, SUBLANE), :][SUBLANE - 1:SUBLANE, :] * keep
            pr = jnp.where(row >= 1, pltpu.roll(sr_ref[pl.ds(base, SUBLANE), :], 1, 0), last_r)
            pi = jnp.where(row >= 1, pltpu.roll(si_ref[pl.ds(base, SUBLANE), :], 1, 0), last_i)
            acc_r = acc_r + pr * xr + pi * xi
            acc_i = acc_i + pr * xi - pi * xr
            return xr[0:1, :], xi[0:1, :], acc_r, acc_i

        zero = jnp.zeros((1, S5_L), F32)
        zacc = jnp.zeros((SUBLANE, S5_L), F32)
        _, _, acc_r, acc_i = lax.fori_loop(0, n_blocks, step, (zero, zero, zacc, zacc))
        dlam_ref[0, 0:1, :] = jnp.sum(acc_r, axis=0, keepdims=True)
        dlam_ref[0, 1:2, :] = jnp.sum(acc_i, axis=0, keepdims=True)
        for ref in (dbdr_ref, dbdi_ref, dcdr_ref, dcdi_ref, dd_ref):
            ref[...] = jnp.zeros_like(ref)

        def grads(rows):
            uv, dyv = u_ref[rows, :], dy_ref[rows, :]
            grv, giv = gr_ref[rows, :], gi_ref[rows, :]
            du_ref[rows, :] = _bdot(grv, bdr_ref[0], _NT) + _bdot(giv, bdi_ref[0], _NT) + d_ref[...] * dyv
            dbdr_ref[0] += _bdot(uv, grv, _TN)
            dbdi_ref[0] += _bdot(uv, giv, _TN)
            dcdr_ref[0] += _bdot(sr_ref[rows, :], dyv, _TN)
            dcdi_ref[0] -= _bdot(si_ref[rows, :], dyv, _TN)
            dd_ref[...] += jnp.sum(dyv * uv, axis=0, keepdims=True)

        _row_chunks(seq, grads)

    scratch = [pltpu.VMEM((seq, S5_L), F32) for _ in range(4)]
    return pl.pallas_call(
        body, grid=(S5_GB,),
        in_specs=[u_spec, bd_spec, bd_spec, cd_spec, cd_spec, lam_spec, d_spec, u_spec],
        out_specs=[u_spec, bd_spec, bd_spec, cd_spec, cd_spec, lam_spec, d_spec],
        out_shape=[jax.ShapeDtypeStruct(a.shape, F32) for a in (u, bdr, bdi, cdr, cdi, lam, d)],
        scratch_shapes=scratch, compiler_params=_cparams(("parallel",)), name="s5_bwd",
    )(u, bdr, bdi, cdr, cdi, lam, d, dy)


@jax.custom_vjp
def s5_core(u, bdr, bdi, cdr, cdi, lam, d):
    return _s5_fwd_call(u, bdr, bdi, cdr, cdi, lam, d)


def _s5_core_fwd(*args):
    return _s5_fwd_call(*args), args


def _s5_core_bwd(res, dy):
    return tuple(_s5_bwd_call(*res, dy))


s5_core.defvjp(_s5_core_fwd, _s5_core_bwd)


def _s5_discretize(a_re, a_im, log_dt, b_re, b_im, c_re, c_im, d):
    dt = jnp.exp(log_dt)[:, None]
    mag = jnp.exp(a_re * dt)
    lbr, lbi = mag * jnp.cos(a_im * dt), mag * jnp.sin(a_im * dt)
    den = a_re * a_re + a_im * a_im
    fr = ((lbr - 1.0) * a_re + lbi * a_im) / den
    fi = (lbi * a_re - (lbr - 1.0) * a_im) / den
    bbr = fr[:, :, None] * b_re - fi[:, :, None] * b_im
    bbi = fr[:, :, None] * b_im + fi[:, :, None] * b_re
    eye = jnp.eye(S5_GROUPS // S5_GB, dtype=F32)
    gl = S5_GROUPS // S5_GB

    def bd(t):
        return jnp.einsum('bgpc,gh->bgchp', t.reshape(S5_GB, gl, S5_STATE, S5_GROUP), eye).reshape(S5_GB, S5_U, S5_L)

    def cd(t):
        return jnp.einsum('bgcp,gh->bgphc', t.reshape(S5_GB, gl, S5_GROUP, S5_STATE), eye).reshape(S5_GB, S5_L, S5_U)

    lam = jnp.stack([lbr.reshape(S5_GB, S5_L), lbi.reshape(S5_GB, S5_L)], axis=1)
    return bd(bbr), bd(bbi), cd(c_re), cd(c_im), lam, d.reshape(1, S5_WIDTH)


@jax.custom_vjp
def _unit_lower_solve(neg_a, rhs, tinv):
    return _split_dot(tinv, rhs)


def _unit_lower_solve_fwd(neg_a, rhs, tinv):
    x = _split_dot(tinv, rhs)
    return x, (x, tinv)


def _unit_lower_solve_bwd(res, dx):
    x, tinv = res
    g = _split_dot(tinv, dx, _TN)
    return _split_dot(g, x, _NT), g, jnp.zeros_like(tinv)


_unit_lower_solve.defvjp(_unit_lower_solve_fwd, _unit_lower_solve_bwd)


def _unit_lower_inverse(neg_a):
    r = lax.broadcasted_iota(jnp.int32, neg_a.shape, 0)
    c = lax.broadcasted_iota(jnp.int32, neg_a.shape, 1)
    p = (r == c).astype(F32) + neg_a
    npow = _split_dot(neg_a, neg_a)
    for _ in range(4):
        y = _split_dot(jnp.concatenate([p, npow], axis=0), npow)
        p = p + y[:CHUNK]
        npow = y[CHUNK:]
    return p + _split_dot(p, npow)


def _gdn_chunk(q, k, v, g_col, b_col, st, tinv=None):
    r = lax.broadcasted_iota(jnp.int32, (CHUNK, CHUNK), 0)
    c = lax.broadcasted_iota(jnp.int32, (CHUNK, CHUNK), 1)
    eye = (r == c).astype(F32)
    strict = r > c
    causal = r >= c
    g_row = jnp.sum(g_col * eye, axis=0, keepdims=True)
    gcum = jnp.sum(jnp.where(causal, g_row, 0.0), axis=1, keepdims=True)
    gcum_row = jnp.sum(gcum * eye, axis=0, keepdims=True)
    diff = gcum - gcum_row
    decay_strict = jnp.where(strict, jnp.exp(jnp.where(strict, diff, 0.0)), 0.0)
    decay_causal = jnp.where(causal, jnp.exp(jnp.where(causal, diff, 0.0)), 0.0)
    gamma = jnp.exp(gcum)
    g_last = jnp.sum(jnp.where(lax.broadcasted_iota(jnp.int32, (CHUNK, 1), 0) == CHUNK - 1, gcum, 0.0),
                     axis=0, keepdims=True)
    kk = _bdot(k, k, _NT)
    neg_a = -(b_col * kk * decay_strict)
    if tinv is None:
        tinv = _unit_lower_inverse(neg_a)
    x = _unit_lower_solve(neg_a, jnp.concatenate([b_col * v, (b_col * gamma) * k], axis=1), lax.stop_gradient(tinv))
    u_new, w_k = x[:, :GDN_HEAD_DIM], x[:, GDN_HEAD_DIM:]
    qk = _bdot(q, k, _NT) * decay_causal
    q_g = q * gamma
    k_tail = k * jnp.exp(g_last - gcum)
    w = u_new - _bdot(w_k, st)
    o = _bdot(q_g, st) + _bdot(qk, w)
    st_new = jnp.exp(g_last) * st + _bdot(k_tail, w, _TN)
    return o, st_new, tinv


def _gdn_cols(bgv, h):
    lane = lax.broadcasted_iota(jnp.int32, bgv.shape, 1)
    g_col = jnp.sum(jnp.where(lane == h, bgv, 0.0), axis=1, keepdims=True)
    b_col = jnp.sum(jnp.where(lane == GDN_HEADS + h, bgv, 0.0), axis=1, keepdims=True)
    return g_col, b_col


GDN_CPS = 2


def _gdn_fwd_call(q, k, v, bg):
    seq = q.shape[0]
    n_chunks = seq // CHUNK
    n_steps = n_chunks // GDN_CPS
    rows = GDN_CPS * CHUNK
    x_spec = pl.BlockSpec((rows, GDN_WIDTH), lambda n: (n, 0))
    bg_spec = pl.BlockSpec((rows, LANE), lambda n: (n, 0))
    st_spec = pl.BlockSpec((GDN_CPS, GDN_WIDTH, GDN_HEAD_DIM), lambda n: (n, 0, 0))
    ti_spec = pl.BlockSpec((GDN_CPS, GDN_HEADS * CHUNK, CHUNK), lambda n: (n, 0, 0))

    def body(q_ref, k_ref, v_ref, bg_ref, o_ref, st_out_ref, ti_out_ref, st_ref):
        @pl.when(pl.program_id(0) == 0)
        def _():
            st_ref[...] = jnp.zeros_like(st_ref)

        for h in range(GDN_HEADS):
            sl = slice(h * GDN_HEAD_DIM, (h + 1) * GDN_HEAD_DIM)
            st = st_ref[sl, :]
            for cc in range(GDN_CPS):
                rs = slice(cc * CHUNK, (cc + 1) * CHUNK)
                g_col, b_col = _gdn_cols(bg_ref[rs, :], h)
                st_out_ref[cc, sl, :] = st
                o, st, tinv = _gdn_chunk(q_ref[rs, sl], k_ref[rs, sl], v_ref[rs, sl], g_col, b_col, st)
                o_ref[rs, sl] = o
                ti_out_ref[cc, h * CHUNK:(h + 1) * CHUNK, :] = tinv
            st_ref[sl, :] = st

    return pl.pallas_call(
        body, grid=(n_steps,), in_specs=[x_spec, x_spec, x_spec, bg_spec], out_specs=[x_spec, st_spec, ti_spec],
        out_shape=[jax.ShapeDtypeStruct(q.shape, F32),
                   jax.ShapeDtypeStruct((n_chunks, GDN_WIDTH, GDN_HEAD_DIM), F32),
                   jax.ShapeDtypeStruct((n_chunks, GDN_HEADS * CHUNK, CHUNK), F32)],
        scratch_shapes=[pltpu.VMEM((GDN_WIDTH, GDN_HEAD_DIM), F32)],
        compiler_params=_cparams(("arbitrary",)), name="gdn_fwd",
    )(q, k, v, bg)


def _gdn_bwd_call(q, k, v, bg, states, tinvs, do):
    seq = q.shape[0]
    n_steps = seq // CHUNK // GDN_CPS
    rows = GDN_CPS * CHUNK
    x_spec = pl.BlockSpec((rows, GDN_WIDTH), lambda i: (n_steps - 1 - i, 0))
    bg_spec = pl.BlockSpec((rows, LANE), lambda i: (n_steps - 1 - i, 0))
    st_spec = pl.BlockSpec((GDN_CPS, GDN_WIDTH, GDN_HEAD_DIM), lambda i: (n_steps - 1 - i, 0, 0))
    ti_spec = pl.BlockSpec((GDN_CPS, GDN_HEADS * CHUNK, CHUNK), lambda i: (n_steps - 1 - i, 0, 0))

    def body(q_ref, k_ref, v_ref, bg_ref, st_in_ref, ti_ref, do_ref, dq_ref, dk_ref, dv_ref, dbg_ref, dst_ref):
        @pl.when(pl.program_id(0) == 0)
        def _():
            dst_ref[...] = jnp.zeros_like(dst_ref)

        lane = lax.broadcasted_iota(jnp.int32, (CHUNK, LANE), 1)
        dbg = [jnp.zeros((CHUNK, LANE), F32) for _ in range(GDN_CPS)]
        for h in range(GDN_HEADS):
            sl = slice(h * GDN_HEAD_DIM, (h + 1) * GDN_HEAD_DIM)
            dst = dst_ref[sl, :]
            for cc in reversed(range(GDN_CPS)):
                rs = slice(cc * CHUNK, (cc + 1) * CHUNK)
                g_col, b_col = _gdn_cols(bg_ref[rs, :], h)
                tinv = ti_ref[cc, h * CHUNK:(h + 1) * CHUNK, :]
                _, pullback = jax.vjp(lambda *a: _gdn_chunk(*a, tinv=tinv)[:2], q_ref[rs, sl], k_ref[rs, sl],
                                      v_ref[rs, sl], g_col, b_col, st_in_ref[cc, sl, :])
                dq, dk, dv, dg, db, dst = pullback((do_ref[rs, sl], dst))
                dq_ref[rs, sl] = dq
                dk_ref[rs, sl] = dk
                dv_ref[rs, sl] = dv
                dbg[cc] = dbg[cc] + jnp.where(lane == h, dg, 0.0) + jnp.where(lane == GDN_HEADS + h, db, 0.0)
            dst_ref[sl, :] = dst
        for cc in range(GDN_CPS):
            dbg_ref[cc * CHUNK:(cc + 1) * CHUNK, :] = dbg[cc]

    return pl.pallas_call(
        body, grid=(n_steps,), in_specs=[x_spec, x_spec, x_spec, bg_spec, st_spec, ti_spec, x_spec],
        out_specs=[x_spec, x_spec, x_spec, bg_spec],
        out_shape=[jax.ShapeDtypeStruct(q.shape, F32)] * 3 + [jax.ShapeDtypeStruct(bg.shape, F32)],
        scratch_shapes=[pltpu.VMEM((GDN_WIDTH, GDN_HEAD_DIM), F32)],
        compiler_params=_cparams(("arbitrary",)), name="gdn_bwd",
    )(q, k, v, bg, states, tinvs, do)


@jax.custom_vjp
def gdn_core(q, k, v, bg):
    return _gdn_fwd_call(q, k, v, bg)[0]


def _gdn_core_fwd(q, k, v, bg):
    o, states, tinvs = _gdn_fwd_call(q, k, v, bg)
    return o, (q, k, v, bg, states, tinvs)


def _gdn_core_bwd(res, do):
    return tuple(_gdn_bwd_call(*res, do))


gdn_core.defvjp(_gdn_core_fwd, _gdn_core_bwd)


def _row(v):
    return v.reshape(1, -1)


def _lane_pad(v):
    return jnp.pad(v, (0, LANE - v.shape[0])).reshape(1, LANE)


def _delay_rows(x, k):
    return jnp.pad(x, ((k, 0), (0, 0)))[:x.shape[0]]


def s5_mixer(u, a_re, a_im, log_dt, b_re, b_im, c_re, c_im, d, w_glu, b_glu):
    y = s5_core(u, *_s5_discretize(a_re, a_im, log_dt, b_re, b_im, c_re, c_im, d))
    return make_rowop(_s5_post_fn, "s5_post")((y,), (w_glu, _row(b_glu)))[0]


def gated_deltanet(qkv, gate, ab, conv_w, a_log, dt_bias, out_g):
    rows = (qkv, _delay_rows(qkv, 1), _delay_rows(qkv, 2), _delay_rows(qkv, 3), ab)
    q, k, v, bg = make_rowop(_gdn_prep_fn, "gdn_prep")(rows, (conv_w, _lane_pad(a_log), _lane_pad(dt_bias)))
    o = gdn_core(q, k, v, bg)
    return make_rowop(_gdn_out_fn, "gdn_out")((o, gate), (_row(out_g),))[0]


def chunk_attention(x, h, w_qkv, w_out, q_g, k_g, rel_bias):
    qkv = linear_cols(h, w_qkv)
    qn, kn = make_rowop(_ca_qknorm_fn, "ca_qknorm")((qkv,), (_row(jnp.tile(q_g, 2)), _row(jnp.tile(k_g, 2))))
    kpad = jnp.pad(kn, ((CA_PAD, 0), (0, 0)))
    vpad = jnp.pad(qkv[:, 2 * D_MODEL:], ((CA_PAD, 0), (0, 0)))
    o = chunk_attn_core(qn, kpad, vpad, _rel_bias_vector(rel_bias))
    return linear_res(o, w_out, x)


def memory_cross_attention(x, h, mem_n, w_q, w_kv, w_out, q_g, k_g):
    q = linear(h, w_q)
    kv = linear_cols(mem_n, w_kv)
    o = make_rowop(_xattn_fn, "xattn")((q,), (kv[:, :D_MODEL], kv[:, D_MODEL:], _row(q_g), _row(k_g)))[0]
    return linear_res(o, w_out, x)


def swiglu(x, h, w_gate, w_up, w_down):
    return ffn_down(linear_cols_dm(h, w_gate), linear_cols_dm(h, w_up), w_down, x)


def _rel_bias_vector(rel_bias):
    heads = rel_bias.shape[0]
    n_far = CA_KB - 1 - MAX_REL
    n_neg = CA_VEC - 1 - n_far - (2 * MAX_REL + 1)
    vec = jnp.concatenate([jnp.zeros((heads, 1), F32),
                           jnp.broadcast_to(rel_bias[:, 2 * MAX_REL:], (heads, n_far)),
                           jnp.flip(rel_bias, axis=1),
                           jnp.broadcast_to(rel_bias[:, :1], (heads, n_neg))], axis=1)
    return vec.reshape(heads // 2, 2, CA_VEC)


def _exchange(arrays, modes, name):
    n = len(arrays)
    out_shapes = [jax.ShapeDtypeStruct((N_DEV,) + a.shape if m == "gather" else a.shape, a.dtype)
                  for a, m in zip(arrays, modes)]

    def body(*refs):
        ins, outs = refs[:n], refs[n:2 * n]
        send_sems, recv_sems, local_sems = refs[2 * n:]
        x, y, c = lax.axis_index("x"), lax.axis_index("y"), lax.axis_index("c")
        me = 4 * x + 2 * y + c
        pending = []
        for i in range(n):
            gather = modes[i] == "gather"
            local = pltpu.make_async_copy(ins[i] if gather else ins[i].at[me], outs[i].at[me], local_sems.at[i])
            local.start()
            pending.append(local)
        for k in range(1, N_DEV):
            px, py, pc = (x + (k >> 2)) % 2, (y + ((k >> 1) & 1)) % 2, (c + (k & 1)) % 2
            peer = 4 * px + 2 * py + pc
            for i in range(n):
                src = ins[i] if modes[i] == "gather" else ins[i].at[peer]
                sem = i * (N_DEV - 1) + k - 1
                send = pltpu.make_async_remote_copy(src_ref=src, dst_ref=outs[i].at[me], send_sem=send_sems.at[sem],
                                                    recv_sem=recv_sems.at[sem], device_id=(px, py, pc),
                                                    device_id_type=pl.DeviceIdType.MESH)
                send.start()
                arrival = pltpu.make_async_remote_copy(src_ref=src, dst_ref=outs[i].at[peer],
                                                       send_sem=send_sems.at[sem], recv_sem=recv_sems.at[sem],
                                                       device_id=(px, py, pc), device_id_type=pl.DeviceIdType.MESH)
                pending.append((send, arrival))
        for item in pending:
            if isinstance(item, tuple):
                item[0].wait_send()
                item[1].wait_recv()
            else:
                item.wait()

    any_spec = pl.BlockSpec(memory_space=pl.ANY)
    return pl.pallas_call(
        body, in_specs=[any_spec] * n, out_specs=[any_spec] * n, out_shape=out_shapes,
        scratch_shapes=[pltpu.SemaphoreType.DMA((n * (N_DEV - 1),)), pltpu.SemaphoreType.DMA((n * (N_DEV - 1),)),
                        pltpu.SemaphoreType.DMA((n,))],
        name=name,
    )(*arrays)


_HBM_SPEC = pl.BlockSpec(memory_space=pltpu.HBM)
_SEM_SPEC = pl.BlockSpec(memory_space=pltpu.SEMAPHORE)
_SIDE_EFFECT = pltpu.SideEffectType.DATAFLOW_SIDE_EFFECTING


def _peer(x, y, c, k):
    return (x + (k >> 2)) % 2, (y + ((k >> 1) & 1)) % 2, (c + (k & 1)) % 2


def _exchange_start(arrays, modes, after, name, carry):
    n = len(arrays)
    n_sem = n * (N_DEV - 1)
    lands = [pltpu.with_memory_space_constraint(lax.empty((N_DEV,) + a.shape if m == "gather" else a.shape, a.dtype),
                                                pltpu.HBM) for a, m in zip(arrays, modes)]
    arrays = [pltpu.with_memory_space_constraint(a, pltpu.HBM) for a in arrays]

    def body(*refs):
        ins, zones = refs[:n], refs[n:2 * n]
        send_sems, recv_sems, own_sems = refs[2 * n + 2:2 * n + 5]
        x, y, c = lax.axis_index("x"), lax.axis_index("y"), lax.axis_index("c")
        me = 4 * x + 2 * y + c
        for i in range(n):
            pltpu.make_async_copy(ins[i] if modes[i] == "gather" else ins[i].at[me], zones[i].at[me],
                                  own_sems.at[i]).start()
        for k in range(1, N_DEV):
            px, py, pc = _peer(x, y, c, k)
            peer = 4 * px + 2 * py + pc
            for i in range(n):
                sem = i * (N_DEV - 1) + k - 1
                pltpu.make_async_remote_copy(src_ref=ins[i] if modes[i] == "gather" else ins[i].at[peer],
                                             dst_ref=zones[i].at[me], send_sem=send_sems.at[sem],
                                             recv_sem=recv_sems.at[sem], device_id=(px, py, pc),
                                             device_id_type=pl.DeviceIdType.MESH).start()

    carry = pltpu.with_memory_space_constraint(carry, pltpu.HBM)
    out_shape = ((pltpu.SemaphoreType.DMA((n_sem,)), pltpu.SemaphoreType.DMA((n_sem,)), pltpu.SemaphoreType.DMA((n,)))
                 + tuple(pltpu.HBM(a.shape, a.dtype) for a in arrays) + tuple(pltpu.HBM(z.shape, z.dtype) for z in lands)
                 + (pltpu.HBM(carry.shape, carry.dtype),))
    aliases = {i: 3 + i for i in range(2 * n)}
    aliases[2 * n + 1] = 3 + 2 * n
    res = pl.pallas_call(
        body, name=name, out_shape=out_shape,
        in_specs=[_HBM_SPEC] * (2 * n) + [pl.BlockSpec(memory_space=pl.ANY), _HBM_SPEC],
        out_specs=(_SEM_SPEC,) * 3 + (_HBM_SPEC,) * (2 * n + 1),
        input_output_aliases=aliases,
        compiler_params=pltpu.CompilerParams(has_side_effects=_SIDE_EFFECT),
    )(*arrays, *lands, after, carry)
    return tuple(res[:3]), list(res[3:3 + n]), list(res[3 + n:3 + 2 * n]), res[3 + 2 * n]


def _exchange_wait(started, modes, after, name):
    sems, sources, zones, _ = started
    n = len(sources)

    def body(*refs):
        ins, lands = refs[:n], refs[n:2 * n]
        send_ref, recv_ref, own_ref = refs[2 * n:2 * n + 3]
        x, y, c = lax.axis_index("x"), lax.axis_index("y"), lax.axis_index("c")
        me = 4 * x + 2 * y + c
        for i in range(n):
            pltpu.make_async_copy(ins[i] if modes[i] == "gather" else ins[i].at[me], lands[i].at[me],
                                  own_ref.at[i]).wait()
        for k in range(1, N_DEV):
            px, py, pc = _peer(x, y, c, k)
            peer = 4 * px + 2 * py + pc
            for i in range(n):
                sem = i * (N_DEV - 1) + k - 1
                cp = pltpu.make_async_remote_copy(src_ref=ins[i] if modes[i] == "gather" else ins[i].at[peer],
                                                  dst_ref=lands[i].at[peer], send_sem=send_ref.at[sem],
                                                  recv_sem=recv_ref.at[sem], device_id=(px, py, pc),
                                                  device_id_type=pl.DeviceIdType.MESH)
                cp.wait_send()
                cp.wait_recv()

    res = pl.pallas_call(
        body, name=name,
        out_shape=tuple(pltpu.HBM(a.shape, a.dtype) for a in sources) + tuple(pltpu.HBM(z.shape, z.dtype) for z in zones),
        in_specs=[_HBM_SPEC] * (2 * n) + [_SEM_SPEC] * 3 + [pl.BlockSpec(memory_space=pl.ANY)],
        out_specs=(_HBM_SPEC,) * (2 * n), input_output_aliases={i: i for i in range(2 * n)},
        compiler_params=pltpu.CompilerParams(has_side_effects=_SIDE_EFFECT),
    )(*sources, *zones, *sems, after)
    return list(res[n:])


ADAM_TILE = 64 * 1024


def _adam_call(w, m, v, slots, name):
    n_layers, rows, cols = w.shape
    tr = rows
    if n_layers * rows * cols > ADAM_TILE:
        fits = [t for t in range(SUBLANE, rows, SUBLANE) if rows % t == 0 and n_layers * t * cols <= ADAM_TILE]
        tr = max(fits) if fits else SUBLANE
    c1 = 1.0 - ADAM_B1 ** ADAM_STEP
    c2 = 1.0 - ADAM_B2 ** ADAM_STEP

    def body(*refs):
        w_ref, m_ref, v_ref = refs[:3]
        slot_refs = refs[3:3 + n_layers]
        grad_ref, delta_ref, nm_ref, nv_ref = refs[3 + n_layers:]
        for layer in range(n_layers):
            g = slot_refs[layer][0].astype(F32)
            for k in range(1, N_DEV):
                g = g + slot_refs[layer][k].astype(F32)
            m_new = ADAM_B1 * m_ref[layer] + (1.0 - ADAM_B1) * g
            v_new = ADAM_B2 * v_ref[layer] + (1.0 - ADAM_B2) * (g * g)
            m_hat = m_new / c1
            v_hat = v_new / c2
            grad_ref[layer] = g
            delta_ref[layer] = -ADAM_LR * (m_hat / (jnp.sqrt(v_hat) + ADAM_EPS) + ADAM_WD * w_ref[layer])
            nm_ref[layer] = m_new
            nv_ref[layer] = v_new

    spec = pl.BlockSpec((n_layers, tr, cols), lambda i: (0, i, 0))
    slot_spec = pl.BlockSpec((N_DEV, tr, cols), lambda i: (0, i, 0))
    return pl.pallas_call(
        body, grid=(rows // tr,), in_specs=[spec, spec, spec] + [slot_spec] * n_layers,
        out_specs=[spec] * 4, out_shape=[jax.ShapeDtypeStruct(w.shape, F32)] * 4,
        compiler_params=_cparams(("parallel",)), name=name,
    )(w, m, v, *slots)


WEIGHT_NAMES = ['ab_norm_g', 'ab_w_in', 'ab_w_out', 's5_a_re', 's5_a_im', 's5_log_dt', 's5_b_re', 's5_b_im', 's5_c_re',
                's5_c_im', 's5_d', 's5_w_glu', 's5_b_glu', 'gdn_conv_w', 'gdn_a_log', 'gdn_dt_bias', 'gdn_out_norm_g',
                'c_norm_g', 'c_w_qkv', 'c_w_out', 'c_q_norm_g', 'c_k_norm_g', 'c_rel_bias', 'mem_norm_g', 'xa_norm_g',
                'xa_w_q', 'xa_w_kv', 'xa_w_out', 'xa_q_norm_g', 'xa_k_norm_g', 'f_norm_g', 'f_w_gate', 'f_w_up',
                'f_w_down']

SHARDED = {
    'ab_w_in': ('col', BF16), 'ab_w_out': ('row', BF16), 's5_w_glu': ('row', BF16), 'gdn_conv_w': ('col', F32),
    'c_norm_g': ('col', F32), 'c_w_qkv': ('col', BF16), 'c_w_out': ('row', BF16), 'xa_w_q': ('row', BF16),
    'xa_w_kv': ('col', BF16), 'xa_w_out': ('row', BF16), 'f_w_gate': ('col', BF16), 'f_w_up': ('col', BF16),
    'f_w_down': ('row', BF16),
}
GATHERED_AS_IS = ('c_w_qkv', 'xa_w_kv', 'f_w_gate', 'f_w_up', 'f_w_down')
REPLICATED = [n for n in WEIGHT_NAMES if n not in SHARDED]
PACK_UNIT = SUBLANE * LANE


def _full_from_gathered(g, axis):
    if axis == "row":
        return g.reshape(g.shape[0] * g.shape[1], g.shape[2])
    return jnp.transpose(g, (1, 0, 2)).reshape(g.shape[1], g.shape[0] * g.shape[2])


def _pack(arrays):
    flat = []
    for a in arrays:
        size = a.size
        padded = -(-size // PACK_UNIT) * PACK_UNIT
        flat.append(jnp.pad(a.reshape(-1), (0, padded - size)).reshape(-1, LANE))
    return jnp.concatenate(flat, axis=0)


def _unpack(buf, shapes):
    out, row = [], 0
    for shape in shapes:
        size = math.prod(shape)
        rows = -(-size // PACK_UNIT) * SUBLANE
        out.append(buf[row:row + rows].reshape(-1)[:size].reshape(shape))
        row += rows
    return out


N_STAGES = 2 * DEPTH
EVEN_SHARDED = ['ab_w_in', 'ab_w_out', 's5_w_glu', 'gdn_conv_w']
ODD_SHARDED = ['c_norm_g', 'c_w_qkv', 'c_w_out']
ALL_SHARDED = ['xa_w_q', 'xa_w_kv', 'xa_w_out', 'f_w_gate', 'f_w_up', 'f_w_down']
EVEN_SMALL = ['ab_norm_g', 's5_a_re', 's5_a_im', 's5_log_dt', 's5_b_re', 's5_b_im', 's5_c_re', 's5_c_im', 's5_d',
              's5_b_glu', 'gdn_a_log', 'gdn_dt_bias', 'gdn_out_norm_g']
ODD_SMALL = ['c_q_norm_g', 'c_k_norm_g', 'c_rel_bias']
ALL_SMALL = ['xa_norm_g', 'xa_q_norm_g', 'xa_k_norm_g', 'f_norm_g']


def _stage_params(stage):
    layer, part = divmod(stage, 2)
    if part == 1:
        return [(n, layer) for n in ALL_SHARDED], [(n, layer) for n in ALL_SMALL]
    big, small = (EVEN_SHARDED, EVEN_SMALL) if layer % 2 == 0 else (ODD_SHARDED, ODD_SMALL)
    return [(n, layer // 2) for n in big], [(n, layer // 2) for n in small]


def _stage_forward(stage, landed, small, x, mem_n):
    layer, part = divmod(stage, 2)
    big = {}
    for (n, _), g in zip(_stage_params(stage)[0], landed):
        if n == 'ab_w_in':
            big[n] = cols_to_natural(g, AB_IN_PAD)
        elif n in GATHERED_AS_IS:
            big[n] = g
        elif n == 's5_w_glu':
            big[n] = _full_from_gathered(g, 'row').astype(F32)
        else:
            big[n] = _full_from_gathered(g, SHARDED[n][0])
    if part == 1:
        h, x = rmsnorm_res(x, small['xa_norm_g'], "xa_norm")
        x = memory_cross_attention(x, h, mem_n, big['xa_w_q'], big['xa_w_kv'], big['xa_w_out'],
                                   small['xa_q_norm_g'], small['xa_k_norm_g'])
        h, x = rmsnorm_res(x, small['f_norm_g'], "f_norm")
        return swiglu(x, h, big['f_w_gate'], big['f_w_up'], big['f_w_down'])
    if layer % 2 == 0:
        h, x = rmsnorm_res(x, small['ab_norm_g'], "ab_norm")
        w_in = big['ab_w_in']
        u = linear(h, w_in[:, :S5_WIDTH])
        qkv = linear(h, w_in[:, S5_WIDTH:S5_WIDTH + 3 * GDN_WIDTH])
        gate = linear(h, w_in[:, S5_WIDTH + 3 * GDN_WIDTH:S5_WIDTH + 4 * GDN_WIDTH])
        ab = linear(h, w_in[:, S5_WIDTH + 4 * GDN_WIDTH:])
        a_out = s5_mixer(u, small['s5_a_re'], small['s5_a_im'], small['s5_log_dt'], small['s5_b_re'], small['s5_b_im'],
                         small['s5_c_re'], small['s5_c_im'], small['s5_d'], big['s5_w_glu'], small['s5_b_glu'])
        b_out = gated_deltanet(qkv, gate, ab, big['gdn_conv_w'], small['gdn_a_log'], small['gdn_dt_bias'],
                               small['gdn_out_norm_g'])
        return linear_res(jnp.concatenate([a_out, b_out], axis=1), big['ab_w_out'], x)
    h, x = rmsnorm_res(x, big['c_norm_g'].reshape(-1), "c_norm")
    return chunk_attention(x, h, big['c_w_qkv'], big['c_w_out'], small['c_q_norm_g'], small['c_k_norm_g'],
                           small['c_rel_bias'])


def _loss_rows(x, target):
    return jnp.sum(make_rowop(_loss_fn, "loss")((x, target), ())[0])


def kernel(x, mem, ab_norm_g, ab_w_in, ab_w_out, s5_a_re, s5_a_im, s5_log_dt, s5_b_re, s5_b_im, s5_c_re, s5_c_im, s5_d, s5_w_glu, s5_b_glu, gdn_conv_w, gdn_a_log, gdn_dt_bias, gdn_out_norm_g, c_norm_g, c_w_qkv, c_w_out, c_q_norm_g, c_k_norm_g, c_rel_bias, mem_norm_g, xa_norm_g, xa_w_q, xa_w_kv, xa_w_out, xa_q_norm_g, xa_k_norm_g, f_norm_g, f_w_gate, f_w_up, f_w_down, loss_target, m_ab_norm_g, m_ab_w_in, m_ab_w_out, m_s5_a_re, m_s5_a_im, m_s5_log_dt, m_s5_b_re, m_s5_b_im, m_s5_c_re, m_s5_c_im, m_s5_d, m_s5_w_glu, m_s5_b_glu, m_gdn_conv_w, m_gdn_a_log, m_gdn_dt_bias, m_gdn_out_norm_g, m_c_norm_g, m_c_w_qkv, m_c_w_out, m_c_q_norm_g, m_c_k_norm_g, m_c_rel_bias, m_mem_norm_g, m_xa_norm_g, m_xa_w_q, m_xa_w_kv, m_xa_w_out, m_xa_q_norm_g, m_xa_k_norm_g, m_f_norm_g, m_f_w_gate, m_f_w_up, m_f_w_down, v_ab_norm_g, v_ab_w_in, v_ab_w_out, v_s5_a_re, v_s5_a_im, v_s5_log_dt, v_s5_b_re, v_s5_b_im, v_s5_c_re, v_s5_c_im, v_s5_d, v_s5_w_glu, v_s5_b_glu, v_gdn_conv_w, v_gdn_a_log, v_gdn_dt_bias, v_gdn_out_norm_g, v_c_norm_g, v_c_w_qkv, v_c_w_out, v_c_q_norm_g, v_c_k_norm_g, v_c_rel_bias, v_mem_norm_g, v_xa_norm_g, v_xa_w_q, v_xa_w_kv, v_xa_w_out, v_xa_q_norm_g, v_xa_k_norm_g, v_f_norm_g, v_f_w_gate, v_f_w_up, v_f_w_down):
    given = dict(locals())
    no_after = jnp.zeros((SUBLANE, LANE), F32)

    def shard(n, idx):
        a = given[n][idx]
        return (a.reshape(1, -1) if a.ndim == 1 else a).astype(SHARDED[n][1])

    def gather_start(stage, after, carry):
        arrays = [shard(n, idx) for n, idx in _stage_params(stage)[0]]
        return _exchange_start(arrays, ["gather"] * len(arrays), after, "gather_start_%d" % stage, carry)

    act = x[0]
    mem_n, mem_pullback = jax.vjp(lambda m, g: rmsnorm(m, g, "mem_norm"), mem[0], mem_norm_g)
    in_flight = {}
    for stage in range(2):
        in_flight[stage] = gather_start(stage, no_after, act)
        act = in_flight[stage][3]
    pullbacks = []
    for stage in range(N_STAGES):
        started = in_flight.pop(stage)
        landed = _exchange_wait(started, ["gather"] * len(started[1]), act, "gather_wait_%d" % stage)
        if stage + 2 < N_STAGES:
            in_flight[stage + 2] = gather_start(stage + 2, landed[0], act)
            act = in_flight[stage + 2][3]
        small = {n: given[n][idx] for n, idx in _stage_params(stage)[1]}
        act, pullback = jax.vjp(functools.partial(_stage_forward, stage), landed, small, act, mem_n)
        pullbacks.append(pullback)
    loss_local, loss_pullback = jax.vjp(_loss_rows, act, loss_target[0])
    d_act = loss_pullback(jnp.ones((), F32))[0]

    d_mem_n = jnp.zeros_like(mem_n)
    g_small = {}
    received = [None] * N_STAGES
    started, after = None, no_after
    for stage in reversed(range(N_STAGES)):
        d_landed, d_small, d_act, d_mem = pullbacks[stage](d_act)
        if stage % 2 == 1:
            d_mem_n = d_mem_n + d_mem
        for n, idx in _stage_params(stage)[1]:
            g_small[(n, idx)] = d_small[n]
        if started is not None:
            received[stage + 1] = _exchange_wait(started, ["scatter"] * len(started[1]), d_act,
                                                 "scatter_wait_%d" % (stage + 1))
            after = received[stage + 1][0]
        started = _exchange_start(list(d_landed), ["scatter"] * len(d_landed), after, "scatter_start_%d" % stage,
                                  d_act)
        d_act = started[3]
    g_small[('mem_norm_g', None)] = mem_pullback(d_mem_n)[1]

    def small_grad(n):
        if n == 'mem_norm_g':
            return g_small[(n, None)]
        return jnp.stack([g_small[(n, i)] for i in range(given[n].shape[0])], axis=0)

    packed = _exchange([_pack([small_grad(n) for n in REPLICATED])], ["gather"], "small_grads_allgather")[0]
    received[0] = _exchange_wait(started, ["scatter"] * len(started[1]), packed, "scatter_wait_0")

    results = {}
    for n in SHARDED:
        slots = {}
        for stage in range(N_STAGES):
            for (pn, idx), r in zip(_stage_params(stage)[0], received[stage]):
                if pn == n:
                    slots[idx] = r
        shape = given[n].shape
        to3d = lambda a: a.reshape(a.shape[0], -1, a.shape[-1])
        outs = _adam_call(to3d(given[n]), to3d(given['m_' + n]), to3d(given['v_' + n]),
                          [slots[i] for i in range(len(slots))], "adamw_" + n)
        results[n] = [o.reshape(shape) for o in outs]
    outs = _adam_call(_pack([given[n] for n in REPLICATED])[None], _pack([given['m_' + n] for n in REPLICATED])[None],
                      _pack([given['v_' + n] for n in REPLICATED])[None], [packed], "adamw_replicated")
    shapes = [given[n].shape for n in REPLICATED]
    for j, parts in enumerate(zip(*[_unpack(o[0], shapes) for o in outs])):
        results[REPLICATED[j]] = list(parts)

    loss = lax.psum(loss_local, ("x", "y", "c"))
    return (loss, d_act[None], *[results[n][0] for n in WEIGHT_NAMES], *[results[n][1] for n in WEIGHT_NAMES],
            *[results[n][2] for n in WEIGHT_NAMES], *[results[n][3] for n in WEIGHT_NAMES])
```

```python
import functools
import math

import jax
import jax.numpy as jnp
import numpy as np
from jax import lax
from jax.experimental import pallas as pl
from jax.experimental.pallas import tpu as pltpu

F32 = jnp.float32
BF16 = jnp.bfloat16
HI = lax.Precision.HIGHEST

N_DEV = 8
D_MODEL = 1024
SEQ = 2048
DEPTH = 4
CHUNK = 64
N_MEM = 256
RMS_EPS = 1e-6
S5_WIDTH = 512
S5_GROUP = 16
S5_GROUPS = 32
S5_STATE = 64
GDN_HEAD_DIM = 128
GDN_WIDTH = 512
GDN_HEADS = 4
GDN_CONV = 4
AB_IN = S5_WIDTH + 4 * GDN_WIDTH + 2 * GDN_HEADS
AB_IN_PAD = 2688
CA_HEADS = 16
CA_HEAD_DIM = 64
CA_LEFT = 8
CA_BAND = (CA_LEFT + 1) * CHUNK
CA_PAD = CA_LEFT * CHUNK
MAX_REL = 128
XA_HEADS = 4
XA_HEAD_DIM = 256
FFN = 2816
ADAM_LR, ADAM_B1, ADAM_B2, ADAM_EPS, ADAM_WD, ADAM_STEP = 0.001, 0.9, 0.999, 1e-08, 0.01, 10

VMEM_LIMIT = 48 * 1024 * 1024
LANE = 128
SUBLANE = 8


def _cparams(sem=None):
    return pltpu.CompilerParams(dimension_semantics=sem, vmem_limit_bytes=VMEM_LIMIT)


def _divisor_tile(n, target, unit=LANE):
    if n <= target:
        return n
    best = None
    for t in range(unit, target + 1, unit):
        if n % t == 0:
            best = t
    assert best is not None, (n, target)
    return best


def _matmul(a, b, *, ta=False, tb=False, out_dtype=F32, name="mm", res=None):
    if ta:
        k_dim, m_dim = a.shape
    else:
        m_dim, k_dim = a.shape
    if tb:
        n_dim, kb = b.shape
    else:
        kb, n_dim = b.shape
    assert kb == k_dim, (a.shape, b.shape, ta, tb)
    tm = _divisor_tile(m_dim, 1024)
    tn = _divisor_tile(n_dim, 512)
    tk = _divisor_tile(k_dim, 1408)
    nk = k_dim // tk
    dims = (((0 if ta else 1,), (1 if tb else 0,)), ((), ()))

    def body(a_ref, b_ref, *rest):
        res_ref = rest[0] if res is not None else None
        o_ref, acc_ref = rest[-2:]
        k = pl.program_id(2)

        @pl.when(k == 0)
        def _():
            acc_ref[...] = jnp.zeros_like(acc_ref)

        acc_ref[...] += lax.dot_general(a_ref[...].astype(BF16), b_ref[...].astype(BF16), dims,
                                        preferred_element_type=F32)

        @pl.when(k == nk - 1)
        def _():
            total = acc_ref[...] if res is None else acc_ref[...] + res_ref[...]
            o_ref[...] = total.astype(o_ref.dtype)

    a_spec = pl.BlockSpec((tk, tm), lambda i, j, k: (k, i)) if ta else pl.BlockSpec((tm, tk), lambda i, j, k: (i, k))
    b_spec = pl.BlockSpec((tn, tk), lambda i, j, k: (j, k)) if tb else pl.BlockSpec((tk, tn), lambda i, j, k: (k, j))
    o_spec = pl.BlockSpec((tm, tn), lambda i, j, k: (i, j))
    return pl.pallas_call(
        body,
        grid=(m_dim // tm, n_dim // tn, nk),
        in_specs=[a_spec, b_spec] + ([o_spec] if res is not None else []),
        out_specs=o_spec,
        out_shape=jax.ShapeDtypeStruct((m_dim, n_dim), out_dtype),
        scratch_shapes=[pltpu.VMEM((tm, tn), F32)],
        compiler_params=_cparams(("parallel", "parallel", "arbitrary")),
        name=name,
    )(*((a, b) if res is None else (a, b, res)))


@jax.custom_vjp
def linear(a, w):
    return _matmul(a, w, name="linear_fwd")


def _linear_fwd(a, w):
    return _matmul(a, w, name="linear_fwd"), (a, w)


def _linear_bwd(res, dy):
    a, w = res
    da = _matmul(dy, w, tb=True, name="linear_da")
    dw = _matmul(a, dy, ta=True, out_dtype=w.dtype, name="linear_dw")
    return da, dw


linear.defvjp(_linear_fwd, _linear_bwd)


@jax.custom_vjp
def linear_res(a, w, x):
    return _matmul(a, w, name="linear_res_fwd", res=x)


def _linear_res_fwd(a, w, x):
    return _matmul(a, w, name="linear_res_fwd", res=x), (a, w)


def _linear_res_bwd(res, dy):
    return _linear_bwd(res, dy) + (dy,)


linear_res.defvjp(_linear_res_fwd, _linear_res_bwd)


def _mm_call(name, a, b, out_struct, grid, a_spec, b_spec, o_spec, dims, lead, res=None, keep_a=None):
    nk = grid[-1]
    acc_shape = o_spec.block_shape[1:] if lead[2] else o_spec.block_shape
    a_shape = a_spec.block_shape[1:] if lead[0] else a_spec.block_shape
    assert keep_a is None or nk == 1

    def body(a_ref, b_ref, *rest):
        res_ref = rest[0] if res is not None else None
        o_ref, acc_ref = rest[-1 - (keep_a is not None) - 1], rest[-1 - (keep_a is not None)]
        k = pl.program_id(len(grid) - 1)

        @pl.when(k == 0)
        def _():
            acc_ref[...] = jnp.zeros_like(acc_ref)

        if keep_a is None:
            av = (a_ref[0] if lead[0] else a_ref[...]).astype(BF16)
        else:
            a16_ref = rest[-1]

            @pl.when(pl.program_id(keep_a) == 0)
            def _():
                a16_ref[...] = (a_ref[0] if lead[0] else a_ref[...]).astype(BF16)

            av = a16_ref[...]
        bv = b_ref[0] if lead[1] else b_ref[...]
        acc_ref[...] += lax.dot_general(av, bv.astype(BF16), dims, preferred_element_type=F32)

        @pl.when(k == nk - 1)
        def _():
            if lead[2]:
                o_ref[0] = acc_ref[...].astype(o_ref.dtype)
            elif res is not None:
                o_ref[...] = (acc_ref[...] + res_ref[...]).astype(o_ref.dtype)
            else:
                o_ref[...] = acc_ref[...].astype(o_ref.dtype)

    return pl.pallas_call(
        body, grid=grid, in_specs=[a_spec, b_spec] + ([o_spec] if res is not None else []), out_specs=o_spec,
        out_shape=out_struct,
        scratch_shapes=[pltpu.VMEM(tuple(acc_shape), F32)] + ([pltpu.VMEM(tuple(a_shape), BF16)] if keep_a is not None else []),
        compiler_params=_cparams(("parallel", "arbitrary", "arbitrary")), name=name,
    )(*((a, b) if res is None else (a, b, res)))


_NN = (((1,), (0,)), ((), ()))
_NT_DIMS = (((1,), (1,)), ((), ()))
_TN_DIMS = (((0,), (0,)), ((), ()))


def _cols_fwd(a, g, dm_out):
    m_dim, k_dim = a.shape
    _, _, c_dim = g.shape
    tm = _divisor_tile(m_dim, 1024)
    tk = _divisor_tile(k_dim, 1024)
    a_spec = pl.BlockSpec((tm, tk), lambda i, j, k: (i, k))
    b_spec = pl.BlockSpec((1, tk, c_dim), lambda i, j, k: (j, k, 0))
    if dm_out:
        o_spec = pl.BlockSpec((1, tm, c_dim), lambda i, j, k: (j, i, 0))
        out = jax.ShapeDtypeStruct((N_DEV, m_dim, c_dim), BF16)
    else:
        o_spec = pl.BlockSpec((tm, c_dim), lambda i, j, k: (i, j))
        out = jax.ShapeDtypeStruct((m_dim, N_DEV * c_dim), F32)
    return _mm_call("cols_fwd", a, g, out, (m_dim // tm, N_DEV, k_dim // tk), a_spec, b_spec, o_spec, _NN,
                    (False, True, dm_out), keep_a=1 if k_dim == tk else None)


def _cols_da(dy, g, dm_out):
    _, k_dim, c_dim = g.shape
    m_dim = dy.shape[1] if dm_out else dy.shape[0]
    tm = _divisor_tile(m_dim, 1024)
    tk = _divisor_tile(k_dim, 1024)
    if dm_out:
        a_spec = pl.BlockSpec((1, tm, c_dim), lambda i, kb, j: (j, i, 0))
    else:
        a_spec = pl.BlockSpec((tm, c_dim), lambda i, kb, j: (i, j))
    b_spec = pl.BlockSpec((1, tk, c_dim), lambda i, kb, j: (j, kb, 0))
    o_spec = pl.BlockSpec((tm, tk), lambda i, kb, j: (i, kb))
    return _mm_call("cols_da", dy, g, jax.ShapeDtypeStruct((m_dim, k_dim), F32), (m_dim // tm, k_dim // tk, N_DEV),
                    a_spec, b_spec, o_spec, _NT_DIMS, (dm_out, True, False))


def _cols_dg(a, dy, g, dm_out):
    _, k_dim, c_dim = g.shape
    m_dim = a.shape[0]
    tm = _divisor_tile(m_dim, 2048)
    tk = _divisor_tile(k_dim, 512)
    a_spec = pl.BlockSpec((tm, tk), lambda kb, j, m: (m, kb))
    if dm_out:
        b_spec = pl.BlockSpec((1, tm, c_dim), lambda kb, j, m: (j, m, 0))
    else:
        b_spec = pl.BlockSpec((tm, c_dim), lambda kb, j, m: (m, j))
    o_spec = pl.BlockSpec((1, tk, c_dim), lambda kb, j, m: (j, kb, 0))
    return _mm_call("cols_dg", a, dy, jax.ShapeDtypeStruct(g.shape, g.dtype), (k_dim // tk, N_DEV, m_dim // tm),
                    a_spec, b_spec, o_spec, _TN_DIMS, (False, dm_out, True), keep_a=1 if m_dim == tm else None)


def _make_linear_cols(dm_out):
    @jax.custom_vjp
    def op(a, g):
        return _cols_fwd(a, g, dm_out)

    def fwd(a, g):
        return _cols_fwd(a, g, dm_out), (a, g)

    def bwd(res, dy):
        a, g = res
        return _cols_da(dy, g, dm_out), _cols_dg(a, dy, g, dm_out)

    op.defvjp(fwd, bwd)
    return op


linear_cols = _make_linear_cols(False)
linear_cols_dm = _make_linear_cols(True)


def _silu_mul(g, u):
    return _silu(g) * u


def _ffn_down_fwd(g, u, w, x):
    _, m_dim, r_dim = g.shape
    n_dim = w.shape[2]
    tm = _divisor_tile(m_dim, 1024)
    tn = _divisor_tile(n_dim, 1024)
    nj = N_DEV

    def body(g_ref, u_ref, w_ref, x_ref, o_ref, acc_ref):
        j = pl.program_id(2)

        @pl.when(j == 0)
        def _():
            acc_ref[...] = jnp.zeros_like(acc_ref)

        acc_ref[...] += _bdot(_silu_mul(g_ref[0].astype(F32), u_ref[0].astype(F32)), w_ref[0])

        @pl.when(j == nj - 1)
        def _():
            o_ref[...] = acc_ref[...] + x_ref[...]

    h_spec = pl.BlockSpec((1, tm, r_dim), lambda i, n, j: (j, i, 0))
    o_spec = pl.BlockSpec((tm, tn), lambda i, n, j: (i, n))
    return pl.pallas_call(
        body, grid=(m_dim // tm, n_dim // tn, nj),
        in_specs=[h_spec, h_spec, pl.BlockSpec((1, r_dim, tn), lambda i, n, j: (j, 0, n)), o_spec], out_specs=o_spec,
        out_shape=jax.ShapeDtypeStruct((m_dim, n_dim), F32), scratch_shapes=[pltpu.VMEM((tm, tn), F32)],
        compiler_params=_cparams(("parallel", "parallel", "arbitrary")), name="ffn_down_fwd",
    )(g, u, w, x)


def _ffn_down_dh(dy, g, u, w):
    m_dim, n_dim = dy.shape
    r_dim = w.shape[1]
    tm = _divisor_tile(m_dim, 1024)
    tn = _divisor_tile(n_dim, 1024)
    nn = n_dim // tn

    def body(dy_ref, w_ref, g_ref, u_ref, dg_ref, du_ref, acc_ref):
        n = pl.program_id(2)

        @pl.when(n == 0)
        def _():
            acc_ref[...] = jnp.zeros_like(acc_ref)

        acc_ref[...] += _bdot(dy_ref[...], w_ref[0], _NT_DIMS)

        @pl.when(n == nn - 1)
        def _():
            _, pullback = jax.vjp(_silu_mul, g_ref[0].astype(F32), u_ref[0].astype(F32))
            dg, du = pullback(acc_ref[...])
            dg_ref[0] = dg.astype(dg_ref.dtype)
            du_ref[0] = du.astype(du_ref.dtype)

    h_spec = pl.BlockSpec((1, tm, r_dim), lambda i, j, n: (j, i, 0))
    return pl.pallas_call(
        body, grid=(m_dim // tm, N_DEV, nn),
        in_specs=[pl.BlockSpec((tm, tn), lambda i, j, n: (i, n)), pl.BlockSpec((1, r_dim, tn), lambda i, j, n: (j, 0, n)),
                  h_spec, h_spec],
        out_specs=[h_spec, h_spec], out_shape=[jax.ShapeDtypeStruct(g.shape, g.dtype)] * 2,
        scratch_shapes=[pltpu.VMEM((tm, r_dim), F32)],
        compiler_params=_cparams(("parallel", "arbitrary", "arbitrary")), name="ffn_down_dh",
    )(dy, w, g, u)


def _ffn_down_dw(g, u, dy, w):
    _, m_dim, r_dim = g.shape
    n_dim = dy.shape[1]
    tm = _divisor_tile(m_dim, 2048)
    tn = _divisor_tile(n_dim, 512)
    nm = m_dim // tm

    def body(g_ref, u_ref, dy_ref, o_ref, acc_ref):
        m = pl.program_id(2)

        @pl.when(m == 0)
        def _():
            acc_ref[...] = jnp.zeros_like(acc_ref)

        acc_ref[...] += _bdot(_silu_mul(g_ref[0].astype(F32), u_ref[0].astype(F32)), dy_ref[...], _TN_DIMS)

        @pl.when(m == nm - 1)
        def _():
            o_ref[0] = acc_ref[...].astype(o_ref.dtype)

    h_spec = pl.BlockSpec((1, tm, r_dim), lambda j, n, m: (j, m, 0))
    return pl.pallas_call(
        body, grid=(N_DEV, n_dim // tn, nm),
        in_specs=[h_spec, h_spec, pl.BlockSpec((tm, tn), lambda j, n, m: (m, n))],
        out_specs=pl.BlockSpec((1, r_dim, tn), lambda j, n, m: (j, 0, n)),
        out_shape=jax.ShapeDtypeStruct(w.shape, w.dtype), scratch_shapes=[pltpu.VMEM((r_dim, tn), F32)],
        compiler_params=_cparams(("parallel", "parallel", "arbitrary")), name="ffn_down_dw",
    )(g, u, dy)


@jax.custom_vjp
def ffn_down(g, u, w, x):
    return _ffn_down_fwd(g, u, w, x)


def _ffn_down_vjp_fwd(g, u, w, x):
    return _ffn_down_fwd(g, u, w, x), (g, u, w)


def _ffn_down_vjp_bwd(res, dy):
    g, u, w = res
    dg, du = _ffn_down_dh(dy, g, u, w)
    return dg, du, _ffn_down_dw(g, u, dy, w), dy


ffn_down.defvjp(_ffn_down_vjp_fwd, _ffn_down_vjp_bwd)


def _cols_to_natural_call(g, width):
    _, k_dim, c_dim = g.shape
    tk = _divisor_tile(k_dim, 256, SUBLANE)

    def body(g_ref, o_ref):
        for j in range(N_DEV):
            o_ref[:, j * c_dim:(j + 1) * c_dim] = g_ref[j]
        if width > N_DEV * c_dim:
            o_ref[:, N_DEV * c_dim:] = jnp.zeros((tk, width - N_DEV * c_dim), o_ref.dtype)

    return pl.pallas_call(
        body, grid=(k_dim // tk,), in_specs=[pl.BlockSpec((N_DEV, tk, c_dim), lambda i: (0, i, 0))],
        out_specs=pl.BlockSpec((tk, width), lambda i: (i, 0)), out_shape=jax.ShapeDtypeStruct((k_dim, width), g.dtype),
        compiler_params=_cparams(("parallel",)), name="cols_to_natural",
    )(g)


def _natural_to_cols_call(w, c_dim):
    k_dim, width = w.shape
    tk = _divisor_tile(k_dim, 256, SUBLANE)

    def body(w_ref, o_ref):
        for j in range(N_DEV):
            o_ref[j] = w_ref[:, j * c_dim:(j + 1) * c_dim]

    return pl.pallas_call(
        body, grid=(k_dim // tk,), in_specs=[pl.BlockSpec((tk, width), lambda i: (i, 0))],
        out_specs=pl.BlockSpec((N_DEV, tk, c_dim), lambda i: (0, i, 0)),
        out_shape=jax.ShapeDtypeStruct((N_DEV, k_dim, c_dim), w.dtype),
        compiler_params=_cparams(("parallel",)), name="natural_to_cols",
    )(w)


@functools.partial(jax.custom_vjp, nondiff_argnums=(1,))
def cols_to_natural(g, width):
    return _cols_to_natural_call(g, width)


def _cols_to_natural_fwd(g, width):
    return _cols_to_natural_call(g, width), g.shape[2]


def _cols_to_natural_bwd(width, c_dim, dw):
    return (_natural_to_cols_call(dw, c_dim),)


cols_to_natural.defvjp(_cols_to_natural_fwd, _cols_to_natural_bwd)


def make_rowop(fn, name, tm=256, passthrough=0):
    def specs(rows, params):
        row_specs = [pl.BlockSpec((tm, r.shape[1]), lambda i: (i, 0)) for r in rows]
        par_specs = [pl.BlockSpec(p.shape, lambda i: (0, 0)) for p in params]
        return row_specs, par_specs

    def out_structs(rows, params):
        tiles = [jax.ShapeDtypeStruct((tm, r.shape[1]), r.dtype) for r in rows]
        return jax.eval_shape(lambda r, p: fn(*r, *p), tiles, list(params))

    def fwd_call(rows, params):
        m_dim = rows[0].shape[0]
        n_in = len(rows) + len(params)
        outs = out_structs(rows, params)

        def body(*refs):
            res = fn(*[r[...] for r in refs[:n_in]])
            for o_ref, r in zip(refs[n_in:], res):
                o_ref[...] = r.astype(o_ref.dtype)

        row_specs, par_specs = specs(rows, params)
        return pl.pallas_call(
            body,
            grid=(m_dim // tm,),
            in_specs=row_specs + par_specs,
            out_specs=[pl.BlockSpec((tm, o.shape[1]), lambda i: (i, 0)) for o in outs],
            out_shape=[jax.ShapeDtypeStruct((m_dim, o.shape[1]), o.dtype) for o in outs],
            compiler_params=_cparams(("parallel",)),
            name=name + "_fwd",
        )(*rows, *params)

    def bwd_call(rows, params, cts):
        m_dim = rows[0].shape[0]
        n_rows, n_par = len(rows), len(params)
        n_in = n_rows + n_par
        n_ct = len(cts)
        n_fn = n_ct - passthrough

        def body(*refs):
            vals = [r[...] for r in refs[:n_in]]
            ct_vals = tuple(r[...] for r in refs[n_in:n_in + n_fn])
            pass_refs = refs[n_in + n_fn:n_in + n_ct]
            drow_refs = refs[n_in + n_ct:n_in + n_ct + n_rows]
            dpar_refs = refs[n_in + n_ct + n_rows:]
            _, pullback = jax.vjp(fn, *vals)
            grads = pullback(ct_vals)
            for i, (d_ref, g) in enumerate(zip(drow_refs, grads[:n_rows])):
                d_ref[...] = g + pass_refs[i][...] if i < passthrough else g

            @pl.when(pl.program_id(0) == 0)
            def _():
                for d_ref in dpar_refs:
                    d_ref[...] = jnp.zeros_like(d_ref)

            for d_ref, g in zip(dpar_refs, grads[n_rows:]):
                d_ref[...] += g

        row_specs, par_specs = specs(rows, params)
        ct_specs = [pl.BlockSpec((tm, c.shape[1]), lambda i: (i, 0)) for c in cts]
        res = pl.pallas_call(
            body,
            grid=(m_dim // tm,),
            in_specs=row_specs + par_specs + ct_specs,
            out_specs=row_specs + par_specs,
            out_shape=[jax.ShapeDtypeStruct(r.shape, r.dtype) for r in rows]
            + [jax.ShapeDtypeStruct(p.shape, p.dtype) for p in params],
            compiler_params=_cparams(("arbitrary",)),
            name=name + "_bwd",
        )(*rows, *params, *cts)
        return tuple(res[:n_rows]), tuple(res[n_rows:])

    @jax.custom_vjp
    def op(rows, params):
        return tuple(fwd_call(rows, params)) + tuple(rows[:passthrough])

    def op_fwd(rows, params):
        return tuple(fwd_call(rows, params)) + tuple(rows[:passthrough]), (rows, params)

    def op_bwd(res, cts):
        rows, params = res
        return bwd_call(rows, params, tuple(cts))

    op.defvjp(op_fwd, op_bwd)
    return op


def _rms(x, g):
    return x * lax.rsqrt(jnp.mean(x * x, axis=-1, keepdims=True) + RMS_EPS) * g


def _sigmoid(x):
    return 1.0 / (1.0 + jnp.exp(-x))


def _silu(x):
    return x * _sigmoid(x)


def _bdot(a, b, dims=(((1,), (0,)), ((), ()))):
    return lax.dot_general(a.astype(BF16), b.astype(BF16), dims, preferred_element_type=F32)


def _rmsnorm_fn(x, g):
    return (_rms(x, g),)


def rmsnorm(x, g, name):
    return make_rowop(_rmsnorm_fn, name)((x,), (g.reshape(1, -1),))[0]


def rmsnorm_res(x, g, name):
    return make_rowop(_rmsnorm_fn, name, passthrough=1)((x,), (g.reshape(1, -1),))


def _gelu_tanh(x):
    return 0.5 * x * (1.0 + jnp.tanh(0.7978845608028654 * (x + 0.044715 * x * x * x)))


def _softplus(x):
    return jnp.maximum(x, 0.0) + jnp.log(1.0 + jnp.exp(-jnp.abs(x)))


def _s5_post_fn(y, w_glu, b_glu):
    h = _gelu_tanh(y)
    return (h * _sigmoid(_bdot(h, w_glu) + b_glu),)


def _loss_fn(y, t):
    err = y - t
    return (0.5 * jnp.mean(err * err, axis=-1, keepdims=True),)


def _pair_headnorm(x, g2):
    lo = lax.broadcasted_iota(jnp.int32, x.shape, 1) < CA_HEAD_DIM
    sq = x * x
    s_lo = jnp.sum(jnp.where(lo, sq, 0.0), axis=-1, keepdims=True)
    s_hi = jnp.sum(jnp.where(lo, 0.0, sq), axis=-1, keepdims=True)
    ms = jnp.where(lo, s_lo, s_hi) * (1.0 / CA_HEAD_DIM)
    return x * lax.rsqrt(ms + RMS_EPS) * g2


def _ca_qknorm_fn(qkv, qg2, kg2):
    qs, ks = [], []
    for j in range(D_MODEL // LANE):
        qs.append(_pair_headnorm(qkv[:, j * LANE:(j + 1) * LANE], qg2))
        ks.append(_pair_headnorm(qkv[:, D_MODEL + j * LANE:D_MODEL + (j + 1) * LANE], kg2))
    return jnp.concatenate(qs, axis=1), jnp.concatenate(ks, axis=1)


def _xattn_fn(q, k, v, qg, kg):
    outs = []
    for h in range(XA_HEADS):
        sl = slice(h * XA_HEAD_DIM, (h + 1) * XA_HEAD_DIM)
        qh = _rms(q[:, sl], qg)
        kh = _rms(k[:, sl], kg)
        s = _bdot(qh, kh, (((1,), (1,)), ((), ()))) * (XA_HEAD_DIM ** -0.5)
        p = jnp.exp(s - jnp.max(s, axis=-1, keepdims=True))
        p = p / jnp.sum(p, axis=-1, keepdims=True)
        outs.append(_bdot(p, v[:, sl]))
    return (jnp.concatenate(outs, axis=1),)


def _gdn_prep_fn(x0, x1, x2, x3, ab, conv_w, alog, dtb):
    c = conv_w[3:4, :] * x0 + conv_w[2:3, :] * x1 + conv_w[1:2, :] * x2 + conv_w[0:1, :] * x3
    c = _silu(c)
    qs, ks = [], []
    for h in range(GDN_HEADS):
        qh = c[:, h * LANE:(h + 1) * LANE]
        kh = c[:, GDN_WIDTH + h * LANE:GDN_WIDTH + (h + 1) * LANE]
        qs.append(qh * lax.rsqrt(jnp.sum(qh * qh, axis=-1, keepdims=True) + RMS_EPS) * (GDN_HEAD_DIM ** -0.5))
        ks.append(kh * lax.rsqrt(jnp.sum(kh * kh, axis=-1, keepdims=True) + RMS_EPS))
    lane = lax.broadcasted_iota(jnp.int32, ab.shape, 1)
    g = -jnp.exp(alog) * _softplus(ab + dtb)
    beta = _sigmoid(ab)
    bg = jnp.where(lane < GDN_HEADS, g, jnp.where(lane < 2 * GDN_HEADS, beta, 0.0))
    return jnp.concatenate(qs, axis=1), jnp.concatenate(ks, axis=1), c[:, 2 * GDN_WIDTH:], bg


def _gdn_out_fn(o, gate, og):
    outs = []
    for h in range(GDN_HEADS):
        sl = slice(h * LANE, (h + 1) * LANE)
        outs.append(_rms(o[:, sl], og) * _silu(gate[:, sl]))
    return (jnp.concatenate(outs, axis=1),)


CA_QB = 4 * CHUNK
CA_KB = CA_QB + CA_PAD


def _ca_math(q2, kb2, vb2, bias2, c):
    lane = lax.broadcasted_iota(jnp.int32, q2.shape, 1)
    qc = lax.broadcasted_iota(jnp.int32, (CA_QB, CA_KB), 0) // CHUNK
    kc = lax.broadcasted_iota(jnp.int32, (CA_QB, CA_KB), 1) // CHUNK
    valid = (kc >= qc) & (kc <= qc + CA_LEFT) & (kc + c * (CA_QB // CHUNK) >= CA_LEFT)
    out = jnp.zeros(q2.shape, F32)
    for h in range(2):
        mine = (lane >= h * CA_HEAD_DIM) & (lane < (h + 1) * CA_HEAD_DIM)
        qh = jnp.where(mine, q2, 0.0)
        s = _bdot(qh, kb2, (((1,), (1,)), ((), ()))) * (CA_HEAD_DIM ** -0.5) + bias2[h]
        s = jnp.where(valid, s, -1e30)
        p = jnp.exp(s - jnp.max(s, axis=-1, keepdims=True))
        p = p / jnp.sum(p, axis=-1, keepdims=True)
        out = out + jnp.where(mine, _bdot(p, vb2), 0.0)
    return out


CA_VEC = CA_QB + CA_KB


def _ca_specs(seq):
    q_spec = pl.BlockSpec((CA_QB, LANE), lambda hp, c: (c, hp))
    kv_spec = pl.BlockSpec((seq + CA_PAD, LANE), lambda hp, c: (0, hp))
    b_spec = pl.BlockSpec((1, 2, CA_VEC), lambda hp, c: (hp, 0, 0))
    return (D_MODEL // LANE, seq // CA_QB), q_spec, kv_spec, b_spec


def _ca_bias_from_vector(vec_ref, bias_ref):
    for h in range(2):
        rows = jnp.broadcast_to(vec_ref[0, h:h + 1, :], (CA_QB, CA_VEC))
        bias_ref[h] = pltpu.roll(rows, 0, 1, stride=1, stride_axis=0)[:, CA_QB:]


def _ca_vector_grad(dbias):
    d = jnp.concatenate([jnp.zeros((CA_QB, CA_QB), F32), dbias], axis=1)
    row = lax.broadcasted_iota(jnp.int32, d.shape, 0)
    for bit in range(CA_QB.bit_length() - 1):
        d = jnp.where((row >> bit) & 1 == 1, pltpu.roll(d, CA_VEC - (1 << bit), 1), d)
    return jnp.sum(d, axis=0, keepdims=True)


def _ca_fwd_call(q, kpad, vpad, vec):
    grid, q_spec, kv_spec, b_spec = _ca_specs(q.shape[0])

    def body(q_ref, k_ref, v_ref, vec_ref, o_ref, bias_ref):
        c = pl.program_id(1)
        start = pl.multiple_of(c * CA_QB, CA_QB)

        @pl.when(c == 0)
        def _():
            _ca_bias_from_vector(vec_ref, bias_ref)

        o_ref[...] = _ca_math(q_ref[...], k_ref[pl.ds(start, CA_KB), :], v_ref[pl.ds(start, CA_KB), :],
                              bias_ref[...], c)

    return pl.pallas_call(
        body, grid=grid, in_specs=[q_spec, kv_spec, kv_spec, b_spec], out_specs=q_spec,
        out_shape=jax.ShapeDtypeStruct(q.shape, F32), scratch_shapes=[pltpu.VMEM((2, CA_QB, CA_KB), F32)],
        compiler_params=_cparams(("parallel", "arbitrary")), name="chunkattn_fwd",
    )(q, kpad, vpad, vec)


def _ca_bwd_call(q, kpad, vpad, vec, do):
    grid, q_spec, kv_spec, b_spec = _ca_specs(q.shape[0])
    last = grid[1] - 1

    def body(q_ref, k_ref, v_ref, vec_ref, do_ref, dq_ref, dk_ref, dv_ref, dvec_ref, bias_ref, dbias_ref):
        c = pl.program_id(1)
        start = pl.multiple_of(c * CA_QB, CA_QB)

        @pl.when(c == 0)
        def _():
            _ca_bias_from_vector(vec_ref, bias_ref)
            dk_ref[...] = jnp.zeros_like(dk_ref)
            dv_ref[...] = jnp.zeros_like(dv_ref)
            dbias_ref[...] = jnp.zeros_like(dbias_ref)

        _, pullback = jax.vjp(lambda a, b, d, e: _ca_math(a, b, d, e, c), q_ref[...],
                              k_ref[pl.ds(start, CA_KB), :], v_ref[pl.ds(start, CA_KB), :], bias_ref[...])
        dq, dkb, dvb, dbias = pullback(do_ref[...])
        dq_ref[...] = dq
        dk_ref[pl.ds(start, CA_KB), :] += dkb
        dv_ref[pl.ds(start, CA_KB), :] += dvb
        dbias_ref[...] += dbias

        @pl.when(c == last)
        def _():
            for h in range(2):
                dvec_ref[0, h:h + 1, :] = _ca_vector_grad(dbias_ref[h])

    return pl.pallas_call(
        body, grid=grid, in_specs=[q_spec, kv_spec, kv_spec, b_spec, q_spec],
        out_specs=[q_spec, kv_spec, kv_spec, b_spec],
        out_shape=[jax.ShapeDtypeStruct(q.shape, F32), jax.ShapeDtypeStruct(kpad.shape, F32),
                   jax.ShapeDtypeStruct(vpad.shape, F32), jax.ShapeDtypeStruct(vec.shape, F32)],
        scratch_shapes=[pltpu.VMEM((2, CA_QB, CA_KB), F32), pltpu.VMEM((2, CA_QB, CA_KB), F32)],
        compiler_params=_cparams(("parallel", "arbitrary")), name="chunkattn_bwd",
    )(q, kpad, vpad, vec, do)


@jax.custom_vjp
def chunk_attn_core(q, kpad, vpad, vec):
    return _ca_fwd_call(q, kpad, vpad, vec)


def _ca_core_fwd(q, kpad, vpad, vec):
    return _ca_fwd_call(q, kpad, vpad, vec), (q, kpad, vpad, vec)


def _ca_core_bwd(res, do):
    return tuple(_ca_bwd_call(*res, do))


chunk_attn_core.defvjp(_ca_core_fwd, _ca_core_bwd)


S5_GB = 4
S5_U = S5_WIDTH // S5_GB
S5_L = S5_GROUPS * S5_STATE // S5_GB


def _cmul(ar, ai, br, bi):
    return ar * br - ai * bi, ar * bi + ai * br


def _hdot(a, b, dims=(((1,), (0,)), ((), ()))):
    return lax.dot_general(a, b, dims, precision=HI, preferred_element_type=F32)


def _split_dot(a, b, dims=(((1,), (0,)), ((), ()))):
    a_hi, b_hi = a.astype(BF16), b.astype(BF16)
    a_lo = (a - a_hi.astype(F32)).astype(BF16)
    b_lo = (b - b_hi.astype(F32)).astype(BF16)
    dot = functools.partial(lax.dot_general, dimension_numbers=dims, preferred_element_type=F32)
    return dot(a_hi, b_hi) + (dot(a_hi, b_lo) + dot(a_lo, b_hi))


_NT = (((1,), (1,)), ((), ()))
_TN = (((0,), (0,)), ((), ()))


def _s5_tables(lr, li, reverse):
    p = {1: (lr, li)}
    p[2] = _cmul(*p[1], *p[1])
    p[4] = _cmul(*p[2], *p[2])
    p[3] = _cmul(*p[2], *p[1])
    p[5] = _cmul(*p[4], *p[1])
    p[6] = _cmul(*p[4], *p[2])
    p[7] = _cmul(*p[4], *p[3])
    p[8] = _cmul(*p[4], *p[4])
    row = lax.broadcasted_iota(jnp.int32, (SUBLANE, lr.shape[1]), 0)
    tr = jnp.zeros(row.shape, F32)
    ti = jnp.zeros(row.shape, F32)
    for i in range(SUBLANE):
        k = SUBLANE - i if reverse else i + 1
        tr = jnp.where(row == i, p[k][0], tr)
        ti = jnp.where(row == i, p[k][1], ti)
    return p, (tr, ti), row


def _s5_block_scan(xr, xi, p, tab, row, hr, hi, reverse):
    for k in (1, 2, 4):
        if reverse:
            sr = jnp.where(row < SUBLANE - k, pltpu.roll(xr, SUBLANE - k, 0), 0.0)
            si = jnp.where(row < SUBLANE - k, pltpu.roll(xi, SUBLANE - k, 0), 0.0)
        else:
            sr = jnp.where(row >= k, pltpu.roll(xr, k, 0), 0.0)
            si = jnp.where(row >= k, pltpu.roll(xi, k, 0), 0.0)
        ar, ai = _cmul(p[k][0], p[k][1], sr, si)
        xr, xi = xr + ar, xi + ai
    cr, ci = _cmul(tab[0], tab[1], hr, hi)
    return xr + cr, xi + ci


def _s5_forward_scan(sr_ref, si_ref, lr, li):
    n_blocks = sr_ref.shape[0] // SUBLANE
    p, tab, row = _s5_tables(lr, li, False)

    def step(b, carry):
        base = pl.multiple_of(b * SUBLANE, SUBLANE)
        xr, xi = _s5_block_scan(sr_ref[pl.ds(base, SUBLANE), :], si_ref[pl.ds(base, SUBLANE), :],
                                p, tab, row, carry[0], carry[1], False)
        sr_ref[pl.ds(base, SUBLANE), :] = xr
        si_ref[pl.ds(base, SUBLANE), :] = xi
        return xr[SUBLANE - 1:SUBLANE, :], xi[SUBLANE - 1:SUBLANE, :]

    zero = jnp.zeros((1, lr.shape[1]), F32)
    lax.fori_loop(0, n_blocks, step, (zero, zero))


def _s5_specs(seq):
    u_spec = pl.BlockSpec((seq, S5_U), lambda g: (0, g))
    bd_spec = pl.BlockSpec((1, S5_U, S5_L), lambda g: (g, 0, 0))
    cd_spec = pl.BlockSpec((1, S5_L, S5_U), lambda g: (g, 0, 0))
    lam_spec = pl.BlockSpec((1, 2, S5_L), lambda g: (g, 0, 0))
    d_spec = pl.BlockSpec((1, S5_U), lambda g: (0, g))
    return u_spec, bd_spec, cd_spec, lam_spec, d_spec


S5_ROWS = 256


def _row_chunks(seq, fn):
    rows_per = min(S5_ROWS, seq)

    def step(r, carry):
        fn(pl.ds(pl.multiple_of(r * rows_per, rows_per), rows_per))
        return carry

    lax.fori_loop(0, seq // rows_per, step, 0)


def _s5_fwd_call(u, bdr, bdi, cdr, cdi, lam, d):
    seq = u.shape[0]
    u_spec, bd_spec, cd_spec, lam_spec, d_spec = _s5_specs(seq)

    def body(u_ref, bdr_ref, bdi_ref, cdr_ref, cdi_ref, lam_ref, d_ref, y_ref, sr_ref, si_ref):
        def project_in(rows):
            uv = u_ref[rows, :]
            sr_ref[rows, :] = _bdot(uv, bdr_ref[0])
            si_ref[rows, :] = _bdot(uv, bdi_ref[0])

        def project_out(rows):
            y_ref[rows, :] = (_bdot(sr_ref[rows, :], cdr_ref[0]) - _bdot(si_ref[rows, :], cdi_ref[0])
                              + d_ref[...] * u_ref[rows, :])

        _row_chunks(seq, project_in)
        _s5_forward_scan(sr_ref, si_ref, lam_ref[0, 0:1, :], lam_ref[0, 1:2, :])
        _row_chunks(seq, project_out)

    return pl.pallas_call(
        body, grid=(S5_GB,), in_specs=[u_spec, bd_spec, bd_spec, cd_spec, cd_spec, lam_spec, d_spec],
        out_specs=u_spec, out_shape=jax.ShapeDtypeStruct(u.shape, F32),
        scratch_shapes=[pltpu.VMEM((seq, S5_L), F32), pltpu.VMEM((seq, S5_L), F32)],
        compiler_params=_cparams(("parallel",)), name="s5_fwd",
    )(u, bdr, bdi, cdr, cdi, lam, d)


def _s5_bwd_call(u, bdr, bdi, cdr, cdi, lam, d, dy):
    seq = u.shape[0]
    n_blocks = seq // SUBLANE
    u_spec, bd_spec, cd_spec, lam_spec, d_spec = _s5_specs(seq)

    def body(u_ref, bdr_ref, bdi_ref, cdr_ref, cdi_ref, lam_ref, d_ref, dy_ref,
             du_ref, dbdr_ref, dbdi_ref, dcdr_ref, dcdi_ref, dlam_ref, dd_ref, sr_ref, si_ref, gr_ref, gi_ref):
        lr, li = lam_ref[0, 0:1, :], lam_ref[0, 1:2, :]

        def project_in(rows):
            uv = u_ref[rows, :]
            dyv = dy_ref[rows, :]
            sr_ref[rows, :] = _bdot(uv, bdr_ref[0])
            si_ref[rows, :] = _bdot(uv, bdi_ref[0])
            gr_ref[rows, :] = _bdot(dyv, cdr_ref[0], _NT)
            gi_ref[rows, :] = -_bdot(dyv, cdi_ref[0], _NT)

        _row_chunks(seq, project_in)
        _s5_forward_scan(sr_ref, si_ref, lr, li)
        p, tab, row = _s5_tables(lr, -li, True)

        def step(i, carry):
            hr, hi, acc_r, acc_i = carry
            b = n_blocks - 1 - i
            base = pl.multiple_of(b * SUBLANE, SUBLANE)
            xr, xi = _s5_block_scan(gr_ref[pl.ds(base, SUBLANE), :], gi_ref[pl.ds(base, SUBLANE), :],
                                    p, tab, row, hr, hi, True)
            gr_ref[pl.ds(base, SUBLANE), :] = xr
            gi_ref[pl.ds(base, SUBLANE), :] = xi
            prev = pl.multiple_of(jnp.maximum(b - 1, 0) * SUBLANE, SUBLANE)
            keep = (b > 0).astype(F32)
            last_r = sr_ref[pl.ds(prev, SUBLANE), :][SUBLANE - 1:SUBLANE, :] * keep
            last_i = si_ref[pl.ds(prev, SUBLANE), :][SUBLANE - 1:SUBLANE, :] * keep
            pr = jnp.where(row >= 1, pltpu.roll(sr_ref[pl.ds(base, SUBLANE), :], 1, 0), last_r)
            pi = jnp.where(row >= 1, pltpu.roll(si_ref[pl.ds(base, SUBLANE), :], 1, 0), last_i)
            acc_r = acc_r + pr * xr + pi * xi
            acc_i = acc_i + pr * xi - pi * xr
            return xr[0:1, :], xi[0:1, :], acc_r, acc_i

        zero = jnp.zeros((1, S5_L), F32)
        zacc = jnp.zeros((SUBLANE, S5_L), F32)
        _, _, acc_r, acc_i = lax.fori_loop(0, n_blocks, step, (zero, zero, zacc, zacc))
        dlam_ref[0, 0:1, :] = jnp.sum(acc_r, axis=0, keepdims=True)
        dlam_ref[0, 1:2, :] = jnp.sum(acc_i, axis=0, keepdims=True)
        for ref in (dbdr_ref, dbdi_ref, dcdr_ref, dcdi_ref, dd_ref):
            ref[...] = jnp.zeros_like(ref)

        def grads(rows):
            uv, dyv = u_ref[rows, :], dy_ref[rows, :]
            grv, giv = gr_ref[rows, :], gi_ref[rows, :]
            du_ref[rows, :] = _bdot(grv, bdr_ref[0], _NT) + _bdot(giv, bdi_ref[0], _NT) + d_ref[...] * dyv
            dbdr_ref[0] += _bdot(uv, grv, _TN)
            dbdi_ref[0] += _bdot(uv, giv, _TN)
            dcdr_ref[0] += _bdot(sr_ref[rows, :], dyv, _TN)
            dcdi_ref[0] -= _bdot(si_ref[rows, :], dyv, _TN)
            dd_ref[...] += jnp.sum(dyv * uv, axis=0, keepdims=True)

        _row_chunks(seq, grads)

    scratch = [pltpu.VMEM((seq, S5_L), F32) for _ in range(4)]
    return pl.pallas_call(
        body, grid=(S5_GB,),
        in_specs=[u_spec, bd_spec, bd_spec, cd_spec, cd_spec, lam_spec, d_spec, u_spec],
        out_specs=[u_spec, bd_spec, bd_spec, cd_spec, cd_spec, lam_spec, d_spec],
        out_shape=[jax.ShapeDtypeStruct(a.shape, F32) for a in (u, bdr, bdi, cdr, cdi, lam, d)],
        scratch_shapes=scratch, compiler_params=_cparams(("parallel",)), name="s5_bwd",
    )(u, bdr, bdi, cdr, cdi, lam, d, dy)


@jax.custom_vjp
def s5_core(u, bdr, bdi, cdr, cdi, lam, d):
    return _s5_fwd_call(u, bdr, bdi, cdr, cdi, lam, d)


def _s5_core_fwd(*args):
    return _s5_fwd_call(*args), args


def _s5_core_bwd(res, dy):
    return tuple(_s5_bwd_call(*res, dy))


s5_core.defvjp(_s5_core_fwd, _s5_core_bwd)


def _s5_discretize(a_re, a_im, log_dt, b_re, b_im, c_re, c_im, d):
    dt = jnp.exp(log_dt)[:, None]
    mag = jnp.exp(a_re * dt)
    lbr, lbi = mag * jnp.cos(a_im * dt), mag * jnp.sin(a_im * dt)
    den = a_re * a_re + a_im * a_im
    fr = ((lbr - 1.0) * a_re + lbi * a_im) / den
    fi = (lbi * a_re - (lbr - 1.0) * a_im) / den
    bbr = fr[:, :, None] * b_re - fi[:, :, None] * b_im
    bbi = fr[:, :, None] * b_im + fi[:, :, None] * b_re
    eye = jnp.eye(S5_GROUPS // S5_GB, dtype=F32)
    gl = S5_GROUPS // S5_GB

    def bd(t):
        return jnp.einsum('bgpc,gh->bgchp', t.reshape(S5_GB, gl, S5_STATE, S5_GROUP), eye).reshape(S5_GB, S5_U, S5_L)

    def cd(t):
        return jnp.einsum('bgcp,gh->bgphc', t.reshape(S5_GB, gl, S5_GROUP, S5_STATE), eye).reshape(S5_GB, S5_L, S5_U)

    lam = jnp.stack([lbr.reshape(S5_GB, S5_L), lbi.reshape(S5_GB, S5_L)], axis=1)
    return bd(bbr), bd(bbi), cd(c_re), cd(c_im), lam, d.reshape(1, S5_WIDTH)


@jax.custom_vjp
def _unit_lower_solve(neg_a, rhs, tinv):
    return _split_dot(tinv, rhs)


def _unit_lower_solve_fwd(neg_a, rhs, tinv):
    x = _split_dot(tinv, rhs)
    return x, (x, tinv)


def _unit_lower_solve_bwd(res, dx):
    x, tinv = res
    g = _split_dot(tinv, dx, _TN)
    return _split_dot(g, x, _NT), g, jnp.zeros_like(tinv)


_unit_lower_solve.defvjp(_unit_lower_solve_fwd, _unit_lower_solve_bwd)


def _unit_lower_inverse(neg_a):
    r = lax.broadcasted_iota(jnp.int32, neg_a.shape, 0)
    c = lax.broadcasted_iota(jnp.int32, neg_a.shape, 1)
    p = (r == c).astype(F32) + neg_a
    npow = _split_dot(neg_a, neg_a)
    for _ in range(4):
        y = _split_dot(jnp.concatenate([p, npow], axis=0), npow)
        p = p + y[:CHUNK]
        npow = y[CHUNK:]
    return p + _split_dot(p, npow)


def _gdn_chunk(q, k, v, g_col, b_col, st, tinv=None):
    r = lax.broadcasted_iota(jnp.int32, (CHUNK, CHUNK), 0)
    c = lax.broadcasted_iota(jnp.int32, (CHUNK, CHUNK), 1)
    eye = (r == c).astype(F32)
    strict = r > c
    causal = r >= c
    g_row = jnp.sum(g_col * eye, axis=0, keepdims=True)
    gcum = jnp.sum(jnp.where(causal, g_row, 0.0), axis=1, keepdims=True)
    gcum_row = jnp.sum(gcum * eye, axis=0, keepdims=True)
    diff = gcum - gcum_row
    decay_strict = jnp.where(strict, jnp.exp(jnp.where(strict, diff, 0.0)), 0.0)
    decay_causal = jnp.where(causal, jnp.exp(jnp.where(causal, diff, 0.0)), 0.0)
    gamma = jnp.exp(gcum)
    g_last = jnp.sum(jnp.where(lax.broadcasted_iota(jnp.int32, (CHUNK, 1), 0) == CHUNK - 1, gcum, 0.0),
                     axis=0, keepdims=True)
    kk = _bdot(k, k, _NT)
    neg_a = -(b_col * kk * decay_strict)
    if tinv is None:
        tinv = _unit_lower_inverse(neg_a)
    x = _unit_lower_solve(neg_a, jnp.concatenate([b_col * v, (b_col * gamma) * k], axis=1), lax.stop_gradient(tinv))
    u_new, w_k = x[:, :GDN_HEAD_DIM], x[:, GDN_HEAD_DIM:]
    qk = _bdot(q, k, _NT) * decay_causal
    q_g = q * gamma
    k_tail = k * jnp.exp(g_last - gcum)
    w = u_new - _bdot(w_k, st)
    o = _bdot(q_g, st) + _bdot(qk, w)
    st_new = jnp.exp(g_last) * st + _bdot(k_tail, w, _TN)
    return o, st_new, tinv


def _gdn_cols(bgv, h):
    lane = lax.broadcasted_iota(jnp.int32, bgv.shape, 1)
    g_col = jnp.sum(jnp.where(lane == h, bgv, 0.0), axis=1, keepdims=True)
    b_col = jnp.sum(jnp.where(lane == GDN_HEADS + h, bgv, 0.0), axis=1, keepdims=True)
    return g_col, b_col


GDN_CPS = 2


def _gdn_fwd_call(q, k, v, bg):
    seq = q.shape[0]
    n_chunks = seq // CHUNK
    n_steps = n_chunks // GDN_CPS
    rows = GDN_CPS * CHUNK
    x_spec = pl.BlockSpec((rows, GDN_WIDTH), lambda n: (n, 0))
    bg_spec = pl.BlockSpec((rows, LANE), lambda n: (n, 0))
    st_spec = pl.BlockSpec((GDN_CPS, GDN_WIDTH, GDN_HEAD_DIM), lambda n: (n, 0, 0))
    ti_spec = pl.BlockSpec((GDN_CPS, GDN_HEADS * CHUNK, CHUNK), lambda n: (n, 0, 0))

    def body(q_ref, k_ref, v_ref, bg_ref, o_ref, st_out_ref, ti_out_ref, st_ref):
        @pl.when(pl.program_id(0) == 0)
        def _():
            st_ref[...] = jnp.zeros_like(st_ref)

        for h in range(GDN_HEADS):
            sl = slice(h * GDN_HEAD_DIM, (h + 1) * GDN_HEAD_DIM)
            st = st_ref[sl, :]
            for cc in range(GDN_CPS):
                rs = slice(cc * CHUNK, (cc + 1) * CHUNK)
                g_col, b_col = _gdn_cols(bg_ref[rs, :], h)
                st_out_ref[cc, sl, :] = st
                o, st, tinv = _gdn_chunk(q_ref[rs, sl], k_ref[rs, sl], v_ref[rs, sl], g_col, b_col, st)
                o_ref[rs, sl] = o
                ti_out_ref[cc, h * CHUNK:(h + 1) * CHUNK, :] = tinv
            st_ref[sl, :] = st

    return pl.pallas_call(
        body, grid=(n_steps,), in_specs=[x_spec, x_spec, x_spec, bg_spec], out_specs=[x_spec, st_spec, ti_spec],
        out_shape=[jax.ShapeDtypeStruct(q.shape, F32),
                   jax.ShapeDtypeStruct((n_chunks, GDN_WIDTH, GDN_HEAD_DIM), F32),
                   jax.ShapeDtypeStruct((n_chunks, GDN_HEADS * CHUNK, CHUNK), F32)],
        scratch_shapes=[pltpu.VMEM((GDN_WIDTH, GDN_HEAD_DIM), F32)],
        compiler_params=_cparams(("arbitrary",)), name="gdn_fwd",
    )(q, k, v, bg)


def _gdn_bwd_call(q, k, v, bg, states, tinvs, do):
    seq = q.shape[0]
    n_steps = seq // CHUNK // GDN_CPS
    rows = GDN_CPS * CHUNK
    x_spec = pl.BlockSpec((rows, GDN_WIDTH), lambda i: (n_steps - 1 - i, 0))
    bg_spec = pl.BlockSpec((rows, LANE), lambda i: (n_steps - 1 - i, 0))
    st_spec = pl.BlockSpec((GDN_CPS, GDN_WIDTH, GDN_HEAD_DIM), lambda i: (n_steps - 1 - i, 0, 0))
    ti_spec = pl.BlockSpec((GDN_CPS, GDN_HEADS * CHUNK, CHUNK), lambda i: (n_steps - 1 - i, 0, 0))

    def body(q_ref, k_ref, v_ref, bg_ref, st_in_ref, ti_ref, do_ref, dq_ref, dk_ref, dv_ref, dbg_ref, dst_ref):
        @pl.when(pl.program_id(0) == 0)
        def _():
            dst_ref[...] = jnp.zeros_like(dst_ref)

        lane = lax.broadcasted_iota(jnp.int32, (CHUNK, LANE), 1)
        dbg = [jnp.zeros((CHUNK, LANE), F32) for _ in range(GDN_CPS)]
        for h in range(GDN_HEADS):
            sl = slice(h * GDN_HEAD_DIM, (h + 1) * GDN_HEAD_DIM)
            dst = dst_ref[sl, :]
            for cc in reversed(range(GDN_CPS)):
                rs = slice(cc * CHUNK, (cc + 1) * CHUNK)
                g_col, b_col = _gdn_cols(bg_ref[rs, :], h)
                tinv = ti_ref[cc, h * CHUNK:(h + 1) * CHUNK, :]
                _, pullback = jax.vjp(lambda *a: _gdn_chunk(*a, tinv=tinv)[:2], q_ref[rs, sl], k_ref[rs, sl],
                                      v_ref[rs, sl], g_col, b_col, st_in_ref[cc, sl, :])
                dq, dk, dv, dg, db, dst = pullback((do_ref[rs, sl], dst))
                dq_ref[rs, sl] = dq
                dk_ref[rs, sl] = dk
                dv_ref[rs, sl] = dv
                dbg[cc] = dbg[cc] + jnp.where(lane == h, dg, 0.0) + jnp.where(lane == GDN_HEADS + h, db, 0.0)
            dst_ref[sl, :] = dst
        for cc in range(GDN_CPS):
            dbg_ref[cc * CHUNK:(cc + 1) * CHUNK, :] = dbg[cc]

    return pl.pallas_call(
        body, grid=(n_steps,), in_specs=[x_spec, x_spec, x_spec, bg_spec, st_spec, ti_spec, x_spec],
        out_specs=[x_spec, x_spec, x_spec, bg_spec],
        out_shape=[jax.ShapeDtypeStruct(q.shape, F32)] * 3 + [jax.ShapeDtypeStruct(bg.shape, F32)],
        scratch_shapes=[pltpu.VMEM((GDN_WIDTH, GDN_HEAD_DIM), F32)],
        compiler_params=_cparams(("arbitrary",)), name="gdn_bwd",
    )(q, k, v, bg, states, tinvs, do)


@jax.custom_vjp
def gdn_core(q, k, v, bg):
    return _gdn_fwd_call(q, k, v, bg)[0]


def _gdn_core_fwd(q, k, v, bg):
    o, states, tinvs = _gdn_fwd_call(q, k, v, bg)
    return o, (q, k, v, bg, states, tinvs)


def _gdn_core_bwd(res, do):
    return tuple(_gdn_bwd_call(*res, do))


gdn_core.defvjp(_gdn_core_fwd, _gdn_core_bwd)


def _row(v):
    return v.reshape(1, -1)


def _lane_pad(v):
    return jnp.pad(v, (0, LANE - v.shape[0])).reshape(1, LANE)


def _delay_rows(x, k):
    return jnp.pad(x, ((k, 0), (0, 0)))[:x.shape[0]]


def s5_mixer(u, a_re, a_im, log_dt, b_re, b_im, c_re, c_im, d, w_glu, b_glu):
    y = s5_core(u, *_s5_discretize(a_re, a_im, log_dt, b_re, b_im, c_re, c_im, d))
    return make_rowop(_s5_post_fn, "s5_post")((y,), (w_glu, _row(b_glu)))[0]


def gated_deltanet(qkv, gate, ab, conv_w, a_log, dt_bias, out_g):
    rows = (qkv, _delay_rows(qkv, 1), _delay_rows(qkv, 2), _delay_rows(qkv, 3), ab)
    q, k, v, bg = make_rowop(_gdn_prep_fn, "gdn_prep")(rows, (conv_w, _lane_pad(a_log), _lane_pad(dt_bias)))
    o = gdn_core(q, k, v, bg)
    return make_rowop(_gdn_out_fn, "gdn_out")((o, gate), (_row(out_g),))[0]


def chunk_attention(x, h, w_qkv, w_out, q_g, k_g, rel_bias):
    qkv = linear_cols(h, w_qkv)
    qn, kn = make_rowop(_ca_qknorm_fn, "ca_qknorm")((qkv,), (_row(jnp.tile(q_g, 2)), _row(jnp.tile(k_g, 2))))
    kpad = jnp.pad(kn, ((CA_PAD, 0), (0, 0)))
    vpad = jnp.pad(qkv[:, 2 * D_MODEL:], ((CA_PAD, 0), (0, 0)))
    o = chunk_attn_core(qn, kpad, vpad, _rel_bias_vector(rel_bias))
    return linear_res(o, w_out, x)


def memory_cross_attention(x, h, mem_n, w_q, w_kv, w_out, q_g, k_g):
    q = linear(h, w_q)
    kv = linear_cols(mem_n, w_kv)
    o = make_rowop(_xattn_fn, "xattn")((q,), (kv[:, :D_MODEL], kv[:, D_MODEL:], _row(q_g), _row(k_g)))[0]
    return linear_res(o, w_out, x)


def swiglu(x, h, w_gate, w_up, w_down):
    return ffn_down(linear_cols_dm(h, w_gate), linear_cols_dm(h, w_up), w_down, x)


def _rel_bias_vector(rel_bias):
    heads = rel_bias.shape[0]
    n_far = CA_KB - 1 - MAX_REL
    n_neg = CA_VEC - 1 - n_far - (2 * MAX_REL + 1)
    vec = jnp.concatenate([jnp.zeros((heads, 1), F32),
                           jnp.broadcast_to(rel_bias[:, 2 * MAX_REL:], (heads, n_far)),
                           jnp.flip(rel_bias, axis=1),
                           jnp.broadcast_to(rel_bias[:, :1], (heads, n_neg))], axis=1)
    return vec.reshape(heads // 2, 2, CA_VEC)


def _exchange(arrays, modes, name):
    n = len(arrays)
    out_shapes = [jax.ShapeDtypeStruct((N_DEV,) + a.shape if m == "gather" else a.shape, a.dtype)
                  for a, m in zip(arrays, modes)]

    def body(*refs):
        ins, outs = refs[:n], refs[n:2 * n]
        send_sems, recv_sems, local_sems = refs[2 * n:]
        x, y, c = lax.axis_index("x"), lax.axis_index("y"), lax.axis_index("c")
        me = 4 * x + 2 * y + c
        pending = []
        for i in range(n):
            gather = modes[i] == "gather"
            local = pltpu.make_async_copy(ins[i] if gather else ins[i].at[me], outs[i].at[me], local_sems.at[i])
            local.start()
            pending.append(local)
        for k in range(1, N_DEV):
            px, py, pc = (x + (k >> 2)) % 2, (y + ((k >> 1) & 1)) % 2, (c + (k & 1)) % 2
            peer = 4 * px + 2 * py + pc
            for i in range(n):
                src = ins[i] if modes[i] == "gather" else ins[i].at[peer]
                sem = i * (N_DEV - 1) + k - 1
                send = pltpu.make_async_remote_copy(src_ref=src, dst_ref=outs[i].at[me], send_sem=send_sems.at[sem],
                                                    recv_sem=recv_sems.at[sem], device_id=(px, py, pc),
                                                    device_id_type=pl.DeviceIdType.MESH)
                send.start()
                arrival = pltpu.make_async_remote_copy(src_ref=src, dst_ref=outs[i].at[peer],
                                                       send_sem=send_sems.at[sem], recv_sem=recv_sems.at[sem],
                                                       device_id=(px, py, pc), device_id_type=pl.DeviceIdType.MESH)
                pending.append((send, arrival))
        for item in pending:
            if isinstance(item, tuple):
                item[0].wait_send()
                item[1].wait_recv()
            else:
                item.wait()

    any_spec = pl.BlockSpec(memory_space=pl.ANY)
    return pl.pallas_call(
        body, in_specs=[any_spec] * n, out_specs=[any_spec] * n, out_shape=out_shapes,
        scratch_shapes=[pltpu.SemaphoreType.DMA((n * (N_DEV - 1),)), pltpu.SemaphoreType.DMA((n * (N_DEV - 1),)),
                        pltpu.SemaphoreType.DMA((n,))],
        name=name,
    )(*arrays)


_HBM_SPEC = pl.BlockSpec(memory_space=pltpu.HBM)
_SEM_SPEC = pl.BlockSpec(memory_space=pltpu.SEMAPHORE)
_SIDE_EFFECT = pltpu.SideEffectType.DATAFLOW_SIDE_EFFECTING


def _peer(x, y, c, k):
    return (x + (k >> 2)) % 2, (y + ((k >> 1) & 1)) % 2, (c + (k & 1)) % 2


def _exchange_start(arrays, modes, after, name, carry):
    n = len(arrays)
    n_sem = n * (N_DEV - 1)
    lands = [pltpu.with_memory_space_constraint(lax.empty((N_DEV,) + a.shape if m == "gather" else a.shape, a.dtype),
                                                pltpu.HBM) for a, m in zip(arrays, modes)]
    arrays = [pltpu.with_memory_space_constraint(a, pltpu.HBM) for a in arrays]

    def body(*refs):
        ins, zones = refs[:n], refs[n:2 * n]
        send_sems, recv_sems, own_sems = refs[2 * n + 2:2 * n + 5]
        x, y, c = lax.axis_index("x"), lax.axis_index("y"), lax.axis_index("c")
        me = 4 * x + 2 * y + c
        for i in range(n):
            pltpu.make_async_copy(ins[i] if modes[i] == "gather" else ins[i].at[me], zones[i].at[me],
                                  own_sems.at[i]).start()
        for k in range(1, N_DEV):
            px, py, pc = _peer(x, y, c, k)
            peer = 4 * px + 2 * py + pc
            for i in range(n):
                sem = i * (N_DEV - 1) + k - 1
                pltpu.make_async_remote_copy(src_ref=ins[i] if modes[i] == "gather" else ins[i].at[peer],
                                             dst_ref=zones[i].at[me], send_sem=send_sems.at[sem],
                                             recv_sem=recv_sems.at[sem], device_id=(px, py, pc),
                                             device_id_type=pl.DeviceIdType.MESH).start()

    carry = pltpu.with_memory_space_constraint(carry, pltpu.HBM)
    out_shape = ((pltpu.SemaphoreType.DMA((n_sem,)), pltpu.SemaphoreType.DMA((n_sem,)), pltpu.SemaphoreType.DMA((n,)))
                 + tuple(pltpu.HBM(a.shape, a.dtype) for a in arrays) + tuple(pltpu.HBM(z.shape, z.dtype) for z in lands)
                 + (pltpu.HBM(carry.shape, carry.dtype),))
    aliases = {i: 3 + i for i in range(2 * n)}
    aliases[2 * n + 1] = 3 + 2 * n
    res = pl.pallas_call(
        body, name=name, out_shape=out_shape,
        in_specs=[_HBM_SPEC] * (2 * n) + [pl.BlockSpec(memory_space=pl.ANY), _HBM_SPEC],
        out_specs=(_SEM_SPEC,) * 3 + (_HBM_SPEC,) * (2 * n + 1),
        input_output_aliases=aliases,
        compiler_params=pltpu.CompilerParams(has_side_effects=_SIDE_EFFECT),
    )(*arrays, *lands, after, carry)
    return tuple(res[:3]), list(res[3:3 + n]), list(res[3 + n:3 + 2 * n]), res[3 + 2 * n]


def _exchange_wait(started, modes, after, name):
    sems, sources, zones, _ = started
    n = len(sources)

    def body(*refs):
        ins, lands = refs[:n], refs[n:2 * n]
        send_ref, recv_ref, own_ref = refs[2 * n:2 * n + 3]
        x, y, c = lax.axis_index("x"), lax.axis_index("y"), lax.axis_index("c")
        me = 4 * x + 2 * y + c
        for i in range(n):
            pltpu.make_async_copy(ins[i] if modes[i] == "gather" else ins[i].at[me], lands[i].at[me],
                                  own_ref.at[i]).wait()
        for k in range(1, N_DEV):
            px, py, pc = _peer(x, y, c, k)
            peer = 4 * px + 2 * py + pc
            for i in range(n):
                sem = i * (N_DEV - 1) + k - 1
                cp = pltpu.make_async_remote_copy(src_ref=ins[i] if modes[i] == "gather" else ins[i].at[peer],
                                                  dst_ref=lands[i].at[peer], send_sem=send_ref.at[sem],
                                                  recv_sem=recv_ref.at[sem], device_id=(px, py, pc),
                                                  device_id_type=pl.DeviceIdType.MESH)
                cp.wait_send()
                cp.wait_recv()

    res = pl.pallas_call(
        body, name=name,
        out_shape=tuple(pltpu.HBM(a.shape, a.dtype) for a in sources) + tuple(pltpu.HBM(z.shape, z.dtype) for z in zones),
        in_specs=[_HBM_SPEC] * (2 * n) + [_SEM_SPEC] * 3 + [pl.BlockSpec(memory_space=pl.ANY)],
        out_specs=(_HBM_SPEC,) * (2 * n), input_output_aliases={i: i for i in range(2 * n)},
        compiler_params=pltpu.CompilerParams(has_side_effects=_SIDE_EFFECT),
    )(*sources, *zones, *sems, after)
    return list(res[n:])


ADAM_TILE = 64 * 1024


def _adam_call(w, m, v, slots, name):
    n_layers, rows, cols = w.shape
    tr = rows
    if n_layers * rows * cols > ADAM_TILE:
        fits = [t for t in range(SUBLANE, rows, SUBLANE) if rows % t == 0 and n_layers * t * cols <= ADAM_TILE]
        tr = max(fits) if fits else SUBLANE
    c1 = 1.0 - ADAM_B1 ** ADAM_STEP
    c2 = 1.0 - ADAM_B2 ** ADAM_STEP

    def body(*refs):
        w_ref, m_ref, v_ref = refs[:3]
        slot_refs = refs[3:3 + n_layers]
        grad_ref, delta_ref, nm_ref, nv_ref = refs[3 + n_layers:]
        for layer in range(n_layers):
            g = slot_refs[layer][0].astype(F32)
            for k in range(1, N_DEV):
                g = g + slot_refs[layer][k].astype(F32)
            m_new = ADAM_B1 * m_ref[layer] + (1.0 - ADAM_B1) * g
            v_new = ADAM_B2 * v_ref[layer] + (1.0 - ADAM_B2) * (g * g)
            m_hat = m_new / c1
            v_hat = v_new / c2
            grad_ref[layer] = g
            delta_ref[layer] = -ADAM_LR * (m_hat / (jnp.sqrt(v_hat) + ADAM_EPS) + ADAM_WD * w_ref[layer])
            nm_ref[layer] = m_new
            nv_ref[layer] = v_new

    spec = pl.BlockSpec((n_layers, tr, cols), lambda i: (0, i, 0))
    slot_spec = pl.BlockSpec((N_DEV, tr, cols), lambda i: (0, i, 0))
    return pl.pallas_call(
        body, grid=(rows // tr,), in_specs=[spec, spec, spec] + [slot_spec] * n_layers,
        out_specs=[spec] * 4, out_shape=[jax.ShapeDtypeStruct(w.shape, F32)] * 4,
        compiler_params=_cparams(("parallel",)), name=name,
    )(w, m, v, *slots)


WEIGHT_NAMES = ['ab_norm_g', 'ab_w_in', 'ab_w_out', 's5_a_re', 's5_a_im', 's5_log_dt', 's5_b_re', 's5_b_im', 's5_c_re',
                's5_c_im', 's5_d', 's5_w_glu', 's5_b_glu', 'gdn_conv_w', 'gdn_a_log', 'gdn_dt_bias', 'gdn_out_norm_g',
                'c_norm_g', 'c_w_qkv', 'c_w_out', 'c_q_norm_g', 'c_k_norm_g', 'c_rel_bias', 'mem_norm_g', 'xa_norm_g',
                'xa_w_q', 'xa_w_kv', 'xa_w_out', 'xa_q_norm_g', 'xa_k_norm_g', 'f_norm_g', 'f_w_gate', 'f_w_up',
                'f_w_down']

SHARDED = {
    'ab_w_in': ('col', BF16), 'ab_w_out': ('row', BF16), 's5_w_glu': ('row', BF16), 'gdn_conv_w': ('col', F32),
    'c_norm_g': ('col', F32), 'c_w_qkv': ('col', BF16), 'c_w_out': ('row', BF16), 'xa_w_q': ('row', BF16),
    'xa_w_kv': ('col', BF16), 'xa_w_out': ('row', BF16), 'f_w_gate': ('col', BF16), 'f_w_up': ('col', BF16),
    'f_w_down': ('row', BF16),
}
GATHERED_AS_IS = ('c_w_qkv', 'xa_w_kv', 'f_w_gate', 'f_w_up', 'f_w_down')
REPLICATED = [n for n in WEIGHT_NAMES if n not in SHARDED]
PACK_UNIT = SUBLANE * LANE


def _full_from_gathered(g, axis):
    if axis == "row":
        return g.reshape(g.shape[0] * g.shape[1], g.shape[2])
    return jnp.transpose(g, (1, 0, 2)).reshape(g.shape[1], g.shape[0] * g.shape[2])


def _pack(arrays):
    flat = []
    for a in arrays:
        size = a.size
        padded = -(-size // PACK_UNIT) * PACK_UNIT
        flat.append(jnp.pad(a.reshape(-1), (0, padded - size)).reshape(-1, LANE))
    return jnp.concatenate(flat, axis=0)


def _unpack(buf, shapes):
    out, row = [], 0
    for shape in shapes:
        size = math.prod(shape)
        rows = -(-size // PACK_UNIT) * SUBLANE
        out.append(buf[row:row + rows].reshape(-1)[:size].reshape(shape))
        row += rows
    return out


N_STAGES = 2 * DEPTH
EVEN_SHARDED = ['ab_w_in', 'ab_w_out', 's5_w_glu', 'gdn_conv_w']
ODD_SHARDED = ['c_norm_g', 'c_w_qkv', 'c_w_out']
ALL_SHARDED = ['xa_w_q', 'xa_w_kv', 'xa_w_out', 'f_w_gate', 'f_w_up', 'f_w_down']
EVEN_SMALL = ['ab_norm_g', 's5_a_re', 's5_a_im', 's5_log_dt', 's5_b_re', 's5_b_im', 's5_c_re', 's5_c_im', 's5_d',
              's5_b_glu', 'gdn_a_log', 'gdn_dt_bias', 'gdn_out_norm_g']
ODD_SMALL = ['c_q_norm_g', 'c_k_norm_g', 'c_rel_bias']
ALL_SMALL = ['xa_norm_g', 'xa_q_norm_g', 'xa_k_norm_g', 'f_norm_g']


def _stage_params(stage):
    layer, part = divmod(stage, 2)
    if part == 1:
        return [(n, layer) for n in ALL_SHARDED], [(n, layer) for n in ALL_SMALL]
    big, small = (EVEN_SHARDED, EVEN_SMALL) if layer % 2 == 0 else (ODD_SHARDED, ODD_SMALL)
    return [(n, layer // 2) for n in big], [(n, layer // 2) for n in small]


def _stage_forward(stage, landed, small, x, mem_n):
    layer, part = divmod(stage, 2)
    big = {}
    for (n, _), g in zip(_stage_params(stage)[0], landed):
        if n == 'ab_w_in':
            big[n] = cols_to_natural(g, AB_IN_PAD)
        elif n in GATHERED_AS_IS:
            big[n] = g
        elif n == 's5_w_glu':
            big[n] = _full_from_gathered(g, 'row').astype(F32)
        else:
            big[n] = _full_from_gathered(g, SHARDED[n][0])
    if part == 1:
        h, x = rmsnorm_res(x, small['xa_norm_g'], "xa_norm")
        x = memory_cross_attention(x, h, mem_n, big['xa_w_q'], big['xa_w_kv'], big['xa_w_out'],
                                   small['xa_q_norm_g'], small['xa_k_norm_g'])
        h, x = rmsnorm_res(x, small['f_norm_g'], "f_norm")
        return swiglu(x, h, big['f_w_gate'], big['f_w_up'], big['f_w_down'])
    if layer % 2 == 0:
        h, x = rmsnorm_res(x, small['ab_norm_g'], "ab_norm")
        w_in = big['ab_w_in']
        u = linear(h, w_in[:, :S5_WIDTH])
        qkv = linear(h, w_in[:, S5_WIDTH:S5_WIDTH + 3 * GDN_WIDTH])
        gate = linear(h, w_in[:, S5_WIDTH + 3 * GDN_WIDTH:S5_WIDTH + 4 * GDN_WIDTH])
        ab = linear(h, w_in[:, S5_WIDTH + 4 * GDN_WIDTH:])
        a_out = s5_mixer(u, small['s5_a_re'], small['s5_a_im'], small['s5_log_dt'], small['s5_b_re'], small['s5_b_im'],
                         small['s5_c_re'], small['s5_c_im'], small['s5_d'], big['s5_w_glu'], small['s5_b_glu'])
        b_out = gated_deltanet(qkv, gate, ab, big['gdn_conv_w'], small['gdn_a_log'], small['gdn_dt_bias'],
                               small['gdn_out_norm_g'])
        return linear_res(jnp.concatenate([a_out, b_out], axis=1), big['ab_w_out'], x)
    h, x = rmsnorm_res(x, big['c_norm_g'].reshape(-1), "c_norm")
    return chunk_attention(x, h, big['c_w_qkv'], big['c_w_out'], small['c_q_norm_g'], small['c_k_norm_g'],
                           small['c_rel_bias'])


def _loss_rows(x, target):
    return jnp.sum(make_rowop(_loss_fn, "loss")((x, target), ())[0])


def kernel(x, mem, ab_norm_g, ab_w_in, ab_w_out, s5_a_re, s5_a_im, s5_log_dt, s5_b_re, s5_b_im, s5_c_re, s5_c_im, s5_d, s5_w_glu, s5_b_glu, gdn_conv_w, gdn_a_log, gdn_dt_bias, gdn_out_norm_g, c_norm_g, c_w_qkv, c_w_out, c_q_norm_g, c_k_norm_g, c_rel_bias, mem_norm_g, xa_norm_g, xa_w_q, xa_w_kv, xa_w_out, xa_q_norm_g, xa_k_norm_g, f_norm_g, f_w_gate, f_w_up, f_w_down, loss_target, m_ab_norm_g, m_ab_w_in, m_ab_w_out, m_s5_a_re, m_s5_a_im, m_s5_log_dt, m_s5_b_re, m_s5_b_im, m_s5_c_re, m_s5_c_im, m_s5_d, m_s5_w_glu, m_s5_b_glu, m_gdn_conv_w, m_gdn_a_log, m_gdn_dt_bias, m_gdn_out_norm_g, m_c_norm_g, m_c_w_qkv, m_c_w_out, m_c_q_norm_g, m_c_k_norm_g, m_c_rel_bias, m_mem_norm_g, m_xa_norm_g, m_xa_w_q, m_xa_w_kv, m_xa_w_out, m_xa_q_norm_g, m_xa_k_norm_g, m_f_norm_g, m_f_w_gate, m_f_w_up, m_f_w_down, v_ab_norm_g, v_ab_w_in, v_ab_w_out, v_s5_a_re, v_s5_a_im, v_s5_log_dt, v_s5_b_re, v_s5_b_im, v_s5_c_re, v_s5_c_im, v_s5_d, v_s5_w_glu, v_s5_b_glu, v_gdn_conv_w, v_gdn_a_log, v_gdn_dt_bias, v_gdn_out_norm_g, v_c_norm_g, v_c_w_qkv, v_c_w_out, v_c_q_norm_g, v_c_k_norm_g, v_c_rel_bias, v_mem_norm_g, v_xa_norm_g, v_xa_w_q, v_xa_w_kv, v_xa_w_out, v_xa_q_norm_g, v_xa_k_norm_g, v_f_norm_g, v_f_w_gate, v_f_w_up, v_f_w_down):
    given = dict(locals())
    no_after = jnp.zeros((SUBLANE, LANE), F32)

    def shard(n, idx):
        a = given[n][idx]
        return (a.reshape(1, -1) if a.ndim == 1 else a).astype(SHARDED[n][1])

    def gather_start(stage, after, carry):
        arrays = [shard(n, idx) for n, idx in _stage_params(stage)[0]]
        return _exchange_start(arrays, ["gather"] * len(arrays), after, "gather_start_%d" % stage, carry)

    act = x[0]
    mem_n, mem_pullback = jax.vjp(lambda m, g: rmsnorm(m, g, "mem_norm"), mem[0], mem_norm_g)
    in_flight = {}
    for stage in range(2):
        in_flight[stage] = gather_start(stage, no_after, act)
        act = in_flight[stage][3]
    pullbacks = []
    for stage in range(N_STAGES):
        started = in_flight.pop(stage)
        landed = _exchange_wait(started, ["gather"] * len(started[1]), act, "gather_wait_%d" % stage)
        if stage + 2 < N_STAGES:
            in_flight[stage + 2] = gather_start(stage + 2, landed[0], act)
            act = in_flight[stage + 2][3]
        small = {n: given[n][idx] for n, idx in _stage_params(stage)[1]}
        act, pullback = jax.vjp(functools.partial(_stage_forward, stage), landed, small, act, mem_n)
        pullbacks.append(pullback)
    loss_local, loss_pullback = jax.vjp(_loss_rows, act, loss_target[0])
    d_act = loss_pullback(jnp.ones((), F32))[0]

    d_mem_n = jnp.zeros_like(mem_n)
    g_small = {}
    received = [None] * N_STAGES
    started, after = None, no_after
    for stage in reversed(range(N_STAGES)):
        d_landed, d_small, d_act, d_mem = pullbacks[stage](d_act)
        if stage % 2 == 1:
            d_mem_n = d_mem_n + d_mem
        for n, idx in _stage_params(stage)[1]:
            g_small[(n, idx)] = d_small[n]
        if started is not None:
            received[stage + 1] = _exchange_wait(started, ["scatter"] * len(started[1]), d_act,
                                                 "scatter_wait_%d" % (stage + 1))
            after = received[stage + 1][0]
        started = _exchange_start(list(d_landed), ["scatter"] * len(d_landed), after, "scatter_start_%d" % stage,
                                  d_act)
        d_act = started[3]
    g_small[('mem_norm_g', None)] = mem_pullback(d_mem_n)[1]

    def small_grad(n):
        if n == 'mem_norm_g':
            return g_small[(n, None)]
        return jnp.stack([g_small[(n, i)] for i in range(given[n].shape[0])], axis=0)

    packed = _exchange([_pack([small_grad(n) for n in REPLICATED])], ["gather"], "small_grads_allgather")[0]
    received[0] = _exchange_wait(started, ["scatter"] * len(started[1]), packed, "scatter_wait_0")

    results = {}
    for n in SHARDED:
        slots = {}
        for stage in range(N_STAGES):
            for (pn, idx), r in zip(_stage_params(stage)[0], received[stage]):
                if pn == n:
                    slots[idx] = r
        shape = given[n].shape
        to3d = lambda a: a.reshape(a.shape[0], -1, a.shape[-1])
        outs = _adam_call(to3d(given[n]), to3d(given['m_' + n]), to3d(given['v_' + n]),
                          [slots[i] for i in range(len(slots))], "adamw_" + n)
        results[n] = [o.reshape(shape) for o in outs]
    outs = _adam_call(_pack([given[n] for n in REPLICATED])[None], _pack([given['m_' + n] for n in REPLICATED])[None],
                      _pack([given['v_' + n] for n in REPLICATED])[None], [packed], "adamw_replicated")
    shapes = [given[n].shape for n in REPLICATED]
    for j, parts in enumerate(zip(*[_unpack(o[0], shapes) for o in outs])):
        results[REPLICATED[j]] = list(parts)

    loss = lax.psum(loss_local, ("x", "y", "c"))
    return (loss, d_act[None], *[results[n][0] for n in WEIGHT_NAMES], *[results[n][1] for n in WEIGHT_NAMES],
            *[results[n][2] for n in WEIGHT_NAMES], *[results[n][3] for n in WEIGHT_NAMES])
```

```python
import functools
import math

import jax
import jax.numpy as jnp
import numpy as np
from jax import lax
from jax.experimental import pallas as pl
from jax.experimental.pallas import tpu as pltpu

F32 = jnp.float32
BF16 = jnp.bfloat16
HI = lax.Precision.HIGHEST

N_DEV = 8
D_MODEL = 1024
SEQ = 2048
DEPTH = 4
CHUNK = 64
N_MEM = 256
RMS_EPS = 1e-6
S5_WIDTH = 512
S5_GROUP = 16
S5_GROUPS = 32
S5_STATE = 64
GDN_HEAD_DIM = 128
GDN_WIDTH = 512
GDN_HEADS = 4
GDN_CONV = 4
AB_IN = S5_WIDTH + 4 * GDN_WIDTH + 2 * GDN_HEADS
AB_IN_PAD = 2688
CA_HEADS = 16
CA_HEAD_DIM = 64
CA_LEFT = 8
CA_BAND = (CA_LEFT + 1) * CHUNK
CA_PAD = CA_LEFT * CHUNK
MAX_REL = 128
XA_HEADS = 4
XA_HEAD_DIM = 256
FFN = 2816
ADAM_LR, ADAM_B1, ADAM_B2, ADAM_EPS, ADAM_WD, ADAM_STEP = 0.001, 0.9, 0.999, 1e-08, 0.01, 10

VMEM_LIMIT = 48 * 1024 * 1024
LANE = 128
SUBLANE = 8


def _cparams(sem=None):
    return pltpu.CompilerParams(dimension_semantics=sem, vmem_limit_bytes=VMEM_LIMIT)


def _divisor_tile(n, target, unit=LANE):
    if n <= target:
        return n
    best = None
    for t in range(unit, target + 1, unit):
        if n % t == 0:
            best = t
    assert best is not None, (n, target)
    return best


def _matmul(a, b, *, ta=False, tb=False, out_dtype=F32, name="mm", res=None):
    if ta:
        k_dim, m_dim = a.shape
    else:
        m_dim, k_dim = a.shape
    if tb:
        n_dim, kb = b.shape
    else:
        kb, n_dim = b.shape
    assert kb == k_dim, (a.shape, b.shape, ta, tb)
    tm = _divisor_tile(m_dim, 1024)
    tn = _divisor_tile(n_dim, 512)
    tk = _divisor_tile(k_dim, 1408)
    nk = k_dim // tk
    dims = (((0 if ta else 1,), (1 if tb else 0,)), ((), ()))

    def body(a_ref, b_ref, *rest):
        res_ref = rest[0] if res is not None else None
        o_ref, acc_ref = rest[-2:]
        k = pl.program_id(2)

        @pl.when(k == 0)
        def _():
            acc_ref[...] = jnp.zeros_like(acc_ref)

        acc_ref[...] += lax.dot_general(a_ref[...].astype(BF16), b_ref[...].astype(BF16), dims,
                                        preferred_element_type=F32)

        @pl.when(k == nk - 1)
        def _():
            total = acc_ref[...] if res is None else acc_ref[...] + res_ref[...]
            o_ref[...] = total.astype(o_ref.dtype)

    a_spec = pl.BlockSpec((tk, tm), lambda i, j, k: (k, i)) if ta else pl.BlockSpec((tm, tk), lambda i, j, k: (i, k))
    b_spec = pl.BlockSpec((tn, tk), lambda i, j, k: (j, k)) if tb else pl.BlockSpec((tk, tn), lambda i, j, k: (k, j))
    o_spec = pl.BlockSpec((tm, tn), lambda i, j, k: (i, j))
    return pl.pallas_call(
        body,
        grid=(m_dim // tm, n_dim // tn, nk),
        in_specs=[a_spec, b_spec] + ([o_spec] if res is not None else []),
        out_specs=o_spec,
        out_shape=jax.ShapeDtypeStruct((m_dim, n_dim), out_dtype),
        scratch_shapes=[pltpu.VMEM((tm, tn), F32)],
        compiler_params=_cparams(("parallel", "parallel", "arbitrary")),
        name=name,
    )(*((a, b) if res is None else (a, b, res)))


@jax.custom_vjp
def linear(a, w):
    return _matmul(a, w, name="linear_fwd")


def _linear_fwd(a, w):
    return _matmul(a, w, name="linear_fwd"), (a, w)


def _linear_bwd(res, dy):
    a, w = res
    da = _matmul(dy, w, tb=True, name="linear_da")
    dw = _matmul(a, dy, ta=True, out_dtype=w.dtype, name="linear_dw")
    return da, dw


linear.defvjp(_linear_fwd, _linear_bwd)


@jax.custom_vjp
def linear_res(a, w, x):
    return _matmul(a, w, name="linear_res_fwd", res=x)


def _linear_res_fwd(a, w, x):
    return _matmul(a, w, name="linear_res_fwd", res=x), (a, w)


def _linear_res_bwd(res, dy):
    return _linear_bwd(res, dy) + (dy,)


linear_res.defvjp(_linear_res_fwd, _linear_res_bwd)


def _mm_call(name, a, b, out_struct, grid, a_spec, b_spec, o_spec, dims, lead, res=None, keep_a=None):
    nk = grid[-1]
    acc_shape = o_spec.block_shape[1:] if lead[2] else o_spec.block_shape
    a_shape = a_spec.block_shape[1:] if lead[0] else a_spec.block_shape
    assert keep_a is None or nk == 1

    def body(a_ref, b_ref, *rest):
        res_ref = rest[0] if res is not None else None
        o_ref, acc_ref = rest[-1 - (keep_a is not None) - 1], rest[-1 - (keep_a is not None)]
        k = pl.program_id(len(grid) - 1)

        @pl.when(k == 0)
        def _():
            acc_ref[...] = jnp.zeros_like(acc_ref)

        if keep_a is None:
            av = (a_ref[0] if lead[0] else a_ref[...]).astype(BF16)
        else:
            a16_ref = rest[-1]

            @pl.when(pl.program_id(keep_a) == 0)
            def _():
                a16_ref[...] = (a_ref[0] if lead[0] else a_ref[...]).astype(BF16)

            av = a16_ref[...]
        bv = b_ref[0] if lead[1] else b_ref[...]
        acc_ref[...] += lax.dot_general(av, bv.astype(BF16), dims, preferred_element_type=F32)

        @pl.when(k == nk - 1)
        def _():
            if lead[2]:
                o_ref[0] = acc_ref[...].astype(o_ref.dtype)
            elif res is not None:
                o_ref[...] = (acc_ref[...] + res_ref[...]).astype(o_ref.dtype)
            else:
                o_ref[...] = acc_ref[...].astype(o_ref.dtype)

    return pl.pallas_call(
        body, grid=grid, in_specs=[a_spec, b_spec] + ([o_spec] if res is not None else []), out_specs=o_spec,
        out_shape=out_struct,
        scratch_shapes=[pltpu.VMEM(tuple(acc_shape), F32)] + ([pltpu.VMEM(tuple(a_shape), BF16)] if keep_a is not None else []),
        compiler_params=_cparams(("parallel", "arbitrary", "arbitrary")), name=name,
    )(*((a, b) if res is None else (a, b, res)))


_NN = (((1,), (0,)), ((), ()))
_NT_DIMS = (((1,), (1,)), ((), ()))
_TN_DIMS = (((0,), (0,)), ((), ()))


def _cols_fwd(a, g, dm_out):
    m_dim, k_dim = a.shape
    _, _, c_dim = g.shape
    tm = _divisor_tile(m_dim, 1024)
    tk = _divisor_tile(k_dim, 1024)
    a_spec = pl.BlockSpec((tm, tk), lambda i, j, k: (i, k))
    b_spec = pl.BlockSpec((1, tk, c_dim), lambda i, j, k: (j, k, 0))
    if dm_out:
        o_spec = pl.BlockSpec((1, tm, c_dim), lambda i, j, k: (j, i, 0))
        out = jax.ShapeDtypeStruct((N_DEV, m_dim, c_dim), BF16)
    else:
        o_spec = pl.BlockSpec((tm, c_dim), lambda i, j, k: (i, j))
        out = jax.ShapeDtypeStruct((m_dim, N_DEV * c_dim), F32)
    return _mm_call("cols_fwd", a, g, out, (m_dim // tm, N_DEV, k_dim // tk), a_spec, b_spec, o_spec, _NN,
                    (False, True, dm_out), keep_a=1 if k_dim == tk else None)


def _cols_da(dy, g, dm_out):
    _, k_dim, c_dim = g.shape
    m_dim = dy.shape[1] if dm_out else dy.shape[0]
    tm = _divisor_tile(m_dim, 1024)
    tk = _divisor_tile(k_dim, 1024)
    if dm_out:
        a_spec = pl.BlockSpec((1, tm, c_dim), lambda i, kb, j: (j, i, 0))
    else:
        a_spec = pl.BlockSpec((tm, c_dim), lambda i, kb, j: (i, j))
    b_spec = pl.BlockSpec((1, tk, c_dim), lambda i, kb, j: (j, kb, 0))
    o_spec = pl.BlockSpec((tm, tk), lambda i, kb, j: (i, kb))
    return _mm_call("cols_da", dy, g, jax.ShapeDtypeStruct((m_dim, k_dim), F32), (m_dim // tm, k_dim // tk, N_DEV),
                    a_spec, b_spec, o_spec, _NT_DIMS, (dm_out, True, False))


def _cols_dg(a, dy, g, dm_out):
    _, k_dim, c_dim = g.shape
    m_dim = a.shape[0]
    tm = _divisor_tile(m_dim, 2048)
    tk = _divisor_tile(k_dim, 512)
    a_spec = pl.BlockSpec((tm, tk), lambda kb, j, m: (m, kb))
    if dm_out:
        b_spec = pl.BlockSpec((1, tm, c_dim), lambda kb, j, m: (j, m, 0))
    else:
        b_spec = pl.BlockSpec((tm, c_dim), lambda kb, j, m: (m, j))
    o_spec = pl.BlockSpec((1, tk, c_dim), lambda kb, j, m: (j, kb, 0))
    return _mm_call("cols_dg", a, dy, jax.ShapeDtypeStruct(g.shape, g.dtype), (k_dim // tk, N_DEV, m_dim // tm),
                    a_spec, b_spec, o_spec, _TN_DIMS, (False, dm_out, True), keep_a=1 if m_dim == tm else None)


def _make_linear_cols(dm_out):
    @jax.custom_vjp
    def op(a, g):
        return _cols_fwd(a, g, dm_out)

    def fwd(a, g):
        return _cols_fwd(a, g, dm_out), (a, g)

    def bwd(res, dy):
        a, g = res
        return _cols_da(dy, g, dm_out), _cols_dg(a, dy, g, dm_out)

    op.defvjp(fwd, bwd)
    return op


linear_cols = _make_linear_cols(False)
linear_cols_dm = _make_linear_cols(True)


def _silu_mul(g, u):
    return _silu(g) * u


def _ffn_down_fwd(g, u, w, x):
    _, m_dim, r_dim = g.shape
    n_dim = w.shape[2]
    tm = _divisor_tile(m_dim, 1024)
    tn = _divisor_tile(n_dim, 1024)
    nj = N_DEV

    def body(g_ref, u_ref, w_ref, x_ref, o_ref, acc_ref):
        j = pl.program_id(2)

        @pl.when(j == 0)
        def _():
            acc_ref[...] = jnp.zeros_like(acc_ref)

        acc_ref[...] += _bdot(_silu_mul(g_ref[0].astype(F32), u_ref[0].astype(F32)), w_ref[0])

        @pl.when(j == nj - 1)
        def _():
            o_ref[...] = acc_ref[...] + x_ref[...]

    h_spec = pl.BlockSpec((1, tm, r_dim), lambda i, n, j: (j, i, 0))
    o_spec = pl.BlockSpec((tm, tn), lambda i, n, j: (i, n))
    return pl.pallas_call(
        body, grid=(m_dim // tm, n_dim // tn, nj),
        in_specs=[h_spec, h_spec, pl.BlockSpec((1, r_dim, tn), lambda i, n, j: (j, 0, n)), o_spec], out_specs=o_spec,
        out_shape=jax.ShapeDtypeStruct((m_dim, n_dim), F32), scratch_shapes=[pltpu.VMEM((tm, tn), F32)],
        compiler_params=_cparams(("parallel", "parallel", "arbitrary")), name="ffn_down_fwd",
    )(g, u, w, x)


def _ffn_down_dh(dy, g, u, w):
    m_dim, n_dim = dy.shape
    r_dim = w.shape[1]
    tm = _divisor_tile(m_dim, 1024)
    tn = _divisor_tile(n_dim, 1024)
    nn = n_dim // tn

    def body(dy_ref, w_ref, g_ref, u_ref, dg_ref, du_ref, acc_ref):
        n = pl.program_id(2)

        @pl.when(n == 0)
        def _():
            acc_ref[...] = jnp.zeros_like(acc_ref)

        acc_ref[...] += _bdot(dy_ref[...], w_ref[0], _NT_DIMS)

        @pl.when(n == nn - 1)
        def _():
            _, pullback = jax.vjp(_silu_mul, g_ref[0].astype(F32), u_ref[0].astype(F32))
            dg, du = pullback(acc_ref[...])
            dg_ref[0] = dg.astype(dg_ref.dtype)
            du_ref[0] = du.astype(du_ref.dtype)

    h_spec = pl.BlockSpec((1, tm, r_dim), lambda i, j, n: (j, i, 0))
    return pl.pallas_call(
        body, grid=(m_dim // tm, N_DEV, nn),
        in_specs=[pl.BlockSpec((tm, tn), lambda i, j, n: (i, n)), pl.BlockSpec((1, r_dim, tn), lambda i, j, n: (j, 0, n)),
                  h_spec, h_spec],
        out_specs=[h_spec, h_spec], out_shape=[jax.ShapeDtypeStruct(g.shape, g.dtype)] * 2,
        scratch_shapes=[pltpu.VMEM((tm, r_dim), F32)],
        compiler_params=_cparams(("parallel", "arbitrary", "arbitrary")), name="ffn_down_dh",
    )(dy, w, g, u)


def _ffn_down_dw(g, u, dy, w):
    _, m_dim, r_dim = g.shape
    n_dim = dy.shape[1]
    tm = _divisor_tile(m_dim, 2048)
    tn = _divisor_tile(n_dim, 512)
    nm = m_dim // tm

    def body(g_ref, u_ref, dy_ref, o_ref, acc_ref):
        m = pl.program_id(2)

        @pl.when(m == 0)
        def _():
            acc_ref[...] = jnp.zeros_like(acc_ref)

        acc_ref[...] += _bdot(_silu_mul(g_ref[0].astype(F32), u_ref[0].astype(F32)), dy_ref[...], _TN_DIMS)

        @pl.when(m == nm - 1)
        def _():
            o_ref[0] = acc_ref[...].astype(o_ref.dtype)

    h_spec = pl.BlockSpec((1, tm, r_dim), lambda j, n, m: (j, m, 0))
    return pl.pallas_call(
        body, grid=(N_DEV, n_dim // tn, nm),
        in_specs=[h_spec, h_spec, pl.BlockSpec((tm, tn), lambda j, n, m: (m, n))],
        out_specs=pl.BlockSpec((1, r_dim, tn), lambda j, n, m: (j, 0, n)),
        out_shape=jax.ShapeDtypeStruct(w.shape, w.dtype), scratch_shapes=[pltpu.VMEM((r_dim, tn), F32)],
        compiler_params=_cparams(("parallel", "parallel", "arbitrary")), name="ffn_down_dw",
    )(g, u, dy)


@jax.custom_vjp
def ffn_down(g, u, w, x):
    return _ffn_down_fwd(g, u, w, x)


def _ffn_down_vjp_fwd(g, u, w, x):
    return _ffn_down_fwd(g, u, w, x), (g, u, w)


def _ffn_down_vjp_bwd(res, dy):
    g, u, w = res
    dg, du = _ffn_down_dh(dy, g, u, w)
    return dg, du, _ffn_down_dw(g, u, dy, w), dy


ffn_down.defvjp(_ffn_down_vjp_fwd, _ffn_down_vjp_bwd)


def _cols_to_natural_call(g, width):
    _, k_dim, c_dim = g.shape
    tk = _divisor_tile(k_dim, 256, SUBLANE)

    def body(g_ref, o_ref):
        for j in range(N_DEV):
            o_ref[:, j * c_dim:(j + 1) * c_dim] = g_ref[j]
        if width > N_DEV * c_dim:
            o_ref[:, N_DEV * c_dim:] = jnp.zeros((tk, width - N_DEV * c_dim), o_ref.dtype)

    return pl.pallas_call(
        body, grid=(k_dim // tk,), in_specs=[pl.BlockSpec((N_DEV, tk, c_dim), lambda i: (0, i, 0))],
        out_specs=pl.BlockSpec((tk, width), lambda i: (i, 0)), out_shape=jax.ShapeDtypeStruct((k_dim, width), g.dtype),
        compiler_params=_cparams(("parallel",)), name="cols_to_natural",
    )(g)


def _natural_to_cols_call(w, c_dim):
    k_dim, width = w.shape
    tk = _divisor_tile(k_dim, 256, SUBLANE)

    def body(w_ref, o_ref):
        for j in range(N_DEV):
            o_ref[j] = w_ref[:, j * c_dim:(j + 1) * c_dim]

    return pl.pallas_call(
        body, grid=(k_dim // tk,), in_specs=[pl.BlockSpec((tk, width), lambda i: (i, 0))],
        out_specs=pl.BlockSpec((N_DEV, tk, c_dim), lambda i: (0, i, 0)),
        out_shape=jax.ShapeDtypeStruct((N_DEV, k_dim, c_dim), w.dtype),
        compiler_params=_cparams(("parallel",)), name="natural_to_cols",
    )(w)


@functools.partial(jax.custom_vjp, nondiff_argnums=(1,))
def cols_to_natural(g, width):
    return _cols_to_natural_call(g, width)


def _cols_to_natural_fwd(g, width):
    return _cols_to_natural_call(g, width), g.shape[2]


def _cols_to_natural_bwd(width, c_dim, dw):
    return (_natural_to_cols_call(dw, c_dim),)


cols_to_natural.defvjp(_cols_to_natural_fwd, _cols_to_natural_bwd)


def make_rowop(fn, name, tm=256, passthrough=0):
    def specs(rows, params):
        row_specs = [pl.BlockSpec((tm, r.shape[1]), lambda i: (i, 0)) for r in rows]
        par_specs = [pl.BlockSpec(p.shape, lambda i: (0, 0)) for p in params]
        return row_specs, par_specs

    def out_structs(rows, params):
        tiles = [jax.ShapeDtypeStruct((tm, r.shape[1]), r.dtype) for r in rows]
        return jax.eval_shape(lambda r, p: fn(*r, *p), tiles, list(params))

    def fwd_call(rows, params):
        m_dim = rows[0].shape[0]
        n_in = len(rows) + len(params)
        outs = out_structs(rows, params)

        def body(*refs):
            res = fn(*[r[...] for r in refs[:n_in]])
            for o_ref, r in zip(refs[n_in:], res):
                o_ref[...] = r.astype(o_ref.dtype)

        row_specs, par_specs = specs(rows, params)
        return pl.pallas_call(
            body,
            grid=(m_dim // tm,),
            in_specs=row_specs + par_specs,
            out_specs=[pl.BlockSpec((tm, o.shape[1]), lambda i: (i, 0)) for o in outs],
            out_shape=[jax.ShapeDtypeStruct((m_dim, o.shape[1]), o.dtype) for o in outs],
            compiler_params=_cparams(("parallel",)),
            name=name + "_fwd",
        )(*rows, *params)

    def bwd_call(rows, params, cts):
        m_dim = rows[0].shape[0]
        n_rows, n_par = len(rows), len(params)
        n_in = n_rows + n_par
        n_ct = len(cts)
        n_fn = n_ct - passthrough

        def body(*refs):
            vals = [r[...] for r in refs[:n_in]]
            ct_vals = tuple(r[...] for r in refs[n_in:n_in + n_fn])
            pass_refs = refs[n_in + n_fn:n_in + n_ct]
            drow_refs = refs[n_in + n_ct:n_in + n_ct + n_rows]
            dpar_refs = refs[n_in + n_ct + n_rows:]
            _, pullback = jax.vjp(fn, *vals)
            grads = pullback(ct_vals)
            for i, (d_ref, g) in enumerate(zip(drow_refs, grads[:n_rows])):
                d_ref[...] = g + pass_refs[i][...] if i < passthrough else g

            @pl.when(pl.program_id(0) == 0)
            def _():
                for d_ref in dpar_refs:
                    d_ref[...] = jnp.zeros_like(d_ref)

            for d_ref, g in zip(dpar_refs, grads[n_rows:]):
                d_ref[...] += g

        row_specs, par_specs = specs(rows, params)
        ct_specs = [pl.BlockSpec((tm, c.shape[1]), lambda i: (i, 0)) for c in cts]
        res = pl.pallas_call(
            body,
            grid=(m_dim // tm,),
            in_specs=row_specs + par_specs + ct_specs,
            out_specs=row_specs + par_specs,
            out_shape=[jax.ShapeDtypeStruct(r.shape, r.dtype) for r in rows]
            + [jax.ShapeDtypeStruct(p.shape, p.dtype) for p in params],
            compiler_params=_cparams(("arbitrary",)),
            name=name + "_bwd",
        )(*rows, *params, *cts)
        return tuple(res[:n_rows]), tuple(res[n_rows:])

    @jax.custom_vjp
    def op(rows, params):
        return tuple(fwd_call(rows, params)) + tuple(rows[:passthrough])

    def op_fwd(rows, params):
        return tuple(fwd_call(rows, params)) + tuple(rows[:passthrough]), (rows, params)

    def op_bwd(res, cts):
        rows, params = res
        return bwd_call(rows, params, tuple(cts))

    op.defvjp(op_fwd, op_bwd)
    return op


def _rms(x, g):
    return x * lax.rsqrt(jnp.mean(x * x, axis=-1, keepdims=True) + RMS_EPS) * g


def _sigmoid(x):
    return 1.0 / (1.0 + jnp.exp(-x))


def _silu(x):
    return x * _sigmoid(x)


def _bdot(a, b, dims=(((1,), (0,)), ((), ()))):
    return lax.dot_general(a.astype(BF16), b.astype(BF16), dims, preferred_element_type=F32)


def _rmsnorm_fn(x, g):
    return (_rms(x, g),)


def rmsnorm(x, g, name):
    return make_rowop(_rmsnorm_fn, name)((x,), (g.reshape(1, -1),))[0]


def rmsnorm_res(x, g, name):
    return make_rowop(_rmsnorm_fn, name, passthrough=1)((x,), (g.reshape(1, -1),))


def _gelu_tanh(x):
    return 0.5 * x * (1.0 + jnp.tanh(0.7978845608028654 * (x + 0.044715 * x * x * x)))


def _softplus(x):
    return jnp.maximum(x, 0.0) + jnp.log(1.0 + jnp.exp(-jnp.abs(x)))


def _s5_post_fn(y, w_glu, b_glu):
    h = _gelu_tanh(y)
    return (h * _sigmoid(_bdot(h, w_glu) + b_glu),)


def _loss_fn(y, t):
    err = y - t
    return (0.5 * jnp.mean(err * err, axis=-1, keepdims=True),)


def _pair_headnorm(x, g2):
    lo = lax.broadcasted_iota(jnp.int32, x.shape, 1) < CA_HEAD_DIM
    sq = x * x
    s_lo = jnp.sum(jnp.where(lo, sq, 0.0), axis=-1, keepdims=True)
    s_hi = jnp.sum(jnp.where(lo, 0.0, sq), axis=-1, keepdims=True)
    ms = jnp.where(lo, s_lo, s_hi) * (1.0 / CA_HEAD_DIM)
    return x * lax.rsqrt(ms + RMS_EPS) * g2


def _ca_qknorm_fn(qkv, qg2, kg2):
    qs, ks = [], []
    for j in range(D_MODEL // LANE):
        qs.append(_pair_headnorm(qkv[:, j * LANE:(j + 1) * LANE], qg2))
        ks.append(_pair_headnorm(qkv[:, D_MODEL + j * LANE:D_MODEL + (j + 1) * LANE], kg2))
    return jnp.concatenate(qs, axis=1), jnp.concatenate(ks, axis=1)


def _xattn_fn(q, k, v, qg, kg):
    outs = []
    for h in range(XA_HEADS):
        sl = slice(h * XA_HEAD_DIM, (h + 1) * XA_HEAD_DIM)
        qh = _rms(q[:, sl], qg)
        kh = _rms(k[:, sl], kg)
        s = _bdot(qh, kh, (((1,), (1,)), ((), ()))) * (XA_HEAD_DIM ** -0.5)
        p = jnp.exp(s - jnp.max(s, axis=-1, keepdims=True))
        p = p / jnp.sum(p, axis=-1, keepdims=True)
        outs.append(_bdot(p, v[:, sl]))
    return (jnp.concatenate(outs, axis=1),)


def _gdn_prep_fn(x0, x1, x2, x3, ab, conv_w, alog, dtb):
    c = conv_w[3:4, :] * x0 + conv_w[2:3, :] * x1 + conv_w[1:2, :] * x2 + conv_w[0:1, :] * x3
    c = _silu(c)
    qs, ks = [], []
    for h in range(GDN_HEADS):
        qh = c[:, h * LANE:(h + 1) * LANE]
        kh = c[:, GDN_WIDTH + h * LANE:GDN_WIDTH + (h + 1) * LANE]
        qs.append(qh * lax.rsqrt(jnp.sum(qh * qh, axis=-1, keepdims=True) + RMS_EPS) * (GDN_HEAD_DIM ** -0.5))
        ks.append(kh * lax.rsqrt(jnp.sum(kh * kh, axis=-1, keepdims=True) + RMS_EPS))
    lane = lax.broadcasted_iota(jnp.int32, ab.shape, 1)
    g = -jnp.exp(alog) * _softplus(ab + dtb)
    beta = _sigmoid(ab)
    bg = jnp.where(lane < GDN_HEADS, g, jnp.where(lane < 2 * GDN_HEADS, beta, 0.0))
    return jnp.concatenate(qs, axis=1), jnp.concatenate(ks, axis=1), c[:, 2 * GDN_WIDTH:], bg


def _gdn_out_fn(o, gate, og):
    outs = []
    for h in range(GDN_HEADS):
        sl = slice(h * LANE, (h + 1) * LANE)
        outs.append(_rms(o[:, sl], og) * _silu(gate[:, sl]))
    return (jnp.concatenate(outs, axis=1),)


CA_QB = 4 * CHUNK
CA_KB = CA_QB + CA_PAD


def _ca_math(q2, kb2, vb2, bias2, c):
    lane = lax.broadcasted_iota(jnp.int32, (CHUNK, LANE), 1)
    kpos = lax.broadcasted_iota(jnp.int32, (CHUNK, CA_BAND), 1)
    outs = []
    for a in range(CA_QB // CHUNK):
        qa = q2[a * CHUNK:(a + 1) * CHUNK]
        ka = kb2[a * CHUNK:a * CHUNK + CA_BAND]
        va = vb2[a * CHUNK:a * CHUNK + CA_BAND]
        valid = kpos + (c * (CA_QB // CHUNK) + a) * CHUNK >= CA_PAD
        out = jnp.zeros((CHUNK, LANE), F32)
        for h in range(2):
            mine = (lane >= h * CA_HEAD_DIM) & (lane < (h + 1) * CA_HEAD_DIM)
            qh = jnp.where(mine, qa, 0.0)
            s = _bdot(qh, ka, (((1,), (1,)), ((), ()))) * (CA_HEAD_DIM ** -0.5) + bias2[h]
            s = jnp.where(valid, s, -1e30)
            p = jnp.exp(s - jnp.max(s, axis=-1, keepdims=True))
            p = p / jnp.sum(p, axis=-1, keepdims=True)
            out = out + jnp.where(mine, _bdot(p, va), 0.0)
        outs.append(out)
    return jnp.concatenate(outs, axis=0)


CA_VEC = CHUNK + CA_BAND


def _ca_specs(seq):
    q_spec = pl.BlockSpec((CA_QB, LANE), lambda hp, c: (c, hp))
    kv_spec = pl.BlockSpec((seq + CA_PAD, LANE), lambda hp, c: (0, hp))
    b_spec = pl.BlockSpec((1, 2, CA_VEC), lambda hp, c: (hp, 0, 0))
    return (D_MODEL // LANE, seq // CA_QB), q_spec, kv_spec, b_spec


def _ca_bias_from_vector(vec_ref, bias_ref):
    for h in range(2):
        rows = jnp.broadcast_to(vec_ref[0, h:h + 1, :], (CHUNK, CA_VEC))
        bias_ref[h] = pltpu.roll(rows, 0, 1, stride=1, stride_axis=0)[:, CHUNK:]


def _ca_vector_grad(dbias):
    d = jnp.concatenate([jnp.zeros((CHUNK, CHUNK), F32), dbias], axis=1)
    row = lax.broadcasted_iota(jnp.int32, d.shape, 0)
    for bit in range(CHUNK.bit_length() - 1):
        d = jnp.where((row >> bit) & 1 == 1, pltpu.roll(d, CA_VEC - (1 << bit), 1), d)
    return jnp.sum(d, axis=0, keepdims=True)


def _ca_fwd_call(q, kpad, vpad, vec):
    grid, q_spec, kv_spec, b_spec = _ca_specs(q.shape[0])

    def body(q_ref, k_ref, v_ref, vec_ref, o_ref, bias_ref):
        c = pl.program_id(1)
        start = pl.multiple_of(c * CA_QB, CA_QB)

        @pl.when(c == 0)
        def _():
            _ca_bias_from_vector(vec_ref, bias_ref)

        o_ref[...] = _ca_math(q_ref[...], k_ref[pl.ds(start, CA_KB), :], v_ref[pl.ds(start, CA_KB), :],
                              bias_ref[...], c)

    return pl.pallas_call(
        body, grid=grid, in_specs=[q_spec, kv_spec, kv_spec, b_spec], out_specs=q_spec,
        out_shape=jax.ShapeDtypeStruct(q.shape, F32), scratch_shapes=[pltpu.VMEM((2, CHUNK, CA_BAND), F32)],
        compiler_params=_cparams(("parallel", "arbitrary")), name="chunkattn_fwd",
    )(q, kpad, vpad, vec)


def _ca_bwd_call(q, kpad, vpad, vec, do):
    grid, q_spec, kv_spec, b_spec = _ca_specs(q.shape[0])
    last = grid[1] - 1

    def body(q_ref, k_ref, v_ref, vec_ref, do_ref, dq_ref, dk_ref, dv_ref, dvec_ref, bias_ref, dbias_ref):
        c = pl.program_id(1)
        start = pl.multiple_of(c * CA_QB, CA_QB)

        @pl.when(c == 0)
        def _():
            _ca_bias_from_vector(vec_ref, bias_ref)
            dk_ref[...] = jnp.zeros_like(dk_ref)
            dv_ref[...] = jnp.zeros_like(dv_ref)
            dbias_ref[...] = jnp.zeros_like(dbias_ref)

        _, pullback = jax.vjp(lambda a, b, d, e: _ca_math(a, b, d, e, c), q_ref[...],
                              k_ref[pl.ds(start, CA_KB), :], v_ref[pl.ds(start, CA_KB), :], bias_ref[...])
        dq, dkb, dvb, dbias = pullback(do_ref[...])
        dq_ref[...] = dq
        dk_ref[pl.ds(start, CA_KB), :] += dkb
        dv_ref[pl.ds(start, CA_KB), :] += dvb
        dbias_ref[...] += dbias

        @pl.when(c == last)
        def _():
            for h in range(2):
                dvec_ref[0, h:h + 1, :] = _ca_vector_grad(dbias_ref[h])

    return pl.pallas_call(
        body, grid=grid, in_specs=[q_spec, kv_spec, kv_spec, b_spec, q_spec],
        out_specs=[q_spec, kv_spec, kv_spec, b_spec],
        out_shape=[jax.ShapeDtypeStruct(q.shape, F32), jax.ShapeDtypeStruct(kpad.shape, F32),
                   jax.ShapeDtypeStruct(vpad.shape, F32), jax.ShapeDtypeStruct(vec.shape, F32)],
        scratch_shapes=[pltpu.VMEM((2, CHUNK, CA_BAND), F32), pltpu.VMEM((2, CHUNK, CA_BAND), F32)],
        compiler_params=_cparams(("parallel", "arbitrary")), name="chunkattn_bwd",
    )(q, kpad, vpad, vec, do)


@jax.custom_vjp
def chunk_attn_core(q, kpad, vpad, vec):
    return _ca_fwd_call(q, kpad, vpad, vec)


def _ca_core_fwd(q, kpad, vpad, vec):
    return _ca_fwd_call(q, kpad, vpad, vec), (q, kpad, vpad, vec)


def _ca_core_bwd(res, do):
    return tuple(_ca_bwd_call(*res, do))


chunk_attn_core.defvjp(_ca_core_fwd, _ca_core_bwd)


S5_GB = 4
S5_U = S5_WIDTH // S5_GB
S5_L = S5_GROUPS * S5_STATE // S5_GB


def _cmul(ar, ai, br, bi):
    return ar * br - ai * bi, ar * bi + ai * br


def _hdot(a, b, dims=(((1,), (0,)), ((), ()))):
    return lax.dot_general(a, b, dims, precision=HI, preferred_element_type=F32)


def _split_dot(a, b, dims=(((1,), (0,)), ((), ()))):
    a_hi, b_hi = a.astype(BF16), b.astype(BF16)
    a_lo = (a - a_hi.astype(F32)).astype(BF16)
    b_lo = (b - b_hi.astype(F32)).astype(BF16)
    dot = functools.partial(lax.dot_general, dimension_numbers=dims, preferred_element_type=F32)
    return dot(a_hi, b_hi) + (dot(a_hi, b_lo) + dot(a_lo, b_hi))


_NT = (((1,), (1,)), ((), ()))
_TN = (((0,), (0,)), ((), ()))


def _s5_tables(lr, li, reverse):
    p = {1: (lr, li)}
    p[2] = _cmul(*p[1], *p[1])
    p[4] = _cmul(*p[2], *p[2])
    p[3] = _cmul(*p[2], *p[1])
    p[5] = _cmul(*p[4], *p[1])
    p[6] = _cmul(*p[4], *p[2])
    p[7] = _cmul(*p[4], *p[3])
    p[8] = _cmul(*p[4], *p[4])
    row = lax.broadcasted_iota(jnp.int32, (SUBLANE, lr.shape[1]), 0)
    tr = jnp.zeros(row.shape, F32)
    ti = jnp.zeros(row.shape, F32)
    for i in range(SUBLANE):
        k = SUBLANE - i if reverse else i + 1
        tr = jnp.where(row == i, p[k][0], tr)
        ti = jnp.where(row == i, p[k][1], ti)
    return p, (tr, ti), row


def _s5_block_scan(xr, xi, p, tab, row, hr, hi, reverse):
    for k in (1, 2, 4):
        if reverse:
            sr = jnp.where(row < SUBLANE - k, pltpu.roll(xr, SUBLANE - k, 0), 0.0)
            si = jnp.where(row < SUBLANE - k, pltpu.roll(xi, SUBLANE - k, 0), 0.0)
        else:
            sr = jnp.where(row >= k, pltpu.roll(xr, k, 0), 0.0)
            si = jnp.where(row >= k, pltpu.roll(xi, k, 0), 0.0)
        ar, ai = _cmul(p[k][0], p[k][1], sr, si)
        xr, xi = xr + ar, xi + ai
    cr, ci = _cmul(tab[0], tab[1], hr, hi)
    return xr + cr, xi + ci


def _s5_forward_scan(sr_ref, si_ref, lr, li):
    n_blocks = sr_ref.shape[0] // SUBLANE
    p, tab, row = _s5_tables(lr, li, False)

    def step(b, carry):
        base = pl.multiple_of(b * SUBLANE, SUBLANE)
        xr, xi = _s5_block_scan(sr_ref[pl.ds(base, SUBLANE), :], si_ref[pl.ds(base, SUBLANE), :],
                                p, tab, row, carry[0], carry[1], False)
        sr_ref[pl.ds(base, SUBLANE), :] = xr
        si_ref[pl.ds(base, SUBLANE), :] = xi
        return xr[SUBLANE - 1:SUBLANE, :], xi[SUBLANE - 1:SUBLANE, :]

    zero = jnp.zeros((1, lr.shape[1]), F32)
    lax.fori_loop(0, n_blocks, step, (zero, zero))


def _s5_specs(seq):
    u_spec = pl.BlockSpec((seq, S5_U), lambda g: (0, g))
    bd_spec = pl.BlockSpec((1, S5_U, S5_L), lambda g: (g, 0, 0))
    cd_spec = pl.BlockSpec((1, S5_L, S5_U), lambda g: (g, 0, 0))
    lam_spec = pl.BlockSpec((1, 2, S5_L), lambda g: (g, 0, 0))
    d_spec = pl.BlockSpec((1, S5_U), lambda g: (0, g))
    return u_spec, bd_spec, cd_spec, lam_spec, d_spec


S5_ROWS = 256


def _row_chunks(seq, fn):
    rows_per = min(S5_ROWS, seq)

    def step(r, carry):
        fn(pl.ds(pl.multiple_of(r * rows_per, rows_per), rows_per))
        return carry

    lax.fori_loop(0, seq // rows_per, step, 0)


def _s5_fwd_call(u, bdr, bdi, cdr, cdi, lam, d):
    seq = u.shape[0]
    u_spec, bd_spec, cd_spec, lam_spec, d_spec = _s5_specs(seq)

    def body(u_ref, bdr_ref, bdi_ref, cdr_ref, cdi_ref, lam_ref, d_ref, y_ref, sr_ref, si_ref):
        def project_in(rows):
            uv = u_ref[rows, :]
            sr_ref[rows, :] = _bdot(uv, bdr_ref[0])
            si_ref[rows, :] = _bdot(uv, bdi_ref[0])

        def project_out(rows):
            y_ref[rows, :] = (_bdot(sr_ref[rows, :], cdr_ref[0]) - _bdot(si_ref[rows, :], cdi_ref[0])
                              + d_ref[...] * u_ref[rows, :])

        _row_chunks(seq, project_in)
        _s5_forward_scan(sr_ref, si_ref, lam_ref[0, 0:1, :], lam_ref[0, 1:2, :])
        _row_chunks(seq, project_out)

    return pl.pallas_call(
        body, grid=(S5_GB,), in_specs=[u_spec, bd_spec, bd_spec, cd_spec, cd_spec, lam_spec, d_spec],
        out_specs=u_spec, out_shape=jax.ShapeDtypeStruct(u.shape, F32),
        scratch_shapes=[pltpu.VMEM((seq, S5_L), F32), pltpu.VMEM((seq, S5_L), F32)],
        compiler_params=_cparams(("parallel",)), name="s5_fwd",
    )(u, bdr, bdi, cdr, cdi, lam, d)


def _s5_bwd_call(u, bdr, bdi, cdr, cdi, lam, d, dy):
    seq = u.shape[0]
    n_blocks = seq // SUBLANE
    u_spec, bd_spec, cd_spec, lam_spec, d_spec = _s5_specs(seq)

    def body(u_ref, bdr_ref, bdi_ref, cdr_ref, cdi_ref, lam_ref, d_ref, dy_ref,
             du_ref, dbdr_ref, dbdi_ref, dcdr_ref, dcdi_ref, dlam_ref, dd_ref, sr_ref, si_ref, gr_ref, gi_ref):
        lr, li = lam_ref[0, 0:1, :], lam_ref[0, 1:2, :]

        def project_in(rows):
            uv = u_ref[rows, :]
            dyv = dy_ref[rows, :]
            sr_ref[rows, :] = _bdot(uv, bdr_ref[0])
            si_ref[rows, :] = _bdot(uv, bdi_ref[0])
            gr_ref[rows, :] = _bdot(dyv, cdr_ref[0], _NT)
            gi_ref[rows, :] = -_bdot(dyv, cdi_ref[0], _NT)

        _row_chunks(seq, project_in)
        _s5_forward_scan(sr_ref, si_ref, lr, li)
        p, tab, row = _s5_tables(lr, -li, True)

        def step(i, carry):
            hr, hi, acc_r, acc_i = carry
            b = n_blocks - 1 - i
            base = pl.multiple_of(b * SUBLANE, SUBLANE)
            xr, xi = _s5_block_scan(gr_ref[pl.ds(base, SUBLANE), :], gi_ref[pl.ds(base, SUBLANE), :],
                                    p, tab, row, hr, hi, True)
            gr_ref[pl.ds(base, SUBLANE), :] = xr
            gi_ref[pl.ds(base, SUBLANE), :] = xi
            prev = pl.multiple_of(jnp.maximum(b - 1, 0) * SUBLANE, SUBLANE)
            keep = (b > 0).astype(F32)
            last_r = sr_ref[pl.ds(prev, SUBLANE), :][SUBLANE - 1:SUBLANE, :] * keep
            last_i = si_ref[pl.ds(prev, SUBLANE), :][SUBLANE - 1:SUBLANE, :] * keep
            pr = jnp.where(row >= 1, pltpu.roll(sr_ref[pl.ds(base, SUBLANE), :], 1, 0), last_r)
            pi = jnp.where(row >= 1, pltpu.roll(si_ref[pl.ds(base, SUBLANE), :], 1, 0), last_i)
            acc_r = acc_r + pr * xr + pi * xi
            acc_i = acc_i + pr * xi - pi * xr
            return xr[0:1, :], xi[0:1, :], acc_r, acc_i

        zero = jnp.zeros((1, S5_L), F32)
        zacc = jnp.zeros((SUBLANE, S5_L), F32)
        _, _, acc_r, acc_i = lax.fori_loop(0, n_blocks, step, (zero, zero, zacc, zacc))
        dlam_ref[0, 0:1, :] = jnp.sum(acc_r, axis=0, keepdims=True)
        dlam_ref[0, 1:2, :] = jnp.sum(acc_i, axis=0, keepdims=True)
        for ref in (dbdr_ref, dbdi_ref, dcdr_ref, dcdi_ref, dd_ref):
            ref[...] = jnp.zeros_like(ref)

        def grads(rows):
            uv, dyv = u_ref[rows, :], dy_ref[rows, :]
            grv, giv = gr_ref[rows, :], gi_ref[rows, :]
            du_ref[rows, :] = _bdot(grv, bdr_ref[0], _NT) + _bdot(giv, bdi_ref[0], _NT) + d_ref[...] * dyv
            dbdr_ref[0] += _bdot(uv, grv, _TN)
            dbdi_ref[0] += _bdot(uv, giv, _TN)
            dcdr_ref[0] += _bdot(sr_ref[rows, :], dyv, _TN)
            dcdi_ref[0] -= _bdot(si_ref[rows, :], dyv, _TN)
            dd_ref[...] += jnp.sum(dyv * uv, axis=0, keepdims=True)

        _row_chunks(seq, grads)

    scratch = [pltpu.VMEM((seq, S5_L), F32) for _ in range(4)]
    return pl.pallas_call(
        body, grid=(S5_GB,),
        in_specs=[u_spec, bd_spec, bd_spec, cd_spec, cd_spec, lam_spec, d_spec, u_spec],
        out_specs=[u_spec, bd_spec, bd_spec, cd_spec, cd_spec, lam_spec, d_spec],
        out_shape=[jax.ShapeDtypeStruct(a.shape, F32) for a in (u, bdr, bdi, cdr, cdi, lam, d)],
        scratch_shapes=scratch, compiler_params=_cparams(("parallel",)), name="s5_bwd",
    )(u, bdr, bdi, cdr, cdi, lam, d, dy)


@jax.custom_vjp
def s5_core(u, bdr, bdi, cdr, cdi, lam, d):
    return _s5_fwd_call(u, bdr, bdi, cdr, cdi, lam, d)


def _s5_core_fwd(*args):
    return _s5_fwd_call(*args), args


def _s5_core_bwd(res, dy):
    return tuple(_s5_bwd_call(*res, dy))


s5_core.defvjp(_s5_core_fwd, _s5_core_bwd)


def _s5_discretize(a_re, a_im, log_dt, b_re, b_im, c_re, c_im, d):
    dt = jnp.exp(log_dt)[:, None]
    mag = jnp.exp(a_re * dt)
    lbr, lbi = mag * jnp.cos(a_im * dt), mag * jnp.sin(a_im * dt)
    den = a_re * a_re + a_im * a_im
    fr = ((lbr - 1.0) * a_re + lbi * a_im) / den
    fi = (lbi * a_re - (lbr - 1.0) * a_im) / den
    bbr = fr[:, :, None] * b_re - fi[:, :, None] * b_im
    bbi = fr[:, :, None] * b_im + fi[:, :, None] * b_re
    eye = jnp.eye(S5_GROUPS // S5_GB, dtype=F32)
    gl = S5_GROUPS // S5_GB

    def bd(t):
        return jnp.einsum('bgpc,gh->bgchp', t.reshape(S5_GB, gl, S5_STATE, S5_GROUP), eye).reshape(S5_GB, S5_U, S5_L)

    def cd(t):
        return jnp.einsum('bgcp,gh->bgphc', t.reshape(S5_GB, gl, S5_GROUP, S5_STATE), eye).reshape(S5_GB, S5_L, S5_U)

    lam = jnp.stack([lbr.reshape(S5_GB, S5_L), lbi.reshape(S5_GB, S5_L)], axis=1)
    return bd(bbr), bd(bbi), cd(c_re), cd(c_im), lam, d.reshape(1, S5_WIDTH)


@jax.custom_vjp
def _unit_lower_solve(neg_a, rhs, tinv):
    return _split_dot(tinv, rhs)


def _unit_lower_solve_fwd(neg_a, rhs, tinv):
    x = _split_dot(tinv, rhs)
    return x, (x, tinv)


def _unit_lower_solve_bwd(res, dx):
    x, tinv = res
    g = _split_dot(tinv, dx, _TN)
    return _split_dot(g, x, _NT), g, jnp.zeros_like(tinv)


_unit_lower_solve.defvjp(_unit_lower_solve_fwd, _unit_lower_solve_bwd)


def _unit_lower_inverse(neg_a):
    r = lax.broadcasted_iota(jnp.int32, neg_a.shape, 0)
    c = lax.broadcasted_iota(jnp.int32, neg_a.shape, 1)
    p = (r == c).astype(F32) + neg_a
    npow = _split_dot(neg_a, neg_a)
    for _ in range(4):
        y = _split_dot(jnp.concatenate([p, npow], axis=0), npow)
        p = p + y[:CHUNK]
        npow = y[CHUNK:]
    return p + _split_dot(p, npow)


def _gdn_chunk(q, k, v, g_col, b_col, st, tinv=None):
    r = lax.broadcasted_iota(jnp.int32, (CHUNK, CHUNK), 0)
    c = lax.broadcasted_iota(jnp.int32, (CHUNK, CHUNK), 1)
    eye = (r == c).astype(F32)
    strict = r > c
    causal = r >= c
    g_row = jnp.sum(g_col * eye, axis=0, keepdims=True)
    gcum = jnp.sum(jnp.where(causal, g_row, 0.0), axis=1, keepdims=True)
    gcum_row = jnp.sum(gcum * eye, axis=0, keepdims=True)
    diff = gcum - gcum_row
    decay_strict = jnp.where(strict, jnp.exp(jnp.where(strict, diff, 0.0)), 0.0)
    decay_causal = jnp.where(causal, jnp.exp(jnp.where(causal, diff, 0.0)), 0.0)
    gamma = jnp.exp(gcum)
    g_last = jnp.sum(jnp.where(lax.broadcasted_iota(jnp.int32, (CHUNK, 1), 0) == CHUNK - 1, gcum, 0.0),
                     axis=0, keepdims=True)
    kk = _bdot(k, k, _NT)
    neg_a = -(b_col * kk * decay_strict)
    if tinv is None:
        tinv = _unit_lower_inverse(neg_a)
    x = _unit_lower_solve(neg_a, jnp.concatenate([b_col * v, (b_col * gamma) * k], axis=1), lax.stop_gradient(tinv))
    u_new, w_k = x[:, :GDN_HEAD_DIM], x[:, GDN_HEAD_DIM:]
    qk = _bdot(q, k, _NT) * decay_causal
    q_g = q * gamma
    k_tail = k * jnp.exp(g_last - gcum)
    w = u_new - _bdot(w_k, st)
    o = _bdot(q_g, st) + _bdot(qk, w)
    st_new = jnp.exp(g_last) * st + _bdot(k_tail, w, _TN)
    return o, st_new, tinv


def _gdn_cols(bgv, h):
    lane = lax.broadcasted_iota(jnp.int32, bgv.shape, 1)
    g_col = jnp.sum(jnp.where(lane == h, bgv, 0.0), axis=1, keepdims=True)
    b_col = jnp.sum(jnp.where(lane == GDN_HEADS + h, bgv, 0.0), axis=1, keepdims=True)
    return g_col, b_col


GDN_CPS = 2


def _gdn_fwd_call(q, k, v, bg):
    seq = q.shape[0]
    n_chunks = seq // CHUNK
    n_steps = n_chunks // GDN_CPS
    rows = GDN_CPS * CHUNK
    x_spec = pl.BlockSpec((rows, GDN_WIDTH), lambda n: (n, 0))
    bg_spec = pl.BlockSpec((rows, LANE), lambda n: (n, 0))
    st_spec = pl.BlockSpec((GDN_CPS, GDN_WIDTH, GDN_HEAD_DIM), lambda n: (n, 0, 0))
    ti_spec = pl.BlockSpec((GDN_CPS, GDN_HEADS * CHUNK, CHUNK), lambda n: (n, 0, 0))

    def body(q_ref, k_ref, v_ref, bg_ref, o_ref, st_out_ref, ti_out_ref, st_ref):
        @pl.when(pl.program_id(0) == 0)
        def _():
            st_ref[...] = jnp.zeros_like(st_ref)

        for h in range(GDN_HEADS):
            sl = slice(h * GDN_HEAD_DIM, (h + 1) * GDN_HEAD_DIM)
            st = st_ref[sl, :]
            for cc in range(GDN_CPS):
                rs = slice(cc * CHUNK, (cc + 1) * CHUNK)
                g_col, b_col = _gdn_cols(bg_ref[rs, :], h)
                st_out_ref[cc, sl, :] = st
                o, st, tinv = _gdn_chunk(q_ref[rs, sl], k_ref[rs, sl], v_ref[rs, sl], g_col, b_col, st)
                o_ref[rs, sl] = o
                ti_out_ref[cc, h * CHUNK:(h + 1) * CHUNK, :] = tinv
            st_ref[sl, :] = st

    return pl.pallas_call(
        body, grid=(n_steps,), in_specs=[x_spec, x_spec, x_spec, bg_spec], out_specs=[x_spec, st_spec, ti_spec],
        out_shape=[jax.ShapeDtypeStruct(q.shape, F32),
                   jax.ShapeDtypeStruct((n_chunks, GDN_WIDTH, GDN_HEAD_DIM), F32),
                   jax.ShapeDtypeStruct((n_chunks, GDN_HEADS * CHUNK, CHUNK), F32)],
        scratch_shapes=[pltpu.VMEM((GDN_WIDTH, GDN_HEAD_DIM), F32)],
        compiler_params=_cparams(("arbitrary",)), name="gdn_fwd",
    )(q, k, v, bg)


def _gdn_bwd_call(q, k, v, bg, states, tinvs, do):
    seq = q.shape[0]
    n_steps = seq // CHUNK // GDN_CPS
    rows = GDN_CPS * CHUNK
    x_spec = pl.BlockSpec((rows, GDN_WIDTH), lambda i: (n_steps - 1 - i, 0))
    bg_spec = pl.BlockSpec((rows, LANE), lambda i: (n_steps - 1 - i, 0))
    st_spec = pl.BlockSpec((GDN_CPS, GDN_WIDTH, GDN_HEAD_DIM), lambda i: (n_steps - 1 - i, 0, 0))
    ti_spec = pl.BlockSpec((GDN_CPS, GDN_HEADS * CHUNK, CHUNK), lambda i: (n_steps - 1 - i, 0, 0))

    def body(q_ref, k_ref, v_ref, bg_ref, st_in_ref, ti_ref, do_ref, dq_ref, dk_ref, dv_ref, dbg_ref, dst_ref):
        @pl.when(pl.program_id(0) == 0)
        def _():
            dst_ref[...] = jnp.zeros_like(dst_ref)

        lane = lax.broadcasted_iota(jnp.int32, (CHUNK, LANE), 1)
        dbg = [jnp.zeros((CHUNK, LANE), F32) for _ in range(GDN_CPS)]
        for h in range(GDN_HEADS):
            sl = slice(h * GDN_HEAD_DIM, (h + 1) * GDN_HEAD_DIM)
            dst = dst_ref[sl, :]
            for cc in reversed(range(GDN_CPS)):
                rs = slice(cc * CHUNK, (cc + 1) * CHUNK)
                g_col, b_col = _gdn_cols(bg_ref[rs, :], h)
                tinv = ti_ref[cc, h * CHUNK:(h + 1) * CHUNK, :]
                _, pullback = jax.vjp(lambda *a: _gdn_chunk(*a, tinv=tinv)[:2], q_ref[rs, sl], k_ref[rs, sl],
                                      v_ref[rs, sl], g_col, b_col, st_in_ref[cc, sl, :])
                dq, dk, dv, dg, db, dst = pullback((do_ref[rs, sl], dst))
                dq_ref[rs, sl] = dq
                dk_ref[rs, sl] = dk
                dv_ref[rs, sl] = dv
                dbg[cc] = dbg[cc] + jnp.where(lane == h, dg, 0.0) + jnp.where(lane == GDN_HEADS + h, db, 0.0)
            dst_ref[sl, :] = dst
        for cc in range(GDN_CPS):
            dbg_ref[cc * CHUNK:(cc + 1) * CHUNK, :] = dbg[cc]

    return pl.pallas_call(
        body, grid=(n_steps,), in_specs=[x_spec, x_spec, x_spec, bg_spec, st_spec, ti_spec, x_spec],
        out_specs=[x_spec, x_spec, x_spec, bg_spec],
        out_shape=[jax.ShapeDtypeStruct(q.shape, F32)] * 3 + [jax.ShapeDtypeStruct(bg.shape, F32)],
        scratch_shapes=[pltpu.VMEM((GDN_WIDTH, GDN_HEAD_DIM), F32)],
        compiler_params=_cparams(("arbitrary",)), name="gdn_bwd",
    )(q, k, v, bg, states, tinvs, do)


@jax.custom_vjp
def gdn_core(q, k, v, bg):
    return _gdn_fwd_call(q, k, v, bg)[0]


def _gdn_core_fwd(q, k, v, bg):
    o, states, tinvs = _gdn_fwd_call(q, k, v, bg)
    return o, (q, k, v, bg, states, tinvs)


def _gdn_core_bwd(res, do):
    return tuple(_gdn_bwd_call(*res, do))


gdn_core.defvjp(_gdn_core_fwd, _gdn_core_bwd)


def _row(v):
    return v.reshape(1, -1)


def _lane_pad(v):
    return jnp.pad(v, (0, LANE - v.shape[0])).reshape(1, LANE)


def _delay_rows(x, k):
    return jnp.pad(x, ((k, 0), (0, 0)))[:x.shape[0]]


def s5_mixer(u, a_re, a_im, log_dt, b_re, b_im, c_re, c_im, d, w_glu, b_glu):
    y = s5_core(u, *_s5_discretize(a_re, a_im, log_dt, b_re, b_im, c_re, c_im, d))
    return make_rowop(_s5_post_fn, "s5_post")((y,), (w_glu, _row(b_glu)))[0]


def gated_deltanet(qkv, gate, ab, conv_w, a_log, dt_bias, out_g):
    rows = (qkv, _delay_rows(qkv, 1), _delay_rows(qkv, 2), _delay_rows(qkv, 3), ab)
    q, k, v, bg = make_rowop(_gdn_prep_fn, "gdn_prep")(rows, (conv_w, _lane_pad(a_log), _lane_pad(dt_bias)))
    o = gdn_core(q, k, v, bg)
    return make_rowop(_gdn_out_fn, "gdn_out")((o, gate), (_row(out_g),))[0]


def chunk_attention(x, h, w_qkv, w_out, q_g, k_g, rel_bias):
    qkv = linear_cols(h, w_qkv)
    qn, kn = make_rowop(_ca_qknorm_fn, "ca_qknorm")((qkv,), (_row(jnp.tile(q_g, 2)), _row(jnp.tile(k_g, 2))))
    kpad = jnp.pad(kn, ((CA_PAD, 0), (0, 0)))
    vpad = jnp.pad(qkv[:, 2 * D_MODEL:], ((CA_PAD, 0), (0, 0)))
    o = chunk_attn_core(qn, kpad, vpad, _rel_bias_vector(rel_bias))
    return linear_res(o, w_out, x)


def memory_cross_attention(x, h, mem_n, w_q, w_kv, w_out, q_g, k_g):
    q = linear(h, w_q)
    kv = linear_cols(mem_n, w_kv)
    o = make_rowop(_xattn_fn, "xattn")((q,), (kv[:, :D_MODEL], kv[:, D_MODEL:], _row(q_g), _row(k_g)))[0]
    return linear_res(o, w_out, x)


def swiglu(x, h, w_gate, w_up, w_down):
    return ffn_down(linear_cols_dm(h, w_gate), linear_cols_dm(h, w_up), w_down, x)


def _rel_bias_vector(rel_bias):
    heads = rel_bias.shape[0]
    n_far = CA_BAND - 1 - MAX_REL
    n_near = CA_VEC - 1 - n_far
    vec = jnp.concatenate([jnp.zeros((heads, 1), F32),
                           jnp.broadcast_to(rel_bias[:, 2 * MAX_REL:], (heads, n_far)),
                           jnp.flip(rel_bias, axis=1)[:, :n_near]], axis=1)
    return vec.reshape(heads // 2, 2, CA_VEC)


def _exchange(arrays, modes, name):
    n = len(arrays)
    out_shapes = [jax.ShapeDtypeStruct((N_DEV,) + a.shape if m == "gather" else a.shape, a.dtype)
                  for a, m in zip(arrays, modes)]

    def body(*refs):
        ins, outs = refs[:n], refs[n:2 * n]
        send_sems, recv_sems, local_sems = refs[2 * n:]
        x, y, c = lax.axis_index("x"), lax.axis_index("y"), lax.axis_index("c")
        me = 4 * x + 2 * y + c
        pending = []
        for i in range(n):
            gather = modes[i] == "gather"
            local = pltpu.make_async_copy(ins[i] if gather else ins[i].at[me], outs[i].at[me], local_sems.at[i])
            local.start()
            pending.append(local)
        for k in range(1, N_DEV):
            px, py, pc = (x + (k >> 2)) % 2, (y + ((k >> 1) & 1)) % 2, (c + (k & 1)) % 2
            peer = 4 * px + 2 * py + pc
            for i in range(n):
                src = ins[i] if modes[i] == "gather" else ins[i].at[peer]
                sem = i * (N_DEV - 1) + k - 1
                send = pltpu.make_async_remote_copy(src_ref=src, dst_ref=outs[i].at[me], send_sem=send_sems.at[sem],
                                                    recv_sem=recv_sems.at[sem], device_id=(px, py, pc),
                                                    device_id_type=pl.DeviceIdType.MESH)
                send.start()
                arrival = pltpu.make_async_remote_copy(src_ref=src, dst_ref=outs[i].at[peer],
                                                       send_sem=send_sems.at[sem], recv_sem=recv_sems.at[sem],
                                                       device_id=(px, py, pc), device_id_type=pl.DeviceIdType.MESH)
                pending.append((send, arrival))
        for item in pending:
            if isinstance(item, tuple):
                item[0].wait_send()
                item[1].wait_recv()
            else:
                item.wait()

    any_spec = pl.BlockSpec(memory_space=pl.ANY)
    return pl.pallas_call(
        body, in_specs=[any_spec] * n, out_specs=[any_spec] * n, out_shape=out_shapes,
        scratch_shapes=[pltpu.SemaphoreType.DMA((n * (N_DEV - 1),)), pltpu.SemaphoreType.DMA((n * (N_DEV - 1),)),
                        pltpu.SemaphoreType.DMA((n,))],
        name=name,
    )(*arrays)


_HBM_SPEC = pl.BlockSpec(memory_space=pltpu.HBM)
_SEM_SPEC = pl.BlockSpec(memory_space=pltpu.SEMAPHORE)
_SIDE_EFFECT = pltpu.SideEffectType.DATAFLOW_SIDE_EFFECTING


def _peer(x, y, c, k):
    return (x + (k >> 2)) % 2, (y + ((k >> 1) & 1)) % 2, (c + (k & 1)) % 2


def _exchange_start(arrays, modes, after, name, carry):
    n = len(arrays)
    n_sem = n * (N_DEV - 1)
    lands = [pltpu.with_memory_space_constraint(lax.empty((N_DEV,) + a.shape if m == "gather" else a.shape, a.dtype),
                                                pltpu.HBM) for a, m in zip(arrays, modes)]
    arrays = [pltpu.with_memory_space_constraint(a, pltpu.HBM) for a in arrays]

    def body(*refs):
        ins, zones = refs[:n], refs[n:2 * n]
        send_sems, recv_sems, own_sems = refs[2 * n + 2:2 * n + 5]
        x, y, c = lax.axis_index("x"), lax.axis_index("y"), lax.axis_index("c")
        me = 4 * x + 2 * y + c
        for i in range(n):
            pltpu.make_async_copy(ins[i] if modes[i] == "gather" else ins[i].at[me], zones[i].at[me],
                                  own_sems.at[i]).start()
        for k in range(1, N_DEV):
            px, py, pc = _peer(x, y, c, k)
            peer = 4 * px + 2 * py + pc
            for i in range(n):
                sem = i * (N_DEV - 1) + k - 1
                pltpu.make_async_remote_copy(src_ref=ins[i] if modes[i] == "gather" else ins[i].at[peer],
                                             dst_ref=zones[i].at[me], send_sem=send_sems.at[sem],
                                             recv_sem=recv_sems.at[sem], device_id=(px, py, pc),
                                             device_id_type=pl.DeviceIdType.MESH).start()

    carry = pltpu.with_memory_space_constraint(carry, pltpu.HBM)
    out_shape = ((pltpu.SemaphoreType.DMA((n_sem,)), pltpu.SemaphoreType.DMA((n_sem,)), pltpu.SemaphoreType.DMA((n,)))
                 + tuple(pltpu.HBM(a.shape, a.dtype) for a in arrays) + tuple(pltpu.HBM(z.shape, z.dtype) for z in lands)
                 + (pltpu.HBM(carry.shape, carry.dtype),))
    aliases = {i: 3 + i for i in range(2 * n)}
    aliases[2 * n + 1] = 3 + 2 * n
    res = pl.pallas_call(
        body, name=name, out_shape=out_shape,
        in_specs=[_HBM_SPEC] * (2 * n) + [pl.BlockSpec(memory_space=pl.ANY), _HBM_SPEC],
        out_specs=(_SEM_SPEC,) * 3 + (_HBM_SPEC,) * (2 * n + 1),
        input_output_aliases=aliases,
        compiler_params=pltpu.CompilerParams(has_side_effects=_SIDE_EFFECT),
    )(*arrays, *lands, after, carry)
    return tuple(res[:3]), list(res[3:3 + n]), list(res[3 + n:3 + 2 * n]), res[3 + 2 * n]


def _exchange_wait(started, modes, after, name):
    sems, sources, zones, _ = started
    n = len(sources)

    def body(*refs):
        ins, lands = refs[:n], refs[n:2 * n]
        send_ref, recv_ref, own_ref = refs[2 * n:2 * n + 3]
        x, y, c = lax.axis_index("x"), lax.axis_index("y"), lax.axis_index("c")
        me = 4 * x + 2 * y + c
        for i in range(n):
            pltpu.make_async_copy(ins[i] if modes[i] == "gather" else ins[i].at[me], lands[i].at[me],
                                  own_ref.at[i]).wait()
        for k in range(1, N_DEV):
            px, py, pc = _peer(x, y, c, k)
            peer = 4 * px + 2 * py + pc
            for i in range(n):
                sem = i * (N_DEV - 1) + k - 1
                cp = pltpu.make_async_remote_copy(src_ref=ins[i] if modes[i] == "gather" else ins[i].at[peer],
                                                  dst_ref=lands[i].at[peer], send_sem=send_ref.at[sem],
                                                  recv_sem=recv_ref.at[sem], device_id=(px, py, pc),
                                                  device_id_type=pl.DeviceIdType.MESH)
                cp.wait_send()
                cp.wait_recv()

    res = pl.pallas_call(
        body, name=name,
        out_shape=tuple(pltpu.HBM(a.shape, a.dtype) for a in sources) + tuple(pltpu.HBM(z.shape, z.dtype) for z in zones),
        in_specs=[_HBM_SPEC] * (2 * n) + [_SEM_SPEC] * 3 + [pl.BlockSpec(memory_space=pl.ANY)],
        out_specs=(_HBM_SPEC,) * (2 * n), input_output_aliases={i: i for i in range(2 * n)},
        compiler_params=pltpu.CompilerParams(has_side_effects=_SIDE_EFFECT),
    )(*sources, *zones, *sems, after)
    return list(res[n:])


ADAM_TILE = 64 * 1024


def _adam_call(w, m, v, slots, name):
    n_layers, rows, cols = w.shape
    tr = rows
    if n_layers * rows * cols > ADAM_TILE:
        fits = [t for t in range(SUBLANE, rows, SUBLANE) if rows % t == 0 and n_layers * t * cols <= ADAM_TILE]
        tr = max(fits) if fits else SUBLANE
    c1 = 1.0 - ADAM_B1 ** ADAM_STEP
    c2 = 1.0 - ADAM_B2 ** ADAM_STEP

    def body(*refs):
        w_ref, m_ref, v_ref = refs[:3]
        slot_refs = refs[3:3 + n_layers]
        grad_ref, delta_ref, nm_ref, nv_ref = refs[3 + n_layers:]
        for layer in range(n_layers):
            g = slot_refs[layer][0].astype(F32)
            for k in range(1, N_DEV):
                g = g + slot_refs[layer][k].astype(F32)
            m_new = ADAM_B1 * m_ref[layer] + (1.0 - ADAM_B1) * g
            v_new = ADAM_B2 * v_ref[layer] + (1.0 - ADAM_B2) * (g * g)
            m_hat = m_new / c1
            v_hat = v_new / c2
            grad_ref[layer] = g
            delta_ref[layer] = -ADAM_LR * (m_hat / (jnp.sqrt(v_hat) + ADAM_EPS) + ADAM_WD * w_ref[layer])
            nm_ref[layer] = m_new
            nv_ref[layer] = v_new

    spec = pl.BlockSpec((n_layers, tr, cols), lambda i: (0, i, 0))
    slot_spec = pl.BlockSpec((N_DEV, tr, cols), lambda i: (0, i, 0))
    return pl.pallas_call(
        body, grid=(rows // tr,), in_specs=[spec, spec, spec] + [slot_spec] * n_layers,
        out_specs=[spec] * 4, out_shape=[jax.ShapeDtypeStruct(w.shape, F32)] * 4,
        compiler_params=_cparams(("parallel",)), name=name,
    )(w, m, v, *slots)


WEIGHT_NAMES = ['ab_norm_g', 'ab_w_in', 'ab_w_out', 's5_a_re', 's5_a_im', 's5_log_dt', 's5_b_re', 's5_b_im', 's5_c_re',
                's5_c_im', 's5_d', 's5_w_glu', 's5_b_glu', 'gdn_conv_w', 'gdn_a_log', 'gdn_dt_bias', 'gdn_out_norm_g',
                'c_norm_g', 'c_w_qkv', 'c_w_out', 'c_q_norm_g', 'c_k_norm_g', 'c_rel_bias', 'mem_norm_g', 'xa_norm_g',
                'xa_w_q', 'xa_w_kv', 'xa_w_out', 'xa_q_norm_g', 'xa_k_norm_g', 'f_norm_g', 'f_w_gate', 'f_w_up',
                'f_w_down']

SHARDED = {
    'ab_w_in': ('col', BF16), 'ab_w_out': ('row', BF16), 's5_w_glu': ('row', BF16), 'gdn_conv_w': ('col', F32),
    'c_norm_g': ('col', F32), 'c_w_qkv': ('col', BF16), 'c_w_out': ('row', BF16), 'xa_w_q': ('row', BF16),
    'xa_w_kv': ('col', BF16), 'xa_w_out': ('row', BF16), 'f_w_gate': ('col', BF16), 'f_w_up': ('col', BF16),
    'f_w_down': ('row', BF16),
}
GATHERED_AS_IS = ('c_w_qkv', 'xa_w_kv', 'f_w_gate', 'f_w_up', 'f_w_down')
REPLICATED = [n for n in WEIGHT_NAMES if n not in SHARDED]
PACK_UNIT = SUBLANE * LANE


def _full_from_gathered(g, axis):
    if axis == "row":
        return g.reshape(g.shape[0] * g.shape[1], g.shape[2])
    return jnp.transpose(g, (1, 0, 2)).reshape(g.shape[1], g.shape[0] * g.shape[2])


def _pack(arrays):
    flat = []
    for a in arrays:
        size = a.size
        padded = -(-size // PACK_UNIT) * PACK_UNIT
        flat.append(jnp.pad(a.reshape(-1), (0, padded - size)).reshape(-1, LANE))
    return jnp.concatenate(flat, axis=0)


def _unpack(buf, shapes):
    out, row = [], 0
    for shape in shapes:
        size = math.prod(shape)
        rows = -(-size // PACK_UNIT) * SUBLANE
        out.append(buf[row:row + rows].reshape(-1)[:size].reshape(shape))
        row += rows
    return out


N_STAGES = 2 * DEPTH
EVEN_SHARDED = ['ab_w_in', 'ab_w_out', 's5_w_glu', 'gdn_conv_w']
ODD_SHARDED = ['c_norm_g', 'c_w_qkv', 'c_w_out']
ALL_SHARDED = ['xa_w_q', 'xa_w_kv', 'xa_w_out', 'f_w_gate', 'f_w_up', 'f_w_down']
EVEN_SMALL = ['ab_norm_g', 's5_a_re', 's5_a_im', 's5_log_dt', 's5_b_re', 's5_b_im', 's5_c_re', 's5_c_im', 's5_d',
              's5_b_glu', 'gdn_a_log', 'gdn_dt_bias', 'gdn_out_norm_g']
ODD_SMALL = ['c_q_norm_g', 'c_k_norm_g', 'c_rel_bias']
ALL_SMALL = ['xa_norm_g', 'xa_q_norm_g', 'xa_k_norm_g', 'f_norm_g']


def _stage_params(stage):
    layer, part = divmod(stage, 2)
    if part == 1:
        return [(n, layer) for n in ALL_SHARDED], [(n, layer) for n in ALL_SMALL]
    big, small = (EVEN_SHARDED, EVEN_SMALL) if layer % 2 == 0 else (ODD_SHARDED, ODD_SMALL)
    return [(n, layer // 2) for n in big], [(n, layer // 2) for n in small]


def _stage_forward(stage, landed, small, x, mem_n):
    layer, part = divmod(stage, 2)
    big = {}
    for (n, _), g in zip(_stage_params(stage)[0], landed):
        if n == 'ab_w_in':
            big[n] = cols_to_natural(g, AB_IN_PAD)
        elif n in GATHERED_AS_IS:
            big[n] = g
        elif n == 's5_w_glu':
            big[n] = _full_from_gathered(g, 'row').astype(F32)
        else:
            big[n] = _full_from_gathered(g, SHARDED[n][0])
    if part == 1:
        h, x = rmsnorm_res(x, small['xa_norm_g'], "xa_norm")
        x = memory_cross_attention(x, h, mem_n, big['xa_w_q'], big['xa_w_kv'], big['xa_w_out'],
                                   small['xa_q_norm_g'], small['xa_k_norm_g'])
        h, x = rmsnorm_res(x, small['f_norm_g'], "f_norm")
        return swiglu(x, h, big['f_w_gate'], big['f_w_up'], big['f_w_down'])
    if layer % 2 == 0:
        h, x = rmsnorm_res(x, small['ab_norm_g'], "ab_norm")
        w_in = big['ab_w_in']
        u = linear(h, w_in[:, :S5_WIDTH])
        qkv = linear(h, w_in[:, S5_WIDTH:S5_WIDTH + 3 * GDN_WIDTH])
        gate = linear(h, w_in[:, S5_WIDTH + 3 * GDN_WIDTH:S5_WIDTH + 4 * GDN_WIDTH])
        ab = linear(h, w_in[:, S5_WIDTH + 4 * GDN_WIDTH:])
        a_out = s5_mixer(u, small['s5_a_re'], small['s5_a_im'], small['s5_log_dt'], small['s5_b_re'], small['s5_b_im'],
                         small['s5_c_re'], small['s5_c_im'], small['s5_d'], big['s5_w_glu'], small['s5_b_glu'])
        b_out = gated_deltanet(qkv, gate, ab, big['gdn_conv_w'], small['gdn_a_log'], small['gdn_dt_bias'],
                               small['gdn_out_norm_g'])
        return linear_res(jnp.concatenate([a_out, b_out], axis=1), big['ab_w_out'], x)
    h, x = rmsnorm_res(x, big['c_norm_g'].reshape(-1), "c_norm")
    return chunk_attention(x, h, big['c_w_qkv'], big['c_w_out'], small['c_q_norm_g'], small['c_k_norm_g'],
                           small['c_rel_bias'])


def _loss_rows(x, target):
    return jnp.sum(make_rowop(_loss_fn, "loss")((x, target), ())[0])


def kernel(x, mem, ab_norm_g, ab_w_in, ab_w_out, s5_a_re, s5_a_im, s5_log_dt, s5_b_re, s5_b_im, s5_c_re, s5_c_im, s5_d, s5_w_glu, s5_b_glu, gdn_conv_w, gdn_a_log, gdn_dt_bias, gdn_out_norm_g, c_norm_g, c_w_qkv, c_w_out, c_q_norm_g, c_k_norm_g, c_rel_bias, mem_norm_g, xa_norm_g, xa_w_q, xa_w_kv, xa_w_out, xa_q_norm_g, xa_k_norm_g, f_norm_g, f_w_gate, f_w_up, f_w_down, loss_target, m_ab_norm_g, m_ab_w_in, m_ab_w_out, m_s5_a_re, m_s5_a_im, m_s5_log_dt, m_s5_b_re, m_s5_b_im, m_s5_c_re, m_s5_c_im, m_s5_d, m_s5_w_glu, m_s5_b_glu, m_gdn_conv_w, m_gdn_a_log, m_gdn_dt_bias, m_gdn_out_norm_g, m_c_norm_g, m_c_w_qkv, m_c_w_out, m_c_q_norm_g, m_c_k_norm_g, m_c_rel_bias, m_mem_norm_g, m_xa_norm_g, m_xa_w_q, m_xa_w_kv, m_xa_w_out, m_xa_q_norm_g, m_xa_k_norm_g, m_f_norm_g, m_f_w_gate, m_f_w_up, m_f_w_down, v_ab_norm_g, v_ab_w_in, v_ab_w_out, v_s5_a_re, v_s5_a_im, v_s5_log_dt, v_s5_b_re, v_s5_b_im, v_s5_c_re, v_s5_c_im, v_s5_d, v_s5_w_glu, v_s5_b_glu, v_gdn_conv_w, v_gdn_a_log, v_gdn_dt_bias, v_gdn_out_norm_g, v_c_norm_g, v_c_w_qkv, v_c_w_out, v_c_q_norm_g, v_c_k_norm_g, v_c_rel_bias, v_mem_norm_g, v_xa_norm_g, v_xa_w_q, v_xa_w_kv, v_xa_w_out, v_xa_q_norm_g, v_xa_k_norm_g, v_f_norm_g, v_f_w_gate, v_f_w_up, v_f_w_down):
    given = dict(locals())
    no_after = jnp.zeros((SUBLANE, LANE), F32)

    def shard(n, idx):
        a = given[n][idx]
        return (a.reshape(1, -1) if a.ndim == 1 else a).astype(SHARDED[n][1])

    def gather_start(stage, after, carry):
        arrays = [shard(n, idx) for n, idx in _stage_params(stage)[0]]
        return _exchange_start(arrays, ["gather"] * len(arrays), after, "gather_start_%d" % stage, carry)

    act = x[0]
    mem_n, mem_pullback = jax.vjp(lambda m, g: rmsnorm(m, g, "mem_norm"), mem[0], mem_norm_g)
    in_flight = {}
    for stage in range(2):
        in_flight[stage] = gather_start(stage, no_after, act)
        act = in_flight[stage][3]
    pullbacks = []
    for stage in range(N_STAGES):
        started = in_flight.pop(stage)
        landed = _exchange_wait(started, ["gather"] * len(started[1]), act, "gather_wait_%d" % stage)
        if stage + 2 < N_STAGES:
            in_flight[stage + 2] = gather_start(stage + 2, landed[0], act)
            act = in_flight[stage + 2][3]
        small = {n: given[n][idx] for n, idx in _stage_params(stage)[1]}
        act, pullback = jax.vjp(functools.partial(_stage_forward, stage), landed, small, act, mem_n)
        pullbacks.append(pullback)
    loss_local, loss_pullback = jax.vjp(_loss_rows, act, loss_target[0])
    d_act = loss_pullback(jnp.ones((), F32))[0]

    d_mem_n = jnp.zeros_like(mem_n)
    g_small = {}
    received = [None] * N_STAGES
    started, after = None, no_after
    for stage in reversed(range(N_STAGES)):
        d_landed, d_small, d_act, d_mem = pullbacks[stage](d_act)
        if stage % 2 == 1:
            d_mem_n = d_mem_n + d_mem
        for n, idx in _stage_params(stage)[1]:
            g_small[(n, idx)] = d_small[n]
        if started is not None:
            received[stage + 1] = _exchange_wait(started, ["scatter"] * len(started[1]), d_act,
                                                 "scatter_wait_%d" % (stage + 1))
            after = received[stage + 1][0]
        started = _exchange_start(list(d_landed), ["scatter"] * len(d_landed), after, "scatter_start_%d" % stage,
                                  d_act)
        d_act = started[3]
    g_small[('mem_norm_g', None)] = mem_pullback(d_mem_n)[1]

    def small_grad(n):
        if n == 'mem_norm_g':
            return g_small[(n, None)]
        return jnp.stack([g_small[(n, i)] for i in range(given[n].shape[0])], axis=0)

    packed = _exchange([_pack([small_grad(n) for n in REPLICATED])], ["gather"], "small_grads_allgather")[0]
    received[0] = _exchange_wait(started, ["scatter"] * len(started[1]), packed, "scatter_wait_0")

    results = {}
    for n in SHARDED:
        slots = {}
        for stage in range(N_STAGES):
            for (pn, idx), r in zip(_stage_params(stage)[0], received[stage]):
                if pn == n:
                    slots[idx] = r
        shape = given[n].shape
        to3d = lambda a: a.reshape(a.shape[0], -1, a.shape[-1])
        outs = _adam_call(to3d(given[n]), to3d(given['m_' + n]), to3d(given['v_' + n]),
                          [slots[i] for i in range(len(slots))], "adamw_" + n)
        results[n] = [o.reshape(shape) for o in outs]
    outs = _adam_call(_pack([given[n] for n in REPLICATED])[None], _pack([given['m_' + n] for n in REPLICATED])[None],
                      _pack([given['v_' + n] for n in REPLICATED])[None], [packed], "adamw_replicated")
    shapes = [given[n].shape for n in REPLICATED]
    for j, parts in enumerate(zip(*[_unpack(o[0], shapes) for o in outs])):
        results[REPLICATED[j]] = list(parts)

    loss = lax.psum(loss_local, ("x", "y", "c"))
    return (loss, d_act[None], *[results[n][0] for n in WEIGHT_NAMES], *[results[n][1] for n in WEIGHT_NAMES],
            *[results[n][2] for n in WEIGHT_NAMES], *[results[n][3] for n in WEIGHT_NAMES])
```

```python
import functools
import math

import jax
import jax.numpy as jnp
import numpy as np
from jax import lax
from jax.experimental import pallas as pl
from jax.experimental.pallas import tpu as pltpu

F32 = jnp.float32
BF16 = jnp.bfloat16
HI = lax.Precision.HIGHEST

N_DEV = 8
D_MODEL = 1024
SEQ = 2048
DEPTH = 4
CHUNK = 64
N_MEM = 256
RMS_EPS = 1e-6
S5_WIDTH = 512
S5_GROUP = 16
S5_GROUPS = 32
S5_STATE = 64
GDN_HEAD_DIM = 128
GDN_WIDTH = 512
GDN_HEADS = 4
GDN_CONV = 4
AB_IN = S5_WIDTH + 4 * GDN_WIDTH + 2 * GDN_HEADS
AB_IN_PAD = 2688
CA_HEADS = 16
CA_HEAD_DIM = 64
CA_LEFT = 8
CA_BAND = (CA_LEFT + 1) * CHUNK
CA_PAD = CA_LEFT * CHUNK
MAX_REL = 128
XA_HEADS = 4
XA_HEAD_DIM = 256
FFN = 2816
ADAM_LR, ADAM_B1, ADAM_B2, ADAM_EPS, ADAM_WD, ADAM_STEP = 0.001, 0.9, 0.999, 1e-08, 0.01, 10

VMEM_LIMIT = 48 * 1024 * 1024
LANE = 128
SUBLANE = 8


def _cparams(sem=None):
    return pltpu.CompilerParams(dimension_semantics=sem, vmem_limit_bytes=VMEM_LIMIT)


def _divisor_tile(n, target, unit=LANE):
    if n <= target:
        return n
    best = None
    for t in range(unit, target + 1, unit):
        if n % t == 0:
            best = t
    assert best is not None, (n, target)
    return best


def _matmul(a, b, *, ta=False, tb=False, out_dtype=F32, name="mm", res=None):
    if ta:
        k_dim, m_dim = a.shape
    else:
        m_dim, k_dim = a.shape
    if tb:
        n_dim, kb = b.shape
    else:
        kb, n_dim = b.shape
    assert kb == k_dim, (a.shape, b.shape, ta, tb)
    tm = _divisor_tile(m_dim, 1024)
    tn = _divisor_tile(n_dim, 512)
    tk = _divisor_tile(k_dim, 1408)
    nk = k_dim // tk
    dims = (((0 if ta else 1,), (1 if tb else 0,)), ((), ()))

    def body(a_ref, b_ref, *rest):
        res_ref = rest[0] if res is not None else None
        o_ref, acc_ref = rest[-2:]
        k = pl.program_id(2)

        @pl.when(k == 0)
        def _():
            acc_ref[...] = jnp.zeros_like(acc_ref)

        acc_ref[...] += lax.dot_general(a_ref[...].astype(BF16), b_ref[...].astype(BF16), dims,
                                        preferred_element_type=F32)

        @pl.when(k == nk - 1)
        def _():
            total = acc_ref[...] if res is None else acc_ref[...] + res_ref[...]
            o_ref[...] = total.astype(o_ref.dtype)

    a_spec = pl.BlockSpec((tk, tm), lambda i, j, k: (k, i)) if ta else pl.BlockSpec((tm, tk), lambda i, j, k: (i, k))
    b_spec = pl.BlockSpec((tn, tk), lambda i, j, k: (j, k)) if tb else pl.BlockSpec((tk, tn), lambda i, j, k: (k, j))
    o_spec = pl.BlockSpec((tm, tn), lambda i, j, k: (i, j))
    return pl.pallas_call(
        body,
        grid=(m_dim // tm, n_dim // tn, nk),
        in_specs=[a_spec, b_spec] + ([o_spec] if res is not None else []),
        out_specs=o_spec,
        out_shape=jax.ShapeDtypeStruct((m_dim, n_dim), out_dtype),
        scratch_shapes=[pltpu.VMEM((tm, tn), F32)],
        compiler_params=_cparams(("parallel", "parallel", "arbitrary")),
        name=name,
    )(*((a, b) if res is None else (a, b, res)))


@jax.custom_vjp
def linear(a, w):
    return _matmul(a, w, name="linear_fwd")


def _linear_fwd(a, w):
    return _matmul(a, w, name="linear_fwd"), (a, w)


def _linear_bwd(res, dy):
    a, w = res
    da = _matmul(dy, w, tb=True, name="linear_da")
    dw = _matmul(a, dy, ta=True, out_dtype=w.dtype, name="linear_dw")
    return da, dw


linear.defvjp(_linear_fwd, _linear_bwd)


@jax.custom_vjp
def linear_res(a, w, x):
    return _matmul(a, w, name="linear_res_fwd", res=x)


def _linear_res_fwd(a, w, x):
    return _matmul(a, w, name="linear_res_fwd", res=x), (a, w)


def _linear_res_bwd(res, dy):
    return _linear_bwd(res, dy) + (dy,)


linear_res.defvjp(_linear_res_fwd, _linear_res_bwd)


def _mm_call(name, a, b, out_struct, grid, a_spec, b_spec, o_spec, dims, lead, res=None, keep_a=None):
    nk = grid[-1]
    acc_shape = o_spec.block_shape[1:] if lead[2] else o_spec.block_shape
    a_shape = a_spec.block_shape[1:] if lead[0] else a_spec.block_shape
    assert keep_a is None or nk == 1

    def body(a_ref, b_ref, *rest):
        res_ref = rest[0] if res is not None else None
        o_ref, acc_ref = rest[-1 - (keep_a is not None) - 1], rest[-1 - (keep_a is not None)]
        k = pl.program_id(len(grid) - 1)

        @pl.when(k == 0)
        def _():
            acc_ref[...] = jnp.zeros_like(acc_ref)

        if keep_a is None:
            av = (a_ref[0] if lead[0] else a_ref[...]).astype(BF16)
        else:
            a16_ref = rest[-1]

            @pl.when(pl.program_id(keep_a) == 0)
            def _():
                a16_ref[...] = (a_ref[0] if lead[0] else a_ref[...]).astype(BF16)

            av = a16_ref[...]
        bv = b_ref[0] if lead[1] else b_ref[...]
        acc_ref[...] += lax.dot_general(av, bv.astype(BF16), dims, preferred_element_type=F32)

        @pl.when(k == nk - 1)
        def _():
            if lead[2]:
                o_ref[0] = acc_ref[...].astype(o_ref.dtype)
            elif res is not None:
                o_ref[...] = (acc_ref[...] + res_ref[...]).astype(o_ref.dtype)
            else:
                o_ref[...] = acc_ref[...].astype(o_ref.dtype)

    return pl.pallas_call(
        body, grid=grid, in_specs=[a_spec, b_spec] + ([o_spec] if res is not None else []), out_specs=o_spec,
        out_shape=out_struct,
        scratch_shapes=[pltpu.VMEM(tuple(acc_shape), F32)] + ([pltpu.VMEM(tuple(a_shape), BF16)] if keep_a is not None else []),
        compiler_params=_cparams(("parallel", "arbitrary", "arbitrary")), name=name,
    )(*((a, b) if res is None else (a, b, res)))


_NN = (((1,), (0,)), ((), ()))
_NT_DIMS = (((1,), (1,)), ((), ()))
_TN_DIMS = (((0,), (0,)), ((), ()))


def _cols_fwd(a, g, dm_out):
    m_dim, k_dim = a.shape
    _, _, c_dim = g.shape
    tm = _divisor_tile(m_dim, 1024)
    tk = _divisor_tile(k_dim, 1024)
    a_spec = pl.BlockSpec((tm, tk), lambda i, j, k: (i, k))
    b_spec = pl.BlockSpec((1, tk, c_dim), lambda i, j, k: (j, k, 0))
    if dm_out:
        o_spec = pl.BlockSpec((1, tm, c_dim), lambda i, j, k: (j, i, 0))
        out = jax.ShapeDtypeStruct((N_DEV, m_dim, c_dim), BF16)
    else:
        o_spec = pl.BlockSpec((tm, c_dim), lambda i, j, k: (i, j))
        out = jax.ShapeDtypeStruct((m_dim, N_DEV * c_dim), F32)
    return _mm_call("cols_fwd", a, g, out, (m_dim // tm, N_DEV, k_dim // tk), a_spec, b_spec, o_spec, _NN,
                    (False, True, dm_out), keep_a=1 if k_dim == tk else None)


def _cols_da(dy, g, dm_out):
    _, k_dim, c_dim = g.shape
    m_dim = dy.shape[1] if dm_out else dy.shape[0]
    tm = _divisor_tile(m_dim, 1024)
    tk = _divisor_tile(k_dim, 1024)
    if dm_out:
        a_spec = pl.BlockSpec((1, tm, c_dim), lambda i, kb, j: (j, i, 0))
    else:
        a_spec = pl.BlockSpec((tm, c_dim), lambda i, kb, j: (i, j))
    b_spec = pl.BlockSpec((1, tk, c_dim), lambda i, kb, j: (j, kb, 0))
    o_spec = pl.BlockSpec((tm, tk), lambda i, kb, j: (i, kb))
    return _mm_call("cols_da", dy, g, jax.ShapeDtypeStruct((m_dim, k_dim), F32), (m_dim // tm, k_dim // tk, N_DEV),
                    a_spec, b_spec, o_spec, _NT_DIMS, (dm_out, True, False))


def _cols_dg(a, dy, g, dm_out):
    _, k_dim, c_dim = g.shape
    m_dim = a.shape[0]
    tm = _divisor_tile(m_dim, 2048)
    tk = _divisor_tile(k_dim, 512)
    a_spec = pl.BlockSpec((tm, tk), lambda kb, j, m: (m, kb))
    if dm_out:
        b_spec = pl.BlockSpec((1, tm, c_dim), lambda kb, j, m: (j, m, 0))
    else:
        b_spec = pl.BlockSpec((tm, c_dim), lambda kb, j, m: (m, j))
    o_spec = pl.BlockSpec((1, tk, c_dim), lambda kb, j, m: (j, kb, 0))
    return _mm_call("cols_dg", a, dy, jax.ShapeDtypeStruct(g.shape, g.dtype), (k_dim // tk, N_DEV, m_dim // tm),
                    a_spec, b_spec, o_spec, _TN_DIMS, (False, dm_out, True), keep_a=1 if m_dim == tm else None)


def _make_linear_cols(dm_out):
    @jax.custom_vjp
    def op(a, g):
        return _cols_fwd(a, g, dm_out)

    def fwd(a, g):
        return _cols_fwd(a, g, dm_out), (a, g)

    def bwd(res, dy):
        a, g = res
        return _cols_da(dy, g, dm_out), _cols_dg(a, dy, g, dm_out)

    op.defvjp(fwd, bwd)
    return op


linear_cols = _make_linear_cols(False)
linear_cols_dm = _make_linear_cols(True)


def _silu_mul(g, u):
    return _silu(g) * u


def _ffn_down_fwd(g, u, w, x):
    _, m_dim, r_dim = g.shape
    n_dim = w.shape[2]
    tm = _divisor_tile(m_dim, 1024)
    tn = _divisor_tile(n_dim, 1024)
    nj = N_DEV

    def body(g_ref, u_ref, w_ref, x_ref, o_ref, acc_ref):
        j = pl.program_id(2)

        @pl.when(j == 0)
        def _():
            acc_ref[...] = jnp.zeros_like(acc_ref)

        acc_ref[...] += _bdot(_silu_mul(g_ref[0].astype(F32), u_ref[0].astype(F32)), w_ref[0])

        @pl.when(j == nj - 1)
        def _():
            o_ref[...] = acc_ref[...] + x_ref[...]

    h_spec = pl.BlockSpec((1, tm, r_dim), lambda i, n, j: (j, i, 0))
    o_spec = pl.BlockSpec((tm, tn), lambda i, n, j: (i, n))
    return pl.pallas_call(
        body, grid=(m_dim // tm, n_dim // tn, nj),
        in_specs=[h_spec, h_spec, pl.BlockSpec((1, r_dim, tn), lambda i, n, j: (j, 0, n)), o_spec], out_specs=o_spec,
        out_shape=jax.ShapeDtypeStruct((m_dim, n_dim), F32), scratch_shapes=[pltpu.VMEM((tm, tn), F32)],
        compiler_params=_cparams(("parallel", "parallel", "arbitrary")), name="ffn_down_fwd",
    )(g, u, w, x)


def _ffn_down_dh(dy, g, u, w):
    m_dim, n_dim = dy.shape
    r_dim = w.shape[1]
    tm = _divisor_tile(m_dim, 1024)
    tn = _divisor_tile(n_dim, 1024)
    nn = n_dim // tn

    def body(dy_ref, w_ref, g_ref, u_ref, dg_ref, du_ref, acc_ref):
        n = pl.program_id(2)

        @pl.when(n == 0)
        def _():
            acc_ref[...] = jnp.zeros_like(acc_ref)

        acc_ref[...] += _bdot(dy_ref[...], w_ref[0], _NT_DIMS)

        @pl.when(n == nn - 1)
        def _():
            _, pullback = jax.vjp(_silu_mul, g_ref[0].astype(F32), u_ref[0].astype(F32))
            dg, du = pullback(acc_ref[...])
            dg_ref[0] = dg.astype(dg_ref.dtype)
            du_ref[0] = du.astype(du_ref.dtype)

    h_spec = pl.BlockSpec((1, tm, r_dim), lambda i, j, n: (j, i, 0))
    return pl.pallas_call(
        body, grid=(m_dim // tm, N_DEV, nn),
        in_specs=[pl.BlockSpec((tm, tn), lambda i, j, n: (i, n)), pl.BlockSpec((1, r_dim, tn), lambda i, j, n: (j, 0, n)),
                  h_spec, h_spec],
        out_specs=[h_spec, h_spec], out_shape=[jax.ShapeDtypeStruct(g.shape, g.dtype)] * 2,
        scratch_shapes=[pltpu.VMEM((tm, r_dim), F32)],
        compiler_params=_cparams(("parallel", "arbitrary", "arbitrary")), name="ffn_down_dh",
    )(dy, w, g, u)


def _ffn_down_dw(g, u, dy, w):
    _, m_dim, r_dim = g.shape
    n_dim = dy.shape[1]
    tm = _divisor_tile(m_dim, 2048)
    tn = _divisor_tile(n_dim, 512)
    nm = m_dim // tm

    def body(g_ref, u_ref, dy_ref, o_ref, acc_ref):
        m = pl.program_id(2)

        @pl.when(m == 0)
        def _():
            acc_ref[...] = jnp.zeros_like(acc_ref)

        acc_ref[...] += _bdot(_silu_mul(g_ref[0].astype(F32), u_ref[0].astype(F32)), dy_ref[...], _TN_DIMS)

        @pl.when(m == nm - 1)
        def _():
            o_ref[0] = acc_ref[...].astype(o_ref.dtype)

    h_spec = pl.BlockSpec((1, tm, r_dim), lambda j, n, m: (j, m, 0))
    return pl.pallas_call(
        body, grid=(N_DEV, n_dim // tn, nm),
        in_specs=[h_spec, h_spec, pl.BlockSpec((tm, tn), lambda j, n, m: (m, n))],
        out_specs=pl.BlockSpec((1, r_dim, tn), lambda j, n, m: (j, 0, n)),
        out_shape=jax.ShapeDtypeStruct(w.shape, w.dtype), scratch_shapes=[pltpu.VMEM((r_dim, tn), F32)],
        compiler_params=_cparams(("parallel", "parallel", "arbitrary")), name="ffn_down_dw",
    )(g, u, dy)


@jax.custom_vjp
def ffn_down(g, u, w, x):
    return _ffn_down_fwd(g, u, w, x)


def _ffn_down_vjp_fwd(g, u, w, x):
    return _ffn_down_fwd(g, u, w, x), (g, u, w)


def _ffn_down_vjp_bwd(res, dy):
    g, u, w = res
    dg, du = _ffn_down_dh(dy, g, u, w)
    return dg, du, _ffn_down_dw(g, u, dy, w), dy


ffn_down.defvjp(_ffn_down_vjp_fwd, _ffn_down_vjp_bwd)


def _cols_to_natural_call(g, width):
    _, k_dim, c_dim = g.shape
    tk = _divisor_tile(k_dim, 256, SUBLANE)

    def body(g_ref, o_ref):
        for j in range(N_DEV):
            o_ref[:, j * c_dim:(j + 1) * c_dim] = g_ref[j]
        if width > N_DEV * c_dim:
            o_ref[:, N_DEV * c_dim:] = jnp.zeros((tk, width - N_DEV * c_dim), o_ref.dtype)

    return pl.pallas_call(
        body, grid=(k_dim // tk,), in_specs=[pl.BlockSpec((N_DEV, tk, c_dim), lambda i: (0, i, 0))],
        out_specs=pl.BlockSpec((tk, width), lambda i: (i, 0)), out_shape=jax.ShapeDtypeStruct((k_dim, width), g.dtype),
        compiler_params=_cparams(("parallel",)), name="cols_to_natural",
    )(g)


def _natural_to_cols_call(w, c_dim):
    k_dim, width = w.shape
    tk = _divisor_tile(k_dim, 256, SUBLANE)

    def body(w_ref, o_ref):
        for j in range(N_DEV):
            o_ref[j] = w_ref[:, j * c_dim:(j + 1) * c_dim]

    return pl.pallas_call(
        body, grid=(k_dim // tk,), in_specs=[pl.BlockSpec((tk, width), lambda i: (i, 0))],
        out_specs=pl.BlockSpec((N_DEV, tk, c_dim), lambda i: (0, i, 0)),
        out_shape=jax.ShapeDtypeStruct((N_DEV, k_dim, c_dim), w.dtype),
        compiler_params=_cparams(("parallel",)), name="natural_to_cols",
    )(w)


@functools.partial(jax.custom_vjp, nondiff_argnums=(1,))
def cols_to_natural(g, width):
    return _cols_to_natural_call(g, width)


def _cols_to_natural_fwd(g, width):
    return _cols_to_natural_call(g, width), g.shape[2]


def _cols_to_natural_bwd(width, c_dim, dw):
    return (_natural_to_cols_call(dw, c_dim),)


cols_to_natural.defvjp(_cols_to_natural_fwd, _cols_to_natural_bwd)


def make_rowop(fn, name, tm=256, passthrough=0):
    def specs(rows, params):
        row_specs = [pl.BlockSpec((tm, r.shape[1]), lambda i: (i, 0)) for r in rows]
        par_specs = [pl.BlockSpec(p.shape, lambda i: (0, 0)) for p in params]
        return row_specs, par_specs

    def out_structs(rows, params):
        tiles = [jax.ShapeDtypeStruct((tm, r.shape[1]), r.dtype) for r in rows]
        return jax.eval_shape(lambda r, p: fn(*r, *p), tiles, list(params))

    def fwd_call(rows, params):
        m_dim = rows[0].shape[0]
        n_in = len(rows) + len(params)
        outs = out_structs(rows, params)

        def body(*refs):
            res = fn(*[r[...] for r in refs[:n_in]])
            for o_ref, r in zip(refs[n_in:], res):
                o_ref[...] = r.astype(o_ref.dtype)

        row_specs, par_specs = specs(rows, params)
        return pl.pallas_call(
            body,
            grid=(m_dim // tm,),
            in_specs=row_specs + par_specs,
            out_specs=[pl.BlockSpec((tm, o.shape[1]), lambda i: (i, 0)) for o in outs],
            out_shape=[jax.ShapeDtypeStruct((m_dim, o.shape[1]), o.dtype) for o in outs],
            compiler_params=_cparams(("parallel",)),
            name=name + "_fwd",
        )(*rows, *params)

    def bwd_call(rows, params, cts):
        m_dim = rows[0].shape[0]
        n_rows, n_par = len(rows), len(params)
        n_in = n_rows + n_par
        n_ct = len(cts)
        n_fn = n_ct - passthrough

        def body(*refs):
            vals = [r[...] for r in refs[:n_in]]
            ct_vals = tuple(r[...] for r in refs[n_in:n_in + n_fn])
            pass_refs = refs[n_in + n_fn:n_in + n_ct]
            drow_refs = refs[n_in + n_ct:n_in + n_ct + n_rows]
            dpar_refs = refs[n_in + n_ct + n_rows:]
            _, pullback = jax.vjp(fn, *vals)
            grads = pullback(ct_vals)
            for i, (d_ref, g) in enumerate(zip(drow_refs, grads[:n_rows])):
                d_ref[...] = g + pass_refs[i][...] if i < passthrough else g

            @pl.when(pl.program_id(0) == 0)
            def _():
                for d_ref in dpar_refs:
                    d_ref[...] = jnp.zeros_like(d_ref)

            for d_ref, g in zip(dpar_refs, grads[n_rows:]):
                d_ref[...] += g

        row_specs, par_specs = specs(rows, params)
        ct_specs = [pl.BlockSpec((tm, c.shape[1]), lambda i: (i, 0)) for c in cts]
        res = pl.pallas_call(
            body,
            grid=(m_dim // tm,),
            in_specs=row_specs + par_specs + ct_specs,
            out_specs=row_specs + par_specs,
            out_shape=[jax.ShapeDtypeStruct(r.shape, r.dtype) for r in rows]
            + [jax.ShapeDtypeStruct(p.shape, p.dtype) for p in params],
            compiler_params=_cparams(("arbitrary",)),
            name=name + "_bwd",
        )(*rows, *params, *cts)
        return tuple(res[:n_rows]), tuple(res[n_rows:])

    @jax.custom_vjp
    def op(rows, params):
        return tuple(fwd_call(rows, params)) + tuple(rows[:passthrough])

    def op_fwd(rows, params):
        return tuple(fwd_call(rows, params)) + tuple(rows[:passthrough]), (rows, params)

    def op_bwd(res, cts):
        rows, params = res
        return bwd_call(rows, params, tuple(cts))

    op.defvjp(op_fwd, op_bwd)
    return op


def _rms(x, g):
    return x * lax.rsqrt(jnp.mean(x * x, axis=-1, keepdims=True) + RMS_EPS) * g


def _sigmoid(x):
    return 1.0 / (1.0 + jnp.exp(-x))


def _silu(x):
    return x * _sigmoid(x)


def _bdot(a, b, dims=(((1,), (0,)), ((), ()))):
    return lax.dot_general(a.astype(BF16), b.astype(BF16), dims, preferred_element_type=F32)


def _rmsnorm_fn(x, g):
    return (_rms(x, g),)


def rmsnorm(x, g, name):
    return make_rowop(_rmsnorm_fn, name)((x,), (g.reshape(1, -1),))[0]


def rmsnorm_res(x, g, name):
    return make_rowop(_rmsnorm_fn, name, passthrough=1)((x,), (g.reshape(1, -1),))


def _gelu_tanh(x):
    return 0.5 * x * (1.0 + jnp.tanh(0.7978845608028654 * (x + 0.044715 * x * x * x)))


def _softplus(x):
    return jnp.maximum(x, 0.0) + jnp.log(1.0 + jnp.exp(-jnp.abs(x)))


def _s5_post_fn(y, w_glu, b_glu):
    h = _gelu_tanh(y)
    return (h * _sigmoid(_bdot(h, w_glu) + b_glu),)


def _loss_fn(y, t):
    err = y - t
    return (0.5 * jnp.mean(err * err, axis=-1, keepdims=True),)


def _pair_headnorm(x, g2):
    lo = lax.broadcasted_iota(jnp.int32, x.shape, 1) < CA_HEAD_DIM
    sq = x * x
    s_lo = jnp.sum(jnp.where(lo, sq, 0.0), axis=-1, keepdims=True)
    s_hi = jnp.sum(jnp.where(lo, 0.0, sq), axis=-1, keepdims=True)
    ms = jnp.where(lo, s_lo, s_hi) * (1.0 / CA_HEAD_DIM)
    return x * lax.rsqrt(ms + RMS_EPS) * g2


def _ca_qknorm_fn(qkv, qg2, kg2):
    qs, ks = [], []
    for j in range(D_MODEL // LANE):
        qs.append(_pair_headnorm(qkv[:, j * LANE:(j + 1) * LANE], qg2))
        ks.append(_pair_headnorm(qkv[:, D_MODEL + j * LANE:D_MODEL + (j + 1) * LANE], kg2))
    return jnp.concatenate(qs, axis=1), jnp.concatenate(ks, axis=1)


def _xattn_fn(q, k, v, qg, kg):
    outs = []
    for h in range(XA_HEADS):
        sl = slice(h * XA_HEAD_DIM, (h + 1) * XA_HEAD_DIM)
        qh = _rms(q[:, sl], qg)
        kh = _rms(k[:, sl], kg)
        s = _bdot(qh, kh, (((1,), (1,)), ((), ()))) * (XA_HEAD_DIM ** -0.5)
        p = jnp.exp(s - jnp.max(s, axis=-1, keepdims=True))
        p = p / jnp.sum(p, axis=-1, keepdims=True)
        outs.append(_bdot(p, v[:, sl]))
    return (jnp.concatenate(outs, axis=1),)


def _gdn_prep_fn(x0, x1, x2, x3, ab, conv_w, alog, dtb):
    c = conv_w[3:4, :] * x0 + conv_w[2:3, :] * x1 + conv_w[1:2, :] * x2 + conv_w[0:1, :] * x3
    c = _silu(c)
    qs, ks = [], []
    for h in range(GDN_HEADS):
        qh = c[:, h * LANE:(h + 1) * LANE]
        kh = c[:, GDN_WIDTH + h * LANE:GDN_WIDTH + (h + 1) * LANE]
        qs.append(qh * lax.rsqrt(jnp.sum(qh * qh, axis=-1, keepdims=True) + RMS_EPS) * (GDN_HEAD_DIM ** -0.5))
        ks.append(kh * lax.rsqrt(jnp.sum(kh * kh, axis=-1, keepdims=True) + RMS_EPS))
    lane = lax.broadcasted_iota(jnp.int32, ab.shape, 1)
    g = -jnp.exp(alog) * _softplus(ab + dtb)
    beta = _sigmoid(ab)
    bg = jnp.where(lane < GDN_HEADS, g, jnp.where(lane < 2 * GDN_HEADS, beta, 0.0))
    return jnp.concatenate(qs, axis=1), jnp.concatenate(ks, axis=1), c[:, 2 * GDN_WIDTH:], bg


def _gdn_out_fn(o, gate, og):
    outs = []
    for h in range(GDN_HEADS):
        sl = slice(h * LANE, (h + 1) * LANE)
        outs.append(_rms(o[:, sl], og) * _silu(gate[:, sl]))
    return (jnp.concatenate(outs, axis=1),)


CA_QB = 4 * CHUNK
CA_KB = CA_QB + CA_PAD


def _ca_math(q2, kb2, vb2, bias2, c):
    lane = lax.broadcasted_iota(jnp.int32, q2.shape, 1)
    qc = lax.broadcasted_iota(jnp.int32, (CA_QB, CA_KB), 0) // CHUNK
    kc = lax.broadcasted_iota(jnp.int32, (CA_QB, CA_KB), 1) // CHUNK
    valid = (kc >= qc) & (kc <= qc + CA_LEFT) & (kc + c * (CA_QB // CHUNK) >= CA_LEFT)
    out = jnp.zeros(q2.shape, F32)
    for h in range(2):
        mine = (lane >= h * CA_HEAD_DIM) & (lane < (h + 1) * CA_HEAD_DIM)
        qh = jnp.where(mine, q2, 0.0)
        s = _bdot(qh, kb2, (((1,), (1,)), ((), ()))) * (CA_HEAD_DIM ** -0.5) + bias2[h]
        s = jnp.where(valid, s, -1e30)
        p = jnp.exp(s - jnp.max(s, axis=-1, keepdims=True))
        p = p / jnp.sum(p, axis=-1, keepdims=True)
        out = out + jnp.where(mine, _bdot(p, vb2), 0.0)
    return out


CA_VEC = CA_QB + CA_KB


def _ca_specs(seq):
    q_spec = pl.BlockSpec((CA_QB, LANE), lambda hp, c: (c, hp))
    kv_spec = pl.BlockSpec((seq + CA_PAD, LANE), lambda hp, c: (0, hp))
    b_spec = pl.BlockSpec((1, 2, CA_VEC), lambda hp, c: (hp, 0, 0))
    return (D_MODEL // LANE, seq // CA_QB), q_spec, kv_spec, b_spec


def _ca_bias_from_vector(vec_ref, bias_ref):
    for h in range(2):
        rows = jnp.broadcast_to(vec_ref[0, h:h + 1, :], (CA_QB, CA_VEC))
        bias_ref[h] = pltpu.roll(rows, 0, 1, stride=1, stride_axis=0)[:, CA_QB:]


def _ca_vector_grad(dbias):
    d = jnp.concatenate([jnp.zeros((CA_QB, CA_QB), F32), dbias], axis=1)
    row = lax.broadcasted_iota(jnp.int32, d.shape, 0)
    for bit in range(CA_QB.bit_length() - 1):
        d = jnp.where((row >> bit) & 1 == 1, pltpu.roll(d, CA_VEC - (1 << bit), 1), d)
    return jnp.sum(d, axis=0, keepdims=True)


def _ca_fwd_call(q, kpad, vpad, vec):
    grid, q_spec, kv_spec, b_spec = _ca_specs(q.shape[0])

    def body(q_ref, k_ref, v_ref, vec_ref, o_ref, bias_ref):
        c = pl.program_id(1)
        start = pl.multiple_of(c * CA_QB, CA_QB)

        @pl.when(c == 0)
        def _():
            _ca_bias_from_vector(vec_ref, bias_ref)

        o_ref[...] = _ca_math(q_ref[...], k_ref[pl.ds(start, CA_KB), :], v_ref[pl.ds(start, CA_KB), :],
                              bias_ref[...], c)

    return pl.pallas_call(
        body, grid=grid, in_specs=[q_spec, kv_spec, kv_spec, b_spec], out_specs=q_spec,
        out_shape=jax.ShapeDtypeStruct(q.shape, F32), scratch_shapes=[pltpu.VMEM((2, CA_QB, CA_KB), F32)],
        compiler_params=_cparams(("parallel", "arbitrary")), name="chunkattn_fwd",
    )(q, kpad, vpad, vec)


def _ca_bwd_call(q, kpad, vpad, vec, do):
    grid, q_spec, kv_spec, b_spec = _ca_specs(q.shape[0])
    last = grid[1] - 1

    def body(q_ref, k_ref, v_ref, vec_ref, do_ref, dq_ref, dk_ref, dv_ref, dvec_ref, bias_ref, dbias_ref):
        c = pl.program_id(1)
        start = pl.multiple_of(c * CA_QB, CA_QB)

        @pl.when(c == 0)
        def _():
            _ca_bias_from_vector(vec_ref, bias_ref)
            dk_ref[...] = jnp.zeros_like(dk_ref)
            dv_ref[...] = jnp.zeros_like(dv_ref)
            dbias_ref[...] = jnp.zeros_like(dbias_ref)

        _, pullback = jax.vjp(lambda a, b, d, e: _ca_math(a, b, d, e, c), q_ref[...],
                              k_ref[pl.ds(start, CA_KB), :], v_ref[pl.ds(start, CA_KB), :], bias_ref[...])
        dq, dkb, dvb, dbias = pullback(do_ref[...])
        dq_ref[...] = dq
        dk_ref[pl.ds(start, CA_KB), :] += dkb
        dv_ref[pl.ds(start, CA_KB), :] += dvb
        dbias_ref[...] += dbias

        @pl.when(c == last)
        def _():
            for h in range(2):
                dvec_ref[0, h:h + 1, :] = _ca_vector_grad(dbias_ref[h])

    return pl.pallas_call(
        body, grid=grid, in_specs=[q_spec, kv_spec, kv_spec, b_spec, q_spec],
        out_specs=[q_spec, kv_spec, kv_spec, b_spec],
        out_shape=[jax.ShapeDtypeStruct(q.shape, F32), jax.ShapeDtypeStruct(kpad.shape, F32),
                   jax.ShapeDtypeStruct(vpad.shape, F32), jax.ShapeDtypeStruct(vec.shape, F32)],
        scratch_shapes=[pltpu.VMEM((2, CA_QB, CA_KB), F32), pltpu.VMEM((2, CA_QB, CA_KB), F32)],
        compiler_params=_cparams(("parallel", "arbitrary")), name="chunkattn_bwd",
    )(q, kpad, vpad, vec, do)


@jax.custom_vjp
def chunk_attn_core(q, kpad, vpad, vec):
    return _ca_fwd_call(q, kpad, vpad, vec)


def _ca_core_fwd(q, kpad, vpad, vec):
    return _ca_fwd_call(q, kpad, vpad, vec), (q, kpad, vpad, vec)


def _ca_core_bwd(res, do):
    return tuple(_ca_bwd_call(*res, do))


chunk_attn_core.defvjp(_ca_core_fwd, _ca_core_bwd)


S5_GB = 4
S5_U = S5_WIDTH // S5_GB
S5_L = S5_GROUPS * S5_STATE // S5_GB


def _cmul(ar, ai, br, bi):
    return ar * br - ai * bi, ar * bi + ai * br


def _hdot(a, b, dims=(((1,), (0,)), ((), ()))):
    return lax.dot_general(a, b, dims, precision=HI, preferred_element_type=F32)


def _split_dot(a, b, dims=(((1,), (0,)), ((), ()))):
    a_hi, b_hi = a.astype(BF16), b.astype(BF16)
    a_lo = (a - a_hi.astype(F32)).astype(BF16)
    b_lo = (b - b_hi.astype(F32)).astype(BF16)
    dot = functools.partial(lax.dot_general, dimension_numbers=dims, preferred_element_type=F32)
    return dot(a_hi, b_hi) + (dot(a_hi, b_lo) + dot(a_lo, b_hi))


_NT = (((1,), (1,)), ((), ()))
_TN = (((0,), (0,)), ((), ()))


def _s5_tables(lr, li, reverse):
    p = {1: (lr, li)}
    p[2] = _cmul(*p[1], *p[1])
    p[4] = _cmul(*p[2], *p[2])
    p[3] = _cmul(*p[2], *p[1])
    p[5] = _cmul(*p[4], *p[1])
    p[6] = _cmul(*p[4], *p[2])
    p[7] = _cmul(*p[4], *p[3])
    p[8] = _cmul(*p[4], *p[4])
    row = lax.broadcasted_iota(jnp.int32, (SUBLANE, lr.shape[1]), 0)
    tr = jnp.zeros(row.shape, F32)
    ti = jnp.zeros(row.shape, F32)
    for i in range(SUBLANE):
        k = SUBLANE - i if reverse else i + 1
        tr = jnp.where(row == i, p[k][0], tr)
        ti = jnp.where(row == i, p[k][1], ti)
    return p, (tr, ti), row


def _s5_block_scan(xr, xi, p, tab, row, hr, hi, reverse):
    for k in (1, 2, 4):
        if reverse:
            sr = jnp.where(row < SUBLANE - k, pltpu.roll(xr, SUBLANE - k, 0), 0.0)
            si = jnp.where(row < SUBLANE - k, pltpu.roll(xi, SUBLANE - k, 0), 0.0)
        else:
            sr = jnp.where(row >= k, pltpu.roll(xr, k, 0), 0.0)
            si = jnp.where(row >= k, pltpu.roll(xi, k, 0), 0.0)
        ar, ai = _cmul(p[k][0], p[k][1], sr, si)
        xr, xi = xr + ar, xi + ai
    cr, ci = _cmul(tab[0], tab[1], hr, hi)
    return xr + cr, xi + ci


def _s5_forward_scan(sr_ref, si_ref, lr, li):
    n_blocks = sr_ref.shape[0] // SUBLANE
    p, tab, row = _s5_tables(lr, li, False)

    def step(b, carry):
        base = pl.multiple_of(b * SUBLANE, SUBLANE)
        xr, xi = _s5_block_scan(sr_ref[pl.ds(base, SUBLANE), :], si_ref[pl.ds(base, SUBLANE), :],
                                p, tab, row, carry[0], carry[1], False)
        sr_ref[pl.ds(base, SUBLANE), :] = xr
        si_ref[pl.ds(base, SUBLANE), :] = xi
        return xr[SUBLANE - 1:SUBLANE, :], xi[SUBLANE - 1:SUBLANE, :]

    zero = jnp.zeros((1, lr.shape[1]), F32)
    lax.fori_loop(0, n_blocks, step, (zero, zero))


def _s5_specs(seq):
    u_spec = pl.BlockSpec((seq, S5_U), lambda g: (0, g))
    bd_spec = pl.BlockSpec((1, S5_U, S5_L), lambda g: (g, 0, 0))
    cd_spec = pl.BlockSpec((1, S5_L, S5_U), lambda g: (g, 0, 0))
    lam_spec = pl.BlockSpec((1, 2, S5_L), lambda g: (g, 0, 0))
    d_spec = pl.BlockSpec((1, S5_U), lambda g: (0, g))
    return u_spec, bd_spec, cd_spec, lam_spec, d_spec


S5_ROWS = 256


def _row_chunks(seq, fn):
    rows_per = min(S5_ROWS, seq)

    def step(r, carry):
        fn(pl.ds(pl.multiple_of(r * rows_per, rows_per), rows_per))
        return carry

    lax.fori_loop(0, seq // rows_per, step, 0)


def _s5_fwd_call(u, bdr, bdi, cdr, cdi, lam, d):
    seq = u.shape[0]
    u_spec, bd_spec, cd_spec, lam_spec, d_spec = _s5_specs(seq)

    def body(u_ref, bdr_ref, bdi_ref, cdr_ref, cdi_ref, lam_ref, d_ref, y_ref, sr_ref, si_ref):
        def project_in(rows):
            uv = u_ref[rows, :]
            sr_ref[rows, :] = _bdot(uv, bdr_ref[0])
            si_ref[rows, :] = _bdot(uv, bdi_ref[0])

        def project_out(rows):
            y_ref[rows, :] = (_bdot(sr_ref[rows, :], cdr_ref[0]) - _bdot(si_ref[rows, :], cdi_ref[0])
                              + d_ref[...] * u_ref[rows, :])

        _row_chunks(seq, project_in)
        _s5_forward_scan(sr_ref, si_ref, lam_ref[0, 0:1, :], lam_ref[0, 1:2, :])
        _row_chunks(seq, project_out)

    return pl.pallas_call(
        body, grid=(S5_GB,), in_specs=[u_spec, bd_spec, bd_spec, cd_spec, cd_spec, lam_spec, d_spec],
        out_specs=u_spec, out_shape=jax.ShapeDtypeStruct(u.shape, F32),
        scratch_shapes=[pltpu.VMEM((seq, S5_L), F32), pltpu.VMEM((seq, S5_L), F32)],
        compiler_params=_cparams(("parallel",)), name="s5_fwd",
    )(u, bdr, bdi, cdr, cdi, lam, d)


def _s5_bwd_call(u, bdr, bdi, cdr, cdi, lam, d, dy):
    seq = u.shape[0]
    n_blocks = seq // SUBLANE
    u_spec, bd_spec, cd_spec, lam_spec, d_spec = _s5_specs(seq)

    def body(u_ref, bdr_ref, bdi_ref, cdr_ref, cdi_ref, lam_ref, d_ref, dy_ref,
             du_ref, dbdr_ref, dbdi_ref, dcdr_ref, dcdi_ref, dlam_ref, dd_ref, sr_ref, si_ref, gr_ref, gi_ref):
        lr, li = lam_ref[0, 0:1, :], lam_ref[0, 1:2, :]

        def project_in(rows):
            uv = u_ref[rows, :]
            dyv = dy_ref[rows, :]
            sr_ref[rows, :] = _bdot(uv, bdr_ref[0])
            si_ref[rows, :] = _bdot(uv, bdi_ref[0])
            gr_ref[rows, :] = _bdot(dyv, cdr_ref[0], _NT)
            gi_ref[rows, :] = -_bdot(dyv, cdi_ref[0], _NT)

        _row_chunks(seq, project_in)
        _s5_forward_scan(sr_ref, si_ref, lr, li)
        p, tab, row = _s5_tables(lr, -li, True)

        def step(i, carry):
            hr, hi, acc_r, acc_i = carry
            b = n_blocks - 1 - i
            base = pl.multiple_of(b * SUBLANE, SUBLANE)
            xr, xi = _s5_block_scan(gr_ref[pl.ds(base, SUBLANE), :], gi_ref[pl.ds(base, SUBLANE), :],
                                    p, tab, row, hr, hi, True)
            gr_ref[pl.ds(base, SUBLANE), :] = xr
            gi_ref[pl.ds(base, SUBLANE), :] = xi
            prev = pl.multiple_of(jnp.maximum(b - 1, 0) * SUBLANE, SUBLANE)
            keep = (b > 0).astype(F32)
            last_r = sr_ref[pl.ds(prev, SUBLANE), :][SUBLANE - 1:SUBLANE, :] * keep
            last_i = si_ref[pl.ds(prev, SUBLANE), :][SUBLANE - 1:SUBLANE, :] * keep
            pr = jnp.where(row >= 1, pltpu.roll(sr_ref[pl.ds(base, SUBLANE), :], 1, 0), last_r)
            pi = jnp.where(row >= 1, pltpu.roll(si_ref[pl.ds(base, SUBLANE), :], 1, 0), last_i)
            acc_r = acc_r + pr * xr + pi * xi
            acc_i = acc_i + pr * xi - pi * xr
            return xr[0:1, :], xi[0:1, :], acc_r, acc_i

        zero = jnp.zeros((1, S5_L), F32)
        zacc = jnp.zeros((SUBLANE, S5_L), F32)
        _, _, acc_r, acc_i = lax.fori_loop(0, n_blocks, step, (zero, zero, zacc, zacc))
        dlam_ref[0, 0:1, :] = jnp.sum(acc_r, axis=0, keepdims=True)
        dlam_ref[0, 1:2, :] = jnp.sum(acc_i, axis=0, keepdims=True)
        for ref in (dbdr_ref, dbdi_ref, dcdr_ref, dcdi_ref, dd_ref):
            ref[...] = jnp.zeros_like(ref)

        def grads(rows):
            uv, dyv = u_ref[rows, :], dy_ref[rows, :]
            grv, giv = gr_ref[rows, :], gi_ref[rows, :]
            du_ref[rows, :] = _bdot(grv, bdr_ref[0], _NT) + _bdot(giv, bdi_ref[0], _NT) + d_ref[...] * dyv
            dbdr_ref[0] += _bdot(uv, grv, _TN)
            dbdi_ref[0] += _bdot(uv, giv, _TN)
            dcdr_ref[0] += _bdot(sr_ref[rows, :], dyv, _TN)
            dcdi_ref[0] -= _bdot(si_ref[rows, :], dyv, _TN)
            dd_ref[...] += jnp.sum(dyv * uv, axis=0, keepdims=True)

        _row_chunks(seq, grads)

    scratch = [pltpu.VMEM((seq, S5_L), F32) for _ in range(4)]
    return pl.pallas_call(
        body, grid=(S5_GB,),
        in_specs=[u_spec, bd_spec, bd_spec, cd_spec, cd_spec, lam_spec, d_spec, u_spec],
        out_specs=[u_spec, bd_spec, bd_spec, cd_spec, cd_spec, lam_spec, d_spec],
        out_shape=[jax.ShapeDtypeStruct(a.shape, F32) for a in (u, bdr, bdi, cdr, cdi, lam, d)],
        scratch_shapes=scratch, compiler_params=_cparams(("parallel",)), name="s5_bwd",
    )(u, bdr, bdi, cdr, cdi, lam, d, dy)


@jax.custom_vjp
def s5_core(u, bdr, bdi, cdr, cdi, lam, d):
    return _s5_fwd_call(u, bdr, bdi, cdr, cdi, lam, d)


def _s5_core_fwd(*args):
    return _s5_fwd_call(*args), args


def _s5_core_bwd(res, dy):
    return tuple(_s5_bwd_call(*res, dy))


s5_core.defvjp(_s5_core_fwd, _s5_core_bwd)


def _s5_discretize(a_re, a_im, log_dt, b_re, b_im, c_re, c_im, d):
    dt = jnp.exp(log_dt)[:, None]
    mag = jnp.exp(a_re * dt)
    lbr, lbi = mag * jnp.cos(a_im * dt), mag * jnp.sin(a_im * dt)
    den = a_re * a_re + a_im * a_im
    fr = ((lbr - 1.0) * a_re + lbi * a_im) / den
    fi = (lbi * a_re - (lbr - 1.0) * a_im) / den
    bbr = fr[:, :, None] * b_re - fi[:, :, None] * b_im
    bbi = fr[:, :, None] * b_im + fi[:, :, None] * b_re
    eye = jnp.eye(S5_GROUPS // S5_GB, dtype=F32)
    gl = S5_GROUPS // S5_GB

    def bd(t):
        return jnp.einsum('bgpc,gh->bgchp', t.reshape(S5_GB, gl, S5_STATE, S5_GROUP), eye).reshape(S5_GB, S5_U, S5_L)

    def cd(t):
        return jnp.einsum('bgcp,gh->bgphc', t.reshape(S5_GB, gl, S5_GROUP, S5_STATE), eye).reshape(S5_GB, S5_L, S5_U)

    lam = jnp.stack([lbr.reshape(S5_GB, S5_L), lbi.reshape(S5_GB, S5_L)], axis=1)
    return bd(bbr), bd(bbi), cd(c_re), cd(c_im), lam, d.reshape(1, S5_WIDTH)


@jax.custom_vjp
def _unit_lower_solve(neg_a, rhs, tinv):
    return _split_dot(tinv, rhs)


def _unit_lower_solve_fwd(neg_a, rhs, tinv):
    x = _split_dot(tinv, rhs)
    return x, (x, tinv)


def _unit_lower_solve_bwd(res, dx):
    x, tinv = res
    g = _split_dot(tinv, dx, _TN)
    return _split_dot(g, x, _NT), g, jnp.zeros_like(tinv)


_unit_lower_solve.defvjp(_unit_lower_solve_fwd, _unit_lower_solve_bwd)


def _unit_lower_inverse(neg_a):
    r = lax.broadcasted_iota(jnp.int32, neg_a.shape, 0)
    c = lax.broadcasted_iota(jnp.int32, neg_a.shape, 1)
    p = (r == c).astype(F32) + neg_a
    npow = _split_dot(neg_a, neg_a)
    for _ in range(4):
        y = _split_dot(jnp.concatenate([p, npow], axis=0), npow)
        p = p + y[:CHUNK]
        npow = y[CHUNK:]
    return p + _split_dot(p, npow)


def _gdn_chunk(q, k, v, g_col, b_col, st, tinv=None):
    r = lax.broadcasted_iota(jnp.int32, (CHUNK, CHUNK), 0)
    c = lax.broadcasted_iota(jnp.int32, (CHUNK, CHUNK), 1)
    eye = (r == c).astype(F32)
    strict = r > c
    causal = r >= c
    g_row = jnp.sum(g_col * eye, axis=0, keepdims=True)
    gcum = jnp.sum(jnp.where(causal, g_row, 0.0), axis=1, keepdims=True)
    gcum_row = jnp.sum(gcum * eye, axis=0, keepdims=True)
    diff = gcum - gcum_row
    decay_strict = jnp.where(strict, jnp.exp(jnp.where(strict, diff, 0.0)), 0.0)
    decay_causal = jnp.where(causal, jnp.exp(jnp.where(causal, diff, 0.0)), 0.0)
    gamma = jnp.exp(gcum)
    g_last = jnp.sum(jnp.where(lax.broadcasted_iota(jnp.int32, (CHUNK, 1), 0) == CHUNK - 1, gcum, 0.0),
                     axis=0, keepdims=True)
    kk = _bdot(k, k, _NT)
    neg_a = -(b_col * kk * decay_strict)
    if tinv is None:
        tinv = _unit_lower_inverse(neg_a)
    x = _unit_lower_solve(neg_a, jnp.concatenate([b_col * v, (b_col * gamma) * k], axis=1), lax.stop_gradient(tinv))
    u_new, w_k = x[:, :GDN_HEAD_DIM], x[:, GDN_HEAD_DIM:]
    qk = _bdot(q, k, _NT) * decay_causal
    q_g = q * gamma
    k_tail = k * jnp.exp(g_last - gcum)
    w = u_new - _bdot(w_k, st)
    o = _bdot(q_g, st) + _bdot(qk, w)
    st_new = jnp.exp(g_last) * st + _bdot(k_tail, w, _TN)
    return o, st_new, tinv


def _gdn_cols(bgv, h):
    lane = lax.broadcasted_iota(jnp.int32, bgv.shape, 1)
    g_col = jnp.sum(jnp.where(lane == h, bgv, 0.0), axis=1, keepdims=True)
    b_col = jnp.sum(jnp.where(lane == GDN_HEADS + h, bgv, 0.0), axis=1, keepdims=True)
    return g_col, b_col


GDN_CPS = 4


def _gdn_fwd_call(q, k, v, bg):
    seq = q.shape[0]
    n_chunks = seq // CHUNK
    n_steps = n_chunks // GDN_CPS
    rows = GDN_CPS * CHUNK
    x_spec = pl.BlockSpec((rows, GDN_WIDTH), lambda n: (n, 0))
    bg_spec = pl.BlockSpec((rows, LANE), lambda n: (n, 0))
    st_spec = pl.BlockSpec((GDN_CPS, GDN_WIDTH, GDN_HEAD_DIM), lambda n: (n, 0, 0))
    ti_spec = pl.BlockSpec((GDN_CPS, GDN_HEADS * CHUNK, CHUNK), lambda n: (n, 0, 0))

    def body(q_ref, k_ref, v_ref, bg_ref, o_ref, st_out_ref, ti_out_ref, st_ref):
        @pl.when(pl.program_id(0) == 0)
        def _():
            st_ref[...] = jnp.zeros_like(st_ref)

        for h in range(GDN_HEADS):
            sl = slice(h * GDN_HEAD_DIM, (h + 1) * GDN_HEAD_DIM)
            st = st_ref[sl, :]
            for cc in range(GDN_CPS):
                rs = slice(cc * CHUNK, (cc + 1) * CHUNK)
                g_col, b_col = _gdn_cols(bg_ref[rs, :], h)
                st_out_ref[cc, sl, :] = st
                o, st, tinv = _gdn_chunk(q_ref[rs, sl], k_ref[rs, sl], v_ref[rs, sl], g_col, b_col, st)
                o_ref[rs, sl] = o
                ti_out_ref[cc, h * CHUNK:(h + 1) * CHUNK, :] = tinv
            st_ref[sl, :] = st

    return pl.pallas_call(
        body, grid=(n_steps,), in_specs=[x_spec, x_spec, x_spec, bg_spec], out_specs=[x_spec, st_spec, ti_spec],
        out_shape=[jax.ShapeDtypeStruct(q.shape, F32),
                   jax.ShapeDtypeStruct((n_chunks, GDN_WIDTH, GDN_HEAD_DIM), F32),
                   jax.ShapeDtypeStruct((n_chunks, GDN_HEADS * CHUNK, CHUNK), F32)],
        scratch_shapes=[pltpu.VMEM((GDN_WIDTH, GDN_HEAD_DIM), F32)],
        compiler_params=_cparams(("arbitrary",)), name="gdn_fwd",
    )(q, k, v, bg)


def _gdn_bwd_call(q, k, v, bg, states, tinvs, do):
    seq = q.shape[0]
    n_steps = seq // CHUNK // GDN_CPS
    rows = GDN_CPS * CHUNK
    x_spec = pl.BlockSpec((rows, GDN_WIDTH), lambda i: (n_steps - 1 - i, 0))
    bg_spec = pl.BlockSpec((rows, LANE), lambda i: (n_steps - 1 - i, 0))
    st_spec = pl.BlockSpec((GDN_CPS, GDN_WIDTH, GDN_HEAD_DIM), lambda i: (n_steps - 1 - i, 0, 0))
    ti_spec = pl.BlockSpec((GDN_CPS, GDN_HEADS * CHUNK, CHUNK), lambda i: (n_steps - 1 - i, 0, 0))

    def body(q_ref, k_ref, v_ref, bg_ref, st_in_ref, ti_ref, do_ref, dq_ref, dk_ref, dv_ref, dbg_ref, dst_ref):
        @pl.when(pl.program_id(0) == 0)
        def _():
            dst_ref[...] = jnp.zeros_like(dst_ref)

        lane = lax.broadcasted_iota(jnp.int32, (CHUNK, LANE), 1)
        dbg = [jnp.zeros((CHUNK, LANE), F32) for _ in range(GDN_CPS)]
        for h in range(GDN_HEADS):
            sl = slice(h * GDN_HEAD_DIM, (h + 1) * GDN_HEAD_DIM)
            dst = dst_ref[sl, :]
            for cc in reversed(range(GDN_CPS)):
                rs = slice(cc * CHUNK, (cc + 1) * CHUNK)
                g_col, b_col = _gdn_cols(bg_ref[rs, :], h)
                tinv = ti_ref[cc, h * CHUNK:(h + 1) * CHUNK, :]
                _, pullback = jax.vjp(lambda *a: _gdn_chunk(*a, tinv=tinv)[:2], q_ref[rs, sl], k_ref[rs, sl],
                                      v_ref[rs, sl], g_col, b_col, st_in_ref[cc, sl, :])
                dq, dk, dv, dg, db, dst = pullback((do_ref[rs, sl], dst))
                dq_ref[rs, sl] = dq
                dk_ref[rs, sl] = dk
                dv_ref[rs, sl] = dv
                dbg[cc] = dbg[cc] + jnp.where(lane == h, dg, 0.0) + jnp.where(lane == GDN_HEADS + h, db, 0.0)
            dst_ref[sl, :] = dst
        for cc in range(GDN_CPS):
            dbg_ref[cc * CHUNK:(cc + 1) * CHUNK, :] = dbg[cc]

    return pl.pallas_call(
        body, grid=(n_steps,), in_specs=[x_spec, x_spec, x_spec, bg_spec, st_spec, ti_spec, x_spec],
        out_specs=[x_spec, x_spec, x_spec, bg_spec],
        out_shape=[jax.ShapeDtypeStruct(q.shape, F32)] * 3 + [jax.ShapeDtypeStruct(bg.shape, F32)],
        scratch_shapes=[pltpu.VMEM((GDN_WIDTH, GDN_HEAD_DIM), F32)],
        compiler_params=_cparams(("arbitrary",)), name="gdn_bwd",
    )(q, k, v, bg, states, tinvs, do)


@jax.custom_vjp
def gdn_core(q, k, v, bg):
    return _gdn_fwd_call(q, k, v, bg)[0]


def _gdn_core_fwd(q, k, v, bg):
    o, states, tinvs = _gdn_fwd_call(q, k, v, bg)
    return o, (q, k, v, bg, states, tinvs)


def _gdn_core_bwd(res, do):
    return tuple(_gdn_bwd_call(*res, do))


gdn_core.defvjp(_gdn_core_fwd, _gdn_core_bwd)


def _row(v):
    return v.reshape(1, -1)


def _lane_pad(v):
    return jnp.pad(v, (0, LANE - v.shape[0])).reshape(1, LANE)


def _delay_rows(x, k):
    return jnp.pad(x, ((k, 0), (0, 0)))[:x.shape[0]]


def s5_mixer(u, a_re, a_im, log_dt, b_re, b_im, c_re, c_im, d, w_glu, b_glu):
    y = s5_core(u, *_s5_discretize(a_re, a_im, log_dt, b_re, b_im, c_re, c_im, d))
    return make_rowop(_s5_post_fn, "s5_post")((y,), (w_glu, _row(b_glu)))[0]


def gated_deltanet(qkv, gate, ab, conv_w, a_log, dt_bias, out_g):
    rows = (qkv, _delay_rows(qkv, 1), _delay_rows(qkv, 2), _delay_rows(qkv, 3), ab)
    q, k, v, bg = make_rowop(_gdn_prep_fn, "gdn_prep")(rows, (conv_w, _lane_pad(a_log), _lane_pad(dt_bias)))
    o = gdn_core(q, k, v, bg)
    return make_rowop(_gdn_out_fn, "gdn_out")((o, gate), (_row(out_g),))[0]


def chunk_attention(x, h, w_qkv, w_out, q_g, k_g, rel_bias):
    qkv = linear_cols(h, w_qkv)
    qn, kn = make_rowop(_ca_qknorm_fn, "ca_qknorm")((qkv,), (_row(jnp.tile(q_g, 2)), _row(jnp.tile(k_g, 2))))
    kpad = jnp.pad(kn, ((CA_PAD, 0), (0, 0)))
    vpad = jnp.pad(qkv[:, 2 * D_MODEL:], ((CA_PAD, 0), (0, 0)))
    o = chunk_attn_core(qn, kpad, vpad, _rel_bias_vector(rel_bias))
    return linear_res(o, w_out, x)


def memory_cross_attention(x, h, mem_n, w_q, w_kv, w_out, q_g, k_g):
    q = linear(h, w_q)
    kv = linear_cols(mem_n, w_kv)
    o = make_rowop(_xattn_fn, "xattn")((q,), (kv[:, :D_MODEL], kv[:, D_MODEL:], _row(q_g), _row(k_g)))[0]
    return linear_res(o, w_out, x)


def swiglu(x, h, w_gate, w_up, w_down):
    return ffn_down(linear_cols_dm(h, w_gate), linear_cols_dm(h, w_up), w_down, x)


def _rel_bias_vector(rel_bias):
    heads = rel_bias.shape[0]
    n_far = CA_KB - 1 - MAX_REL
    n_neg = CA_VEC - 1 - n_far - (2 * MAX_REL + 1)
    vec = jnp.concatenate([jnp.zeros((heads, 1), F32),
                           jnp.broadcast_to(rel_bias[:, 2 * MAX_REL:], (heads, n_far)),
                           jnp.flip(rel_bias, axis=1),
                           jnp.broadcast_to(rel_bias[:, :1], (heads, n_neg))], axis=1)
    return vec.reshape(heads // 2, 2, CA_VEC)


def _exchange(arrays, modes, name):
    n = len(arrays)
    out_shapes = [jax.ShapeDtypeStruct((N_DEV,) + a.shape if m == "gather" else a.shape, a.dtype)
                  for a, m in zip(arrays, modes)]

    def body(*refs):
        ins, outs = refs[:n], refs[n:2 * n]
        send_sems, recv_sems, local_sems = refs[2 * n:]
        x, y, c = lax.axis_index("x"), lax.axis_index("y"), lax.axis_index("c")
        me = 4 * x + 2 * y + c
        pending = []
        for i in range(n):
            gather = modes[i] == "gather"
            local = pltpu.make_async_copy(ins[i] if gather else ins[i].at[me], outs[i].at[me], local_sems.at[i])
            local.start()
            pending.append(local)
        for k in range(1, N_DEV):
            px, py, pc = (x + (k >> 2)) % 2, (y + ((k >> 1) & 1)) % 2, (c + (k & 1)) % 2
            peer = 4 * px + 2 * py + pc
            for i in range(n):
                src = ins[i] if modes[i] == "gather" else ins[i].at[peer]
                sem = i * (N_DEV - 1) + k - 1
                send = pltpu.make_async_remote_copy(src_ref=src, dst_ref=outs[i].at[me], send_sem=send_sems.at[sem],
                                                    recv_sem=recv_sems.at[sem], device_id=(px, py, pc),
                                                    device_id_type=pl.DeviceIdType.MESH)
                send.start()
                arrival = pltpu.make_async_remote_copy(src_ref=src, dst_ref=outs[i].at[peer],
                                                       send_sem=send_sems.at[sem], recv_sem=recv_sems.at[sem],
                                                       device_id=(px, py, pc), device_id_type=pl.DeviceIdType.MESH)
                pending.append((send, arrival))
        for item in pending:
            if isinstance(item, tuple):
                item[0].wait_send()
                item[1].wait_recv()
            else:
                item.wait()

    any_spec = pl.BlockSpec(memory_space=pl.ANY)
    return pl.pallas_call(
        body, in_specs=[any_spec] * n, out_specs=[any_spec] * n, out_shape=out_shapes,
        scratch_shapes=[pltpu.SemaphoreType.DMA((n * (N_DEV - 1),)), pltpu.SemaphoreType.DMA((n * (N_DEV - 1),)),
                        pltpu.SemaphoreType.DMA((n,))],
        name=name,
    )(*arrays)


_HBM_SPEC = pl.BlockSpec(memory_space=pltpu.HBM)
_SEM_SPEC = pl.BlockSpec(memory_space=pltpu.SEMAPHORE)
_SIDE_EFFECT = pltpu.SideEffectType.DATAFLOW_SIDE_EFFECTING


def _peer(x, y, c, k):
    return (x + (k >> 2)) % 2, (y + ((k >> 1) & 1)) % 2, (c + (k & 1)) % 2


def _exchange_start(arrays, modes, after, name, carry):
    n = len(arrays)
    n_sem = n * (N_DEV - 1)
    lands = [pltpu.with_memory_space_constraint(lax.empty((N_DEV,) + a.shape if m == "gather" else a.shape, a.dtype),
                                                pltpu.HBM) for a, m in zip(arrays, modes)]
    arrays = [pltpu.with_memory_space_constraint(a, pltpu.HBM) for a in arrays]

    def body(*refs):
        ins, zones = refs[:n], refs[n:2 * n]
        send_sems, recv_sems, own_sems = refs[2 * n + 2:2 * n + 5]
        x, y, c = lax.axis_index("x"), lax.axis_index("y"), lax.axis_index("c")
        me = 4 * x + 2 * y + c
        for i in range(n):
            pltpu.make_async_copy(ins[i] if modes[i] == "gather" else ins[i].at[me], zones[i].at[me],
                                  own_sems.at[i]).start()
        for k in range(1, N_DEV):
            px, py, pc = _peer(x, y, c, k)
            peer = 4 * px + 2 * py + pc
            for i in range(n):
                sem = i * (N_DEV - 1) + k - 1
                pltpu.make_async_remote_copy(src_ref=ins[i] if modes[i] == "gather" else ins[i].at[peer],
                                             dst_ref=zones[i].at[me], send_sem=send_sems.at[sem],
                                             recv_sem=recv_sems.at[sem], device_id=(px, py, pc),
                                             device_id_type=pl.DeviceIdType.MESH).start()

    carry = pltpu.with_memory_space_constraint(carry, pltpu.HBM)
    out_shape = ((pltpu.SemaphoreType.DMA((n_sem,)), pltpu.SemaphoreType.DMA((n_sem,)), pltpu.SemaphoreType.DMA((n,)))
                 + tuple(pltpu.HBM(a.shape, a.dtype) for a in arrays) + tuple(pltpu.HBM(z.shape, z.dtype) for z in lands)
                 + (pltpu.HBM(carry.shape, carry.dtype),))
    aliases = {i: 3 + i for i in range(2 * n)}
    aliases[2 * n + 1] = 3 + 2 * n
    res = pl.pallas_call(
        body, name=name, out_shape=out_shape,
        in_specs=[_HBM_SPEC] * (2 * n) + [pl.BlockSpec(memory_space=pl.ANY), _HBM_SPEC],
        out_specs=(_SEM_SPEC,) * 3 + (_HBM_SPEC,) * (2 * n + 1),
        input_output_aliases=aliases,
        compiler_params=pltpu.CompilerParams(has_side_effects=_SIDE_EFFECT),
    )(*arrays, *lands, after, carry)
    return tuple(res[:3]), list(res[3:3 + n]), list(res[3 + n:3 + 2 * n]), res[3 + 2 * n]


def _exchange_wait(started, modes, after, name):
    sems, sources, zones, _ = started
    n = len(sources)

    def body(*refs):
        ins, lands = refs[:n], refs[n:2 * n]
        send_ref, recv_ref, own_ref = refs[2 * n:2 * n + 3]
        x, y, c = lax.axis_index("x"), lax.axis_index("y"), lax.axis_index("c")
        me = 4 * x + 2 * y + c
        for i in range(n):
            pltpu.make_async_copy(ins[i] if modes[i] == "gather" else ins[i].at[me], lands[i].at[me],
                                  own_ref.at[i]).wait()
        for k in range(1, N_DEV):
            px, py, pc = _peer(x, y, c, k)
            peer = 4 * px + 2 * py + pc
            for i in range(n):
                sem = i * (N_DEV - 1) + k - 1
                cp = pltpu.make_async_remote_copy(src_ref=ins[i] if modes[i] == "gather" else ins[i].at[peer],
                                                  dst_ref=lands[i].at[peer], send_sem=send_ref.at[sem],
                                                  recv_sem=recv_ref.at[sem], device_id=(px, py, pc),
                                                  device_id_type=pl.DeviceIdType.MESH)
                cp.wait_send()
                cp.wait_recv()

    res = pl.pallas_call(
        body, name=name,
        out_shape=tuple(pltpu.HBM(a.shape, a.dtype) for a in sources) + tuple(pltpu.HBM(z.shape, z.dtype) for z in zones),
        in_specs=[_HBM_SPEC] * (2 * n) + [_SEM_SPEC] * 3 + [pl.BlockSpec(memory_space=pl.ANY)],
        out_specs=(_HBM_SPEC,) * (2 * n), input_output_aliases={i: i for i in range(2 * n)},
        compiler_params=pltpu.CompilerParams(has_side_effects=_SIDE_EFFECT),
    )(*sources, *zones, *sems, after)
    return list(res[n:])


ADAM_TILE = 64 * 1024


def _adam_call(w, m, v, slots, name):
    n_layers, rows, cols = w.shape
    tr = rows
    if n_layers * rows * cols > ADAM_TILE:
        fits = [t for t in range(SUBLANE, rows, SUBLANE) if rows % t == 0 and n_layers * t * cols <= ADAM_TILE]
        tr = max(fits) if fits else SUBLANE
    c1 = 1.0 - ADAM_B1 ** ADAM_STEP
    c2 = 1.0 - ADAM_B2 ** ADAM_STEP

    def body(*refs):
        w_ref, m_ref, v_ref = refs[:3]
        slot_refs = refs[3:3 + n_layers]
        grad_ref, delta_ref, nm_ref, nv_ref = refs[3 + n_layers:]
        for layer in range(n_layers):
            g = slot_refs[layer][0].astype(F32)
            for k in range(1, N_DEV):
                g = g + slot_refs[layer][k].astype(F32)
            m_new = ADAM_B1 * m_ref[layer] + (1.0 - ADAM_B1) * g
            v_new = ADAM_B2 * v_ref[layer] + (1.0 - ADAM_B2) * (g * g)
            m_hat = m_new / c1
            v_hat = v_new / c2
            grad_ref[layer] = g
            delta_ref[layer] = -ADAM_LR * (m_hat / (jnp.sqrt(v_hat) + ADAM_EPS) + ADAM_WD * w_ref[layer])
            nm_ref[layer] = m_new
            nv_ref[layer] = v_new

    spec = pl.BlockSpec((n_layers, tr, cols), lambda i: (0, i, 0))
    slot_spec = pl.BlockSpec((N_DEV, tr, cols), lambda i: (0, i, 0))
    return pl.pallas_call(
        body, grid=(rows // tr,), in_specs=[spec, spec, spec] + [slot_spec] * n_layers,
        out_specs=[spec] * 4, out_shape=[jax.ShapeDtypeStruct(w.shape, F32)] * 4,
        compiler_params=_cparams(("parallel",)), name=name,
    )(w, m, v, *slots)


WEIGHT_NAMES = ['ab_norm_g', 'ab_w_in', 'ab_w_out', 's5_a_re', 's5_a_im', 's5_log_dt', 's5_b_re', 's5_b_im', 's5_c_re',
                's5_c_im', 's5_d', 's5_w_glu', 's5_b_glu', 'gdn_conv_w', 'gdn_a_log', 'gdn_dt_bias', 'gdn_out_norm_g',
                'c_norm_g', 'c_w_qkv', 'c_w_out', 'c_q_norm_g', 'c_k_norm_g', 'c_rel_bias', 'mem_norm_g', 'xa_norm_g',
                'xa_w_q', 'xa_w_kv', 'xa_w_out', 'xa_q_norm_g', 'xa_k_norm_g', 'f_norm_g', 'f_w_gate', 'f_w_up',
                'f_w_down']

SHARDED = {
    'ab_w_in': ('col', BF16), 'ab_w_out': ('row', BF16), 's5_w_glu': ('row', BF16), 'gdn_conv_w': ('col', F32),
    'c_norm_g': ('col', F32), 'c_w_qkv': ('col', BF16), 'c_w_out': ('row', BF16), 'xa_w_q': ('row', BF16),
    'xa_w_kv': ('col', BF16), 'xa_w_out': ('row', BF16), 'f_w_gate': ('col', BF16), 'f_w_up': ('col', BF16),
    'f_w_down': ('row', BF16),
}
GATHERED_AS_IS = ('c_w_qkv', 'xa_w_kv', 'f_w_gate', 'f_w_up', 'f_w_down')
REPLICATED = [n for n in WEIGHT_NAMES if n not in SHARDED]
PACK_UNIT = SUBLANE * LANE


def _full_from_gathered(g, axis):
    if axis == "row":
        return g.reshape(g.shape[0] * g.shape[1], g.shape[2])
    return jnp.transpose(g, (1, 0, 2)).reshape(g.shape[1], g.shape[0] * g.shape[2])


def _pack(arrays):
    flat = []
    for a in arrays:
        size = a.size
        padded = -(-size // PACK_UNIT) * PACK_UNIT
        flat.append(jnp.pad(a.reshape(-1), (0, padded - size)).reshape(-1, LANE))
    return jnp.concatenate(flat, axis=0)


def _unpack(buf, shapes):
    out, row = [], 0
    for shape in shapes:
        size = math.prod(shape)
        rows = -(-size // PACK_UNIT) * SUBLANE
        out.append(buf[row:row + rows].reshape(-1)[:size].reshape(shape))
        row += rows
    return out


N_STAGES = 2 * DEPTH
EVEN_SHARDED = ['ab_w_in', 'ab_w_out', 's5_w_glu', 'gdn_conv_w']
ODD_SHARDED = ['c_norm_g', 'c_w_qkv', 'c_w_out']
ALL_SHARDED = ['xa_w_q', 'xa_w_kv', 'xa_w_out', 'f_w_gate', 'f_w_up', 'f_w_down']
EVEN_SMALL = ['ab_norm_g', 's5_a_re', 's5_a_im', 's5_log_dt', 's5_b_re', 's5_b_im', 's5_c_re', 's5_c_im', 's5_d',
              's5_b_glu', 'gdn_a_log', 'gdn_dt_bias', 'gdn_out_norm_g']
ODD_SMALL = ['c_q_norm_g', 'c_k_norm_g', 'c_rel_bias']
ALL_SMALL = ['xa_norm_g', 'xa_q_norm_g', 'xa_k_norm_g', 'f_norm_g']


def _stage_params(stage):
    layer, part = divmod(stage, 2)
    if part == 1:
        return [(n, layer) for n in ALL_SHARDED], [(n, layer) for n in ALL_SMALL]
    big, small = (EVEN_SHARDED, EVEN_SMALL) if layer % 2 == 0 else (ODD_SHARDED, ODD_SMALL)
    return [(n, layer // 2) for n in big], [(n, layer // 2) for n in small]


def _stage_forward(stage, landed, small, x, mem_n):
    layer, part = divmod(stage, 2)
    big = {}
    for (n, _), g in zip(_stage_params(stage)[0], landed):
        if n == 'ab_w_in':
            big[n] = cols_to_natural(g, AB_IN_PAD)
        elif n in GATHERED_AS_IS:
            big[n] = g
        elif n == 's5_w_glu':
            big[n] = _full_from_gathered(g, 'row').astype(F32)
        else:
            big[n] = _full_from_gathered(g, SHARDED[n][0])
    if part == 1:
        h, x = rmsnorm_res(x, small['xa_norm_g'], "xa_norm")
        x = memory_cross_attention(x, h, mem_n, big['xa_w_q'], big['xa_w_kv'], big['xa_w_out'],
                                   small['xa_q_norm_g'], small['xa_k_norm_g'])
        h, x = rmsnorm_res(x, small['f_norm_g'], "f_norm")
        return swiglu(x, h, big['f_w_gate'], big['f_w_up'], big['f_w_down'])
    if layer % 2 == 0:
        h, x = rmsnorm_res(x, small['ab_norm_g'], "ab_norm")
        w_in = big['ab_w_in']
        u = linear(h, w_in[:, :S5_WIDTH])
        qkv = linear(h, w_in[:, S5_WIDTH:S5_WIDTH + 3 * GDN_WIDTH])
        gate = linear(h, w_in[:, S5_WIDTH + 3 * GDN_WIDTH:S5_WIDTH + 4 * GDN_WIDTH])
        ab = linear(h, w_in[:, S5_WIDTH + 4 * GDN_WIDTH:])
        a_out = s5_mixer(u, small['s5_a_re'], small['s5_a_im'], small['s5_log_dt'], small['s5_b_re'], small['s5_b_im'],
                         small['s5_c_re'], small['s5_c_im'], small['s5_d'], big['s5_w_glu'], small['s5_b_glu'])
        b_out = gated_deltanet(qkv, gate, ab, big['gdn_conv_w'], small['gdn_a_log'], small['gdn_dt_bias'],
                               small['gdn_out_norm_g'])
        w_out = big['ab_w_out']
        return linear_res(b_out, w_out[S5_WIDTH:], linear_res(a_out, w_out[:S5_WIDTH], x))
    h, x = rmsnorm_res(x, big['c_norm_g'].reshape(-1), "c_norm")
    return chunk_attention(x, h, big['c_w_qkv'], big['c_w_out'], small['c_q_norm_g'], small['c_k_norm_g'],
                           small['c_rel_bias'])


def _loss_rows(x, target):
    return jnp.sum(make_rowop(_loss_fn, "loss")((x, target), ())[0])


def kernel(x, mem, ab_norm_g, ab_w_in, ab_w_out, s5_a_re, s5_a_im, s5_log_dt, s5_b_re, s5_b_im, s5_c_re, s5_c_im, s5_d, s5_w_glu, s5_b_glu, gdn_conv_w, gdn_a_log, gdn_dt_bias, gdn_out_norm_g, c_norm_g, c_w_qkv, c_w_out, c_q_norm_g, c_k_norm_g, c_rel_bias, mem_norm_g, xa_norm_g, xa_w_q, xa_w_kv, xa_w_out, xa_q_norm_g, xa_k_norm_g, f_norm_g, f_w_gate, f_w_up, f_w_down, loss_target, m_ab_norm_g, m_ab_w_in, m_ab_w_out, m_s5_a_re, m_s5_a_im, m_s5_log_dt, m_s5_b_re, m_s5_b_im, m_s5_c_re, m_s5_c_im, m_s5_d, m_s5_w_glu, m_s5_b_glu, m_gdn_conv_w, m_gdn_a_log, m_gdn_dt_bias, m_gdn_out_norm_g, m_c_norm_g, m_c_w_qkv, m_c_w_out, m_c_q_norm_g, m_c_k_norm_g, m_c_rel_bias, m_mem_norm_g, m_xa_norm_g, m_xa_w_q, m_xa_w_kv, m_xa_w_out, m_xa_q_norm_g, m_xa_k_norm_g, m_f_norm_g, m_f_w_gate, m_f_w_up, m_f_w_down, v_ab_norm_g, v_ab_w_in, v_ab_w_out, v_s5_a_re, v_s5_a_im, v_s5_log_dt, v_s5_b_re, v_s5_b_im, v_s5_c_re, v_s5_c_im, v_s5_d, v_s5_w_glu, v_s5_b_glu, v_gdn_conv_w, v_gdn_a_log, v_gdn_dt_bias, v_gdn_out_norm_g, v_c_norm_g, v_c_w_qkv, v_c_w_out, v_c_q_norm_g, v_c_k_norm_g, v_c_rel_bias, v_mem_norm_g, v_xa_norm_g, v_xa_w_q, v_xa_w_kv, v_xa_w_out, v_xa_q_norm_g, v_xa_k_norm_g, v_f_norm_g, v_f_w_gate, v_f_w_up, v_f_w_down):
    given = dict(locals())
    no_after = jnp.zeros((SUBLANE, LANE), F32)

    def shard(n, idx):
        a = given[n][idx]
        return (a.reshape(1, -1) if a.ndim == 1 else a).astype(SHARDED[n][1])

    def gather_start(stage, after, carry):
        arrays = [shard(n, idx) for n, idx in _stage_params(stage)[0]]
        return _exchange_start(arrays, ["gather"] * len(arrays), after, "gather_start_%d" % stage, carry)

    act = x[0]
    mem_n, mem_pullback = jax.vjp(lambda m, g: rmsnorm(m, g, "mem_norm"), mem[0], mem_norm_g)
    in_flight = {}
    for stage in range(2):
        in_flight[stage] = gather_start(stage, no_after, act)
        act = in_flight[stage][3]
    pullbacks = []
    for stage in range(N_STAGES):
        started = in_flight.pop(stage)
        landed = _exchange_wait(started, ["gather"] * len(started[1]), act, "gather_wait_%d" % stage)
        if stage + 2 < N_STAGES:
            in_flight[stage + 2] = gather_start(stage + 2, landed[0], act)
            act = in_flight[stage + 2][3]
        small = {n: given[n][idx] for n, idx in _stage_params(stage)[1]}
        act, pullback = jax.vjp(functools.partial(_stage_forward, stage), landed, small, act, mem_n)
        pullbacks.append(pullback)
    loss_local, loss_pullback = jax.vjp(_loss_rows, act, loss_target[0])
    d_act = loss_pullback(jnp.ones((), F32))[0]

    d_mem_n = jnp.zeros_like(mem_n)
    g_small = {}
    received = [None] * N_STAGES
    started, after = None, no_after
    for stage in reversed(range(N_STAGES)):
        d_landed, d_small, d_act, d_mem = pullbacks[stage](d_act)
        if stage % 2 == 1:
            d_mem_n = d_mem_n + d_mem
        for n, idx in _stage_params(stage)[1]:
            g_small[(n, idx)] = d_small[n]
        if started is not None:
            received[stage + 1] = _exchange_wait(started, ["scatter"] * len(started[1]), d_act,
                                                 "scatter_wait_%d" % (stage + 1))
            after = received[stage + 1][0]
        started = _exchange_start(list(d_landed), ["scatter"] * len(d_landed), after, "scatter_start_%d" % stage,
                                  d_act)
        d_act = started[3]
    g_small[('mem_norm_g', None)] = mem_pullback(d_mem_n)[1]

    def small_grad(n):
        if n == 'mem_norm_g':
            return g_small[(n, None)]
        return jnp.stack([g_small[(n, i)] for i in range(given[n].shape[0])], axis=0)

    packed = _exchange([_pack([small_grad(n) for n in REPLICATED])], ["gather"], "small_grads_allgather")[0]
    received[0] = _exchange_wait(started, ["scatter"] * len(started[1]), packed, "scatter_wait_0")

    results = {}
    for n in SHARDED:
        slots = {}
        for stage in range(N_STAGES):
            for (pn, idx), r in zip(_stage_params(stage)[0], received[stage]):
                if pn == n:
                    slots[idx] = r
        shape = given[n].shape
        to3d = lambda a: a.reshape(a.shape[0], -1, a.shape[-1])
        outs = _adam_call(to3d(given[n]), to3d(given['m_' + n]), to3d(given['v_' + n]),
                          [slots[i] for i in range(len(slots))], "adamw_" + n)
        results[n] = [o.reshape(shape) for o in outs]
    outs = _adam_call(_pack([given[n] for n in REPLICATED])[None], _pack([given['m_' + n] for n in REPLICATED])[None],
                      _pack([given['v_' + n] for n in REPLICATED])[None], [packed], "adamw_replicated")
    shapes = [given[n].shape for n in REPLICATED]
    for j, parts in enumerate(zip(*[_unpack(o[0], shapes) for o in outs])):
        results[REPLICATED[j]] = list(parts)

    loss = lax.psum(loss_local, ("x", "y", "c"))
    return (loss, d_act[None], *[results[n][0] for n in WEIGHT_NAMES], *[results[n][1] for n in WEIGHT_NAMES],
            *[results[n][2] for n in WEIGHT_NAMES], *[results[n][3] for n in WEIGHT_NAMES])
```

```python
import functools
import math

import jax
import jax.numpy as jnp
import numpy as np
from jax import lax
from jax.experimental import pallas as pl
from jax.experimental.pallas import tpu as pltpu

F32 = jnp.float32
BF16 = jnp.bfloat16
HI = lax.Precision.HIGHEST

N_DEV = 8
D_MODEL = 1024
SEQ = 2048
DEPTH = 4
CHUNK = 64
N_MEM = 256
RMS_EPS = 1e-6
S5_WIDTH = 512
S5_GROUP = 16
S5_GROUPS = 32
S5_STATE = 64
GDN_HEAD_DIM = 128
GDN_WIDTH = 512
GDN_HEADS = 4
GDN_CONV = 4
AB_IN = S5_WIDTH + 4 * GDN_WIDTH + 2 * GDN_HEADS
AB_IN_PAD = 2688
CA_HEADS = 16
CA_HEAD_DIM = 64
CA_LEFT = 8
CA_BAND = (CA_LEFT + 1) * CHUNK
CA_PAD = CA_LEFT * CHUNK
MAX_REL = 128
XA_HEADS = 4
XA_HEAD_DIM = 256
FFN = 2816
ADAM_LR, ADAM_B1, ADAM_B2, ADAM_EPS, ADAM_WD, ADAM_STEP = 0.001, 0.9, 0.999, 1e-08, 0.01, 10

VMEM_LIMIT = 48 * 1024 * 1024
LANE = 128
SUBLANE = 8


def _cparams(sem=None):
    return pltpu.CompilerParams(dimension_semantics=sem, vmem_limit_bytes=VMEM_LIMIT)


def _divisor_tile(n, target, unit=LANE):
    if n <= target:
        return n
    best = None
    for t in range(unit, target + 1, unit):
        if n % t == 0:
            best = t
    assert best is not None, (n, target)
    return best


def _matmul(a, b, *, ta=False, tb=False, out_dtype=F32, name="mm", res=None):
    if ta:
        k_dim, m_dim = a.shape
    else:
        m_dim, k_dim = a.shape
    if tb:
        n_dim, kb = b.shape
    else:
        kb, n_dim = b.shape
    assert kb == k_dim, (a.shape, b.shape, ta, tb)
    tm = _divisor_tile(m_dim, 1024)
    tn = _divisor_tile(n_dim, 512)
    tk = _divisor_tile(k_dim, 1408)
    nk = k_dim // tk
    dims = (((0 if ta else 1,), (1 if tb else 0,)), ((), ()))

    def body(a_ref, b_ref, *rest):
        res_ref = rest[0] if res is not None else None
        o_ref, acc_ref = rest[-2:]
        k = pl.program_id(2)

        @pl.when(k == 0)
        def _():
            acc_ref[...] = jnp.zeros_like(acc_ref)

        acc_ref[...] += lax.dot_general(a_ref[...].astype(BF16), b_ref[...].astype(BF16), dims,
                                        preferred_element_type=F32)

        @pl.when(k == nk - 1)
        def _():
            total = acc_ref[...] if res is None else acc_ref[...] + res_ref[...]
            o_ref[...] = total.astype(o_ref.dtype)

    a_spec = pl.BlockSpec((tk, tm), lambda i, j, k: (k, i)) if ta else pl.BlockSpec((tm, tk), lambda i, j, k: (i, k))
    b_spec = pl.BlockSpec((tn, tk), lambda i, j, k: (j, k)) if tb else pl.BlockSpec((tk, tn), lambda i, j, k: (k, j))
    o_spec = pl.BlockSpec((tm, tn), lambda i, j, k: (i, j))
    return pl.pallas_call(
        body,
        grid=(m_dim // tm, n_dim // tn, nk),
        in_specs=[a_spec, b_spec] + ([o_spec] if res is not None else []),
        out_specs=o_spec,
        out_shape=jax.ShapeDtypeStruct((m_dim, n_dim), out_dtype),
        scratch_shapes=[pltpu.VMEM((tm, tn), F32)],
        compiler_params=_cparams(("parallel", "parallel", "arbitrary")),
        name=name,
    )(*((a, b) if res is None else (a, b, res)))


@jax.custom_vjp
def linear(a, w):
    return _matmul(a, w, name="linear_fwd")


def _linear_fwd(a, w):
    return _matmul(a, w, name="linear_fwd"), (a, w)


def _linear_bwd(res, dy):
    a, w = res
    da = _matmul(dy, w, tb=True, name="linear_da")
    dw = _matmul(a, dy, ta=True, out_dtype=w.dtype, name="linear_dw")
    return da, dw


linear.defvjp(_linear_fwd, _linear_bwd)


@jax.custom_vjp
def linear_res(a, w, x):
    return _matmul(a, w, name="linear_res_fwd", res=x)


def _linear_res_fwd(a, w, x):
    return _matmul(a, w, name="linear_res_fwd", res=x), (a, w)


def _linear_res_bwd(res, dy):
    return _linear_bwd(res, dy) + (dy,)


linear_res.defvjp(_linear_res_fwd, _linear_res_bwd)


def _mm_call(name, a, b, out_struct, grid, a_spec, b_spec, o_spec, dims, lead, res=None, keep_a=None):
    nk = grid[-1]
    acc_shape = o_spec.block_shape[1:] if lead[2] else o_spec.block_shape
    a_shape = a_spec.block_shape[1:] if lead[0] else a_spec.block_shape
    assert keep_a is None or nk == 1

    def body(a_ref, b_ref, *rest):
        res_ref = rest[0] if res is not None else None
        o_ref, acc_ref = rest[-1 - (keep_a is not None) - 1], rest[-1 - (keep_a is not None)]
        k = pl.program_id(len(grid) - 1)

        @pl.when(k == 0)
        def _():
            acc_ref[...] = jnp.zeros_like(acc_ref)

        if keep_a is None:
            av = (a_ref[0] if lead[0] else a_ref[...]).astype(BF16)
        else:
            a16_ref = rest[-1]

            @pl.when(pl.program_id(keep_a) == 0)
            def _():
                a16_ref[...] = (a_ref[0] if lead[0] else a_ref[...]).astype(BF16)

            av = a16_ref[...]
        bv = b_ref[0] if lead[1] else b_ref[...]
        acc_ref[...] += lax.dot_general(av, bv.astype(BF16), dims, preferred_element_type=F32)

        @pl.when(k == nk - 1)
        def _():
            if lead[2]:
                o_ref[0] = acc_ref[...].astype(o_ref.dtype)
            elif res is not None:
                o_ref[...] = (acc_ref[...] + res_ref[...]).astype(o_ref.dtype)
            else:
                o_ref[...] = acc_ref[...].astype(o_ref.dtype)

    return pl.pallas_call(
        body, grid=grid, in_specs=[a_spec, b_spec] + ([o_spec] if res is not None else []), out_specs=o_spec,
        out_shape=out_struct,
        scratch_shapes=[pltpu.VMEM(tuple(acc_shape), F32)] + ([pltpu.VMEM(tuple(a_shape), BF16)] if keep_a is not None else []),
        compiler_params=_cparams(("parallel", "arbitrary", "arbitrary")), name=name,
    )(*((a, b) if res is None else (a, b, res)))


_NN = (((1,), (0,)), ((), ()))
_NT_DIMS = (((1,), (1,)), ((), ()))
_TN_DIMS = (((0,), (0,)), ((), ()))


def _cols_fwd(a, g, dm_out):
    m_dim, k_dim = a.shape
    _, _, c_dim = g.shape
    tm = _divisor_tile(m_dim, 1024)
    tk = _divisor_tile(k_dim, 1024)
    a_spec = pl.BlockSpec((tm, tk), lambda i, j, k: (i, k))
    b_spec = pl.BlockSpec((1, tk, c_dim), lambda i, j, k: (j, k, 0))
    if dm_out:
        o_spec = pl.BlockSpec((1, tm, c_dim), lambda i, j, k: (j, i, 0))
        out = jax.ShapeDtypeStruct((N_DEV, m_dim, c_dim), BF16)
    else:
        o_spec = pl.BlockSpec((tm, c_dim), lambda i, j, k: (i, j))
        out = jax.ShapeDtypeStruct((m_dim, N_DEV * c_dim), F32)
    return _mm_call("cols_fwd", a, g, out, (m_dim // tm, N_DEV, k_dim // tk), a_spec, b_spec, o_spec, _NN,
                    (False, True, dm_out), keep_a=1 if k_dim == tk else None)


def _cols_da(dy, g, dm_out):
    _, k_dim, c_dim = g.shape
    m_dim = dy.shape[1] if dm_out else dy.shape[0]
    tm = _divisor_tile(m_dim, 1024)
    tk = _divisor_tile(k_dim, 1024)
    if dm_out:
        a_spec = pl.BlockSpec((1, tm, c_dim), lambda i, kb, j: (j, i, 0))
    else:
        a_spec = pl.BlockSpec((tm, c_dim), lambda i, kb, j: (i, j))
    b_spec = pl.BlockSpec((1, tk, c_dim), lambda i, kb, j: (j, kb, 0))
    o_spec = pl.BlockSpec((tm, tk), lambda i, kb, j: (i, kb))
    return _mm_call("cols_da", dy, g, jax.ShapeDtypeStruct((m_dim, k_dim), F32), (m_dim // tm, k_dim // tk, N_DEV),
                    a_spec, b_spec, o_spec, _NT_DIMS, (dm_out, True, False))


def _cols_dg(a, dy, g, dm_out):
    _, k_dim, c_dim = g.shape
    m_dim = a.shape[0]
    tm = _divisor_tile(m_dim, 2048)
    tk = _divisor_tile(k_dim, 512)
    a_spec = pl.BlockSpec((tm, tk), lambda kb, j, m: (m, kb))
    if dm_out:
        b_spec = pl.BlockSpec((1, tm, c_dim), lambda kb, j, m: (j, m, 0))
    else:
        b_spec = pl.BlockSpec((tm, c_dim), lambda kb, j, m: (m, j))
    o_spec = pl.BlockSpec((1, tk, c_dim), lambda kb, j, m: (j, kb, 0))
    return _mm_call("cols_dg", a, dy, jax.ShapeDtypeStruct(g.shape, g.dtype), (k_dim // tk, N_DEV, m_dim // tm),
                    a_spec, b_spec, o_spec, _TN_DIMS, (False, dm_out, True), keep_a=1 if m_dim == tm else None)


def _make_linear_cols(dm_out):
    @jax.custom_vjp
    def op(a, g):
        return _cols_fwd(a, g, dm_out)

    def fwd(a, g):
        return _cols_fwd(a, g, dm_out), (a, g)

    def bwd(res, dy):
        a, g = res
        return _cols_da(dy, g, dm_out), _cols_dg(a, dy, g, dm_out)

    op.defvjp(fwd, bwd)
    return op


linear_cols = _make_linear_cols(False)
linear_cols_dm = _make_linear_cols(True)


def _silu_mul(g, u):
    return _silu(g) * u


def _ffn_down_fwd(g, u, w, x):
    _, m_dim, r_dim = g.shape
    n_dim = w.shape[2]
    tm = _divisor_tile(m_dim, 1024)
    tn = _divisor_tile(n_dim, 1024)
    nj = N_DEV

    def body(g_ref, u_ref, w_ref, x_ref, o_ref, acc_ref):
        j = pl.program_id(2)

        @pl.when(j == 0)
        def _():
            acc_ref[...] = jnp.zeros_like(acc_ref)

        acc_ref[...] += _bdot(_silu_mul(g_ref[0].astype(F32), u_ref[0].astype(F32)), w_ref[0])

        @pl.when(j == nj - 1)
        def _():
            o_ref[...] = acc_ref[...] + x_ref[...]

    h_spec = pl.BlockSpec((1, tm, r_dim), lambda i, n, j: (j, i, 0))
    o_spec = pl.BlockSpec((tm, tn), lambda i, n, j: (i, n))
    return pl.pallas_call(
        body, grid=(m_dim // tm, n_dim // tn, nj),
        in_specs=[h_spec, h_spec, pl.BlockSpec((1, r_dim, tn), lambda i, n, j: (j, 0, n)), o_spec], out_specs=o_spec,
        out_shape=jax.ShapeDtypeStruct((m_dim, n_dim), F32), scratch_shapes=[pltpu.VMEM((tm, tn), F32)],
        compiler_params=_cparams(("parallel", "parallel", "arbitrary")), name="ffn_down_fwd",
    )(g, u, w, x)


def _ffn_down_dh(dy, g, u, w):
    m_dim, n_dim = dy.shape
    r_dim = w.shape[1]
    tm = _divisor_tile(m_dim, 1024)
    tn = _divisor_tile(n_dim, 1024)
    nn = n_dim // tn

    def body(dy_ref, w_ref, g_ref, u_ref, dg_ref, du_ref, acc_ref):
        n = pl.program_id(2)

        @pl.when(n == 0)
        def _():
            acc_ref[...] = jnp.zeros_like(acc_ref)

        acc_ref[...] += _bdot(dy_ref[...], w_ref[0], _NT_DIMS)

        @pl.when(n == nn - 1)
        def _():
            _, pullback = jax.vjp(_silu_mul, g_ref[0].astype(F32), u_ref[0].astype(F32))
            dg, du = pullback(acc_ref[...])
            dg_ref[0] = dg.astype(dg_ref.dtype)
            du_ref[0] = du.astype(du_ref.dtype)

    h_spec = pl.BlockSpec((1, tm, r_dim), lambda i, j, n: (j, i, 0))
    return pl.pallas_call(
        body, grid=(m_dim // tm, N_DEV, nn),
        in_specs=[pl.BlockSpec((tm, tn), lambda i, j, n: (i, n)), pl.BlockSpec((1, r_dim, tn), lambda i, j, n: (j, 0, n)),
                  h_spec, h_spec],
        out_specs=[h_spec, h_spec], out_shape=[jax.ShapeDtypeStruct(g.shape, g.dtype)] * 2,
        scratch_shapes=[pltpu.VMEM((tm, r_dim), F32)],
        compiler_params=_cparams(("parallel", "arbitrary", "arbitrary")), name="ffn_down_dh",
    )(dy, w, g, u)


def _ffn_down_dw(g, u, dy, w):
    _, m_dim, r_dim = g.shape
    n_dim = dy.shape[1]
    tm = _divisor_tile(m_dim, 2048)
    tn = _divisor_tile(n_dim, 512)
    nm = m_dim // tm

    def body(g_ref, u_ref, dy_ref, o_ref, acc_ref):
        m = pl.program_id(2)

        @pl.when(m == 0)
        def _():
            acc_ref[...] = jnp.zeros_like(acc_ref)

        acc_ref[...] += _bdot(_silu_mul(g_ref[0].astype(F32), u_ref[0].astype(F32)), dy_ref[...], _TN_DIMS)

        @pl.when(m == nm - 1)
        def _():
            o_ref[0] = acc_ref[...].astype(o_ref.dtype)

    h_spec = pl.BlockSpec((1, tm, r_dim), lambda j, n, m: (j, m, 0))
    return pl.pallas_call(
        body, grid=(N_DEV, n_dim // tn, nm),
        in_specs=[h_spec, h_spec, pl.BlockSpec((tm, tn), lambda j, n, m: (m, n))],
        out_specs=pl.BlockSpec((1, r_dim, tn), lambda j, n, m: (j, 0, n)),
        out_shape=jax.ShapeDtypeStruct(w.shape, w.dtype), scratch_shapes=[pltpu.VMEM((r_dim, tn), F32)],
        compiler_params=_cparams(("parallel", "parallel", "arbitrary")), name="ffn_down_dw",
    )(g, u, dy)


@jax.custom_vjp
def ffn_down(g, u, w, x):
    return _ffn_down_fwd(g, u, w, x)


def _ffn_down_vjp_fwd(g, u, w, x):
    return _ffn_down_fwd(g, u, w, x), (g, u, w)


def _ffn_down_vjp_bwd(res, dy):
    g, u, w = res
    dg, du = _ffn_down_dh(dy, g, u, w)
    return dg, du, _ffn_down_dw(g, u, dy, w), dy


ffn_down.defvjp(_ffn_down_vjp_fwd, _ffn_down_vjp_bwd)


def _cols_to_natural_call(g, width):
    _, k_dim, c_dim = g.shape
    tk = _divisor_tile(k_dim, 256, SUBLANE)

    def body(g_ref, o_ref):
        for j in range(N_DEV):
            o_ref[:, j * c_dim:(j + 1) * c_dim] = g_ref[j]
        if width > N_DEV * c_dim:
            o_ref[:, N_DEV * c_dim:] = jnp.zeros((tk, width - N_DEV * c_dim), o_ref.dtype)

    return pl.pallas_call(
        body, grid=(k_dim // tk,), in_specs=[pl.BlockSpec((N_DEV, tk, c_dim), lambda i: (0, i, 0))],
        out_specs=pl.BlockSpec((tk, width), lambda i: (i, 0)), out_shape=jax.ShapeDtypeStruct((k_dim, width), g.dtype),
        compiler_params=_cparams(("parallel",)), name="cols_to_natural",
    )(g)


def _natural_to_cols_call(w, c_dim):
    k_dim, width = w.shape
    tk = _divisor_tile(k_dim, 256, SUBLANE)

    def body(w_ref, o_ref):
        for j in range(N_DEV):
            o_ref[j] = w_ref[:, j * c_dim:(j + 1) * c_dim]

    return pl.pallas_call(
        body, grid=(k_dim // tk,), in_specs=[pl.BlockSpec((tk, width), lambda i: (i, 0))],
        out_specs=pl.BlockSpec((N_DEV, tk, c_dim), lambda i: (0, i, 0)),
        out_shape=jax.ShapeDtypeStruct((N_DEV, k_dim, c_dim), w.dtype),
        compiler_params=_cparams(("parallel",)), name="natural_to_cols",
    )(w)


@functools.partial(jax.custom_vjp, nondiff_argnums=(1,))
def cols_to_natural(g, width):
    return _cols_to_natural_call(g, width)


def _cols_to_natural_fwd(g, width):
    return _cols_to_natural_call(g, width), g.shape[2]


def _cols_to_natural_bwd(width, c_dim, dw):
    return (_natural_to_cols_call(dw, c_dim),)


cols_to_natural.defvjp(_cols_to_natural_fwd, _cols_to_natural_bwd)


def make_rowop(fn, name, tm=256, passthrough=0):
    def specs(rows, params):
        row_specs = [pl.BlockSpec((tm, r.shape[1]), lambda i: (i, 0)) for r in rows]
        par_specs = [pl.BlockSpec(p.shape, lambda i: (0, 0)) for p in params]
        return row_specs, par_specs

    def out_structs(rows, params):
        tiles = [jax.ShapeDtypeStruct((tm, r.shape[1]), r.dtype) for r in rows]
        return jax.eval_shape(lambda r, p: fn(*r, *p), tiles, list(params))

    def fwd_call(rows, params):
        m_dim = rows[0].shape[0]
        n_in = len(rows) + len(params)
        outs = out_structs(rows, params)

        def body(*refs):
            res = fn(*[r[...] for r in refs[:n_in]])
            for o_ref, r in zip(refs[n_in:], res):
                o_ref[...] = r.astype(o_ref.dtype)

        row_specs, par_specs = specs(rows, params)
        return pl.pallas_call(
            body,
            grid=(m_dim // tm,),
            in_specs=row_specs + par_specs,
            out_specs=[pl.BlockSpec((tm, o.shape[1]), lambda i: (i, 0)) for o in outs],
            out_shape=[jax.ShapeDtypeStruct((m_dim, o.shape[1]), o.dtype) for o in outs],
            compiler_params=_cparams(("parallel",)),
            name=name + "_fwd",
        )(*rows, *params)

    def bwd_call(rows, params, cts):
        m_dim = rows[0].shape[0]
        n_rows, n_par = len(rows), len(params)
        n_in = n_rows + n_par
        n_ct = len(cts)
        n_fn = n_ct - passthrough

        def body(*refs):
            vals = [r[...] for r in refs[:n_in]]
            ct_vals = tuple(r[...] for r in refs[n_in:n_in + n_fn])
            pass_refs = refs[n_in + n_fn:n_in + n_ct]
            drow_refs = refs[n_in + n_ct:n_in + n_ct + n_rows]
            dpar_refs = refs[n_in + n_ct + n_rows:]
            _, pullback = jax.vjp(fn, *vals)
            grads = pullback(ct_vals)
            for i, (d_ref, g) in enumerate(zip(drow_refs, grads[:n_rows])):
                d_ref[...] = g + pass_refs[i][...] if i < passthrough else g

            @pl.when(pl.program_id(0) == 0)
            def _():
                for d_ref in dpar_refs:
                    d_ref[...] = jnp.zeros_like(d_ref)

            for d_ref, g in zip(dpar_refs, grads[n_rows:]):
                d_ref[...] += g

        row_specs, par_specs = specs(rows, params)
        ct_specs = [pl.BlockSpec((tm, c.shape[1]), lambda i: (i, 0)) for c in cts]
        res = pl.pallas_call(
            body,
            grid=(m_dim // tm,),
            in_specs=row_specs + par_specs + ct_specs,
            out_specs=row_specs + par_specs,
            out_shape=[jax.ShapeDtypeStruct(r.shape, r.dtype) for r in rows]
            + [jax.ShapeDtypeStruct(p.shape, p.dtype) for p in params],
            compiler_params=_cparams(("arbitrary",)),
            name=name + "_bwd",
        )(*rows, *params, *cts)
        return tuple(res[:n_rows]), tuple(res[n_rows:])

    @jax.custom_vjp
    def op(rows, params):
        return tuple(fwd_call(rows, params)) + tuple(rows[:passthrough])

    def op_fwd(rows, params):
        return tuple(fwd_call(rows, params)) + tuple(rows[:passthrough]), (rows, params)

    def op_bwd(res, cts):
        rows, params = res
        return bwd_call(rows, params, tuple(cts))

    op.defvjp(op_fwd, op_bwd)
    return op


def _rms(x, g):
    return x * lax.rsqrt(jnp.mean(x * x, axis=-1, keepdims=True) + RMS_EPS) * g


def _sigmoid(x):
    return 1.0 / (1.0 + jnp.exp(-x))


def _silu(x):
    return x * _sigmoid(x)


def _bdot(a, b, dims=(((1,), (0,)), ((), ()))):
    return lax.dot_general(a.astype(BF16), b.astype(BF16), dims, preferred_element_type=F32)


def _rmsnorm_fn(x, g):
    return (_rms(x, g),)


def rmsnorm(x, g, name):
    return make_rowop(_rmsnorm_fn, name)((x,), (g.reshape(1, -1),))[0]


def rmsnorm_res(x, g, name):
    return make_rowop(_rmsnorm_fn, name, passthrough=1)((x,), (g.reshape(1, -1),))


def _gelu_tanh(x):
    return 0.5 * x * (1.0 + jnp.tanh(0.7978845608028654 * (x + 0.044715 * x * x * x)))


def _softplus(x):
    return jnp.maximum(x, 0.0) + jnp.log(1.0 + jnp.exp(-jnp.abs(x)))


def _s5_post_fn(y, w_glu, b_glu):
    h = _gelu_tanh(y)
    return (h * _sigmoid(_bdot(h, w_glu) + b_glu),)


def _loss_fn(y, t):
    err = y - t
    return (0.5 * jnp.mean(err * err, axis=-1, keepdims=True),)


def _pair_headnorm(x, g2):
    lo = lax.broadcasted_iota(jnp.int32, x.shape, 1) < CA_HEAD_DIM
    sq = x * x
    s_lo = jnp.sum(jnp.where(lo, sq, 0.0), axis=-1, keepdims=True)
    s_hi = jnp.sum(jnp.where(lo, 0.0, sq), axis=-1, keepdims=True)
    ms = jnp.where(lo, s_lo, s_hi) * (1.0 / CA_HEAD_DIM)
    return x * lax.rsqrt(ms + RMS_EPS) * g2


def _ca_qknorm_fn(qkv, qg2, kg2):
    qs, ks = [], []
    for j in range(D_MODEL // LANE):
        qs.append(_pair_headnorm(qkv[:, j * LANE:(j + 1) * LANE], qg2))
        ks.append(_pair_headnorm(qkv[:, D_MODEL + j * LANE:D_MODEL + (j + 1) * LANE], kg2))
    return jnp.concatenate(qs, axis=1), jnp.concatenate(ks, axis=1)


def _xattn_fn(q, k, v, qg, kg):
    outs = []
    for h in range(XA_HEADS):
        sl = slice(h * XA_HEAD_DIM, (h + 1) * XA_HEAD_DIM)
        qh = _rms(q[:, sl], qg)
        kh = _rms(k[:, sl], kg)
        s = _bdot(qh, kh, (((1,), (1,)), ((), ()))) * (XA_HEAD_DIM ** -0.5)
        p = jnp.exp(s - jnp.max(s, axis=-1, keepdims=True))
        p = p / jnp.sum(p, axis=-1, keepdims=True)
        outs.append(_bdot(p, v[:, sl]))
    return (jnp.concatenate(outs, axis=1),)


def _gdn_prep_fn(x0, x1, x2, x3, ab, conv_w, alog, dtb):
    c = conv_w[3:4, :] * x0 + conv_w[2:3, :] * x1 + conv_w[1:2, :] * x2 + conv_w[0:1, :] * x3
    c = _silu(c)
    qs, ks = [], []
    for h in range(GDN_HEADS):
        qh = c[:, h * LANE:(h + 1) * LANE]
        kh = c[:, GDN_WIDTH + h * LANE:GDN_WIDTH + (h + 1) * LANE]
        qs.append(qh * lax.rsqrt(jnp.sum(qh * qh, axis=-1, keepdims=True) + RMS_EPS) * (GDN_HEAD_DIM ** -0.5))
        ks.append(kh * lax.rsqrt(jnp.sum(kh * kh, axis=-1, keepdims=True) + RMS_EPS))
    lane = lax.broadcasted_iota(jnp.int32, ab.shape, 1)
    g = -jnp.exp(alog) * _softplus(ab + dtb)
    beta = _sigmoid(ab)
    bg = jnp.where(lane < GDN_HEADS, g, jnp.where(lane < 2 * GDN_HEADS, beta, 0.0))
    return jnp.concatenate(qs, axis=1), jnp.concatenate(ks, axis=1), c[:, 2 * GDN_WIDTH:], bg


def _gdn_out_fn(o, gate, og):
    outs = []
    for h in range(GDN_HEADS):
        sl = slice(h * LANE, (h + 1) * LANE)
        outs.append(_rms(o[:, sl], og) * _silu(gate[:, sl]))
    return (jnp.concatenate(outs, axis=1),)


CA_QB = 4 * CHUNK
CA_KB = CA_QB + CA_PAD


def _ca_math(q2, kb2, vb2, bias2, c):
    lane = lax.broadcasted_iota(jnp.int32, q2.shape, 1)
    qc = lax.broadcasted_iota(jnp.int32, (CA_QB, CA_KB), 0) // CHUNK
    kc = lax.broadcasted_iota(jnp.int32, (CA_QB, CA_KB), 1) // CHUNK
    valid = (kc >= qc) & (kc <= qc + CA_LEFT) & (kc + c * (CA_QB // CHUNK) >= CA_LEFT)
    out = jnp.zeros(q2.shape, F32)
    for h in range(2):
        mine = (lane >= h * CA_HEAD_DIM) & (lane < (h + 1) * CA_HEAD_DIM)
        qh = jnp.where(mine, q2, 0.0)
        s = _bdot(qh, kb2, (((1,), (1,)), ((), ()))) * (CA_HEAD_DIM ** -0.5) + bias2[h]
        s = jnp.where(valid, s, -1e30)
        p = jnp.exp(s - jnp.max(s, axis=-1, keepdims=True))
        p = p / jnp.sum(p, axis=-1, keepdims=True)
        out = out + jnp.where(mine, _bdot(p, vb2), 0.0)
    return out


CA_VEC = CA_QB + CA_KB


def _ca_specs(seq):
    q_spec = pl.BlockSpec((CA_QB, LANE), lambda hp, c: (c, hp))
    kv_spec = pl.BlockSpec((seq + CA_PAD, LANE), lambda hp, c: (0, hp))
    b_spec = pl.BlockSpec((1, 2, CA_VEC), lambda hp, c: (hp, 0, 0))
    return (D_MODEL // LANE, seq // CA_QB), q_spec, kv_spec, b_spec


def _ca_bias_from_vector(vec_ref, bias_ref):
    for h in range(2):
        rows = jnp.broadcast_to(vec_ref[0, h:h + 1, :], (CA_QB, CA_VEC))
        bias_ref[h] = pltpu.roll(rows, 0, 1, stride=1, stride_axis=0)[:, CA_QB:]


def _ca_vector_grad(dbias):
    d = jnp.concatenate([jnp.zeros((CA_QB, CA_QB), F32), dbias], axis=1)
    row = lax.broadcasted_iota(jnp.int32, d.shape, 0)
    for bit in range(CA_QB.bit_length() - 1):
        d = jnp.where((row >> bit) & 1 == 1, pltpu.roll(d, CA_VEC - (1 << bit), 1), d)
    return jnp.sum(d, axis=0, keepdims=True)


def _ca_fwd_call(q, kpad, vpad, vec):
    grid, q_spec, kv_spec, b_spec = _ca_specs(q.shape[0])

    def body(q_ref, k_ref, v_ref, vec_ref, o_ref, bias_ref):
        c = pl.program_id(1)
        start = pl.multiple_of(c * CA_QB, CA_QB)

        @pl.when(c == 0)
        def _():
            _ca_bias_from_vector(vec_ref, bias_ref)

        o_ref[...] = _ca_math(q_ref[...], k_ref[pl.ds(start, CA_KB), :], v_ref[pl.ds(start, CA_KB), :],
                              bias_ref[...], c)

    return pl.pallas_call(
        body, grid=grid, in_specs=[q_spec, kv_spec, kv_spec, b_spec], out_specs=q_spec,
        out_shape=jax.ShapeDtypeStruct(q.shape, F32), scratch_shapes=[pltpu.VMEM((2, CA_QB, CA_KB), F32)],
        compiler_params=_cparams(("parallel", "arbitrary")), name="chunkattn_fwd",
    )(q, kpad, vpad, vec)


def _ca_bwd_call(q, kpad, vpad, vec, do):
    grid, q_spec, kv_spec, b_spec = _ca_specs(q.shape[0])
    last = grid[1] - 1

    def body(q_ref, k_ref, v_ref, vec_ref, do_ref, dq_ref, dk_ref, dv_ref, dvec_ref, bias_ref, dbias_ref):
        c = pl.program_id(1)
        start = pl.multiple_of(c * CA_QB, CA_QB)

        @pl.when(c == 0)
        def _():
            _ca_bias_from_vector(vec_ref, bias_ref)
            dk_ref[...] = jnp.zeros_like(dk_ref)
            dv_ref[...] = jnp.zeros_like(dv_ref)
            dbias_ref[...] = jnp.zeros_like(dbias_ref)

        _, pullback = jax.vjp(lambda a, b, d, e: _ca_math(a, b, d, e, c), q_ref[...],
                              k_ref[pl.ds(start, CA_KB), :], v_ref[pl.ds(start, CA_KB), :], bias_ref[...])
        dq, dkb, dvb, dbias = pullback(do_ref[...])
        dq_ref[...] = dq
        dk_ref[pl.ds(start, CA_KB), :] += dkb
        dv_ref[pl.ds(start, CA_KB), :] += dvb
        dbias_ref[...] += dbias

        @pl.when(c == last)
        def _():
            for h in range(2):
                dvec_ref[0, h:h + 1, :] = _ca_vector_grad(dbias_ref[h])

    return pl.pallas_call(
        body, grid=grid, in_specs=[q_spec, kv_spec, kv_spec, b_spec, q_spec],
        out_specs=[q_spec, kv_spec, kv_spec, b_spec],
        out_shape=[jax.ShapeDtypeStruct(q.shape, F32), jax.ShapeDtypeStruct(kpad.shape, F32),
                   jax.ShapeDtypeStruct(vpad.shape, F32), jax.ShapeDtypeStruct(vec.shape, F32)],
        scratch_shapes=[pltpu.VMEM((2, CA_QB, CA_KB), F32), pltpu.VMEM((2, CA_QB, CA_KB), F32)],
        compiler_params=_cparams(("parallel", "arbitrary")), name="chunkattn_bwd",
    )(q, kpad, vpad, vec, do)


@jax.custom_vjp
def chunk_attn_core(q, kpad, vpad, vec):
    return _ca_fwd_call(q, kpad, vpad, vec)


def _ca_core_fwd(q, kpad, vpad, vec):
    return _ca_fwd_call(q, kpad, vpad, vec), (q, kpad, vpad, vec)


def _ca_core_bwd(res, do):
    return tuple(_ca_bwd_call(*res, do))


chunk_attn_core.defvjp(_ca_core_fwd, _ca_core_bwd)


S5_GB = 4
S5_U = S5_WIDTH // S5_GB
S5_L = S5_GROUPS * S5_STATE // S5_GB


def _cmul(ar, ai, br, bi):
    return ar * br - ai * bi, ar * bi + ai * br


def _hdot(a, b, dims=(((1,), (0,)), ((), ()))):
    return lax.dot_general(a, b, dims, precision=HI, preferred_element_type=F32)


def _split_dot(a, b, dims=(((1,), (0,)), ((), ()))):
    a_hi, b_hi = a.astype(BF16), b.astype(BF16)
    a_lo = (a - a_hi.astype(F32)).astype(BF16)
    b_lo = (b - b_hi.astype(F32)).astype(BF16)
    dot = functools.partial(lax.dot_general, dimension_numbers=dims, preferred_element_type=F32)
    return dot(a_hi, b_hi) + (dot(a_hi, b_lo) + dot(a_lo, b_hi))


_NT = (((1,), (1,)), ((), ()))
_TN = (((0,), (0,)), ((), ()))


def _s5_tables(lr, li, reverse):
    p = {1: (lr, li)}
    p[2] = _cmul(*p[1], *p[1])
    p[4] = _cmul(*p[2], *p[2])
    p[3] = _cmul(*p[2], *p[1])
    p[5] = _cmul(*p[4], *p[1])
    p[6] = _cmul(*p[4], *p[2])
    p[7] = _cmul(*p[4], *p[3])
    p[8] = _cmul(*p[4], *p[4])
    row = lax.broadcasted_iota(jnp.int32, (SUBLANE, lr.shape[1]), 0)
    tr = jnp.zeros(row.shape, F32)
    ti = jnp.zeros(row.shape, F32)
    for i in range(SUBLANE):
        k = SUBLANE - i if reverse else i + 1
        tr = jnp.where(row == i, p[k][0], tr)
        ti = jnp.where(row == i, p[k][1], ti)
    return p, (tr, ti), row


def _s5_block_scan(xr, xi, p, tab, row, hr, hi, reverse):
    for k in (1, 2, 4):
        if reverse:
            sr = jnp.where(row < SUBLANE - k, pltpu.roll(xr, SUBLANE - k, 0), 0.0)
            si = jnp.where(row < SUBLANE - k, pltpu.roll(xi, SUBLANE - k, 0), 0.0)
        else:
            sr = jnp.where(row >= k, pltpu.roll(xr, k, 0), 0.0)
            si = jnp.where(row >= k, pltpu.roll(xi, k, 0), 0.0)
        ar, ai = _cmul(p[k][0], p[k][1], sr, si)
        xr, xi = xr + ar, xi + ai
    cr, ci = _cmul(tab[0], tab[1], hr, hi)
    return xr + cr, xi + ci


def _s5_forward_scan(sr_ref, si_ref, lr, li):
    n_blocks = sr_ref.shape[0] // SUBLANE
    p, tab, row = _s5_tables(lr, li, False)

    def step(b, carry):
        base = pl.multiple_of(b * SUBLANE, SUBLANE)
        xr, xi = _s5_block_scan(sr_ref[pl.ds(base, SUBLANE), :], si_ref[pl.ds(base, SUBLANE), :],
                                p, tab, row, carry[0], carry[1], False)
        sr_ref[pl.ds(base, SUBLANE), :] = xr
        si_ref[pl.ds(base, SUBLANE), :] = xi
        return xr[SUBLANE - 1:SUBLANE, :], xi[SUBLANE - 1:SUBLANE, :]

    zero = jnp.zeros((1, lr.shape[1]), F32)
    lax.fori_loop(0, n_blocks, step, (zero, zero))


def _s5_specs(seq):
    u_spec = pl.BlockSpec((seq, S5_U), lambda g: (0, g))
    bd_spec = pl.BlockSpec((1, S5_U, S5_L), lambda g: (g, 0, 0))
    cd_spec = pl.BlockSpec((1, S5_L, S5_U), lambda g: (g, 0, 0))
    lam_spec = pl.BlockSpec((1, 2, S5_L), lambda g: (g, 0, 0))
    d_spec = pl.BlockSpec((1, S5_U), lambda g: (0, g))
    return u_spec, bd_spec, cd_spec, lam_spec, d_spec


S5_ROWS = 256


def _row_chunks(seq, fn):
    rows_per = min(S5_ROWS, seq)

    def step(r, carry):
        fn(pl.ds(pl.multiple_of(r * rows_per, rows_per), rows_per))
        return carry

    lax.fori_loop(0, seq // rows_per, step, 0)


def _s5_fwd_call(u, bdr, bdi, cdr, cdi, lam, d):
    seq = u.shape[0]
    u_spec, bd_spec, cd_spec, lam_spec, d_spec = _s5_specs(seq)

    def body(u_ref, bdr_ref, bdi_ref, cdr_ref, cdi_ref, lam_ref, d_ref, y_ref, sr_ref, si_ref):
        def project_in(rows):
            uv = u_ref[rows, :]
            sr_ref[rows, :] = _bdot(uv, bdr_ref[0])
            si_ref[rows, :] = _bdot(uv, bdi_ref[0])

        def project_out(rows):
            y_ref[rows, :] = (_bdot(sr_ref[rows, :], cdr_ref[0]) - _bdot(si_ref[rows, :], cdi_ref[0])
                              + d_ref[...] * u_ref[rows, :])

        _row_chunks(seq, project_in)
        _s5_forward_scan(sr_ref, si_ref, lam_ref[0, 0:1, :], lam_ref[0, 1:2, :])
        _row_chunks(seq, project_out)

    return pl.pallas_call(
        body, grid=(S5_GB,), in_specs=[u_spec, bd_spec, bd_spec, cd_spec, cd_spec, lam_spec, d_spec],
        out_specs=u_spec, out_shape=jax.ShapeDtypeStruct(u.shape, F32),
        scratch_shapes=[pltpu.VMEM((seq, S5_L), F32), pltpu.VMEM((seq, S5_L), F32)],
        compiler_params=_cparams(("parallel",)), name="s5_fwd",
    )(u, bdr, bdi, cdr, cdi, lam, d)


def _s5_bwd_call(u, bdr, bdi, cdr, cdi, lam, d, dy):
    seq = u.shape[0]
    n_blocks = seq // SUBLANE
    u_spec, bd_spec, cd_spec, lam_spec, d_spec = _s5_specs(seq)

    def body(u_ref, bdr_ref, bdi_ref, cdr_ref, cdi_ref, lam_ref, d_ref, dy_ref,
             du_ref, dbdr_ref, dbdi_ref, dcdr_ref, dcdi_ref, dlam_ref, dd_ref, sr_ref, si_ref, gr_ref, gi_ref):
        lr, li = lam_ref[0, 0:1, :], lam_ref[0, 1:2, :]

        def project_in(rows):
            uv = u_ref[rows, :]
            dyv = dy_ref[rows, :]
            sr_ref[rows, :] = _bdot(uv, bdr_ref[0])
            si_ref[rows, :] = _bdot(uv, bdi_ref[0])
            gr_ref[rows, :] = _bdot(dyv, cdr_ref[0], _NT)
            gi_ref[rows, :] = -_bdot(dyv, cdi_ref[0], _NT)

        _row_chunks(seq, project_in)
        _s5_forward_scan(sr_ref, si_ref, lr, li)
        p, tab, row = _s5_tables(lr, -li, True)

        def step(i, carry):
            hr, hi, acc_r, acc_i = carry
            b = n_blocks - 1 - i
            base = pl.multiple_of(b * SUBLANE, SUBLANE)
            xr, xi = _s5_block_scan(gr_ref[pl.ds(base, SUBLANE), :], gi_ref[pl.ds(base, SUBLANE), :],
                                    p, tab, row, hr, hi, True)
            gr_ref[pl.ds(base, SUBLANE), :] = xr
            gi_ref[pl.ds(base, SUBLANE), :] = xi
            prev = pl.multiple_of(jnp.maximum(b - 1, 0) * SUBLANE, SUBLANE)
            keep = (b > 0).astype(F32)
            last_r = sr_ref[pl.ds(prev, SUBLANE), :][SUBLANE - 1:SUBLANE, :] * keep
            last_i = si_ref[pl.ds(prev, SUBLANE), :][SUBLANE - 1:SUBLANE, :] * keep
            pr = jnp.where(row >= 1, pltpu.roll(sr_ref[pl.ds(base, SUBLANE), :], 1, 0), last_r)
            pi = jnp.where(row >= 1, pltpu.roll(si_ref[pl.ds(base, SUBLANE), :], 1, 0), last_i)
            acc_r = acc_r + pr * xr + pi * xi
            acc_i = acc_i + pr * xi - pi * xr
            return xr[0:1, :], xi[0:1, :], acc_r, acc_i

        zero = jnp.zeros((1, S5_L), F32)
        zacc = jnp.zeros((SUBLANE, S5_L), F32)
        _, _, acc_r, acc_i = lax.fori_loop(0, n_blocks, step, (zero, zero, zacc, zacc))
        dlam_ref[0, 0:1, :] = jnp.sum(acc_r, axis=0, keepdims=True)
        dlam_ref[0, 1:2, :] = jnp.sum(acc_i, axis=0, keepdims=True)
        for ref in (dbdr_ref, dbdi_ref, dcdr_ref, dcdi_ref, dd_ref):
            ref[...] = jnp.zeros_like(ref)

        def grads(rows):
            uv, dyv = u_ref[rows, :], dy_ref[rows, :]
            grv, giv = gr_ref[rows, :], gi_ref[rows, :]
            du_ref[rows, :] = _bdot(grv, bdr_ref[0], _NT) + _bdot(giv, bdi_ref[0], _NT) + d_ref[...] * dyv
            dbdr_ref[0] += _bdot(uv, grv, _TN)
            dbdi_ref[0] += _bdot(uv, giv, _TN)
            dcdr_ref[0] += _bdot(sr_ref[rows, :], dyv, _TN)
            dcdi_ref[0] -= _bdot(si_ref[rows, :], dyv, _TN)
            dd_ref[...] += jnp.sum(dyv * uv, axis=0, keepdims=True)

        _row_chunks(seq, grads)

    scratch = [pltpu.VMEM((seq, S5_L), F32) for _ in range(4)]
    return pl.pallas_call(
        body, grid=(S5_GB,),
        in_specs=[u_spec, bd_spec, bd_spec, cd_spec, cd_spec, lam_spec, d_spec, u_spec],
        out_specs=[u_spec, bd_spec, bd_spec, cd_spec, cd_spec, lam_spec, d_spec],
        out_shape=[jax.ShapeDtypeStruct(a.shape, F32) for a in (u, bdr, bdi, cdr, cdi, lam, d)],
        scratch_shapes=scratch, compiler_params=_cparams(("parallel",)), name="s5_bwd",
    )(u, bdr, bdi, cdr, cdi, lam, d, dy)


@jax.custom_vjp
def s5_core(u, bdr, bdi, cdr, cdi, lam, d):
    return _s5_fwd_call(u, bdr, bdi, cdr, cdi, lam, d)


def _s5_core_fwd(*args):
    return _s5_fwd_call(*args), args


def _s5_core_bwd(res, dy):
    return tuple(_s5_bwd_call(*res, dy))


s5_core.defvjp(_s5_core_fwd, _s5_core_bwd)


def _s5_discretize(a_re, a_im, log_dt, b_re, b_im, c_re, c_im, d):
    dt = jnp.exp(log_dt)[:, None]
    mag = jnp.exp(a_re * dt)
    lbr, lbi = mag * jnp.cos(a_im * dt), mag * jnp.sin(a_im * dt)
    den = a_re * a_re + a_im * a_im
    fr = ((lbr - 1.0) * a_re + lbi * a_im) / den
    fi = (lbi * a_re - (lbr - 1.0) * a_im) / den
    bbr = fr[:, :, None] * b_re - fi[:, :, None] * b_im
    bbi = fr[:, :, None] * b_im + fi[:, :, None] * b_re
    eye = jnp.eye(S5_GROUPS // S5_GB, dtype=F32)
    gl = S5_GROUPS // S5_GB

    def bd(t):
        return jnp.einsum('bgpc,gh->bgchp', t.reshape(S5_GB, gl, S5_STATE, S5_GROUP), eye).reshape(S5_GB, S5_U, S5_L)

    def cd(t):
        return jnp.einsum('bgcp,gh->bgphc', t.reshape(S5_GB, gl, S5_GROUP, S5_STATE), eye).reshape(S5_GB, S5_L, S5_U)

    lam = jnp.stack([lbr.reshape(S5_GB, S5_L), lbi.reshape(S5_GB, S5_L)], axis=1)
    return bd(bbr), bd(bbi), cd(c_re), cd(c_im), lam, d.reshape(1, S5_WIDTH)


@jax.custom_vjp
def _unit_lower_solve(neg_a, rhs, tinv):
    return _split_dot(tinv, rhs)


def _unit_lower_solve_fwd(neg_a, rhs, tinv):
    x = _split_dot(tinv, rhs)
    return x, (x, tinv)


def _unit_lower_solve_bwd(res, dx):
    x, tinv = res
    g = _split_dot(tinv, dx, _TN)
    return _split_dot(g, x, _NT), g, jnp.zeros_like(tinv)


_unit_lower_solve.defvjp(_unit_lower_solve_fwd, _unit_lower_solve_bwd)


def _unit_lower_inverse(neg_a):
    r = lax.broadcasted_iota(jnp.int32, neg_a.shape, 0)
    c = lax.broadcasted_iota(jnp.int32, neg_a.shape, 1)
    p = (r == c).astype(F32) + neg_a
    npow = _split_dot(neg_a, neg_a)
    for _ in range(4):
        y = _split_dot(jnp.concatenate([p, npow], axis=0), npow)
        p = p + y[:CHUNK]
        npow = y[CHUNK:]
    return p + _split_dot(p, npow)


def _gdn_chunk(q, k, v, g_col, b_col, st, tinv=None):
    r = lax.broadcasted_iota(jnp.int32, (CHUNK, CHUNK), 0)
    c = lax.broadcasted_iota(jnp.int32, (CHUNK, CHUNK), 1)
    eye = (r == c).astype(F32)
    strict = r > c
    causal = r >= c
    g_row = jnp.sum(g_col * eye, axis=0, keepdims=True)
    gcum = jnp.sum(jnp.where(causal, g_row, 0.0), axis=1, keepdims=True)
    gcum_row = jnp.sum(gcum * eye, axis=0, keepdims=True)
    diff = gcum - gcum_row
    decay_strict = jnp.where(strict, jnp.exp(jnp.where(strict, diff, 0.0)), 0.0)
    decay_causal = jnp.where(causal, jnp.exp(jnp.where(causal, diff, 0.0)), 0.0)
    gamma = jnp.exp(gcum)
    g_last = jnp.sum(jnp.where(lax.broadcasted_iota(jnp.int32, (CHUNK, 1), 0) == CHUNK - 1, gcum, 0.0),
                     axis=0, keepdims=True)
    kk = _bdot(k, k, _NT)
    neg_a = -(b_col * kk * decay_strict)
    if tinv is None:
        tinv = _unit_lower_inverse(neg_a)
    x = _unit_lower_solve(neg_a, jnp.concatenate([b_col * v, (b_col * gamma) * k], axis=1), lax.stop_gradient(tinv))
    u_new, w_k = x[:, :GDN_HEAD_DIM], x[:, GDN_HEAD_DIM:]
    qk = _bdot(q, k, _NT) * decay_causal
    q_g = q * gamma
    k_tail = k * jnp.exp(g_last - gcum)
    w = u_new - _bdot(w_k, st)
    o = _bdot(q_g, st) + _bdot(qk, w)
    st_new = jnp.exp(g_last) * st + _bdot(k_tail, w, _TN)
    return o, st_new, tinv


def _gdn_cols(bgv, h):
    lane = lax.broadcasted_iota(jnp.int32, bgv.shape, 1)
    g_col = jnp.sum(jnp.where(lane == h, bgv, 0.0), axis=1, keepdims=True)
    b_col = jnp.sum(jnp.where(lane == GDN_HEADS + h, bgv, 0.0), axis=1, keepdims=True)
    return g_col, b_col


GDN_CPS = 4


def _gdn_fwd_call(q, k, v, bg):
    seq = q.shape[0]
    n_chunks = seq // CHUNK
    n_steps = n_chunks // GDN_CPS
    rows = GDN_CPS * CHUNK
    x_spec = pl.BlockSpec((rows, GDN_WIDTH), lambda n: (n, 0))
    bg_spec = pl.BlockSpec((rows, LANE), lambda n: (n, 0))
    st_spec = pl.BlockSpec((GDN_CPS, GDN_WIDTH, GDN_HEAD_DIM), lambda n: (n, 0, 0))
    ti_spec = pl.BlockSpec((GDN_CPS, GDN_HEADS * CHUNK, CHUNK), lambda n: (n, 0, 0))

    def body(q_ref, k_ref, v_ref, bg_ref, o_ref, st_out_ref, ti_out_ref, st_ref):
        @pl.when(pl.program_id(0) == 0)
        def _():
            st_ref[...] = jnp.zeros_like(st_ref)

        for h in range(GDN_HEADS):
            sl = slice(h * GDN_HEAD_DIM, (h + 1) * GDN_HEAD_DIM)
            st = st_ref[sl, :]
            for cc in range(GDN_CPS):
                rs = slice(cc * CHUNK, (cc + 1) * CHUNK)
                g_col, b_col = _gdn_cols(bg_ref[rs, :], h)
                st_out_ref[cc, sl, :] = st
                o, st, tinv = _gdn_chunk(q_ref[rs, sl], k_ref[rs, sl], v_ref[rs, sl], g_col, b_col, st)
                o_ref[rs, sl] = o
                ti_out_ref[cc, h * CHUNK:(h + 1) * CHUNK, :] = tinv
            st_ref[sl, :] = st

    return pl.pallas_call(
        body, grid=(n_steps,), in_specs=[x_spec, x_spec, x_spec, bg_spec], out_specs=[x_spec, st_spec, ti_spec],
        out_shape=[jax.ShapeDtypeStruct(q.shape, F32),
                   jax.ShapeDtypeStruct((n_chunks, GDN_WIDTH, GDN_HEAD_DIM), F32),
                   jax.ShapeDtypeStruct((n_chunks, GDN_HEADS * CHUNK, CHUNK), F32)],
        scratch_shapes=[pltpu.VMEM((GDN_WIDTH, GDN_HEAD_DIM), F32)],
        compiler_params=_cparams(("arbitrary",)), name="gdn_fwd",
    )(q, k, v, bg)


def _gdn_bwd_call(q, k, v, bg, states, tinvs, do):
    seq = q.shape[0]
    n_steps = seq // CHUNK // GDN_CPS
    rows = GDN_CPS * CHUNK
    x_spec = pl.BlockSpec((rows, GDN_WIDTH), lambda i: (n_steps - 1 - i, 0))
    bg_spec = pl.BlockSpec((rows, LANE), lambda i: (n_steps - 1 - i, 0))
    st_spec = pl.BlockSpec((GDN_CPS, GDN_WIDTH, GDN_HEAD_DIM), lambda i: (n_steps - 1 - i, 0, 0))
    ti_spec = pl.BlockSpec((GDN_CPS, GDN_HEADS * CHUNK, CHUNK), lambda i: (n_steps - 1 - i, 0, 0))

    def body(q_ref, k_ref, v_ref, bg_ref, st_in_ref, ti_ref, do_ref, dq_ref, dk_ref, dv_ref, dbg_ref, dst_ref):
        @pl.when(pl.program_id(0) == 0)
        def _():
            dst_ref[...] = jnp.zeros_like(dst_ref)

        lane = lax.broadcasted_iota(jnp.int32, (CHUNK, LANE), 1)
        dbg = [jnp.zeros((CHUNK, LANE), F32) for _ in range(GDN_CPS)]
        for h in range(GDN_HEADS):
            sl = slice(h * GDN_HEAD_DIM, (h + 1) * GDN_HEAD_DIM)
            dst = dst_ref[sl, :]
            for cc in reversed(range(GDN_CPS)):
                rs = slice(cc * CHUNK, (cc + 1) * CHUNK)
                g_col, b_col = _gdn_cols(bg_ref[rs, :], h)
                tinv = ti_ref[cc, h * CHUNK:(h + 1) * CHUNK, :]
                _, pullback = jax.vjp(lambda *a: _gdn_chunk(*a, tinv=tinv)[:2], q_ref[rs, sl], k_ref[rs, sl],
                                      v_ref[rs, sl], g_col, b_col, st_in_ref[cc, sl, :])
                dq, dk, dv, dg, db, dst = pullback((do_ref[rs, sl], dst))
                dq_ref[rs, sl] = dq
                dk_ref[rs, sl] = dk
                dv_ref[rs, sl] = dv
                dbg[cc] = dbg[cc] + jnp.where(lane == h, dg, 0.0) + jnp.where(lane == GDN_HEADS + h, db, 0.0)
            dst_ref[sl, :] = dst
        for cc in range(GDN_CPS):
            dbg_ref[cc * CHUNK:(cc + 1) * CHUNK, :] = dbg[cc]

    return pl.pallas_call(
        body, grid=(n_steps,), in_specs=[x_spec, x_spec, x_spec, bg_spec, st_spec, ti_spec, x_spec],
        out_specs=[x_spec, x_spec, x_spec, bg_spec],
        out_shape=[jax.ShapeDtypeStruct(q.shape, F32)] * 3 + [jax.ShapeDtypeStruct(bg.shape, F32)],
        scratch_shapes=[pltpu.VMEM((GDN_WIDTH, GDN_HEAD_DIM), F32)],
        compiler_params=_cparams(("arbitrary",)), name="gdn_bwd",
    )(q, k, v, bg, states, tinvs, do)


@jax.custom_vjp
def gdn_core(q, k, v, bg):
    return _gdn_fwd_call(q, k, v, bg)[0]


def _gdn_core_fwd(q, k, v, bg):
    o, states, tinvs = _gdn_fwd_call(q, k, v, bg)
    return o, (q, k, v, bg, states, tinvs)


def _gdn_core_bwd(res, do):
    return tuple(_gdn_bwd_call(*res, do))


gdn_core.defvjp(_gdn_core_fwd, _gdn_core_bwd)


def _row(v):
    return v.reshape(1, -1)


def _lane_pad(v):
    return jnp.pad(v, (0, LANE - v.shape[0])).reshape(1, LANE)


def _delay_rows(x, k):
    return jnp.pad(x, ((k, 0), (0, 0)))[:x.shape[0]]


def s5_mixer(u, a_re, a_im, log_dt, b_re, b_im, c_re, c_im, d, w_glu, b_glu):
    y = s5_core(u, *_s5_discretize(a_re, a_im, log_dt, b_re, b_im, c_re, c_im, d))
    return make_rowop(_s5_post_fn, "s5_post")((y,), (w_glu, _row(b_glu)))[0]


def gated_deltanet(qkv, gate, ab, conv_w, a_log, dt_bias, out_g):
    rows = (qkv, _delay_rows(qkv, 1), _delay_rows(qkv, 2), _delay_rows(qkv, 3), ab)
    q, k, v, bg = make_rowop(_gdn_prep_fn, "gdn_prep")(rows, (conv_w, _lane_pad(a_log), _lane_pad(dt_bias)))
    o = gdn_core(q, k, v, bg)
    return make_rowop(_gdn_out_fn, "gdn_out")((o, gate), (_row(out_g),))[0]


def chunk_attention(x, h, w_qkv, w_out, q_g, k_g, rel_bias):
    qkv = linear_cols(h, w_qkv)
    qn, kn = make_rowop(_ca_qknorm_fn, "ca_qknorm")((qkv,), (_row(jnp.tile(q_g, 2)), _row(jnp.tile(k_g, 2))))
    kpad = jnp.pad(kn, ((CA_PAD, 0), (0, 0)))
    vpad = jnp.pad(qkv[:, 2 * D_MODEL:], ((CA_PAD, 0), (0, 0)))
    o = chunk_attn_core(qn, kpad, vpad, _rel_bias_vector(rel_bias))
    return linear_res(o, w_out, x)


def memory_cross_attention(x, h, mem_n, w_q, w_kv, w_out, q_g, k_g):
    q = linear(h, w_q)
    kv = linear_cols(mem_n, w_kv)
    o = make_rowop(_xattn_fn, "xattn", tm=512)((q,), (kv[:, :D_MODEL], kv[:, D_MODEL:], _row(q_g), _row(k_g)))[0]
    return linear_res(o, w_out, x)


def swiglu(x, h, w_gate, w_up, w_down):
    return ffn_down(linear_cols_dm(h, w_gate), linear_cols_dm(h, w_up), w_down, x)


def _rel_bias_vector(rel_bias):
    heads = rel_bias.shape[0]
    n_far = CA_KB - 1 - MAX_REL
    n_neg = CA_VEC - 1 - n_far - (2 * MAX_REL + 1)
    vec = jnp.concatenate([jnp.zeros((heads, 1), F32),
                           jnp.broadcast_to(rel_bias[:, 2 * MAX_REL:], (heads, n_far)),
                           jnp.flip(rel_bias, axis=1),
                           jnp.broadcast_to(rel_bias[:, :1], (heads, n_neg))], axis=1)
    return vec.reshape(heads // 2, 2, CA_VEC)


_HBM_SPEC = pl.BlockSpec(memory_space=pltpu.HBM)
_SEM_SPEC = pl.BlockSpec(memory_space=pltpu.SEMAPHORE)
_SIDE_EFFECT = pltpu.SideEffectType.DATAFLOW_SIDE_EFFECTING


def _peer(x, y, c, k):
    return (x + (k >> 2)) % 2, (y + ((k >> 1) & 1)) % 2, (c + (k & 1)) % 2


def _exchange_start(arrays, modes, after, name, carry):
    n = len(arrays)
    n_sem = n * (N_DEV - 1)
    lands = [pltpu.with_memory_space_constraint(lax.empty((N_DEV,) + a.shape if m == "gather" else a.shape, a.dtype),
                                                pltpu.HBM) for a, m in zip(arrays, modes)]
    arrays = [pltpu.with_memory_space_constraint(a, pltpu.HBM) for a in arrays]

    def body(*refs):
        ins, zones = refs[:n], refs[n:2 * n]
        send_sems, recv_sems, own_sems = refs[2 * n + 2:2 * n + 5]
        x, y, c = lax.axis_index("x"), lax.axis_index("y"), lax.axis_index("c")
        me = 4 * x + 2 * y + c
        for i in range(n):
            pltpu.make_async_copy(ins[i] if modes[i] == "gather" else ins[i].at[me], zones[i].at[me],
                                  own_sems.at[i]).start()
        for k in range(1, N_DEV):
            px, py, pc = _peer(x, y, c, k)
            peer = 4 * px + 2 * py + pc
            for i in range(n):
                sem = i * (N_DEV - 1) + k - 1
                pltpu.make_async_remote_copy(src_ref=ins[i] if modes[i] == "gather" else ins[i].at[peer],
                                             dst_ref=zones[i].at[me], send_sem=send_sems.at[sem],
                                             recv_sem=recv_sems.at[sem], device_id=(px, py, pc),
                                             device_id_type=pl.DeviceIdType.MESH).start()

    carry = pltpu.with_memory_space_constraint(carry, pltpu.HBM)
    out_shape = ((pltpu.SemaphoreType.DMA((n_sem,)), pltpu.SemaphoreType.DMA((n_sem,)), pltpu.SemaphoreType.DMA((n,)))
                 + tuple(pltpu.HBM(a.shape, a.dtype) for a in arrays) + tuple(pltpu.HBM(z.shape, z.dtype) for z in lands)
                 + (pltpu.HBM(carry.shape, carry.dtype),))
    aliases = {i: 3 + i for i in range(2 * n)}
    aliases[2 * n + 1] = 3 + 2 * n
    res = pl.pallas_call(
        body, name=name, out_shape=out_shape,
        in_specs=[_HBM_SPEC] * (2 * n) + [pl.BlockSpec(memory_space=pl.ANY), _HBM_SPEC],
        out_specs=(_SEM_SPEC,) * 3 + (_HBM_SPEC,) * (2 * n + 1),
        input_output_aliases=aliases,
        compiler_params=pltpu.CompilerParams(has_side_effects=_SIDE_EFFECT),
    )(*arrays, *lands, after, carry)
    return tuple(res[:3]), list(res[3:3 + n]), list(res[3 + n:3 + 2 * n]), res[3 + 2 * n]


def _exchange_wait(started, modes, after, name):
    sems, sources, zones, _ = started
    n = len(sources)

    def body(*refs):
        ins, lands = refs[:n], refs[n:2 * n]
        send_ref, recv_ref, own_ref = refs[2 * n:2 * n + 3]
        x, y, c = lax.axis_index("x"), lax.axis_index("y"), lax.axis_index("c")
        me = 4 * x + 2 * y + c
        for i in range(n):
            pltpu.make_async_copy(ins[i] if modes[i] == "gather" else ins[i].at[me], lands[i].at[me],
                                  own_ref.at[i]).wait()
        for k in range(1, N_DEV):
            px, py, pc = _peer(x, y, c, k)
            peer = 4 * px + 2 * py + pc
            for i in range(n):
                sem = i * (N_DEV - 1) + k - 1
                cp = pltpu.make_async_remote_copy(src_ref=ins[i] if modes[i] == "gather" else ins[i].at[peer],
                                                  dst_ref=lands[i].at[peer], send_sem=send_ref.at[sem],
                                                  recv_sem=recv_ref.at[sem], device_id=(px, py, pc),
                                                  device_id_type=pl.DeviceIdType.MESH)
                cp.wait_send()
                cp.wait_recv()

    res = pl.pallas_call(
        body, name=name,
        out_shape=tuple(pltpu.HBM(a.shape, a.dtype) for a in sources) + tuple(pltpu.HBM(z.shape, z.dtype) for z in zones),
        in_specs=[_HBM_SPEC] * (2 * n) + [_SEM_SPEC] * 3 + [pl.BlockSpec(memory_space=pl.ANY)],
        out_specs=(_HBM_SPEC,) * (2 * n), input_output_aliases={i: i for i in range(2 * n)},
        compiler_params=pltpu.CompilerParams(has_side_effects=_SIDE_EFFECT),
    )(*sources, *zones, *sems, after)
    return list(res[n:])


ADAM_TILE = 64 * 1024


def _adam_call(w, m, v, slots, name):
    n_layers, rows, cols = w.shape
    tr = rows
    if n_layers * rows * cols > ADAM_TILE:
        fits = [t for t in range(SUBLANE, rows, SUBLANE) if rows % t == 0 and n_layers * t * cols <= ADAM_TILE]
        tr = max(fits) if fits else SUBLANE
    c1 = 1.0 - ADAM_B1 ** ADAM_STEP
    c2 = 1.0 - ADAM_B2 ** ADAM_STEP

    def body(*refs):
        w_ref, m_ref, v_ref = refs[:3]
        slot_refs = refs[3:3 + n_layers]
        grad_ref, delta_ref, nm_ref, nv_ref = refs[3 + n_layers:]
        for layer in range(n_layers):
            g = slot_refs[layer][0].astype(F32)
            for k in range(1, N_DEV):
                g = g + slot_refs[layer][k].astype(F32)
            m_new = ADAM_B1 * m_ref[layer] + (1.0 - ADAM_B1) * g
            v_new = ADAM_B2 * v_ref[layer] + (1.0 - ADAM_B2) * (g * g)
            m_hat = m_new / c1
            v_hat = v_new / c2
            grad_ref[layer] = g
            delta_ref[layer] = -ADAM_LR * (m_hat / (jnp.sqrt(v_hat) + ADAM_EPS) + ADAM_WD * w_ref[layer])
            nm_ref[layer] = m_new
            nv_ref[layer] = v_new

    spec = pl.BlockSpec((n_layers, tr, cols), lambda i: (0, i, 0))
    slot_spec = pl.BlockSpec((N_DEV, tr, cols), lambda i: (0, i, 0))
    return pl.pallas_call(
        body, grid=(rows // tr,), in_specs=[spec, spec, spec] + [slot_spec] * n_layers,
        out_specs=[spec] * 4, out_shape=[jax.ShapeDtypeStruct(w.shape, F32)] * 4,
        compiler_params=_cparams(("parallel",)), name=name,
    )(w, m, v, *slots)


WEIGHT_NAMES = ['ab_norm_g', 'ab_w_in', 'ab_w_out', 's5_a_re', 's5_a_im', 's5_log_dt', 's5_b_re', 's5_b_im', 's5_c_re',
                's5_c_im', 's5_d', 's5_w_glu', 's5_b_glu', 'gdn_conv_w', 'gdn_a_log', 'gdn_dt_bias', 'gdn_out_norm_g',
                'c_norm_g', 'c_w_qkv', 'c_w_out', 'c_q_norm_g', 'c_k_norm_g', 'c_rel_bias', 'mem_norm_g', 'xa_norm_g',
                'xa_w_q', 'xa_w_kv', 'xa_w_out', 'xa_q_norm_g', 'xa_k_norm_g', 'f_norm_g', 'f_w_gate', 'f_w_up',
                'f_w_down']

SHARDED = {
    'ab_w_in': ('col', BF16), 'ab_w_out': ('row', BF16), 's5_w_glu': ('row', BF16), 'gdn_conv_w': ('col', F32),
    'c_norm_g': ('col', F32), 'c_w_qkv': ('col', BF16), 'c_w_out': ('row', BF16), 'xa_w_q': ('row', BF16),
    'xa_w_kv': ('col', BF16), 'xa_w_out': ('row', BF16), 'f_w_gate': ('col', BF16), 'f_w_up': ('col', BF16),
    'f_w_down': ('row', BF16),
}
GATHERED_AS_IS = ('c_w_qkv', 'xa_w_kv', 'f_w_gate', 'f_w_up', 'f_w_down')
REPLICATED = [n for n in WEIGHT_NAMES if n not in SHARDED]
PACK_UNIT = SUBLANE * LANE


def _full_from_gathered(g, axis):
    if axis == "row":
        return g.reshape(g.shape[0] * g.shape[1], g.shape[2])
    return jnp.transpose(g, (1, 0, 2)).reshape(g.shape[1], g.shape[0] * g.shape[2])


def _pack(arrays):
    flat = []
    for a in arrays:
        size = a.size
        padded = -(-size // PACK_UNIT) * PACK_UNIT
        flat.append(jnp.pad(a.reshape(-1), (0, padded - size)).reshape(-1, LANE))
    return jnp.concatenate(flat, axis=0)


def _unpack(buf, shapes):
    out, row = [], 0
    for shape in shapes:
        size = math.prod(shape)
        rows = -(-size // PACK_UNIT) * SUBLANE
        out.append(buf[row:row + rows].reshape(-1)[:size].reshape(shape))
        row += rows
    return out


N_STAGES = 2 * DEPTH
EVEN_SHARDED = ['ab_w_in', 'ab_w_out', 's5_w_glu', 'gdn_conv_w']
ODD_SHARDED = ['c_norm_g', 'c_w_qkv', 'c_w_out']
ALL_SHARDED = ['xa_w_q', 'xa_w_kv', 'xa_w_out', 'f_w_gate', 'f_w_up', 'f_w_down']
EVEN_SMALL = ['ab_norm_g', 's5_a_re', 's5_a_im', 's5_log_dt', 's5_b_re', 's5_b_im', 's5_c_re', 's5_c_im', 's5_d',
              's5_b_glu', 'gdn_a_log', 'gdn_dt_bias', 'gdn_out_norm_g']
ODD_SMALL = ['c_q_norm_g', 'c_k_norm_g', 'c_rel_bias']
ALL_SMALL = ['xa_norm_g', 'xa_q_norm_g', 'xa_k_norm_g', 'f_norm_g']


def _stage_params(stage):
    layer, part = divmod(stage, 2)
    if part == 1:
        return [(n, layer) for n in ALL_SHARDED], [(n, layer) for n in ALL_SMALL]
    big, small = (EVEN_SHARDED, EVEN_SMALL) if layer % 2 == 0 else (ODD_SHARDED, ODD_SMALL)
    return [(n, layer // 2) for n in big], [(n, layer // 2) for n in small]


def _stage_forward(stage, landed, small, x, mem_n):
    layer, part = divmod(stage, 2)
    big = {}
    for (n, _), g in zip(_stage_params(stage)[0], landed):
        if n == 'ab_w_in':
            big[n] = cols_to_natural(g, AB_IN_PAD)
        elif n in GATHERED_AS_IS:
            big[n] = g
        elif n == 's5_w_glu':
            big[n] = _full_from_gathered(g, 'row').astype(F32)
        else:
            big[n] = _full_from_gathered(g, SHARDED[n][0])
    if part == 1:
        h, x = rmsnorm_res(x, small['xa_norm_g'], "xa_norm")
        x = memory_cross_attention(x, h, mem_n, big['xa_w_q'], big['xa_w_kv'], big['xa_w_out'],
                                   small['xa_q_norm_g'], small['xa_k_norm_g'])
        h, x = rmsnorm_res(x, small['f_norm_g'], "f_norm")
        return swiglu(x, h, big['f_w_gate'], big['f_w_up'], big['f_w_down'])
    if layer % 2 == 0:
        h, x = rmsnorm_res(x, small['ab_norm_g'], "ab_norm")
        w_in = big['ab_w_in']
        u = linear(h, w_in[:, :S5_WIDTH])
        qkv = linear(h, w_in[:, S5_WIDTH:S5_WIDTH + 3 * GDN_WIDTH])
        gate = linear(h, w_in[:, S5_WIDTH + 3 * GDN_WIDTH:S5_WIDTH + 4 * GDN_WIDTH])
        ab = linear(h, w_in[:, S5_WIDTH + 4 * GDN_WIDTH:])
        a_out = s5_mixer(u, small['s5_a_re'], small['s5_a_im'], small['s5_log_dt'], small['s5_b_re'], small['s5_b_im'],
                         small['s5_c_re'], small['s5_c_im'], small['s5_d'], big['s5_w_glu'], small['s5_b_glu'])
        b_out = gated_deltanet(qkv, gate, ab, big['gdn_conv_w'], small['gdn_a_log'], small['gdn_dt_bias'],
                               small['gdn_out_norm_g'])
        w_out = big['ab_w_out']
        return linear_res(b_out, w_out[S5_WIDTH:], linear_res(a_out, w_out[:S5_WIDTH], x))
    h, x = rmsnorm_res(x, big['c_norm_g'].reshape(-1), "c_norm")
    return chunk_attention(x, h, big['c_w_qkv'], big['c_w_out'], small['c_q_norm_g'], small['c_k_norm_g'],
                           small['c_rel_bias'])


def _loss_rows(x, target):
    return jnp.sum(make_rowop(_loss_fn, "loss")((x, target), ())[0])


def kernel(x, mem, ab_norm_g, ab_w_in, ab_w_out, s5_a_re, s5_a_im, s5_log_dt, s5_b_re, s5_b_im, s5_c_re, s5_c_im, s5_d, s5_w_glu, s5_b_glu, gdn_conv_w, gdn_a_log, gdn_dt_bias, gdn_out_norm_g, c_norm_g, c_w_qkv, c_w_out, c_q_norm_g, c_k_norm_g, c_rel_bias, mem_norm_g, xa_norm_g, xa_w_q, xa_w_kv, xa_w_out, xa_q_norm_g, xa_k_norm_g, f_norm_g, f_w_gate, f_w_up, f_w_down, loss_target, m_ab_norm_g, m_ab_w_in, m_ab_w_out, m_s5_a_re, m_s5_a_im, m_s5_log_dt, m_s5_b_re, m_s5_b_im, m_s5_c_re, m_s5_c_im, m_s5_d, m_s5_w_glu, m_s5_b_glu, m_gdn_conv_w, m_gdn_a_log, m_gdn_dt_bias, m_gdn_out_norm_g, m_c_norm_g, m_c_w_qkv, m_c_w_out, m_c_q_norm_g, m_c_k_norm_g, m_c_rel_bias, m_mem_norm_g, m_xa_norm_g, m_xa_w_q, m_xa_w_kv, m_xa_w_out, m_xa_q_norm_g, m_xa_k_norm_g, m_f_norm_g, m_f_w_gate, m_f_w_up, m_f_w_down, v_ab_norm_g, v_ab_w_in, v_ab_w_out, v_s5_a_re, v_s5_a_im, v_s5_log_dt, v_s5_b_re, v_s5_b_im, v_s5_c_re, v_s5_c_im, v_s5_d, v_s5_w_glu, v_s5_b_glu, v_gdn_conv_w, v_gdn_a_log, v_gdn_dt_bias, v_gdn_out_norm_g, v_c_norm_g, v_c_w_qkv, v_c_w_out, v_c_q_norm_g, v_c_k_norm_g, v_c_rel_bias, v_mem_norm_g, v_xa_norm_g, v_xa_w_q, v_xa_w_kv, v_xa_w_out, v_xa_q_norm_g, v_xa_k_norm_g, v_f_norm_g, v_f_w_gate, v_f_w_up, v_f_w_down):
    given = dict(locals())
    no_after = jnp.zeros((SUBLANE, LANE), F32)

    def shard(n, idx):
        a = given[n][idx]
        return (a.reshape(1, -1) if a.ndim == 1 else a).astype(SHARDED[n][1])

    def gather_start(stage, after, carry):
        arrays = [shard(n, idx) for n, idx in _stage_params(stage)[0]]
        return _exchange_start(arrays, ["gather"] * len(arrays), after, "gather_start_%d" % stage, carry)

    act = x[0]
    mem_n, mem_pullback = jax.vjp(lambda m, g: rmsnorm(m, g, "mem_norm"), mem[0], mem_norm_g)
    in_flight = {}
    for stage in range(2):
        in_flight[stage] = gather_start(stage, no_after, act)
        act = in_flight[stage][3]
    pullbacks = []
    for stage in range(N_STAGES):
        started = in_flight.pop(stage)
        landed = _exchange_wait(started, ["gather"] * len(started[1]), act, "gather_wait_%d" % stage)
        if stage + 2 < N_STAGES:
            in_flight[stage + 2] = gather_start(stage + 2, landed[0], act)
            act = in_flight[stage + 2][3]
        small = {n: given[n][idx] for n, idx in _stage_params(stage)[1]}
        act, pullback = jax.vjp(functools.partial(_stage_forward, stage), landed, small, act, mem_n)
        pullbacks.append(pullback)
    loss_local, loss_pullback = jax.vjp(_loss_rows, act, loss_target[0])
    d_act = loss_pullback(jnp.ones((), F32))[0]

    d_mem_n = jnp.zeros_like(mem_n)
    g_small = {}
    received = [None] * N_STAGES
    started, after = None, no_after
    for stage in reversed(range(N_STAGES)):
        d_landed, d_small, d_act, d_mem = pullbacks[stage](d_act)
        if stage % 2 == 1:
            d_mem_n = d_mem_n + d_mem
        for n, idx in _stage_params(stage)[1]:
            g_small[(n, idx)] = d_small[n]
        if started is not None:
            received[stage + 1] = _exchange_wait(started, ["scatter"] * len(started[1]), d_act,
                                                 "scatter_wait_%d" % (stage + 1))
            after = received[stage + 1][0]
        started = _exchange_start(list(d_landed), ["scatter"] * len(d_landed), after, "scatter_start_%d" % stage,
                                  d_act)
        d_act = started[3]
    g_small[('mem_norm_g', None)] = mem_pullback(d_mem_n)[1]

    def small_grad(n):
        if n == 'mem_norm_g':
            return g_small[(n, None)]
        return jnp.stack([g_small[(n, i)] for i in range(given[n].shape[0])], axis=0)

    small_started = _exchange_start([_pack([small_grad(n) for n in REPLICATED])], ["gather"], d_act,
                                    "small_grads_start", no_after)
    received[0] = _exchange_wait(started, ["scatter"] * len(started[1]), small_started[3], "scatter_wait_0")

    results = {}
    for n in SHARDED:
        slots = {}
        for stage in range(N_STAGES):
            for (pn, idx), r in zip(_stage_params(stage)[0], received[stage]):
                if pn == n:
                    slots[idx] = r
        shape = given[n].shape
        to3d = lambda a: a.reshape(a.shape[0], -1, a.shape[-1])
        outs = _adam_call(to3d(given[n]), to3d(given['m_' + n]), to3d(given['v_' + n]),
                          [slots[i] for i in range(len(slots))], "adamw_" + n)
        results[n] = [o.reshape(shape) for o in outs]
    packed = _exchange_wait(small_started, ["gather"], results['f_w_down'][0], "small_grads_wait")[0]
    outs = _adam_call(_pack([given[n] for n in REPLICATED])[None], _pack([given['m_' + n] for n in REPLICATED])[None],
                      _pack([given['v_' + n] for n in REPLICATED])[None], [packed], "adamw_replicated")
    shapes = [given[n].shape for n in REPLICATED]
    for j, parts in enumerate(zip(*[_unpack(o[0], shapes) for o in outs])):
        results[REPLICATED[j]] = list(parts)

    loss = lax.psum(loss_local, ("x", "y", "c"))
    return (loss, d_act[None], *[results[n][0] for n in WEIGHT_NAMES], *[results[n][1] for n in WEIGHT_NAMES],
            *[results[n][2] for n in WEIGHT_NAMES], *[results[n][3] for n in WEIGHT_NAMES])
```

```python
import functools
import math

import jax
import jax.numpy as jnp
import numpy as np
from jax import lax
from jax.experimental import pallas as pl
from jax.experimental.pallas import tpu as pltpu

F32 = jnp.float32
BF16 = jnp.bfloat16
HI = lax.Precision.HIGHEST

N_DEV = 8
D_MODEL = 1024
SEQ = 2048
DEPTH = 4
CHUNK = 64
N_MEM = 256
RMS_EPS = 1e-6
S5_WIDTH = 512
S5_GROUP = 16
S5_GROUPS = 32
S5_STATE = 64
GDN_HEAD_DIM = 128
GDN_WIDTH = 512
GDN_HEADS = 4
GDN_CONV = 4
AB_IN = S5_WIDTH + 4 * GDN_WIDTH + 2 * GDN_HEADS
AB_IN_PAD = 2688
CA_HEADS = 16
CA_HEAD_DIM = 64
CA_LEFT = 8
CA_BAND = (CA_LEFT + 1) * CHUNK
CA_PAD = CA_LEFT * CHUNK
MAX_REL = 128
XA_HEADS = 4
XA_HEAD_DIM = 256
FFN = 2816
ADAM_LR, ADAM_B1, ADAM_B2, ADAM_EPS, ADAM_WD, ADAM_STEP = 0.001, 0.9, 0.999, 1e-08, 0.01, 10

VMEM_LIMIT = 48 * 1024 * 1024
LANE = 128
SUBLANE = 8


def _cparams(sem=None):
    return pltpu.CompilerParams(dimension_semantics=sem, vmem_limit_bytes=VMEM_LIMIT)


def _divisor_tile(n, target, unit=LANE):
    if n <= target:
        return n
    best = None
    for t in range(unit, target + 1, unit):
        if n % t == 0:
            best = t
    assert best is not None, (n, target)
    return best


def _matmul(a, b, *, ta=False, tb=False, out_dtype=F32, name="mm", res=None):
    if ta:
        k_dim, m_dim = a.shape
    else:
        m_dim, k_dim = a.shape
    if tb:
        n_dim, kb = b.shape
    else:
        kb, n_dim = b.shape
    assert kb == k_dim, (a.shape, b.shape, ta, tb)
    tm = _divisor_tile(m_dim, 1024)
    tn = _divisor_tile(n_dim, 512)
    tk = _divisor_tile(k_dim, 1408)
    nk = k_dim // tk
    dims = (((0 if ta else 1,), (1 if tb else 0,)), ((), ()))

    def body(a_ref, b_ref, *rest):
        res_ref = rest[0] if res is not None else None
        o_ref, acc_ref = rest[-2:]
        k = pl.program_id(2)

        @pl.when(k == 0)
        def _():
            acc_ref[...] = jnp.zeros_like(acc_ref)

        acc_ref[...] += lax.dot_general(a_ref[...].astype(BF16), b_ref[...].astype(BF16), dims,
                                        preferred_element_type=F32)

        @pl.when(k == nk - 1)
        def _():
            total = acc_ref[...] if res is None else acc_ref[...] + res_ref[...]
            o_ref[...] = total.astype(o_ref.dtype)

    a_spec = pl.BlockSpec((tk, tm), lambda i, j, k: (k, i)) if ta else pl.BlockSpec((tm, tk), lambda i, j, k: (i, k))
    b_spec = pl.BlockSpec((tn, tk), lambda i, j, k: (j, k)) if tb else pl.BlockSpec((tk, tn), lambda i, j, k: (k, j))
    o_spec = pl.BlockSpec((tm, tn), lambda i, j, k: (i, j))
    return pl.pallas_call(
        body,
        grid=(m_dim // tm, n_dim // tn, nk),
        in_specs=[a_spec, b_spec] + ([o_spec] if res is not None else []),
        out_specs=o_spec,
        out_shape=jax.ShapeDtypeStruct((m_dim, n_dim), out_dtype),
        scratch_shapes=[pltpu.VMEM((tm, tn), F32)],
        compiler_params=_cparams(("parallel", "parallel", "arbitrary")),
        name=name,
    )(*((a, b) if res is None else (a, b, res)))


@jax.custom_vjp
def linear(a, w):
    return _matmul(a, w, name="linear_fwd")


def _linear_fwd(a, w):
    return _matmul(a, w, name="linear_fwd"), (a, w)


def _linear_bwd(res, dy):
    a, w = res
    da = _matmul(dy, w, tb=True, name="linear_da")
    dw = _matmul(a, dy, ta=True, out_dtype=w.dtype, name="linear_dw")
    return da, dw


linear.defvjp(_linear_fwd, _linear_bwd)


@jax.custom_vjp
def linear_res(a, w, x):
    return _matmul(a, w, name="linear_res_fwd", res=x)


def _linear_res_fwd(a, w, x):
    return _matmul(a, w, name="linear_res_fwd", res=x), (a, w)


def _linear_res_bwd(res, dy):
    return _linear_bwd(res, dy) + (dy,)


linear_res.defvjp(_linear_res_fwd, _linear_res_bwd)


def _mm_call(name, a, b, out_struct, grid, a_spec, b_spec, o_spec, dims, lead, res=None, keep_a=None):
    nk = grid[-1]
    acc_shape = o_spec.block_shape[1:] if lead[2] else o_spec.block_shape
    a_shape = a_spec.block_shape[1:] if lead[0] else a_spec.block_shape
    assert keep_a is None or nk == 1

    def body(a_ref, b_ref, *rest):
        res_ref = rest[0] if res is not None else None
        o_ref, acc_ref = rest[-1 - (keep_a is not None) - 1], rest[-1 - (keep_a is not None)]
        k = pl.program_id(len(grid) - 1)

        @pl.when(k == 0)
        def _():
            acc_ref[...] = jnp.zeros_like(acc_ref)

        if keep_a is None:
            av = (a_ref[0] if lead[0] else a_ref[...]).astype(BF16)
        else:
            a16_ref = rest[-1]

            @pl.when(pl.program_id(keep_a) == 0)
            def _():
                a16_ref[...] = (a_ref[0] if lead[0] else a_ref[...]).astype(BF16)

            av = a16_ref[...]
        bv = b_ref[0] if lead[1] else b_ref[...]
        acc_ref[...] += lax.dot_general(av, bv.astype(BF16), dims, preferred_element_type=F32)

        @pl.when(k == nk - 1)
        def _():
            if lead[2]:
                o_ref[0] = acc_ref[...].astype(o_ref.dtype)
            elif res is not None:
                o_ref[...] = (acc_ref[...] + res_ref[...]).astype(o_ref.dtype)
            else:
                o_ref[...] = acc_ref[...].astype(o_ref.dtype)

    return pl.pallas_call(
        body, grid=grid, in_specs=[a_spec, b_spec] + ([o_spec] if res is not None else []), out_specs=o_spec,
        out_shape=out_struct,
        scratch_shapes=[pltpu.VMEM(tuple(acc_shape), F32)] + ([pltpu.VMEM(tuple(a_shape), BF16)] if keep_a is not None else []),
        compiler_params=_cparams(("parallel", "arbitrary", "arbitrary")), name=name,
    )(*((a, b) if res is None else (a, b, res)))


_NN = (((1,), (0,)), ((), ()))
_NT_DIMS = (((1,), (1,)), ((), ()))
_TN_DIMS = (((0,), (0,)), ((), ()))


def _cols_fwd(a, g, dm_out):
    m_dim, k_dim = a.shape
    _, _, c_dim = g.shape
    tm = _divisor_tile(m_dim, 2048 if dm_out else 1024)
    tk = _divisor_tile(k_dim, 1024)
    a_spec = pl.BlockSpec((tm, tk), lambda i, j, k: (i, k))
    b_spec = pl.BlockSpec((1, tk, c_dim), lambda i, j, k: (j, k, 0))
    if dm_out:
        o_spec = pl.BlockSpec((1, tm, c_dim), lambda i, j, k: (j, i, 0))
        out = jax.ShapeDtypeStruct((N_DEV, m_dim, c_dim), BF16)
    else:
        o_spec = pl.BlockSpec((tm, c_dim), lambda i, j, k: (i, j))
        out = jax.ShapeDtypeStruct((m_dim, N_DEV * c_dim), F32)
    return _mm_call("cols_fwd", a, g, out, (m_dim // tm, N_DEV, k_dim // tk), a_spec, b_spec, o_spec, _NN,
                    (False, True, dm_out), keep_a=1 if k_dim == tk else None)


def _cols_da(dy, g, dm_out):
    _, k_dim, c_dim = g.shape
    m_dim = dy.shape[1] if dm_out else dy.shape[0]
    tm = _divisor_tile(m_dim, 1024)
    tk = _divisor_tile(k_dim, 1024)
    if dm_out:
        a_spec = pl.BlockSpec((1, tm, c_dim), lambda i, kb, j: (j, i, 0))
    else:
        a_spec = pl.BlockSpec((tm, c_dim), lambda i, kb, j: (i, j))
    b_spec = pl.BlockSpec((1, tk, c_dim), lambda i, kb, j: (j, kb, 0))
    o_spec = pl.BlockSpec((tm, tk), lambda i, kb, j: (i, kb))
    return _mm_call("cols_da", dy, g, jax.ShapeDtypeStruct((m_dim, k_dim), F32), (m_dim // tm, k_dim // tk, N_DEV),
                    a_spec, b_spec, o_spec, _NT_DIMS, (dm_out, True, False))


def _cols_dg(a, dy, g, dm_out):
    _, k_dim, c_dim = g.shape
    m_dim = a.shape[0]
    tm = _divisor_tile(m_dim, 2048)
    tk = _divisor_tile(k_dim, 512)
    a_spec = pl.BlockSpec((tm, tk), lambda kb, j, m: (m, kb))
    if dm_out:
        b_spec = pl.BlockSpec((1, tm, c_dim), lambda kb, j, m: (j, m, 0))
    else:
        b_spec = pl.BlockSpec((tm, c_dim), lambda kb, j, m: (m, j))
    o_spec = pl.BlockSpec((1, tk, c_dim), lambda kb, j, m: (j, kb, 0))
    return _mm_call("cols_dg", a, dy, jax.ShapeDtypeStruct(g.shape, g.dtype), (k_dim // tk, N_DEV, m_dim // tm),
                    a_spec, b_spec, o_spec, _TN_DIMS, (False, dm_out, True), keep_a=1 if m_dim == tm else None)


def _make_linear_cols(dm_out):
    @jax.custom_vjp
    def op(a, g):
        return _cols_fwd(a, g, dm_out)

    def fwd(a, g):
        return _cols_fwd(a, g, dm_out), (a, g)

    def bwd(res, dy):
        a, g = res
        return _cols_da(dy, g, dm_out), _cols_dg(a, dy, g, dm_out)

    op.defvjp(fwd, bwd)
    return op


linear_cols = _make_linear_cols(False)
linear_cols_dm = _make_linear_cols(True)


def _silu_mul(g, u):
    return _silu(g) * u


def _ffn_down_fwd(g, u, w, x):
    _, m_dim, r_dim = g.shape
    n_dim = w.shape[2]
    tm = _divisor_tile(m_dim, 1024)
    tn = _divisor_tile(n_dim, 1024)
    nj = N_DEV

    def body(g_ref, u_ref, w_ref, x_ref, o_ref, acc_ref):
        j = pl.program_id(2)

        @pl.when(j == 0)
        def _():
            acc_ref[...] = jnp.zeros_like(acc_ref)

        acc_ref[...] += _bdot(_silu_mul(g_ref[0].astype(F32), u_ref[0].astype(F32)), w_ref[0])

        @pl.when(j == nj - 1)
        def _():
            o_ref[...] = acc_ref[...] + x_ref[...]

    h_spec = pl.BlockSpec((1, tm, r_dim), lambda i, n, j: (j, i, 0))
    o_spec = pl.BlockSpec((tm, tn), lambda i, n, j: (i, n))
    return pl.pallas_call(
        body, grid=(m_dim // tm, n_dim // tn, nj),
        in_specs=[h_spec, h_spec, pl.BlockSpec((1, r_dim, tn), lambda i, n, j: (j, 0, n)), o_spec], out_specs=o_spec,
        out_shape=jax.ShapeDtypeStruct((m_dim, n_dim), F32), scratch_shapes=[pltpu.VMEM((tm, tn), F32)],
        compiler_params=_cparams(("parallel", "parallel", "arbitrary")), name="ffn_down_fwd",
    )(g, u, w, x)


def _ffn_down_dh(dy, g, u, w):
    m_dim, n_dim = dy.shape
    r_dim = w.shape[1]
    tm = _divisor_tile(m_dim, 1024)
    tn = _divisor_tile(n_dim, 1024)
    nn = n_dim // tn

    def body(dy_ref, w_ref, g_ref, u_ref, dg_ref, du_ref, acc_ref):
        n = pl.program_id(2)

        @pl.when(n == 0)
        def _():
            acc_ref[...] = jnp.zeros_like(acc_ref)

        acc_ref[...] += _bdot(dy_ref[...], w_ref[0], _NT_DIMS)

        @pl.when(n == nn - 1)
        def _():
            _, pullback = jax.vjp(_silu_mul, g_ref[0].astype(F32), u_ref[0].astype(F32))
            dg, du = pullback(acc_ref[...])
            dg_ref[0] = dg.astype(dg_ref.dtype)
            du_ref[0] = du.astype(du_ref.dtype)

    h_spec = pl.BlockSpec((1, tm, r_dim), lambda i, j, n: (j, i, 0))
    return pl.pallas_call(
        body, grid=(m_dim // tm, N_DEV, nn),
        in_specs=[pl.BlockSpec((tm, tn), lambda i, j, n: (i, n)), pl.BlockSpec((1, r_dim, tn), lambda i, j, n: (j, 0, n)),
                  h_spec, h_spec],
        out_specs=[h_spec, h_spec], out_shape=[jax.ShapeDtypeStruct(g.shape, g.dtype)] * 2,
        scratch_shapes=[pltpu.VMEM((tm, r_dim), F32)],
        compiler_params=_cparams(("parallel", "arbitrary", "arbitrary")), name="ffn_down_dh",
    )(dy, w, g, u)


def _ffn_down_dw(g, u, dy, w):
    _, m_dim, r_dim = g.shape
    n_dim = dy.shape[1]
    tm = _divisor_tile(m_dim, 2048)
    tn = _divisor_tile(n_dim, 512)
    nm = m_dim // tm

    def body(g_ref, u_ref, dy_ref, o_ref, acc_ref):
        m = pl.program_id(2)

        @pl.when(m == 0)
        def _():
            acc_ref[...] = jnp.zeros_like(acc_ref)

        acc_ref[...] += _bdot(_silu_mul(g_ref[0].astype(F32), u_ref[0].astype(F32)), dy_ref[...], _TN_DIMS)

        @pl.when(m == nm - 1)
        def _():
            o_ref[0] = acc_ref[...].astype(o_ref.dtype)

    h_spec = pl.BlockSpec((1, tm, r_dim), lambda j, n, m: (j, m, 0))
    return pl.pallas_call(
        body, grid=(N_DEV, n_dim // tn, nm),
        in_specs=[h_spec, h_spec, pl.BlockSpec((tm, tn), lambda j, n, m: (m, n))],
        out_specs=pl.BlockSpec((1, r_dim, tn), lambda j, n, m: (j, 0, n)),
        out_shape=jax.ShapeDtypeStruct(w.shape, w.dtype), scratch_shapes=[pltpu.VMEM((r_dim, tn), F32)],
        compiler_params=_cparams(("parallel", "parallel", "arbitrary")), name="ffn_down_dw",
    )(g, u, dy)


@jax.custom_vjp
def ffn_down(g, u, w, x):
    return _ffn_down_fwd(g, u, w, x)


def _ffn_down_vjp_fwd(g, u, w, x):
    return _ffn_down_fwd(g, u, w, x), (g, u, w)


def _ffn_down_vjp_bwd(res, dy):
    g, u, w = res
    dg, du = _ffn_down_dh(dy, g, u, w)
    return dg, du, _ffn_down_dw(g, u, dy, w), dy


ffn_down.defvjp(_ffn_down_vjp_fwd, _ffn_down_vjp_bwd)


def _cols_to_natural_call(g, width):
    _, k_dim, c_dim = g.shape
    tk = _divisor_tile(k_dim, 256, SUBLANE)

    def body(g_ref, o_ref):
        for j in range(N_DEV):
            o_ref[:, j * c_dim:(j + 1) * c_dim] = g_ref[j]
        if width > N_DEV * c_dim:
            o_ref[:, N_DEV * c_dim:] = jnp.zeros((tk, width - N_DEV * c_dim), o_ref.dtype)

    return pl.pallas_call(
        body, grid=(k_dim // tk,), in_specs=[pl.BlockSpec((N_DEV, tk, c_dim), lambda i: (0, i, 0))],
        out_specs=pl.BlockSpec((tk, width), lambda i: (i, 0)), out_shape=jax.ShapeDtypeStruct((k_dim, width), g.dtype),
        compiler_params=_cparams(("parallel",)), name="cols_to_natural",
    )(g)


def _natural_to_cols_call(w, c_dim):
    k_dim, width = w.shape
    tk = _divisor_tile(k_dim, 256, SUBLANE)

    def body(w_ref, o_ref):
        for j in range(N_DEV):
            o_ref[j] = w_ref[:, j * c_dim:(j + 1) * c_dim]

    return pl.pallas_call(
        body, grid=(k_dim // tk,), in_specs=[pl.BlockSpec((tk, width), lambda i: (i, 0))],
        out_specs=pl.BlockSpec((N_DEV, tk, c_dim), lambda i: (0, i, 0)),
        out_shape=jax.ShapeDtypeStruct((N_DEV, k_dim, c_dim), w.dtype),
        compiler_params=_cparams(("parallel",)), name="natural_to_cols",
    )(w)


@functools.partial(jax.custom_vjp, nondiff_argnums=(1,))
def cols_to_natural(g, width):
    return _cols_to_natural_call(g, width)


def _cols_to_natural_fwd(g, width):
    return _cols_to_natural_call(g, width), g.shape[2]


def _cols_to_natural_bwd(width, c_dim, dw):
    return (_natural_to_cols_call(dw, c_dim),)


cols_to_natural.defvjp(_cols_to_natural_fwd, _cols_to_natural_bwd)


def make_rowop(fn, name, tm=256, passthrough=0):
    def specs(rows, params):
        row_specs = [pl.BlockSpec((tm, r.shape[1]), lambda i: (i, 0)) for r in rows]
        par_specs = [pl.BlockSpec(p.shape, lambda i: (0, 0)) for p in params]
        return row_specs, par_specs

    def out_structs(rows, params):
        tiles = [jax.ShapeDtypeStruct((tm, r.shape[1]), r.dtype) for r in rows]
        return jax.eval_shape(lambda r, p: fn(*r, *p), tiles, list(params))

    def fwd_call(rows, params):
        m_dim = rows[0].shape[0]
        n_in = len(rows) + len(params)
        outs = out_structs(rows, params)

        def body(*refs):
            res = fn(*[r[...] for r in refs[:n_in]])
            for o_ref, r in zip(refs[n_in:], res):
                o_ref[...] = r.astype(o_ref.dtype)

        row_specs, par_specs = specs(rows, params)
        return pl.pallas_call(
            body,
            grid=(m_dim // tm,),
            in_specs=row_specs + par_specs,
            out_specs=[pl.BlockSpec((tm, o.shape[1]), lambda i: (i, 0)) for o in outs],
            out_shape=[jax.ShapeDtypeStruct((m_dim, o.shape[1]), o.dtype) for o in outs],
            compiler_params=_cparams(("parallel",)),
            name=name + "_fwd",
        )(*rows, *params)

    def bwd_call(rows, params, cts):
        m_dim = rows[0].shape[0]
        n_rows, n_par = len(rows), len(params)
        n_in = n_rows + n_par
        n_ct = len(cts)
        n_fn = n_ct - passthrough

        def body(*refs):
            vals = [r[...] for r in refs[:n_in]]
            ct_vals = tuple(r[...] for r in refs[n_in:n_in + n_fn])
            pass_refs = refs[n_in + n_fn:n_in + n_ct]
            drow_refs = refs[n_in + n_ct:n_in + n_ct + n_rows]
            dpar_refs = refs[n_in + n_ct + n_rows:]
            _, pullback = jax.vjp(fn, *vals)
            grads = pullback(ct_vals)
            for i, (d_ref, g) in enumerate(zip(drow_refs, grads[:n_rows])):
                d_ref[...] = g + pass_refs[i][...] if i < passthrough else g

            @pl.when(pl.program_id(0) == 0)
            def _():
                for d_ref in dpar_refs:
                    d_ref[...] = jnp.zeros_like(d_ref)

            for d_ref, g in zip(dpar_refs, grads[n_rows:]):
                d_ref[...] += g

        row_specs, par_specs = specs(rows, params)
        ct_specs = [pl.BlockSpec((tm, c.shape[1]), lambda i: (i, 0)) for c in cts]
        res = pl.pallas_call(
            body,
            grid=(m_dim // tm,),
            in_specs=row_specs + par_specs + ct_specs,
            out_specs=row_specs + par_specs,
            out_shape=[jax.ShapeDtypeStruct(r.shape, r.dtype) for r in rows]
            + [jax.ShapeDtypeStruct(p.shape, p.dtype) for p in params],
            compiler_params=_cparams(("arbitrary",)),
            name=name + "_bwd",
        )(*rows, *params, *cts)
        return tuple(res[:n_rows]), tuple(res[n_rows:])

    @jax.custom_vjp
    def op(rows, params):
        return tuple(fwd_call(rows, params)) + tuple(rows[:passthrough])

    def op_fwd(rows, params):
        return tuple(fwd_call(rows, params)) + tuple(rows[:passthrough]), (rows, params)

    def op_bwd(res, cts):
        rows, params = res
        return bwd_call(rows, params, tuple(cts))

    op.defvjp(op_fwd, op_bwd)
    return op


def _rms(x, g):
    return x * lax.rsqrt(jnp.mean(x * x, axis=-1, keepdims=True) + RMS_EPS) * g


def _sigmoid(x):
    return 1.0 / (1.0 + jnp.exp(-x))


def _silu(x):
    return x * _sigmoid(x)


def _bdot(a, b, dims=(((1,), (0,)), ((), ()))):
    return lax.dot_general(a.astype(BF16), b.astype(BF16), dims, preferred_element_type=F32)


def _rmsnorm_fn(x, g):
    return (_rms(x, g),)


NORM_ROWS = 512


def rmsnorm(x, g, name):
    return make_rowop(_rmsnorm_fn, name, tm=min(NORM_ROWS, x.shape[0]))((x,), (g.reshape(1, -1),))[0]


def rmsnorm_res(x, g, name):
    return make_rowop(_rmsnorm_fn, name, tm=NORM_ROWS, passthrough=1)((x,), (g.reshape(1, -1),))


def _gelu_tanh(x):
    return 0.5 * x * (1.0 + jnp.tanh(0.7978845608028654 * (x + 0.044715 * x * x * x)))


def _softplus(x):
    return jnp.maximum(x, 0.0) + jnp.log(1.0 + jnp.exp(-jnp.abs(x)))


def _s5_post_fn(y, w_glu, b_glu):
    h = _gelu_tanh(y)
    return (h * _sigmoid(_bdot(h, w_glu) + b_glu),)


def _loss_fn(y, t):
    err = y - t
    return (0.5 * jnp.mean(err * err, axis=-1, keepdims=True),)


def _pair_headnorm(x, g2):
    lo = lax.broadcasted_iota(jnp.int32, x.shape, 1) < CA_HEAD_DIM
    sq = x * x
    s_lo = jnp.sum(jnp.where(lo, sq, 0.0), axis=-1, keepdims=True)
    s_hi = jnp.sum(jnp.where(lo, 0.0, sq), axis=-1, keepdims=True)
    ms = jnp.where(lo, s_lo, s_hi) * (1.0 / CA_HEAD_DIM)
    return x * lax.rsqrt(ms + RMS_EPS) * g2


def _ca_qknorm_fn(qkv, qg2, kg2):
    qs, ks = [], []
    for j in range(D_MODEL // LANE):
        qs.append(_pair_headnorm(qkv[:, j * LANE:(j + 1) * LANE], qg2))
        ks.append(_pair_headnorm(qkv[:, D_MODEL + j * LANE:D_MODEL + (j + 1) * LANE], kg2))
    return jnp.concatenate(qs, axis=1), jnp.concatenate(ks, axis=1)


def _xattn_fn(q, k, v, qg, kg):
    outs = []
    for h in range(XA_HEADS):
        sl = slice(h * XA_HEAD_DIM, (h + 1) * XA_HEAD_DIM)
        qh = _rms(q[:, sl], qg)
        kh = _rms(k[:, sl], kg)
        s = _bdot(qh, kh, (((1,), (1,)), ((), ()))) * (XA_HEAD_DIM ** -0.5)
        p = jnp.exp(s - jnp.max(s, axis=-1, keepdims=True))
        p = p / jnp.sum(p, axis=-1, keepdims=True)
        outs.append(_bdot(p, v[:, sl]))
    return (jnp.concatenate(outs, axis=1),)


def _gdn_prep_fn(x0, x1, x2, x3, ab, conv_w, alog, dtb):
    c = conv_w[3:4, :] * x0 + conv_w[2:3, :] * x1 + conv_w[1:2, :] * x2 + conv_w[0:1, :] * x3
    c = _silu(c)
    qs, ks = [], []
    for h in range(GDN_HEADS):
        qh = c[:, h * LANE:(h + 1) * LANE]
        kh = c[:, GDN_WIDTH + h * LANE:GDN_WIDTH + (h + 1) * LANE]
        qs.append(qh * lax.rsqrt(jnp.sum(qh * qh, axis=-1, keepdims=True) + RMS_EPS) * (GDN_HEAD_DIM ** -0.5))
        ks.append(kh * lax.rsqrt(jnp.sum(kh * kh, axis=-1, keepdims=True) + RMS_EPS))
    lane = lax.broadcasted_iota(jnp.int32, ab.shape, 1)
    g = -jnp.exp(alog) * _softplus(ab + dtb)
    beta = _sigmoid(ab)
    bg = jnp.where(lane < GDN_HEADS, g, jnp.where(lane < 2 * GDN_HEADS, beta, 0.0))
    return jnp.concatenate(qs, axis=1), jnp.concatenate(ks, axis=1), c[:, 2 * GDN_WIDTH:], bg


def _gdn_out_fn(o, gate, og):
    outs = []
    for h in range(GDN_HEADS):
        sl = slice(h * LANE, (h + 1) * LANE)
        outs.append(_rms(o[:, sl], og) * _silu(gate[:, sl]))
    return (jnp.concatenate(outs, axis=1),)


CA_QB = 4 * CHUNK
CA_KB = CA_QB + CA_PAD


def _ca_math(q2, kb2, vb2, bias2, c):
    lane = lax.broadcasted_iota(jnp.int32, q2.shape, 1)
    qc = lax.broadcasted_iota(jnp.int32, (CA_QB, CA_KB), 0) // CHUNK
    kc = lax.broadcasted_iota(jnp.int32, (CA_QB, CA_KB), 1) // CHUNK
    valid = (kc >= qc) & (kc <= qc + CA_LEFT) & (kc + c * (CA_QB // CHUNK) >= CA_LEFT)
    out = jnp.zeros(q2.shape, F32)
    for h in range(2):
        mine = (lane >= h * CA_HEAD_DIM) & (lane < (h + 1) * CA_HEAD_DIM)
        qh = jnp.where(mine, q2, 0.0)
        s = _bdot(qh, kb2, (((1,), (1,)), ((), ()))) * (CA_HEAD_DIM ** -0.5) + bias2[h]
        s = jnp.where(valid, s, -1e30)
        p = jnp.exp(s - jnp.max(s, axis=-1, keepdims=True))
        p = p / jnp.sum(p, axis=-1, keepdims=True)
        out = out + jnp.where(mine, _bdot(p, vb2), 0.0)
    return out


CA_VEC = CA_QB + CA_KB


def _ca_specs(seq):
    q_spec = pl.BlockSpec((CA_QB, LANE), lambda hp, c: (c, hp))
    kv_spec = pl.BlockSpec((seq + CA_PAD, LANE), lambda hp, c: (0, hp))
    b_spec = pl.BlockSpec((1, 2, CA_VEC), lambda hp, c: (hp, 0, 0))
    return (D_MODEL // LANE, seq // CA_QB), q_spec, kv_spec, b_spec


def _ca_bias_from_vector(vec_ref, bias_ref):
    for h in range(2):
        rows = jnp.broadcast_to(vec_ref[0, h:h + 1, :], (CA_QB, CA_VEC))
        bias_ref[h] = pltpu.roll(rows, 0, 1, stride=1, stride_axis=0)[:, CA_QB:]


def _ca_vector_grad(dbias):
    d = jnp.concatenate([jnp.zeros((CA_QB, CA_QB), F32), dbias], axis=1)
    row = lax.broadcasted_iota(jnp.int32, d.shape, 0)
    for bit in range(CA_QB.bit_length() - 1):
        d = jnp.where((row >> bit) & 1 == 1, pltpu.roll(d, CA_VEC - (1 << bit), 1), d)
    return jnp.sum(d, axis=0, keepdims=True)


def _ca_fwd_call(q, kpad, vpad, vec):
    grid, q_spec, kv_spec, b_spec = _ca_specs(q.shape[0])

    def body(q_ref, k_ref, v_ref, vec_ref, o_ref, bias_ref):
        c = pl.program_id(1)
        start = pl.multiple_of(c * CA_QB, CA_QB)

        @pl.when(c == 0)
        def _():
            _ca_bias_from_vector(vec_ref, bias_ref)

        o_ref[...] = _ca_math(q_ref[...], k_ref[pl.ds(start, CA_KB), :], v_ref[pl.ds(start, CA_KB), :],
                              bias_ref[...], c)

    return pl.pallas_call(
        body, grid=grid, in_specs=[q_spec, kv_spec, kv_spec, b_spec], out_specs=q_spec,
        out_shape=jax.ShapeDtypeStruct(q.shape, F32), scratch_shapes=[pltpu.VMEM((2, CA_QB, CA_KB), F32)],
        compiler_params=_cparams(("parallel", "arbitrary")), name="chunkattn_fwd",
    )(q, kpad, vpad, vec)


def _ca_bwd_call(q, kpad, vpad, vec, do):
    grid, q_spec, kv_spec, b_spec = _ca_specs(q.shape[0])
    last = grid[1] - 1

    def body(q_ref, k_ref, v_ref, vec_ref, do_ref, dq_ref, dk_ref, dv_ref, dvec_ref, bias_ref, dbias_ref):
        c = pl.program_id(1)
        start = pl.multiple_of(c * CA_QB, CA_QB)

        @pl.when(c == 0)
        def _():
            _ca_bias_from_vector(vec_ref, bias_ref)
            dk_ref[...] = jnp.zeros_like(dk_ref)
            dv_ref[...] = jnp.zeros_like(dv_ref)
            dbias_ref[...] = jnp.zeros_like(dbias_ref)

        _, pullback = jax.vjp(lambda a, b, d, e: _ca_math(a, b, d, e, c), q_ref[...],
                              k_ref[pl.ds(start, CA_KB), :], v_ref[pl.ds(start, CA_KB), :], bias_ref[...])
        dq, dkb, dvb, dbias = pullback(do_ref[...])
        dq_ref[...] = dq
        dk_ref[pl.ds(start, CA_KB), :] += dkb
        dv_ref[pl.ds(start, CA_KB), :] += dvb
        dbias_ref[...] += dbias

        @pl.when(c == last)
        def _():
            for h in range(2):
                dvec_ref[0, h:h + 1, :] = _ca_vector_grad(dbias_ref[h])

    return pl.pallas_call(
        body, grid=grid, in_specs=[q_spec, kv_spec, kv_spec, b_spec, q_spec],
        out_specs=[q_spec, kv_spec, kv_spec, b_spec],
        out_shape=[jax.ShapeDtypeStruct(q.shape, F32), jax.ShapeDtypeStruct(kpad.shape, F32),
                   jax.ShapeDtypeStruct(vpad.shape, F32), jax.ShapeDtypeStruct(vec.shape, F32)],
        scratch_shapes=[pltpu.VMEM((2, CA_QB, CA_KB), F32), pltpu.VMEM((2, CA_QB, CA_KB), F32)],
        compiler_params=_cparams(("parallel", "arbitrary")), name="chunkattn_bwd",
    )(q, kpad, vpad, vec, do)


@jax.custom_vjp
def chunk_attn_core(q, kpad, vpad, vec):
    return _ca_fwd_call(q, kpad, vpad, vec)


def _ca_core_fwd(q, kpad, vpad, vec):
    return _ca_fwd_call(q, kpad, vpad, vec), (q, kpad, vpad, vec)


def _ca_core_bwd(res, do):
    return tuple(_ca_bwd_call(*res, do))


chunk_attn_core.defvjp(_ca_core_fwd, _ca_core_bwd)


S5_GB = 4
S5_U = S5_WIDTH // S5_GB
S5_L = S5_GROUPS * S5_STATE // S5_GB


def _cmul(ar, ai, br, bi):
    return ar * br - ai * bi, ar * bi + ai * br


def _hdot(a, b, dims=(((1,), (0,)), ((), ()))):
    return lax.dot_general(a, b, dims, precision=HI, preferred_element_type=F32)


def _split_dot(a, b, dims=(((1,), (0,)), ((), ()))):
    a_hi, b_hi = a.astype(BF16), b.astype(BF16)
    a_lo = (a - a_hi.astype(F32)).astype(BF16)
    b_lo = (b - b_hi.astype(F32)).astype(BF16)
    dot = functools.partial(lax.dot_general, dimension_numbers=dims, preferred_element_type=F32)
    return dot(a_hi, b_hi) + (dot(a_hi, b_lo) + dot(a_lo, b_hi))


_NT = (((1,), (1,)), ((), ()))
_TN = (((0,), (0,)), ((), ()))


def _s5_tables(lr, li, reverse):
    p = {1: (lr, li)}
    p[2] = _cmul(*p[1], *p[1])
    p[4] = _cmul(*p[2], *p[2])
    p[3] = _cmul(*p[2], *p[1])
    p[5] = _cmul(*p[4], *p[1])
    p[6] = _cmul(*p[4], *p[2])
    p[7] = _cmul(*p[4], *p[3])
    p[8] = _cmul(*p[4], *p[4])
    row = lax.broadcasted_iota(jnp.int32, (SUBLANE, lr.shape[1]), 0)
    tr = jnp.zeros(row.shape, F32)
    ti = jnp.zeros(row.shape, F32)
    for i in range(SUBLANE):
        k = SUBLANE - i if reverse else i + 1
        tr = jnp.where(row == i, p[k][0], tr)
        ti = jnp.where(row == i, p[k][1], ti)
    return p, (tr, ti), row


def _s5_block_scan(xr, xi, p, tab, row, hr, hi, reverse):
    for k in (1, 2, 4):
        if reverse:
            sr = jnp.where(row < SUBLANE - k, pltpu.roll(xr, SUBLANE - k, 0), 0.0)
            si = jnp.where(row < SUBLANE - k, pltpu.roll(xi, SUBLANE - k, 0), 0.0)
        else:
            sr = jnp.where(row >= k, pltpu.roll(xr, k, 0), 0.0)
            si = jnp.where(row >= k, pltpu.roll(xi, k, 0), 0.0)
        ar, ai = _cmul(p[k][0], p[k][1], sr, si)
        xr, xi = xr + ar, xi + ai
    cr, ci = _cmul(tab[0], tab[1], hr, hi)
    return xr + cr, xi + ci


def _s5_forward_scan(sr_ref, si_ref, lr, li):
    n_blocks = sr_ref.shape[0] // SUBLANE
    p, tab, row = _s5_tables(lr, li, False)

    def step(b, carry):
        base = pl.multiple_of(b * SUBLANE, SUBLANE)
        xr, xi = _s5_block_scan(sr_ref[pl.ds(base, SUBLANE), :], si_ref[pl.ds(base, SUBLANE), :],
                                p, tab, row, carry[0], carry[1], False)
        sr_ref[pl.ds(base, SUBLANE), :] = xr
        si_ref[pl.ds(base, SUBLANE), :] = xi
        return xr[SUBLANE - 1:SUBLANE, :], xi[SUBLANE - 1:SUBLANE, :]

    zero = jnp.zeros((1, lr.shape[1]), F32)
    lax.fori_loop(0, n_blocks, step, (zero, zero))


def _s5_specs(seq):
    u_spec = pl.BlockSpec((seq, S5_U), lambda g: (0, g))
    bd_spec = pl.BlockSpec((1, S5_U, S5_L), lambda g: (g, 0, 0))
    cd_spec = pl.BlockSpec((1, S5_L, S5_U), lambda g: (g, 0, 0))
    lam_spec = pl.BlockSpec((1, 2, S5_L), lambda g: (g, 0, 0))
    d_spec = pl.BlockSpec((1, S5_U), lambda g: (0, g))
    return u_spec, bd_spec, cd_spec, lam_spec, d_spec


S5_ROWS = 256


def _row_chunks(seq, fn):
    rows_per = min(S5_ROWS, seq)

    def step(r, carry):
        fn(pl.ds(pl.multiple_of(r * rows_per, rows_per), rows_per))
        return carry

    lax.fori_loop(0, seq // rows_per, step, 0)


def _s5_fwd_call(u, bdr, bdi, cdr, cdi, lam, d):
    seq = u.shape[0]
    u_spec, bd_spec, cd_spec, lam_spec, d_spec = _s5_specs(seq)

    def body(u_ref, bdr_ref, bdi_ref, cdr_ref, cdi_ref, lam_ref, d_ref, y_ref, sr_ref, si_ref):
        def project_in(rows):
            uv = u_ref[rows, :]
            sr_ref[rows, :] = _bdot(uv, bdr_ref[0])
            si_ref[rows, :] = _bdot(uv, bdi_ref[0])

        def project_out(rows):
            y_ref[rows, :] = (_bdot(sr_ref[rows, :], cdr_ref[0]) - _bdot(si_ref[rows, :], cdi_ref[0])
                              + d_ref[...] * u_ref[rows, :])

        _row_chunks(seq, project_in)
        _s5_forward_scan(sr_ref, si_ref, lam_ref[0, 0:1, :], lam_ref[0, 1:2, :])
        _row_chunks(seq, project_out)

    return pl.pallas_call(
        body, grid=(S5_GB,), in_specs=[u_spec, bd_spec, bd_spec, cd_spec, cd_spec, lam_spec, d_spec],
        out_specs=u_spec, out_shape=jax.ShapeDtypeStruct(u.shape, F32),
        scratch_shapes=[pltpu.VMEM((seq, S5_L), F32), pltpu.VMEM((seq, S5_L), F32)],
        compiler_params=_cparams(("parallel",)), name="s5_fwd",
    )(u, bdr, bdi, cdr, cdi, lam, d)


def _s5_bwd_call(u, bdr, bdi, cdr, cdi, lam, d, dy):
    seq = u.shape[0]
    n_blocks = seq // SUBLANE
    u_spec, bd_spec, cd_spec, lam_spec, d_spec = _s5_specs(seq)

    def body(u_ref, bdr_ref, bdi_ref, cdr_ref, cdi_ref, lam_ref, d_ref, dy_ref,
             du_ref, dbdr_ref, dbdi_ref, dcdr_ref, dcdi_ref, dlam_ref, dd_ref, sr_ref, si_ref, gr_ref, gi_ref):
        lr, li = lam_ref[0, 0:1, :], lam_ref[0, 1:2, :]

        def project_in(rows):
            uv = u_ref[rows, :]
            dyv = dy_ref[rows, :]
            sr_ref[rows, :] = _bdot(uv, bdr_ref[0])
            si_ref[rows, :] = _bdot(uv, bdi_ref[0])
            gr_ref[rows, :] = _bdot(dyv, cdr_ref[0], _NT)
            gi_ref[rows, :] = -_bdot(dyv, cdi_ref[0], _NT)

        _row_chunks(seq, project_in)
        _s5_forward_scan(sr_ref, si_ref, lr, li)
        p, tab, row = _s5_tables(lr, -li, True)

        def step(i, carry):
            hr, hi, acc_r, acc_i = carry
            b = n_blocks - 1 - i
            base = pl.multiple_of(b * SUBLANE, SUBLANE)
            xr, xi = _s5_block_scan(gr_ref[pl.ds(base, SUBLANE), :], gi_ref[pl.ds(base, SUBLANE), :],
                                    p, tab, row, hr, hi, True)
            gr_ref[pl.ds(base, SUBLANE), :] = xr
            gi_ref[pl.ds(base, SUBLANE), :] = xi
            prev = pl.multiple_of(jnp.maximum(b - 1, 0) * SUBLANE, SUBLANE)
            keep = (b > 0).astype(F32)
            last_r = sr_ref[pl.ds(prev, SUBLANE), :][SUBLANE - 1:SUBLANE, :] * keep
            last_i = si_ref[pl.ds(prev, SUBLANE), :][SUBLANE - 1:SUBLANE, :] * keep
            pr = jnp.where(row >= 1, pltpu.roll(sr_ref[pl.ds(base, SUBLANE), :], 1, 0), last_r)
            pi = jnp.where(row >= 1, pltpu.roll(si_ref[pl.ds(base, SUBLANE), :], 1, 0), last_i)
            acc_r = acc_r + pr * xr + pi * xi
            acc_i = acc_i + pr * xi - pi * xr
            return xr[0:1, :], xi[0:1, :], acc_r, acc_i

        zero = jnp.zeros((1, S5_L), F32)
        zacc = jnp.zeros((SUBLANE, S5_L), F32)
        _, _, acc_r, acc_i = lax.fori_loop(0, n_blocks, step, (zero, zero, zacc, zacc))
        dlam_ref[0, 0:1, :] = jnp.sum(acc_r, axis=0, keepdims=True)
        dlam_ref[0, 1:2, :] = jnp.sum(acc_i, axis=0, keepdims=True)
        for ref in (dbdr_ref, dbdi_ref, dcdr_ref, dcdi_ref, dd_ref):
            ref[...] = jnp.zeros_like(ref)

        def grads(rows):
            uv, dyv = u_ref[rows, :], dy_ref[rows, :]
            grv, giv = gr_ref[rows, :], gi_ref[rows, :]
            du_ref[rows, :] = _bdot(grv, bdr_ref[0], _NT) + _bdot(giv, bdi_ref[0], _NT) + d_ref[...] * dyv
            dbdr_ref[0] += _bdot(uv, grv, _TN)
            dbdi_ref[0] += _bdot(uv, giv, _TN)
            dcdr_ref[0] += _bdot(sr_ref[rows, :], dyv, _TN)
            dcdi_ref[0] -= _bdot(si_ref[rows, :], dyv, _TN)
            dd_ref[...] += jnp.sum(dyv * uv, axis=0, keepdims=True)

        _row_chunks(seq, grads)

    scratch = [pltpu.VMEM((seq, S5_L), F32) for _ in range(4)]
    return pl.pallas_call(
        body, grid=(S5_GB,),
        in_specs=[u_spec, bd_spec, bd_spec, cd_spec, cd_spec, lam_spec, d_spec, u_spec],
        out_specs=[u_spec, bd_spec, bd_spec, cd_spec, cd_spec, lam_spec, d_spec],
        out_shape=[jax.ShapeDtypeStruct(a.shape, F32) for a in (u, bdr, bdi, cdr, cdi, lam, d)],
        scratch_shapes=scratch, compiler_params=_cparams(("parallel",)), name="s5_bwd",
    )(u, bdr, bdi, cdr, cdi, lam, d, dy)


@jax.custom_vjp
def s5_core(u, bdr, bdi, cdr, cdi, lam, d):
    return _s5_fwd_call(u, bdr, bdi, cdr, cdi, lam, d)


def _s5_core_fwd(*args):
    return _s5_fwd_call(*args), args


def _s5_core_bwd(res, dy):
    return tuple(_s5_bwd_call(*res, dy))


s5_core.defvjp(_s5_core_fwd, _s5_core_bwd)


def _s5_discretize(a_re, a_im, log_dt, b_re, b_im, c_re, c_im, d):
    dt = jnp.exp(log_dt)[:, None]
    mag = jnp.exp(a_re * dt)
    lbr, lbi = mag * jnp.cos(a_im * dt), mag * jnp.sin(a_im * dt)
    den = a_re * a_re + a_im * a_im
    fr = ((lbr - 1.0) * a_re + lbi * a_im) / den
    fi = (lbi * a_re - (lbr - 1.0) * a_im) / den
    bbr = fr[:, :, None] * b_re - fi[:, :, None] * b_im
    bbi = fr[:, :, None] * b_im + fi[:, :, None] * b_re
    eye = jnp.eye(S5_GROUPS // S5_GB, dtype=F32)
    gl = S5_GROUPS // S5_GB

    def bd(t):
        return jnp.einsum('bgpc,gh->bgchp', t.reshape(S5_GB, gl, S5_STATE, S5_GROUP), eye).reshape(S5_GB, S5_U, S5_L)

    def cd(t):
        return jnp.einsum('bgcp,gh->bgphc', t.reshape(S5_GB, gl, S5_GROUP, S5_STATE), eye).reshape(S5_GB, S5_L, S5_U)

    lam = jnp.stack([lbr.reshape(S5_GB, S5_L), lbi.reshape(S5_GB, S5_L)], axis=1)
    return bd(bbr), bd(bbi), cd(c_re), cd(c_im), lam, d.reshape(1, S5_WIDTH)


@jax.custom_vjp
def _unit_lower_solve(neg_a, rhs, tinv):
    return _split_dot(tinv, rhs)


def _unit_lower_solve_fwd(neg_a, rhs, tinv):
    x = _split_dot(tinv, rhs)
    return x, (x, tinv)


def _unit_lower_solve_bwd(res, dx):
    x, tinv = res
    g = _split_dot(tinv, dx, _TN)
    return _split_dot(g, x, _NT), g, jnp.zeros_like(tinv)


_unit_lower_solve.defvjp(_unit_lower_solve_fwd, _unit_lower_solve_bwd)


def _unit_lower_inverse(neg_a):
    r = lax.broadcasted_iota(jnp.int32, neg_a.shape, 0)
    c = lax.broadcasted_iota(jnp.int32, neg_a.shape, 1)
    p = (r == c).astype(F32) + neg_a
    npow = _split_dot(neg_a, neg_a)
    for _ in range(4):
        y = _split_dot(jnp.concatenate([p, npow], axis=0), npow)
        p = p + y[:CHUNK]
        npow = y[CHUNK:]
    return p + _split_dot(p, npow)


def _gdn_chunk(q, k, v, g_col, b_col, st, tinv=None):
    r = lax.broadcasted_iota(jnp.int32, (CHUNK, CHUNK), 0)
    c = lax.broadcasted_iota(jnp.int32, (CHUNK, CHUNK), 1)
    eye = (r == c).astype(F32)
    strict = r > c
    causal = r >= c
    g_row = jnp.sum(g_col * eye, axis=0, keepdims=True)
    gcum = jnp.sum(jnp.where(causal, g_row, 0.0), axis=1, keepdims=True)
    gcum_row = jnp.sum(gcum * eye, axis=0, keepdims=True)
    diff = gcum - gcum_row
    decay_strict = jnp.where(strict, jnp.exp(jnp.where(strict, diff, 0.0)), 0.0)
    decay_causal = jnp.where(causal, jnp.exp(jnp.where(causal, diff, 0.0)), 0.0)
    gamma = jnp.exp(gcum)
    g_last = jnp.sum(jnp.where(lax.broadcasted_iota(jnp.int32, (CHUNK, 1), 0) == CHUNK - 1, gcum, 0.0),
                     axis=0, keepdims=True)
    kk = _bdot(k, k, _NT)
    neg_a = -(b_col * kk * decay_strict)
    if tinv is None:
        tinv = _unit_lower_inverse(neg_a)
    x = _unit_lower_solve(neg_a, jnp.concatenate([b_col * v, (b_col * gamma) * k], axis=1), lax.stop_gradient(tinv))
    u_new, w_k = x[:, :GDN_HEAD_DIM], x[:, GDN_HEAD_DIM:]
    qk = _bdot(q, k, _NT) * decay_causal
    q_g = q * gamma
    k_tail = k * jnp.exp(g_last - gcum)
    w = u_new - _bdot(w_k, st)
    o = _bdot(q_g, st) + _bdot(qk, w)
    st_new = jnp.exp(g_last) * st + _bdot(k_tail, w, _TN)
    return o, st_new, tinv


def _gdn_cols(bgv, h):
    lane = lax.broadcasted_iota(jnp.int32, bgv.shape, 1)
    g_col = jnp.sum(jnp.where(lane == h, bgv, 0.0), axis=1, keepdims=True)
    b_col = jnp.sum(jnp.where(lane == GDN_HEADS + h, bgv, 0.0), axis=1, keepdims=True)
    return g_col, b_col


GDN_CPS = 4


def _gdn_fwd_call(q, k, v, bg):
    seq = q.shape[0]
    n_chunks = seq // CHUNK
    n_steps = n_chunks // GDN_CPS
    rows = GDN_CPS * CHUNK
    x_spec = pl.BlockSpec((rows, GDN_WIDTH), lambda n: (n, 0))
    bg_spec = pl.BlockSpec((rows, LANE), lambda n: (n, 0))
    st_spec = pl.BlockSpec((GDN_CPS, GDN_WIDTH, GDN_HEAD_DIM), lambda n: (n, 0, 0))
    ti_spec = pl.BlockSpec((GDN_CPS, GDN_HEADS * CHUNK, CHUNK), lambda n: (n, 0, 0))

    def body(q_ref, k_ref, v_ref, bg_ref, o_ref, st_out_ref, ti_out_ref, st_ref):
        @pl.when(pl.program_id(0) == 0)
        def _():
            st_ref[...] = jnp.zeros_like(st_ref)

        for h in range(GDN_HEADS):
            sl = slice(h * GDN_HEAD_DIM, (h + 1) * GDN_HEAD_DIM)
            st = st_ref[sl, :]
            for cc in range(GDN_CPS):
                rs = slice(cc * CHUNK, (cc + 1) * CHUNK)
                g_col, b_col = _gdn_cols(bg_ref[rs, :], h)
                st_out_ref[cc, sl, :] = st
                o, st, tinv = _gdn_chunk(q_ref[rs, sl], k_ref[rs, sl], v_ref[rs, sl], g_col, b_col, st)
                o_ref[rs, sl] = o
                ti_out_ref[cc, h * CHUNK:(h + 1) * CHUNK, :] = tinv
            st_ref[sl, :] = st

    return pl.pallas_call(
        body, grid=(n_steps,), in_specs=[x_spec, x_spec, x_spec, bg_spec], out_specs=[x_spec, st_spec, ti_spec],
        out_shape=[jax.ShapeDtypeStruct(q.shape, F32),
                   jax.ShapeDtypeStruct((n_chunks, GDN_WIDTH, GDN_HEAD_DIM), F32),
                   jax.ShapeDtypeStruct((n_chunks, GDN_HEADS * CHUNK, CHUNK), F32)],
        scratch_shapes=[pltpu.VMEM((GDN_WIDTH, GDN_HEAD_DIM), F32)],
        compiler_params=_cparams(("arbitrary",)), name="gdn_fwd",
    )(q, k, v, bg)


def _gdn_bwd_call(q, k, v, bg, states, tinvs, do):
    seq = q.shape[0]
    n_steps = seq // CHUNK // GDN_CPS
    rows = GDN_CPS * CHUNK
    x_spec = pl.BlockSpec((rows, GDN_WIDTH), lambda i: (n_steps - 1 - i, 0))
    bg_spec = pl.BlockSpec((rows, LANE), lambda i: (n_steps - 1 - i, 0))
    st_spec = pl.BlockSpec((GDN_CPS, GDN_WIDTH, GDN_HEAD_DIM), lambda i: (n_steps - 1 - i, 0, 0))
    ti_spec = pl.BlockSpec((GDN_CPS, GDN_HEADS * CHUNK, CHUNK), lambda i: (n_steps - 1 - i, 0, 0))

    def body(q_ref, k_ref, v_ref, bg_ref, st_in_ref, ti_ref, do_ref, dq_ref, dk_ref, dv_ref, dbg_ref, dst_ref):
        @pl.when(pl.program_id(0) == 0)
        def _():
            dst_ref[...] = jnp.zeros_like(dst_ref)

        lane = lax.broadcasted_iota(jnp.int32, (CHUNK, LANE), 1)
        dbg = [jnp.zeros((CHUNK, LANE), F32) for _ in range(GDN_CPS)]
        for h in range(GDN_HEADS):
            sl = slice(h * GDN_HEAD_DIM, (h + 1) * GDN_HEAD_DIM)
            dst = dst_ref[sl, :]
            for cc in reversed(range(GDN_CPS)):
                rs = slice(cc * CHUNK, (cc + 1) * CHUNK)
                g_col, b_col = _gdn_cols(bg_ref[rs, :], h)
                tinv = ti_ref[cc, h * CHUNK:(h + 1) * CHUNK, :]
                _, pullback = jax.vjp(lambda *a: _gdn_chunk(*a, tinv=tinv)[:2], q_ref[rs, sl], k_ref[rs, sl],
                                      v_ref[rs, sl], g_col, b_col, st_in_ref[cc, sl, :])
                dq, dk, dv, dg, db, dst = pullback((do_ref[rs, sl], dst))
                dq_ref[rs, sl] = dq
                dk_ref[rs, sl] = dk
                dv_ref[rs, sl] = dv
                dbg[cc] = dbg[cc] + jnp.where(lane == h, dg, 0.0) + jnp.where(lane == GDN_HEADS + h, db, 0.0)
            dst_ref[sl, :] = dst
        for cc in range(GDN_CPS):
            dbg_ref[cc * CHUNK:(cc + 1) * CHUNK, :] = dbg[cc]

    return pl.pallas_call(
        body, grid=(n_steps,), in_specs=[x_spec, x_spec, x_spec, bg_spec, st_spec, ti_spec, x_spec],
        out_specs=[x_spec, x_spec, x_spec, bg_spec],
        out_shape=[jax.ShapeDtypeStruct(q.shape, F32)] * 3 + [jax.ShapeDtypeStruct(bg.shape, F32)],
        scratch_shapes=[pltpu.VMEM((GDN_WIDTH, GDN_HEAD_DIM), F32)],
        compiler_params=_cparams(("arbitrary",)), name="gdn_bwd",
    )(q, k, v, bg, states, tinvs, do)


@jax.custom_vjp
def gdn_core(q, k, v, bg):
    return _gdn_fwd_call(q, k, v, bg)[0]


def _gdn_core_fwd(q, k, v, bg):
    o, states, tinvs = _gdn_fwd_call(q, k, v, bg)
    return o, (q, k, v, bg, states, tinvs)


def _gdn_core_bwd(res, do):
    return tuple(_gdn_bwd_call(*res, do))


gdn_core.defvjp(_gdn_core_fwd, _gdn_core_bwd)


def _row(v):
    return v.reshape(1, -1)


def _lane_pad(v):
    return jnp.pad(v, (0, LANE - v.shape[0])).reshape(1, LANE)


def _delay_rows(x, k):
    return jnp.pad(x, ((k, 0), (0, 0)))[:x.shape[0]]


def s5_mixer(u, a_re, a_im, log_dt, b_re, b_im, c_re, c_im, d, w_glu, b_glu):
    y = s5_core(u, *_s5_discretize(a_re, a_im, log_dt, b_re, b_im, c_re, c_im, d))
    return make_rowop(_s5_post_fn, "s5_post")((y,), (w_glu, _row(b_glu)))[0]


def gated_deltanet(qkv, gate, ab, conv_w, a_log, dt_bias, out_g):
    rows = (qkv, _delay_rows(qkv, 1), _delay_rows(qkv, 2), _delay_rows(qkv, 3), ab)
    q, k, v, bg = make_rowop(_gdn_prep_fn, "gdn_prep")(rows, (conv_w, _lane_pad(a_log), _lane_pad(dt_bias)))
    o = gdn_core(q, k, v, bg)
    return make_rowop(_gdn_out_fn, "gdn_out")((o, gate), (_row(out_g),))[0]


def chunk_attention(x, h, w_qkv, w_out, q_g, k_g, rel_bias):
    qkv = linear_cols(h, w_qkv)
    qn, kn = make_rowop(_ca_qknorm_fn, "ca_qknorm")((qkv,), (_row(jnp.tile(q_g, 2)), _row(jnp.tile(k_g, 2))))
    kpad = jnp.pad(kn, ((CA_PAD, 0), (0, 0)))
    vpad = jnp.pad(qkv[:, 2 * D_MODEL:], ((CA_PAD, 0), (0, 0)))
    o = chunk_attn_core(qn, kpad, vpad, _rel_bias_vector(rel_bias))
    return linear_res(o, w_out, x)


def memory_cross_attention(x, h, mem_n, w_q, w_kv, w_out, q_g, k_g):
    q = linear(h, w_q)
    kv = linear_cols(mem_n, w_kv)
    o = make_rowop(_xattn_fn, "xattn", tm=512)((q,), (kv[:, :D_MODEL], kv[:, D_MODEL:], _row(q_g), _row(k_g)))[0]
    return linear_res(o, w_out, x)


def swiglu(x, h, w_gate, w_up, w_down):
    return ffn_down(linear_cols_dm(h, w_gate), linear_cols_dm(h, w_up), w_down, x)


def _rel_bias_vector(rel_bias):
    heads = rel_bias.shape[0]
    n_far = CA_KB - 1 - MAX_REL
    n_neg = CA_VEC - 1 - n_far - (2 * MAX_REL + 1)
    vec = jnp.concatenate([jnp.zeros((heads, 1), F32),
                           jnp.broadcast_to(rel_bias[:, 2 * MAX_REL:], (heads, n_far)),
                           jnp.flip(rel_bias, axis=1),
                           jnp.broadcast_to(rel_bias[:, :1], (heads, n_neg))], axis=1)
    return vec.reshape(heads // 2, 2, CA_VEC)


_HBM_SPEC = pl.BlockSpec(memory_space=pltpu.HBM)
_SEM_SPEC = pl.BlockSpec(memory_space=pltpu.SEMAPHORE)
_SIDE_EFFECT = pltpu.SideEffectType.DATAFLOW_SIDE_EFFECTING


def _peer(x, y, c, k):
    return (x + (k >> 2)) % 2, (y + ((k >> 1) & 1)) % 2, (c + (k & 1)) % 2


def _exchange_start(arrays, modes, after, name, carry):
    n = len(arrays)
    n_sem = n * (N_DEV - 1)
    lands = [pltpu.with_memory_space_constraint(lax.empty((N_DEV,) + a.shape if m == "gather" else a.shape, a.dtype),
                                                pltpu.HBM) for a, m in zip(arrays, modes)]
    arrays = [pltpu.with_memory_space_constraint(a, pltpu.HBM) for a in arrays]

    def body(*refs):
        ins, zones = refs[:n], refs[n:2 * n]
        send_sems, recv_sems, own_sems = refs[2 * n + 2:2 * n + 5]
        x, y, c = lax.axis_index("x"), lax.axis_index("y"), lax.axis_index("c")
        me = 4 * x + 2 * y + c
        for i in range(n):
            pltpu.make_async_copy(ins[i] if modes[i] == "gather" else ins[i].at[me], zones[i].at[me],
                                  own_sems.at[i]).start()
        for k in range(1, N_DEV):
            px, py, pc = _peer(x, y, c, k)
            peer = 4 * px + 2 * py + pc
            for i in range(n):
                sem = i * (N_DEV - 1) + k - 1
                pltpu.make_async_remote_copy(src_ref=ins[i] if modes[i] == "gather" else ins[i].at[peer],
                                             dst_ref=zones[i].at[me], send_sem=send_sems.at[sem],
                                             recv_sem=recv_sems.at[sem], device_id=(px, py, pc),
                                             device_id_type=pl.DeviceIdType.MESH).start()

    carry = pltpu.with_memory_space_constraint(carry, pltpu.HBM)
    out_shape = ((pltpu.SemaphoreType.DMA((n_sem,)), pltpu.SemaphoreType.DMA((n_sem,)), pltpu.SemaphoreType.DMA((n,)))
                 + tuple(pltpu.HBM(a.shape, a.dtype) for a in arrays) + tuple(pltpu.HBM(z.shape, z.dtype) for z in lands)
                 + (pltpu.HBM(carry.shape, carry.dtype),))
    aliases = {i: 3 + i for i in range(2 * n)}
    aliases[2 * n + 1] = 3 + 2 * n
    res = pl.pallas_call(
        body, name=name, out_shape=out_shape,
        in_specs=[_HBM_SPEC] * (2 * n) + [pl.BlockSpec(memory_space=pl.ANY), _HBM_SPEC],
        out_specs=(_SEM_SPEC,) * 3 + (_HBM_SPEC,) * (2 * n + 1),
        input_output_aliases=aliases,
        compiler_params=pltpu.CompilerParams(has_side_effects=_SIDE_EFFECT),
    )(*arrays, *lands, after, carry)
    return tuple(res[:3]), list(res[3:3 + n]), list(res[3 + n:3 + 2 * n]), res[3 + 2 * n]


def _exchange_wait(started, modes, after, name):
    sems, sources, zones, _ = started
    n = len(sources)

    def body(*refs):
        ins, lands = refs[:n], refs[n:2 * n]
        send_ref, recv_ref, own_ref = refs[2 * n:2 * n + 3]
        x, y, c = lax.axis_index("x"), lax.axis_index("y"), lax.axis_index("c")
        me = 4 * x + 2 * y + c
        for i in range(n):
            pltpu.make_async_copy(ins[i] if modes[i] == "gather" else ins[i].at[me], lands[i].at[me],
                                  own_ref.at[i]).wait()
        for k in range(1, N_DEV):
            px, py, pc = _peer(x, y, c, k)
            peer = 4 * px + 2 * py + pc
            for i in range(n):
                sem = i * (N_DEV - 1) + k - 1
                cp = pltpu.make_async_remote_copy(src_ref=ins[i] if modes[i] == "gather" else ins[i].at[peer],
                                                  dst_ref=lands[i].at[peer], send_sem=send_ref.at[sem],
                                                  recv_sem=recv_ref.at[sem], device_id=(px, py, pc),
                                                  device_id_type=pl.DeviceIdType.MESH)
                cp.wait_send()
                cp.wait_recv()

    res = pl.pallas_call(
        body, name=name,
        out_shape=tuple(pltpu.HBM(a.shape, a.dtype) for a in sources) + tuple(pltpu.HBM(z.shape, z.dtype) for z in zones),
        in_specs=[_HBM_SPEC] * (2 * n) + [_SEM_SPEC] * 3 + [pl.BlockSpec(memory_space=pl.ANY)],
        out_specs=(_HBM_SPEC,) * (2 * n), input_output_aliases={i: i for i in range(2 * n)},
        compiler_params=pltpu.CompilerParams(has_side_effects=_SIDE_EFFECT),
    )(*sources, *zones, *sems, after)
    return list(res[n:])


ADAM_TILE = 64 * 1024


def _adam_call(w, m, v, slots, name):
    n_layers, rows, cols = w.shape
    tr = rows
    if n_layers * rows * cols > ADAM_TILE:
        fits = [t for t in range(SUBLANE, rows, SUBLANE) if rows % t == 0 and n_layers * t * cols <= ADAM_TILE]
        tr = max(fits) if fits else SUBLANE
    c1 = 1.0 - ADAM_B1 ** ADAM_STEP
    c2 = 1.0 - ADAM_B2 ** ADAM_STEP

    def body(*refs):
        w_ref, m_ref, v_ref = refs[:3]
        slot_refs = refs[3:3 + n_layers]
        grad_ref, delta_ref, nm_ref, nv_ref = refs[3 + n_layers:]
        for layer in range(n_layers):
            g = slot_refs[layer][0].astype(F32)
            for k in range(1, N_DEV):
                g = g + slot_refs[layer][k].astype(F32)
            m_new = ADAM_B1 * m_ref[layer] + (1.0 - ADAM_B1) * g
            v_new = ADAM_B2 * v_ref[layer] + (1.0 - ADAM_B2) * (g * g)
            m_hat = m_new / c1
            v_hat = v_new / c2
            grad_ref[layer] = g
            delta_ref[layer] = -ADAM_LR * (m_hat / (jnp.sqrt(v_hat) + ADAM_EPS) + ADAM_WD * w_ref[layer])
            nm_ref[layer] = m_new
            nv_ref[layer] = v_new

    spec = pl.BlockSpec((n_layers, tr, cols), lambda i: (0, i, 0))
    slot_spec = pl.BlockSpec((N_DEV, tr, cols), lambda i: (0, i, 0))
    return pl.pallas_call(
        body, grid=(rows // tr,), in_specs=[spec, spec, spec] + [slot_spec] * n_layers,
        out_specs=[spec] * 4, out_shape=[jax.ShapeDtypeStruct(w.shape, F32)] * 4,
        compiler_params=_cparams(("parallel",)), name=name,
    )(w, m, v, *slots)


WEIGHT_NAMES = ['ab_norm_g', 'ab_w_in', 'ab_w_out', 's5_a_re', 's5_a_im', 's5_log_dt', 's5_b_re', 's5_b_im', 's5_c_re',
                's5_c_im', 's5_d', 's5_w_glu', 's5_b_glu', 'gdn_conv_w', 'gdn_a_log', 'gdn_dt_bias', 'gdn_out_norm_g',
                'c_norm_g', 'c_w_qkv', 'c_w_out', 'c_q_norm_g', 'c_k_norm_g', 'c_rel_bias', 'mem_norm_g', 'xa_norm_g',
                'xa_w_q', 'xa_w_kv', 'xa_w_out', 'xa_q_norm_g', 'xa_k_norm_g', 'f_norm_g', 'f_w_gate', 'f_w_up',
                'f_w_down']

SHARDED = {
    'ab_w_in': ('col', BF16), 'ab_w_out': ('row', BF16), 's5_w_glu': ('row', BF16), 'gdn_conv_w': ('col', F32),
    'c_norm_g': ('col', F32), 'c_w_qkv': ('col', BF16), 'c_w_out': ('row', BF16), 'xa_w_q': ('row', BF16),
    'xa_w_kv': ('col', BF16), 'xa_w_out': ('row', BF16), 'f_w_gate': ('col', BF16), 'f_w_up': ('col', BF16),
    'f_w_down': ('row', BF16),
}
GATHERED_AS_IS = ('c_w_qkv', 'xa_w_kv', 'f_w_gate', 'f_w_up', 'f_w_down')
REPLICATED = [n for n in WEIGHT_NAMES if n not in SHARDED]
PACK_UNIT = SUBLANE * LANE


def _full_from_gathered(g, axis):
    if axis == "row":
        return g.reshape(g.shape[0] * g.shape[1], g.shape[2])
    return jnp.transpose(g, (1, 0, 2)).reshape(g.shape[1], g.shape[0] * g.shape[2])


def _pack(arrays):
    flat = []
    for a in arrays:
        size = a.size
        padded = -(-size // PACK_UNIT) * PACK_UNIT
        flat.append(jnp.pad(a.reshape(-1), (0, padded - size)).reshape(-1, LANE))
    return jnp.concatenate(flat, axis=0)


def _unpack(buf, shapes):
    out, row = [], 0
    for shape in shapes:
        size = math.prod(shape)
        rows = -(-size // PACK_UNIT) * SUBLANE
        out.append(buf[row:row + rows].reshape(-1)[:size].reshape(shape))
        row += rows
    return out


N_STAGES = 2 * DEPTH
EVEN_SHARDED = ['ab_w_in', 'ab_w_out', 's5_w_glu', 'gdn_conv_w']
ODD_SHARDED = ['c_norm_g', 'c_w_qkv', 'c_w_out']
ALL_SHARDED = ['xa_w_q', 'xa_w_kv', 'xa_w_out', 'f_w_gate', 'f_w_up', 'f_w_down']
EVEN_SMALL = ['ab_norm_g', 's5_a_re', 's5_a_im', 's5_log_dt', 's5_b_re', 's5_b_im', 's5_c_re', 's5_c_im', 's5_d',
              's5_b_glu', 'gdn_a_log', 'gdn_dt_bias', 'gdn_out_norm_g']
ODD_SMALL = ['c_q_norm_g', 'c_k_norm_g', 'c_rel_bias']
ALL_SMALL = ['xa_norm_g', 'xa_q_norm_g', 'xa_k_norm_g', 'f_norm_g']


def _stage_params(stage):
    layer, part = divmod(stage, 2)
    if part == 1:
        return [(n, layer) for n in ALL_SHARDED], [(n, layer) for n in ALL_SMALL]
    big, small = (EVEN_SHARDED, EVEN_SMALL) if layer % 2 == 0 else (ODD_SHARDED, ODD_SMALL)
    return [(n, layer // 2) for n in big], [(n, layer // 2) for n in small]


def _stage_forward(stage, landed, small, x, mem_n):
    layer, part = divmod(stage, 2)
    big = {}
    for (n, _), g in zip(_stage_params(stage)[0], landed):
        if n == 'ab_w_in':
            big[n] = cols_to_natural(g, AB_IN_PAD)
        elif n in GATHERED_AS_IS:
            big[n] = g
        elif n == 's5_w_glu':
            big[n] = _full_from_gathered(g, 'row').astype(F32)
        else:
            big[n] = _full_from_gathered(g, SHARDED[n][0])
    if part == 1:
        h, x = rmsnorm_res(x, small['xa_norm_g'], "xa_norm")
        x = memory_cross_attention(x, h, mem_n, big['xa_w_q'], big['xa_w_kv'], big['xa_w_out'],
                                   small['xa_q_norm_g'], small['xa_k_norm_g'])
        h, x = rmsnorm_res(x, small['f_norm_g'], "f_norm")
        return swiglu(x, h, big['f_w_gate'], big['f_w_up'], big['f_w_down'])
    if layer % 2 == 0:
        h, x = rmsnorm_res(x, small['ab_norm_g'], "ab_norm")
        w_in = big['ab_w_in']
        u = linear(h, w_in[:, :S5_WIDTH])
        qkv = linear(h, w_in[:, S5_WIDTH:S5_WIDTH + 3 * GDN_WIDTH])
        gate = linear(h, w_in[:, S5_WIDTH + 3 * GDN_WIDTH:S5_WIDTH + 4 * GDN_WIDTH])
        ab = linear(h, w_in[:, S5_WIDTH + 4 * GDN_WIDTH:])
        a_out = s5_mixer(u, small['s5_a_re'], small['s5_a_im'], small['s5_log_dt'], small['s5_b_re'], small['s5_b_im'],
                         small['s5_c_re'], small['s5_c_im'], small['s5_d'], big['s5_w_glu'], small['s5_b_glu'])
        b_out = gated_deltanet(qkv, gate, ab, big['gdn_conv_w'], small['gdn_a_log'], small['gdn_dt_bias'],
                               small['gdn_out_norm_g'])
        w_out = big['ab_w_out']
        return linear_res(b_out, w_out[S5_WIDTH:], linear_res(a_out, w_out[:S5_WIDTH], x))
    h, x = rmsnorm_res(x, big['c_norm_g'].reshape(-1), "c_norm")
    return chunk_attention(x, h, big['c_w_qkv'], big['c_w_out'], small['c_q_norm_g'], small['c_k_norm_g'],
                           small['c_rel_bias'])


def _loss_rows(x, target):
    return jnp.sum(make_rowop(_loss_fn, "loss", tm=NORM_ROWS)((x, target), ())[0])


def kernel(x, mem, ab_norm_g, ab_w_in, ab_w_out, s5_a_re, s5_a_im, s5_log_dt, s5_b_re, s5_b_im, s5_c_re, s5_c_im, s5_d, s5_w_glu, s5_b_glu, gdn_conv_w, gdn_a_log, gdn_dt_bias, gdn_out_norm_g, c_norm_g, c_w_qkv, c_w_out, c_q_norm_g, c_k_norm_g, c_rel_bias, mem_norm_g, xa_norm_g, xa_w_q, xa_w_kv, xa_w_out, xa_q_norm_g, xa_k_norm_g, f_norm_g, f_w_gate, f_w_up, f_w_down, loss_target, m_ab_norm_g, m_ab_w_in, m_ab_w_out, m_s5_a_re, m_s5_a_im, m_s5_log_dt, m_s5_b_re, m_s5_b_im, m_s5_c_re, m_s5_c_im, m_s5_d, m_s5_w_glu, m_s5_b_glu, m_gdn_conv_w, m_gdn_a_log, m_gdn_dt_bias, m_gdn_out_norm_g, m_c_norm_g, m_c_w_qkv, m_c_w_out, m_c_q_norm_g, m_c_k_norm_g, m_c_rel_bias, m_mem_norm_g, m_xa_norm_g, m_xa_w_q, m_xa_w_kv, m_xa_w_out, m_xa_q_norm_g, m_xa_k_norm_g, m_f_norm_g, m_f_w_gate, m_f_w_up, m_f_w_down, v_ab_norm_g, v_ab_w_in, v_ab_w_out, v_s5_a_re, v_s5_a_im, v_s5_log_dt, v_s5_b_re, v_s5_b_im, v_s5_c_re, v_s5_c_im, v_s5_d, v_s5_w_glu, v_s5_b_glu, v_gdn_conv_w, v_gdn_a_log, v_gdn_dt_bias, v_gdn_out_norm_g, v_c_norm_g, v_c_w_qkv, v_c_w_out, v_c_q_norm_g, v_c_k_norm_g, v_c_rel_bias, v_mem_norm_g, v_xa_norm_g, v_xa_w_q, v_xa_w_kv, v_xa_w_out, v_xa_q_norm_g, v_xa_k_norm_g, v_f_norm_g, v_f_w_gate, v_f_w_up, v_f_w_down):
    given = dict(locals())
    no_after = jnp.zeros((SUBLANE, LANE), F32)

    def shard(n, idx):
        a = given[n][idx]
        return (a.reshape(1, -1) if a.ndim == 1 else a).astype(SHARDED[n][1])

    def gather_start(stage, after, carry):
        arrays = [shard(n, idx) for n, idx in _stage_params(stage)[0]]
        return _exchange_start(arrays, ["gather"] * len(arrays), after, "gather_start_%d" % stage, carry)

    act = x[0]
    mem_n, mem_pullback = jax.vjp(lambda m, g: rmsnorm(m, g, "mem_norm"), mem[0], mem_norm_g)
    in_flight = {}
    for stage in range(2):
        in_flight[stage] = gather_start(stage, no_after, act)
        act = in_flight[stage][3]
    pullbacks = []
    for stage in range(N_STAGES):
        started = in_flight.pop(stage)
        landed = _exchange_wait(started, ["gather"] * len(started[1]), act, "gather_wait_%d" % stage)
        if stage + 2 < N_STAGES:
            in_flight[stage + 2] = gather_start(stage + 2, landed[0], act)
            act = in_flight[stage + 2][3]
        small = {n: given[n][idx] for n, idx in _stage_params(stage)[1]}
        act, pullback = jax.vjp(functools.partial(_stage_forward, stage), landed, small, act, mem_n)
        pullbacks.append(pullback)
    loss_local, loss_pullback = jax.vjp(_loss_rows, act, loss_target[0])
    d_act = loss_pullback(jnp.ones((), F32))[0]

    d_mem_n = jnp.zeros_like(mem_n)
    g_small = {}
    received = [None] * N_STAGES
    started, after = None, no_after
    for stage in reversed(range(N_STAGES)):
        d_landed, d_small, d_act, d_mem = pullbacks[stage](d_act)
        if stage % 2 == 1:
            d_mem_n = d_mem_n + d_mem
        for n, idx in _stage_params(stage)[1]:
            g_small[(n, idx)] = d_small[n]
        if started is not None:
            received[stage + 1] = _exchange_wait(started, ["scatter"] * len(started[1]), d_act,
                                                 "scatter_wait_%d" % (stage + 1))
            after = received[stage + 1][0]
        started = _exchange_start(list(d_landed), ["scatter"] * len(d_landed), after, "scatter_start_%d" % stage,
                                  d_act)
        d_act = started[3]
    g_small[('mem_norm_g', None)] = mem_pullback(d_mem_n)[1]

    def small_grad(n):
        if n == 'mem_norm_g':
            return g_small[(n, None)]
        return jnp.stack([g_small[(n, i)] for i in range(given[n].shape[0])], axis=0)

    small_started = _exchange_start([_pack([small_grad(n) for n in REPLICATED])], ["gather"], d_act,
                                    "small_grads_start", no_after)
    received[0] = _exchange_wait(started, ["scatter"] * len(started[1]), small_started[3], "scatter_wait_0")

    results = {}
    for n in SHARDED:
        slots = {}
        for stage in range(N_STAGES):
            for (pn, idx), r in zip(_stage_params(stage)[0], received[stage]):
                if pn == n:
                    slots[idx] = r
        shape = given[n].shape
        to3d = lambda a: a.reshape(a.shape[0], -1, a.shape[-1])
        outs = _adam_call(to3d(given[n]), to3d(given['m_' + n]), to3d(given['v_' + n]),
                          [slots[i] for i in range(len(slots))], "adamw_" + n)
        results[n] = [o.reshape(shape) for o in outs]
    packed = _exchange_wait(small_started, ["gather"], results['f_w_down'][0], "small_grads_wait")[0]
    outs = _adam_call(_pack([given[n] for n in REPLICATED])[None], _pack([given['m_' + n] for n in REPLICATED])[None],
                      _pack([given['v_' + n] for n in REPLICATED])[None], [packed], "adamw_replicated")
    shapes = [given[n].shape for n in REPLICATED]
    for j, parts in enumerate(zip(*[_unpack(o[0], shapes) for o in outs])):
        results[REPLICATED[j]] = list(parts)

    loss = lax.psum(loss_local, ("x", "y", "c"))
    return (loss, d_act[None], *[results[n][0] for n in WEIGHT_NAMES], *[results[n][1] for n in WEIGHT_NAMES],
            *[results[n][2] for n in WEIGHT_NAMES], *[results[n][3] for n in WEIGHT_NAMES])
```

```python
import functools
import math

import jax
import jax.numpy as jnp
import numpy as np
from jax import lax
from jax.experimental import pallas as pl
from jax.experimental.pallas import tpu as pltpu

F32 = jnp.float32
BF16 = jnp.bfloat16
HI = lax.Precision.HIGHEST

N_DEV = 8
D_MODEL = 1024
SEQ = 2048
DEPTH = 4
CHUNK = 64
N_MEM = 256
RMS_EPS = 1e-6
S5_WIDTH = 512
S5_GROUP = 16
S5_GROUPS = 32
S5_STATE = 64
GDN_HEAD_DIM = 128
GDN_WIDTH = 512
GDN_HEADS = 4
GDN_CONV = 4
AB_IN = S5_WIDTH + 4 * GDN_WIDTH + 2 * GDN_HEADS
AB_IN_PAD = 2688
CA_HEADS = 16
CA_HEAD_DIM = 64
CA_LEFT = 8
CA_BAND = (CA_LEFT + 1) * CHUNK
CA_PAD = CA_LEFT * CHUNK
MAX_REL = 128
XA_HEADS = 4
XA_HEAD_DIM = 256
FFN = 2816
ADAM_LR, ADAM_B1, ADAM_B2, ADAM_EPS, ADAM_WD, ADAM_STEP = 0.001, 0.9, 0.999, 1e-08, 0.01, 10

VMEM_LIMIT = 48 * 1024 * 1024
LANE = 128
SUBLANE = 8


def _cparams(sem=None):
    return pltpu.CompilerParams(dimension_semantics=sem, vmem_limit_bytes=VMEM_LIMIT)


def _divisor_tile(n, target, unit=LANE):
    if n <= target:
        return n
    best = None
    for t in range(unit, target + 1, unit):
        if n % t == 0:
            best = t
    assert best is not None, (n, target)
    return best


def _matmul(a, b, *, ta=False, tb=False, out_dtype=F32, name="mm", res=None):
    if ta:
        k_dim, m_dim = a.shape
    else:
        m_dim, k_dim = a.shape
    if tb:
        n_dim, kb = b.shape
    else:
        kb, n_dim = b.shape
    assert kb == k_dim, (a.shape, b.shape, ta, tb)
    tm = _divisor_tile(m_dim, 1024)
    tn = _divisor_tile(n_dim, 512)
    tk = _divisor_tile(k_dim, 1408)
    nk = k_dim // tk
    dims = (((0 if ta else 1,), (1 if tb else 0,)), ((), ()))

    def body(a_ref, b_ref, *rest):
        res_ref = rest[0] if res is not None else None
        o_ref, acc_ref = rest[-2:]
        k = pl.program_id(2)

        @pl.when(k == 0)
        def _():
            acc_ref[...] = jnp.zeros_like(acc_ref)

        acc_ref[...] += lax.dot_general(a_ref[...].astype(BF16), b_ref[...].astype(BF16), dims,
                                        preferred_element_type=F32)

        @pl.when(k == nk - 1)
        def _():
            total = acc_ref[...] if res is None else acc_ref[...] + res_ref[...]
            o_ref[...] = total.astype(o_ref.dtype)

    a_spec = pl.BlockSpec((tk, tm), lambda i, j, k: (k, i)) if ta else pl.BlockSpec((tm, tk), lambda i, j, k: (i, k))
    b_spec = pl.BlockSpec((tn, tk), lambda i, j, k: (j, k)) if tb else pl.BlockSpec((tk, tn), lambda i, j, k: (k, j))
    o_spec = pl.BlockSpec((tm, tn), lambda i, j, k: (i, j))
    return pl.pallas_call(
        body,
        grid=(m_dim // tm, n_dim // tn, nk),
        in_specs=[a_spec, b_spec] + ([o_spec] if res is not None else []),
        out_specs=o_spec,
        out_shape=jax.ShapeDtypeStruct((m_dim, n_dim), out_dtype),
        scratch_shapes=[pltpu.VMEM((tm, tn), F32)],
        compiler_params=_cparams(("parallel", "parallel", "arbitrary")),
        name=name,
    )(*((a, b) if res is None else (a, b, res)))


@jax.custom_vjp
def linear(a, w):
    return _matmul(a, w, name="linear_fwd")


def _linear_fwd(a, w):
    return _matmul(a, w, name="linear_fwd"), (a, w)


def _linear_bwd(res, dy):
    a, w = res
    da = _matmul(dy, w, tb=True, name="linear_da")
    dw = _matmul(a, dy, ta=True, out_dtype=w.dtype, name="linear_dw")
    return da, dw


linear.defvjp(_linear_fwd, _linear_bwd)


@jax.custom_vjp
def linear_res(a, w, x):
    return _matmul(a, w, name="linear_res_fwd", res=x)


def _linear_res_fwd(a, w, x):
    return _matmul(a, w, name="linear_res_fwd", res=x), (a, w)


def _linear_res_bwd(res, dy):
    return _linear_bwd(res, dy) + (dy,)


linear_res.defvjp(_linear_res_fwd, _linear_res_bwd)


def _mm_call(name, a, b, out_struct, grid, a_spec, b_spec, o_spec, dims, lead, res=None, keep_a=None):
    nk = grid[-1]
    acc_shape = o_spec.block_shape[1:] if lead[2] else o_spec.block_shape
    a_shape = a_spec.block_shape[1:] if lead[0] else a_spec.block_shape
    assert keep_a is None or nk == 1

    def body(a_ref, b_ref, *rest):
        res_ref = rest[0] if res is not None else None
        o_ref, acc_ref = rest[-1 - (keep_a is not None) - 1], rest[-1 - (keep_a is not None)]
        k = pl.program_id(len(grid) - 1)

        @pl.when(k == 0)
        def _():
            acc_ref[...] = jnp.zeros_like(acc_ref)

        if keep_a is None:
            av = (a_ref[0] if lead[0] else a_ref[...]).astype(BF16)
        else:
            a16_ref = rest[-1]

            @pl.when(pl.program_id(keep_a) == 0)
            def _():
                a16_ref[...] = (a_ref[0] if lead[0] else a_ref[...]).astype(BF16)

            av = a16_ref[...]
        bv = b_ref[0] if lead[1] else b_ref[...]
        acc_ref[...] += lax.dot_general(av, bv.astype(BF16), dims, preferred_element_type=F32)

        @pl.when(k == nk - 1)
        def _():
            if lead[2]:
                o_ref[0] = acc_ref[...].astype(o_ref.dtype)
            elif res is not None:
                o_ref[...] = (acc_ref[...] + res_ref[...]).astype(o_ref.dtype)
            else:
                o_ref[...] = acc_ref[...].astype(o_ref.dtype)

    return pl.pallas_call(
        body, grid=grid, in_specs=[a_spec, b_spec] + ([o_spec] if res is not None else []), out_specs=o_spec,
        out_shape=out_struct,
        scratch_shapes=[pltpu.VMEM(tuple(acc_shape), F32)] + ([pltpu.VMEM(tuple(a_shape), BF16)] if keep_a is not None else []),
        compiler_params=_cparams(("parallel", "arbitrary", "arbitrary")), name=name,
    )(*((a, b) if res is None else (a, b, res)))


_NN = (((1,), (0,)), ((), ()))
_NT_DIMS = (((1,), (1,)), ((), ()))
_TN_DIMS = (((0,), (0,)), ((), ()))


def _cols_fwd(a, g, dm_out):
    m_dim, k_dim = a.shape
    _, _, c_dim = g.shape
    tm = _divisor_tile(m_dim, 2048 if dm_out else 1024)
    tk = _divisor_tile(k_dim, 1024)
    a_spec = pl.BlockSpec((tm, tk), lambda i, j, k: (i, k))
    b_spec = pl.BlockSpec((1, tk, c_dim), lambda i, j, k: (j, k, 0))
    if dm_out:
        o_spec = pl.BlockSpec((1, tm, c_dim), lambda i, j, k: (j, i, 0))
        out = jax.ShapeDtypeStruct((N_DEV, m_dim, c_dim), BF16)
    else:
        o_spec = pl.BlockSpec((tm, c_dim), lambda i, j, k: (i, j))
        out = jax.ShapeDtypeStruct((m_dim, N_DEV * c_dim), F32)
    return _mm_call("cols_fwd", a, g, out, (m_dim // tm, N_DEV, k_dim // tk), a_spec, b_spec, o_spec, _NN,
                    (False, True, dm_out), keep_a=1 if k_dim == tk else None)


def _cols_da(dy, g, dm_out):
    _, k_dim, c_dim = g.shape
    m_dim = dy.shape[1] if dm_out else dy.shape[0]
    tm = _divisor_tile(m_dim, 2048 if dm_out else 1024)
    tk = _divisor_tile(k_dim, 1024)
    if dm_out:
        a_spec = pl.BlockSpec((1, tm, c_dim), lambda i, kb, j: (j, i, 0))
    else:
        a_spec = pl.BlockSpec((tm, c_dim), lambda i, kb, j: (i, j))
    b_spec = pl.BlockSpec((1, tk, c_dim), lambda i, kb, j: (j, kb, 0))
    o_spec = pl.BlockSpec((tm, tk), lambda i, kb, j: (i, kb))
    return _mm_call("cols_da", dy, g, jax.ShapeDtypeStruct((m_dim, k_dim), F32), (m_dim // tm, k_dim // tk, N_DEV),
                    a_spec, b_spec, o_spec, _NT_DIMS, (dm_out, True, False))


def _cols_dg(a, dy, g, dm_out):
    _, k_dim, c_dim = g.shape
    m_dim = a.shape[0]
    tm = _divisor_tile(m_dim, 2048)
    tk = _divisor_tile(k_dim, 512)
    a_spec = pl.BlockSpec((tm, tk), lambda kb, j, m: (m, kb))
    if dm_out:
        b_spec = pl.BlockSpec((1, tm, c_dim), lambda kb, j, m: (j, m, 0))
    else:
        b_spec = pl.BlockSpec((tm, c_dim), lambda kb, j, m: (m, j))
    o_spec = pl.BlockSpec((1, tk, c_dim), lambda kb, j, m: (j, kb, 0))
    return _mm_call("cols_dg", a, dy, jax.ShapeDtypeStruct(g.shape, g.dtype), (k_dim // tk, N_DEV, m_dim // tm),
                    a_spec, b_spec, o_spec, _TN_DIMS, (False, dm_out, True), keep_a=1 if m_dim == tm else None)


def _make_linear_cols(dm_out):
    @jax.custom_vjp
    def op(a, g):
        return _cols_fwd(a, g, dm_out)

    def fwd(a, g):
        return _cols_fwd(a, g, dm_out), (a, g)

    def bwd(res, dy):
        a, g = res
        return _cols_da(dy, g, dm_out), _cols_dg(a, dy, g, dm_out)

    op.defvjp(fwd, bwd)
    return op


linear_cols = _make_linear_cols(False)
linear_cols_dm = _make_linear_cols(True)


def _silu_mul(g, u):
    return _silu(g) * u


def _ffn_down_fwd(g, u, w, x):
    _, m_dim, r_dim = g.shape
    n_dim = w.shape[2]
    tm = _divisor_tile(m_dim, 1024)
    tn = _divisor_tile(n_dim, 1024)
    nj = N_DEV

    def body(g_ref, u_ref, w_ref, x_ref, o_ref, acc_ref):
        j = pl.program_id(2)

        @pl.when(j == 0)
        def _():
            acc_ref[...] = jnp.zeros_like(acc_ref)

        acc_ref[...] += _bdot(_silu_mul(g_ref[0].astype(F32), u_ref[0].astype(F32)), w_ref[0])

        @pl.when(j == nj - 1)
        def _():
            o_ref[...] = acc_ref[...] + x_ref[...]

    h_spec = pl.BlockSpec((1, tm, r_dim), lambda i, n, j: (j, i, 0))
    o_spec = pl.BlockSpec((tm, tn), lambda i, n, j: (i, n))
    return pl.pallas_call(
        body, grid=(m_dim // tm, n_dim // tn, nj),
        in_specs=[h_spec, h_spec, pl.BlockSpec((1, r_dim, tn), lambda i, n, j: (j, 0, n)), o_spec], out_specs=o_spec,
        out_shape=jax.ShapeDtypeStruct((m_dim, n_dim), F32), scratch_shapes=[pltpu.VMEM((tm, tn), F32)],
        compiler_params=_cparams(("parallel", "parallel", "arbitrary")), name="ffn_down_fwd",
    )(g, u, w, x)


def _ffn_down_dh(dy, g, u, w):
    m_dim, n_dim = dy.shape
    r_dim = w.shape[1]
    tm = _divisor_tile(m_dim, 2048)
    tn = _divisor_tile(n_dim, 1024)
    nn = n_dim // tn

    def body(dy_ref, w_ref, g_ref, u_ref, dg_ref, du_ref, acc_ref):
        n = pl.program_id(2)

        @pl.when(n == 0)
        def _():
            acc_ref[...] = jnp.zeros_like(acc_ref)

        acc_ref[...] += _bdot(dy_ref[...], w_ref[0], _NT_DIMS)

        @pl.when(n == nn - 1)
        def _():
            _, pullback = jax.vjp(_silu_mul, g_ref[0].astype(F32), u_ref[0].astype(F32))
            dg, du = pullback(acc_ref[...])
            dg_ref[0] = dg.astype(dg_ref.dtype)
            du_ref[0] = du.astype(du_ref.dtype)

    h_spec = pl.BlockSpec((1, tm, r_dim), lambda i, j, n: (j, i, 0))
    return pl.pallas_call(
        body, grid=(m_dim // tm, N_DEV, nn),
        in_specs=[pl.BlockSpec((tm, tn), lambda i, j, n: (i, n)), pl.BlockSpec((1, r_dim, tn), lambda i, j, n: (j, 0, n)),
                  h_spec, h_spec],
        out_specs=[h_spec, h_spec], out_shape=[jax.ShapeDtypeStruct(g.shape, g.dtype)] * 2,
        scratch_shapes=[pltpu.VMEM((tm, r_dim), F32)],
        compiler_params=_cparams(("parallel", "arbitrary", "arbitrary")), name="ffn_down_dh",
    )(dy, w, g, u)


def _ffn_down_dw(g, u, dy, w):
    _, m_dim, r_dim = g.shape
    n_dim = dy.shape[1]
    tm = _divisor_tile(m_dim, 2048)
    tn = _divisor_tile(n_dim, 512)
    nm = m_dim // tm

    def body(g_ref, u_ref, dy_ref, o_ref, acc_ref):
        m = pl.program_id(2)

        @pl.when(m == 0)
        def _():
            acc_ref[...] = jnp.zeros_like(acc_ref)

        acc_ref[...] += _bdot(_silu_mul(g_ref[0].astype(F32), u_ref[0].astype(F32)), dy_ref[...], _TN_DIMS)

        @pl.when(m == nm - 1)
        def _():
            o_ref[0] = acc_ref[...].astype(o_ref.dtype)

    h_spec = pl.BlockSpec((1, tm, r_dim), lambda j, n, m: (j, m, 0))
    return pl.pallas_call(
        body, grid=(N_DEV, n_dim // tn, nm),
        in_specs=[h_spec, h_spec, pl.BlockSpec((tm, tn), lambda j, n, m: (m, n))],
        out_specs=pl.BlockSpec((1, r_dim, tn), lambda j, n, m: (j, 0, n)),
        out_shape=jax.ShapeDtypeStruct(w.shape, w.dtype), scratch_shapes=[pltpu.VMEM((r_dim, tn), F32)],
        compiler_params=_cparams(("parallel", "parallel", "arbitrary")), name="ffn_down_dw",
    )(g, u, dy)


@jax.custom_vjp
def ffn_down(g, u, w, x):
    return _ffn_down_fwd(g, u, w, x)


def _ffn_down_vjp_fwd(g, u, w, x):
    return _ffn_down_fwd(g, u, w, x), (g, u, w)


def _ffn_down_vjp_bwd(res, dy):
    g, u, w = res
    dg, du = _ffn_down_dh(dy, g, u, w)
    return dg, du, _ffn_down_dw(g, u, dy, w), dy


ffn_down.defvjp(_ffn_down_vjp_fwd, _ffn_down_vjp_bwd)


def _cols_to_natural_call(g, width):
    _, k_dim, c_dim = g.shape
    tk = _divisor_tile(k_dim, 256, SUBLANE)

    def body(g_ref, o_ref):
        for j in range(N_DEV):
            o_ref[:, j * c_dim:(j + 1) * c_dim] = g_ref[j]
        if width > N_DEV * c_dim:
            o_ref[:, N_DEV * c_dim:] = jnp.zeros((tk, width - N_DEV * c_dim), o_ref.dtype)

    return pl.pallas_call(
        body, grid=(k_dim // tk,), in_specs=[pl.BlockSpec((N_DEV, tk, c_dim), lambda i: (0, i, 0))],
        out_specs=pl.BlockSpec((tk, width), lambda i: (i, 0)), out_shape=jax.ShapeDtypeStruct((k_dim, width), g.dtype),
        compiler_params=_cparams(("parallel",)), name="cols_to_natural",
    )(g)


def _natural_to_cols_call(w, c_dim):
    k_dim, width = w.shape
    tk = _divisor_tile(k_dim, 256, SUBLANE)

    def body(w_ref, o_ref):
        for j in range(N_DEV):
            o_ref[j] = w_ref[:, j * c_dim:(j + 1) * c_dim]

    return pl.pallas_call(
        body, grid=(k_dim // tk,), in_specs=[pl.BlockSpec((tk, width), lambda i: (i, 0))],
        out_specs=pl.BlockSpec((N_DEV, tk, c_dim), lambda i: (0, i, 0)),
        out_shape=jax.ShapeDtypeStruct((N_DEV, k_dim, c_dim), w.dtype),
        compiler_params=_cparams(("parallel",)), name="natural_to_cols",
    )(w)


@functools.partial(jax.custom_vjp, nondiff_argnums=(1,))
def cols_to_natural(g, width):
    return _cols_to_natural_call(g, width)


def _cols_to_natural_fwd(g, width):
    return _cols_to_natural_call(g, width), g.shape[2]


def _cols_to_natural_bwd(width, c_dim, dw):
    return (_natural_to_cols_call(dw, c_dim),)


cols_to_natural.defvjp(_cols_to_natural_fwd, _cols_to_natural_bwd)


def make_rowop(fn, name, tm=256, passthrough=0):
    def specs(rows, params):
        row_specs = [pl.BlockSpec((tm, r.shape[1]), lambda i: (i, 0)) for r in rows]
        par_specs = [pl.BlockSpec(p.shape, lambda i: (0, 0)) for p in params]
        return row_specs, par_specs

    def out_structs(rows, params):
        tiles = [jax.ShapeDtypeStruct((tm, r.shape[1]), r.dtype) for r in rows]
        return jax.eval_shape(lambda r, p: fn(*r, *p), tiles, list(params))

    def fwd_call(rows, params):
        m_dim = rows[0].shape[0]
        n_in = len(rows) + len(params)
        outs = out_structs(rows, params)

        def body(*refs):
            res = fn(*[r[...] for r in refs[:n_in]])
            for o_ref, r in zip(refs[n_in:], res):
                o_ref[...] = r.astype(o_ref.dtype)

        row_specs, par_specs = specs(rows, params)
        return pl.pallas_call(
            body,
            grid=(m_dim // tm,),
            in_specs=row_specs + par_specs,
            out_specs=[pl.BlockSpec((tm, o.shape[1]), lambda i: (i, 0)) for o in outs],
            out_shape=[jax.ShapeDtypeStruct((m_dim, o.shape[1]), o.dtype) for o in outs],
            compiler_params=_cparams(("parallel",)),
            name=name + "_fwd",
        )(*rows, *params)

    def bwd_call(rows, params, cts):
        m_dim = rows[0].shape[0]
        n_rows, n_par = len(rows), len(params)
        n_in = n_rows + n_par
        n_ct = len(cts)
        n_fn = n_ct - passthrough

        def body(*refs):
            vals = [r[...] for r in refs[:n_in]]
            ct_vals = tuple(r[...] for r in refs[n_in:n_in + n_fn])
            pass_refs = refs[n_in + n_fn:n_in + n_ct]
            drow_refs = refs[n_in + n_ct:n_in + n_ct + n_rows]
            dpar_refs = refs[n_in + n_ct + n_rows:]
            _, pullback = jax.vjp(fn, *vals)
            grads = pullback(ct_vals)
            for i, (d_ref, g) in enumerate(zip(drow_refs, grads[:n_rows])):
                d_ref[...] = g + pass_refs[i][...] if i < passthrough else g

            @pl.when(pl.program_id(0) == 0)
            def _():
                for d_ref in dpar_refs:
                    d_ref[...] = jnp.zeros_like(d_ref)

            for d_ref, g in zip(dpar_refs, grads[n_rows:]):
                d_ref[...] += g

        row_specs, par_specs = specs(rows, params)
        ct_specs = [pl.BlockSpec((tm, c.shape[1]), lambda i: (i, 0)) for c in cts]
        res = pl.pallas_call(
            body,
            grid=(m_dim // tm,),
            in_specs=row_specs + par_specs + ct_specs,
            out_specs=row_specs + par_specs,
            out_shape=[jax.ShapeDtypeStruct(r.shape, r.dtype) for r in rows]
            + [jax.ShapeDtypeStruct(p.shape, p.dtype) for p in params],
            compiler_params=_cparams(("arbitrary",)),
            name=name + "_bwd",
        )(*rows, *params, *cts)
        return tuple(res[:n_rows]), tuple(res[n_rows:])

    @jax.custom_vjp
    def op(rows, params):
        return tuple(fwd_call(rows, params)) + tuple(rows[:passthrough])

    def op_fwd(rows, params):
        return tuple(fwd_call(rows, params)) + tuple(rows[:passthrough]), (rows, params)

    def op_bwd(res, cts):
        rows, params = res
        return bwd_call(rows, params, tuple(cts))

    op.defvjp(op_fwd, op_bwd)
    return op


def _rms(x, g):
    return x * lax.rsqrt(jnp.mean(x * x, axis=-1, keepdims=True) + RMS_EPS) * g


def _sigmoid(x):
    return 1.0 / (1.0 + jnp.exp(-x))


def _silu(x):
    return x * _sigmoid(x)


def _bdot(a, b, dims=(((1,), (0,)), ((), ()))):
    return lax.dot_general(a.astype(BF16), b.astype(BF16), dims, preferred_element_type=F32)


def _rmsnorm_fn(x, g):
    return (_rms(x, g),)


NORM_ROWS = 512


def rmsnorm(x, g, name):
    return make_rowop(_rmsnorm_fn, name, tm=min(NORM_ROWS, x.shape[0]))((x,), (g.reshape(1, -1),))[0]


def rmsnorm_res(x, g, name):
    return make_rowop(_rmsnorm_fn, name, tm=NORM_ROWS, passthrough=1)((x,), (g.reshape(1, -1),))


def _gelu_tanh(x):
    return 0.5 * x * (1.0 + jnp.tanh(0.7978845608028654 * (x + 0.044715 * x * x * x)))


def _softplus(x):
    return jnp.maximum(x, 0.0) + jnp.log(1.0 + jnp.exp(-jnp.abs(x)))


def _s5_post_fn(y, w_glu, b_glu):
    h = _gelu_tanh(y)
    return (h * _sigmoid(_bdot(h, w_glu) + b_glu),)


def _loss_fn(y, t):
    err = y - t
    return (0.5 * jnp.mean(err * err, axis=-1, keepdims=True),)


def _pair_headnorm(x, g2):
    lo = lax.broadcasted_iota(jnp.int32, x.shape, 1) < CA_HEAD_DIM
    sq = x * x
    s_lo = jnp.sum(jnp.where(lo, sq, 0.0), axis=-1, keepdims=True)
    s_hi = jnp.sum(jnp.where(lo, 0.0, sq), axis=-1, keepdims=True)
    ms = jnp.where(lo, s_lo, s_hi) * (1.0 / CA_HEAD_DIM)
    return x * lax.rsqrt(ms + RMS_EPS) * g2


def _ca_qknorm_fn(qkv, qg2, kg2):
    qs, ks = [], []
    for j in range(D_MODEL // LANE):
        qs.append(_pair_headnorm(qkv[:, j * LANE:(j + 1) * LANE], qg2))
        ks.append(_pair_headnorm(qkv[:, D_MODEL + j * LANE:D_MODEL + (j + 1) * LANE], kg2))
    return jnp.concatenate(qs, axis=1), jnp.concatenate(ks, axis=1)


def _xattn_fn(q, k, v, qg, kg):
    outs = []
    for h in range(XA_HEADS):
        sl = slice(h * XA_HEAD_DIM, (h + 1) * XA_HEAD_DIM)
        qh = _rms(q[:, sl], qg)
        kh = _rms(k[:, sl], kg)
        s = _bdot(qh, kh, (((1,), (1,)), ((), ()))) * (XA_HEAD_DIM ** -0.5)
        p = jnp.exp(s - jnp.max(s, axis=-1, keepdims=True))
        p = p / jnp.sum(p, axis=-1, keepdims=True)
        outs.append(_bdot(p, v[:, sl]))
    return (jnp.concatenate(outs, axis=1),)


def _gdn_prep_fn(x0, x1, x2, x3, ab, conv_w, alog, dtb):
    c = conv_w[3:4, :] * x0 + conv_w[2:3, :] * x1 + conv_w[1:2, :] * x2 + conv_w[0:1, :] * x3
    c = _silu(c)
    qs, ks = [], []
    for h in range(GDN_HEADS):
        qh = c[:, h * LANE:(h + 1) * LANE]
        kh = c[:, GDN_WIDTH + h * LANE:GDN_WIDTH + (h + 1) * LANE]
        qs.append(qh * lax.rsqrt(jnp.sum(qh * qh, axis=-1, keepdims=True) + RMS_EPS) * (GDN_HEAD_DIM ** -0.5))
        ks.append(kh * lax.rsqrt(jnp.sum(kh * kh, axis=-1, keepdims=True) + RMS_EPS))
    lane = lax.broadcasted_iota(jnp.int32, ab.shape, 1)
    g = -jnp.exp(alog) * _softplus(ab + dtb)
    beta = _sigmoid(ab)
    bg = jnp.where(lane < GDN_HEADS, g, jnp.where(lane < 2 * GDN_HEADS, beta, 0.0))
    return jnp.concatenate(qs, axis=1), jnp.concatenate(ks, axis=1), c[:, 2 * GDN_WIDTH:], bg


def _gdn_out_fn(o, gate, og):
    outs = []
    for h in range(GDN_HEADS):
        sl = slice(h * LANE, (h + 1) * LANE)
        outs.append(_rms(o[:, sl], og) * _silu(gate[:, sl]))
    return (jnp.concatenate(outs, axis=1),)


CA_QB = 4 * CHUNK
CA_KB = CA_QB + CA_PAD


def _ca_math(q2, kb2, vb2, bias2, c):
    lane = lax.broadcasted_iota(jnp.int32, q2.shape, 1)
    qc = lax.broadcasted_iota(jnp.int32, (CA_QB, CA_KB), 0) // CHUNK
    kc = lax.broadcasted_iota(jnp.int32, (CA_QB, CA_KB), 1) // CHUNK
    valid = (kc >= qc) & (kc <= qc + CA_LEFT) & (kc + c * (CA_QB // CHUNK) >= CA_LEFT)
    out = jnp.zeros(q2.shape, F32)
    for h in range(2):
        mine = (lane >= h * CA_HEAD_DIM) & (lane < (h + 1) * CA_HEAD_DIM)
        qh = jnp.where(mine, q2, 0.0)
        s = _bdot(qh, kb2, (((1,), (1,)), ((), ()))) * (CA_HEAD_DIM ** -0.5) + bias2[h]
        s = jnp.where(valid, s, -1e30)
        p = jnp.exp(s - jnp.max(s, axis=-1, keepdims=True))
        p = p / jnp.sum(p, axis=-1, keepdims=True)
        out = out + jnp.where(mine, _bdot(p, vb2), 0.0)
    return out


CA_VEC = CA_QB + CA_KB


def _ca_specs(seq):
    q_spec = pl.BlockSpec((CA_QB, LANE), lambda hp, c: (c, hp))
    kv_spec = pl.BlockSpec((seq + CA_PAD, LANE), lambda hp, c: (0, hp))
    b_spec = pl.BlockSpec((1, 2, CA_VEC), lambda hp, c: (hp, 0, 0))
    return (D_MODEL // LANE, seq // CA_QB), q_spec, kv_spec, b_spec


def _ca_bias_from_vector(vec_ref, bias_ref):
    for h in range(2):
        rows = jnp.broadcast_to(vec_ref[0, h:h + 1, :], (CA_QB, CA_VEC))
        bias_ref[h] = pltpu.roll(rows, 0, 1, stride=1, stride_axis=0)[:, CA_QB:]


def _ca_vector_grad(dbias):
    d = jnp.concatenate([jnp.zeros((CA_QB, CA_QB), F32), dbias], axis=1)
    row = lax.broadcasted_iota(jnp.int32, d.shape, 0)
    for bit in range(CA_QB.bit_length() - 1):
        d = jnp.where((row >> bit) & 1 == 1, pltpu.roll(d, CA_VEC - (1 << bit), 1), d)
    return jnp.sum(d, axis=0, keepdims=True)


def _ca_fwd_call(q, kpad, vpad, vec):
    grid, q_spec, kv_spec, b_spec = _ca_specs(q.shape[0])

    def body(q_ref, k_ref, v_ref, vec_ref, o_ref, bias_ref):
        c = pl.program_id(1)
        start = pl.multiple_of(c * CA_QB, CA_QB)

        @pl.when(c == 0)
        def _():
            _ca_bias_from_vector(vec_ref, bias_ref)

        o_ref[...] = _ca_math(q_ref[...], k_ref[pl.ds(start, CA_KB), :], v_ref[pl.ds(start, CA_KB), :],
                              bias_ref[...], c)

    return pl.pallas_call(
        body, grid=grid, in_specs=[q_spec, kv_spec, kv_spec, b_spec], out_specs=q_spec,
        out_shape=jax.ShapeDtypeStruct(q.shape, F32), scratch_shapes=[pltpu.VMEM((2, CA_QB, CA_KB), F32)],
        compiler_params=_cparams(("parallel", "arbitrary")), name="chunkattn_fwd",
    )(q, kpad, vpad, vec)


def _ca_bwd_call(q, kpad, vpad, vec, do):
    grid, q_spec, kv_spec, b_spec = _ca_specs(q.shape[0])
    last = grid[1] - 1

    def body(q_ref, k_ref, v_ref, vec_ref, do_ref, dq_ref, dk_ref, dv_ref, dvec_ref, bias_ref, dbias_ref):
        c = pl.program_id(1)
        start = pl.multiple_of(c * CA_QB, CA_QB)

        @pl.when(c == 0)
        def _():
            _ca_bias_from_vector(vec_ref, bias_ref)
            dk_ref[...] = jnp.zeros_like(dk_ref)
            dv_ref[...] = jnp.zeros_like(dv_ref)
            dbias_ref[...] = jnp.zeros_like(dbias_ref)

        _, pullback = jax.vjp(lambda a, b, d, e: _ca_math(a, b, d, e, c), q_ref[...],
                              k_ref[pl.ds(start, CA_KB), :], v_ref[pl.ds(start, CA_KB), :], bias_ref[...])
        dq, dkb, dvb, dbias = pullback(do_ref[...])
        dq_ref[...] = dq
        dk_ref[pl.ds(start, CA_KB), :] += dkb
        dv_ref[pl.ds(start, CA_KB), :] += dvb
        dbias_ref[...] += dbias

        @pl.when(c == last)
        def _():
            for h in range(2):
                dvec_ref[0, h:h + 1, :] = _ca_vector_grad(dbias_ref[h])

    return pl.pallas_call(
        body, grid=grid, in_specs=[q_spec, kv_spec, kv_spec, b_spec, q_spec],
        out_specs=[q_spec, kv_spec, kv_spec, b_spec],
        out_shape=[jax.ShapeDtypeStruct(q.shape, F32), jax.ShapeDtypeStruct(kpad.shape, F32),
                   jax.ShapeDtypeStruct(vpad.shape, F32), jax.ShapeDtypeStruct(vec.shape, F32)],
        scratch_shapes=[pltpu.VMEM((2, CA_QB, CA_KB), F32), pltpu.VMEM((2, CA_QB, CA_KB), F32)],
        compiler_params=_cparams(("parallel", "arbitrary")), name="chunkattn_bwd",
    )(q, kpad, vpad, vec, do)


@jax.custom_vjp
def chunk_attn_core(q, kpad, vpad, vec):
    return _ca_fwd_call(q, kpad, vpad, vec)


def _ca_core_fwd(q, kpad, vpad, vec):
    return _ca_fwd_call(q, kpad, vpad, vec), (q, kpad, vpad, vec)


def _ca_core_bwd(res, do):
    return tuple(_ca_bwd_call(*res, do))


chunk_attn_core.defvjp(_ca_core_fwd, _ca_core_bwd)


S5_GB = 4
S5_U = S5_WIDTH // S5_GB
S5_L = S5_GROUPS * S5_STATE // S5_GB


def _cmul(ar, ai, br, bi):
    return ar * br - ai * bi, ar * bi + ai * br


def _hdot(a, b, dims=(((1,), (0,)), ((), ()))):
    return lax.dot_general(a, b, dims, precision=HI, preferred_element_type=F32)


def _split_dot(a, b, dims=(((1,), (0,)), ((), ()))):
    a_hi, b_hi = a.astype(BF16), b.astype(BF16)
    a_lo = (a - a_hi.astype(F32)).astype(BF16)
    b_lo = (b - b_hi.astype(F32)).astype(BF16)
    dot = functools.partial(lax.dot_general, dimension_numbers=dims, preferred_element_type=F32)
    return dot(a_hi, b_hi) + (dot(a_hi, b_lo) + dot(a_lo, b_hi))


_NT = (((1,), (1,)), ((), ()))
_TN = (((0,), (0,)), ((), ()))


def _s5_tables(lr, li, reverse):
    p = {1: (lr, li)}
    p[2] = _cmul(*p[1], *p[1])
    p[4] = _cmul(*p[2], *p[2])
    p[3] = _cmul(*p[2], *p[1])
    p[5] = _cmul(*p[4], *p[1])
    p[6] = _cmul(*p[4], *p[2])
    p[7] = _cmul(*p[4], *p[3])
    p[8] = _cmul(*p[4], *p[4])
    row = lax.broadcasted_iota(jnp.int32, (SUBLANE, lr.shape[1]), 0)
    tr = jnp.zeros(row.shape, F32)
    ti = jnp.zeros(row.shape, F32)
    for i in range(SUBLANE):
        k = SUBLANE - i if reverse else i + 1
        tr = jnp.where(row == i, p[k][0], tr)
        ti = jnp.where(row == i, p[k][1], ti)
    return p, (tr, ti), row


def _s5_block_scan(xr, xi, p, tab, row, hr, hi, reverse):
    for k in (1, 2, 4):
        if reverse:
            sr = jnp.where(row < SUBLANE - k, pltpu.roll(xr, SUBLANE - k, 0), 0.0)
            si = jnp.where(row < SUBLANE - k, pltpu.roll(xi, SUBLANE - k, 0), 0.0)
        else:
            sr = jnp.where(row >= k, pltpu.roll(xr, k, 0), 0.0)
            si = jnp.where(row >= k, pltpu.roll(xi, k, 0), 0.0)
        ar, ai = _cmul(p[k][0], p[k][1], sr, si)
        xr, xi = xr + ar, xi + ai
    cr, ci = _cmul(tab[0], tab[1], hr, hi)
    return xr + cr, xi + ci


def _s5_forward_scan(sr_ref, si_ref, lr, li):
    n_blocks = sr_ref.shape[0] // SUBLANE
    p, tab, row = _s5_tables(lr, li, False)

    def step(b, carry):
        base = pl.multiple_of(b * SUBLANE, SUBLANE)
        xr, xi = _s5_block_scan(sr_ref[pl.ds(base, SUBLANE), :], si_ref[pl.ds(base, SUBLANE), :],
                                p, tab, row, carry[0], carry[1], False)
        sr_ref[pl.ds(base, SUBLANE), :] = xr
        si_ref[pl.ds(base, SUBLANE), :] = xi
        return xr[SUBLANE - 1:SUBLANE, :], xi[SUBLANE - 1:SUBLANE, :]

    zero = jnp.zeros((1, lr.shape[1]), F32)
    lax.fori_loop(0, n_blocks, step, (zero, zero))


def _s5_specs(seq):
    u_spec = pl.BlockSpec((seq, S5_U), lambda g: (0, g))
    bd_spec = pl.BlockSpec((1, S5_U, S5_L), lambda g: (g, 0, 0))
    cd_spec = pl.BlockSpec((1, S5_L, S5_U), lambda g: (g, 0, 0))
    lam_spec = pl.BlockSpec((1, 2, S5_L), lambda g: (g, 0, 0))
    d_spec = pl.BlockSpec((1, S5_U), lambda g: (0, g))
    return u_spec, bd_spec, cd_spec, lam_spec, d_spec


S5_ROWS = 256


def _row_chunks(seq, fn):
    rows_per = min(S5_ROWS, seq)

    def step(r, carry):
        fn(pl.ds(pl.multiple_of(r * rows_per, rows_per), rows_per))
        return carry

    lax.fori_loop(0, seq // rows_per, step, 0)


def _s5_fwd_call(u, bdr, bdi, cdr, cdi, lam, d):
    seq = u.shape[0]
    u_spec, bd_spec, cd_spec, lam_spec, d_spec = _s5_specs(seq)

    def body(u_ref, bdr_ref, bdi_ref, cdr_ref, cdi_ref, lam_ref, d_ref, y_ref, sr_ref, si_ref):
        def project_in(rows):
            uv = u_ref[rows, :]
            sr_ref[rows, :] = _bdot(uv, bdr_ref[0])
            si_ref[rows, :] = _bdot(uv, bdi_ref[0])

        def project_out(rows):
            y_ref[rows, :] = (_bdot(sr_ref[rows, :], cdr_ref[0]) - _bdot(si_ref[rows, :], cdi_ref[0])
                              + d_ref[...] * u_ref[rows, :])

        _row_chunks(seq, project_in)
        _s5_forward_scan(sr_ref, si_ref, lam_ref[0, 0:1, :], lam_ref[0, 1:2, :])
        _row_chunks(seq, project_out)

    return pl.pallas_call(
        body, grid=(S5_GB,), in_specs=[u_spec, bd_spec, bd_spec, cd_spec, cd_spec, lam_spec, d_spec],
        out_specs=u_spec, out_shape=jax.ShapeDtypeStruct(u.shape, F32),
        scratch_shapes=[pltpu.VMEM((seq, S5_L), F32), pltpu.VMEM((seq, S5_L), F32)],
        compiler_params=_cparams(("parallel",)), name="s5_fwd",
    )(u, bdr, bdi, cdr, cdi, lam, d)


def _s5_bwd_call(u, bdr, bdi, cdr, cdi, lam, d, dy):
    seq = u.shape[0]
    n_blocks = seq // SUBLANE
    u_spec, bd_spec, cd_spec, lam_spec, d_spec = _s5_specs(seq)

    def body(u_ref, bdr_ref, bdi_ref, cdr_ref, cdi_ref, lam_ref, d_ref, dy_ref,
             du_ref, dbdr_ref, dbdi_ref, dcdr_ref, dcdi_ref, dlam_ref, dd_ref, sr_ref, si_ref, gr_ref, gi_ref):
        lr, li = lam_ref[0, 0:1, :], lam_ref[0, 1:2, :]

        def project_in(rows):
            uv = u_ref[rows, :]
            dyv = dy_ref[rows, :]
            sr_ref[rows, :] = _bdot(uv, bdr_ref[0])
            si_ref[rows, :] = _bdot(uv, bdi_ref[0])
            gr_ref[rows, :] = _bdot(dyv, cdr_ref[0], _NT)
            gi_ref[rows, :] = -_bdot(dyv, cdi_ref[0], _NT)

        _row_chunks(seq, project_in)
        _s5_forward_scan(sr_ref, si_ref, lr, li)
        p, tab, row = _s5_tables(lr, -li, True)

        def step(i, carry):
            hr, hi, acc_r, acc_i = carry
            b = n_blocks - 1 - i
            base = pl.multiple_of(b * SUBLANE, SUBLANE)
            xr, xi = _s5_block_scan(gr_ref[pl.ds(base, SUBLANE), :], gi_ref[pl.ds(base, SUBLANE), :],
                                    p, tab, row, hr, hi, True)
            gr_ref[pl.ds(base, SUBLANE), :] = xr
            gi_ref[pl.ds(base, SUBLANE), :] = xi
            prev = pl.multiple_of(jnp.maximum(b - 1, 0) * SUBLANE, SUBLANE)
            keep = (b > 0).astype(F32)
            last_r = sr_ref[pl.ds(prev, SUBLANE), :][SUBLANE - 1:SUBLANE, :] * keep
            last_i = si_ref[pl.ds(prev, SUBLANE), :][SUBLANE - 1:SUBLANE, :] * keep
            pr = jnp.where(row >= 1, pltpu.roll(sr_ref[pl.ds(base, SUBLANE), :], 1, 0), last_r)
            pi = jnp.where(row >= 1, pltpu.roll(si_ref[pl.ds(base, SUBLANE), :], 1, 0), last_i)
            acc_r = acc_r + pr * xr + pi * xi
            acc_i = acc_i + pr * xi - pi * xr
            return xr[0:1, :], xi[0:1, :], acc_r, acc_i

        zero = jnp.zeros((1, S5_L), F32)
        zacc = jnp.zeros((SUBLANE, S5_L), F32)
        _, _, acc_r, acc_i = lax.fori_loop(0, n_blocks, step, (zero, zero, zacc, zacc))
        dlam_ref[0, 0:1, :] = jnp.sum(acc_r, axis=0, keepdims=True)
        dlam_ref[0, 1:2, :] = jnp.sum(acc_i, axis=0, keepdims=True)
        for ref in (dbdr_ref, dbdi_ref, dcdr_ref, dcdi_ref, dd_ref):
            ref[...] = jnp.zeros_like(ref)

        def grads(rows):
            uv, dyv = u_ref[rows, :], dy_ref[rows, :]
            grv, giv = gr_ref[rows, :], gi_ref[rows, :]
            du_ref[rows, :] = _bdot(grv, bdr_ref[0], _NT) + _bdot(giv, bdi_ref[0], _NT) + d_ref[...] * dyv
            dbdr_ref[0] += _bdot(uv, grv, _TN)
            dbdi_ref[0] += _bdot(uv, giv, _TN)
            dcdr_ref[0] += _bdot(sr_ref[rows, :], dyv, _TN)
            dcdi_ref[0] -= _bdot(si_ref[rows, :], dyv, _TN)
            dd_ref[...] += jnp.sum(dyv * uv, axis=0, keepdims=True)

        _row_chunks(seq, grads)

    scratch = [pltpu.VMEM((seq, S5_L), F32) for _ in range(4)]
    return pl.pallas_call(
        body, grid=(S5_GB,),
        in_specs=[u_spec, bd_spec, bd_spec, cd_spec, cd_spec, lam_spec, d_spec, u_spec],
        out_specs=[u_spec, bd_spec, bd_spec, cd_spec, cd_spec, lam_spec, d_spec],
        out_shape=[jax.ShapeDtypeStruct(a.shape, F32) for a in (u, bdr, bdi, cdr, cdi, lam, d)],
        scratch_shapes=scratch, compiler_params=_cparams(("parallel",)), name="s5_bwd",
    )(u, bdr, bdi, cdr, cdi, lam, d, dy)


@jax.custom_vjp
def s5_core(u, bdr, bdi, cdr, cdi, lam, d):
    return _s5_fwd_call(u, bdr, bdi, cdr, cdi, lam, d)


def _s5_core_fwd(*args):
    return _s5_fwd_call(*args), args


def _s5_core_bwd(res, dy):
    return tuple(_s5_bwd_call(*res, dy))


s5_core.defvjp(_s5_core_fwd, _s5_core_bwd)


def _s5_discretize(a_re, a_im, log_dt, b_re, b_im, c_re, c_im, d):
    dt = jnp.exp(log_dt)[:, None]
    mag = jnp.exp(a_re * dt)
    lbr, lbi = mag * jnp.cos(a_im * dt), mag * jnp.sin(a_im * dt)
    den = a_re * a_re + a_im * a_im
    fr = ((lbr - 1.0) * a_re + lbi * a_im) / den
    fi = (lbi * a_re - (lbr - 1.0) * a_im) / den
    bbr = fr[:, :, None] * b_re - fi[:, :, None] * b_im
    bbi = fr[:, :, None] * b_im + fi[:, :, None] * b_re
    eye = jnp.eye(S5_GROUPS // S5_GB, dtype=F32)
    gl = S5_GROUPS // S5_GB

    def bd(t):
        return jnp.einsum('bgpc,gh->bgchp', t.reshape(S5_GB, gl, S5_STATE, S5_GROUP), eye).reshape(S5_GB, S5_U, S5_L)

    def cd(t):
        return jnp.einsum('bgcp,gh->bgphc', t.reshape(S5_GB, gl, S5_GROUP, S5_STATE), eye).reshape(S5_GB, S5_L, S5_U)

    lam = jnp.stack([lbr.reshape(S5_GB, S5_L), lbi.reshape(S5_GB, S5_L)], axis=1)
    return bd(bbr), bd(bbi), cd(c_re), cd(c_im), lam, d.reshape(1, S5_WIDTH)


@jax.custom_vjp
def _unit_lower_solve(neg_a, rhs, tinv):
    return _split_dot(tinv, rhs)


def _unit_lower_solve_fwd(neg_a, rhs, tinv):
    x = _split_dot(tinv, rhs)
    return x, (x, tinv)


def _unit_lower_solve_bwd(res, dx):
    x, tinv = res
    g = _split_dot(tinv, dx, _TN)
    return _split_dot(g, x, _NT), g, jnp.zeros_like(tinv)


_unit_lower_solve.defvjp(_unit_lower_solve_fwd, _unit_lower_solve_bwd)


def _unit_lower_inverse(neg_a):
    r = lax.broadcasted_iota(jnp.int32, neg_a.shape, 0)
    c = lax.broadcasted_iota(jnp.int32, neg_a.shape, 1)
    p = (r == c).astype(F32) + neg_a
    npow = _split_dot(neg_a, neg_a)
    for _ in range(4):
        y = _split_dot(jnp.concatenate([p, npow], axis=0), npow)
        p = p + y[:CHUNK]
        npow = y[CHUNK:]
    return p + _split_dot(p, npow)


def _gdn_chunk(q, k, v, g_col, b_col, st, tinv=None):
    r = lax.broadcasted_iota(jnp.int32, (CHUNK, CHUNK), 0)
    c = lax.broadcasted_iota(jnp.int32, (CHUNK, CHUNK), 1)
    eye = (r == c).astype(F32)
    strict = r > c
    causal = r >= c
    g_row = jnp.sum(g_col * eye, axis=0, keepdims=True)
    gcum = jnp.sum(jnp.where(causal, g_row, 0.0), axis=1, keepdims=True)
    gcum_row = jnp.sum(gcum * eye, axis=0, keepdims=True)
    diff = gcum - gcum_row
    decay_strict = jnp.where(strict, jnp.exp(jnp.where(strict, diff, 0.0)), 0.0)
    decay_causal = jnp.where(causal, jnp.exp(jnp.where(causal, diff, 0.0)), 0.0)
    gamma = jnp.exp(gcum)
    g_last = jnp.sum(jnp.where(lax.broadcasted_iota(jnp.int32, (CHUNK, 1), 0) == CHUNK - 1, gcum, 0.0),
                     axis=0, keepdims=True)
    kk = _bdot(k, k, _NT)
    neg_a = -(b_col * kk * decay_strict)
    if tinv is None:
        tinv = _unit_lower_inverse(neg_a)
    x = _unit_lower_solve(neg_a, jnp.concatenate([b_col * v, (b_col * gamma) * k], axis=1), lax.stop_gradient(tinv))
    u_new, w_k = x[:, :GDN_HEAD_DIM], x[:, GDN_HEAD_DIM:]
    qk = _bdot(q, k, _NT) * decay_causal
    q_g = q * gamma
    k_tail = k * jnp.exp(g_last - gcum)
    w = u_new - _bdot(w_k, st)
    o = _bdot(q_g, st) + _bdot(qk, w)
    st_new = jnp.exp(g_last) * st + _bdot(k_tail, w, _TN)
    return o, st_new, tinv


def _gdn_cols(bgv, h):
    lane = lax.broadcasted_iota(jnp.int32, bgv.shape, 1)
    g_col = jnp.sum(jnp.where(lane == h, bgv, 0.0), axis=1, keepdims=True)
    b_col = jnp.sum(jnp.where(lane == GDN_HEADS + h, bgv, 0.0), axis=1, keepdims=True)
    return g_col, b_col


GDN_CPS = 4


def _gdn_fwd_call(q, k, v, bg):
    seq = q.shape[0]
    n_chunks = seq // CHUNK
    n_steps = n_chunks // GDN_CPS
    rows = GDN_CPS * CHUNK
    x_spec = pl.BlockSpec((rows, GDN_WIDTH), lambda n: (n, 0))
    bg_spec = pl.BlockSpec((rows, LANE), lambda n: (n, 0))
    st_spec = pl.BlockSpec((GDN_CPS, GDN_WIDTH, GDN_HEAD_DIM), lambda n: (n, 0, 0))
    ti_spec = pl.BlockSpec((GDN_CPS, GDN_HEADS * CHUNK, CHUNK), lambda n: (n, 0, 0))

    def body(q_ref, k_ref, v_ref, bg_ref, o_ref, st_out_ref, ti_out_ref, st_ref):
        @pl.when(pl.program_id(0) == 0)
        def _():
            st_ref[...] = jnp.zeros_like(st_ref)

        for h in range(GDN_HEADS):
            sl = slice(h * GDN_HEAD_DIM, (h + 1) * GDN_HEAD_DIM)
            st = st_ref[sl, :]
            for cc in range(GDN_CPS):
                rs = slice(cc * CHUNK, (cc + 1) * CHUNK)
                g_col, b_col = _gdn_cols(bg_ref[rs, :], h)
                st_out_ref[cc, sl, :] = st
                o, st, tinv = _gdn_chunk(q_ref[rs, sl], k_ref[rs, sl], v_ref[rs, sl], g_col, b_col, st)
                o_ref[rs, sl] = o
                ti_out_ref[cc, h * CHUNK:(h + 1) * CHUNK, :] = tinv
            st_ref[sl, :] = st

    return pl.pallas_call(
        body, grid=(n_steps,), in_specs=[x_spec, x_spec, x_spec, bg_spec], out_specs=[x_spec, st_spec, ti_spec],
        out_shape=[jax.ShapeDtypeStruct(q.shape, F32),
                   jax.ShapeDtypeStruct((n_chunks, GDN_WIDTH, GDN_HEAD_DIM), F32),
                   jax.ShapeDtypeStruct((n_chunks, GDN_HEADS * CHUNK, CHUNK), F32)],
        scratch_shapes=[pltpu.VMEM((GDN_WIDTH, GDN_HEAD_DIM), F32)],
        compiler_params=_cparams(("arbitrary",)), name="gdn_fwd",
    )(q, k, v, bg)


def _gdn_bwd_call(q, k, v, bg, states, tinvs, do):
    seq = q.shape[0]
    n_steps = seq // CHUNK // GDN_CPS
    rows = GDN_CPS * CHUNK
    x_spec = pl.BlockSpec((rows, GDN_WIDTH), lambda i: (n_steps - 1 - i, 0))
    bg_spec = pl.BlockSpec((rows, LANE), lambda i: (n_steps - 1 - i, 0))
    st_spec = pl.BlockSpec((GDN_CPS, GDN_WIDTH, GDN_HEAD_DIM), lambda i: (n_steps - 1 - i, 0, 0))
    ti_spec = pl.BlockSpec((GDN_CPS, GDN_HEADS * CHUNK, CHUNK), lambda i: (n_steps - 1 - i, 0, 0))

    def body(q_ref, k_ref, v_ref, bg_ref, st_in_ref, ti_ref, do_ref, dq_ref, dk_ref, dv_ref, dbg_ref, dst_ref):
        @pl.when(pl.program_id(0) == 0)
        def _():
            dst_ref[...] = jnp.zeros_like(dst_ref)

        lane = lax.broadcasted_iota(jnp.int32, (CHUNK, LANE), 1)
        dbg = [jnp.zeros((CHUNK, LANE), F32) for _ in range(GDN_CPS)]
        for h in range(GDN_HEADS):
            sl = slice(h * GDN_HEAD_DIM, (h + 1) * GDN_HEAD_DIM)
            dst = dst_ref[sl, :]
            for cc in reversed(range(GDN_CPS)):
                rs = slice(cc * CHUNK, (cc + 1) * CHUNK)
                g_col, b_col = _gdn_cols(bg_ref[rs, :], h)
                tinv = ti_ref[cc, h * CHUNK:(h + 1) * CHUNK, :]
                _, pullback = jax.vjp(lambda *a: _gdn_chunk(*a, tinv=tinv)[:2], q_ref[rs, sl], k_ref[rs, sl],
                                      v_ref[rs, sl], g_col, b_col, st_in_ref[cc, sl, :])
                dq, dk, dv, dg, db, dst = pullback((do_ref[rs, sl], dst))
                dq_ref[rs, sl] = dq
                dk_ref[rs, sl] = dk
                dv_ref[rs, sl] = dv
                dbg[cc] = dbg[cc] + jnp.where(lane == h, dg, 0.0) + jnp.where(lane == GDN_HEADS + h, db, 0.0)
            dst_ref[sl, :] = dst
        for cc in range(GDN_CPS):
            dbg_ref[cc * CHUNK:(cc + 1) * CHUNK, :] = dbg[cc]

    return pl.pallas_call(
        body, grid=(n_steps,), in_specs=[x_spec, x_spec, x_spec, bg_spec, st_spec, ti_spec, x_spec],
        out_specs=[x_spec, x_spec, x_spec, bg_spec],
        out_shape=[jax.ShapeDtypeStruct(q.shape, F32)] * 3 + [jax.ShapeDtypeStruct(bg.shape, F32)],
        scratch_shapes=[pltpu.VMEM((GDN_WIDTH, GDN_HEAD_DIM), F32)],
        compiler_params=_cparams(("arbitrary",)), name="gdn_bwd",
    )(q, k, v, bg, states, tinvs, do)


@jax.custom_vjp
def gdn_core(q, k, v, bg):
    return _gdn_fwd_call(q, k, v, bg)[0]


def _gdn_core_fwd(q, k, v, bg):
    o, states, tinvs = _gdn_fwd_call(q, k, v, bg)
    return o, (q, k, v, bg, states, tinvs)


def _gdn_core_bwd(res, do):
    return tuple(_gdn_bwd_call(*res, do))


gdn_core.defvjp(_gdn_core_fwd, _gdn_core_bwd)


def _row(v):
    return v.reshape(1, -1)


def _lane_pad(v):
    return jnp.pad(v, (0, LANE - v.shape[0])).reshape(1, LANE)


def _delay_rows(x, k):
    return jnp.pad(x, ((k, 0), (0, 0)))[:x.shape[0]]


def s5_mixer(u, a_re, a_im, log_dt, b_re, b_im, c_re, c_im, d, w_glu, b_glu):
    y = s5_core(u, *_s5_discretize(a_re, a_im, log_dt, b_re, b_im, c_re, c_im, d))
    return make_rowop(_s5_post_fn, "s5_post", tm=NORM_ROWS)((y,), (w_glu, _row(b_glu)))[0]


def gated_deltanet(qkv, gate, ab, conv_w, a_log, dt_bias, out_g):
    rows = (qkv, _delay_rows(qkv, 1), _delay_rows(qkv, 2), _delay_rows(qkv, 3), ab)
    q, k, v, bg = make_rowop(_gdn_prep_fn, "gdn_prep")(rows, (conv_w, _lane_pad(a_log), _lane_pad(dt_bias)))
    o = gdn_core(q, k, v, bg)
    return make_rowop(_gdn_out_fn, "gdn_out", tm=NORM_ROWS)((o, gate), (_row(out_g),))[0]


def chunk_attention(x, h, w_qkv, w_out, q_g, k_g, rel_bias):
    qkv = linear_cols(h, w_qkv)
    qn, kn = make_rowop(_ca_qknorm_fn, "ca_qknorm")((qkv,), (_row(jnp.tile(q_g, 2)), _row(jnp.tile(k_g, 2))))
    kpad = jnp.pad(kn, ((CA_PAD, 0), (0, 0)))
    vpad = jnp.pad(qkv[:, 2 * D_MODEL:], ((CA_PAD, 0), (0, 0)))
    o = chunk_attn_core(qn, kpad, vpad, _rel_bias_vector(rel_bias))
    return linear_res(o, w_out, x)


def memory_cross_attention(x, h, mem_n, w_q, w_kv, w_out, q_g, k_g):
    q = linear(h, w_q)
    kv = linear_cols(mem_n, w_kv)
    o = make_rowop(_xattn_fn, "xattn", tm=512)((q,), (kv[:, :D_MODEL], kv[:, D_MODEL:], _row(q_g), _row(k_g)))[0]
    return linear_res(o, w_out, x)


def swiglu(x, h, w_gate, w_up, w_down):
    return ffn_down(linear_cols_dm(h, w_gate), linear_cols_dm(h, w_up), w_down, x)


def _rel_bias_vector(rel_bias):
    heads = rel_bias.shape[0]
    n_far = CA_KB - 1 - MAX_REL
    n_neg = CA_VEC - 1 - n_far - (2 * MAX_REL + 1)
    vec = jnp.concatenate([jnp.zeros((heads, 1), F32),
                           jnp.broadcast_to(rel_bias[:, 2 * MAX_REL:], (heads, n_far)),
                           jnp.flip(rel_bias, axis=1),
                           jnp.broadcast_to(rel_bias[:, :1], (heads, n_neg))], axis=1)
    return vec.reshape(heads // 2, 2, CA_VEC)


_HBM_SPEC = pl.BlockSpec(memory_space=pltpu.HBM)
_SEM_SPEC = pl.BlockSpec(memory_space=pltpu.SEMAPHORE)
_SIDE_EFFECT = pltpu.SideEffectType.DATAFLOW_SIDE_EFFECTING


def _peer(x, y, c, k):
    return (x + (k >> 2)) % 2, (y + ((k >> 1) & 1)) % 2, (c + (k & 1)) % 2


def _exchange_start(arrays, modes, after, name, carry):
    n = len(arrays)
    n_sem = n * (N_DEV - 1)
    lands = [pltpu.with_memory_space_constraint(lax.empty((N_DEV,) + a.shape if m == "gather" else a.shape, a.dtype),
                                                pltpu.HBM) for a, m in zip(arrays, modes)]
    arrays = [pltpu.with_memory_space_constraint(a, pltpu.HBM) for a in arrays]

    def body(*refs):
        ins, zones = refs[:n], refs[n:2 * n]
        send_sems, recv_sems, own_sems = refs[2 * n + 2:2 * n + 5]
        x, y, c = lax.axis_index("x"), lax.axis_index("y"), lax.axis_index("c")
        me = 4 * x + 2 * y + c
        for i in range(n):
            pltpu.make_async_copy(ins[i] if modes[i] == "gather" else ins[i].at[me], zones[i].at[me],
                                  own_sems.at[i]).start()
        for k in range(1, N_DEV):
            px, py, pc = _peer(x, y, c, k)
            peer = 4 * px + 2 * py + pc
            for i in range(n):
                sem = i * (N_DEV - 1) + k - 1
                pltpu.make_async_remote_copy(src_ref=ins[i] if modes[i] == "gather" else ins[i].at[peer],
                                             dst_ref=zones[i].at[me], send_sem=send_sems.at[sem],
                                             recv_sem=recv_sems.at[sem], device_id=(px, py, pc),
                                             device_id_type=pl.DeviceIdType.MESH).start()

    carry = pltpu.with_memory_space_constraint(carry, pltpu.HBM)
    out_shape = ((pltpu.SemaphoreType.DMA((n_sem,)), pltpu.SemaphoreType.DMA((n_sem,)), pltpu.SemaphoreType.DMA((n,)))
                 + tuple(pltpu.HBM(a.shape, a.dtype) for a in arrays) + tuple(pltpu.HBM(z.shape, z.dtype) for z in lands)
                 + (pltpu.HBM(carry.shape, carry.dtype),))
    aliases = {i: 3 + i for i in range(2 * n)}
    aliases[2 * n + 1] = 3 + 2 * n
    res = pl.pallas_call(
        body, name=name, out_shape=out_shape,
        in_specs=[_HBM_SPEC] * (2 * n) + [pl.BlockSpec(memory_space=pl.ANY), _HBM_SPEC],
        out_specs=(_SEM_SPEC,) * 3 + (_HBM_SPEC,) * (2 * n + 1),
        input_output_aliases=aliases,
        compiler_params=pltpu.CompilerParams(has_side_effects=_SIDE_EFFECT),
    )(*arrays, *lands, after, carry)
    return tuple(res[:3]), list(res[3:3 + n]), list(res[3 + n:3 + 2 * n]), res[3 + 2 * n]


def _exchange_wait(started, modes, after, name):
    sems, sources, zones, _ = started
    n = len(sources)

    def body(*refs):
        ins, lands = refs[:n], refs[n:2 * n]
        send_ref, recv_ref, own_ref = refs[2 * n:2 * n + 3]
        x, y, c = lax.axis_index("x"), lax.axis_index("y"), lax.axis_index("c")
        me = 4 * x + 2 * y + c
        for i in range(n):
            pltpu.make_async_copy(ins[i] if modes[i] == "gather" else ins[i].at[me], lands[i].at[me],
                                  own_ref.at[i]).wait()
        for k in range(1, N_DEV):
            px, py, pc = _peer(x, y, c, k)
            peer = 4 * px + 2 * py + pc
            for i in range(n):
                sem = i * (N_DEV - 1) + k - 1
                cp = pltpu.make_async_remote_copy(src_ref=ins[i] if modes[i] == "gather" else ins[i].at[peer],
                                                  dst_ref=lands[i].at[peer], send_sem=send_ref.at[sem],
                                                  recv_sem=recv_ref.at[sem], device_id=(px, py, pc),
                                                  device_id_type=pl.DeviceIdType.MESH)
                cp.wait_send()
                cp.wait_recv()

    res = pl.pallas_call(
        body, name=name,
        out_shape=tuple(pltpu.HBM(a.shape, a.dtype) for a in sources) + tuple(pltpu.HBM(z.shape, z.dtype) for z in zones),
        in_specs=[_HBM_SPEC] * (2 * n) + [_SEM_SPEC] * 3 + [pl.BlockSpec(memory_space=pl.ANY)],
        out_specs=(_HBM_SPEC,) * (2 * n), input_output_aliases={i: i for i in range(2 * n)},
        compiler_params=pltpu.CompilerParams(has_side_effects=_SIDE_EFFECT),
    )(*sources, *zones, *sems, after)
    return list(res[n:])


ADAM_TILE = 64 * 1024


def _adam_call(w, m, v, slots, name):
    n_layers, rows, cols = w.shape
    tr = rows
    if n_layers * rows * cols > ADAM_TILE:
        fits = [t for t in range(SUBLANE, rows, SUBLANE) if rows % t == 0 and n_layers * t * cols <= ADAM_TILE]
        tr = max(fits) if fits else SUBLANE
    c1 = 1.0 - ADAM_B1 ** ADAM_STEP
    c2 = 1.0 - ADAM_B2 ** ADAM_STEP

    def body(*refs):
        w_ref, m_ref, v_ref = refs[:3]
        slot_refs = refs[3:3 + n_layers]
        grad_ref, delta_ref, nm_ref, nv_ref = refs[3 + n_layers:]
        for layer in range(n_layers):
            g = slot_refs[layer][0].astype(F32)
            for k in range(1, N_DEV):
                g = g + slot_refs[layer][k].astype(F32)
            m_new = ADAM_B1 * m_ref[layer] + (1.0 - ADAM_B1) * g
            v_new = ADAM_B2 * v_ref[layer] + (1.0 - ADAM_B2) * (g * g)
            m_hat = m_new / c1
            v_hat = v_new / c2
            grad_ref[layer] = g
            delta_ref[layer] = -ADAM_LR * (m_hat / (jnp.sqrt(v_hat) + ADAM_EPS) + ADAM_WD * w_ref[layer])
            nm_ref[layer] = m_new
            nv_ref[layer] = v_new

    spec = pl.BlockSpec((n_layers, tr, cols), lambda i: (0, i, 0))
    slot_spec = pl.BlockSpec((N_DEV, tr, cols), lambda i: (0, i, 0))
    return pl.pallas_call(
        body, grid=(rows // tr,), in_specs=[spec, spec, spec] + [slot_spec] * n_layers,
        out_specs=[spec] * 4, out_shape=[jax.ShapeDtypeStruct(w.shape, F32)] * 4,
        compiler_params=_cparams(("parallel",)), name=name,
    )(w, m, v, *slots)


WEIGHT_NAMES = ['ab_norm_g', 'ab_w_in', 'ab_w_out', 's5_a_re', 's5_a_im', 's5_log_dt', 's5_b_re', 's5_b_im', 's5_c_re',
                's5_c_im', 's5_d', 's5_w_glu', 's5_b_glu', 'gdn_conv_w', 'gdn_a_log', 'gdn_dt_bias', 'gdn_out_norm_g',
                'c_norm_g', 'c_w_qkv', 'c_w_out', 'c_q_norm_g', 'c_k_norm_g', 'c_rel_bias', 'mem_norm_g', 'xa_norm_g',
                'xa_w_q', 'xa_w_kv', 'xa_w_out', 'xa_q_norm_g', 'xa_k_norm_g', 'f_norm_g', 'f_w_gate', 'f_w_up',
                'f_w_down']

SHARDED = {
    'ab_w_in': ('col', BF16), 'ab_w_out': ('row', BF16), 's5_w_glu': ('row', BF16), 'gdn_conv_w': ('col', F32),
    'c_norm_g': ('col', F32), 'c_w_qkv': ('col', BF16), 'c_w_out': ('row', BF16), 'xa_w_q': ('row', BF16),
    'xa_w_kv': ('col', BF16), 'xa_w_out': ('row', BF16), 'f_w_gate': ('col', BF16), 'f_w_up': ('col', BF16),
    'f_w_down': ('row', BF16),
}
GATHERED_AS_IS = ('c_w_qkv', 'xa_w_kv', 'f_w_gate', 'f_w_up', 'f_w_down')
REPLICATED = [n for n in WEIGHT_NAMES if n not in SHARDED]
PACK_UNIT = SUBLANE * LANE


def _full_from_gathered(g, axis):
    if axis == "row":
        return g.reshape(g.shape[0] * g.shape[1], g.shape[2])
    return jnp.transpose(g, (1, 0, 2)).reshape(g.shape[1], g.shape[0] * g.shape[2])


def _pack(arrays):
    flat = []
    for a in arrays:
        size = a.size
        padded = -(-size // PACK_UNIT) * PACK_UNIT
        flat.append(jnp.pad(a.reshape(-1), (0, padded - size)).reshape(-1, LANE))
    return jnp.concatenate(flat, axis=0)


def _unpack(buf, shapes):
    out, row = [], 0
    for shape in shapes:
        size = math.prod(shape)
        rows = -(-size // PACK_UNIT) * SUBLANE
        out.append(buf[row:row + rows].reshape(-1)[:size].reshape(shape))
        row += rows
    return out


N_STAGES = 2 * DEPTH
EVEN_SHARDED = ['ab_w_in', 'ab_w_out', 's5_w_glu', 'gdn_conv_w']
ODD_SHARDED = ['c_norm_g', 'c_w_qkv', 'c_w_out']
ALL_SHARDED = ['xa_w_q', 'xa_w_kv', 'xa_w_out', 'f_w_gate', 'f_w_up', 'f_w_down']
EVEN_SMALL = ['ab_norm_g', 's5_a_re', 's5_a_im', 's5_log_dt', 's5_b_re', 's5_b_im', 's5_c_re', 's5_c_im', 's5_d',
              's5_b_glu', 'gdn_a_log', 'gdn_dt_bias', 'gdn_out_norm_g']
ODD_SMALL = ['c_q_norm_g', 'c_k_norm_g', 'c_rel_bias']
ALL_SMALL = ['xa_norm_g', 'xa_q_norm_g', 'xa_k_norm_g', 'f_norm_g']


def _stage_params(stage):
    layer, part = divmod(stage, 2)
    if part == 1:
        return [(n, layer) for n in ALL_SHARDED], [(n, layer) for n in ALL_SMALL]
    big, small = (EVEN_SHARDED, EVEN_SMALL) if layer % 2 == 0 else (ODD_SHARDED, ODD_SMALL)
    return [(n, layer // 2) for n in big], [(n, layer // 2) for n in small]


def _stage_forward(stage, landed, small, x, mem_n):
    layer, part = divmod(stage, 2)
    big = {}
    for (n, _), g in zip(_stage_params(stage)[0], landed):
        if n == 'ab_w_in':
            big[n] = cols_to_natural(g, AB_IN_PAD)
        elif n in GATHERED_AS_IS:
            big[n] = g
        elif n == 's5_w_glu':
            big[n] = _full_from_gathered(g, 'row').astype(F32)
        else:
            big[n] = _full_from_gathered(g, SHARDED[n][0])
    if part == 1:
        h, x = rmsnorm_res(x, small['xa_norm_g'], "xa_norm")
        x = memory_cross_attention(x, h, mem_n, big['xa_w_q'], big['xa_w_kv'], big['xa_w_out'],
                                   small['xa_q_norm_g'], small['xa_k_norm_g'])
        h, x = rmsnorm_res(x, small['f_norm_g'], "f_norm")
        return swiglu(x, h, big['f_w_gate'], big['f_w_up'], big['f_w_down'])
    if layer % 2 == 0:
        h, x = rmsnorm_res(x, small['ab_norm_g'], "ab_norm")
        w_in = big['ab_w_in']
        u = linear(h, w_in[:, :S5_WIDTH])
        qkv = linear(h, w_in[:, S5_WIDTH:S5_WIDTH + 3 * GDN_WIDTH])
        gate = linear(h, w_in[:, S5_WIDTH + 3 * GDN_WIDTH:S5_WIDTH + 4 * GDN_WIDTH])
        ab = linear(h, w_in[:, S5_WIDTH + 4 * GDN_WIDTH:])
        a_out = s5_mixer(u, small['s5_a_re'], small['s5_a_im'], small['s5_log_dt'], small['s5_b_re'], small['s5_b_im'],
                         small['s5_c_re'], small['s5_c_im'], small['s5_d'], big['s5_w_glu'], small['s5_b_glu'])
        b_out = gated_deltanet(qkv, gate, ab, big['gdn_conv_w'], small['gdn_a_log'], small['gdn_dt_bias'],
                               small['gdn_out_norm_g'])
        w_out = big['ab_w_out']
        return linear_res(b_out, w_out[S5_WIDTH:], linear_res(a_out, w_out[:S5_WIDTH], x))
    h, x = rmsnorm_res(x, big['c_norm_g'].reshape(-1), "c_norm")
    return chunk_attention(x, h, big['c_w_qkv'], big['c_w_out'], small['c_q_norm_g'], small['c_k_norm_g'],
                           small['c_rel_bias'])


def _loss_rows(x, target):
    return jnp.sum(make_rowop(_loss_fn, "loss", tm=NORM_ROWS)((x, target), ())[0])


def kernel(x, mem, ab_norm_g, ab_w_in, ab_w_out, s5_a_re, s5_a_im, s5_log_dt, s5_b_re, s5_b_im, s5_c_re, s5_c_im, s5_d, s5_w_glu, s5_b_glu, gdn_conv_w, gdn_a_log, gdn_dt_bias, gdn_out_norm_g, c_norm_g, c_w_qkv, c_w_out, c_q_norm_g, c_k_norm_g, c_rel_bias, mem_norm_g, xa_norm_g, xa_w_q, xa_w_kv, xa_w_out, xa_q_norm_g, xa_k_norm_g, f_norm_g, f_w_gate, f_w_up, f_w_down, loss_target, m_ab_norm_g, m_ab_w_in, m_ab_w_out, m_s5_a_re, m_s5_a_im, m_s5_log_dt, m_s5_b_re, m_s5_b_im, m_s5_c_re, m_s5_c_im, m_s5_d, m_s5_w_glu, m_s5_b_glu, m_gdn_conv_w, m_gdn_a_log, m_gdn_dt_bias, m_gdn_out_norm_g, m_c_norm_g, m_c_w_qkv, m_c_w_out, m_c_q_norm_g, m_c_k_norm_g, m_c_rel_bias, m_mem_norm_g, m_xa_norm_g, m_xa_w_q, m_xa_w_kv, m_xa_w_out, m_xa_q_norm_g, m_xa_k_norm_g, m_f_norm_g, m_f_w_gate, m_f_w_up, m_f_w_down, v_ab_norm_g, v_ab_w_in, v_ab_w_out, v_s5_a_re, v_s5_a_im, v_s5_log_dt, v_s5_b_re, v_s5_b_im, v_s5_c_re, v_s5_c_im, v_s5_d, v_s5_w_glu, v_s5_b_glu, v_gdn_conv_w, v_gdn_a_log, v_gdn_dt_bias, v_gdn_out_norm_g, v_c_norm_g, v_c_w_qkv, v_c_w_out, v_c_q_norm_g, v_c_k_norm_g, v_c_rel_bias, v_mem_norm_g, v_xa_norm_g, v_xa_w_q, v_xa_w_kv, v_xa_w_out, v_xa_q_norm_g, v_xa_k_norm_g, v_f_norm_g, v_f_w_gate, v_f_w_up, v_f_w_down):
    given = dict(locals())
    no_after = jnp.zeros((SUBLANE, LANE), F32)

    def shard(n, idx):
        a = given[n][idx]
        return (a.reshape(1, -1) if a.ndim == 1 else a).astype(SHARDED[n][1])

    def gather_start(stage, after, carry):
        arrays = [shard(n, idx) for n, idx in _stage_params(stage)[0]]
        return _exchange_start(arrays, ["gather"] * len(arrays), after, "gather_start_%d" % stage, carry)

    act = x[0]
    mem_n, mem_pullback = jax.vjp(lambda m, g: rmsnorm(m, g, "mem_norm"), mem[0], mem_norm_g)
    in_flight = {}
    for stage in range(2):
        in_flight[stage] = gather_start(stage, no_after, act)
        act = in_flight[stage][3]
    pullbacks = []
    for stage in range(N_STAGES):
        started = in_flight.pop(stage)
        landed = _exchange_wait(started, ["gather"] * len(started[1]), act, "gather_wait_%d" % stage)
        if stage + 2 < N_STAGES:
            in_flight[stage + 2] = gather_start(stage + 2, landed[0], act)
            act = in_flight[stage + 2][3]
        small = {n: given[n][idx] for n, idx in _stage_params(stage)[1]}
        act, pullback = jax.vjp(functools.partial(_stage_forward, stage), landed, small, act, mem_n)
        pullbacks.append(pullback)
    loss_local, loss_pullback = jax.vjp(_loss_rows, act, loss_target[0])
    d_act = loss_pullback(jnp.ones((), F32))[0]

    d_mem_n = jnp.zeros_like(mem_n)
    g_small = {}
    received = [None] * N_STAGES
    started, after = None, no_after
    for stage in reversed(range(N_STAGES)):
        d_landed, d_small, d_act, d_mem = pullbacks[stage](d_act)
        if stage % 2 == 1:
            d_mem_n = d_mem_n + d_mem
        for n, idx in _stage_params(stage)[1]:
            g_small[(n, idx)] = d_small[n]
        if started is not None:
            received[stage + 1] = _exchange_wait(started, ["scatter"] * len(started[1]), d_act,
                                                 "scatter_wait_%d" % (stage + 1))
            after = received[stage + 1][0]
        started = _exchange_start(list(d_landed), ["scatter"] * len(d_landed), after, "scatter_start_%d" % stage,
                                  d_act)
        d_act = started[3]
    g_small[('mem_norm_g', None)] = mem_pullback(d_mem_n)[1]

    def small_grad(n):
        if n == 'mem_norm_g':
            return g_small[(n, None)]
        return jnp.stack([g_small[(n, i)] for i in range(given[n].shape[0])], axis=0)

    small_started = _exchange_start([_pack([small_grad(n) for n in REPLICATED])], ["gather"], d_act,
                                    "small_grads_start", no_after)
    received[0] = _exchange_wait(started, ["scatter"] * len(started[1]), small_started[3], "scatter_wait_0")

    results = {}
    for n in SHARDED:
        slots = {}
        for stage in range(N_STAGES):
            for (pn, idx), r in zip(_stage_params(stage)[0], received[stage]):
                if pn == n:
                    slots[idx] = r
        shape = given[n].shape
        to3d = lambda a: a.reshape(a.shape[0], -1, a.shape[-1])
        outs = _adam_call(to3d(given[n]), to3d(given['m_' + n]), to3d(given['v_' + n]),
                          [slots[i] for i in range(len(slots))], "adamw_" + n)
        results[n] = [o.reshape(shape) for o in outs]
    packed = _exchange_wait(small_started, ["gather"], results['f_w_down'][0], "small_grads_wait")[0]
    outs = _adam_call(_pack([given[n] for n in REPLICATED])[None], _pack([given['m_' + n] for n in REPLICATED])[None],
                      _pack([given['v_' + n] for n in REPLICATED])[None], [packed], "adamw_replicated")
    shapes = [given[n].shape for n in REPLICATED]
    for j, parts in enumerate(zip(*[_unpack(o[0], shapes) for o in outs])):
        results[REPLICATED[j]] = list(parts)

    loss = lax.psum(loss_local, ("x", "y", "c"))
    return (loss, d_act[None], *[results[n][0] for n in WEIGHT_NAMES], *[results[n][1] for n in WEIGHT_NAMES],
            *[results[n][2] for n in WEIGHT_NAMES], *[results[n][3] for n in WEIGHT_NAMES])
```

```python
import functools
import math

import jax
import jax.numpy as jnp
import numpy as np
from jax import lax
from jax.experimental import pallas as pl
from jax.experimental.pallas import tpu as pltpu

F32 = jnp.float32
BF16 = jnp.bfloat16
HI = lax.Precision.HIGHEST

N_DEV = 8
D_MODEL = 1024
SEQ = 2048
DEPTH = 4
CHUNK = 64
N_MEM = 256
RMS_EPS = 1e-6
S5_WIDTH = 512
S5_GROUP = 16
S5_GROUPS = 32
S5_STATE = 64
GDN_HEAD_DIM = 128
GDN_WIDTH = 512
GDN_HEADS = 4
GDN_CONV = 4
AB_IN = S5_WIDTH + 4 * GDN_WIDTH + 2 * GDN_HEADS
AB_IN_PAD = 2688
CA_HEADS = 16
CA_HEAD_DIM = 64
CA_LEFT = 8
CA_BAND = (CA_LEFT + 1) * CHUNK
CA_PAD = CA_LEFT * CHUNK
MAX_REL = 128
XA_HEADS = 4
XA_HEAD_DIM = 256
FFN = 2816
ADAM_LR, ADAM_B1, ADAM_B2, ADAM_EPS, ADAM_WD, ADAM_STEP = 0.001, 0.9, 0.999, 1e-08, 0.01, 10

VMEM_LIMIT = 48 * 1024 * 1024
LANE = 128
SUBLANE = 8


def _cparams(sem=None):
    return pltpu.CompilerParams(dimension_semantics=sem, vmem_limit_bytes=VMEM_LIMIT)


def _divisor_tile(n, target, unit=LANE):
    if n <= target:
        return n
    best = None
    for t in range(unit, target + 1, unit):
        if n % t == 0:
            best = t
    assert best is not None, (n, target)
    return best


def _matmul(a, b, *, ta=False, tb=False, out_dtype=F32, name="mm", res=None):
    if ta:
        k_dim, m_dim = a.shape
    else:
        m_dim, k_dim = a.shape
    if tb:
        n_dim, kb = b.shape
    else:
        kb, n_dim = b.shape
    assert kb == k_dim, (a.shape, b.shape, ta, tb)
    tm = _divisor_tile(m_dim, 1024)
    tn = _divisor_tile(n_dim, 1024)
    tk = _divisor_tile(k_dim, 1408)
    nk = k_dim // tk
    dims = (((0 if ta else 1,), (1 if tb else 0,)), ((), ()))

    def body(a_ref, b_ref, *rest):
        res_ref = rest[0] if res is not None else None
        o_ref, acc_ref = rest[-2:]
        k = pl.program_id(2)

        @pl.when(k == 0)
        def _():
            acc_ref[...] = jnp.zeros_like(acc_ref)

        acc_ref[...] += lax.dot_general(a_ref[...].astype(BF16), b_ref[...].astype(BF16), dims,
                                        preferred_element_type=F32)

        @pl.when(k == nk - 1)
        def _():
            total = acc_ref[...] if res is None else acc_ref[...] + res_ref[...]
            o_ref[...] = total.astype(o_ref.dtype)

    a_spec = pl.BlockSpec((tk, tm), lambda i, j, k: (k, i)) if ta else pl.BlockSpec((tm, tk), lambda i, j, k: (i, k))
    b_spec = pl.BlockSpec((tn, tk), lambda i, j, k: (j, k)) if tb else pl.BlockSpec((tk, tn), lambda i, j, k: (k, j))
    o_spec = pl.BlockSpec((tm, tn), lambda i, j, k: (i, j))
    return pl.pallas_call(
        body,
        grid=(m_dim // tm, n_dim // tn, nk),
        in_specs=[a_spec, b_spec] + ([o_spec] if res is not None else []),
        out_specs=o_spec,
        out_shape=jax.ShapeDtypeStruct((m_dim, n_dim), out_dtype),
        scratch_shapes=[pltpu.VMEM((tm, tn), F32)],
        compiler_params=_cparams(("parallel", "parallel", "arbitrary")),
        name=name,
    )(*((a, b) if res is None else (a, b, res)))


@jax.custom_vjp
def linear(a, w):
    return _matmul(a, w, name="linear_fwd")


def _linear_fwd(a, w):
    return _matmul(a, w, name="linear_fwd"), (a, w)


def _linear_bwd(res, dy):
    a, w = res
    da = _matmul(dy, w, tb=True, name="linear_da")
    dw = _matmul(a, dy, ta=True, out_dtype=w.dtype, name="linear_dw")
    return da, dw


linear.defvjp(_linear_fwd, _linear_bwd)


@jax.custom_vjp
def linear_res(a, w, x):
    return _matmul(a, w, name="linear_res_fwd", res=x)


def _linear_res_fwd(a, w, x):
    return _matmul(a, w, name="linear_res_fwd", res=x), (a, w)


def _linear_res_bwd(res, dy):
    return _linear_bwd(res, dy) + (dy,)


linear_res.defvjp(_linear_res_fwd, _linear_res_bwd)


def _mm_call(name, a, b, out_struct, grid, a_spec, b_spec, o_spec, dims, lead, res=None, keep_a=None):
    nk = grid[-1]
    acc_shape = o_spec.block_shape[1:] if lead[2] else o_spec.block_shape
    a_shape = a_spec.block_shape[1:] if lead[0] else a_spec.block_shape
    assert keep_a is None or nk == 1

    def body(a_ref, b_ref, *rest):
        res_ref = rest[0] if res is not None else None
        o_ref, acc_ref = rest[-1 - (keep_a is not None) - 1], rest[-1 - (keep_a is not None)]
        k = pl.program_id(len(grid) - 1)

        @pl.when(k == 0)
        def _():
            acc_ref[...] = jnp.zeros_like(acc_ref)

        if keep_a is None:
            av = (a_ref[0] if lead[0] else a_ref[...]).astype(BF16)
        else:
            a16_ref = rest[-1]

            @pl.when(pl.program_id(keep_a) == 0)
            def _():
                a16_ref[...] = (a_ref[0] if lead[0] else a_ref[...]).astype(BF16)

            av = a16_ref[...]
        bv = b_ref[0] if lead[1] else b_ref[...]
        acc_ref[...] += lax.dot_general(av, bv.astype(BF16), dims, preferred_element_type=F32)

        @pl.when(k == nk - 1)
        def _():
            if lead[2]:
                o_ref[0] = acc_ref[...].astype(o_ref.dtype)
            elif res is not None:
                o_ref[...] = (acc_ref[...] + res_ref[...]).astype(o_ref.dtype)
            else:
                o_ref[...] = acc_ref[...].astype(o_ref.dtype)

    return pl.pallas_call(
        body, grid=grid, in_specs=[a_spec, b_spec] + ([o_spec] if res is not None else []), out_specs=o_spec,
        out_shape=out_struct,
        scratch_shapes=[pltpu.VMEM(tuple(acc_shape), F32)] + ([pltpu.VMEM(tuple(a_shape), BF16)] if keep_a is not None else []),
        compiler_params=_cparams(("parallel", "arbitrary", "arbitrary")), name=name,
    )(*((a, b) if res is None else (a, b, res)))


_NN = (((1,), (0,)), ((), ()))
_NT_DIMS = (((1,), (1,)), ((), ()))
_TN_DIMS = (((0,), (0,)), ((), ()))


def _cols_fwd(a, g, dm_out):
    m_dim, k_dim = a.shape
    _, _, c_dim = g.shape
    tm = _divisor_tile(m_dim, 2048 if dm_out else 1024)
    tk = _divisor_tile(k_dim, 1024)
    a_spec = pl.BlockSpec((tm, tk), lambda i, j, k: (i, k))
    b_spec = pl.BlockSpec((1, tk, c_dim), lambda i, j, k: (j, k, 0))
    if dm_out:
        o_spec = pl.BlockSpec((1, tm, c_dim), lambda i, j, k: (j, i, 0))
        out = jax.ShapeDtypeStruct((N_DEV, m_dim, c_dim), BF16)
    else:
        o_spec = pl.BlockSpec((tm, c_dim), lambda i, j, k: (i, j))
        out = jax.ShapeDtypeStruct((m_dim, N_DEV * c_dim), F32)
    return _mm_call("cols_fwd", a, g, out, (m_dim // tm, N_DEV, k_dim // tk), a_spec, b_spec, o_spec, _NN,
                    (False, True, dm_out), keep_a=1 if k_dim == tk else None)


def _cols_da(dy, g, dm_out):
    _, k_dim, c_dim = g.shape
    m_dim = dy.shape[1] if dm_out else dy.shape[0]
    tm = _divisor_tile(m_dim, 2048 if dm_out else 1024)
    tk = _divisor_tile(k_dim, 1024)
    if dm_out:
        a_spec = pl.BlockSpec((1, tm, c_dim), lambda i, kb, j: (j, i, 0))
    else:
        a_spec = pl.BlockSpec((tm, c_dim), lambda i, kb, j: (i, j))
    b_spec = pl.BlockSpec((1, tk, c_dim), lambda i, kb, j: (j, kb, 0))
    o_spec = pl.BlockSpec((tm, tk), lambda i, kb, j: (i, kb))
    return _mm_call("cols_da", dy, g, jax.ShapeDtypeStruct((m_dim, k_dim), F32), (m_dim // tm, k_dim // tk, N_DEV),
                    a_spec, b_spec, o_spec, _NT_DIMS, (dm_out, True, False))


def _cols_dg(a, dy, g, dm_out):
    _, k_dim, c_dim = g.shape
    m_dim = a.shape[0]
    tm = _divisor_tile(m_dim, 2048)
    tk = _divisor_tile(k_dim, 512)
    a_spec = pl.BlockSpec((tm, tk), lambda kb, j, m: (m, kb))
    if dm_out:
        b_spec = pl.BlockSpec((1, tm, c_dim), lambda kb, j, m: (j, m, 0))
    else:
        b_spec = pl.BlockSpec((tm, c_dim), lambda kb, j, m: (m, j))
    o_spec = pl.BlockSpec((1, tk, c_dim), lambda kb, j, m: (j, kb, 0))
    return _mm_call("cols_dg", a, dy, jax.ShapeDtypeStruct(g.shape, g.dtype), (k_dim // tk, N_DEV, m_dim // tm),
                    a_spec, b_spec, o_spec, _TN_DIMS, (False, dm_out, True), keep_a=1 if m_dim == tm else None)


def _make_linear_cols(dm_out):
    @jax.custom_vjp
    def op(a, g):
        return _cols_fwd(a, g, dm_out)

    def fwd(a, g):
        return _cols_fwd(a, g, dm_out), (a, g)

    def bwd(res, dy):
        a, g = res
        return _cols_da(dy, g, dm_out), _cols_dg(a, dy, g, dm_out)

    op.defvjp(fwd, bwd)
    return op


linear_cols = _make_linear_cols(False)
linear_cols_dm = _make_linear_cols(True)


def _silu_mul(g, u):
    return _silu(g) * u


def _ffn_down_fwd(g, u, w, x):
    _, m_dim, r_dim = g.shape
    n_dim = w.shape[2]
    tm = _divisor_tile(m_dim, 1024)
    tn = _divisor_tile(n_dim, 1024)
    nj = N_DEV

    def body(g_ref, u_ref, w_ref, x_ref, o_ref, acc_ref):
        j = pl.program_id(2)

        @pl.when(j == 0)
        def _():
            acc_ref[...] = jnp.zeros_like(acc_ref)

        acc_ref[...] += _bdot(_silu_mul(g_ref[0].astype(F32), u_ref[0].astype(F32)), w_ref[0])

        @pl.when(j == nj - 1)
        def _():
            o_ref[...] = acc_ref[...] + x_ref[...]

    h_spec = pl.BlockSpec((1, tm, r_dim), lambda i, n, j: (j, i, 0))
    o_spec = pl.BlockSpec((tm, tn), lambda i, n, j: (i, n))
    return pl.pallas_call(
        body, grid=(m_dim // tm, n_dim // tn, nj),
        in_specs=[h_spec, h_spec, pl.BlockSpec((1, r_dim, tn), lambda i, n, j: (j, 0, n)), o_spec], out_specs=o_spec,
        out_shape=jax.ShapeDtypeStruct((m_dim, n_dim), F32), scratch_shapes=[pltpu.VMEM((tm, tn), F32)],
        compiler_params=_cparams(("parallel", "parallel", "arbitrary")), name="ffn_down_fwd",
    )(g, u, w, x)


def _ffn_down_dh(dy, g, u, w):
    m_dim, n_dim = dy.shape
    r_dim = w.shape[1]
    tm = _divisor_tile(m_dim, 2048)
    tn = _divisor_tile(n_dim, 1024)
    nn = n_dim // tn

    def body(dy_ref, w_ref, g_ref, u_ref, dg_ref, du_ref, acc_ref):
        n = pl.program_id(2)

        @pl.when(n == 0)
        def _():
            acc_ref[...] = jnp.zeros_like(acc_ref)

        acc_ref[...] += _bdot(dy_ref[...], w_ref[0], _NT_DIMS)

        @pl.when(n == nn - 1)
        def _():
            _, pullback = jax.vjp(_silu_mul, g_ref[0].astype(F32), u_ref[0].astype(F32))
            dg, du = pullback(acc_ref[...])
            dg_ref[0] = dg.astype(dg_ref.dtype)
            du_ref[0] = du.astype(du_ref.dtype)

    h_spec = pl.BlockSpec((1, tm, r_dim), lambda i, j, n: (j, i, 0))
    return pl.pallas_call(
        body, grid=(m_dim // tm, N_DEV, nn),
        in_specs=[pl.BlockSpec((tm, tn), lambda i, j, n: (i, n)), pl.BlockSpec((1, r_dim, tn), lambda i, j, n: (j, 0, n)),
                  h_spec, h_spec],
        out_specs=[h_spec, h_spec], out_shape=[jax.ShapeDtypeStruct(g.shape, g.dtype)] * 2,
        scratch_shapes=[pltpu.VMEM((tm, r_dim), F32)],
        compiler_params=_cparams(("parallel", "arbitrary", "arbitrary")), name="ffn_down_dh",
    )(dy, w, g, u)


def _ffn_down_dw(g, u, dy, w):
    _, m_dim, r_dim = g.shape
    n_dim = dy.shape[1]
    tm = _divisor_tile(m_dim, 2048)
    tn = _divisor_tile(n_dim, 512)
    nm = m_dim // tm

    def body(g_ref, u_ref, dy_ref, o_ref, acc_ref):
        m = pl.program_id(2)

        @pl.when(m == 0)
        def _():
            acc_ref[...] = jnp.zeros_like(acc_ref)

        acc_ref[...] += _bdot(_silu_mul(g_ref[0].astype(F32), u_ref[0].astype(F32)), dy_ref[...], _TN_DIMS)

        @pl.when(m == nm - 1)
        def _():
            o_ref[0] = acc_ref[...].astype(o_ref.dtype)

    h_spec = pl.BlockSpec((1, tm, r_dim), lambda j, n, m: (j, m, 0))
    return pl.pallas_call(
        body, grid=(N_DEV, n_dim // tn, nm),
        in_specs=[h_spec, h_spec, pl.BlockSpec((tm, tn), lambda j, n, m: (m, n))],
        out_specs=pl.BlockSpec((1, r_dim, tn), lambda j, n, m: (j, 0, n)),
        out_shape=jax.ShapeDtypeStruct(w.shape, w.dtype), scratch_shapes=[pltpu.VMEM((r_dim, tn), F32)],
        compiler_params=_cparams(("parallel", "parallel", "arbitrary")), name="ffn_down_dw",
    )(g, u, dy)


@jax.custom_vjp
def ffn_down(g, u, w, x):
    return _ffn_down_fwd(g, u, w, x)


def _ffn_down_vjp_fwd(g, u, w, x):
    return _ffn_down_fwd(g, u, w, x), (g, u, w)


def _ffn_down_vjp_bwd(res, dy):
    g, u, w = res
    dg, du = _ffn_down_dh(dy, g, u, w)
    return dg, du, _ffn_down_dw(g, u, dy, w), dy


ffn_down.defvjp(_ffn_down_vjp_fwd, _ffn_down_vjp_bwd)


def _cols_to_natural_call(g, width):
    _, k_dim, c_dim = g.shape
    tk = _divisor_tile(k_dim, 256, SUBLANE)

    def body(g_ref, o_ref):
        for j in range(N_DEV):
            o_ref[:, j * c_dim:(j + 1) * c_dim] = g_ref[j]
        if width > N_DEV * c_dim:
            o_ref[:, N_DEV * c_dim:] = jnp.zeros((tk, width - N_DEV * c_dim), o_ref.dtype)

    return pl.pallas_call(
        body, grid=(k_dim // tk,), in_specs=[pl.BlockSpec((N_DEV, tk, c_dim), lambda i: (0, i, 0))],
        out_specs=pl.BlockSpec((tk, width), lambda i: (i, 0)), out_shape=jax.ShapeDtypeStruct((k_dim, width), g.dtype),
        compiler_params=_cparams(("parallel",)), name="cols_to_natural",
    )(g)


def _natural_to_cols_call(w, c_dim):
    k_dim, width = w.shape
    tk = _divisor_tile(k_dim, 256, SUBLANE)

    def body(w_ref, o_ref):
        for j in range(N_DEV):
            o_ref[j] = w_ref[:, j * c_dim:(j + 1) * c_dim]

    return pl.pallas_call(
        body, grid=(k_dim // tk,), in_specs=[pl.BlockSpec((tk, width), lambda i: (i, 0))],
        out_specs=pl.BlockSpec((N_DEV, tk, c_dim), lambda i: (0, i, 0)),
        out_shape=jax.ShapeDtypeStruct((N_DEV, k_dim, c_dim), w.dtype),
        compiler_params=_cparams(("parallel",)), name="natural_to_cols",
    )(w)


@functools.partial(jax.custom_vjp, nondiff_argnums=(1,))
def cols_to_natural(g, width):
    return _cols_to_natural_call(g, width)


def _cols_to_natural_fwd(g, width):
    return _cols_to_natural_call(g, width), g.shape[2]


def _cols_to_natural_bwd(width, c_dim, dw):
    return (_natural_to_cols_call(dw, c_dim),)


cols_to_natural.defvjp(_cols_to_natural_fwd, _cols_to_natural_bwd)


def make_rowop(fn, name, tm=256, passthrough=0):
    def specs(rows, params):
        row_specs = [pl.BlockSpec((tm, r.shape[1]), lambda i: (i, 0)) for r in rows]
        par_specs = [pl.BlockSpec(p.shape, lambda i: (0, 0)) for p in params]
        return row_specs, par_specs

    def out_structs(rows, params):
        tiles = [jax.ShapeDtypeStruct((tm, r.shape[1]), r.dtype) for r in rows]
        return jax.eval_shape(lambda r, p: fn(*r, *p), tiles, list(params))

    def fwd_call(rows, params):
        m_dim = rows[0].shape[0]
        n_in = len(rows) + len(params)
        outs = out_structs(rows, params)

        def body(*refs):
            res = fn(*[r[...] for r in refs[:n_in]])
            for o_ref, r in zip(refs[n_in:], res):
                o_ref[...] = r.astype(o_ref.dtype)

        row_specs, par_specs = specs(rows, params)
        return pl.pallas_call(
            body,
            grid=(m_dim // tm,),
            in_specs=row_specs + par_specs,
            out_specs=[pl.BlockSpec((tm, o.shape[1]), lambda i: (i, 0)) for o in outs],
            out_shape=[jax.ShapeDtypeStruct((m_dim, o.shape[1]), o.dtype) for o in outs],
            compiler_params=_cparams(("parallel",)),
            name=name + "_fwd",
        )(*rows, *params)

    def bwd_call(rows, params, cts):
        m_dim = rows[0].shape[0]
        n_rows, n_par = len(rows), len(params)
        n_in = n_rows + n_par
        n_ct = len(cts)
        n_fn = n_ct - passthrough

        def body(*refs):
            vals = [r[...] for r in refs[:n_in]]
            ct_vals = tuple(r[...] for r in refs[n_in:n_in + n_fn])
            pass_refs = refs[n_in + n_fn:n_in + n_ct]
            drow_refs = refs[n_in + n_ct:n_in + n_ct + n_rows]
            dpar_refs = refs[n_in + n_ct + n_rows:]
            _, pullback = jax.vjp(fn, *vals)
            grads = pullback(ct_vals)
            for i, (d_ref, g) in enumerate(zip(drow_refs, grads[:n_rows])):
                d_ref[...] = g + pass_refs[i][...] if i < passthrough else g

            @pl.when(pl.program_id(0) == 0)
            def _():
                for d_ref in dpar_refs:
                    d_ref[...] = jnp.zeros_like(d_ref)

            for d_ref, g in zip(dpar_refs, grads[n_rows:]):
                d_ref[...] += g

        row_specs, par_specs = specs(rows, params)
        ct_specs = [pl.BlockSpec((tm, c.shape[1]), lambda i: (i, 0)) for c in cts]
        res = pl.pallas_call(
            body,
            grid=(m_dim // tm,),
            in_specs=row_specs + par_specs + ct_specs,
            out_specs=row_specs + par_specs,
            out_shape=[jax.ShapeDtypeStruct(r.shape, r.dtype) for r in rows]
            + [jax.ShapeDtypeStruct(p.shape, p.dtype) for p in params],
            compiler_params=_cparams(("arbitrary",)),
            name=name + "_bwd",
        )(*rows, *params, *cts)
        return tuple(res[:n_rows]), tuple(res[n_rows:])

    @jax.custom_vjp
    def op(rows, params):
        return tuple(fwd_call(rows, params)) + tuple(rows[:passthrough])

    def op_fwd(rows, params):
        return tuple(fwd_call(rows, params)) + tuple(rows[:passthrough]), (rows, params)

    def op_bwd(res, cts):
        rows, params = res
        return bwd_call(rows, params, tuple(cts))

    op.defvjp(op_fwd, op_bwd)
    return op


def _rms(x, g):
    return x * lax.rsqrt(jnp.mean(x * x, axis=-1, keepdims=True) + RMS_EPS) * g


def _sigmoid(x):
    return 1.0 / (1.0 + jnp.exp(-x))


def _silu(x):
    return x * _sigmoid(x)


def _bdot(a, b, dims=(((1,), (0,)), ((), ()))):
    return lax.dot_general(a.astype(BF16), b.astype(BF16), dims, preferred_element_type=F32)


def _rmsnorm_fn(x, g):
    return (_rms(x, g),)


NORM_ROWS = 512


def rmsnorm(x, g, name):
    return make_rowop(_rmsnorm_fn, name, tm=min(NORM_ROWS, x.shape[0]))((x,), (g.reshape(1, -1),))[0]


def rmsnorm_res(x, g, name):
    return make_rowop(_rmsnorm_fn, name, tm=NORM_ROWS, passthrough=1)((x,), (g.reshape(1, -1),))


def _gelu_tanh(x):
    return 0.5 * x * (1.0 + jnp.tanh(0.7978845608028654 * (x + 0.044715 * x * x * x)))


def _softplus(x):
    return jnp.maximum(x, 0.0) + jnp.log(1.0 + jnp.exp(-jnp.abs(x)))


def _s5_post_fn(y, w_glu, b_glu):
    h = _gelu_tanh(y)
    return (h * _sigmoid(_bdot(h, w_glu) + b_glu),)


def _loss_fn(y, t):
    err = y - t
    return (0.5 * jnp.mean(err * err, axis=-1, keepdims=True),)


def _pair_headnorm(x, g2):
    lo = lax.broadcasted_iota(jnp.int32, x.shape, 1) < CA_HEAD_DIM
    sq = x * x
    s_lo = jnp.sum(jnp.where(lo, sq, 0.0), axis=-1, keepdims=True)
    s_hi = jnp.sum(jnp.where(lo, 0.0, sq), axis=-1, keepdims=True)
    ms = jnp.where(lo, s_lo, s_hi) * (1.0 / CA_HEAD_DIM)
    return x * lax.rsqrt(ms + RMS_EPS) * g2


def _ca_qknorm_fn(qkv, qg2, kg2):
    qs, ks = [], []
    for j in range(D_MODEL // LANE):
        qs.append(_pair_headnorm(qkv[:, j * LANE:(j + 1) * LANE], qg2))
        ks.append(_pair_headnorm(qkv[:, D_MODEL + j * LANE:D_MODEL + (j + 1) * LANE], kg2))
    return jnp.concatenate(qs, axis=1), jnp.concatenate(ks, axis=1)


def _xattn_fn(q, k, v, qg, kg):
    outs = []
    for h in range(XA_HEADS):
        sl = slice(h * XA_HEAD_DIM, (h + 1) * XA_HEAD_DIM)
        qh = _rms(q[:, sl], qg)
        kh = _rms(k[:, sl], kg)
        s = _bdot(qh, kh, (((1,), (1,)), ((), ()))) * (XA_HEAD_DIM ** -0.5)
        p = jnp.exp(s - jnp.max(s, axis=-1, keepdims=True))
        p = p / jnp.sum(p, axis=-1, keepdims=True)
        outs.append(_bdot(p, v[:, sl]))
    return (jnp.concatenate(outs, axis=1),)


def _gdn_prep_fn(x0, x1, x2, x3, ab, conv_w, alog, dtb):
    c = conv_w[3:4, :] * x0 + conv_w[2:3, :] * x1 + conv_w[1:2, :] * x2 + conv_w[0:1, :] * x3
    c = _silu(c)
    qs, ks = [], []
    for h in range(GDN_HEADS):
        qh = c[:, h * LANE:(h + 1) * LANE]
        kh = c[:, GDN_WIDTH + h * LANE:GDN_WIDTH + (h + 1) * LANE]
        qs.append(qh * lax.rsqrt(jnp.sum(qh * qh, axis=-1, keepdims=True) + RMS_EPS) * (GDN_HEAD_DIM ** -0.5))
        ks.append(kh * lax.rsqrt(jnp.sum(kh * kh, axis=-1, keepdims=True) + RMS_EPS))
    lane = lax.broadcasted_iota(jnp.int32, ab.shape, 1)
    g = -jnp.exp(alog) * _softplus(ab + dtb)
    beta = _sigmoid(ab)
    bg = jnp.where(lane < GDN_HEADS, g, jnp.where(lane < 2 * GDN_HEADS, beta, 0.0))
    return jnp.concatenate(qs, axis=1), jnp.concatenate(ks, axis=1), c[:, 2 * GDN_WIDTH:], bg


def _gdn_out_fn(o, gate, og):
    outs = []
    for h in range(GDN_HEADS):
        sl = slice(h * LANE, (h + 1) * LANE)
        outs.append(_rms(o[:, sl], og) * _silu(gate[:, sl]))
    return (jnp.concatenate(outs, axis=1),)


CA_QB = 4 * CHUNK
CA_KB = CA_QB + CA_PAD


def _ca_math(q2, kb2, vb2, bias2, c):
    lane = lax.broadcasted_iota(jnp.int32, q2.shape, 1)
    qc = lax.broadcasted_iota(jnp.int32, (CA_QB, CA_KB), 0) // CHUNK
    kc = lax.broadcasted_iota(jnp.int32, (CA_QB, CA_KB), 1) // CHUNK
    valid = (kc >= qc) & (kc <= qc + CA_LEFT) & (kc + c * (CA_QB // CHUNK) >= CA_LEFT)
    out = jnp.zeros(q2.shape, F32)
    for h in range(2):
        mine = (lane >= h * CA_HEAD_DIM) & (lane < (h + 1) * CA_HEAD_DIM)
        qh = jnp.where(mine, q2, 0.0)
        s = _bdot(qh, kb2, (((1,), (1,)), ((), ()))) * (CA_HEAD_DIM ** -0.5) + bias2[h]
        s = jnp.where(valid, s, -1e30)
        p = jnp.exp(s - jnp.max(s, axis=-1, keepdims=True))
        p = p / jnp.sum(p, axis=-1, keepdims=True)
        out = out + jnp.where(mine, _bdot(p, vb2), 0.0)
    return out


CA_VEC = CA_QB + CA_KB


def _ca_specs(seq):
    q_spec = pl.BlockSpec((CA_QB, LANE), lambda hp, c: (c, hp))
    kv_spec = pl.BlockSpec((seq + CA_PAD, LANE), lambda hp, c: (0, hp))
    b_spec = pl.BlockSpec((1, 2, CA_VEC), lambda hp, c: (hp, 0, 0))
    return (D_MODEL // LANE, seq // CA_QB), q_spec, kv_spec, b_spec


def _ca_bias_from_vector(vec_ref, bias_ref):
    for h in range(2):
        rows = jnp.broadcast_to(vec_ref[0, h:h + 1, :], (CA_QB, CA_VEC))
        bias_ref[h] = pltpu.roll(rows, 0, 1, stride=1, stride_axis=0)[:, CA_QB:]


def _ca_vector_grad(dbias):
    d = jnp.concatenate([jnp.zeros((CA_QB, CA_QB), F32), dbias], axis=1)
    row = lax.broadcasted_iota(jnp.int32, d.shape, 0)
    for bit in range(CA_QB.bit_length() - 1):
        d = jnp.where((row >> bit) & 1 == 1, pltpu.roll(d, CA_VEC - (1 << bit), 1), d)
    return jnp.sum(d, axis=0, keepdims=True)


def _ca_fwd_call(q, kpad, vpad, vec):
    grid, q_spec, kv_spec, b_spec = _ca_specs(q.shape[0])

    def body(q_ref, k_ref, v_ref, vec_ref, o_ref, bias_ref):
        c = pl.program_id(1)
        start = pl.multiple_of(c * CA_QB, CA_QB)

        @pl.when(c == 0)
        def _():
            _ca_bias_from_vector(vec_ref, bias_ref)

        o_ref[...] = _ca_math(q_ref[...], k_ref[pl.ds(start, CA_KB), :], v_ref[pl.ds(start, CA_KB), :],
                              bias_ref[...], c)

    return pl.pallas_call(
        body, grid=grid, in_specs=[q_spec, kv_spec, kv_spec, b_spec], out_specs=q_spec,
        out_shape=jax.ShapeDtypeStruct(q.shape, F32), scratch_shapes=[pltpu.VMEM((2, CA_QB, CA_KB), F32)],
        compiler_params=_cparams(("parallel", "arbitrary")), name="chunkattn_fwd",
    )(q, kpad, vpad, vec)


def _ca_bwd_call(q, kpad, vpad, vec, do):
    grid, q_spec, kv_spec, b_spec = _ca_specs(q.shape[0])
    last = grid[1] - 1

    def body(q_ref, k_ref, v_ref, vec_ref, do_ref, dq_ref, dk_ref, dv_ref, dvec_ref, bias_ref, dbias_ref):
        c = pl.program_id(1)
        start = pl.multiple_of(c * CA_QB, CA_QB)

        @pl.when(c == 0)
        def _():
            _ca_bias_from_vector(vec_ref, bias_ref)
            dk_ref[...] = jnp.zeros_like(dk_ref)
            dv_ref[...] = jnp.zeros_like(dv_ref)
            dbias_ref[...] = jnp.zeros_like(dbias_ref)

        _, pullback = jax.vjp(lambda a, b, d, e: _ca_math(a, b, d, e, c), q_ref[...],
                              k_ref[pl.ds(start, CA_KB), :], v_ref[pl.ds(start, CA_KB), :], bias_ref[...])
        dq, dkb, dvb, dbias = pullback(do_ref[...])
        dq_ref[...] = dq
        dk_ref[pl.ds(start, CA_KB), :] += dkb
        dv_ref[pl.ds(start, CA_KB), :] += dvb
        dbias_ref[...] += dbias

        @pl.when(c == last)
        def _():
            for h in range(2):
                dvec_ref[0, h:h + 1, :] = _ca_vector_grad(dbias_ref[h])

    return pl.pallas_call(
        body, grid=grid, in_specs=[q_spec, kv_spec, kv_spec, b_spec, q_spec],
        out_specs=[q_spec, kv_spec, kv_spec, b_spec],
        out_shape=[jax.ShapeDtypeStruct(q.shape, F32), jax.ShapeDtypeStruct(kpad.shape, F32),
                   jax.ShapeDtypeStruct(vpad.shape, F32), jax.ShapeDtypeStruct(vec.shape, F32)],
        scratch_shapes=[pltpu.VMEM((2, CA_QB, CA_KB), F32), pltpu.VMEM((2, CA_QB, CA_KB), F32)],
        compiler_params=_cparams(("parallel", "arbitrary")), name="chunkattn_bwd",
    )(q, kpad, vpad, vec, do)


@jax.custom_vjp
def chunk_attn_core(q, kpad, vpad, vec):
    return _ca_fwd_call(q, kpad, vpad, vec)


def _ca_core_fwd(q, kpad, vpad, vec):
    return _ca_fwd_call(q, kpad, vpad, vec), (q, kpad, vpad, vec)


def _ca_core_bwd(res, do):
    return tuple(_ca_bwd_call(*res, do))


chunk_attn_core.defvjp(_ca_core_fwd, _ca_core_bwd)


S5_GB = 4
S5_U = S5_WIDTH // S5_GB
S5_L = S5_GROUPS * S5_STATE // S5_GB


def _cmul(ar, ai, br, bi):
    return ar * br - ai * bi, ar * bi + ai * br


def _hdot(a, b, dims=(((1,), (0,)), ((), ()))):
    return lax.dot_general(a, b, dims, precision=HI, preferred_element_type=F32)


def _split_dot(a, b, dims=(((1,), (0,)), ((), ()))):
    a_hi, b_hi = a.astype(BF16), b.astype(BF16)
    a_lo = (a - a_hi.astype(F32)).astype(BF16)
    b_lo = (b - b_hi.astype(F32)).astype(BF16)
    dot = functools.partial(lax.dot_general, dimension_numbers=dims, preferred_element_type=F32)
    return dot(a_hi, b_hi) + (dot(a_hi, b_lo) + dot(a_lo, b_hi))


_NT = (((1,), (1,)), ((), ()))
_TN = (((0,), (0,)), ((), ()))


def _s5_tables(lr, li, reverse):
    p = {1: (lr, li)}
    p[2] = _cmul(*p[1], *p[1])
    p[4] = _cmul(*p[2], *p[2])
    p[3] = _cmul(*p[2], *p[1])
    p[5] = _cmul(*p[4], *p[1])
    p[6] = _cmul(*p[4], *p[2])
    p[7] = _cmul(*p[4], *p[3])
    p[8] = _cmul(*p[4], *p[4])
    row = lax.broadcasted_iota(jnp.int32, (SUBLANE, lr.shape[1]), 0)
    tr = jnp.zeros(row.shape, F32)
    ti = jnp.zeros(row.shape, F32)
    for i in range(SUBLANE):
        k = SUBLANE - i if reverse else i + 1
        tr = jnp.where(row == i, p[k][0], tr)
        ti = jnp.where(row == i, p[k][1], ti)
    return p, (tr, ti), row


def _s5_block_scan(xr, xi, p, tab, row, hr, hi, reverse):
    for k in (1, 2, 4):
        if reverse:
            sr = jnp.where(row < SUBLANE - k, pltpu.roll(xr, SUBLANE - k, 0), 0.0)
            si = jnp.where(row < SUBLANE - k, pltpu.roll(xi, SUBLANE - k, 0), 0.0)
        else:
            sr = jnp.where(row >= k, pltpu.roll(xr, k, 0), 0.0)
            si = jnp.where(row >= k, pltpu.roll(xi, k, 0), 0.0)
        ar, ai = _cmul(p[k][0], p[k][1], sr, si)
        xr, xi = xr + ar, xi + ai
    cr, ci = _cmul(tab[0], tab[1], hr, hi)
    return xr + cr, xi + ci


def _s5_forward_scan(sr_ref, si_ref, lr, li):
    n_blocks = sr_ref.shape[0] // SUBLANE
    p, tab, row = _s5_tables(lr, li, False)

    def step(b, carry):
        base = pl.multiple_of(b * SUBLANE, SUBLANE)
        xr, xi = _s5_block_scan(sr_ref[pl.ds(base, SUBLANE), :], si_ref[pl.ds(base, SUBLANE), :],
                                p, tab, row, carry[0], carry[1], False)
        sr_ref[pl.ds(base, SUBLANE), :] = xr
        si_ref[pl.ds(base, SUBLANE), :] = xi
        return xr[SUBLANE - 1:SUBLANE, :], xi[SUBLANE - 1:SUBLANE, :]

    zero = jnp.zeros((1, lr.shape[1]), F32)
    lax.fori_loop(0, n_blocks, step, (zero, zero))


def _s5_specs(seq):
    u_spec = pl.BlockSpec((seq, S5_U), lambda g: (0, g))
    bd_spec = pl.BlockSpec((1, S5_U, S5_L), lambda g: (g, 0, 0))
    cd_spec = pl.BlockSpec((1, S5_L, S5_U), lambda g: (g, 0, 0))
    lam_spec = pl.BlockSpec((1, 2, S5_L), lambda g: (g, 0, 0))
    d_spec = pl.BlockSpec((1, S5_U), lambda g: (0, g))
    return u_spec, bd_spec, cd_spec, lam_spec, d_spec


S5_ROWS = 256


def _row_chunks(seq, fn):
    rows_per = min(S5_ROWS, seq)

    def step(r, carry):
        fn(pl.ds(pl.multiple_of(r * rows_per, rows_per), rows_per))
        return carry

    lax.fori_loop(0, seq // rows_per, step, 0)


def _s5_fwd_call(u, bdr, bdi, cdr, cdi, lam, d):
    seq = u.shape[0]
    u_spec, bd_spec, cd_spec, lam_spec, d_spec = _s5_specs(seq)

    def body(u_ref, bdr_ref, bdi_ref, cdr_ref, cdi_ref, lam_ref, d_ref, y_ref, sr_ref, si_ref):
        def project_in(rows):
            uv = u_ref[rows, :]
            sr_ref[rows, :] = _bdot(uv, bdr_ref[0])
            si_ref[rows, :] = _bdot(uv, bdi_ref[0])

        def project_out(rows):
            y_ref[rows, :] = (_bdot(sr_ref[rows, :], cdr_ref[0]) - _bdot(si_ref[rows, :], cdi_ref[0])
                              + d_ref[...] * u_ref[rows, :])

        _row_chunks(seq, project_in)
        _s5_forward_scan(sr_ref, si_ref, lam_ref[0, 0:1, :], lam_ref[0, 1:2, :])
        _row_chunks(seq, project_out)

    return pl.pallas_call(
        body, grid=(S5_GB,), in_specs=[u_spec, bd_spec, bd_spec, cd_spec, cd_spec, lam_spec, d_spec],
        out_specs=u_spec, out_shape=jax.ShapeDtypeStruct(u.shape, F32),
        scratch_shapes=[pltpu.VMEM((seq, S5_L), F32), pltpu.VMEM((seq, S5_L), F32)],
        compiler_params=_cparams(("parallel",)), name="s5_fwd",
    )(u, bdr, bdi, cdr, cdi, lam, d)


def _s5_bwd_call(u, bdr, bdi, cdr, cdi, lam, d, dy):
    seq = u.shape[0]
    n_blocks = seq // SUBLANE
    u_spec, bd_spec, cd_spec, lam_spec, d_spec = _s5_specs(seq)

    def body(u_ref, bdr_ref, bdi_ref, cdr_ref, cdi_ref, lam_ref, d_ref, dy_ref,
             du_ref, dbdr_ref, dbdi_ref, dcdr_ref, dcdi_ref, dlam_ref, dd_ref, sr_ref, si_ref, gr_ref, gi_ref):
        lr, li = lam_ref[0, 0:1, :], lam_ref[0, 1:2, :]

        def project_in(rows):
            uv = u_ref[rows, :]
            dyv = dy_ref[rows, :]
            sr_ref[rows, :] = _bdot(uv, bdr_ref[0])
            si_ref[rows, :] = _bdot(uv, bdi_ref[0])
            gr_ref[rows, :] = _bdot(dyv, cdr_ref[0], _NT)
            gi_ref[rows, :] = -_bdot(dyv, cdi_ref[0], _NT)

        _row_chunks(seq, project_in)
        _s5_forward_scan(sr_ref, si_ref, lr, li)
        p, tab, row = _s5_tables(lr, -li, True)

        def step(i, carry):
            hr, hi, acc_r, acc_i = carry
            b = n_blocks - 1 - i
            base = pl.multiple_of(b * SUBLANE, SUBLANE)
            xr, xi = _s5_block_scan(gr_ref[pl.ds(base, SUBLANE), :], gi_ref[pl.ds(base, SUBLANE), :],
                                    p, tab, row, hr, hi, True)
            gr_ref[pl.ds(base, SUBLANE), :] = xr
            gi_ref[pl.ds(base, SUBLANE), :] = xi
            prev = pl.multiple_of(jnp.maximum(b - 1, 0) * SUBLANE, SUBLANE)
            keep = (b > 0).astype(F32)
            last_r = sr_ref[pl.ds(prev, SUBLANE), :][SUBLANE - 1:SUBLANE, :] * keep
            last_i = si_ref[pl.ds(prev, SUBLANE), :][SUBLANE - 1:SUBLANE, :] * keep
            pr = jnp.where(row >= 1, pltpu.roll(sr_ref[pl.ds(base, SUBLANE), :], 1, 0), last_r)
            pi = jnp.where(row >= 1, pltpu.roll(si_ref[pl.ds(base, SUBLANE), :], 1, 0), last_i)
            acc_r = acc_r + pr * xr + pi * xi
            acc_i = acc_i + pr * xi - pi * xr
            return xr[0:1, :], xi[0:1, :], acc_r, acc_i

        zero = jnp.zeros((1, S5_L), F32)
        zacc = jnp.zeros((SUBLANE, S5_L), F32)
        _, _, acc_r, acc_i = lax.fori_loop(0, n_blocks, step, (zero, zero, zacc, zacc))
        dlam_ref[0, 0:1, :] = jnp.sum(acc_r, axis=0, keepdims=True)
        dlam_ref[0, 1:2, :] = jnp.sum(acc_i, axis=0, keepdims=True)
        for ref in (dbdr_ref, dbdi_ref, dcdr_ref, dcdi_ref, dd_ref):
            ref[...] = jnp.zeros_like(ref)

        def grads(rows):
            uv, dyv = u_ref[rows, :], dy_ref[rows, :]
            grv, giv = gr_ref[rows, :], gi_ref[rows, :]
            du_ref[rows, :] = _bdot(grv, bdr_ref[0], _NT) + _bdot(giv, bdi_ref[0], _NT) + d_ref[...] * dyv
            dbdr_ref[0] += _bdot(uv, grv, _TN)
            dbdi_ref[0] += _bdot(uv, giv, _TN)
            dcdr_ref[0] += _bdot(sr_ref[rows, :], dyv, _TN)
            dcdi_ref[0] -= _bdot(si_ref[rows, :], dyv, _TN)
            dd_ref[...] += jnp.sum(dyv * uv, axis=0, keepdims=True)

        _row_chunks(seq, grads)

    scratch = [pltpu.VMEM((seq, S5_L), F32) for _ in range(4)]
    return pl.pallas_call(
        body, grid=(S5_GB,),
        in_specs=[u_spec, bd_spec, bd_spec, cd_spec, cd_spec, lam_spec, d_spec, u_spec],
        out_specs=[u_spec, bd_spec, bd_spec, cd_spec, cd_spec, lam_spec, d_spec],
        out_shape=[jax.ShapeDtypeStruct(a.shape, F32) for a in (u, bdr, bdi, cdr, cdi, lam, d)],
        scratch_shapes=scratch, compiler_params=_cparams(("parallel",)), name="s5_bwd",
    )(u, bdr, bdi, cdr, cdi, lam, d, dy)


@jax.custom_vjp
def s5_core(u, bdr, bdi, cdr, cdi, lam, d):
    return _s5_fwd_call(u, bdr, bdi, cdr, cdi, lam, d)


def _s5_core_fwd(*args):
    return _s5_fwd_call(*args), args


def _s5_core_bwd(res, dy):
    return tuple(_s5_bwd_call(*res, dy))


s5_core.defvjp(_s5_core_fwd, _s5_core_bwd)


def _s5_discretize(a_re, a_im, log_dt, b_re, b_im, c_re, c_im, d):
    dt = jnp.exp(log_dt)[:, None]
    mag = jnp.exp(a_re * dt)
    lbr, lbi = mag * jnp.cos(a_im * dt), mag * jnp.sin(a_im * dt)
    den = a_re * a_re + a_im * a_im
    fr = ((lbr - 1.0) * a_re + lbi * a_im) / den
    fi = (lbi * a_re - (lbr - 1.0) * a_im) / den
    bbr = fr[:, :, None] * b_re - fi[:, :, None] * b_im
    bbi = fr[:, :, None] * b_im + fi[:, :, None] * b_re
    eye = jnp.eye(S5_GROUPS // S5_GB, dtype=F32)
    gl = S5_GROUPS // S5_GB

    def bd(t):
        return jnp.einsum('bgpc,gh->bgchp', t.reshape(S5_GB, gl, S5_STATE, S5_GROUP), eye).reshape(S5_GB, S5_U, S5_L)

    def cd(t):
        return jnp.einsum('bgcp,gh->bgphc', t.reshape(S5_GB, gl, S5_GROUP, S5_STATE), eye).reshape(S5_GB, S5_L, S5_U)

    lam = jnp.stack([lbr.reshape(S5_GB, S5_L), lbi.reshape(S5_GB, S5_L)], axis=1)
    return bd(bbr), bd(bbi), cd(c_re), cd(c_im), lam, d.reshape(1, S5_WIDTH)


@jax.custom_vjp
def _unit_lower_solve(neg_a, rhs, tinv):
    return _split_dot(tinv, rhs)


def _unit_lower_solve_fwd(neg_a, rhs, tinv):
    x = _split_dot(tinv, rhs)
    return x, (x, tinv)


def _unit_lower_solve_bwd(res, dx):
    x, tinv = res
    g = _split_dot(tinv, dx, _TN)
    return _split_dot(g, x, _NT), g, jnp.zeros_like(tinv)


_unit_lower_solve.defvjp(_unit_lower_solve_fwd, _unit_lower_solve_bwd)


def _unit_lower_inverse(neg_a):
    r = lax.broadcasted_iota(jnp.int32, neg_a.shape, 0)
    c = lax.broadcasted_iota(jnp.int32, neg_a.shape, 1)
    p = (r == c).astype(F32) + neg_a
    npow = _split_dot(neg_a, neg_a)
    for _ in range(4):
        y = _split_dot(jnp.concatenate([p, npow], axis=0), npow)
        p = p + y[:CHUNK]
        npow = y[CHUNK:]
    return p + _split_dot(p, npow)


def _gdn_chunk(q, k, v, g_col, b_col, st, tinv=None):
    r = lax.broadcasted_iota(jnp.int32, (CHUNK, CHUNK), 0)
    c = lax.broadcasted_iota(jnp.int32, (CHUNK, CHUNK), 1)
    eye = (r == c).astype(F32)
    strict = r > c
    causal = r >= c
    g_row = jnp.sum(g_col * eye, axis=0, keepdims=True)
    gcum = jnp.sum(jnp.where(causal, g_row, 0.0), axis=1, keepdims=True)
    gcum_row = jnp.sum(gcum * eye, axis=0, keepdims=True)
    diff = gcum - gcum_row
    decay_strict = jnp.where(strict, jnp.exp(jnp.where(strict, diff, 0.0)), 0.0)
    decay_causal = jnp.where(causal, jnp.exp(jnp.where(causal, diff, 0.0)), 0.0)
    gamma = jnp.exp(gcum)
    g_last = jnp.sum(jnp.where(lax.broadcasted_iota(jnp.int32, (CHUNK, 1), 0) == CHUNK - 1, gcum, 0.0),
                     axis=0, keepdims=True)
    kk = _bdot(k, k, _NT)
    neg_a = -(b_col * kk * decay_strict)
    if tinv is None:
        tinv = _unit_lower_inverse(neg_a)
    x = _unit_lower_solve(neg_a, jnp.concatenate([b_col * v, (b_col * gamma) * k], axis=1), lax.stop_gradient(tinv))
    u_new, w_k = x[:, :GDN_HEAD_DIM], x[:, GDN_HEAD_DIM:]
    qk = _bdot(q, k, _NT) * decay_causal
    q_g = q * gamma
    k_tail = k * jnp.exp(g_last - gcum)
    w = u_new - _bdot(w_k, st)
    o = _bdot(q_g, st) + _bdot(qk, w)
    st_new = jnp.exp(g_last) * st + _bdot(k_tail, w, _TN)
    return o, st_new, tinv


def _gdn_cols(bgv, h):
    lane = lax.broadcasted_iota(jnp.int32, bgv.shape, 1)
    g_col = jnp.sum(jnp.where(lane == h, bgv, 0.0), axis=1, keepdims=True)
    b_col = jnp.sum(jnp.where(lane == GDN_HEADS + h, bgv, 0.0), axis=1, keepdims=True)
    return g_col, b_col


GDN_CPS = 4


def _gdn_fwd_call(q, k, v, bg):
    seq = q.shape[0]
    n_chunks = seq // CHUNK
    n_steps = n_chunks // GDN_CPS
    rows = GDN_CPS * CHUNK
    x_spec = pl.BlockSpec((rows, GDN_WIDTH), lambda n: (n, 0))
    bg_spec = pl.BlockSpec((rows, LANE), lambda n: (n, 0))
    st_spec = pl.BlockSpec((GDN_CPS, GDN_WIDTH, GDN_HEAD_DIM), lambda n: (n, 0, 0))
    ti_spec = pl.BlockSpec((GDN_CPS, GDN_HEADS * CHUNK, CHUNK), lambda n: (n, 0, 0))

    def body(q_ref, k_ref, v_ref, bg_ref, o_ref, st_out_ref, ti_out_ref, st_ref):
        @pl.when(pl.program_id(0) == 0)
        def _():
            st_ref[...] = jnp.zeros_like(st_ref)

        for h in range(GDN_HEADS):
            sl = slice(h * GDN_HEAD_DIM, (h + 1) * GDN_HEAD_DIM)
            st = st_ref[sl, :]
            for cc in range(GDN_CPS):
                rs = slice(cc * CHUNK, (cc + 1) * CHUNK)
                g_col, b_col = _gdn_cols(bg_ref[rs, :], h)
                st_out_ref[cc, sl, :] = st
                o, st, tinv = _gdn_chunk(q_ref[rs, sl], k_ref[rs, sl], v_ref[rs, sl], g_col, b_col, st)
                o_ref[rs, sl] = o
                ti_out_ref[cc, h * CHUNK:(h + 1) * CHUNK, :] = tinv
            st_ref[sl, :] = st

    return pl.pallas_call(
        body, grid=(n_steps,), in_specs=[x_spec, x_spec, x_spec, bg_spec], out_specs=[x_spec, st_spec, ti_spec],
        out_shape=[jax.ShapeDtypeStruct(q.shape, F32),
                   jax.ShapeDtypeStruct((n_chunks, GDN_WIDTH, GDN_HEAD_DIM), F32),
                   jax.ShapeDtypeStruct((n_chunks, GDN_HEADS * CHUNK, CHUNK), F32)],
        scratch_shapes=[pltpu.VMEM((GDN_WIDTH, GDN_HEAD_DIM), F32)],
        compiler_params=_cparams(("arbitrary",)), name="gdn_fwd",
    )(q, k, v, bg)


def _gdn_bwd_call(q, k, v, bg, states, tinvs, do):
    seq = q.shape[0]
    n_steps = seq // CHUNK // GDN_CPS
    rows = GDN_CPS * CHUNK
    x_spec = pl.BlockSpec((rows, GDN_WIDTH), lambda i: (n_steps - 1 - i, 0))
    bg_spec = pl.BlockSpec((rows, LANE), lambda i: (n_steps - 1 - i, 0))
    st_spec = pl.BlockSpec((GDN_CPS, GDN_WIDTH, GDN_HEAD_DIM), lambda i: (n_steps - 1 - i, 0, 0))
    ti_spec = pl.BlockSpec((GDN_CPS, GDN_HEADS * CHUNK, CHUNK), lambda i: (n_steps - 1 - i, 0, 0))

    def body(q_ref, k_ref, v_ref, bg_ref, st_in_ref, ti_ref, do_ref, dq_ref, dk_ref, dv_ref, dbg_ref, dst_ref):
        @pl.when(pl.program_id(0) == 0)
        def _():
            dst_ref[...] = jnp.zeros_like(dst_ref)

        lane = lax.broadcasted_iota(jnp.int32, (CHUNK, LANE), 1)
        dbg = [jnp.zeros((CHUNK, LANE), F32) for _ in range(GDN_CPS)]
        for h in range(GDN_HEADS):
            sl = slice(h * GDN_HEAD_DIM, (h + 1) * GDN_HEAD_DIM)
            dst = dst_ref[sl, :]
            for cc in reversed(range(GDN_CPS)):
                rs = slice(cc * CHUNK, (cc + 1) * CHUNK)
                g_col, b_col = _gdn_cols(bg_ref[rs, :], h)
                tinv = ti_ref[cc, h * CHUNK:(h + 1) * CHUNK, :]
                _, pullback = jax.vjp(lambda *a: _gdn_chunk(*a, tinv=tinv)[:2], q_ref[rs, sl], k_ref[rs, sl],
                                      v_ref[rs, sl], g_col, b_col, st_in_ref[cc, sl, :])
                dq, dk, dv, dg, db, dst = pullback((do_ref[rs, sl], dst))
                dq_ref[rs, sl] = dq
                dk_ref[rs, sl] = dk
                dv_ref[rs, sl] = dv
                dbg[cc] = dbg[cc] + jnp.where(lane == h, dg, 0.0) + jnp.where(lane == GDN_HEADS + h, db, 0.0)
            dst_ref[sl, :] = dst
        for cc in range(GDN_CPS):
            dbg_ref[cc * CHUNK:(cc + 1) * CHUNK, :] = dbg[cc]

    return pl.pallas_call(
        body, grid=(n_steps,), in_specs=[x_spec, x_spec, x_spec, bg_spec, st_spec, ti_spec, x_spec],
        out_specs=[x_spec, x_spec, x_spec, bg_spec],
        out_shape=[jax.ShapeDtypeStruct(q.shape, F32)] * 3 + [jax.ShapeDtypeStruct(bg.shape, F32)],
        scratch_shapes=[pltpu.VMEM((GDN_WIDTH, GDN_HEAD_DIM), F32)],
        compiler_params=_cparams(("arbitrary",)), name="gdn_bwd",
    )(q, k, v, bg, states, tinvs, do)


@jax.custom_vjp
def gdn_core(q, k, v, bg):
    return _gdn_fwd_call(q, k, v, bg)[0]


def _gdn_core_fwd(q, k, v, bg):
    o, states, tinvs = _gdn_fwd_call(q, k, v, bg)
    return o, (q, k, v, bg, states, tinvs)


def _gdn_core_bwd(res, do):
    return tuple(_gdn_bwd_call(*res, do))


gdn_core.defvjp(_gdn_core_fwd, _gdn_core_bwd)


def _row(v):
    return v.reshape(1, -1)


def _lane_pad(v):
    return jnp.pad(v, (0, LANE - v.shape[0])).reshape(1, LANE)


def _delay_rows(x, k):
    return jnp.pad(x, ((k, 0), (0, 0)))[:x.shape[0]]


def s5_mixer(u, a_re, a_im, log_dt, b_re, b_im, c_re, c_im, d, w_glu, b_glu):
    y = s5_core(u, *_s5_discretize(a_re, a_im, log_dt, b_re, b_im, c_re, c_im, d))
    return make_rowop(_s5_post_fn, "s5_post", tm=NORM_ROWS)((y,), (w_glu, _row(b_glu)))[0]


def gated_deltanet(qkv, gate, ab, conv_w, a_log, dt_bias, out_g):
    rows = (qkv, _delay_rows(qkv, 1), _delay_rows(qkv, 2), _delay_rows(qkv, 3), ab)
    q, k, v, bg = make_rowop(_gdn_prep_fn, "gdn_prep")(rows, (conv_w, _lane_pad(a_log), _lane_pad(dt_bias)))
    o = gdn_core(q, k, v, bg)
    return make_rowop(_gdn_out_fn, "gdn_out", tm=NORM_ROWS)((o, gate), (_row(out_g),))[0]


def chunk_attention(x, h, w_qkv, w_out, q_g, k_g, rel_bias):
    qkv = linear_cols(h, w_qkv)
    qn, kn = make_rowop(_ca_qknorm_fn, "ca_qknorm")((qkv,), (_row(jnp.tile(q_g, 2)), _row(jnp.tile(k_g, 2))))
    kpad = jnp.pad(kn, ((CA_PAD, 0), (0, 0)))
    vpad = jnp.pad(qkv[:, 2 * D_MODEL:], ((CA_PAD, 0), (0, 0)))
    o = chunk_attn_core(qn, kpad, vpad, _rel_bias_vector(rel_bias))
    return linear_res(o, w_out, x)


def memory_cross_attention(x, h, mem_n, w_q, w_kv, w_out, q_g, k_g):
    q = linear(h, w_q)
    kv = linear_cols(mem_n, w_kv)
    o = make_rowop(_xattn_fn, "xattn", tm=512)((q,), (kv[:, :D_MODEL], kv[:, D_MODEL:], _row(q_g), _row(k_g)))[0]
    return linear_res(o, w_out, x)


def swiglu(x, h, w_gate, w_up, w_down):
    return ffn_down(linear_cols_dm(h, w_gate), linear_cols_dm(h, w_up), w_down, x)


def _rel_bias_vector(rel_bias):
    heads = rel_bias.shape[0]
    n_far = CA_KB - 1 - MAX_REL
    n_neg = CA_VEC - 1 - n_far - (2 * MAX_REL + 1)
    vec = jnp.concatenate([jnp.zeros((heads, 1), F32),
                           jnp.broadcast_to(rel_bias[:, 2 * MAX_REL:], (heads, n_far)),
                           jnp.flip(rel_bias, axis=1),
                           jnp.broadcast_to(rel_bias[:, :1], (heads, n_neg))], axis=1)
    return vec.reshape(heads // 2, 2, CA_VEC)


_HBM_SPEC = pl.BlockSpec(memory_space=pltpu.HBM)
_SEM_SPEC = pl.BlockSpec(memory_space=pltpu.SEMAPHORE)
_SIDE_EFFECT = pltpu.SideEffectType.DATAFLOW_SIDE_EFFECTING


def _peer(x, y, c, k):
    return (x + (k >> 2)) % 2, (y + ((k >> 1) & 1)) % 2, (c + (k & 1)) % 2


def _exchange_start(arrays, modes, after, name, carry):
    n = len(arrays)
    n_sem = n * (N_DEV - 1)
    lands = [pltpu.with_memory_space_constraint(lax.empty((N_DEV,) + a.shape if m == "gather" else a.shape, a.dtype),
                                                pltpu.HBM) for a, m in zip(arrays, modes)]
    arrays = [pltpu.with_memory_space_constraint(a, pltpu.HBM) for a in arrays]

    def body(*refs):
        ins, zones = refs[:n], refs[n:2 * n]
        send_sems, recv_sems, own_sems = refs[2 * n + 2:2 * n + 5]
        x, y, c = lax.axis_index("x"), lax.axis_index("y"), lax.axis_index("c")
        me = 4 * x + 2 * y + c
        for i in range(n):
            pltpu.make_async_copy(ins[i] if modes[i] == "gather" else ins[i].at[me], zones[i].at[me],
                                  own_sems.at[i]).start()
        for k in range(1, N_DEV):
            px, py, pc = _peer(x, y, c, k)
            peer = 4 * px + 2 * py + pc
            for i in range(n):
                sem = i * (N_DEV - 1) + k - 1
                pltpu.make_async_remote_copy(src_ref=ins[i] if modes[i] == "gather" else ins[i].at[peer],
                                             dst_ref=zones[i].at[me], send_sem=send_sems.at[sem],
                                             recv_sem=recv_sems.at[sem], device_id=(px, py, pc),
                                             device_id_type=pl.DeviceIdType.MESH).start()

    carry = pltpu.with_memory_space_constraint(carry, pltpu.HBM)
    out_shape = ((pltpu.SemaphoreType.DMA((n_sem,)), pltpu.SemaphoreType.DMA((n_sem,)), pltpu.SemaphoreType.DMA((n,)))
                 + tuple(pltpu.HBM(a.shape, a.dtype) for a in arrays) + tuple(pltpu.HBM(z.shape, z.dtype) for z in lands)
                 + (pltpu.HBM(carry.shape, carry.dtype),))
    aliases = {i: 3 + i for i in range(2 * n)}
    aliases[2 * n + 1] = 3 + 2 * n
    res = pl.pallas_call(
        body, name=name, out_shape=out_shape,
        in_specs=[_HBM_SPEC] * (2 * n) + [pl.BlockSpec(memory_space=pl.ANY), _HBM_SPEC],
        out_specs=(_SEM_SPEC,) * 3 + (_HBM_SPEC,) * (2 * n + 1),
        input_output_aliases=aliases,
        compiler_params=pltpu.CompilerParams(has_side_effects=_SIDE_EFFECT),
    )(*arrays, *lands, after, carry)
    return tuple(res[:3]), list(res[3:3 + n]), list(res[3 + n:3 + 2 * n]), res[3 + 2 * n]


def _exchange_wait(started, modes, after, name):
    sems, sources, zones, _ = started
    n = len(sources)

    def body(*refs):
        ins, lands = refs[:n], refs[n:2 * n]
        send_ref, recv_ref, own_ref = refs[2 * n:2 * n + 3]
        x, y, c = lax.axis_index("x"), lax.axis_index("y"), lax.axis_index("c")
        me = 4 * x + 2 * y + c
        for i in range(n):
            pltpu.make_async_copy(ins[i] if modes[i] == "gather" else ins[i].at[me], lands[i].at[me],
                                  own_ref.at[i]).wait()
        for k in range(1, N_DEV):
            px, py, pc = _peer(x, y, c, k)
            peer = 4 * px + 2 * py + pc
            for i in range(n):
                sem = i * (N_DEV - 1) + k - 1
                cp = pltpu.make_async_remote_copy(src_ref=ins[i] if modes[i] == "gather" else ins[i].at[peer],
                                                  dst_ref=lands[i].at[peer], send_sem=send_ref.at[sem],
                                                  recv_sem=recv_ref.at[sem], device_id=(px, py, pc),
                                                  device_id_type=pl.DeviceIdType.MESH)
                cp.wait_send()
                cp.wait_recv()

    res = pl.pallas_call(
        body, name=name,
        out_shape=tuple(pltpu.HBM(a.shape, a.dtype) for a in sources) + tuple(pltpu.HBM(z.shape, z.dtype) for z in zones),
        in_specs=[_HBM_SPEC] * (2 * n) + [_SEM_SPEC] * 3 + [pl.BlockSpec(memory_space=pl.ANY)],
        out_specs=(_HBM_SPEC,) * (2 * n), input_output_aliases={i: i for i in range(2 * n)},
        compiler_params=pltpu.CompilerParams(has_side_effects=_SIDE_EFFECT),
    )(*sources, *zones, *sems, after)
    return list(res[n:])


ADAM_TILE = 64 * 1024


def _adam_call(w, m, v, slots, name):
    n_layers, rows, cols = w.shape
    tr = rows
    if n_layers * rows * cols > ADAM_TILE:
        fits = [t for t in range(SUBLANE, rows, SUBLANE) if rows % t == 0 and n_layers * t * cols <= ADAM_TILE]
        tr = max(fits) if fits else SUBLANE
    c1 = 1.0 - ADAM_B1 ** ADAM_STEP
    c2 = 1.0 - ADAM_B2 ** ADAM_STEP

    def body(*refs):
        w_ref, m_ref, v_ref = refs[:3]
        slot_refs = refs[3:3 + n_layers]
        grad_ref, delta_ref, nm_ref, nv_ref = refs[3 + n_layers:]
        for layer in range(n_layers):
            g = slot_refs[layer][0].astype(F32)
            for k in range(1, N_DEV):
                g = g + slot_refs[layer][k].astype(F32)
            m_new = ADAM_B1 * m_ref[layer] + (1.0 - ADAM_B1) * g
            v_new = ADAM_B2 * v_ref[layer] + (1.0 - ADAM_B2) * (g * g)
            m_hat = m_new / c1
            v_hat = v_new / c2
            grad_ref[layer] = g
            delta_ref[layer] = -ADAM_LR * (m_hat / (jnp.sqrt(v_hat) + ADAM_EPS) + ADAM_WD * w_ref[layer])
            nm_ref[layer] = m_new
            nv_ref[layer] = v_new

    spec = pl.BlockSpec((n_layers, tr, cols), lambda i: (0, i, 0))
    slot_spec = pl.BlockSpec((N_DEV, tr, cols), lambda i: (0, i, 0))
    return pl.pallas_call(
        body, grid=(rows // tr,), in_specs=[spec, spec, spec] + [slot_spec] * n_layers,
        out_specs=[spec] * 4, out_shape=[jax.ShapeDtypeStruct(w.shape, F32)] * 4,
        compiler_params=_cparams(("parallel",)), name=name,
    )(w, m, v, *slots)


WEIGHT_NAMES = ['ab_norm_g', 'ab_w_in', 'ab_w_out', 's5_a_re', 's5_a_im', 's5_log_dt', 's5_b_re', 's5_b_im', 's5_c_re',
                's5_c_im', 's5_d', 's5_w_glu', 's5_b_glu', 'gdn_conv_w', 'gdn_a_log', 'gdn_dt_bias', 'gdn_out_norm_g',
                'c_norm_g', 'c_w_qkv', 'c_w_out', 'c_q_norm_g', 'c_k_norm_g', 'c_rel_bias', 'mem_norm_g', 'xa_norm_g',
                'xa_w_q', 'xa_w_kv', 'xa_w_out', 'xa_q_norm_g', 'xa_k_norm_g', 'f_norm_g', 'f_w_gate', 'f_w_up',
                'f_w_down']

SHARDED = {
    'ab_w_in': ('col', BF16), 'ab_w_out': ('row', BF16), 's5_w_glu': ('row', BF16), 'gdn_conv_w': ('col', F32),
    'c_norm_g': ('col', F32), 'c_w_qkv': ('col', BF16), 'c_w_out': ('row', BF16), 'xa_w_q': ('row', BF16),
    'xa_w_kv': ('col', BF16), 'xa_w_out': ('row', BF16), 'f_w_gate': ('col', BF16), 'f_w_up': ('col', BF16),
    'f_w_down': ('row', BF16),
}
GATHERED_AS_IS = ('c_w_qkv', 'xa_w_kv', 'f_w_gate', 'f_w_up', 'f_w_down')
REPLICATED = [n for n in WEIGHT_NAMES if n not in SHARDED]
PACK_UNIT = SUBLANE * LANE


def _full_from_gathered(g, axis):
    if axis == "row":
        return g.reshape(g.shape[0] * g.shape[1], g.shape[2])
    return jnp.transpose(g, (1, 0, 2)).reshape(g.shape[1], g.shape[0] * g.shape[2])


def _pack(arrays):
    flat = []
    for a in arrays:
        size = a.size
        padded = -(-size // PACK_UNIT) * PACK_UNIT
        flat.append(jnp.pad(a.reshape(-1), (0, padded - size)).reshape(-1, LANE))
    return jnp.concatenate(flat, axis=0)


def _unpack(buf, shapes):
    out, row = [], 0
    for shape in shapes:
        size = math.prod(shape)
        rows = -(-size // PACK_UNIT) * SUBLANE
        out.append(buf[row:row + rows].reshape(-1)[:size].reshape(shape))
        row += rows
    return out


N_STAGES = 2 * DEPTH
EVEN_SHARDED = ['ab_w_in', 'ab_w_out', 's5_w_glu', 'gdn_conv_w']
ODD_SHARDED = ['c_norm_g', 'c_w_qkv', 'c_w_out']
ALL_SHARDED = ['xa_w_q', 'xa_w_kv', 'xa_w_out', 'f_w_gate', 'f_w_up', 'f_w_down']
EVEN_SMALL = ['ab_norm_g', 's5_a_re', 's5_a_im', 's5_log_dt', 's5_b_re', 's5_b_im', 's5_c_re', 's5_c_im', 's5_d',
              's5_b_glu', 'gdn_a_log', 'gdn_dt_bias', 'gdn_out_norm_g']
ODD_SMALL = ['c_q_norm_g', 'c_k_norm_g', 'c_rel_bias']
ALL_SMALL = ['xa_norm_g', 'xa_q_norm_g', 'xa_k_norm_g', 'f_norm_g']


def _stage_params(stage):
    layer, part = divmod(stage, 2)
    if part == 1:
        return [(n, layer) for n in ALL_SHARDED], [(n, layer) for n in ALL_SMALL]
    big, small = (EVEN_SHARDED, EVEN_SMALL) if layer % 2 == 0 else (ODD_SHARDED, ODD_SMALL)
    return [(n, layer // 2) for n in big], [(n, layer // 2) for n in small]


def _stage_forward(stage, landed, small, x, mem_n):
    layer, part = divmod(stage, 2)
    big = {}
    for (n, _), g in zip(_stage_params(stage)[0], landed):
        if n == 'ab_w_in':
            big[n] = cols_to_natural(g, AB_IN_PAD)
        elif n in GATHERED_AS_IS:
            big[n] = g
        elif n == 's5_w_glu':
            big[n] = _full_from_gathered(g, 'row').astype(F32)
        else:
            big[n] = _full_from_gathered(g, SHARDED[n][0])
    if part == 1:
        h, x = rmsnorm_res(x, small['xa_norm_g'], "xa_norm")
        x = memory_cross_attention(x, h, mem_n, big['xa_w_q'], big['xa_w_kv'], big['xa_w_out'],
                                   small['xa_q_norm_g'], small['xa_k_norm_g'])
        h, x = rmsnorm_res(x, small['f_norm_g'], "f_norm")
        return swiglu(x, h, big['f_w_gate'], big['f_w_up'], big['f_w_down'])
    if layer % 2 == 0:
        h, x = rmsnorm_res(x, small['ab_norm_g'], "ab_norm")
        w_in = big['ab_w_in']
        u = linear(h, w_in[:, :S5_WIDTH])
        qkv = linear(h, w_in[:, S5_WIDTH:S5_WIDTH + 3 * GDN_WIDTH])
        gate = linear(h, w_in[:, S5_WIDTH + 3 * GDN_WIDTH:S5_WIDTH + 4 * GDN_WIDTH])
        ab = linear(h, w_in[:, S5_WIDTH + 4 * GDN_WIDTH:])
        a_out = s5_mixer(u, small['s5_a_re'], small['s5_a_im'], small['s5_log_dt'], small['s5_b_re'], small['s5_b_im'],
                         small['s5_c_re'], small['s5_c_im'], small['s5_d'], big['s5_w_glu'], small['s5_b_glu'])
        b_out = gated_deltanet(qkv, gate, ab, big['gdn_conv_w'], small['gdn_a_log'], small['gdn_dt_bias'],
                               small['gdn_out_norm_g'])
        w_out = big['ab_w_out']
        return linear_res(b_out, w_out[S5_WIDTH:], linear_res(a_out, w_out[:S5_WIDTH], x))
    h, x = rmsnorm_res(x, big['c_norm_g'].reshape(-1), "c_norm")
    return chunk_attention(x, h, big['c_w_qkv'], big['c_w_out'], small['c_q_norm_g'], small['c_k_norm_g'],
                           small['c_rel_bias'])


def _loss_rows(x, target):
    return jnp.sum(make_rowop(_loss_fn, "loss", tm=NORM_ROWS)((x, target), ())[0])


def kernel(x, mem, ab_norm_g, ab_w_in, ab_w_out, s5_a_re, s5_a_im, s5_log_dt, s5_b_re, s5_b_im, s5_c_re, s5_c_im, s5_d, s5_w_glu, s5_b_glu, gdn_conv_w, gdn_a_log, gdn_dt_bias, gdn_out_norm_g, c_norm_g, c_w_qkv, c_w_out, c_q_norm_g, c_k_norm_g, c_rel_bias, mem_norm_g, xa_norm_g, xa_w_q, xa_w_kv, xa_w_out, xa_q_norm_g, xa_k_norm_g, f_norm_g, f_w_gate, f_w_up, f_w_down, loss_target, m_ab_norm_g, m_ab_w_in, m_ab_w_out, m_s5_a_re, m_s5_a_im, m_s5_log_dt, m_s5_b_re, m_s5_b_im, m_s5_c_re, m_s5_c_im, m_s5_d, m_s5_w_glu, m_s5_b_glu, m_gdn_conv_w, m_gdn_a_log, m_gdn_dt_bias, m_gdn_out_norm_g, m_c_norm_g, m_c_w_qkv, m_c_w_out, m_c_q_norm_g, m_c_k_norm_g, m_c_rel_bias, m_mem_norm_g, m_xa_norm_g, m_xa_w_q, m_xa_w_kv, m_xa_w_out, m_xa_q_norm_g, m_xa_k_norm_g, m_f_norm_g, m_f_w_gate, m_f_w_up, m_f_w_down, v_ab_norm_g, v_ab_w_in, v_ab_w_out, v_s5_a_re, v_s5_a_im, v_s5_log_dt, v_s5_b_re, v_s5_b_im, v_s5_c_re, v_s5_c_im, v_s5_d, v_s5_w_glu, v_s5_b_glu, v_gdn_conv_w, v_gdn_a_log, v_gdn_dt_bias, v_gdn_out_norm_g, v_c_norm_g, v_c_w_qkv, v_c_w_out, v_c_q_norm_g, v_c_k_norm_g, v_c_rel_bias, v_mem_norm_g, v_xa_norm_g, v_xa_w_q, v_xa_w_kv, v_xa_w_out, v_xa_q_norm_g, v_xa_k_norm_g, v_f_norm_g, v_f_w_gate, v_f_w_up, v_f_w_down):
    given = dict(locals())
    no_after = jnp.zeros((SUBLANE, LANE), F32)

    def shard(n, idx):
        a = given[n][idx]
        return (a.reshape(1, -1) if a.ndim == 1 else a).astype(SHARDED[n][1])

    def gather_start(stage, after, carry):
        arrays = [shard(n, idx) for n, idx in _stage_params(stage)[0]]
        return _exchange_start(arrays, ["gather"] * len(arrays), after, "gather_start_%d" % stage, carry)

    act = x[0]
    mem_n, mem_pullback = jax.vjp(lambda m, g: rmsnorm(m, g, "mem_norm"), mem[0], mem_norm_g)
    in_flight = {}
    for stage in range(2):
        in_flight[stage] = gather_start(stage, no_after, act)
        act = in_flight[stage][3]
    pullbacks = []
    for stage in range(N_STAGES):
        started = in_flight.pop(stage)
        landed = _exchange_wait(started, ["gather"] * len(started[1]), act, "gather_wait_%d" % stage)
        if stage + 2 < N_STAGES:
            in_flight[stage + 2] = gather_start(stage + 2, landed[0], act)
            act = in_flight[stage + 2][3]
        small = {n: given[n][idx] for n, idx in _stage_params(stage)[1]}
        act, pullback = jax.vjp(functools.partial(_stage_forward, stage), landed, small, act, mem_n)
        pullbacks.append(pullback)
    loss_local, loss_pullback = jax.vjp(_loss_rows, act, loss_target[0])
    d_act = loss_pullback(jnp.ones((), F32))[0]

    d_mem_n = jnp.zeros_like(mem_n)
    g_small = {}
    received = [None] * N_STAGES
    started, after = None, no_after
    for stage in reversed(range(N_STAGES)):
        d_landed, d_small, d_act, d_mem = pullbacks[stage](d_act)
        if stage % 2 == 1:
            d_mem_n = d_mem_n + d_mem
        for n, idx in _stage_params(stage)[1]:
            g_small[(n, idx)] = d_small[n]
        if started is not None:
            received[stage + 1] = _exchange_wait(started, ["scatter"] * len(started[1]), d_act,
                                                 "scatter_wait_%d" % (stage + 1))
            after = received[stage + 1][0]
        started = _exchange_start(list(d_landed), ["scatter"] * len(d_landed), after, "scatter_start_%d" % stage,
                                  d_act)
        d_act = started[3]
    g_small[('mem_norm_g', None)] = mem_pullback(d_mem_n)[1]

    def small_grad(n):
        if n == 'mem_norm_g':
            return g_small[(n, None)]
        return jnp.stack([g_small[(n, i)] for i in range(given[n].shape[0])], axis=0)

    small_started = _exchange_start([_pack([small_grad(n) for n in REPLICATED])], ["gather"], d_act,
                                    "small_grads_start", no_after)
    received[0] = _exchange_wait(started, ["scatter"] * len(started[1]), small_started[3], "scatter_wait_0")

    results = {}
    for n in SHARDED:
        slots = {}
        for stage in range(N_STAGES):
            for (pn, idx), r in zip(_stage_params(stage)[0], received[stage]):
                if pn == n:
                    slots[idx] = r
        shape = given[n].shape
        to3d = lambda a: a.reshape(a.shape[0], -1, a.shape[-1])
        outs = _adam_call(to3d(given[n]), to3d(given['m_' + n]), to3d(given['v_' + n]),
                          [slots[i] for i in range(len(slots))], "adamw_" + n)
        results[n] = [o.reshape(shape) for o in outs]
    packed = _exchange_wait(small_started, ["gather"], results['f_w_down'][0], "small_grads_wait")[0]
    outs = _adam_call(_pack([given[n] for n in REPLICATED])[None], _pack([given['m_' + n] for n in REPLICATED])[None],
                      _pack([given['v_' + n] for n in REPLICATED])[None], [packed], "adamw_replicated")
    shapes = [given[n].shape for n in REPLICATED]
    for j, parts in enumerate(zip(*[_unpack(o[0], shapes) for o in outs])):
        results[REPLICATED[j]] = list(parts)

    loss = lax.psum(loss_local, ("x", "y", "c"))
    return (loss, d_act[None], *[results[n][0] for n in WEIGHT_NAMES], *[results[n][1] for n in WEIGHT_NAMES],
            *[results[n][2] for n in WEIGHT_NAMES], *[results[n][3] for n in WEIGHT_NAMES])
```
